```python
import math
import jax, jax.numpy as jnp
from jax import lax
import numpy as np

D_MODEL = 1024
BATCH = 16
SEQ = 2048
DEPTH = 4

D_MIX = 2 * D_MODEL
HEAD_DIM = 64
A_WIDTH = D_MIX // 4
A_HEADS = A_WIDTH // HEAD_DIM
B_WIDTH = D_MIX // 4
B_HEADS = B_WIDTH // HEAD_DIM
C_WIDTH = D_MIX // 2
C_HEADS = C_WIDTH // HEAD_DIM
CONV_A_WIDTH = 31
GMLP_CHUNK = 128
SSM_STATE = 128
SSM_GROUPS = 2
SSM_CONV = 4
SSD_CHUNK = 128
D_CONV_C = C_WIDTH + 2 * SSM_GROUPS * SSM_STATE
D_IN_PROJ = 2 * A_WIDTH + 2 * B_WIDTH + C_WIDTH + D_CONV_C + C_HEADS
D_FF = 4 * D_MODEL
EPS = 1e-5

kernel_name = "hybrid_conv_gmlp_ssd_trunk"


def rmsnorm(x, g):
    xf = x.astype(jnp.float32)
    y = xf * lax.rsqrt(jnp.mean(xf * xf, axis=-1, keepdims=True) + EPS)
    return (y * g.astype(jnp.float32)).astype(x.dtype)


def head_layernorm(x, g, b, n_heads):
    lead = x.shape[:-1]
    xf = x.astype(jnp.float32).reshape(*lead, n_heads, -1)
    mu = jnp.mean(xf, axis=-1, keepdims=True)
    xc = xf - mu
    var = jnp.mean(xc * xc, axis=-1, keepdims=True)
    y = (xc * lax.rsqrt(var + EPS)).reshape(*lead, -1)
    return (y * g.astype(jnp.float32) + b.astype(jnp.float32)).astype(x.dtype)


def causal_dwconv(x, w, b):
    k, c = w.shape
    y = lax.conv_general_dilated(
        x, w[:, None, :].astype(x.dtype), window_strides=(1,), padding=[(k - 1, 0)],
        dimension_numbers=("NWC", "WIO", "NWC"), feature_group_count=c)
    return y + b.astype(x.dtype)


def conformer_mixer(a_val, a_gate, conv_w, conv_b, ln_g, ln_b):
    h = a_val * jax.nn.sigmoid(a_gate)
    h = causal_dwconv(h, conv_w, conv_b)
    h = head_layernorm(h, ln_g, ln_b, A_HEADS)
    return jax.nn.silu(h)


def gmlp_mixer(u, v, ln_g, ln_b, w_s, b_s):
    u = jax.nn.gelu(u, approximate=False)
    v = jax.nn.gelu(v, approximate=False)
    v = head_layernorm(v, ln_g, ln_b, B_HEADS)
    bsz, s, _ = v.shape
    nc = s // GMLP_CHUNK
    v = v.reshape(bsz, nc, GMLP_CHUNK, B_HEADS, HEAD_DIM)
    mask = jnp.tril(jnp.ones((GMLP_CHUNK, GMLP_CHUNK), dtype=bool))
    w = jnp.where(mask, w_s, jnp.zeros_like(w_s))
    mix = jnp.einsum("hts,bcshp->bcthp", w, v) + b_s.T[:, :, None]
    return (u.reshape(v.shape) * mix).reshape(bsz, s, B_WIDTH)


def ssd_chunked(x, dt, A, B, C):
    bsz, s, h, p = x.shape
    g, n = B.shape[2], B.shape[3]
    hg = h // g
    nc = s // SSD_CHUNK
    l = SSD_CHUNK
    xc = (x * dt[..., None]).reshape(bsz, nc, l, g, hg, p)
    a_cs = jnp.cumsum((dt * A).reshape(bsz, nc, l, g, hg), axis=2)
    Bc = B.reshape(bsz, nc, l, g, n)
    Cc = C.reshape(bsz, nc, l, g, n)
    a_t = jnp.moveaxis(a_cs, 2, -1)
    seg = a_t[..., :, None] - a_t[..., None, :]
    mask = jnp.tril(jnp.ones((l, l), dtype=bool))
    L = jnp.exp(jnp.where(mask, seg, -jnp.inf))
    CB = jnp.einsum("bclgn,bcsgn->bcgls", Cc, Bc)
    y_diag = jnp.einsum("bcghls,bcsghp->bclghp", CB[:, :, :, None] * L, xc)
    decay_states = jnp.exp(a_cs[:, :, -1:] - a_cs)
    states = jnp.einsum("bclgn,bclghp->bcghpn", Bc, xc * decay_states[..., None])
    chunk_decay = jnp.exp(a_cs[:, :, -1])

    def step(carry, inp):
        st, dec = inp
        return carry * dec[..., None, None] + st, carry

    init = jnp.zeros((bsz, g, hg, p, n), x.dtype)
    _, prev = lax.scan(step, init, (jnp.moveaxis(states, 1, 0), jnp.moveaxis(chunk_decay, 1, 0)))
    prev = jnp.moveaxis(prev, 0, 1)
    y_off = jnp.einsum("bclgn,bcghpn->bclghp", Cc, prev) * jnp.exp(a_cs)[..., None]
    return (y_diag + y_off).reshape(bsz, s, h, p)


def mamba2_mixer(z, xbc, dt_raw, conv_w, conv_b, dt_bias, a_log, d_skip, norm_g):
    xbc = jax.nn.silu(causal_dwconv(xbc, conv_w, conv_b))
    xs, Bm, Cm = jnp.split(xbc, [C_WIDTH, C_WIDTH + SSM_GROUPS * SSM_STATE], axis=-1)
    bsz, s, _ = xs.shape
    xs = xs.astype(jnp.float32).reshape(bsz, s, C_HEADS, HEAD_DIM)
    Bm = Bm.astype(jnp.float32).reshape(bsz, s, SSM_GROUPS, SSM_STATE)
    Cm = Cm.astype(jnp.float32).reshape(bsz, s, SSM_GROUPS, SSM_STATE)
    dt = jax.nn.softplus(dt_raw.astype(jnp.float32) + dt_bias.astype(jnp.float32))
    A = -jnp.exp(a_log.astype(jnp.float32))
    y = ssd_chunked(xs, dt, A, Bm, Cm) + d_skip.astype(jnp.float32)[:, None] * xs
    y = y.reshape(bsz, s, C_WIDTH) * jax.nn.silu(z.astype(jnp.float32))
    yg = y.reshape(bsz, s, SSM_GROUPS, -1)
    yg = yg * lax.rsqrt(jnp.mean(yg * yg, axis=-1, keepdims=True) + EPS)
    y = yg.reshape(bsz, s, C_WIDTH) * norm_g.astype(jnp.float32)
    return y.astype(z.dtype)


def _fwd_setup_inputs(seed: int = 0) -> dict:
    key = jax.random.key(seed)
    ks = jax.random.split(key, 24)
    f32 = jnp.float32

    def nrm(k, shape, scale):
        return jax.random.normal(k, shape, f32) * scale

    dt0 = jnp.exp(jax.random.uniform(ks[13], (DEPTH, C_HEADS), f32) * (math.log(0.1) - math.log(0.001)) + math.log(0.001))
    return {
        "x": nrm(ks[0], (BATCH, SEQ, D_MODEL), 1.0),
        "norm1_g": 1.0 + nrm(ks[1], (DEPTH, D_MODEL), 0.02),
        "w_in": nrm(ks[2], (DEPTH, D_MODEL, D_IN_PROJ), D_MODEL ** -0.5),
        "conv_a_w": nrm(ks[3], (DEPTH, CONV_A_WIDTH, A_WIDTH), CONV_A_WIDTH ** -0.5),
        "conv_a_b": nrm(ks[4], (DEPTH, A_WIDTH), 0.02),
        "ln_a_g": 1.0 + nrm(ks[5], (DEPTH, A_WIDTH), 0.02),
        "ln_a_b": nrm(ks[6], (DEPTH, A_WIDTH), 0.02),
        "ln_b_g": 1.0 + nrm(ks[7], (DEPTH, B_WIDTH), 0.02),
        "ln_b_b": nrm(ks[8], (DEPTH, B_WIDTH), 0.02),
        "w_spatial": nrm(ks[9], (DEPTH, B_HEADS, GMLP_CHUNK, GMLP_CHUNK), GMLP_CHUNK ** -0.5),
        "b_spatial": 1.0 + nrm(ks[10], (DEPTH, B_HEADS, GMLP_CHUNK), 0.1),
        "conv_c_w": nrm(ks[11], (DEPTH, SSM_CONV, D_CONV_C), SSM_CONV ** -0.5),
        "conv_c_b": nrm(ks[12], (DEPTH, D_CONV_C), 0.02),
        "dt_bias": dt0 + jnp.log(-jnp.expm1(-dt0)),
        "a_log": jnp.log(jax.random.uniform(ks[14], (DEPTH, C_HEADS), f32, 1.0, 16.0)),
        "d_skip": 1.0 + nrm(ks[15], (DEPTH, C_HEADS), 0.1),
        "norm_c_g": 1.0 + nrm(ks[16], (DEPTH, C_WIDTH), 0.02),
        "w_out": nrm(ks[17], (DEPTH, D_MIX, D_MODEL), D_MIX ** -0.5),
        "norm2_g": 1.0 + nrm(ks[18], (DEPTH, D_MODEL), 0.02),
        "w_ff1": nrm(ks[19], (DEPTH, D_MODEL, D_FF), D_MODEL ** -0.5),
        "w_ff2": nrm(ks[20], (DEPTH, D_FF, D_MODEL), D_FF ** -0.5),
        "final_g": 1.0 + nrm(ks[21], (D_MODEL,), 0.02),
    }


def _fwd_reference(x, norm1_g, w_in, conv_a_w, conv_a_b, ln_a_g, ln_a_b, ln_b_g, ln_b_b,
              w_spatial, b_spatial, conv_c_w, conv_c_b, dt_bias, a_log, d_skip, norm_c_g,
              w_out, norm2_g, w_ff1, w_ff2, final_g):
    split_idx = [A_WIDTH, 2 * A_WIDTH, 2 * A_WIDTH + B_WIDTH, 2 * A_WIDTH + 2 * B_WIDTH,
                 2 * A_WIDTH + 2 * B_WIDTH + C_WIDTH, 2 * A_WIDTH + 2 * B_WIDTH + C_WIDTH + D_CONV_C]
    for i in range(DEPTH):
        h = rmsnorm(x, norm1_g[i])
        proj = h @ w_in[i]
        a_val, a_gate, b_u, b_v, z, xbc, dt_raw = jnp.split(proj, split_idx, axis=-1)
        ya = conformer_mixer(a_val, a_gate, conv_a_w[i], conv_a_b[i], ln_a_g[i], ln_a_b[i])
        yb = gmlp_mixer(b_u, b_v, ln_b_g[i], ln_b_b[i], w_spatial[i], b_spatial[i])
        yc = mamba2_mixer(z, xbc, dt_raw, conv_c_w[i], conv_c_b[i], dt_bias[i], a_log[i],
                          d_skip[i], norm_c_g[i])
        x = x + jnp.concatenate([ya, yb, yc], axis=-1) @ w_out[i]
        h = rmsnorm(x, norm2_g[i])
        x = x + jnp.square(jax.nn.relu(h @ w_ff1[i])) @ w_ff2[i]
    return rmsnorm(x, final_g)


import jax as _jax
import jax.numpy as _jnp

TWIN_FORMAT = 'train_step'
FWD_PARAMS = ['x', 'norm1_g', 'w_in', 'conv_a_w', 'conv_a_b', 'ln_a_g', 'ln_a_b', 'ln_b_g', 'ln_b_b', 'w_spatial', 'b_spatial', 'conv_c_w', 'conv_c_b', 'dt_bias', 'a_log', 'd_skip', 'norm_c_g', 'w_out', 'norm2_g', 'w_ff1', 'w_ff2', 'final_g']
TWIN_WEIGHTS = ['norm1_g', 'w_in', 'conv_a_w', 'conv_a_b', 'ln_a_g', 'ln_a_b', 'ln_b_g', 'ln_b_b', 'w_spatial', 'b_spatial', 'conv_c_w', 'conv_c_b', 'dt_bias', 'a_log', 'd_skip', 'norm_c_g', 'w_out', 'norm2_g', 'w_ff1', 'w_ff2', 'final_g']
TWIN_DIFF_INPUT = 'x'
TWIN_INPUTS = ['x', 'norm1_g', 'w_in', 'conv_a_w', 'conv_a_b', 'ln_a_g', 'ln_a_b', 'ln_b_g', 'ln_b_b', 'w_spatial', 'b_spatial', 'conv_c_w', 'conv_c_b', 'dt_bias', 'a_log', 'd_skip', 'norm_c_g', 'w_out', 'norm2_g', 'w_ff1', 'w_ff2', 'final_g', 'loss_target', 'm_norm1_g', 'm_w_in', 'm_conv_a_w', 'm_conv_a_b', 'm_ln_a_g', 'm_ln_a_b', 'm_ln_b_g', 'm_ln_b_b', 'm_w_spatial', 'm_b_spatial', 'm_conv_c_w', 'm_conv_c_b', 'm_dt_bias', 'm_a_log', 'm_d_skip', 'm_norm_c_g', 'm_w_out', 'm_norm2_g', 'm_w_ff1', 'm_w_ff2', 'm_final_g', 'v_norm1_g', 'v_w_in', 'v_conv_a_w', 'v_conv_a_b', 'v_ln_a_g', 'v_ln_a_b', 'v_ln_b_g', 'v_ln_b_b', 'v_w_spatial', 'v_b_spatial', 'v_conv_c_w', 'v_conv_c_b', 'v_dt_bias', 'v_a_log', 'v_d_skip', 'v_norm_c_g', 'v_w_out', 'v_norm2_g', 'v_w_ff1', 'v_w_ff2', 'v_final_g']
TWIN_OUTPUTS = ['loss', 'grad_x', 'grad_norm1_g', 'grad_w_in', 'grad_conv_a_w', 'grad_conv_a_b', 'grad_ln_a_g', 'grad_ln_a_b', 'grad_ln_b_g', 'grad_ln_b_b', 'grad_w_spatial', 'grad_b_spatial', 'grad_conv_c_w', 'grad_conv_c_b', 'grad_dt_bias', 'grad_a_log', 'grad_d_skip', 'grad_norm_c_g', 'grad_w_out', 'grad_norm2_g', 'grad_w_ff1', 'grad_w_ff2', 'grad_final_g', 'delta_norm1_g', 'delta_w_in', 'delta_conv_a_w', 'delta_conv_a_b', 'delta_ln_a_g', 'delta_ln_a_b', 'delta_ln_b_g', 'delta_ln_b_b', 'delta_w_spatial', 'delta_b_spatial', 'delta_conv_c_w', 'delta_conv_c_b', 'delta_dt_bias', 'delta_a_log', 'delta_d_skip', 'delta_norm_c_g', 'delta_w_out', 'delta_norm2_g', 'delta_w_ff1', 'delta_w_ff2', 'delta_final_g', 'new_m_norm1_g', 'new_m_w_in', 'new_m_conv_a_w', 'new_m_conv_a_b', 'new_m_ln_a_g', 'new_m_ln_a_b', 'new_m_ln_b_g', 'new_m_ln_b_b', 'new_m_w_spatial', 'new_m_b_spatial', 'new_m_conv_c_w', 'new_m_conv_c_b', 'new_m_dt_bias', 'new_m_a_log', 'new_m_d_skip', 'new_m_norm_c_g', 'new_m_w_out', 'new_m_norm2_g', 'new_m_w_ff1', 'new_m_w_ff2', 'new_m_final_g', 'new_v_norm1_g', 'new_v_w_in', 'new_v_conv_a_w', 'new_v_conv_a_b', 'new_v_ln_a_g', 'new_v_ln_a_b', 'new_v_ln_b_g', 'new_v_ln_b_b', 'new_v_w_spatial', 'new_v_b_spatial', 'new_v_conv_c_w', 'new_v_conv_c_b', 'new_v_dt_bias', 'new_v_a_log', 'new_v_d_skip', 'new_v_norm_c_g', 'new_v_w_out', 'new_v_norm2_g', 'new_v_w_ff1', 'new_v_w_ff2', 'new_v_final_g']
TWIN_LEAF_KINDS = {'loss': 'loss', 'grad_x': 'grad_x', 'grad_norm1_g': 'grad_w', 'grad_w_in': 'grad_w', 'grad_conv_a_w': 'grad_w', 'grad_conv_a_b': 'grad_w', 'grad_ln_a_g': 'grad_w', 'grad_ln_a_b': 'grad_w', 'grad_ln_b_g': 'grad_w', 'grad_ln_b_b': 'grad_w', 'grad_w_spatial': 'grad_w', 'grad_b_spatial': 'grad_w', 'grad_conv_c_w': 'grad_w', 'grad_conv_c_b': 'grad_w', 'grad_dt_bias': 'grad_w', 'grad_a_log': 'grad_w', 'grad_d_skip': 'grad_w', 'grad_norm_c_g': 'grad_w', 'grad_w_out': 'grad_w', 'grad_norm2_g': 'grad_w', 'grad_w_ff1': 'grad_w', 'grad_w_ff2': 'grad_w', 'grad_final_g': 'grad_w', 'delta_norm1_g': 'delta_w', 'delta_w_in': 'delta_w', 'delta_conv_a_w': 'delta_w', 'delta_conv_a_b': 'delta_w', 'delta_ln_a_g': 'delta_w', 'delta_ln_a_b': 'delta_w', 'delta_ln_b_g': 'delta_w', 'delta_ln_b_b': 'delta_w', 'delta_w_spatial': 'delta_w', 'delta_b_spatial': 'delta_w', 'delta_conv_c_w': 'delta_w', 'delta_conv_c_b': 'delta_w', 'delta_dt_bias': 'delta_w', 'delta_a_log': 'delta_w', 'delta_d_skip': 'delta_w', 'delta_norm_c_g': 'delta_w', 'delta_w_out': 'delta_w', 'delta_norm2_g': 'delta_w', 'delta_w_ff1': 'delta_w', 'delta_w_ff2': 'delta_w', 'delta_final_g': 'delta_w', 'new_m_norm1_g': 'new_m', 'new_m_w_in': 'new_m', 'new_m_conv_a_w': 'new_m', 'new_m_conv_a_b': 'new_m', 'new_m_ln_a_g': 'new_m', 'new_m_ln_a_b': 'new_m', 'new_m_ln_b_g': 'new_m', 'new_m_ln_b_b': 'new_m', 'new_m_w_spatial': 'new_m', 'new_m_b_spatial': 'new_m', 'new_m_conv_c_w': 'new_m', 'new_m_conv_c_b': 'new_m', 'new_m_dt_bias': 'new_m', 'new_m_a_log': 'new_m', 'new_m_d_skip': 'new_m', 'new_m_norm_c_g': 'new_m', 'new_m_w_out': 'new_m', 'new_m_norm2_g': 'new_m', 'new_m_w_ff1': 'new_m', 'new_m_w_ff2': 'new_m', 'new_m_final_g': 'new_m', 'new_v_norm1_g': 'new_v', 'new_v_w_in': 'new_v', 'new_v_conv_a_w': 'new_v', 'new_v_conv_a_b': 'new_v', 'new_v_ln_a_g': 'new_v', 'new_v_ln_a_b': 'new_v', 'new_v_ln_b_g': 'new_v', 'new_v_ln_b_b': 'new_v', 'new_v_w_spatial': 'new_v', 'new_v_b_spatial': 'new_v', 'new_v_conv_c_w': 'new_v', 'new_v_conv_c_b': 'new_v', 'new_v_dt_bias': 'new_v', 'new_v_a_log': 'new_v', 'new_v_d_skip': 'new_v', 'new_v_norm_c_g': 'new_v', 'new_v_w_out': 'new_v', 'new_v_norm2_g': 'new_v', 'new_v_w_ff1': 'new_v', 'new_v_w_ff2': 'new_v', 'new_v_final_g': 'new_v'}


def _forward(args):
    return _fwd_reference(*[args[k] for k in FWD_PARAMS])


def _output_shape():
    out = _jax.eval_shape(lambda: _forward(_fwd_setup_inputs(0)))
    return out.shape, out.dtype

N_MICROBATCH = 1
ADAM_LR = 0.001
ADAM_B1 = 0.9
ADAM_B2 = 0.999
ADAM_EPS = 1e-08
ADAM_WD = 0.01
ADAM_STEP = 10
PER_EXAMPLE_BATCH_AXIS = {'x': 0, 'loss_target': 0}
SHARED_INPUTS = []
_WEIGHT_DTYPES = {'norm1_g': _jnp.float32, 'w_in': _jnp.float32, 'conv_a_w': _jnp.float32, 'conv_a_b': _jnp.float32, 'ln_a_g': _jnp.float32, 'ln_a_b': _jnp.float32, 'ln_b_g': _jnp.float32, 'ln_b_b': _jnp.float32, 'w_spatial': _jnp.float32, 'b_spatial': _jnp.float32, 'conv_c_w': _jnp.float32, 'conv_c_b': _jnp.float32, 'dt_bias': _jnp.float32, 'a_log': _jnp.float32, 'd_skip': _jnp.float32, 'norm_c_g': _jnp.float32, 'w_out': _jnp.float32, 'norm2_g': _jnp.float32, 'w_ff1': _jnp.float32, 'w_ff2': _jnp.float32, 'final_g': _jnp.float32}
MOMENT_SCALE = {'norm1_g': 1.513734e-01, 'w_in': 7.079054e-02, 'conv_a_w': 5.507963e-02, 'conv_a_b': 1.198301e-01, 'ln_a_g': 7.607050e-02, 'ln_a_b': 7.399613e-02, 'ln_b_g': 4.257727e-02, 'ln_b_b': 4.091968e-02, 'w_spatial': 2.761352e-02, 'b_spatial': 4.006922e-02, 'conv_c_w': 7.641881e-02, 'conv_c_b': 1.046025e-01, 'dt_bias': 3.974114e-01, 'a_log': 2.990760e-01, 'd_skip': 4.755316e-01, 'norm_c_g': 8.634654e-02, 'w_out': 1.075831e-01, 'norm2_g': 1.362086e-01, 'w_ff1': 6.507535e-02, 'w_ff2': 1.241596e-01, 'final_g': 3.283940e+01}


def _to_microbatches(a, axis):
    t = _jnp.moveaxis(a, axis, 0)
    t = t.reshape((N_MICROBATCH, t.shape[0] // N_MICROBATCH) + t.shape[1:])
    return _jnp.moveaxis(t, 1, axis + 1)


def setup_inputs(seed: int = 0) -> dict:
    inp = _fwd_setup_inputs(seed)
    key = _jax.random.fold_in(_jax.random.key(seed), 7919)
    shape, _ = _output_shape()
    out = dict(inp)
    out["loss_target"] = _jax.random.normal(_jax.random.fold_in(key, 0), shape, _jnp.float32)
    for i, name in enumerate(TWIN_WEIGHTS):
        w = inp[name].astype(_jnp.float32)
        if MOMENT_SCALE is None:
            s = _jnp.sqrt(_jnp.mean(_jnp.square(w)) + 1e-30)
        else:
            s = MOMENT_SCALE[name]
        km, kv = _jax.random.split(_jax.random.fold_in(key, i + 1))
        out[name] = w
        out["m_" + name] = s * _jax.random.normal(km, w.shape, _jnp.float32)
        out["v_" + name] = (s * s) * _jax.random.uniform(kv, w.shape, _jnp.float32, 0.5, 1.5)
    if N_MICROBATCH > 1:
        for name, axis in PER_EXAMPLE_BATCH_AXIS.items():
            out[name] = _to_microbatches(out[name], axis)
    return {'x': out['x'], 'norm1_g': out['norm1_g'], 'w_in': out['w_in'], 'conv_a_w': out['conv_a_w'], 'conv_a_b': out['conv_a_b'], 'ln_a_g': out['ln_a_g'], 'ln_a_b': out['ln_a_b'], 'ln_b_g': out['ln_b_g'], 'ln_b_b': out['ln_b_b'], 'w_spatial': out['w_spatial'], 'b_spatial': out['b_spatial'], 'conv_c_w': out['conv_c_w'], 'conv_c_b': out['conv_c_b'], 'dt_bias': out['dt_bias'], 'a_log': out['a_log'], 'd_skip': out['d_skip'], 'norm_c_g': out['norm_c_g'], 'w_out': out['w_out'], 'norm2_g': out['norm2_g'], 'w_ff1': out['w_ff1'], 'w_ff2': out['w_ff2'], 'final_g': out['final_g'], 'loss_target': out['loss_target'], 'm_norm1_g': out['m_norm1_g'], 'm_w_in': out['m_w_in'], 'm_conv_a_w': out['m_conv_a_w'], 'm_conv_a_b': out['m_conv_a_b'], 'm_ln_a_g': out['m_ln_a_g'], 'm_ln_a_b': out['m_ln_a_b'], 'm_ln_b_g': out['m_ln_b_g'], 'm_ln_b_b': out['m_ln_b_b'], 'm_w_spatial': out['m_w_spatial'], 'm_b_spatial': out['m_b_spatial'], 'm_conv_c_w': out['m_conv_c_w'], 'm_conv_c_b': out['m_conv_c_b'], 'm_dt_bias': out['m_dt_bias'], 'm_a_log': out['m_a_log'], 'm_d_skip': out['m_d_skip'], 'm_norm_c_g': out['m_norm_c_g'], 'm_w_out': out['m_w_out'], 'm_norm2_g': out['m_norm2_g'], 'm_w_ff1': out['m_w_ff1'], 'm_w_ff2': out['m_w_ff2'], 'm_final_g': out['m_final_g'], 'v_norm1_g': out['v_norm1_g'], 'v_w_in': out['v_w_in'], 'v_conv_a_w': out['v_conv_a_w'], 'v_conv_a_b': out['v_conv_a_b'], 'v_ln_a_g': out['v_ln_a_g'], 'v_ln_a_b': out['v_ln_a_b'], 'v_ln_b_g': out['v_ln_b_g'], 'v_ln_b_b': out['v_ln_b_b'], 'v_w_spatial': out['v_w_spatial'], 'v_b_spatial': out['v_b_spatial'], 'v_conv_c_w': out['v_conv_c_w'], 'v_conv_c_b': out['v_conv_c_b'], 'v_dt_bias': out['v_dt_bias'], 'v_a_log': out['v_a_log'], 'v_d_skip': out['v_d_skip'], 'v_norm_c_g': out['v_norm_c_g'], 'v_w_out': out['v_w_out'], 'v_norm2_g': out['v_norm2_g'], 'v_w_ff1': out['v_w_ff1'], 'v_w_ff2': out['v_w_ff2'], 'v_final_g': out['v_final_g']}


def _loss(weights, diff, rest, loss_target):
    with _jax.named_scope("forward"):
        args = {**rest, TWIN_DIFF_INPUT: diff, **{k: w.astype(_WEIGHT_DTYPES[k]) for k, w in weights.items()}}
        y = _forward(args)
    with _jax.named_scope("loss_head"):
        err = _jnp.square(y.astype(_jnp.float32) - loss_target)
        return 0.5 * _jnp.sum(_jnp.mean(err, axis=-1)) if err.ndim else 0.5 * err


def _adamw(w, g, m, v):
    m = ADAM_B1 * m + (1.0 - ADAM_B1) * g
    v = ADAM_B2 * v + (1.0 - ADAM_B2) * _jnp.square(g)
    m_hat = m / (1.0 - ADAM_B1 ** ADAM_STEP)
    v_hat = v / (1.0 - ADAM_B2 ** ADAM_STEP)
    delta = -ADAM_LR * (m_hat / (_jnp.sqrt(v_hat) + ADAM_EPS) + ADAM_WD * w)
    return delta, m, v


def reference(x, norm1_g, w_in, conv_a_w, conv_a_b, ln_a_g, ln_a_b, ln_b_g, ln_b_b, w_spatial, b_spatial, conv_c_w, conv_c_b, dt_bias, a_log, d_skip, norm_c_g, w_out, norm2_g, w_ff1, w_ff2, final_g, loss_target, m_norm1_g, m_w_in, m_conv_a_w, m_conv_a_b, m_ln_a_g, m_ln_a_b, m_ln_b_g, m_ln_b_b, m_w_spatial, m_b_spatial, m_conv_c_w, m_conv_c_b, m_dt_bias, m_a_log, m_d_skip, m_norm_c_g, m_w_out, m_norm2_g, m_w_ff1, m_w_ff2, m_final_g, v_norm1_g, v_w_in, v_conv_a_w, v_conv_a_b, v_ln_a_g, v_ln_a_b, v_ln_b_g, v_ln_b_b, v_w_spatial, v_b_spatial, v_conv_c_w, v_conv_c_b, v_dt_bias, v_a_log, v_d_skip, v_norm_c_g, v_w_out, v_norm2_g, v_w_ff1, v_w_ff2, v_final_g):
    given = dict(x=x, norm1_g=norm1_g, w_in=w_in, conv_a_w=conv_a_w, conv_a_b=conv_a_b, ln_a_g=ln_a_g, ln_a_b=ln_a_b, ln_b_g=ln_b_g, ln_b_b=ln_b_b, w_spatial=w_spatial, b_spatial=b_spatial, conv_c_w=conv_c_w, conv_c_b=conv_c_b, dt_bias=dt_bias, a_log=a_log, d_skip=d_skip, norm_c_g=norm_c_g, w_out=w_out, norm2_g=norm2_g, w_ff1=w_ff1, w_ff2=w_ff2, final_g=final_g, loss_target=loss_target, m_norm1_g=m_norm1_g, m_w_in=m_w_in, m_conv_a_w=m_conv_a_w, m_conv_a_b=m_conv_a_b, m_ln_a_g=m_ln_a_g, m_ln_a_b=m_ln_a_b, m_ln_b_g=m_ln_b_g, m_ln_b_b=m_ln_b_b, m_w_spatial=m_w_spatial, m_b_spatial=m_b_spatial, m_conv_c_w=m_conv_c_w, m_conv_c_b=m_conv_c_b, m_dt_bias=m_dt_bias, m_a_log=m_a_log, m_d_skip=m_d_skip, m_norm_c_g=m_norm_c_g, m_w_out=m_w_out, m_norm2_g=m_norm2_g, m_w_ff1=m_w_ff1, m_w_ff2=m_w_ff2, m_final_g=m_final_g, v_norm1_g=v_norm1_g, v_w_in=v_w_in, v_conv_a_w=v_conv_a_w, v_conv_a_b=v_conv_a_b, v_ln_a_g=v_ln_a_g, v_ln_a_b=v_ln_a_b, v_ln_b_g=v_ln_b_g, v_ln_b_b=v_ln_b_b, v_w_spatial=v_w_spatial, v_b_spatial=v_b_spatial, v_conv_c_w=v_conv_c_w, v_conv_c_b=v_conv_c_b, v_dt_bias=v_dt_bias, v_a_log=v_a_log, v_d_skip=v_d_skip, v_norm_c_g=v_norm_c_g, v_w_out=v_w_out, v_norm2_g=v_norm2_g, v_w_ff1=v_w_ff1, v_w_ff2=v_w_ff2, v_final_g=v_final_g)
    weights = {n: given[n] for n in TWIN_WEIGHTS}
    shared = {n: given[n] for n in SHARED_INPUTS}
    per_example = {n: given[n] for n in ['x']}
    grad_fn = _jax.value_and_grad(_loss, argnums=(0, 1))

    def one_microbatch(ex, loss_target):
        ex = dict(ex)
        diff = ex.pop(TWIN_DIFF_INPUT)
        return grad_fn(weights, diff, {**shared, **ex}, loss_target)

    if N_MICROBATCH == 1:
        loss, (grad_w, grad_x) = one_microbatch(per_example, given["loss_target"])
    else:
        def body(carry, xs):
            loss_sum, grad_sum = carry
            l_k, (gw_k, gx_k) = one_microbatch(xs[0], xs[1])
            with _jax.named_scope("update"):
                return (loss_sum + l_k, _jax.tree.map(_jnp.add, grad_sum, gw_k)), gx_k

        init = (_jnp.zeros((), _jnp.float32), _jax.tree.map(_jnp.zeros_like, weights))
        (loss, grad_w), grad_x = _jax.lax.scan(body, init, (per_example, given["loss_target"]))
    with _jax.named_scope("update"):
        delta_w, new_m, new_v = {}, {}, {}
        for n in TWIN_WEIGHTS:
            delta_w[n], new_m[n], new_v[n] = _adamw(weights[n], grad_w[n], given["m_" + n], given["v_" + n])
    return (loss, grad_x, *[grad_w[n] for n in TWIN_WEIGHTS], *[delta_w[n] for n in TWIN_WEIGHTS],
            *[new_m[n] for n in TWIN_WEIGHTS], *[new_v[n] for n in TWIN_WEIGHTS])
```

```python
import functools
import math

import jax
import jax.numpy as jnp
from jax import lax
from jax.experimental import pallas as pl
from jax.experimental.pallas import tpu as pltpu

f32 = jnp.float32
bf16 = jnp.bfloat16
HI = lax.Precision.HIGHEST
EPS = 1e-5
HEAD = 64
CHUNK = 128
KA = 31
KC = 4
N_DEV = 8
VMEM_LIMIT = 56 * 1024 * 1024

ADAM_LR = 0.001
ADAM_B1 = 0.9
ADAM_B2 = 0.999
ADAM_EPS = 1e-08
ADAM_WD = 0.01
ADAM_STEP = 10


def _cparams(sem=None):
    return pltpu.CompilerParams(dimension_semantics=sem, vmem_limit_bytes=VMEM_LIMIT)


def _sds(shape, dtype):
    return jax.ShapeDtypeStruct(shape, dtype)


_DIMS = {"nn": ((1,), (0,)), "nt": ((1,), (1,)), "tn": ((0,), (0,))}


def _mm(name, a, b, form, out_dtypes, epilogue=None, extras=(), tm=512, tn=512, tk=512):
    if form == "tn":
        K, M = a.shape
    else:
        M, K = a.shape
    N = b.shape[0] if form == "nt" else b.shape[1]
    tm, tn, tk = min(tm, M), min(tn, N), min(tk, K)
    assert M % tm == 0 and N % tn == 0 and K % tk == 0, (name, M, N, K, tm, tn, tk)
    nk = K // tk
    ne, no = len(extras), len(out_dtypes)
    if epilogue is None:
        epilogue = lambda acc: (acc,)

    def body(a_ref, b_ref, *rest):
        extra_refs, out_refs, acc_ref = rest[:ne], rest[ne:ne + no], rest[ne + no]
        k = pl.program_id(2)

        @pl.when(k == 0)
        def _():
            acc_ref[...] = jnp.zeros_like(acc_ref)

        acc_ref[...] += lax.dot_general(a_ref[...].astype(bf16), b_ref[...].astype(bf16),
                                        (_DIMS[form], ((), ())), preferred_element_type=f32)

        @pl.when(k == nk - 1)
        def _():
            outs = epilogue(acc_ref[...], *[e[...] for e in extra_refs])
            for o_ref, v in zip(out_refs, outs):
                o_ref[...] = v.astype(o_ref.dtype)

    a_spec = pl.BlockSpec((tk, tm), lambda i, j, k: (k, i)) if form == "tn" else pl.BlockSpec((tm, tk), lambda i, j, k: (i, k))
    b_spec = pl.BlockSpec((tn, tk), lambda i, j, k: (j, k)) if form == "nt" else pl.BlockSpec((tk, tn), lambda i, j, k: (k, j))
    mn_spec = pl.BlockSpec((tm, tn), lambda i, j, k: (i, j))
    return pl.pallas_call(
        body, name=name, grid=(M // tm, N // tn, nk),
        in_specs=[a_spec, b_spec] + [mn_spec] * ne,
        out_specs=[mn_spec] * no,
        out_shape=[_sds((M, N), d) for d in out_dtypes],
        scratch_shapes=[pltpu.VMEM((tm, tn), f32)],
        compiler_params=_cparams(("parallel", "parallel", "arbitrary")),
    )(a, b, *extras)


def _ep_add(acc, r):
    return (acc + r,)


def _ep_relu2(acc):
    r = jnp.maximum(acc, 0.0)
    return acc, r * r


def _ep_drelu2(acc, f):
    return (acc * 2.0 * jnp.maximum(f, 0.0),)


def _rms(x, g):
    return x * lax.rsqrt(jnp.mean(x * x, axis=-1, keepdims=True) + EPS) * g


TT = 512


def _rms_fwd(x, g):
    T, D = x.shape

    def body(x_ref, g_ref, h_ref):
        h_ref[...] = _rms(x_ref[...], g_ref[...]).astype(bf16)

    return pl.pallas_call(
        body, name="rms_fwd", grid=(T // TT,),
        in_specs=[pl.BlockSpec((TT, D), lambda i: (i, 0)), pl.BlockSpec((1, D), lambda i: (0, 0))],
        out_specs=pl.BlockSpec((TT, D), lambda i: (i, 0)),
        out_shape=_sds((T, D), bf16), compiler_params=_cparams(("arbitrary",)),
    )(x, g.reshape(1, D))


def _rms_bwd(x, g, dh, dres):
    T, D = x.shape

    def body(x_ref, g_ref, dh_ref, dres_ref, dx_ref, dg_ref):
        _, vjp = jax.vjp(_rms, x_ref[...], g_ref[...])
        dx, dg = vjp(dh_ref[...])
        dx_ref[...] = dres_ref[...] + dx

        @pl.when(pl.program_id(0) == 0)
        def _():
            dg_ref[...] = jnp.zeros_like(dg_ref)

        dg_ref[0:1, :] += dg

    tile = pl.BlockSpec((TT, D), lambda i: (i, 0))
    return pl.pallas_call(
        body, name="rms_bwd", grid=(T // TT,),
        in_specs=[tile, pl.BlockSpec((1, D), lambda i: (0, 0)), tile, tile],
        out_specs=[tile, pl.BlockSpec((8, D), lambda i: (0, 0))],
        out_shape=[_sds((T, D), f32), _sds((8, D), f32)], compiler_params=_cparams(("arbitrary",)),
    )(x, g.reshape(1, D), dh, dres)


def _loss_head(x, g, tgt):
    T, D = x.shape

    def f(xv, gv, tv):
        e = _rms(xv, gv) - tv
        return 0.5 * jnp.sum(jnp.sum(e * e, axis=-1, keepdims=True) * (1.0 / D), axis=0, keepdims=True)

    def body(x_ref, g_ref, t_ref, loss_ref, dx_ref, dg_ref):
        tv = t_ref[...]
        l, vjp = jax.vjp(lambda xv, gv: f(xv, gv, tv), x_ref[...], g_ref[...])
        dx, dg = vjp(jnp.ones((1, 1), f32))
        dx_ref[...] = dx

        @pl.when(pl.program_id(0) == 0)
        def _():
            dg_ref[...] = jnp.zeros_like(dg_ref)
            loss_ref[...] = jnp.zeros_like(loss_ref)

        dg_ref[0:1, :] += dg
        loss_ref[...] += jnp.broadcast_to(l, loss_ref.shape)

    tile = pl.BlockSpec((TT, D), lambda i: (i, 0))
    return pl.pallas_call(
        body, name="loss_head", grid=(T // TT,),
        in_specs=[tile, pl.BlockSpec((1, D), lambda i: (0, 0)), tile],
        out_specs=[pl.BlockSpec((8, 128), lambda i: (0, 0)), tile, pl.BlockSpec((8, D), lambda i: (0, 0))],
        out_shape=[_sds((8, 128), f32), _sds((T, D), f32), _sds((8, D), f32)],
        compiler_params=_cparams(("arbitrary",)),
    )(x, g.reshape(1, D), tgt)


TB = 256


def _glu(a_val, a_gate):
    return a_val * jax.nn.sigmoid(a_gate)


def _ln_silu(v, g, b):
    mu = jnp.mean(v, axis=-1, keepdims=True)
    vc = v - mu
    var = jnp.mean(vc * vc, axis=-1, keepdims=True)
    return jax.nn.silu(vc * lax.rsqrt(var + EPS) * g + b)


def _conv_geom(kw):
    halo = 32 if kw > 9 else 8
    return halo, halo - (kw - 1)


def _conv_taps(hp_ref, w_ref, b_ref, acc_ref, kw, off, width):
    for c in range(width // 128):
        ls = pl.ds(c * 128, 128)
        acc = jnp.broadcast_to(b_ref[:, ls], (TB, 128))
        for k in range(kw):
            acc = acc + w_ref[k:k + 1, ls] * hp_ref[pl.ds(off + k, TB), ls]
        acc_ref[:, ls] = acc


def _conv_fwd(name, src, col_block, w, b, kw, conformer, n_seq, ln_g=None, ln_b=None):
    T = src.shape[0]
    cout = w.shape[1]
    cin = 2 * cout if conformer else cout
    halo, off = _conv_geom(kw)
    nblk = T // n_seq // TB
    hb = TB // halo

    def body(cur_ref, halo_ref, w_ref, b_ref, *rest):
        if conformer:
            g_ref, lb_ref, out_ref, hp_ref, acc_ref = rest
        else:
            out_ref, hp_ref, acc_ref = rest
        i = pl.program_id(1)
        first = (i == 0)
        if conformer:
            hp_ref[pl.ds(halo, TB), :] = _glu(cur_ref[:, 0:cout], cur_ref[:, cout:cin])
            hh = _glu(halo_ref[:, 0:cout], halo_ref[:, cout:cin])
        else:
            hp_ref[pl.ds(halo, TB), :] = cur_ref[...]
            hh = halo_ref[...]
        hp_ref[pl.ds(0, halo), :] = jnp.where(first, 0.0, hh)
        _conv_taps(hp_ref, w_ref, b_ref, acc_ref, kw, off, cout)
        if conformer:
            for h in range(cout // HEAD):
                ls = pl.ds(h * HEAD, HEAD)
                out_ref[:, ls] = _ln_silu(acc_ref[:, ls], g_ref[:, ls], lb_ref[:, ls]).astype(out_ref.dtype)
        else:
            out_ref[...] = jax.nn.silu(acc_ref[...]).astype(out_ref.dtype)

    row = pl.BlockSpec((1, cout), lambda s, i: (0, 0))
    in_specs = [pl.BlockSpec((TB, cin), lambda s, i: (s * nblk + i, col_block)),
                pl.BlockSpec((halo, cin), lambda s, i: (jnp.maximum((s * nblk + i) * hb - 1, 0), col_block)),
                pl.BlockSpec((w.shape[0], cout), lambda s, i: (0, 0)), row]
    args = [src, src, w, b.reshape(1, cout)]
    if conformer:
        in_specs += [row, row]
        args += [ln_g.reshape(1, cout), ln_b.reshape(1, cout)]
    out_dtype = bf16 if conformer else f32
    return pl.pallas_call(
        body, name=name, grid=(n_seq, nblk), in_specs=in_specs,
        out_specs=pl.BlockSpec((TB, cout), lambda s, i: (s * nblk + i, 0)),
        out_shape=_sds((T, cout), out_dtype),
        scratch_shapes=[pltpu.VMEM((halo + TB, cout), f32), pltpu.VMEM((TB, cout), f32)],
        compiler_params=_cparams(("arbitrary", "arbitrary")),
    )(*args)


def _conv_bwd(name, src, col_block, w, b, dy, dy_col_block, kw, conformer, n_seq, ln_g=None, ln_b=None):
    T = src.shape[0]
    cout = w.shape[1]
    wrows = w.shape[0]
    cin = 2 * cout if conformer else cout
    halo, off = _conv_geom(kw)
    nblk = T // n_seq // TB
    hb = TB // halo

    def body(cur_ref, halo_ref, w_ref, b_ref, dy_ref, *rest):
        if conformer:
            g_ref, lb_ref, dsrc_ref, dw_ref, db_ref, dg_ref, dlb_ref, hp_ref, acc_ref, dz_ref, dhp_ref, carry_ref = rest
        else:
            dsrc_ref, dw_ref, db_ref, hp_ref, acc_ref, dz_ref, dhp_ref, carry_ref = rest
        s, ii = pl.program_id(0), pl.program_id(1)
        i = nblk - 1 - ii
        first = (i == 0)

        @pl.when((s == 0) & (ii == 0))
        def _():
            dw_ref[...] = jnp.zeros_like(dw_ref)
            db_ref[...] = jnp.zeros_like(db_ref)
            if conformer:
                dg_ref[...] = jnp.zeros_like(dg_ref)
                dlb_ref[...] = jnp.zeros_like(dlb_ref)

        @pl.when(ii == 0)
        def _():
            carry_ref[...] = jnp.zeros_like(carry_ref)
            dz_ref[pl.ds(0, halo), :] = jnp.zeros((halo, cout), f32)
            dz_ref[pl.ds(halo + TB, halo), :] = jnp.zeros((halo, cout), f32)

        if conformer:
            hp_ref[pl.ds(halo, TB), :] = _glu(cur_ref[:, 0:cout], cur_ref[:, cout:cin])
            hh = _glu(halo_ref[:, 0:cout], halo_ref[:, cout:cin])
        else:
            hp_ref[pl.ds(halo, TB), :] = cur_ref[...]
            hh = halo_ref[...]
        hp_ref[pl.ds(0, halo), :] = jnp.where(first, 0.0, hh)
        _conv_taps(hp_ref, w_ref, b_ref, acc_ref, kw, off, cout)

        if conformer:
            for h in range(cout // HEAD):
                ls = pl.ds(h * HEAD, HEAD)
                _, vjp = jax.vjp(_ln_silu, acc_ref[:, ls], g_ref[:, ls], lb_ref[:, ls])
                da, dg, dlb = vjp(dy_ref[:, ls].astype(f32))
                dz_ref[pl.ds(halo, TB), ls] = da
                dg_ref[0:1, ls] += dg
                dlb_ref[0:1, ls] += dlb
        else:
            _, vjp = jax.vjp(jax.nn.silu, acc_ref[...])
            dz_ref[pl.ds(halo, TB), :] = vjp(dy_ref[...].astype(f32))[0]

        for c in range(cout // 128):
            ls = pl.ds(c * 128, 128)
            dacc = dz_ref[pl.ds(halo, TB), ls]
            db_ref[0:1, ls] += jnp.sum(dacc, axis=0, keepdims=True)
            dhp = jnp.zeros((halo + TB, 128), f32)
            for k in range(kw):
                dw_ref[k:k + 1, ls] += jnp.sum(dacc * hp_ref[pl.ds(off + k, TB), ls], axis=0, keepdims=True)
                dhp = dhp + w_ref[k:k + 1, ls] * dz_ref[pl.ds(kw - 1 - k, halo + TB), ls]
            dhp_ref[:, ls] = dhp
        dhp_ref[pl.ds(TB, halo), :] += carry_ref[...]
        carry_ref[...] = dhp_ref[pl.ds(0, halo), :]
        dcur = dhp_ref[pl.ds(halo, TB), :]
        if conformer:
            _, vjp = jax.vjp(_glu, cur_ref[:, 0:cout], cur_ref[:, cout:cin])
            dval, dgate = vjp(dcur)
            dsrc_ref[:, 0:cout] = dval.astype(dsrc_ref.dtype)
            dsrc_ref[:, cout:cin] = dgate.astype(dsrc_ref.dtype)
        else:
            dsrc_ref[...] = dcur.astype(dsrc_ref.dtype)

    def blk(s, ii):
        return s * nblk + (nblk - 1 - ii)

    row = pl.BlockSpec((1, cout), lambda s, ii: (0, 0))
    acc8 = pl.BlockSpec((8, cout), lambda s, ii: (0, 0))
    in_specs = [pl.BlockSpec((TB, cin), lambda s, ii: (blk(s, ii), col_block)),
                pl.BlockSpec((halo, cin), lambda s, ii: (jnp.maximum(blk(s, ii) * hb - 1, 0), col_block)),
                pl.BlockSpec((wrows, cout), lambda s, ii: (0, 0)), row,
                pl.BlockSpec((TB, cout), lambda s, ii: (blk(s, ii), dy_col_block))]
    args = [src, src, w, b.reshape(1, cout), dy]
    out_specs = [pl.BlockSpec((TB, cin), lambda s, ii: (blk(s, ii), 0)),
                 pl.BlockSpec((wrows, cout), lambda s, ii: (0, 0)), acc8]
    out_shape = [_sds((T, cin), bf16), _sds((wrows, cout), f32), _sds((8, cout), f32)]
    if conformer:
        in_specs += [row, row]
        args += [ln_g.reshape(1, cout), ln_b.reshape(1, cout)]
        out_specs += [acc8, acc8]
        out_shape += [_sds((8, cout), f32), _sds((8, cout), f32)]
    return pl.pallas_call(
        body, name=name, grid=(n_seq, nblk), in_specs=in_specs, out_specs=out_specs, out_shape=out_shape,
        scratch_shapes=[pltpu.VMEM((halo + TB, cout), f32), pltpu.VMEM((TB, cout), f32),
                        pltpu.VMEM((halo + TB + halo, cout), f32), pltpu.VMEM((halo + TB, cout), f32),
                        pltpu.VMEM((halo, cout), f32)],
        compiler_params=_cparams(("arbitrary", "arbitrary")),
    )(*args)


def _gelu(x):
    return 0.5 * x * (1.0 + lax.erf(x * (1.0 / math.sqrt(2.0))))


def _tril_mask(n):
    r = lax.broadcasted_iota(jnp.int32, (n, n), 0)
    c = lax.broadcasted_iota(jnp.int32, (n, n), 1)
    return r >= c


def _row_select(rows, h, width):
    r = lax.broadcasted_iota(jnp.int32, (rows, width), 0)
    return (r == h).astype(f32)


def _gmlp_head(h, bu, bv, g, b, w_h, bs):
    u = _gelu(bu)
    v = _gelu(bv)
    mu = jnp.mean(v, axis=-1, keepdims=True)
    vc = v - mu
    var = jnp.mean(vc * vc, axis=-1, keepdims=True)
    vn = vc * lax.rsqrt(var + EPS) * g + b
    wm = jnp.where(_tril_mask(CHUNK), w_h, 0.0)
    mix = jnp.dot(wm.astype(bf16), vn.astype(bf16), preferred_element_type=f32)
    bias = lax.dot_general(bs, _row_select(bs.shape[0], h, HEAD), (((0,), (0,)), ((), ())),
                           precision=HI, preferred_element_type=f32)
    return u * (mix + bias)


def _gmlp_fwd(proj, col_block, ln_g, ln_b, w_s, b_s):
    T = proj.shape[0]
    nh = w_s.shape[0]
    width = nh * HEAD

    def body(p_ref, g_ref, b_ref, w_ref, bs_ref, out_ref):
        bs = bs_ref[...]
        for h in range(nh):
            ls = pl.ds(h * HEAD, HEAD)
            lv = pl.ds(width + h * HEAD, HEAD)
            out_ref[:, ls] = _gmlp_head(h, p_ref[:, ls], p_ref[:, lv], g_ref[:, ls], b_ref[:, ls], w_ref[h], bs).astype(out_ref.dtype)

    row = pl.BlockSpec((1, width), lambda i: (0, 0))
    return pl.pallas_call(
        body, name="gmlp_fwd", grid=(T // CHUNK,),
        in_specs=[pl.BlockSpec((CHUNK, 2 * width), lambda i: (i, col_block)), row, row,
                  pl.BlockSpec((nh, CHUNK, CHUNK), lambda i: (0, 0, 0)), pl.BlockSpec((nh, CHUNK), lambda i: (0, 0))],
        out_specs=pl.BlockSpec((CHUNK, width), lambda i: (i, 0)),
        out_shape=_sds((T, width), bf16), compiler_params=_cparams(("arbitrary",)),
    )(proj, ln_g.reshape(1, width), ln_b.reshape(1, width), w_s, b_s)


def _gmlp_bwd(proj, col_block, ln_g, ln_b, w_s, b_s, dy, dy_col_block):
    T = proj.shape[0]
    nh = w_s.shape[0]
    width = nh * HEAD

    def body(p_ref, g_ref, b_ref, w_ref, bs_ref, dy_ref, dp_ref, dg_ref, db_ref, dw_ref, dbs_ref):
        @pl.when(pl.program_id(0) == 0)
        def _():
            dg_ref[...] = jnp.zeros_like(dg_ref)
            db_ref[...] = jnp.zeros_like(db_ref)
            dw_ref[...] = jnp.zeros_like(dw_ref)
            dbs_ref[...] = jnp.zeros_like(dbs_ref)

        bs = bs_ref[...]
        for h in range(nh):
            ls = pl.ds(h * HEAD, HEAD)
            lv = pl.ds(width + h * HEAD, HEAD)
            _, vjp = jax.vjp(functools.partial(_gmlp_head, h), p_ref[:, ls], p_ref[:, lv], g_ref[:, ls], b_ref[:, ls], w_ref[h], bs)
            dbu, dbv, dg, db, dw, dbs = vjp(dy_ref[:, ls].astype(f32))
            dp_ref[:, ls] = dbu.astype(dp_ref.dtype)
            dp_ref[:, lv] = dbv.astype(dp_ref.dtype)
            dg_ref[0:1, ls] += dg
            db_ref[0:1, ls] += db
            dw_ref[h] += dw
            dbs_ref[...] += dbs

    row = pl.BlockSpec((1, width), lambda i: (0, 0))
    acc8 = pl.BlockSpec((8, width), lambda i: (0, 0))
    wspec = pl.BlockSpec((nh, CHUNK, CHUNK), lambda i: (0, 0, 0))
    bspec = pl.BlockSpec((nh, CHUNK), lambda i: (0, 0))
    return pl.pallas_call(
        body, name="gmlp_bwd", grid=(T // CHUNK,),
        in_specs=[pl.BlockSpec((CHUNK, 2 * width), lambda i: (i, col_block)), row, row, wspec, bspec,
                  pl.BlockSpec((CHUNK, width), lambda i: (i, dy_col_block))],
        out_specs=[pl.BlockSpec((CHUNK, 2 * width), lambda i: (i, 0)), acc8, acc8, wspec, bspec],
        out_shape=[_sds((T, 2 * width), bf16), _sds((8, width), f32), _sds((8, width), f32),
                   _sds((nh, CHUNK, CHUNK), f32), _sds((nh, CHUNK), f32)],
        compiler_params=_cparams(("arbitrary",)),
    )(proj, ln_g.reshape(1, width), ln_b.reshape(1, width), w_s, b_s, dy)


def _sel_col(x, h):
    lane = lax.broadcasted_iota(jnp.int32, x.shape, 1)
    return jnp.sum(jnp.where(lane == h, x, 0.0), axis=1, keepdims=True)


def _sel_row(x, h):
    sub = lax.broadcasted_iota(jnp.int32, x.shape, 0)
    return jnp.sum(jnp.where(sub == h, x, 0.0), axis=0, keepdims=True)


def _ssd_chunk(nh, ngrp, xs_l, z_l, b_l, c_l, dtraw, dtb, alog, dskip, ng_l, prev_l):
    hg = nh // ngrp
    tril = _tril_mask(CHUNK)
    tl = tril.astype(f32)
    dt = jax.nn.softplus(dtraw + dtb)
    a = dt * (-jnp.exp(alog))
    cs = jnp.dot(tl, a, precision=HI, preferred_element_type=f32)
    cst = lax.dot_general(a, tl, (((0,), (1,)), ((), ())), precision=HI, preferred_element_type=f32)
    cb_l = [lax.dot_general(c_l[g].astype(bf16), b_l[g].astype(bf16), (((1,), (1,)), ((), ())),
                            preferred_element_type=f32) for g in range(ngrp)]
    yz_l, new_prev = [], []
    for h in range(nh):
        g = h // hg
        dt_h = _sel_col(dt, h)
        cs_h = _sel_col(cs, h)
        cs_r = _sel_row(cst, h)
        tot = _sel_row(cs_h, CHUNK - 1)
        seg = jnp.where(tril, cs_h - cs_r, 0.0)
        lmat = jnp.where(tril, jnp.exp(seg), 0.0)
        x = xs_l[h] * dt_h
        ydiag = jnp.dot((cb_l[g] * lmat).astype(bf16), x.astype(bf16), preferred_element_type=f32)
        yoff = jnp.dot(c_l[g].astype(bf16), prev_l[h].astype(bf16), preferred_element_type=f32) * jnp.exp(cs_h)
        st = lax.dot_general(b_l[g].astype(bf16), (x * jnp.exp(tot - cs_h)).astype(bf16), (((0,), (0,)), ((), ())),
                             preferred_element_type=f32)
        new_prev.append(prev_l[h] * jnp.exp(tot) + st)
        y = ydiag + yoff + _sel_col(dskip, h) * xs_l[h]
        yz_l.append(y * jax.nn.silu(z_l[h]))
    out = [None] * nh
    for g in range(ngrp):
        ssq = sum(jnp.sum(yz_l[h] * yz_l[h], axis=-1, keepdims=True) for h in range(g * hg, (g + 1) * hg))
        r = lax.rsqrt(ssq * (1.0 / (hg * HEAD)) + EPS)
        for h in range(g * hg, (g + 1) * hg):
            out[h] = yz_l[h] * r * ng_l[h]
    return out, new_prev


def _ssd_read(nh, ngrp, nst, xbc_ref, z_ref, ng_ref, st_ref):
    cw = nh * HEAD
    xs_l = [xbc_ref[:, pl.ds(h * HEAD, HEAD)] for h in range(nh)]
    b_l = [xbc_ref[:, pl.ds(cw + g * nst, nst)] for g in range(ngrp)]
    c_l = [xbc_ref[:, pl.ds(cw + ngrp * nst + g * nst, nst)] for g in range(ngrp)]
    z_l = [z_ref[:, pl.ds(h * HEAD, HEAD)] for h in range(nh)]
    ng_l = [ng_ref[:, pl.ds(h * HEAD, HEAD)] for h in range(nh)]
    prev_l = [st_ref[:, pl.ds(h * HEAD, HEAD)] for h in range(nh)]
    return xs_l, z_l, b_l, c_l, ng_l, prev_l


def _ssd_fwd(xbc, proj, z_col_block, pdt, dtb, alog, dskip, ng, nh, ngrp, nst, n_seq):
    T = xbc.shape[0]
    cw = nh * HEAD
    nchunk = T // n_seq // CHUNK
    assert nst == CHUNK

    def body(xbc_ref, z_ref, dt_ref, dtb_ref, alog_ref, dskip_ref, ng_ref, y_ref, sin_ref, st_ref):
        @pl.when(pl.program_id(1) == 0)
        def _():
            st_ref[...] = jnp.zeros_like(st_ref)

        sin_ref[...] = st_ref[...]
        xs_l, z_l, b_l, c_l, ng_l, prev_l = _ssd_read(nh, ngrp, nst, xbc_ref, z_ref, ng_ref, st_ref)
        y_l, new_prev = _ssd_chunk(nh, ngrp, xs_l, z_l, b_l, c_l, dt_ref[...], dtb_ref[...], alog_ref[...],
                                   dskip_ref[...], ng_l, prev_l)
        for h in range(nh):
            ls = pl.ds(h * HEAD, HEAD)
            y_ref[:, ls] = y_l[h].astype(y_ref.dtype)
            st_ref[:, ls] = new_prev[h]

    def blk(s, c):
        return s * nchunk + c

    prow = pl.BlockSpec((1, 128), lambda s, c: (0, 0))
    return pl.pallas_call(
        body, name="ssd_fwd", grid=(n_seq, nchunk),
        in_specs=[pl.BlockSpec((CHUNK, xbc.shape[1]), lambda s, c: (blk(s, c), 0)),
                  pl.BlockSpec((CHUNK, cw), lambda s, c: (blk(s, c), z_col_block)),
                  pl.BlockSpec((CHUNK, 128), lambda s, c: (blk(s, c), 0)),
                  prow, prow, prow, pl.BlockSpec((1, cw), lambda s, c: (0, 0))],
        out_specs=[pl.BlockSpec((CHUNK, cw), lambda s, c: (blk(s, c), 0)),
                   pl.BlockSpec((nst, cw), lambda s, c: (blk(s, c), 0))],
        out_shape=[_sds((T, cw), bf16), _sds((T, cw), f32)],
        scratch_shapes=[pltpu.VMEM((nst, cw), f32)],
        compiler_params=_cparams(("arbitrary", "arbitrary")),
    )(xbc, proj, pdt, dtb, alog, dskip, ng.reshape(1, cw))


def _ssd_bwd(xbc, proj, z_col_block, pdt, dtb, alog, dskip, ng, sin, dy, dy_col_block, nh, ngrp, nst, n_seq):
    T, xw = xbc.shape
    cw = nh * HEAD
    nchunk = T // n_seq // CHUNK

    def body(xbc_ref, z_ref, dt_ref, dtb_ref, alog_ref, dskip_ref, ng_ref, sin_ref, dy_ref,
             dxbc_ref, dz_ref, ddt_ref, ddtb_ref, dalog_ref, ddskip_ref, dng_ref, dst_ref):
        s, cc = pl.program_id(0), pl.program_id(1)

        @pl.when((s == 0) & (cc == 0))
        def _():
            ddtb_ref[...] = jnp.zeros_like(ddtb_ref)
            dalog_ref[...] = jnp.zeros_like(dalog_ref)
            ddskip_ref[...] = jnp.zeros_like(ddskip_ref)
            dng_ref[...] = jnp.zeros_like(dng_ref)

        @pl.when(cc == 0)
        def _():
            dst_ref[...] = jnp.zeros_like(dst_ref)

        xs_l, z_l, b_l, c_l, ng_l, prev_l = _ssd_read(nh, ngrp, nst, xbc_ref, z_ref, ng_ref, sin_ref)
        _, vjp = jax.vjp(functools.partial(_ssd_chunk, nh, ngrp), xs_l, z_l, b_l, c_l, dt_ref[...], dtb_ref[...],
                         alog_ref[...], dskip_ref[...], ng_l, prev_l)
        dy_l = [dy_ref[:, pl.ds(h * HEAD, HEAD)].astype(f32) for h in range(nh)]
        dst_l = [dst_ref[:, pl.ds(h * HEAD, HEAD)] for h in range(nh)]
        dxs_l, dz_l, db_l, dc_l, ddt, ddtb, dalog, ddskip, dng_l, dprev_l = vjp((dy_l, dst_l))
        for h in range(nh):
            ls = pl.ds(h * HEAD, HEAD)
            dxbc_ref[:, ls] = dxs_l[h]
            dz_ref[:, ls] = dz_l[h].astype(dz_ref.dtype)
            dng_ref[0:1, ls] += dng_l[h]
            dst_ref[:, ls] = dprev_l[h]
        for g in range(ngrp):
            dxbc_ref[:, pl.ds(cw + g * nst, nst)] = db_l[g]
            dxbc_ref[:, pl.ds(cw + ngrp * nst + g * nst, nst)] = dc_l[g]
        ddt_ref[...] = ddt.astype(ddt_ref.dtype)
        ddtb_ref[0:1, :] += ddtb
        dalog_ref[0:1, :] += dalog
        ddskip_ref[0:1, :] += ddskip

    def blk(s, cc):
        return s * nchunk + (nchunk - 1 - cc)

    prow = pl.BlockSpec((1, 128), lambda s, c: (0, 0))
    pacc = pl.BlockSpec((8, 128), lambda s, c: (0, 0))
    return pl.pallas_call(
        body, name="ssd_bwd", grid=(n_seq, nchunk),
        in_specs=[pl.BlockSpec((CHUNK, xw), lambda s, c: (blk(s, c), 0)),
                  pl.BlockSpec((CHUNK, cw), lambda s, c: (blk(s, c), z_col_block)),
                  pl.BlockSpec((CHUNK, 128), lambda s, c: (blk(s, c), 0)),
                  prow, prow, prow, pl.BlockSpec((1, cw), lambda s, c: (0, 0)),
                  pl.BlockSpec((nst, cw), lambda s, c: (blk(s, c), 0)),
                  pl.BlockSpec((CHUNK, cw), lambda s, c: (blk(s, c), dy_col_block))],
        out_specs=[pl.BlockSpec((CHUNK, xw), lambda s, c: (blk(s, c), 0)),
                   pl.BlockSpec((CHUNK, cw), lambda s, c: (blk(s, c), 0)),
                   pl.BlockSpec((CHUNK, 128), lambda s, c: (blk(s, c), 0)),
                   pacc, pacc, pacc, pl.BlockSpec((8, cw), lambda s, c: (0, 0))],
        out_shape=[_sds((T, xw), f32), _sds((T, cw), bf16), _sds((T, 128), bf16),
                   _sds((8, 128), f32), _sds((8, 128), f32), _sds((8, 128), f32), _sds((8, cw), f32)],
        scratch_shapes=[pltpu.VMEM((nst, cw), f32)],
        compiler_params=_cparams(("arbitrary", "arbitrary")),
    )(xbc, proj, pdt, dtb, alog, dskip, ng.reshape(1, cw), sin, dy)


_HBM = pl.BlockSpec(memory_space=pltpu.HBM)


def _exchange(name, arrs, scatter):
    n = len(arrs)
    npeer = N_DEV - 1

    def body(*refs):
        in_refs, out_refs = refs[:n], refs[n:2 * n]
        send_sems, recv_sems, local_sems = refs[2 * n:]
        x, y, c = lax.axis_index("x"), lax.axis_index("y"), lax.axis_index("c")
        me = 4 * x + 2 * y + c
        copies = []
        for i in range(n):
            src = in_refs[i].at[me] if scatter else in_refs[i]
            loc = pltpu.make_async_copy(src, out_refs[i].at[me], local_sems.at[i])
            loc.start()
            copies.append(loc)
            for k in range(1, N_DEV):
                px = 1 - x if k & 4 else x
                py = 1 - y if k & 2 else y
                pc = 1 - c if k & 1 else c
                src = in_refs[i].at[4 * px + 2 * py + pc] if scatter else in_refs[i]
                cp = pltpu.make_async_remote_copy(
                    src_ref=src, dst_ref=out_refs[i].at[me],
                    send_sem=send_sems.at[i * npeer + k - 1], recv_sem=recv_sems.at[i * npeer + k - 1],
                    device_id=(px, py, pc), device_id_type=pl.DeviceIdType.MESH)
                cp.start()
                copies.append(cp)
        for cp in copies:
            cp.wait()

    out_shape = [_sds(a.shape if scatter else (N_DEV,) + a.shape, a.dtype) for a in arrs]
    return pl.pallas_call(
        body, name=name, in_specs=[_HBM] * n, out_specs=[_HBM] * n, out_shape=out_shape,
        scratch_shapes=[pltpu.SemaphoreType.DMA((n * npeer,)), pltpu.SemaphoreType.DMA((n * npeer,)),
                        pltpu.SemaphoreType.DMA((n,))],
    )(*arrs)


def _adam_tiles(R, C):
    if R % 256 == 0:
        return (256, C), (R // 256, 1)
    assert C % 128 == 0
    return (R, 128), (1, C // 128)


def _adam(name, parts, w, m, v):
    P, R, C = parts.shape
    (tr, tc), (gr, gc) = _adam_tiles(R, C)
    c1 = 1.0 / (1.0 - ADAM_B1 ** ADAM_STEP)
    c2 = 1.0 / (1.0 - ADAM_B2 ** ADAM_STEP)

    def body(p_ref, w_ref, m_ref, v_ref, g_ref, d_ref, nm_ref, nv_ref):
        g = p_ref[0].astype(f32)
        for s in range(1, P):
            g = g + p_ref[s].astype(f32)
        nm = ADAM_B1 * m_ref[...] + (1.0 - ADAM_B1) * g
        nv = ADAM_B2 * v_ref[...] + (1.0 - ADAM_B2) * (g * g)
        g_ref[...] = g
        nm_ref[...] = nm
        nv_ref[...] = nv
        d_ref[...] = -ADAM_LR * ((nm * c1) / (jnp.sqrt(nv * c2) + ADAM_EPS) + ADAM_WD * w_ref[...])

    tile = pl.BlockSpec((tr, tc), lambda i, j: (i, j))
    return pl.pallas_call(
        body, name=name, grid=(gr, gc),
        in_specs=[pl.BlockSpec((P, tr, tc), lambda i, j: (0, i, j)), tile, tile, tile],
        out_specs=[tile] * 4, out_shape=[_sds((R, C), f32)] * 4,
        compiler_params=_cparams(("arbitrary", "arbitrary")),
    )(parts, w, m, v)


def _sum_parts(name, parts):
    P, R, C = parts.shape
    tr = 256 if R % 256 == 0 else R

    def body(p_ref, o_ref):
        g = p_ref[0]
        for s in range(1, P):
            g = g + p_ref[s]
        o_ref[...] = g

    return pl.pallas_call(
        body, name=name, grid=(R // tr,),
        in_specs=[pl.BlockSpec((P, tr, C), lambda i: (0, i, 0))], out_specs=pl.BlockSpec((tr, C), lambda i: (i, 0)),
        out_shape=_sds((R, C), f32), compiler_params=_cparams(("arbitrary",)),
    )(parts)


def _pad_to(a, n, axis):
    if a.shape[axis] == n:
        return a
    cfg = [(0, 0)] * a.ndim
    cfg[axis] = (0, n - a.shape[axis])
    return jnp.pad(a, cfg)


def _pack(layered, final):
    depth = layered[0].shape[0]
    cols = []
    for a in layered:
        a = a.reshape(depth, -1)
        cols.append(_pad_to(a, -(-a.shape[1] // 128) * 128, 1))
    body = jnp.concatenate(cols, axis=1).reshape(-1, 128)
    rows = jnp.concatenate([body, final.reshape(-1, 128)], axis=0)
    return _pad_to(rows, -(-rows.shape[0] // 256) * 256, 0)


def _unpack(slab, shapes, final_n):
    depth = shapes[0][0]
    widths = [-(-math.prod(s[1:]) // 128) * 128 for s in shapes]
    nl = sum(widths)
    body = slab[:depth * nl // 128].reshape(depth, nl)
    out, o = [], 0
    for s, wd in zip(shapes, widths):
        out.append(body[:, o:o + math.prod(s[1:])].reshape(s))
        o += wd
    r0 = depth * nl // 128
    final = slab[r0:r0 + final_n // 128].reshape(final_n)
    return out, final


_NAMES = ['norm1_g', 'w_in', 'conv_a_w', 'conv_a_b', 'ln_a_g', 'ln_a_b', 'ln_b_g', 'ln_b_b', 'w_spatial', 'b_spatial',
          'conv_c_w', 'conv_c_b', 'dt_bias', 'a_log', 'd_skip', 'norm_c_g', 'w_out', 'norm2_g', 'w_ff1', 'w_ff2', 'final_g']
_REPL = ['norm1_g', 'conv_a_b', 'ln_a_g', 'ln_a_b', 'ln_b_g', 'ln_b_b', 'w_spatial', 'b_spatial', 'conv_c_b',
         'dt_bias', 'a_log', 'd_skip', 'norm_c_g', 'norm2_g']
_CONVW = ['conv_a_w', 'conv_c_w']
_BIG = ['w_in', 'w_out', 'w_ff1', 'w_ff2']
_BIG_T = {'w_in': True, 'w_out': False, 'w_ff1': True, 'w_ff2': False}


def _row128(v):
    return _pad_to(v.reshape(1, -1), 128, 1)


def _step(p, m, v, x, loss_target):
    nb, S, D = x.shape
    T = nb * S
    depth = p['norm1_g'].shape[0]
    a_w = p['conv_a_b'].shape[1]
    b_w = p['ln_b_g'].shape[1]
    nh = p['dt_bias'].shape[1]
    c_w = p['norm_c_g'].shape[1]
    xw = p['conv_c_b'].shape[1]
    ngrp = 2
    nst = (xw - c_w) // (2 * ngrp)
    d_in = p['w_in'].shape[2] * N_DEV
    main = d_in - nh
    assert main == 2 * a_w + 2 * b_w + c_w + xw and 2 * a_w == 2 * b_w == c_w and xw % c_w == c_w // 2
    me = 4 * lax.axis_index("x") + 2 * lax.axis_index("y") + lax.axis_index("c")

    x2 = x.reshape(T, D)
    tgt = loss_target.reshape(T, D)

    W = []
    for i in range(depth):
        got = _exchange("gather_w", [p['w_in'][i].T.astype(bf16), p['w_out'][i].astype(bf16), p['w_ff1'][i].T.astype(bf16),
                                     p['w_ff2'][i].astype(bf16), p['conv_a_w'][i], p['conv_c_w'][i]], False)
        wt = got[0].reshape(d_in, D)
        ca = jnp.transpose(got[4], (1, 0, 2)).reshape(KA, a_w)
        cc = jnp.transpose(got[5], (1, 0, 2)).reshape(KC, xw)
        W.append(dict(wt_main=wt[:main], wt_dt=_pad_to(wt[main:], 128, 0), wout=got[1].reshape(-1, D),
                      w1t=got[2].reshape(-1, D), w2=got[3].reshape(-1, D),
                      ca=_pad_to(ca, 32, 0), cc=_pad_to(cc, 8, 0)))

    saved = []
    xc = x2
    for i in range(depth):
        w = W[i]
        h1 = _rms_fwd(xc, p['norm1_g'][i])
        (proj,) = _mm("mm_proj", h1, w['wt_main'], "nt", [f32])
        (pdt,) = _mm("mm_pdt", h1, w['wt_dt'], "nt", [f32])
        ya = _conv_fwd("confa_fwd", proj, 0, w['ca'], p['conv_a_b'][i], KA, True, nb, p['ln_a_g'][i], p['ln_a_b'][i])
        yb = _gmlp_fwd(proj, 1, p['ln_b_g'][i], p['ln_b_b'][i], p['w_spatial'][i], p['b_spatial'][i])
        xbc = _conv_fwd("convc_fwd", proj, 2, w['cc'], p['conv_c_b'][i], KC, False, nb)
        dtb, alog, dsk = _row128(p['dt_bias'][i]), _row128(p['a_log'][i]), _row128(p['d_skip'][i])
        yc, sin = _ssd_fwd(xbc, proj, 2, pdt, dtb, alog, dsk, p['norm_c_g'][i], nh, ngrp, nst, nb)
        ycat = jnp.concatenate([ya, yb, yc], axis=1)
        (xm,) = _mm("mm_out", ycat, w['wout'], "nn", [f32], _ep_add, (xc,))
        h2 = _rms_fwd(xm, p['norm2_g'][i])
        f, a = _mm("mm_ff1", h2, w['w1t'], "nt", [f32, bf16], _ep_relu2)
        (xo,) = _mm("mm_ff2", a, w['w2'], "nn", [f32], _ep_add, (xm,))
        saved.append(dict(x_in=xc, h1=h1, proj=proj, pdt=pdt, xbc=xbc, sin=sin, ycat=ycat, xm=xm, h2=h2, f=f, a=a,
                          dtb=dtb, alog=alog, dsk=dsk))
        xc = xo

    lp, dx, dfinal = _loss_head(xc, p['final_g'], tgt)
    loss = lax.psum(lp[0, 0], ("x", "y", "c"))

    gs = {n: [None] * depth for n in _REPL + _CONVW}
    gbig = {n: [None] * depth for n in _BIG}
    for i in reversed(range(depth)):
        w, sv = W[i], saved[i]
        (df,) = _mm("mm_df", dx, w['w2'], "nt", [bf16], _ep_drelu2, (sv['f'],))
        (gw2,) = _mm("mm_gw2", sv['a'], dx, "tn", [bf16])
        (dh2,) = _mm("mm_dh2", df, w['w1t'], "nn", [f32])
        (gw1t,) = _mm("mm_gw1", df, sv['h2'], "tn", [bf16])
        dxm, dg2 = _rms_bwd(sv['xm'], p['norm2_g'][i], dh2, dx)
        (dycat,) = _mm("mm_dycat", dxm, w['wout'], "nt", [f32])
        (gwout,) = _mm("mm_gwout", sv['ycat'], dxm, "tn", [bf16])
        da, dwa, dba, dlag, dlab = _conv_bwd("confa_bwd", sv['proj'], 0, w['ca'], p['conv_a_b'][i], dycat, 0, KA, True, nb,
                                             p['ln_a_g'][i], p['ln_a_b'][i])
        dbb, dlbg, dlbb, dws, dbs = _gmlp_bwd(sv['proj'], 1, p['ln_b_g'][i], p['ln_b_b'][i], p['w_spatial'][i],
                                              p['b_spatial'][i], dycat, 1)
        dxbc, dz, ddt, ddtb, dalog, ddsk, dng = _ssd_bwd(sv['xbc'], sv['proj'], 2, sv['pdt'], sv['dtb'], sv['alog'], sv['dsk'],
                                                         p['norm_c_g'][i], sv['sin'], dycat, 1, nh, ngrp, nst, nb)
        dxbcp, dwc, dbc = _conv_bwd("convc_bwd", sv['proj'], 2, w['cc'], p['conv_c_b'][i], dxbc, 0, KC, False, nb)
        dproj = jnp.concatenate([da, dbb, dz, dxbcp], axis=1)
        (dh_main,) = _mm("mm_dh1", dproj, w['wt_main'], "nn", [f32])
        (dh,) = _mm("mm_dh1dt", ddt, w['wt_dt'], "nn", [f32], _ep_add, (dh_main,))
        (gwt_main,) = _mm("mm_gwin", dproj, sv['h1'], "tn", [bf16])
        (gwt_dt,) = _mm("mm_gwdt", ddt, sv['h1'], "tn", [bf16])
        dx, dg1 = _rms_bwd(sv['x_in'], p['norm1_g'][i], dh, dxm)

        gbig['w_in'][i] = jnp.concatenate([gwt_main, gwt_dt[:nh]], axis=0).reshape(N_DEV, -1, D)
        gbig['w_out'][i] = gwout.reshape(N_DEV, -1, D)
        gbig['w_ff1'][i] = gw1t.reshape(N_DEV, -1, D)
        gbig['w_ff2'][i] = gw2.reshape(N_DEV, -1, D)
        gs['norm1_g'][i] = dg1[0]
        gs['norm2_g'][i] = dg2[0]
        gs['conv_a_w'][i] = dwa[:KA]
        gs['conv_a_b'][i] = dba[0]
        gs['ln_a_g'][i] = dlag[0]
        gs['ln_a_b'][i] = dlab[0]
        gs['ln_b_g'][i] = dlbg[0]
        gs['ln_b_b'][i] = dlbb[0]
        gs['w_spatial'][i] = dws
        gs['b_spatial'][i] = dbs
        gs['conv_c_w'][i] = dwc[:KC]
        gs['conv_c_b'][i] = dbc[0]
        gs['dt_bias'][i] = ddtb[0, :nh]
        gs['a_log'][i] = dalog[0, :nh]
        gs['d_skip'][i] = ddsk[0, :nh]
        gs['norm_c_g'][i] = dng[0]
    grad_x = dx.reshape(nb, S, D)

    out = {}
    for i in range(depth):
        got = _exchange("scatter_g", [gbig[n][i] for n in _BIG], True)
        for n, parts in zip(_BIG, got):
            tr = (lambda t: t.T) if _BIG_T[n] else (lambda t: t)
            res = _adam("adam_" + n, parts, tr(p[n][i]), tr(m[n][i]), tr(v[n][i]))
            for kind, r in zip(("grad", "delta", "new_m", "new_v"), res):
                out.setdefault((kind, n), []).append(tr(r))
    for n in _BIG:
        for kind in ("grad", "delta", "new_m", "new_v"):
            out[(kind, n)] = jnp.stack(out[(kind, n)])

    names1 = _REPL + _CONVW
    g1 = _pack([jnp.stack(gs[n]) for n in names1], dfinal[0])
    (parts1,) = _exchange("gather_g", [g1], False)
    gsum = _sum_parts("sum_small", parts1)
    shapes1 = [(depth,) + gs[n][0].shape for n in names1]
    glist, gfinal = _unpack(gsum, shapes1, D)
    gd = dict(zip(names1, glist))
    for n in _CONVW:
        cw_shard = p[n].shape[2]
        gd[n] = lax.dynamic_slice_in_dim(gd[n], me * cw_shard, cw_shard, axis=2)
    g2 = _pack([gd[n] for n in names1], gfinal)
    w2_, m2_, v2_ = (_pack([q[n] for n in names1], q['final_g']) for q in (p, m, v))
    res = _adam("adam_small", g2[None], w2_, m2_, v2_)
    shapes2 = [p[n].shape for n in names1]
    for kind, r in zip(("grad", "delta", "new_m", "new_v"), res):
        lst, fin = _unpack(r, shapes2, D)
        for n, arr in zip(names1, lst):
            out[(kind, n)] = arr
        out[(kind, 'final_g')] = fin

    flat = [loss, grad_x]
    for kind in ("grad", "delta", "new_m", "new_v"):
        flat += [out[(kind, n)] for n in _NAMES]
    return tuple(flat)


def kernel(x, norm1_g, w_in, conv_a_w, conv_a_b, ln_a_g, ln_a_b, ln_b_g, ln_b_b, w_spatial, b_spatial, conv_c_w, conv_c_b, dt_bias, a_log, d_skip, norm_c_g, w_out, norm2_g, w_ff1, w_ff2, final_g, loss_target, m_norm1_g, m_w_in, m_conv_a_w, m_conv_a_b, m_ln_a_g, m_ln_a_b, m_ln_b_g, m_ln_b_b, m_w_spatial, m_b_spatial, m_conv_c_w, m_conv_c_b, m_dt_bias, m_a_log, m_d_skip, m_norm_c_g, m_w_out, m_norm2_g, m_w_ff1, m_w_ff2, m_final_g, v_norm1_g, v_w_in, v_conv_a_w, v_conv_a_b, v_ln_a_g, v_ln_a_b, v_ln_b_g, v_ln_b_b, v_w_spatial, v_b_spatial, v_conv_c_w, v_conv_c_b, v_dt_bias, v_a_log, v_d_skip, v_norm_c_g, v_w_out, v_norm2_g, v_w_ff1, v_w_ff2, v_final_g):
    p = dict(zip(_NAMES, (norm1_g, w_in, conv_a_w, conv_a_b, ln_a_g, ln_a_b, ln_b_g, ln_b_b, w_spatial, b_spatial, conv_c_w,
                          conv_c_b, dt_bias, a_log, d_skip, norm_c_g, w_out, norm2_g, w_ff1, w_ff2, final_g)))
    m = dict(zip(_NAMES, (m_norm1_g, m_w_in, m_conv_a_w, m_conv_a_b, m_ln_a_g, m_ln_a_b, m_ln_b_g, m_ln_b_b, m_w_spatial,
                          m_b_spatial, m_conv_c_w, m_conv_c_b, m_dt_bias, m_a_log, m_d_skip, m_norm_c_g, m_w_out, m_norm2_g,
                          m_w_ff1, m_w_ff2, m_final_g)))
    v = dict(zip(_NAMES, (v_norm1_g, v_w_in, v_conv_a_w, v_conv_a_b, v_ln_a_g, v_ln_a_b, v_ln_b_g, v_ln_b_b, v_w_spatial,
                          v_b_spatial, v_conv_c_w, v_conv_c_b, v_dt_bias, v_a_log, v_d_skip, v_norm_c_g, v_w_out, v_norm2_g,
                          v_w_ff1, v_w_ff2, v_final_g)))
    return _step(p, m, v, x, loss_target)
```

```python
import functools
import math

import jax
import jax.numpy as jnp
from jax import lax
from jax.experimental import pallas as pl
from jax.experimental.pallas import tpu as pltpu

f32 = jnp.float32
bf16 = jnp.bfloat16
HI = lax.Precision.HIGHEST
EPS = 1e-5
HEAD = 64
CHUNK = 128
KA = 31
KC = 4
N_DEV = 8
VMEM_LIMIT = 56 * 1024 * 1024

ADAM_LR = 0.001
ADAM_B1 = 0.9
ADAM_B2 = 0.999
ADAM_EPS = 1e-08
ADAM_WD = 0.01
ADAM_STEP = 10


def _cparams(sem=None):
    return pltpu.CompilerParams(dimension_semantics=sem, vmem_limit_bytes=VMEM_LIMIT)


def _sds(shape, dtype):
    return jax.ShapeDtypeStruct(shape, dtype)


_DIMS = {"nn": ((1,), (0,)), "nt": ((1,), (1,)), "tn": ((0,), (0,))}


def _tile(n, cap):
    if n <= cap:
        return n
    for d in range(cap - cap % 128, 0, -128):
        if n % d == 0:
            return d
    raise ValueError((n, cap))


def _mm(name, a, b, form, out_dtypes, epilogue=None, extras=(), tm=1024, tn=512, tk=2048):
    if form == "tn":
        K, M = a.shape
    else:
        M, K = a.shape
    N = b.shape[0] if form == "nt" else b.shape[1]
    tm, tn, tk = _tile(M, tm), _tile(N, tn), _tile(K, tk)
    nk = K // tk
    ne, no = len(extras), len(out_dtypes)
    if epilogue is None:
        epilogue = lambda acc: (acc,)

    def body(a_ref, b_ref, *rest):
        extra_refs, out_refs = rest[:ne], rest[ne:ne + no]
        part = lax.dot_general(a_ref[...].astype(bf16), b_ref[...].astype(bf16),
                               (_DIMS[form], ((), ())), preferred_element_type=f32)

        def finish(acc):
            outs = epilogue(acc, *[e[...] for e in extra_refs])
            for o_ref, v in zip(out_refs, outs):
                o_ref[...] = v.astype(o_ref.dtype)

        if nk == 1:
            finish(part)
            return
        acc_ref = rest[ne + no]
        k = pl.program_id(2)

        @pl.when(k == 0)
        def _():
            acc_ref[...] = part

        @pl.when((k > 0) & (k < nk - 1))
        def _():
            acc_ref[...] += part

        @pl.when(k == nk - 1)
        def _():
            finish(acc_ref[...] + part)

    a_spec = pl.BlockSpec((tk, tm), lambda i, j, k: (k, i)) if form == "tn" else pl.BlockSpec((tm, tk), lambda i, j, k: (i, k))
    b_spec = pl.BlockSpec((tn, tk), lambda i, j, k: (j, k)) if form == "nt" else pl.BlockSpec((tk, tn), lambda i, j, k: (k, j))
    mn_spec = pl.BlockSpec((tm, tn), lambda i, j, k: (i, j))
    return pl.pallas_call(
        body, name=name, grid=(M // tm, N // tn, nk),
        in_specs=[a_spec, b_spec] + [mn_spec] * ne,
        out_specs=[mn_spec] * no,
        out_shape=[_sds((M, N), d) for d in out_dtypes],
        scratch_shapes=[pltpu.VMEM((tm, tn), f32)] if nk > 1 else [],
        compiler_params=_cparams(("parallel", "parallel", "arbitrary")),
    )(a, b, *extras)


def _ep_add(acc, r):
    return (acc + r,)


def _ep_relu2(acc):
    r = jnp.maximum(acc, 0.0)
    return acc, r * r


def _ep_drelu2(acc, f):
    return (acc * 2.0 * jnp.maximum(f, 0.0),)


def _rms(x, g):
    return x * lax.rsqrt(jnp.mean(x * x, axis=-1, keepdims=True) + EPS) * g


TT = 512


def _rms_fwd(x, g):
    T, D = x.shape

    def body(x_ref, g_ref, h_ref):
        h_ref[...] = _rms(x_ref[...], g_ref[...]).astype(bf16)

    return pl.pallas_call(
        body, name="rms_fwd", grid=(T // TT,),
        in_specs=[pl.BlockSpec((TT, D), lambda i: (i, 0)), pl.BlockSpec((1, D), lambda i: (0, 0))],
        out_specs=pl.BlockSpec((TT, D), lambda i: (i, 0)),
        out_shape=_sds((T, D), bf16), compiler_params=_cparams(("arbitrary",)),
    )(x, g.reshape(1, D))


def _rms_bwd(x, g, dh, dres):
    T, D = x.shape

    def body(x_ref, g_ref, dh_ref, dres_ref, dx_ref, dg_ref):
        _, vjp = jax.vjp(_rms, x_ref[...], g_ref[...])
        dx, dg = vjp(dh_ref[...])
        dx_ref[...] = dres_ref[...] + dx

        @pl.when(pl.program_id(0) == 0)
        def _():
            dg_ref[...] = jnp.zeros_like(dg_ref)

        dg_ref[0:1, :] += dg

    tile = pl.BlockSpec((TT, D), lambda i: (i, 0))
    return pl.pallas_call(
        body, name="rms_bwd", grid=(T // TT,),
        in_specs=[tile, pl.BlockSpec((1, D), lambda i: (0, 0)), tile, tile],
        out_specs=[tile, pl.BlockSpec((8, D), lambda i: (0, 0))],
        out_shape=[_sds((T, D), f32), _sds((8, D), f32)], compiler_params=_cparams(("arbitrary",)),
    )(x, g.reshape(1, D), dh, dres)


def _loss_head(x, g, tgt):
    T, D = x.shape

    def f(xv, gv, tv):
        e = _rms(xv, gv) - tv
        return 0.5 * jnp.sum(jnp.sum(e * e, axis=-1, keepdims=True) * (1.0 / D), axis=0, keepdims=True)

    def body(x_ref, g_ref, t_ref, loss_ref, dx_ref, dg_ref):
        tv = t_ref[...]
        l, vjp = jax.vjp(lambda xv, gv: f(xv, gv, tv), x_ref[...], g_ref[...])
        dx, dg = vjp(jnp.ones((1, 1), f32))
        dx_ref[...] = dx

        @pl.when(pl.program_id(0) == 0)
        def _():
            dg_ref[...] = jnp.zeros_like(dg_ref)
            loss_ref[...] = jnp.zeros_like(loss_ref)

        dg_ref[0:1, :] += dg
        loss_ref[...] += jnp.broadcast_to(l, loss_ref.shape)

    tile = pl.BlockSpec((TT, D), lambda i: (i, 0))
    return pl.pallas_call(
        body, name="loss_head", grid=(T // TT,),
        in_specs=[tile, pl.BlockSpec((1, D), lambda i: (0, 0)), tile],
        out_specs=[pl.BlockSpec((8, 128), lambda i: (0, 0)), tile, pl.BlockSpec((8, D), lambda i: (0, 0))],
        out_shape=[_sds((8, 128), f32), _sds((T, D), f32), _sds((8, D), f32)],
        compiler_params=_cparams(("arbitrary",)),
    )(x, g.reshape(1, D), tgt)


TB = 256


def _glu(a_val, a_gate):
    return a_val * jax.nn.sigmoid(a_gate)


def _ln_silu(v, g, b):
    mu = jnp.mean(v, axis=-1, keepdims=True)
    vc = v - mu
    var = jnp.mean(vc * vc, axis=-1, keepdims=True)
    return jax.nn.silu(vc * lax.rsqrt(var + EPS) * g + b)


def _conv_geom(kw):
    halo = 32 if kw > 9 else 8
    return halo, halo - (kw - 1)


def _conv_taps(hp_ref, w_ref, b_ref, acc_ref, kw, off, width):
    for c in range(width // 128):
        ls = pl.ds(c * 128, 128)
        acc = jnp.broadcast_to(b_ref[:, ls], (TB, 128))
        for k in range(kw):
            acc = acc + w_ref[k:k + 1, ls] * hp_ref[pl.ds(off + k, TB), ls]
        acc_ref[:, ls] = acc


def _conv_fwd(name, src, col_block, w, b, kw, conformer, n_seq, ln_g=None, ln_b=None):
    T = src.shape[0]
    cout = w.shape[1]
    cin = 2 * cout if conformer else cout
    halo, off = _conv_geom(kw)
    nblk = T // n_seq // TB
    hb = TB // halo

    def body(cur_ref, halo_ref, w_ref, b_ref, *rest):
        if conformer:
            g_ref, lb_ref, out_ref, hp_ref, acc_ref = rest
        else:
            out_ref, hp_ref, acc_ref = rest
        i = pl.program_id(1)
        first = (i == 0)
        if conformer:
            hp_ref[pl.ds(halo, TB), :] = _glu(cur_ref[:, 0:cout], cur_ref[:, cout:cin])
            hh = _glu(halo_ref[:, 0:cout], halo_ref[:, cout:cin])
        else:
            hp_ref[pl.ds(halo, TB), :] = cur_ref[...]
            hh = halo_ref[...]
        hp_ref[pl.ds(0, halo), :] = jnp.where(first, 0.0, hh)
        _conv_taps(hp_ref, w_ref, b_ref, acc_ref, kw, off, cout)
        if conformer:
            for h in range(cout // HEAD):
                ls = pl.ds(h * HEAD, HEAD)
                out_ref[:, ls] = _ln_silu(acc_ref[:, ls], g_ref[:, ls], lb_ref[:, ls]).astype(out_ref.dtype)
        else:
            out_ref[...] = jax.nn.silu(acc_ref[...]).astype(out_ref.dtype)

    row = pl.BlockSpec((1, cout), lambda s, i: (0, 0))
    in_specs = [pl.BlockSpec((TB, cin), lambda s, i: (s * nblk + i, col_block)),
                pl.BlockSpec((halo, cin), lambda s, i: (jnp.maximum((s * nblk + i) * hb - 1, 0), col_block)),
                pl.BlockSpec((w.shape[0], cout), lambda s, i: (0, 0)), row]
    args = [src, src, w, b.reshape(1, cout)]
    if conformer:
        in_specs += [row, row]
        args += [ln_g.reshape(1, cout), ln_b.reshape(1, cout)]
    out_dtype = bf16 if conformer else f32
    return pl.pallas_call(
        body, name=name, grid=(n_seq, nblk), in_specs=in_specs,
        out_specs=pl.BlockSpec((TB, cout), lambda s, i: (s * nblk + i, 0)),
        out_shape=_sds((T, cout), out_dtype),
        scratch_shapes=[pltpu.VMEM((halo + TB, cout), f32), pltpu.VMEM((TB, cout), f32)],
        compiler_params=_cparams(("arbitrary", "arbitrary")),
    )(*args)


def _conv_bwd(name, src, col_block, w, b, dy, dy_col_block, kw, conformer, n_seq, ln_g=None, ln_b=None):
    T = src.shape[0]
    cout = w.shape[1]
    wrows = w.shape[0]
    cin = 2 * cout if conformer else cout
    halo, off = _conv_geom(kw)
    nblk = T // n_seq // TB
    hb = TB // halo

    def body(cur_ref, halo_ref, w_ref, b_ref, dy_ref, *rest):
        if conformer:
            g_ref, lb_ref, dsrc_ref, dw_ref, db_ref, dg_ref, dlb_ref, hp_ref, acc_ref, dz_ref, dhp_ref, carry_ref = rest
        else:
            dsrc_ref, dw_ref, db_ref, hp_ref, acc_ref, dz_ref, dhp_ref, carry_ref = rest
        s, ii = pl.program_id(0), pl.program_id(1)
        i = nblk - 1 - ii
        first = (i == 0)

        @pl.when((s == 0) & (ii == 0))
        def _():
            dw_ref[...] = jnp.zeros_like(dw_ref)
            db_ref[...] = jnp.zeros_like(db_ref)
            if conformer:
                dg_ref[...] = jnp.zeros_like(dg_ref)
                dlb_ref[...] = jnp.zeros_like(dlb_ref)

        @pl.when(ii == 0)
        def _():
            carry_ref[...] = jnp.zeros_like(carry_ref)
            dz_ref[pl.ds(0, halo), :] = jnp.zeros((halo, cout), f32)
            dz_ref[pl.ds(halo + TB, halo), :] = jnp.zeros((halo, cout), f32)

        if conformer:
            hp_ref[pl.ds(halo, TB), :] = _glu(cur_ref[:, 0:cout], cur_ref[:, cout:cin])
            hh = _glu(halo_ref[:, 0:cout], halo_ref[:, cout:cin])
        else:
            hp_ref[pl.ds(halo, TB), :] = cur_ref[...]
            hh = halo_ref[...]
        hp_ref[pl.ds(0, halo), :] = jnp.where(first, 0.0, hh)
        _conv_taps(hp_ref, w_ref, b_ref, acc_ref, kw, off, cout)

        if conformer:
            for h in range(cout // HEAD):
                ls = pl.ds(h * HEAD, HEAD)
                _, vjp = jax.vjp(_ln_silu, acc_ref[:, ls], g_ref[:, ls], lb_ref[:, ls])
                da, dg, dlb = vjp(dy_ref[:, ls].astype(f32))
                dz_ref[pl.ds(halo, TB), ls] = da
                dg_ref[0:1, ls] += dg
                dlb_ref[0:1, ls] += dlb
        else:
            _, vjp = jax.vjp(jax.nn.silu, acc_ref[...])
            dz_ref[pl.ds(halo, TB), :] = vjp(dy_ref[...].astype(f32))[0]

        for c in range(cout // 128):
            ls = pl.ds(c * 128, 128)
            dacc = dz_ref[pl.ds(halo, TB), ls]
            db_ref[0:1, ls] += jnp.sum(dacc, axis=0, keepdims=True)
            dhp = jnp.zeros((halo + TB, 128), f32)
            for k in range(kw):
                dw_ref[k:k + 1, ls] += jnp.sum(dacc * hp_ref[pl.ds(off + k, TB), ls], axis=0, keepdims=True)
                dhp = dhp + w_ref[k:k + 1, ls] * dz_ref[pl.ds(kw - 1 - k, halo + TB), ls]
            dhp_ref[:, ls] = dhp
        dhp_ref[pl.ds(TB, halo), :] += carry_ref[...]
        carry_ref[...] = dhp_ref[pl.ds(0, halo), :]
        dcur = dhp_ref[pl.ds(halo, TB), :]
        if conformer:
            _, vjp = jax.vjp(_glu, cur_ref[:, 0:cout], cur_ref[:, cout:cin])
            dval, dgate = vjp(dcur)
            dsrc_ref[:, 0:cout] = dval.astype(dsrc_ref.dtype)
            dsrc_ref[:, cout:cin] = dgate.astype(dsrc_ref.dtype)
        else:
            dsrc_ref[...] = dcur.astype(dsrc_ref.dtype)

    def blk(s, ii):
        return s * nblk + (nblk - 1 - ii)

    row = pl.BlockSpec((1, cout), lambda s, ii: (0, 0))
    acc8 = pl.BlockSpec((8, cout), lambda s, ii: (0, 0))
    in_specs = [pl.BlockSpec((TB, cin), lambda s, ii: (blk(s, ii), col_block)),
                pl.BlockSpec((halo, cin), lambda s, ii: (jnp.maximum(blk(s, ii) * hb - 1, 0), col_block)),
                pl.BlockSpec((wrows, cout), lambda s, ii: (0, 0)), row,
                pl.BlockSpec((TB, cout), lambda s, ii: (blk(s, ii), dy_col_block))]
    args = [src, src, w, b.reshape(1, cout), dy]
    out_specs = [pl.BlockSpec((TB, cin), lambda s, ii: (blk(s, ii), 0)),
                 pl.BlockSpec((wrows, cout), lambda s, ii: (0, 0)), acc8]
    out_shape = [_sds((T, cin), bf16), _sds((wrows, cout), f32), _sds((8, cout), f32)]
    if conformer:
        in_specs += [row, row]
        args += [ln_g.reshape(1, cout), ln_b.reshape(1, cout)]
        out_specs += [acc8, acc8]
        out_shape += [_sds((8, cout), f32), _sds((8, cout), f32)]
    return pl.pallas_call(
        body, name=name, grid=(n_seq, nblk), in_specs=in_specs, out_specs=out_specs, out_shape=out_shape,
        scratch_shapes=[pltpu.VMEM((halo + TB, cout), f32), pltpu.VMEM((TB, cout), f32),
                        pltpu.VMEM((halo + TB + halo, cout), f32), pltpu.VMEM((halo + TB, cout), f32),
                        pltpu.VMEM((halo, cout), f32)],
        compiler_params=_cparams(("arbitrary", "arbitrary")),
    )(*args)


def _gelu(x):
    return 0.5 * x * (1.0 + lax.erf(x * (1.0 / math.sqrt(2.0))))


def _tril_mask(n):
    r = lax.broadcasted_iota(jnp.int32, (n, n), 0)
    c = lax.broadcasted_iota(jnp.int32, (n, n), 1)
    return r >= c


def _row_select(rows, h, width):
    r = lax.broadcasted_iota(jnp.int32, (rows, width), 0)
    return (r == h).astype(f32)


def _gmlp_head(h, bu, bv, g, b, w_h, bs):
    u = _gelu(bu)
    v = _gelu(bv)
    mu = jnp.mean(v, axis=-1, keepdims=True)
    vc = v - mu
    var = jnp.mean(vc * vc, axis=-1, keepdims=True)
    vn = vc * lax.rsqrt(var + EPS) * g + b
    wm = jnp.where(_tril_mask(CHUNK), w_h, 0.0)
    mix = jnp.dot(wm.astype(bf16), vn.astype(bf16), preferred_element_type=f32)
    bias = lax.dot_general(bs, _row_select(bs.shape[0], h, HEAD), (((0,), (0,)), ((), ())),
                           precision=HI, preferred_element_type=f32)
    return u * (mix + bias)


def _gmlp_fwd(proj, col_block, ln_g, ln_b, w_s, b_s):
    T = proj.shape[0]
    nh = w_s.shape[0]
    width = nh * HEAD

    def body(p_ref, g_ref, b_ref, w_ref, bs_ref, out_ref):
        bs = bs_ref[...]
        for h in range(nh):
            ls = pl.ds(h * HEAD, HEAD)
            lv = pl.ds(width + h * HEAD, HEAD)
            out_ref[:, ls] = _gmlp_head(h, p_ref[:, ls], p_ref[:, lv], g_ref[:, ls], b_ref[:, ls], w_ref[h], bs).astype(out_ref.dtype)

    row = pl.BlockSpec((1, width), lambda i: (0, 0))
    return pl.pallas_call(
        body, name="gmlp_fwd", grid=(T // CHUNK,),
        in_specs=[pl.BlockSpec((CHUNK, 2 * width), lambda i: (i, col_block)), row, row,
                  pl.BlockSpec((nh, CHUNK, CHUNK), lambda i: (0, 0, 0)), pl.BlockSpec((nh, CHUNK), lambda i: (0, 0))],
        out_specs=pl.BlockSpec((CHUNK, width), lambda i: (i, 0)),
        out_shape=_sds((T, width), bf16), compiler_params=_cparams(("arbitrary",)),
    )(proj, ln_g.reshape(1, width), ln_b.reshape(1, width), w_s, b_s)


def _gmlp_bwd(proj, col_block, ln_g, ln_b, w_s, b_s, dy, dy_col_block):
    T = proj.shape[0]
    nh = w_s.shape[0]
    width = nh * HEAD

    def body(p_ref, g_ref, b_ref, w_ref, bs_ref, dy_ref, dp_ref, dg_ref, db_ref, dw_ref, dbs_ref):
        @pl.when(pl.program_id(0) == 0)
        def _():
            dg_ref[...] = jnp.zeros_like(dg_ref)
            db_ref[...] = jnp.zeros_like(db_ref)
            dw_ref[...] = jnp.zeros_like(dw_ref)
            dbs_ref[...] = jnp.zeros_like(dbs_ref)

        bs = bs_ref[...]
        for h in range(nh):
            ls = pl.ds(h * HEAD, HEAD)
            lv = pl.ds(width + h * HEAD, HEAD)
            _, vjp = jax.vjp(functools.partial(_gmlp_head, h), p_ref[:, ls], p_ref[:, lv], g_ref[:, ls], b_ref[:, ls], w_ref[h], bs)
            dbu, dbv, dg, db, dw, dbs = vjp(dy_ref[:, ls].astype(f32))
            dp_ref[:, ls] = dbu.astype(dp_ref.dtype)
            dp_ref[:, lv] = dbv.astype(dp_ref.dtype)
            dg_ref[0:1, ls] += dg
            db_ref[0:1, ls] += db
            dw_ref[h] += dw
            dbs_ref[...] += dbs

    row = pl.BlockSpec((1, width), lambda i: (0, 0))
    acc8 = pl.BlockSpec((8, width), lambda i: (0, 0))
    wspec = pl.BlockSpec((nh, CHUNK, CHUNK), lambda i: (0, 0, 0))
    bspec = pl.BlockSpec((nh, CHUNK), lambda i: (0, 0))
    return pl.pallas_call(
        body, name="gmlp_bwd", grid=(T // CHUNK,),
        in_specs=[pl.BlockSpec((CHUNK, 2 * width), lambda i: (i, col_block)), row, row, wspec, bspec,
                  pl.BlockSpec((CHUNK, width), lambda i: (i, dy_col_block))],
        out_specs=[pl.BlockSpec((CHUNK, 2 * width), lambda i: (i, 0)), acc8, acc8, wspec, bspec],
        out_shape=[_sds((T, 2 * width), bf16), _sds((8, width), f32), _sds((8, width), f32),
                   _sds((nh, CHUNK, CHUNK), f32), _sds((nh, CHUNK), f32)],
        compiler_params=_cparams(("arbitrary",)),
    )(proj, ln_g.reshape(1, width), ln_b.reshape(1, width), w_s, b_s, dy)


def _sel_col(x, h):
    lane = lax.broadcasted_iota(jnp.int32, x.shape, 1)
    return jnp.sum(jnp.where(lane == h, x, 0.0), axis=1, keepdims=True)


def _sel_row(x, h):
    sub = lax.broadcasted_iota(jnp.int32, x.shape, 0)
    return jnp.sum(jnp.where(sub == h, x, 0.0), axis=0, keepdims=True)


PAIR = 2 * HEAD


def _ssd_chunk(nh, ngrp, xs_l, z_l, b_l, c_l, dtraw, dtb, alog, dskip, ng_l, prev_l):
    hg = nh // ngrp
    tril = _tril_mask(CHUNK)
    tl = tril.astype(f32)
    lo = lax.broadcasted_iota(jnp.int32, (CHUNK, PAIR), 1) < HEAD
    lo_row = lo[0:1, :]
    dt = jax.nn.softplus(dtraw + dtb)
    a = dt * (-jnp.exp(alog))
    cs = jnp.dot(tl, a, precision=HI, preferred_element_type=f32)
    cst = lax.dot_general(a, tl, (((0,), (1,)), ((), ())), precision=HI, preferred_element_type=f32)
    cb_l = [lax.dot_general(c_l[g].astype(bf16), b_l[g].astype(bf16), (((1,), (1,)), ((), ())),
                            preferred_element_type=f32) for g in range(ngrp)]
    yz_l, new_prev = [], []
    for q in range(nh // 2):
        g = (2 * q) // hg
        cols = []
        for h in (2 * q, 2 * q + 1):
            cs_h = _sel_col(cs, h)
            tot = _sel_row(cs_h, CHUNK - 1)
            seg = jnp.where(tril, cs_h - _sel_row(cst, h), 0.0)
            lmat = jnp.where(tril, jnp.exp(seg), 0.0)
            cols.append((_sel_col(dt, h), cs_h, tot, lmat, _sel_col(dskip, h)))
        (dt_a, cs_a, tot_a, l_a, dsk_a), (dt_b, cs_b, tot_b, l_b, dsk_b) = cols
        xs = xs_l[q]
        x = xs * jnp.where(lo, dt_a, dt_b)
        xb = x.astype(bf16)
        ydiag = jnp.where(lo, jnp.dot((cb_l[g] * l_a).astype(bf16), xb, preferred_element_type=f32),
                          jnp.dot((cb_l[g] * l_b).astype(bf16), xb, preferred_element_type=f32))
        yoff = (jnp.dot(c_l[g].astype(bf16), prev_l[q].astype(bf16), preferred_element_type=f32)
                * jnp.where(lo, jnp.exp(cs_a), jnp.exp(cs_b)))
        xdec = x * jnp.where(lo, jnp.exp(tot_a - cs_a), jnp.exp(tot_b - cs_b))
        st = lax.dot_general(b_l[g].astype(bf16), xdec.astype(bf16), (((0,), (0,)), ((), ())),
                             preferred_element_type=f32)
        new_prev.append(prev_l[q] * jnp.where(lo_row, jnp.exp(tot_a), jnp.exp(tot_b)) + st)
        y = ydiag + yoff + jnp.where(lo_row, dsk_a, dsk_b) * xs
        yz_l.append(y * jax.nn.silu(z_l[q]))
    out = [None] * (nh // 2)
    qg = hg // 2
    for g in range(ngrp):
        ssq = sum(jnp.sum(yz_l[q] * yz_l[q], axis=-1, keepdims=True) for q in range(g * qg, (g + 1) * qg))
        r = lax.rsqrt(ssq * (1.0 / (hg * HEAD)) + EPS)
        for q in range(g * qg, (g + 1) * qg):
            out[q] = yz_l[q] * r * ng_l[q]
    return out, new_prev


def _ssd_read(nh, ngrp, nst, xbc_ref, z_ref, ng_ref, st_ref):
    cw = nh * HEAD
    xs_l = [xbc_ref[:, pl.ds(q * PAIR, PAIR)] for q in range(nh // 2)]
    b_l = [xbc_ref[:, pl.ds(cw + g * nst, nst)] for g in range(ngrp)]
    c_l = [xbc_ref[:, pl.ds(cw + ngrp * nst + g * nst, nst)] for g in range(ngrp)]
    z_l = [z_ref[:, pl.ds(q * PAIR, PAIR)] for q in range(nh // 2)]
    ng_l = [ng_ref[:, pl.ds(q * PAIR, PAIR)] for q in range(nh // 2)]
    prev_l = [st_ref[:, pl.ds(q * PAIR, PAIR)] for q in range(nh // 2)]
    return xs_l, z_l, b_l, c_l, ng_l, prev_l


def _ssd_fwd(xbc, proj, z_col_block, pdt, dtb, alog, dskip, ng, nh, ngrp, nst, n_seq):
    T = xbc.shape[0]
    cw = nh * HEAD
    nchunk = T // n_seq // CHUNK
    assert nst == CHUNK

    def body(xbc_ref, z_ref, dt_ref, dtb_ref, alog_ref, dskip_ref, ng_ref, y_ref, sin_ref, st_ref):
        @pl.when(pl.program_id(1) == 0)
        def _():
            st_ref[...] = jnp.zeros_like(st_ref)

        sin_ref[...] = st_ref[...]
        xs_l, z_l, b_l, c_l, ng_l, prev_l = _ssd_read(nh, ngrp, nst, xbc_ref, z_ref, ng_ref, st_ref)
        y_l, new_prev = _ssd_chunk(nh, ngrp, xs_l, z_l, b_l, c_l, dt_ref[...], dtb_ref[...], alog_ref[...],
                                   dskip_ref[...], ng_l, prev_l)
        for q in range(nh // 2):
            ls = pl.ds(q * PAIR, PAIR)
            y_ref[:, ls] = y_l[q].astype(y_ref.dtype)
            st_ref[:, ls] = new_prev[q]

    def blk(s, c):
        return s * nchunk + c

    prow = pl.BlockSpec((1, 128), lambda s, c: (0, 0))
    return pl.pallas_call(
        body, name="ssd_fwd", grid=(n_seq, nchunk),
        in_specs=[pl.BlockSpec((CHUNK, xbc.shape[1]), lambda s, c: (blk(s, c), 0)),
                  pl.BlockSpec((CHUNK, cw), lambda s, c: (blk(s, c), z_col_block)),
                  pl.BlockSpec((CHUNK, 128), lambda s, c: (blk(s, c), 0)),
                  prow, prow, prow, pl.BlockSpec((1, cw), lambda s, c: (0, 0))],
        out_specs=[pl.BlockSpec((CHUNK, cw), lambda s, c: (blk(s, c), 0)),
                   pl.BlockSpec((nst, cw), lambda s, c: (blk(s, c), 0))],
        out_shape=[_sds((T, cw), bf16), _sds((T, cw), f32)],
        scratch_shapes=[pltpu.VMEM((nst, cw), f32)],
        compiler_params=_cparams(("arbitrary", "arbitrary")),
    )(xbc, proj, pdt, dtb, alog, dskip, ng.reshape(1, cw))


def _ssd_bwd(xbc, proj, z_col_block, pdt, dtb, alog, dskip, ng, sin, dy, dy_col_block, nh, ngrp, nst, n_seq):
    T, xw = xbc.shape
    cw = nh * HEAD
    nchunk = T // n_seq // CHUNK

    def body(xbc_ref, z_ref, dt_ref, dtb_ref, alog_ref, dskip_ref, ng_ref, sin_ref, dy_ref,
             dxbc_ref, dz_ref, ddt_ref, ddtb_ref, dalog_ref, ddskip_ref, dng_ref, dst_ref):
        s, cc = pl.program_id(0), pl.program_id(1)

        @pl.when((s == 0) & (cc == 0))
        def _():
            ddtb_ref[...] = jnp.zeros_like(ddtb_ref)
            dalog_ref[...] = jnp.zeros_like(dalog_ref)
            ddskip_ref[...] = jnp.zeros_like(ddskip_ref)
            dng_ref[...] = jnp.zeros_like(dng_ref)

        @pl.when(cc == 0)
        def _():
            dst_ref[...] = jnp.zeros_like(dst_ref)

        xs_l, z_l, b_l, c_l, ng_l, prev_l = _ssd_read(nh, ngrp, nst, xbc_ref, z_ref, ng_ref, sin_ref)
        _, vjp = jax.vjp(functools.partial(_ssd_chunk, nh, ngrp), xs_l, z_l, b_l, c_l, dt_ref[...], dtb_ref[...],
                         alog_ref[...], dskip_ref[...], ng_l, prev_l)
        dy_l = [dy_ref[:, pl.ds(q * PAIR, PAIR)].astype(f32) for q in range(nh // 2)]
        dst_l = [dst_ref[:, pl.ds(q * PAIR, PAIR)] for q in range(nh // 2)]
        dxs_l, dz_l, db_l, dc_l, ddt, ddtb, dalog, ddskip, dng_l, dprev_l = vjp((dy_l, dst_l))
        for q in range(nh // 2):
            ls = pl.ds(q * PAIR, PAIR)
            dxbc_ref[:, ls] = dxs_l[q]
            dz_ref[:, ls] = dz_l[q].astype(dz_ref.dtype)
            dng_ref[0:1, ls] += dng_l[q]
            dst_ref[:, ls] = dprev_l[q]
        for g in range(ngrp):
            dxbc_ref[:, pl.ds(cw + g * nst, nst)] = db_l[g]
            dxbc_ref[:, pl.ds(cw + ngrp * nst + g * nst, nst)] = dc_l[g]
        ddt_ref[...] = ddt.astype(ddt_ref.dtype)
        ddtb_ref[0:1, :] += ddtb
        dalog_ref[0:1, :] += dalog
        ddskip_ref[0:1, :] += ddskip

    def blk(s, cc):
        return s * nchunk + (nchunk - 1 - cc)

    prow = pl.BlockSpec((1, 128), lambda s, c: (0, 0))
    pacc = pl.BlockSpec((8, 128), lambda s, c: (0, 0))
    return pl.pallas_call(
        body, name="ssd_bwd", grid=(n_seq, nchunk),
        in_specs=[pl.BlockSpec((CHUNK, xw), lambda s, c: (blk(s, c), 0)),
                  pl.BlockSpec((CHUNK, cw), lambda s, c: (blk(s, c), z_col_block)),
                  pl.BlockSpec((CHUNK, 128), lambda s, c: (blk(s, c), 0)),
                  prow, prow, prow, pl.BlockSpec((1, cw), lambda s, c: (0, 0)),
                  pl.BlockSpec((nst, cw), lambda s, c: (blk(s, c), 0)),
                  pl.BlockSpec((CHUNK, cw), lambda s, c: (blk(s, c), dy_col_block))],
        out_specs=[pl.BlockSpec((CHUNK, xw), lambda s, c: (blk(s, c), 0)),
                   pl.BlockSpec((CHUNK, cw), lambda s, c: (blk(s, c), 0)),
                   pl.BlockSpec((CHUNK, 128), lambda s, c: (blk(s, c), 0)),
                   pacc, pacc, pacc, pl.BlockSpec((8, cw), lambda s, c: (0, 0))],
        out_shape=[_sds((T, xw), f32), _sds((T, cw), bf16), _sds((T, 128), bf16),
                   _sds((8, 128), f32), _sds((8, 128), f32), _sds((8, 128), f32), _sds((8, cw), f32)],
        scratch_shapes=[pltpu.VMEM((nst, cw), f32)],
        compiler_params=_cparams(("arbitrary", "arbitrary")),
    )(xbc, proj, pdt, dtb, alog, dskip, ng.reshape(1, cw), sin, dy)


_HBM = pl.BlockSpec(memory_space=pltpu.HBM)


def _exchange(name, arrs, scatter):
    n = len(arrs)
    npeer = N_DEV - 1

    def body(*refs):
        in_refs, out_refs = refs[:n], refs[n:2 * n]
        send_sems, recv_sems, local_sems = refs[2 * n:]
        x, y, c = lax.axis_index("x"), lax.axis_index("y"), lax.axis_index("c")
        me = 4 * x + 2 * y + c
        copies = []
        for i in range(n):
            src = in_refs[i].at[me] if scatter else in_refs[i]
            loc = pltpu.make_async_copy(src, out_refs[i].at[me], local_sems.at[i])
            loc.start()
            copies.append(loc)
            for k in range(1, N_DEV):
                px = 1 - x if k & 4 else x
                py = 1 - y if k & 2 else y
                pc = 1 - c if k & 1 else c
                src = in_refs[i].at[4 * px + 2 * py + pc] if scatter else in_refs[i]
                cp = pltpu.make_async_remote_copy(
                    src_ref=src, dst_ref=out_refs[i].at[me],
                    send_sem=send_sems.at[i * npeer + k - 1], recv_sem=recv_sems.at[i * npeer + k - 1],
                    device_id=(px, py, pc), device_id_type=pl.DeviceIdType.MESH)
                cp.start()
                copies.append(cp)
        for cp in copies:
            cp.wait()

    out_shape = [_sds(a.shape if scatter else (N_DEV,) + a.shape, a.dtype) for a in arrs]
    return pl.pallas_call(
        body, name=name, in_specs=[_HBM] * n, out_specs=[_HBM] * n, out_shape=out_shape,
        scratch_shapes=[pltpu.SemaphoreType.DMA((n * npeer,)), pltpu.SemaphoreType.DMA((n * npeer,)),
                        pltpu.SemaphoreType.DMA((n,))],
    )(*arrs)


def _adam_tiles(R, C):
    if R % 256 == 0:
        return (256, C), (R // 256, 1)
    assert C % 128 == 0
    return (R, 128), (1, C // 128)


def _adam(name, parts, w, m, v):
    P, R, C = parts.shape
    (tr, tc), (gr, gc) = _adam_tiles(R, C)
    c1 = 1.0 / (1.0 - ADAM_B1 ** ADAM_STEP)
    c2 = 1.0 / (1.0 - ADAM_B2 ** ADAM_STEP)

    def body(p_ref, w_ref, m_ref, v_ref, g_ref, d_ref, nm_ref, nv_ref):
        g = p_ref[0].astype(f32)
        for s in range(1, P):
            g = g + p_ref[s].astype(f32)
        nm = ADAM_B1 * m_ref[...] + (1.0 - ADAM_B1) * g
        nv = ADAM_B2 * v_ref[...] + (1.0 - ADAM_B2) * (g * g)
        g_ref[...] = g
        nm_ref[...] = nm
        nv_ref[...] = nv
        d_ref[...] = -ADAM_LR * ((nm * c1) / (jnp.sqrt(nv * c2) + ADAM_EPS) + ADAM_WD * w_ref[...])

    tile = pl.BlockSpec((tr, tc), lambda i, j: (i, j))
    return pl.pallas_call(
        body, name=name, grid=(gr, gc),
        in_specs=[pl.BlockSpec((P, tr, tc), lambda i, j: (0, i, j)), tile, tile, tile],
        out_specs=[tile] * 4, out_shape=[_sds((R, C), f32)] * 4,
        compiler_params=_cparams(("arbitrary", "arbitrary")),
    )(parts, w, m, v)


def _sum_parts(name, parts):
    P, R, C = parts.shape
    tr = 256 if R % 256 == 0 else R

    def body(p_ref, o_ref):
        g = p_ref[0]
        for s in range(1, P):
            g = g + p_ref[s]
        o_ref[...] = g

    return pl.pallas_call(
        body, name=name, grid=(R // tr,),
        in_specs=[pl.BlockSpec((P, tr, C), lambda i: (0, i, 0))], out_specs=pl.BlockSpec((tr, C), lambda i: (i, 0)),
        out_shape=_sds((R, C), f32), compiler_params=_cparams(("arbitrary",)),
    )(parts)


def _pad_to(a, n, axis):
    if a.shape[axis] == n:
        return a
    cfg = [(0, 0)] * a.ndim
    cfg[axis] = (0, n - a.shape[axis])
    return jnp.pad(a, cfg)


def _pack(layered, final):
    depth = layered[0].shape[0]
    cols = []
    for a in layered:
        a = a.reshape(depth, -1)
        cols.append(_pad_to(a, -(-a.shape[1] // 128) * 128, 1))
    body = jnp.concatenate(cols, axis=1).reshape(-1, 128)
    rows = jnp.concatenate([body, final.reshape(-1, 128)], axis=0)
    return _pad_to(rows, -(-rows.shape[0] // 256) * 256, 0)


def _unpack(slab, shapes, final_n):
    depth = shapes[0][0]
    widths = [-(-math.prod(s[1:]) // 128) * 128 for s in shapes]
    nl = sum(widths)
    body = slab[:depth * nl // 128].reshape(depth, nl)
    out, o = [], 0
    for s, wd in zip(shapes, widths):
        out.append(body[:, o:o + math.prod(s[1:])].reshape(s))
        o += wd
    r0 = depth * nl // 128
    final = slab[r0:r0 + final_n // 128].reshape(final_n)
    return out, final


_NAMES = ['norm1_g', 'w_in', 'conv_a_w', 'conv_a_b', 'ln_a_g', 'ln_a_b', 'ln_b_g', 'ln_b_b', 'w_spatial', 'b_spatial',
          'conv_c_w', 'conv_c_b', 'dt_bias', 'a_log', 'd_skip', 'norm_c_g', 'w_out', 'norm2_g', 'w_ff1', 'w_ff2', 'final_g']
_REPL = ['norm1_g', 'conv_a_b', 'ln_a_g', 'ln_a_b', 'ln_b_g', 'ln_b_b', 'w_spatial', 'b_spatial', 'conv_c_b',
         'dt_bias', 'a_log', 'd_skip', 'norm_c_g', 'norm2_g']
_CONVW = ['conv_a_w', 'conv_c_w']
_BIG = ['w_in', 'w_out', 'w_ff1', 'w_ff2']
_BIG_T = {'w_in': True, 'w_out': False, 'w_ff1': True, 'w_ff2': False}


def _row128(v):
    return _pad_to(v.reshape(1, -1), 128, 1)


def _step(p, m, v, x, loss_target):
    nb, S, D = x.shape
    T = nb * S
    depth = p['norm1_g'].shape[0]
    a_w = p['conv_a_b'].shape[1]
    b_w = p['ln_b_g'].shape[1]
    nh = p['dt_bias'].shape[1]
    c_w = p['norm_c_g'].shape[1]
    xw = p['conv_c_b'].shape[1]
    ngrp = 2
    nst = (xw - c_w) // (2 * ngrp)
    d_in = p['w_in'].shape[2] * N_DEV
    main = d_in - nh
    assert main == 2 * a_w + 2 * b_w + c_w + xw and 2 * a_w == 2 * b_w == c_w and xw % c_w == c_w // 2
    me = 4 * lax.axis_index("x") + 2 * lax.axis_index("y") + lax.axis_index("c")

    x2 = x.reshape(T, D)
    tgt = loss_target.reshape(T, D)

    W = []
    for i in range(depth):
        got = _exchange("gather_w", [p['w_in'][i].T.astype(bf16), p['w_out'][i].astype(bf16), p['w_ff1'][i].T.astype(bf16),
                                     p['w_ff2'][i].astype(bf16), p['conv_a_w'][i], p['conv_c_w'][i]], False)
        wt = got[0].reshape(d_in, D)
        ca = jnp.transpose(got[4], (1, 0, 2)).reshape(KA, a_w)
        cc = jnp.transpose(got[5], (1, 0, 2)).reshape(KC, xw)
        W.append(dict(wt_main=wt[:main], wt_dt=_pad_to(wt[main:], 128, 0), wout=got[1].reshape(-1, D),
                      w1t=got[2].reshape(-1, D), w2=got[3].reshape(-1, D),
                      ca=_pad_to(ca, 32, 0), cc=_pad_to(cc, 8, 0)))

    saved = []
    xc = x2
    for i in range(depth):
        w = W[i]
        h1 = _rms_fwd(xc, p['norm1_g'][i])
        (proj,) = _mm("mm_proj", h1, w['wt_main'], "nt", [f32])
        (pdt,) = _mm("mm_pdt", h1, w['wt_dt'], "nt", [f32])
        ya = _conv_fwd("confa_fwd", proj, 0, w['ca'], p['conv_a_b'][i], KA, True, nb, p['ln_a_g'][i], p['ln_a_b'][i])
        yb = _gmlp_fwd(proj, 1, p['ln_b_g'][i], p['ln_b_b'][i], p['w_spatial'][i], p['b_spatial'][i])
        xbc = _conv_fwd("convc_fwd", proj, 2, w['cc'], p['conv_c_b'][i], KC, False, nb)
        dtb, alog, dsk = _row128(p['dt_bias'][i]), _row128(p['a_log'][i]), _row128(p['d_skip'][i])
        yc, sin = _ssd_fwd(xbc, proj, 2, pdt, dtb, alog, dsk, p['norm_c_g'][i], nh, ngrp, nst, nb)
        ycat = jnp.concatenate([ya, yb, yc], axis=1)
        (xm,) = _mm("mm_out", ycat, w['wout'], "nn", [f32], _ep_add, (xc,))
        h2 = _rms_fwd(xm, p['norm2_g'][i])
        f, a = _mm("mm_ff1", h2, w['w1t'], "nt", [f32, bf16], _ep_relu2)
        (xo,) = _mm("mm_ff2", a, w['w2'], "nn", [f32], _ep_add, (xm,))
        saved.append(dict(x_in=xc, h1=h1, proj=proj, pdt=pdt, xbc=xbc, sin=sin, ycat=ycat, xm=xm, h2=h2, f=f, a=a,
                          dtb=dtb, alog=alog, dsk=dsk))
        xc = xo

    lp, dx, dfinal = _loss_head(xc, p['final_g'], tgt)
    loss = lax.psum(lp[0, 0], ("x", "y", "c"))

    gs = {n: [None] * depth for n in _REPL + _CONVW}
    gbig = {n: [None] * depth for n in _BIG}
    for i in reversed(range(depth)):
        w, sv = W[i], saved[i]
        (df,) = _mm("mm_df", dx, w['w2'], "nt", [bf16], _ep_drelu2, (sv['f'],))
        (gw2,) = _mm("mm_gw2", sv['a'], dx, "tn", [bf16])
        (dh2,) = _mm("mm_dh2", df, w['w1t'], "nn", [f32])
        (gw1t,) = _mm("mm_gw1", df, sv['h2'], "tn", [bf16])
        dxm, dg2 = _rms_bwd(sv['xm'], p['norm2_g'][i], dh2, dx)
        (dycat,) = _mm("mm_dycat", dxm, w['wout'], "nt", [f32])
        (gwout,) = _mm("mm_gwout", sv['ycat'], dxm, "tn", [bf16])
        da, dwa, dba, dlag, dlab = _conv_bwd("confa_bwd", sv['proj'], 0, w['ca'], p['conv_a_b'][i], dycat, 0, KA, True, nb,
                                             p['ln_a_g'][i], p['ln_a_b'][i])
        dbb, dlbg, dlbb, dws, dbs = _gmlp_bwd(sv['proj'], 1, p['ln_b_g'][i], p['ln_b_b'][i], p['w_spatial'][i],
                                              p['b_spatial'][i], dycat, 1)
        dxbc, dz, ddt, ddtb, dalog, ddsk, dng = _ssd_bwd(sv['xbc'], sv['proj'], 2, sv['pdt'], sv['dtb'], sv['alog'], sv['dsk'],
                                                         p['norm_c_g'][i], sv['sin'], dycat, 1, nh, ngrp, nst, nb)
        dxbcp, dwc, dbc = _conv_bwd("convc_bwd", sv['proj'], 2, w['cc'], p['conv_c_b'][i], dxbc, 0, KC, False, nb)
        dproj = jnp.concatenate([da, dbb, dz, dxbcp], axis=1)
        (dh_main,) = _mm("mm_dh1", dproj, w['wt_main'], "nn", [f32])
        (dh,) = _mm("mm_dh1dt", ddt, w['wt_dt'], "nn", [f32], _ep_add, (dh_main,))
        (gwt_main,) = _mm("mm_gwin", dproj, sv['h1'], "tn", [bf16])
        (gwt_dt,) = _mm("mm_gwdt", ddt, sv['h1'], "tn", [bf16])
        dx, dg1 = _rms_bwd(sv['x_in'], p['norm1_g'][i], dh, dxm)

        gbig['w_in'][i] = jnp.concatenate([gwt_main, gwt_dt[:nh]], axis=0).reshape(N_DEV, -1, D)
        gbig['w_out'][i] = gwout.reshape(N_DEV, -1, D)
        gbig['w_ff1'][i] = gw1t.reshape(N_DEV, -1, D)
        gbig['w_ff2'][i] = gw2.reshape(N_DEV, -1, D)
        gs['norm1_g'][i] = dg1[0]
        gs['norm2_g'][i] = dg2[0]
        gs['conv_a_w'][i] = dwa[:KA]
        gs['conv_a_b'][i] = dba[0]
        gs['ln_a_g'][i] = dlag[0]
        gs['ln_a_b'][i] = dlab[0]
        gs['ln_b_g'][i] = dlbg[0]
        gs['ln_b_b'][i] = dlbb[0]
        gs['w_spatial'][i] = dws
        gs['b_spatial'][i] = dbs
        gs['conv_c_w'][i] = dwc[:KC]
        gs['conv_c_b'][i] = dbc[0]
        gs['dt_bias'][i] = ddtb[0, :nh]
        gs['a_log'][i] = dalog[0, :nh]
        gs['d_skip'][i] = ddsk[0, :nh]
        gs['norm_c_g'][i] = dng[0]
    grad_x = dx.reshape(nb, S, D)

    out = {}
    for i in range(depth):
        got = _exchange("scatter_g", [gbig[n][i] for n in _BIG], True)
        for n, parts in zip(_BIG, got):
            tr = (lambda t: t.T) if _BIG_T[n] else (lambda t: t)
            res = _adam("adam_" + n, parts, tr(p[n][i]), tr(m[n][i]), tr(v[n][i]))
            for kind, r in zip(("grad", "delta", "new_m", "new_v"), res):
                out.setdefault((kind, n), []).append(tr(r))
    for n in _BIG:
        for kind in ("grad", "delta", "new_m", "new_v"):
            out[(kind, n)] = jnp.stack(out[(kind, n)])

    names1 = _REPL + _CONVW
    g1 = _pack([jnp.stack(gs[n]) for n in names1], dfinal[0])
    (parts1,) = _exchange("gather_g", [g1], False)
    gsum = _sum_parts("sum_small", parts1)
    shapes1 = [(depth,) + gs[n][0].shape for n in names1]
    glist, gfinal = _unpack(gsum, shapes1, D)
    gd = dict(zip(names1, glist))
    for n in _CONVW:
        cw_shard = p[n].shape[2]
        gd[n] = lax.dynamic_slice_in_dim(gd[n], me * cw_shard, cw_shard, axis=2)
    g2 = _pack([gd[n] for n in names1], gfinal)
    w2_, m2_, v2_ = (_pack([q[n] for n in names1], q['final_g']) for q in (p, m, v))
    res = _adam("adam_small", g2[None], w2_, m2_, v2_)
    shapes2 = [p[n].shape for n in names1]
    for kind, r in zip(("grad", "delta", "new_m", "new_v"), res):
        lst, fin = _unpack(r, shapes2, D)
        for n, arr in zip(names1, lst):
            out[(kind, n)] = arr
        out[(kind, 'final_g')] = fin

    flat = [loss, grad_x]
    for kind in ("grad", "delta", "new_m", "new_v"):
        flat += [out[(kind, n)] for n in _NAMES]
    return tuple(flat)


def kernel(x, norm1_g, w_in, conv_a_w, conv_a_b, ln_a_g, ln_a_b, ln_b_g, ln_b_b, w_spatial, b_spatial, conv_c_w, conv_c_b, dt_bias, a_log, d_skip, norm_c_g, w_out, norm2_g, w_ff1, w_ff2, final_g, loss_target, m_norm1_g, m_w_in, m_conv_a_w, m_conv_a_b, m_ln_a_g, m_ln_a_b, m_ln_b_g, m_ln_b_b, m_w_spatial, m_b_spatial, m_conv_c_w, m_conv_c_b, m_dt_bias, m_a_log, m_d_skip, m_norm_c_g, m_w_out, m_norm2_g, m_w_ff1, m_w_ff2, m_final_g, v_norm1_g, v_w_in, v_conv_a_w, v_conv_a_b, v_ln_a_g, v_ln_a_b, v_ln_b_g, v_ln_b_b, v_w_spatial, v_b_spatial, v_conv_c_w, v_conv_c_b, v_dt_bias, v_a_log, v_d_skip, v_norm_c_g, v_w_out, v_norm2_g, v_w_ff1, v_w_ff2, v_final_g):
    p = dict(zip(_NAMES, (norm1_g, w_in, conv_a_w, conv_a_b, ln_a_g, ln_a_b, ln_b_g, ln_b_b, w_spatial, b_spatial, conv_c_w,
                          conv_c_b, dt_bias, a_log, d_skip, norm_c_g, w_out, norm2_g, w_ff1, w_ff2, final_g)))
    m = dict(zip(_NAMES, (m_norm1_g, m_w_in, m_conv_a_w, m_conv_a_b, m_ln_a_g, m_ln_a_b, m_ln_b_g, m_ln_b_b, m_w_spatial,
                          m_b_spatial, m_conv_c_w, m_conv_c_b, m_dt_bias, m_a_log, m_d_skip, m_norm_c_g, m_w_out, m_norm2_g,
                          m_w_ff1, m_w_ff2, m_final_g)))
    v = dict(zip(_NAMES, (v_norm1_g, v_w_in, v_conv_a_w, v_conv_a_b, v_ln_a_g, v_ln_a_b, v_ln_b_g, v_ln_b_b, v_w_spatial,
                          v_b_spatial, v_conv_c_w, v_conv_c_b, v_dt_bias, v_a_log, v_d_skip, v_norm_c_g, v_w_out, v_norm2_g,
                          v_w_ff1, v_w_ff2, v_final_g)))
    return _step(p, m, v, x, loss_target)
```

```python
import functools
import math

import jax
import jax.numpy as jnp
from jax import lax
from jax.experimental import pallas as pl
from jax.experimental.pallas import tpu as pltpu

f32 = jnp.float32
bf16 = jnp.bfloat16
HI = lax.Precision.HIGHEST
EPS = 1e-5
HEAD = 64
CHUNK = 128
KA = 31
KC = 4
N_DEV = 8
VMEM_LIMIT = 56 * 1024 * 1024

ADAM_LR = 0.001
ADAM_B1 = 0.9
ADAM_B2 = 0.999
ADAM_EPS = 1e-08
ADAM_WD = 0.01
ADAM_STEP = 10


def _cparams(sem=None):
    return pltpu.CompilerParams(dimension_semantics=sem, vmem_limit_bytes=VMEM_LIMIT)


def _sds(shape, dtype):
    return jax.ShapeDtypeStruct(shape, dtype)


_DIMS = {"nn": ((1,), (0,)), "nt": ((1,), (1,)), "tn": ((0,), (0,))}


def _tile(n, cap):
    if n <= cap:
        return n
    for d in range(cap - cap % 128, 0, -128):
        if n % d == 0:
            return d
    raise ValueError((n, cap))


def _mm(name, a, b, form, out_dtypes, epilogue=None, extras=(), tm=1024, tn=512, tk=2048):
    if form == "tn":
        K, M = a.shape
    else:
        M, K = a.shape
    N = b.shape[0] if form == "nt" else b.shape[1]
    tm, tn, tk = _tile(M, tm), _tile(N, tn), _tile(K, tk)
    nk = K // tk
    ne, no = len(extras), len(out_dtypes)
    if epilogue is None:
        epilogue = lambda acc: (acc,)

    def body(a_ref, b_ref, *rest):
        extra_refs, out_refs = rest[:ne], rest[ne:ne + no]
        part = lax.dot_general(a_ref[...].astype(bf16), b_ref[...].astype(bf16),
                               (_DIMS[form], ((), ())), preferred_element_type=f32)

        def finish(acc):
            outs = epilogue(acc, *[e[...] for e in extra_refs])
            for o_ref, v in zip(out_refs, outs):
                o_ref[...] = v.astype(o_ref.dtype)

        if nk == 1:
            finish(part)
            return
        acc_ref = rest[ne + no]
        k = pl.program_id(2)

        @pl.when(k == 0)
        def _():
            acc_ref[...] = part

        @pl.when((k > 0) & (k < nk - 1))
        def _():
            acc_ref[...] += part

        @pl.when(k == nk - 1)
        def _():
            finish(acc_ref[...] + part)

    a_spec = pl.BlockSpec((tk, tm), lambda i, j, k: (k, i)) if form == "tn" else pl.BlockSpec((tm, tk), lambda i, j, k: (i, k))
    b_spec = pl.BlockSpec((tn, tk), lambda i, j, k: (j, k)) if form == "nt" else pl.BlockSpec((tk, tn), lambda i, j, k: (k, j))
    mn_spec = pl.BlockSpec((tm, tn), lambda i, j, k: (i, j))
    return pl.pallas_call(
        body, name=name, grid=(M // tm, N // tn, nk),
        in_specs=[a_spec, b_spec] + [mn_spec] * ne,
        out_specs=[mn_spec] * no,
        out_shape=[_sds((M, N), d) for d in out_dtypes],
        scratch_shapes=[pltpu.VMEM((tm, tn), f32)] if nk > 1 else [],
        compiler_params=_cparams(("parallel", "parallel", "arbitrary")),
    )(a, b, *extras)


def _ep_add(acc, r):
    return (acc + r,)


def _ep_relu2(acc):
    r = jnp.maximum(acc, 0.0)
    return acc, r * r


def _ep_drelu2(acc, f):
    return (acc * 2.0 * jnp.maximum(f, 0.0),)


def _rms(x, g):
    return x * lax.rsqrt(jnp.mean(x * x, axis=-1, keepdims=True) + EPS) * g


TT = 512


def _rms_fwd(x, g):
    T, D = x.shape

    def body(x_ref, g_ref, h_ref):
        h_ref[...] = _rms(x_ref[...], g_ref[...]).astype(bf16)

    return pl.pallas_call(
        body, name="rms_fwd", grid=(T // TT,),
        in_specs=[pl.BlockSpec((TT, D), lambda i: (i, 0)), pl.BlockSpec((1, D), lambda i: (0, 0))],
        out_specs=pl.BlockSpec((TT, D), lambda i: (i, 0)),
        out_shape=_sds((T, D), bf16), compiler_params=_cparams(("arbitrary",)),
    )(x, g.reshape(1, D))


def _rms_bwd(x, g, dh, dres):
    T, D = x.shape

    def body(x_ref, g_ref, dh_ref, dres_ref, dx_ref, dg_ref):
        _, vjp = jax.vjp(_rms, x_ref[...], g_ref[...])
        dx, dg = vjp(dh_ref[...])
        dx_ref[...] = dres_ref[...] + dx

        @pl.when(pl.program_id(0) == 0)
        def _():
            dg_ref[...] = jnp.zeros_like(dg_ref)

        dg_ref[0:1, :] += dg

    tile = pl.BlockSpec((TT, D), lambda i: (i, 0))
    return pl.pallas_call(
        body, name="rms_bwd", grid=(T // TT,),
        in_specs=[tile, pl.BlockSpec((1, D), lambda i: (0, 0)), tile, tile],
        out_specs=[tile, pl.BlockSpec((8, D), lambda i: (0, 0))],
        out_shape=[_sds((T, D), f32), _sds((8, D), f32)], compiler_params=_cparams(("arbitrary",)),
    )(x, g.reshape(1, D), dh, dres)


def _loss_head(x, g, tgt):
    T, D = x.shape

    def f(xv, gv, tv):
        e = _rms(xv, gv) - tv
        return 0.5 * jnp.sum(jnp.sum(e * e, axis=-1, keepdims=True) * (1.0 / D), axis=0, keepdims=True)

    def body(x_ref, g_ref, t_ref, loss_ref, dx_ref, dg_ref):
        tv = t_ref[...]
        l, vjp = jax.vjp(lambda xv, gv: f(xv, gv, tv), x_ref[...], g_ref[...])
        dx, dg = vjp(jnp.ones((1, 1), f32))
        dx_ref[...] = dx

        @pl.when(pl.program_id(0) == 0)
        def _():
            dg_ref[...] = jnp.zeros_like(dg_ref)
            loss_ref[...] = jnp.zeros_like(loss_ref)

        dg_ref[0:1, :] += dg
        loss_ref[...] += jnp.broadcast_to(l, loss_ref.shape)

    tile = pl.BlockSpec((TT, D), lambda i: (i, 0))
    return pl.pallas_call(
        body, name="loss_head", grid=(T // TT,),
        in_specs=[tile, pl.BlockSpec((1, D), lambda i: (0, 0)), tile],
        out_specs=[pl.BlockSpec((8, 128), lambda i: (0, 0)), tile, pl.BlockSpec((8, D), lambda i: (0, 0))],
        out_shape=[_sds((8, 128), f32), _sds((T, D), f32), _sds((8, D), f32)],
        compiler_params=_cparams(("arbitrary",)),
    )(x, g.reshape(1, D), tgt)


TB = 256


def _glu(a_val, a_gate):
    return a_val * jax.nn.sigmoid(a_gate)


def _ln_silu(v, g, b):
    mu = jnp.mean(v, axis=-1, keepdims=True)
    vc = v - mu
    var = jnp.mean(vc * vc, axis=-1, keepdims=True)
    return jax.nn.silu(vc * lax.rsqrt(var + EPS) * g + b)


def _conv_geom(kw):
    halo = 32 if kw > 9 else 8
    return halo, halo - (kw - 1)


def _conv_taps(hp_ref, w_ref, b_ref, acc_ref, kw, off, width):
    for c in range(width // 128):
        ls = pl.ds(c * 128, 128)
        acc = jnp.broadcast_to(b_ref[:, ls], (TB, 128))
        for k in range(kw):
            acc = acc + w_ref[k:k + 1, ls] * hp_ref[pl.ds(off + k, TB), ls]
        acc_ref[:, ls] = acc


def _conv_fwd(name, src, col_block, w, b, kw, conformer, n_seq, ln_g=None, ln_b=None):
    T = src.shape[0]
    cout = w.shape[1]
    cin = 2 * cout if conformer else cout
    halo, off = _conv_geom(kw)
    nblk = T // n_seq // TB
    hb = TB // halo

    def body(cur_ref, halo_ref, w_ref, b_ref, *rest):
        if conformer:
            g_ref, lb_ref, out_ref, hp_ref, acc_ref = rest
        else:
            out_ref, hp_ref, acc_ref = rest
        i = pl.program_id(1)
        first = (i == 0)
        if conformer:
            hp_ref[pl.ds(halo, TB), :] = _glu(cur_ref[:, 0:cout], cur_ref[:, cout:cin])
            hh = _glu(halo_ref[:, 0:cout], halo_ref[:, cout:cin])
        else:
            hp_ref[pl.ds(halo, TB), :] = cur_ref[...]
            hh = halo_ref[...]
        hp_ref[pl.ds(0, halo), :] = jnp.where(first, 0.0, hh)
        _conv_taps(hp_ref, w_ref, b_ref, acc_ref, kw, off, cout)
        if conformer:
            for h in range(cout // HEAD):
                ls = pl.ds(h * HEAD, HEAD)
                out_ref[:, ls] = _ln_silu(acc_ref[:, ls], g_ref[:, ls], lb_ref[:, ls]).astype(out_ref.dtype)
        else:
            out_ref[...] = jax.nn.silu(acc_ref[...]).astype(out_ref.dtype)

    row = pl.BlockSpec((1, cout), lambda s, i: (0, 0))
    in_specs = [pl.BlockSpec((TB, cin), lambda s, i: (s * nblk + i, col_block)),
                pl.BlockSpec((halo, cin), lambda s, i: (jnp.maximum((s * nblk + i) * hb - 1, 0), col_block)),
                pl.BlockSpec((w.shape[0], cout), lambda s, i: (0, 0)), row]
    args = [src, src, w, b.reshape(1, cout)]
    if conformer:
        in_specs += [row, row]
        args += [ln_g.reshape(1, cout), ln_b.reshape(1, cout)]
    out_dtype = bf16 if conformer else f32
    return pl.pallas_call(
        body, name=name, grid=(n_seq, nblk), in_specs=in_specs,
        out_specs=pl.BlockSpec((TB, cout), lambda s, i: (s * nblk + i, 0)),
        out_shape=_sds((T, cout), out_dtype),
        scratch_shapes=[pltpu.VMEM((halo + TB, cout), f32), pltpu.VMEM((TB, cout), f32)],
        compiler_params=_cparams(("arbitrary", "arbitrary")),
    )(*args)


def _conv_bwd(name, src, col_block, w, b, dy, dy_col_block, kw, conformer, n_seq, ln_g=None, ln_b=None):
    T = src.shape[0]
    cout = w.shape[1]
    wrows = w.shape[0]
    cin = 2 * cout if conformer else cout
    halo, off = _conv_geom(kw)
    nblk = T // n_seq // TB
    hb = TB // halo

    def body(cur_ref, halo_ref, w_ref, b_ref, dy_ref, *rest):
        if conformer:
            g_ref, lb_ref, dsrc_ref, dw_ref, db_ref, dg_ref, dlb_ref, hp_ref, acc_ref, dz_ref, dhp_ref, carry_ref = rest
        else:
            dsrc_ref, dw_ref, db_ref, hp_ref, acc_ref, dz_ref, dhp_ref, carry_ref = rest
        s, ii = pl.program_id(0), pl.program_id(1)
        i = nblk - 1 - ii
        first = (i == 0)

        @pl.when((s == 0) & (ii == 0))
        def _():
            dw_ref[...] = jnp.zeros_like(dw_ref)
            db_ref[...] = jnp.zeros_like(db_ref)
            if conformer:
                dg_ref[...] = jnp.zeros_like(dg_ref)
                dlb_ref[...] = jnp.zeros_like(dlb_ref)

        @pl.when(ii == 0)
        def _():
            carry_ref[...] = jnp.zeros_like(carry_ref)
            dz_ref[pl.ds(0, halo), :] = jnp.zeros((halo, cout), f32)
            dz_ref[pl.ds(halo + TB, halo), :] = jnp.zeros((halo, cout), f32)

        if conformer:
            hp_ref[pl.ds(halo, TB), :] = _glu(cur_ref[:, 0:cout], cur_ref[:, cout:cin])
            hh = _glu(halo_ref[:, 0:cout], halo_ref[:, cout:cin])
        else:
            hp_ref[pl.ds(halo, TB), :] = cur_ref[...]
            hh = halo_ref[...]
        hp_ref[pl.ds(0, halo), :] = jnp.where(first, 0.0, hh)
        _conv_taps(hp_ref, w_ref, b_ref, acc_ref, kw, off, cout)

        if conformer:
            for h in range(cout // HEAD):
                ls = pl.ds(h * HEAD, HEAD)
                _, vjp = jax.vjp(_ln_silu, acc_ref[:, ls], g_ref[:, ls], lb_ref[:, ls])
                da, dg, dlb = vjp(dy_ref[:, ls].astype(f32))
                dz_ref[pl.ds(halo, TB), ls] = da
                dg_ref[0:1, ls] += dg
                dlb_ref[0:1, ls] += dlb
        else:
            _, vjp = jax.vjp(jax.nn.silu, acc_ref[...])
            dz_ref[pl.ds(halo, TB), :] = vjp(dy_ref[...].astype(f32))[0]

        for c in range(cout // 128):
            ls = pl.ds(c * 128, 128)
            dacc = dz_ref[pl.ds(halo, TB), ls]
            db_ref[0:1, ls] += jnp.sum(dacc, axis=0, keepdims=True)
            dhp = jnp.zeros((halo + TB, 128), f32)
            for k in range(kw):
                dw_ref[k:k + 1, ls] += jnp.sum(dacc * hp_ref[pl.ds(off + k, TB), ls], axis=0, keepdims=True)
                dhp = dhp + w_ref[k:k + 1, ls] * dz_ref[pl.ds(kw - 1 - k, halo + TB), ls]
            dhp_ref[:, ls] = dhp
        dhp_ref[pl.ds(TB, halo), :] += carry_ref[...]
        carry_ref[...] = dhp_ref[pl.ds(0, halo), :]
        dcur = dhp_ref[pl.ds(halo, TB), :]
        if conformer:
            _, vjp = jax.vjp(_glu, cur_ref[:, 0:cout], cur_ref[:, cout:cin])
            dval, dgate = vjp(dcur)
            dsrc_ref[:, 0:cout] = dval.astype(dsrc_ref.dtype)
            dsrc_ref[:, cout:cin] = dgate.astype(dsrc_ref.dtype)
        else:
            dsrc_ref[...] = dcur.astype(dsrc_ref.dtype)

    def blk(s, ii):
        return s * nblk + (nblk - 1 - ii)

    row = pl.BlockSpec((1, cout), lambda s, ii: (0, 0))
    acc8 = pl.BlockSpec((8, cout), lambda s, ii: (0, 0))
    in_specs = [pl.BlockSpec((TB, cin), lambda s, ii: (blk(s, ii), col_block)),
                pl.BlockSpec((halo, cin), lambda s, ii: (jnp.maximum(blk(s, ii) * hb - 1, 0), col_block)),
                pl.BlockSpec((wrows, cout), lambda s, ii: (0, 0)), row,
                pl.BlockSpec((TB, cout), lambda s, ii: (blk(s, ii), dy_col_block))]
    args = [src, src, w, b.reshape(1, cout), dy]
    out_specs = [pl.BlockSpec((TB, cin), lambda s, ii: (blk(s, ii), 0)),
                 pl.BlockSpec((wrows, cout), lambda s, ii: (0, 0)), acc8]
    out_shape = [_sds((T, cin), bf16), _sds((wrows, cout), f32), _sds((8, cout), f32)]
    if conformer:
        in_specs += [row, row]
        args += [ln_g.reshape(1, cout), ln_b.reshape(1, cout)]
        out_specs += [acc8, acc8]
        out_shape += [_sds((8, cout), f32), _sds((8, cout), f32)]
    return pl.pallas_call(
        body, name=name, grid=(n_seq, nblk), in_specs=in_specs, out_specs=out_specs, out_shape=out_shape,
        scratch_shapes=[pltpu.VMEM((halo + TB, cout), f32), pltpu.VMEM((TB, cout), f32),
                        pltpu.VMEM((halo + TB + halo, cout), f32), pltpu.VMEM((halo + TB, cout), f32),
                        pltpu.VMEM((halo, cout), f32)],
        compiler_params=_cparams(("arbitrary", "arbitrary")),
    )(*args)


def _gelu(x):
    return 0.5 * x * (1.0 + lax.erf(x * (1.0 / math.sqrt(2.0))))


def _tril_mask(n):
    r = lax.broadcasted_iota(jnp.int32, (n, n), 0)
    c = lax.broadcasted_iota(jnp.int32, (n, n), 1)
    return r >= c


def _row_select(rows, h, width):
    r = lax.broadcasted_iota(jnp.int32, (rows, width), 0)
    return (r == h).astype(f32)


def _gmlp_head(h, bu, bv, g, b, w_h, bs):
    u = _gelu(bu)
    v = _gelu(bv)
    mu = jnp.mean(v, axis=-1, keepdims=True)
    vc = v - mu
    var = jnp.mean(vc * vc, axis=-1, keepdims=True)
    vn = vc * lax.rsqrt(var + EPS) * g + b
    wm = jnp.where(_tril_mask(CHUNK), w_h, 0.0)
    mix = jnp.dot(wm.astype(bf16), vn.astype(bf16), preferred_element_type=f32)
    bias = lax.dot_general(bs, _row_select(bs.shape[0], h, HEAD), (((0,), (0,)), ((), ())),
                           precision=HI, preferred_element_type=f32)
    return u * (mix + bias)


def _gmlp_fwd(proj, col_block, ln_g, ln_b, w_s, b_s):
    T = proj.shape[0]
    nh = w_s.shape[0]
    width = nh * HEAD

    def body(p_ref, g_ref, b_ref, w_ref, bs_ref, out_ref):
        bs = bs_ref[...]
        for h in range(nh):
            ls = pl.ds(h * HEAD, HEAD)
            lv = pl.ds(width + h * HEAD, HEAD)
            out_ref[:, ls] = _gmlp_head(h, p_ref[:, ls], p_ref[:, lv], g_ref[:, ls], b_ref[:, ls], w_ref[h], bs).astype(out_ref.dtype)

    row = pl.BlockSpec((1, width), lambda i: (0, 0))
    return pl.pallas_call(
        body, name="gmlp_fwd", grid=(T // CHUNK,),
        in_specs=[pl.BlockSpec((CHUNK, 2 * width), lambda i: (i, col_block)), row, row,
                  pl.BlockSpec((nh, CHUNK, CHUNK), lambda i: (0, 0, 0)), pl.BlockSpec((nh, CHUNK), lambda i: (0, 0))],
        out_specs=pl.BlockSpec((CHUNK, width), lambda i: (i, 0)),
        out_shape=_sds((T, width), bf16), compiler_params=_cparams(("arbitrary",)),
    )(proj, ln_g.reshape(1, width), ln_b.reshape(1, width), w_s, b_s)


def _gmlp_bwd(proj, col_block, ln_g, ln_b, w_s, b_s, dy, dy_col_block):
    T = proj.shape[0]
    nh = w_s.shape[0]
    width = nh * HEAD

    def body(p_ref, g_ref, b_ref, w_ref, bs_ref, dy_ref, dp_ref, dg_ref, db_ref, dw_ref, dbs_ref):
        @pl.when(pl.program_id(0) == 0)
        def _():
            dg_ref[...] = jnp.zeros_like(dg_ref)
            db_ref[...] = jnp.zeros_like(db_ref)
            dw_ref[...] = jnp.zeros_like(dw_ref)
            dbs_ref[...] = jnp.zeros_like(dbs_ref)

        bs = bs_ref[...]
        for h in range(nh):
            ls = pl.ds(h * HEAD, HEAD)
            lv = pl.ds(width + h * HEAD, HEAD)
            _, vjp = jax.vjp(functools.partial(_gmlp_head, h), p_ref[:, ls], p_ref[:, lv], g_ref[:, ls], b_ref[:, ls], w_ref[h], bs)
            dbu, dbv, dg, db, dw, dbs = vjp(dy_ref[:, ls].astype(f32))
            dp_ref[:, ls] = dbu.astype(dp_ref.dtype)
            dp_ref[:, lv] = dbv.astype(dp_ref.dtype)
            dg_ref[0:1, ls] += dg
            db_ref[0:1, ls] += db
            dw_ref[h] += dw
            dbs_ref[...] += dbs

    row = pl.BlockSpec((1, width), lambda i: (0, 0))
    acc8 = pl.BlockSpec((8, width), lambda i: (0, 0))
    wspec = pl.BlockSpec((nh, CHUNK, CHUNK), lambda i: (0, 0, 0))
    bspec = pl.BlockSpec((nh, CHUNK), lambda i: (0, 0))
    return pl.pallas_call(
        body, name="gmlp_bwd", grid=(T // CHUNK,),
        in_specs=[pl.BlockSpec((CHUNK, 2 * width), lambda i: (i, col_block)), row, row, wspec, bspec,
                  pl.BlockSpec((CHUNK, width), lambda i: (i, dy_col_block))],
        out_specs=[pl.BlockSpec((CHUNK, 2 * width), lambda i: (i, 0)), acc8, acc8, wspec, bspec],
        out_shape=[_sds((T, 2 * width), bf16), _sds((8, width), f32), _sds((8, width), f32),
                   _sds((nh, CHUNK, CHUNK), f32), _sds((nh, CHUNK), f32)],
        compiler_params=_cparams(("arbitrary",)),
    )(proj, ln_g.reshape(1, width), ln_b.reshape(1, width), w_s, b_s, dy)


def _sel_col(x, h):
    lane = lax.broadcasted_iota(jnp.int32, x.shape, 1)
    return jnp.sum(jnp.where(lane == h, x, 0.0), axis=1, keepdims=True)


def _sel_row(x, h):
    sub = lax.broadcasted_iota(jnp.int32, x.shape, 0)
    return jnp.sum(jnp.where(sub == h, x, 0.0), axis=0, keepdims=True)


PAIR = 2 * HEAD


def _ssd_chunk(nh, ngrp, xs_l, z_l, b_l, c_l, dtraw, dtb, alog, dskip, ng_l, prev_l):
    hg = nh // ngrp
    tril = _tril_mask(CHUNK)
    tl = tril.astype(f32)
    lo = lax.broadcasted_iota(jnp.int32, (CHUNK, PAIR), 1) < HEAD
    lo_row = lo[0:1, :]
    dt = jax.nn.softplus(dtraw + dtb)
    a = dt * (-jnp.exp(alog))
    cs = jnp.dot(tl, a, precision=HI, preferred_element_type=f32)
    cst = lax.dot_general(a, tl, (((0,), (1,)), ((), ())), precision=HI, preferred_element_type=f32)
    cb_l = [lax.dot_general(c_l[g].astype(bf16), b_l[g].astype(bf16), (((1,), (1,)), ((), ())),
                            preferred_element_type=f32) for g in range(ngrp)]
    yz_l, new_prev = [], []
    for q in range(nh // 2):
        g = (2 * q) // hg
        cols = []
        for h in (2 * q, 2 * q + 1):
            cs_h = _sel_col(cs, h)
            tot = _sel_row(cs_h, CHUNK - 1)
            seg = jnp.where(tril, cs_h - _sel_row(cst, h), 0.0)
            lmat = jnp.where(tril, jnp.exp(seg), 0.0)
            cols.append((_sel_col(dt, h), cs_h, tot, lmat, _sel_col(dskip, h)))
        (dt_a, cs_a, tot_a, l_a, dsk_a), (dt_b, cs_b, tot_b, l_b, dsk_b) = cols
        xs = xs_l[q]
        x = xs * jnp.where(lo, dt_a, dt_b)
        xb = x.astype(bf16)
        ydiag = jnp.where(lo, jnp.dot((cb_l[g] * l_a).astype(bf16), xb, preferred_element_type=f32),
                          jnp.dot((cb_l[g] * l_b).astype(bf16), xb, preferred_element_type=f32))
        yoff = (jnp.dot(c_l[g].astype(bf16), prev_l[q].astype(bf16), preferred_element_type=f32)
                * jnp.where(lo, jnp.exp(cs_a), jnp.exp(cs_b)))
        xdec = x * jnp.where(lo, jnp.exp(tot_a - cs_a), jnp.exp(tot_b - cs_b))
        st = lax.dot_general(b_l[g].astype(bf16), xdec.astype(bf16), (((0,), (0,)), ((), ())),
                             preferred_element_type=f32)
        new_prev.append(prev_l[q] * jnp.where(lo_row, jnp.exp(tot_a), jnp.exp(tot_b)) + st)
        y = ydiag + yoff + jnp.where(lo_row, dsk_a, dsk_b) * xs
        yz_l.append(y * jax.nn.silu(z_l[q]))
    out = [None] * (nh // 2)
    qg = hg // 2
    for g in range(ngrp):
        ssq = sum(jnp.sum(yz_l[q] * yz_l[q], axis=-1, keepdims=True) for q in range(g * qg, (g + 1) * qg))
        r = lax.rsqrt(ssq * (1.0 / (hg * HEAD)) + EPS)
        for q in range(g * qg, (g + 1) * qg):
            out[q] = yz_l[q] * r * ng_l[q]
    return out, new_prev


def _ssd_read(nh, ngrp, nst, xbc_ref, z_ref, ng_ref, st_ref):
    cw = nh * HEAD
    xs_l = [xbc_ref[:, pl.ds(q * PAIR, PAIR)] for q in range(nh // 2)]
    b_l = [xbc_ref[:, pl.ds(cw + g * nst, nst)] for g in range(ngrp)]
    c_l = [xbc_ref[:, pl.ds(cw + ngrp * nst + g * nst, nst)] for g in range(ngrp)]
    z_l = [z_ref[:, pl.ds(q * PAIR, PAIR)] for q in range(nh // 2)]
    ng_l = [ng_ref[:, pl.ds(q * PAIR, PAIR)] for q in range(nh // 2)]
    prev_l = [st_ref[:, pl.ds(q * PAIR, PAIR)] for q in range(nh // 2)]
    return xs_l, z_l, b_l, c_l, ng_l, prev_l


def _ssd_fwd(xbc, proj, z_col_block, pdt, dtb, alog, dskip, ng, nh, ngrp, nst, n_seq):
    T = xbc.shape[0]
    cw = nh * HEAD
    nchunk = T // n_seq // CHUNK
    assert nst == CHUNK

    def body(xbc_ref, z_ref, dt_ref, dtb_ref, alog_ref, dskip_ref, ng_ref, y_ref, sin_ref, st_ref):
        @pl.when(pl.program_id(1) == 0)
        def _():
            st_ref[...] = jnp.zeros_like(st_ref)

        sin_ref[...] = st_ref[...]
        xs_l, z_l, b_l, c_l, ng_l, prev_l = _ssd_read(nh, ngrp, nst, xbc_ref, z_ref, ng_ref, st_ref)
        y_l, new_prev = _ssd_chunk(nh, ngrp, xs_l, z_l, b_l, c_l, dt_ref[...], dtb_ref[...], alog_ref[...],
                                   dskip_ref[...], ng_l, prev_l)
        for q in range(nh // 2):
            ls = pl.ds(q * PAIR, PAIR)
            y_ref[:, ls] = y_l[q].astype(y_ref.dtype)
            st_ref[:, ls] = new_prev[q]

    def blk(s, c):
        return s * nchunk + c

    prow = pl.BlockSpec((1, 128), lambda s, c: (0, 0))
    return pl.pallas_call(
        body, name="ssd_fwd", grid=(n_seq, nchunk),
        in_specs=[pl.BlockSpec((CHUNK, xbc.shape[1]), lambda s, c: (blk(s, c), 0)),
                  pl.BlockSpec((CHUNK, cw), lambda s, c: (blk(s, c), z_col_block)),
                  pl.BlockSpec((CHUNK, 128), lambda s, c: (blk(s, c), 0)),
                  prow, prow, prow, pl.BlockSpec((1, cw), lambda s, c: (0, 0))],
        out_specs=[pl.BlockSpec((CHUNK, cw), lambda s, c: (blk(s, c), 0)),
                   pl.BlockSpec((nst, cw), lambda s, c: (blk(s, c), 0))],
        out_shape=[_sds((T, cw), bf16), _sds((T, cw), f32)],
        scratch_shapes=[pltpu.VMEM((nst, cw), f32)],
        compiler_params=_cparams(("arbitrary", "arbitrary")),
    )(xbc, proj, pdt, dtb, alog, dskip, ng.reshape(1, cw))


def _ssd_bwd(xbc, proj, z_col_block, pdt, dtb, alog, dskip, ng, sin, dy, dy_col_block, nh, ngrp, nst, n_seq):
    T, xw = xbc.shape
    cw = nh * HEAD
    nchunk = T // n_seq // CHUNK

    def body(xbc_ref, z_ref, dt_ref, dtb_ref, alog_ref, dskip_ref, ng_ref, sin_ref, dy_ref,
             dxbc_ref, dz_ref, ddt_ref, ddtb_ref, dalog_ref, ddskip_ref, dng_ref, dst_ref):
        s, cc = pl.program_id(0), pl.program_id(1)

        @pl.when((s == 0) & (cc == 0))
        def _():
            ddtb_ref[...] = jnp.zeros_like(ddtb_ref)
            dalog_ref[...] = jnp.zeros_like(dalog_ref)
            ddskip_ref[...] = jnp.zeros_like(ddskip_ref)
            dng_ref[...] = jnp.zeros_like(dng_ref)

        @pl.when(cc == 0)
        def _():
            dst_ref[...] = jnp.zeros_like(dst_ref)

        xs_l, z_l, b_l, c_l, ng_l, prev_l = _ssd_read(nh, ngrp, nst, xbc_ref, z_ref, ng_ref, sin_ref)
        _, vjp = jax.vjp(functools.partial(_ssd_chunk, nh, ngrp), xs_l, z_l, b_l, c_l, dt_ref[...], dtb_ref[...],
                         alog_ref[...], dskip_ref[...], ng_l, prev_l)
        dy_l = [dy_ref[:, pl.ds(q * PAIR, PAIR)].astype(f32) for q in range(nh // 2)]
        dst_l = [dst_ref[:, pl.ds(q * PAIR, PAIR)] for q in range(nh // 2)]
        dxs_l, dz_l, db_l, dc_l, ddt, ddtb, dalog, ddskip, dng_l, dprev_l = vjp((dy_l, dst_l))
        for q in range(nh // 2):
            ls = pl.ds(q * PAIR, PAIR)
            dxbc_ref[:, ls] = dxs_l[q]
            dz_ref[:, ls] = dz_l[q].astype(dz_ref.dtype)
            dng_ref[0:1, ls] += dng_l[q]
            dst_ref[:, ls] = dprev_l[q]
        for g in range(ngrp):
            dxbc_ref[:, pl.ds(cw + g * nst, nst)] = db_l[g]
            dxbc_ref[:, pl.ds(cw + ngrp * nst + g * nst, nst)] = dc_l[g]
        ddt_ref[...] = ddt.astype(ddt_ref.dtype)
        ddtb_ref[0:1, :] += ddtb
        dalog_ref[0:1, :] += dalog
        ddskip_ref[0:1, :] += ddskip

    def blk(s, cc):
        return s * nchunk + (nchunk - 1 - cc)

    prow = pl.BlockSpec((1, 128), lambda s, c: (0, 0))
    pacc = pl.BlockSpec((8, 128), lambda s, c: (0, 0))
    return pl.pallas_call(
        body, name="ssd_bwd", grid=(n_seq, nchunk),
        in_specs=[pl.BlockSpec((CHUNK, xw), lambda s, c: (blk(s, c), 0)),
                  pl.BlockSpec((CHUNK, cw), lambda s, c: (blk(s, c), z_col_block)),
                  pl.BlockSpec((CHUNK, 128), lambda s, c: (blk(s, c), 0)),
                  prow, prow, prow, pl.BlockSpec((1, cw), lambda s, c: (0, 0)),
                  pl.BlockSpec((nst, cw), lambda s, c: (blk(s, c), 0)),
                  pl.BlockSpec((CHUNK, cw), lambda s, c: (blk(s, c), dy_col_block))],
        out_specs=[pl.BlockSpec((CHUNK, xw), lambda s, c: (blk(s, c), 0)),
                   pl.BlockSpec((CHUNK, cw), lambda s, c: (blk(s, c), 0)),
                   pl.BlockSpec((CHUNK, 128), lambda s, c: (blk(s, c), 0)),
                   pacc, pacc, pacc, pl.BlockSpec((8, cw), lambda s, c: (0, 0))],
        out_shape=[_sds((T, xw), f32), _sds((T, cw), bf16), _sds((T, 128), bf16),
                   _sds((8, 128), f32), _sds((8, 128), f32), _sds((8, 128), f32), _sds((8, cw), f32)],
        scratch_shapes=[pltpu.VMEM((nst, cw), f32)],
        compiler_params=_cparams(("arbitrary", "arbitrary")),
    )(xbc, proj, pdt, dtb, alog, dskip, ng.reshape(1, cw), sin, dy)


_HBM = pl.BlockSpec(memory_space=pltpu.HBM)


def _exchange(name, arrs, scatter):
    n = len(arrs)
    npeer = N_DEV - 1

    def body(*refs):
        in_refs, out_refs = refs[:n], refs[n:2 * n]
        send_sems, recv_sems, local_sems = refs[2 * n:]
        x, y, c = lax.axis_index("x"), lax.axis_index("y"), lax.axis_index("c")
        me = 4 * x + 2 * y + c
        copies = []
        for i in range(n):
            src = in_refs[i].at[me] if scatter else in_refs[i]
            loc = pltpu.make_async_copy(src, out_refs[i].at[me], local_sems.at[i])
            loc.start()
            copies.append(loc)
            for k in range(1, N_DEV):
                px = 1 - x if k & 4 else x
                py = 1 - y if k & 2 else y
                pc = 1 - c if k & 1 else c
                src = in_refs[i].at[4 * px + 2 * py + pc] if scatter else in_refs[i]
                cp = pltpu.make_async_remote_copy(
                    src_ref=src, dst_ref=out_refs[i].at[me],
                    send_sem=send_sems.at[i * npeer + k - 1], recv_sem=recv_sems.at[i * npeer + k - 1],
                    device_id=(px, py, pc), device_id_type=pl.DeviceIdType.MESH)
                cp.start()
                copies.append(cp)
        for cp in copies:
            cp.wait()

    out_shape = [_sds(a.shape if scatter else (N_DEV,) + a.shape, a.dtype) for a in arrs]
    return pl.pallas_call(
        body, name=name, in_specs=[_HBM] * n, out_specs=[_HBM] * n, out_shape=out_shape,
        scratch_shapes=[pltpu.SemaphoreType.DMA((n * npeer,)), pltpu.SemaphoreType.DMA((n * npeer,)),
                        pltpu.SemaphoreType.DMA((n,))],
    )(*arrs)


_SEM = pl.BlockSpec(memory_space=pltpu.SEMAPHORE)
_EFFECT = pltpu.SideEffectType.DATAFLOW_SIDE_EFFECTING


def _split_copies(n, scatter, src_refs, land_refs, send_sems, recv_sems):
    npeer = N_DEV - 1
    x, y, c = lax.axis_index("x"), lax.axis_index("y"), lax.axis_index("c")
    me = 4 * x + 2 * y + c
    copies = []
    for i in range(n):
        for k in range(1, N_DEV):
            px = 1 - x if k & 4 else x
            py = 1 - y if k & 2 else y
            pc = 1 - c if k & 1 else c
            src = src_refs[i].at[4 * px + 2 * py + pc] if scatter else src_refs[i]
            copies.append(pltpu.make_async_remote_copy(
                src_ref=src, dst_ref=land_refs[i].at[me],
                send_sem=send_sems.at[i * npeer + k - 1], recv_sem=recv_sems.at[i * npeer + k - 1],
                device_id=(px, py, pc), device_id_type=pl.DeviceIdType.MESH))
    return copies


def _exchange_start(name, arrs, scatter):
    n = len(arrs)
    nsem = n * (N_DEV - 1)
    me = 4 * lax.axis_index("x") + 2 * lax.axis_index("y") + lax.axis_index("c")
    lands = []
    for a in arrs:
        own = lax.dynamic_index_in_dim(a, me, 0, keepdims=True) if scatter else a[None]
        full = lax.empty(a.shape if scatter else (N_DEV,) + a.shape, a.dtype)
        lands.append(lax.dynamic_update_slice(full, own, (me,) + (0,) * (full.ndim - 1)))

    def body(*refs):
        src_refs, land_refs = refs[:n], refs[n:2 * n]
        send_sems, recv_sems = refs[2 * n], refs[2 * n + 1]
        token = refs[-1]
        for cp in _split_copies(n, scatter, src_refs, land_refs, send_sems, recv_sems):
            cp.start()
        token[...] = jnp.zeros_like(token)

    res = pl.pallas_call(
        body, name=name,
        out_shape=(pltpu.SemaphoreType.DMA((nsem,)), pltpu.SemaphoreType.DMA((nsem,)),
                   *[pltpu.HBM(a.shape, a.dtype) for a in arrs], *[pltpu.HBM(l.shape, l.dtype) for l in lands],
                   _sds((8, 128), f32)),
        in_specs=[_HBM] * (2 * n),
        out_specs=(_SEM, _SEM, *[_HBM] * (2 * n), pl.BlockSpec(memory_space=pltpu.VMEM)),
        input_output_aliases={j: 2 + j for j in range(2 * n)},
        compiler_params=pltpu.CompilerParams(has_side_effects=_EFFECT),
    )(*[pltpu.with_memory_space_constraint(a, pltpu.HBM) for a in arrs],
      *[pltpu.with_memory_space_constraint(l, pltpu.HBM) for l in lands])
    return (n, scatter, res[0], res[1], res[2:2 + n], res[2 + n:2 + 2 * n]), res[-1]


def _exchange_wait(name, handle, after):
    n, scatter, send_sems, recv_sems, srcs, lands = handle

    def body(*refs):
        src_refs, land_refs = refs[:n], refs[n:2 * n]
        for cp in _split_copies(n, scatter, src_refs, land_refs, refs[2 * n], refs[2 * n + 1]):
            cp.wait_send()
            cp.wait_recv()

    res = pl.pallas_call(
        body, name=name,
        out_shape=[pltpu.HBM(a.shape, a.dtype) for a in (*srcs, *lands)],
        in_specs=[_HBM] * (2 * n) + [_SEM, _SEM, pl.BlockSpec(memory_space=pl.ANY)],
        out_specs=[_HBM] * (2 * n),
        input_output_aliases={j: j for j in range(2 * n)},
        compiler_params=pltpu.CompilerParams(has_side_effects=_EFFECT),
    )(*srcs, *lands, send_sems, recv_sems, after)
    return res[n:]


def _adam_tiles(R, C):
    if R % 256 == 0:
        return (256, C), (R // 256, 1)
    assert C % 128 == 0
    return (R, 128), (1, C // 128)


def _adam(name, parts, w, m, v):
    P, R, C = parts.shape
    (tr, tc), (gr, gc) = _adam_tiles(R, C)
    c1 = 1.0 / (1.0 - ADAM_B1 ** ADAM_STEP)
    c2 = 1.0 / (1.0 - ADAM_B2 ** ADAM_STEP)

    def body(p_ref, w_ref, m_ref, v_ref, g_ref, d_ref, nm_ref, nv_ref):
        g = p_ref[0].astype(f32)
        for s in range(1, P):
            g = g + p_ref[s].astype(f32)
        nm = ADAM_B1 * m_ref[...] + (1.0 - ADAM_B1) * g
        nv = ADAM_B2 * v_ref[...] + (1.0 - ADAM_B2) * (g * g)
        g_ref[...] = g
        nm_ref[...] = nm
        nv_ref[...] = nv
        d_ref[...] = -ADAM_LR * ((nm * c1) / (jnp.sqrt(nv * c2) + ADAM_EPS) + ADAM_WD * w_ref[...])

    tile = pl.BlockSpec((tr, tc), lambda i, j: (i, j))
    return pl.pallas_call(
        body, name=name, grid=(gr, gc),
        in_specs=[pl.BlockSpec((P, tr, tc), lambda i, j: (0, i, j)), tile, tile, tile],
        out_specs=[tile] * 4, out_shape=[_sds((R, C), f32)] * 4,
        compiler_params=_cparams(("arbitrary", "arbitrary")),
    )(parts, w, m, v)


def _sum_parts(name, parts):
    P, R, C = parts.shape
    tr = 256 if R % 256 == 0 else R

    def body(p_ref, o_ref):
        g = p_ref[0]
        for s in range(1, P):
            g = g + p_ref[s]
        o_ref[...] = g

    return pl.pallas_call(
        body, name=name, grid=(R // tr,),
        in_specs=[pl.BlockSpec((P, tr, C), lambda i: (0, i, 0))], out_specs=pl.BlockSpec((tr, C), lambda i: (i, 0)),
        out_shape=_sds((R, C), f32), compiler_params=_cparams(("arbitrary",)),
    )(parts)


def _pad_to(a, n, axis):
    if a.shape[axis] == n:
        return a
    cfg = [(0, 0)] * a.ndim
    cfg[axis] = (0, n - a.shape[axis])
    return jnp.pad(a, cfg)


def _pack(layered, final):
    depth = layered[0].shape[0]
    cols = []
    for a in layered:
        a = a.reshape(depth, -1)
        cols.append(_pad_to(a, -(-a.shape[1] // 128) * 128, 1))
    body = jnp.concatenate(cols, axis=1).reshape(-1, 128)
    rows = jnp.concatenate([body, final.reshape(-1, 128)], axis=0)
    return _pad_to(rows, -(-rows.shape[0] // 256) * 256, 0)


def _unpack(slab, shapes, final_n):
    depth = shapes[0][0]
    widths = [-(-math.prod(s[1:]) // 128) * 128 for s in shapes]
    nl = sum(widths)
    body = slab[:depth * nl // 128].reshape(depth, nl)
    out, o = [], 0
    for s, wd in zip(shapes, widths):
        out.append(body[:, o:o + math.prod(s[1:])].reshape(s))
        o += wd
    r0 = depth * nl // 128
    final = slab[r0:r0 + final_n // 128].reshape(final_n)
    return out, final


_NAMES = ['norm1_g', 'w_in', 'conv_a_w', 'conv_a_b', 'ln_a_g', 'ln_a_b', 'ln_b_g', 'ln_b_b', 'w_spatial', 'b_spatial',
          'conv_c_w', 'conv_c_b', 'dt_bias', 'a_log', 'd_skip', 'norm_c_g', 'w_out', 'norm2_g', 'w_ff1', 'w_ff2', 'final_g']
_REPL = ['norm1_g', 'conv_a_b', 'ln_a_g', 'ln_a_b', 'ln_b_g', 'ln_b_b', 'w_spatial', 'b_spatial', 'conv_c_b',
         'dt_bias', 'a_log', 'd_skip', 'norm_c_g', 'norm2_g']
_CONVW = ['conv_a_w', 'conv_c_w']
_BIG = ['w_in', 'w_out', 'w_ff1', 'w_ff2']
_BIG_T = {'w_in': True, 'w_out': False, 'w_ff1': True, 'w_ff2': False}


def _row128(v):
    return _pad_to(v.reshape(1, -1), 128, 1)


def _step(p, m, v, x, loss_target):
    nb, S, D = x.shape
    T = nb * S
    depth = p['norm1_g'].shape[0]
    a_w = p['conv_a_b'].shape[1]
    b_w = p['ln_b_g'].shape[1]
    nh = p['dt_bias'].shape[1]
    c_w = p['norm_c_g'].shape[1]
    xw = p['conv_c_b'].shape[1]
    ngrp = 2
    nst = (xw - c_w) // (2 * ngrp)
    d_in = p['w_in'].shape[2] * N_DEV
    main = d_in - nh
    assert main == 2 * a_w + 2 * b_w + c_w + xw and 2 * a_w == 2 * b_w == c_w and xw % c_w == c_w // 2
    me = 4 * lax.axis_index("x") + 2 * lax.axis_index("y") + lax.axis_index("c")

    x2 = x.reshape(T, D)
    tgt = loss_target.reshape(T, D)

    def shards(i):
        return [p['w_in'][i].T.astype(bf16), p['w_out'][i].astype(bf16), p['w_ff1'][i].T.astype(bf16),
                p['w_ff2'][i].astype(bf16), p['conv_a_w'][i], p['conv_c_w'][i]]

    def gathered(got):
        wt = got[0].reshape(d_in, D)
        ca = jnp.transpose(got[4], (1, 0, 2)).reshape(KA, a_w)
        cc = jnp.transpose(got[5], (1, 0, 2)).reshape(KC, xw)
        return dict(wt_main=wt[:main], wt_dt=_pad_to(wt[main:], 128, 0), wout=got[1].reshape(-1, D),
                    w1t=got[2].reshape(-1, D), w2=got[3].reshape(-1, D),
                    ca=_pad_to(ca, 32, 0), cc=_pad_to(cc, 8, 0))

    W = [gathered(_exchange("gather_w0", shards(0), False))]

    saved = []
    xc = x2
    for i in range(depth):
        w = W[i]
        g1 = p['norm1_g'][i]
        if i + 1 < depth:
            handle, token = _exchange_start("gather_w%d_start" % (i + 1), shards(i + 1), False)
            g1 = g1 + token[0, 0]
        h1 = _rms_fwd(xc, g1)
        (proj,) = _mm("mm_proj", h1, w['wt_main'], "nt", [f32])
        (pdt,) = _mm("mm_pdt", h1, w['wt_dt'], "nt", [f32])
        ya = _conv_fwd("confa_fwd", proj, 0, w['ca'], p['conv_a_b'][i], KA, True, nb, p['ln_a_g'][i], p['ln_a_b'][i])
        yb = _gmlp_fwd(proj, 1, p['ln_b_g'][i], p['ln_b_b'][i], p['w_spatial'][i], p['b_spatial'][i])
        xbc = _conv_fwd("convc_fwd", proj, 2, w['cc'], p['conv_c_b'][i], KC, False, nb)
        dtb, alog, dsk = _row128(p['dt_bias'][i]), _row128(p['a_log'][i]), _row128(p['d_skip'][i])
        yc, sin = _ssd_fwd(xbc, proj, 2, pdt, dtb, alog, dsk, p['norm_c_g'][i], nh, ngrp, nst, nb)
        ycat = jnp.concatenate([ya, yb, yc], axis=1)
        (xm,) = _mm("mm_out", ycat, w['wout'], "nn", [f32], _ep_add, (xc,))
        h2 = _rms_fwd(xm, p['norm2_g'][i])
        f, a = _mm("mm_ff1", h2, w['w1t'], "nt", [f32, bf16], _ep_relu2)
        (xo,) = _mm("mm_ff2", a, w['w2'], "nn", [f32], _ep_add, (xm,))
        saved.append(dict(x_in=xc, h1=h1, proj=proj, pdt=pdt, xbc=xbc, sin=sin, ycat=ycat, xm=xm, h2=h2, f=f, a=a,
                          dtb=dtb, alog=alog, dsk=dsk))
        xc = xo
        if i + 1 < depth:
            W.append(gathered(_exchange_wait("gather_w%d_wait" % (i + 1), handle, xo)))

    lp, dx, dfinal = _loss_head(xc, p['final_g'], tgt)
    loss = lax.psum(lp[0, 0], ("x", "y", "c"))

    out = {}

    def adam_big(i, got):
        for n, parts in zip(_BIG, got):
            tr = (lambda t: t.T) if _BIG_T[n] else (lambda t: t)
            res = _adam("adam_" + n, parts, tr(p[n][i]), tr(m[n][i]), tr(v[n][i]))
            for kind, r in zip(("grad", "delta", "new_m", "new_v"), res):
                out.setdefault((kind, n), [None] * depth)[i] = tr(r)

    gs = {n: [None] * depth for n in _REPL + _CONVW}
    gbig = {n: [None] * depth for n in _BIG}
    pending = None
    for i in reversed(range(depth)):
        w, sv = W[i], saved[i]
        if pending is not None:
            dx = dx + pending[2][0, 0]
        (df,) = _mm("mm_df", dx, w['w2'], "nt", [bf16], _ep_drelu2, (sv['f'],))
        (gw2,) = _mm("mm_gw2", sv['a'], dx, "tn", [bf16])
        (dh2,) = _mm("mm_dh2", df, w['w1t'], "nn", [f32])
        (gw1t,) = _mm("mm_gw1", df, sv['h2'], "tn", [bf16])
        dxm, dg2 = _rms_bwd(sv['xm'], p['norm2_g'][i], dh2, dx)
        (dycat,) = _mm("mm_dycat", dxm, w['wout'], "nt", [f32])
        (gwout,) = _mm("mm_gwout", sv['ycat'], dxm, "tn", [bf16])
        da, dwa, dba, dlag, dlab = _conv_bwd("confa_bwd", sv['proj'], 0, w['ca'], p['conv_a_b'][i], dycat, 0, KA, True, nb,
                                             p['ln_a_g'][i], p['ln_a_b'][i])
        dbb, dlbg, dlbb, dws, dbs = _gmlp_bwd(sv['proj'], 1, p['ln_b_g'][i], p['ln_b_b'][i], p['w_spatial'][i],
                                              p['b_spatial'][i], dycat, 1)
        dxbc, dz, ddt, ddtb, dalog, ddsk, dng = _ssd_bwd(sv['xbc'], sv['proj'], 2, sv['pdt'], sv['dtb'], sv['alog'], sv['dsk'],
                                                         p['norm_c_g'][i], sv['sin'], dycat, 1, nh, ngrp, nst, nb)
        dxbcp, dwc, dbc = _conv_bwd("convc_bwd", sv['proj'], 2, w['cc'], p['conv_c_b'][i], dxbc, 0, KC, False, nb)
        dproj = jnp.concatenate([da, dbb, dz, dxbcp], axis=1)
        (dh_main,) = _mm("mm_dh1", dproj, w['wt_main'], "nn", [f32])
        (dh,) = _mm("mm_dh1dt", ddt, w['wt_dt'], "nn", [f32], _ep_add, (dh_main,))
        (gwt_main,) = _mm("mm_gwin", dproj, sv['h1'], "tn", [bf16])
        (gwt_dt,) = _mm("mm_gwdt", ddt, sv['h1'], "tn", [bf16])
        dx, dg1 = _rms_bwd(sv['x_in'], p['norm1_g'][i], dh, dxm)

        gbig['w_in'][i] = jnp.concatenate([gwt_main, gwt_dt[:nh]], axis=0).reshape(N_DEV, -1, D)
        gbig['w_out'][i] = gwout.reshape(N_DEV, -1, D)
        gbig['w_ff1'][i] = gw1t.reshape(N_DEV, -1, D)
        gbig['w_ff2'][i] = gw2.reshape(N_DEV, -1, D)
        gs['norm1_g'][i] = dg1[0]
        gs['norm2_g'][i] = dg2[0]
        gs['conv_a_w'][i] = dwa[:KA]
        gs['conv_a_b'][i] = dba[0]
        gs['ln_a_g'][i] = dlag[0]
        gs['ln_a_b'][i] = dlab[0]
        gs['ln_b_g'][i] = dlbg[0]
        gs['ln_b_b'][i] = dlbb[0]
        gs['w_spatial'][i] = dws
        gs['b_spatial'][i] = dbs
        gs['conv_c_w'][i] = dwc[:KC]
        gs['conv_c_b'][i] = dbc[0]
        gs['dt_bias'][i] = ddtb[0, :nh]
        gs['a_log'][i] = dalog[0, :nh]
        gs['d_skip'][i] = ddsk[0, :nh]
        gs['norm_c_g'][i] = dng[0]
        handle, token = _exchange_start("scatter_g%d_start" % i, [gbig[n][i] for n in _BIG], True)
        if pending is not None:
            adam_big(pending[0], _exchange_wait("scatter_g%d_wait" % pending[0], pending[1], token))
        pending = (i, handle, token)
    grad_x = dx.reshape(nb, S, D)

    names1 = _REPL + _CONVW
    g1 = _pack([jnp.stack(gs[n]) for n in names1], dfinal[0])
    small_handle, small_token = _exchange_start("gather_g_start", [g1], False)
    adam_big(pending[0], _exchange_wait("scatter_g%d_wait" % pending[0], pending[1], small_token))
    for n in _BIG:
        for kind in ("grad", "delta", "new_m", "new_v"):
            out[(kind, n)] = jnp.stack(out[(kind, n)])
    (parts1,) = _exchange_wait("gather_g_wait", small_handle, out[("new_v", "w_ff2")])
    gsum = _sum_parts("sum_small", parts1)
    shapes1 = [(depth,) + gs[n][0].shape for n in names1]
    glist, gfinal = _unpack(gsum, shapes1, D)
    gd = dict(zip(names1, glist))
    for n in _CONVW:
        cw_shard = p[n].shape[2]
        gd[n] = lax.dynamic_slice_in_dim(gd[n], me * cw_shard, cw_shard, axis=2)
    g2 = _pack([gd[n] for n in names1], gfinal)
    w2_, m2_, v2_ = (_pack([q[n] for n in names1], q['final_g']) for q in (p, m, v))
    res = _adam("adam_small", g2[None], w2_, m2_, v2_)
    shapes2 = [p[n].shape for n in names1]
    for kind, r in zip(("grad", "delta", "new_m", "new_v"), res):
        lst, fin = _unpack(r, shapes2, D)
        for n, arr in zip(names1, lst):
            out[(kind, n)] = arr
        out[(kind, 'final_g')] = fin

    flat = [loss, grad_x]
    for kind in ("grad", "delta", "new_m", "new_v"):
        flat += [out[(kind, n)] for n in _NAMES]
    return tuple(flat)


def kernel(x, norm1_g, w_in, conv_a_w, conv_a_b, ln_a_g, ln_a_b, ln_b_g, ln_b_b, w_spatial, b_spatial, conv_c_w, conv_c_b, dt_bias, a_log, d_skip, norm_c_g, w_out, norm2_g, w_ff1, w_ff2, final_g, loss_target, m_norm1_g, m_w_in, m_conv_a_w, m_conv_a_b, m_ln_a_g, m_ln_a_b, m_ln_b_g, m_ln_b_b, m_w_spatial, m_b_spatial, m_conv_c_w, m_conv_c_b, m_dt_bias, m_a_log, m_d_skip, m_norm_c_g, m_w_out, m_norm2_g, m_w_ff1, m_w_ff2, m_final_g, v_norm1_g, v_w_in, v_conv_a_w, v_conv_a_b, v_ln_a_g, v_ln_a_b, v_ln_b_g, v_ln_b_b, v_w_spatial, v_b_spatial, v_conv_c_w, v_conv_c_b, v_dt_bias, v_a_log, v_d_skip, v_norm_c_g, v_w_out, v_norm2_g, v_w_ff1, v_w_ff2, v_final_g):
    p = dict(zip(_NAMES, (norm1_g, w_in, conv_a_w, conv_a_b, ln_a_g, ln_a_b, ln_b_g, ln_b_b, w_spatial, b_spatial, conv_c_w,
                          conv_c_b, dt_bias, a_log, d_skip, norm_c_g, w_out, norm2_g, w_ff1, w_ff2, final_g)))
    m = dict(zip(_NAMES, (m_norm1_g, m_w_in, m_conv_a_w, m_conv_a_b, m_ln_a_g, m_ln_a_b, m_ln_b_g, m_ln_b_b, m_w_spatial,
                          m_b_spatial, m_conv_c_w, m_conv_c_b, m_dt_bias, m_a_log, m_d_skip, m_norm_c_g, m_w_out, m_norm2_g,
                          m_w_ff1, m_w_ff2, m_final_g)))
    v = dict(zip(_NAMES, (v_norm1_g, v_w_in, v_conv_a_w, v_conv_a_b, v_ln_a_g, v_ln_a_b, v_ln_b_g, v_ln_b_b, v_w_spatial,
                          v_b_spatial, v_conv_c_w, v_conv_c_b, v_dt_bias, v_a_log, v_d_skip, v_norm_c_g, v_w_out, v_norm2_g,
                          v_w_ff1, v_w_ff2, v_final_g)))
    return _step(p, m, v, x, loss_target)
```

```python
import functools
import math

import jax
import jax.numpy as jnp
from jax import lax
from jax.experimental import pallas as pl
from jax.experimental.pallas import tpu as pltpu

f32 = jnp.float32
bf16 = jnp.bfloat16
HI = lax.Precision.HIGHEST
EPS = 1e-5
HEAD = 64
CHUNK = 128
KA = 31
KC = 4
N_DEV = 8
VMEM_LIMIT = 56 * 1024 * 1024

ADAM_LR = 0.001
ADAM_B1 = 0.9
ADAM_B2 = 0.999
ADAM_EPS = 1e-08
ADAM_WD = 0.01
ADAM_STEP = 10


def _cparams(sem=None):
    return pltpu.CompilerParams(dimension_semantics=sem, vmem_limit_bytes=VMEM_LIMIT)


def _sds(shape, dtype):
    return jax.ShapeDtypeStruct(shape, dtype)


_DIMS = {"nn": ((1,), (0,)), "nt": ((1,), (1,)), "tn": ((0,), (0,))}


def _tile(n, cap):
    if n <= cap:
        return n
    for d in range(cap - cap % 128, 0, -128):
        if n % d == 0:
            return d
    raise ValueError((n, cap))


def _mm(name, a, b, form, out_dtypes, epilogue=None, extras=(), tm=1024, tn=512, tk=2048):
    if form == "tn":
        K, M = a.shape
    else:
        M, K = a.shape
    N = b.shape[0] if form == "nt" else b.shape[1]
    tm, tn, tk = _tile(M, tm), _tile(N, tn), _tile(K, tk)
    nk = K // tk
    ne, no = len(extras), len(out_dtypes)
    if epilogue is None:
        epilogue = lambda acc: (acc,)

    def body(a_ref, b_ref, *rest):
        extra_refs, out_refs = rest[:ne], rest[ne:ne + no]
        part = lax.dot_general(a_ref[...].astype(bf16), b_ref[...].astype(bf16),
                               (_DIMS[form], ((), ())), preferred_element_type=f32)

        def finish(acc):
            outs = epilogue(acc, *[e[...] for e in extra_refs])
            for o_ref, v in zip(out_refs, outs):
                o_ref[...] = v.astype(o_ref.dtype)

        if nk == 1:
            finish(part)
            return
        acc_ref = rest[ne + no]
        k = pl.program_id(2)

        @pl.when(k == 0)
        def _():
            acc_ref[...] = part

        @pl.when((k > 0) & (k < nk - 1))
        def _():
            acc_ref[...] += part

        @pl.when(k == nk - 1)
        def _():
            finish(acc_ref[...] + part)

    a_spec = pl.BlockSpec((tk, tm), lambda i, j, k: (k, i)) if form == "tn" else pl.BlockSpec((tm, tk), lambda i, j, k: (i, k))
    b_spec = pl.BlockSpec((tn, tk), lambda i, j, k: (j, k)) if form == "nt" else pl.BlockSpec((tk, tn), lambda i, j, k: (k, j))
    mn_spec = pl.BlockSpec((tm, tn), lambda i, j, k: (i, j))
    return pl.pallas_call(
        body, name=name, grid=(M // tm, N // tn, nk),
        in_specs=[a_spec, b_spec] + [mn_spec] * ne,
        out_specs=[mn_spec] * no,
        out_shape=[_sds((M, N), d) for d in out_dtypes],
        scratch_shapes=[pltpu.VMEM((tm, tn), f32)] if nk > 1 else [],
        compiler_params=_cparams(("parallel", "parallel", "arbitrary")),
    )(a, b, *extras)


def _ep_add(acc, r):
    return (acc + r,)


def _ep_relu2(acc):
    r = jnp.maximum(acc, 0.0)
    return acc, r * r


def _ep_drelu2(acc, f):
    return (acc * 2.0 * jnp.maximum(f, 0.0),)


def _rms(x, g):
    return x * lax.rsqrt(jnp.mean(x * x, axis=-1, keepdims=True) + EPS) * g


TT = 512


def _rms_fwd(x, g):
    T, D = x.shape

    def body(x_ref, g_ref, h_ref):
        h_ref[...] = _rms(x_ref[...], g_ref[...]).astype(bf16)

    return pl.pallas_call(
        body, name="rms_fwd", grid=(T // TT,),
        in_specs=[pl.BlockSpec((TT, D), lambda i: (i, 0)), pl.BlockSpec((1, D), lambda i: (0, 0))],
        out_specs=pl.BlockSpec((TT, D), lambda i: (i, 0)),
        out_shape=_sds((T, D), bf16), compiler_params=_cparams(("arbitrary",)),
    )(x, g.reshape(1, D))


def _rms_bwd(x, g, dh, dres):
    T, D = x.shape

    def body(x_ref, g_ref, dh_ref, dres_ref, dx_ref, dg_ref):
        _, vjp = jax.vjp(_rms, x_ref[...], g_ref[...])
        dx, dg = vjp(dh_ref[...])
        dx_ref[...] = dres_ref[...] + dx

        @pl.when(pl.program_id(0) == 0)
        def _():
            dg_ref[...] = jnp.zeros_like(dg_ref)

        dg_ref[0:1, :] += dg

    tile = pl.BlockSpec((TT, D), lambda i: (i, 0))
    return pl.pallas_call(
        body, name="rms_bwd", grid=(T // TT,),
        in_specs=[tile, pl.BlockSpec((1, D), lambda i: (0, 0)), tile, tile],
        out_specs=[tile, pl.BlockSpec((8, D), lambda i: (0, 0))],
        out_shape=[_sds((T, D), f32), _sds((8, D), f32)], compiler_params=_cparams(("arbitrary",)),
    )(x, g.reshape(1, D), dh, dres)


def _loss_head(x, g, tgt):
    T, D = x.shape

    def f(xv, gv, tv):
        e = _rms(xv, gv) - tv
        return 0.5 * jnp.sum(jnp.sum(e * e, axis=-1, keepdims=True) * (1.0 / D), axis=0, keepdims=True)

    def body(x_ref, g_ref, t_ref, loss_ref, dx_ref, dg_ref):
        tv = t_ref[...]
        l, vjp = jax.vjp(lambda xv, gv: f(xv, gv, tv), x_ref[...], g_ref[...])
        dx, dg = vjp(jnp.ones((1, 1), f32))
        dx_ref[...] = dx

        @pl.when(pl.program_id(0) == 0)
        def _():
            dg_ref[...] = jnp.zeros_like(dg_ref)
            loss_ref[...] = jnp.zeros_like(loss_ref)

        dg_ref[0:1, :] += dg
        loss_ref[...] += jnp.broadcast_to(l, loss_ref.shape)

    tile = pl.BlockSpec((TT, D), lambda i: (i, 0))
    return pl.pallas_call(
        body, name="loss_head", grid=(T // TT,),
        in_specs=[tile, pl.BlockSpec((1, D), lambda i: (0, 0)), tile],
        out_specs=[pl.BlockSpec((8, 128), lambda i: (0, 0)), tile, pl.BlockSpec((8, D), lambda i: (0, 0))],
        out_shape=[_sds((8, 128), f32), _sds((T, D), f32), _sds((8, D), f32)],
        compiler_params=_cparams(("arbitrary",)),
    )(x, g.reshape(1, D), tgt)


TB = 256


def _glu(a_val, a_gate):
    return a_val * jax.nn.sigmoid(a_gate)


def _ln_silu(v, g, b):
    mu = jnp.mean(v, axis=-1, keepdims=True)
    vc = v - mu
    var = jnp.mean(vc * vc, axis=-1, keepdims=True)
    return jax.nn.silu(vc * lax.rsqrt(var + EPS) * g + b)


def _conv_geom(kw):
    halo = 32 if kw > 9 else 8
    return halo, halo - (kw - 1)


def _conv_taps(hp_ref, w_ref, b_ref, acc_ref, kw, off, width):
    for c in range(width // 128):
        ls = pl.ds(c * 128, 128)
        acc = jnp.broadcast_to(b_ref[:, ls], (TB, 128))
        for k in range(kw):
            acc = acc + w_ref[k:k + 1, ls] * hp_ref[pl.ds(off + k, TB), ls]
        acc_ref[:, ls] = acc


def _conv_fwd(name, src, col_block, w, b, kw, conformer, n_seq, ln_g=None, ln_b=None):
    T = src.shape[0]
    cout = w.shape[1]
    cin = 2 * cout if conformer else cout
    halo, off = _conv_geom(kw)
    nblk = T // n_seq // TB
    hb = TB // halo

    def body(cur_ref, halo_ref, w_ref, b_ref, *rest):
        if conformer:
            g_ref, lb_ref, out_ref, hp_ref, acc_ref = rest
        else:
            out_ref, hp_ref, acc_ref = rest
        i = pl.program_id(1)
        first = (i == 0)
        if conformer:
            hp_ref[pl.ds(halo, TB), :] = _glu(cur_ref[:, 0:cout], cur_ref[:, cout:cin])
            hh = _glu(halo_ref[:, 0:cout], halo_ref[:, cout:cin])
        else:
            hp_ref[pl.ds(halo, TB), :] = cur_ref[...]
            hh = halo_ref[...]
        hp_ref[pl.ds(0, halo), :] = jnp.where(first, 0.0, hh)
        _conv_taps(hp_ref, w_ref, b_ref, acc_ref, kw, off, cout)
        if conformer:
            for h in range(cout // HEAD):
                ls = pl.ds(h * HEAD, HEAD)
                out_ref[:, ls] = _ln_silu(acc_ref[:, ls], g_ref[:, ls], lb_ref[:, ls]).astype(out_ref.dtype)
        else:
            out_ref[...] = jax.nn.silu(acc_ref[...]).astype(out_ref.dtype)

    row = pl.BlockSpec((1, cout), lambda s, i: (0, 0))
    in_specs = [pl.BlockSpec((TB, cin), lambda s, i: (s * nblk + i, col_block)),
                pl.BlockSpec((halo, cin), lambda s, i: (jnp.maximum((s * nblk + i) * hb - 1, 0), col_block)),
                pl.BlockSpec((w.shape[0], cout), lambda s, i: (0, 0)), row]
    args = [src, src, w, b.reshape(1, cout)]
    if conformer:
        in_specs += [row, row]
        args += [ln_g.reshape(1, cout), ln_b.reshape(1, cout)]
    out_dtype = bf16 if conformer else f32
    return pl.pallas_call(
        body, name=name, grid=(n_seq, nblk), in_specs=in_specs,
        out_specs=pl.BlockSpec((TB, cout), lambda s, i: (s * nblk + i, 0)),
        out_shape=_sds((T, cout), out_dtype),
        scratch_shapes=[pltpu.VMEM((halo + TB, cout), f32), pltpu.VMEM((TB, cout), f32)],
        compiler_params=_cparams(("arbitrary", "arbitrary")),
    )(*args)


def _conv_bwd(name, src, col_block, w, b, dy, dy_col_block, kw, conformer, n_seq, ln_g=None, ln_b=None):
    T = src.shape[0]
    cout = w.shape[1]
    wrows = w.shape[0]
    cin = 2 * cout if conformer else cout
    halo, off = _conv_geom(kw)
    nblk = T // n_seq // TB
    hb = TB // halo

    def body(cur_ref, halo_ref, w_ref, b_ref, dy_ref, *rest):
        if conformer:
            g_ref, lb_ref, dsrc_ref, dw_ref, db_ref, dg_ref, dlb_ref, hp_ref, acc_ref, dz_ref, dhp_ref, carry_ref = rest
        else:
            dsrc_ref, dw_ref, db_ref, hp_ref, acc_ref, dz_ref, dhp_ref, carry_ref = rest
        s, ii = pl.program_id(0), pl.program_id(1)
        i = nblk - 1 - ii
        first = (i == 0)

        @pl.when((s == 0) & (ii == 0))
        def _():
            dw_ref[...] = jnp.zeros_like(dw_ref)
            db_ref[...] = jnp.zeros_like(db_ref)
            if conformer:
                dg_ref[...] = jnp.zeros_like(dg_ref)
                dlb_ref[...] = jnp.zeros_like(dlb_ref)

        @pl.when(ii == 0)
        def _():
            carry_ref[...] = jnp.zeros_like(carry_ref)
            dz_ref[pl.ds(0, halo), :] = jnp.zeros((halo, cout), f32)
            dz_ref[pl.ds(halo + TB, halo), :] = jnp.zeros((halo, cout), f32)

        if conformer:
            hp_ref[pl.ds(halo, TB), :] = _glu(cur_ref[:, 0:cout], cur_ref[:, cout:cin])
            hh = _glu(halo_ref[:, 0:cout], halo_ref[:, cout:cin])
        else:
            hp_ref[pl.ds(halo, TB), :] = cur_ref[...]
            hh = halo_ref[...]
        hp_ref[pl.ds(0, halo), :] = jnp.where(first, 0.0, hh)
        _conv_taps(hp_ref, w_ref, b_ref, acc_ref, kw, off, cout)

        if conformer:
            for h in range(cout // HEAD):
                ls = pl.ds(h * HEAD, HEAD)
                _, vjp = jax.vjp(_ln_silu, acc_ref[:, ls], g_ref[:, ls], lb_ref[:, ls])
                da, dg, dlb = vjp(dy_ref[:, ls].astype(f32))
                dz_ref[pl.ds(halo, TB), ls] = da
                dg_ref[0:1, ls] += dg
                dlb_ref[0:1, ls] += dlb
        else:
            _, vjp = jax.vjp(jax.nn.silu, acc_ref[...])
            dz_ref[pl.ds(halo, TB), :] = vjp(dy_ref[...].astype(f32))[0]

        for c in range(cout // 128):
            ls = pl.ds(c * 128, 128)
            dacc = dz_ref[pl.ds(halo, TB), ls]
            db_ref[0:1, ls] += jnp.sum(dacc, axis=0, keepdims=True)
            dhp = jnp.zeros((halo + TB, 128), f32)
            for k in range(kw):
                dw_ref[k:k + 1, ls] += jnp.sum(dacc * hp_ref[pl.ds(off + k, TB), ls], axis=0, keepdims=True)
                dhp = dhp + w_ref[k:k + 1, ls] * dz_ref[pl.ds(kw - 1 - k, halo + TB), ls]
            dhp_ref[:, ls] = dhp
        dhp_ref[pl.ds(TB, halo), :] += carry_ref[...]
        carry_ref[...] = dhp_ref[pl.ds(0, halo), :]
        dcur = dhp_ref[pl.ds(halo, TB), :]
        if conformer:
            _, vjp = jax.vjp(_glu, cur_ref[:, 0:cout], cur_ref[:, cout:cin])
            dval, dgate = vjp(dcur)
            dsrc_ref[:, 0:cout] = dval.astype(dsrc_ref.dtype)
            dsrc_ref[:, cout:cin] = dgate.astype(dsrc_ref.dtype)
        else:
            dsrc_ref[...] = dcur.astype(dsrc_ref.dtype)

    def blk(s, ii):
        return s * nblk + (nblk - 1 - ii)

    row = pl.BlockSpec((1, cout), lambda s, ii: (0, 0))
    acc8 = pl.BlockSpec((8, cout), lambda s, ii: (0, 0))
    in_specs = [pl.BlockSpec((TB, cin), lambda s, ii: (blk(s, ii), col_block)),
                pl.BlockSpec((halo, cin), lambda s, ii: (jnp.maximum(blk(s, ii) * hb - 1, 0), col_block)),
                pl.BlockSpec((wrows, cout), lambda s, ii: (0, 0)), row,
                pl.BlockSpec((TB, cout), lambda s, ii: (blk(s, ii), dy_col_block))]
    args = [src, src, w, b.reshape(1, cout), dy]
    out_specs = [pl.BlockSpec((TB, cin), lambda s, ii: (blk(s, ii), 0)),
                 pl.BlockSpec((wrows, cout), lambda s, ii: (0, 0)), acc8]
    out_shape = [_sds((T, cin), bf16), _sds((wrows, cout), f32), _sds((8, cout), f32)]
    if conformer:
        in_specs += [row, row]
        args += [ln_g.reshape(1, cout), ln_b.reshape(1, cout)]
        out_specs += [acc8, acc8]
        out_shape += [_sds((8, cout), f32), _sds((8, cout), f32)]
    return pl.pallas_call(
        body, name=name, grid=(n_seq, nblk), in_specs=in_specs, out_specs=out_specs, out_shape=out_shape,
        scratch_shapes=[pltpu.VMEM((halo + TB, cout), f32), pltpu.VMEM((TB, cout), f32),
                        pltpu.VMEM((halo + TB + halo, cout), f32), pltpu.VMEM((halo + TB, cout), f32),
                        pltpu.VMEM((halo, cout), f32)],
        compiler_params=_cparams(("arbitrary", "arbitrary")),
    )(*args)


def _gelu(x):
    return 0.5 * x * (1.0 + lax.erf(x * (1.0 / math.sqrt(2.0))))


def _tril_mask(n):
    r = lax.broadcasted_iota(jnp.int32, (n, n), 0)
    c = lax.broadcasted_iota(jnp.int32, (n, n), 1)
    return r >= c


def _row_select(rows, h, width):
    r = lax.broadcasted_iota(jnp.int32, (rows, width), 0)
    return (r == h).astype(f32)


def _gmlp_head(h, bu, bv, g, b, w_h, bs):
    u = _gelu(bu)
    v = _gelu(bv)
    mu = jnp.mean(v, axis=-1, keepdims=True)
    vc = v - mu
    var = jnp.mean(vc * vc, axis=-1, keepdims=True)
    vn = vc * lax.rsqrt(var + EPS) * g + b
    wm = jnp.where(_tril_mask(CHUNK), w_h, 0.0)
    mix = jnp.dot(wm.astype(bf16), vn.astype(bf16), preferred_element_type=f32)
    bias = lax.dot_general(bs, _row_select(bs.shape[0], h, HEAD), (((0,), (0,)), ((), ())),
                           precision=HI, preferred_element_type=f32)
    return u * (mix + bias)


def _gmlp_fwd(proj, col_block, ln_g, ln_b, w_s, b_s):
    T = proj.shape[0]
    nh = w_s.shape[0]
    width = nh * HEAD

    def body(p_ref, g_ref, b_ref, w_ref, bs_ref, out_ref):
        bs = bs_ref[...]
        for h in range(nh):
            ls = pl.ds(h * HEAD, HEAD)
            lv = pl.ds(width + h * HEAD, HEAD)
            out_ref[:, ls] = _gmlp_head(h, p_ref[:, ls], p_ref[:, lv], g_ref[:, ls], b_ref[:, ls], w_ref[h], bs).astype(out_ref.dtype)

    row = pl.BlockSpec((1, width), lambda i: (0, 0))
    return pl.pallas_call(
        body, name="gmlp_fwd", grid=(T // CHUNK,),
        in_specs=[pl.BlockSpec((CHUNK, 2 * width), lambda i: (i, col_block)), row, row,
                  pl.BlockSpec((nh, CHUNK, CHUNK), lambda i: (0, 0, 0)), pl.BlockSpec((nh, CHUNK), lambda i: (0, 0))],
        out_specs=pl.BlockSpec((CHUNK, width), lambda i: (i, 0)),
        out_shape=_sds((T, width), bf16), compiler_params=_cparams(("arbitrary",)),
    )(proj, ln_g.reshape(1, width), ln_b.reshape(1, width), w_s, b_s)


def _gmlp_bwd(proj, col_block, ln_g, ln_b, w_s, b_s, dy, dy_col_block):
    T = proj.shape[0]
    nh = w_s.shape[0]
    width = nh * HEAD

    def body(p_ref, g_ref, b_ref, w_ref, bs_ref, dy_ref, dp_ref, dg_ref, db_ref, dw_ref, dbs_ref):
        @pl.when(pl.program_id(0) == 0)
        def _():
            dg_ref[...] = jnp.zeros_like(dg_ref)
            db_ref[...] = jnp.zeros_like(db_ref)
            dw_ref[...] = jnp.zeros_like(dw_ref)
            dbs_ref[...] = jnp.zeros_like(dbs_ref)

        bs = bs_ref[...]
        for h in range(nh):
            ls = pl.ds(h * HEAD, HEAD)
            lv = pl.ds(width + h * HEAD, HEAD)
            _, vjp = jax.vjp(functools.partial(_gmlp_head, h), p_ref[:, ls], p_ref[:, lv], g_ref[:, ls], b_ref[:, ls], w_ref[h], bs)
            dbu, dbv, dg, db, dw, dbs = vjp(dy_ref[:, ls].astype(f32))
            dp_ref[:, ls] = dbu.astype(dp_ref.dtype)
            dp_ref[:, lv] = dbv.astype(dp_ref.dtype)
            dg_ref[0:1, ls] += dg
            db_ref[0:1, ls] += db
            dw_ref[h] += dw
            dbs_ref[...] += dbs

    row = pl.BlockSpec((1, width), lambda i: (0, 0))
    acc8 = pl.BlockSpec((8, width), lambda i: (0, 0))
    wspec = pl.BlockSpec((nh, CHUNK, CHUNK), lambda i: (0, 0, 0))
    bspec = pl.BlockSpec((nh, CHUNK), lambda i: (0, 0))
    return pl.pallas_call(
        body, name="gmlp_bwd", grid=(T // CHUNK,),
        in_specs=[pl.BlockSpec((CHUNK, 2 * width), lambda i: (i, col_block)), row, row, wspec, bspec,
                  pl.BlockSpec((CHUNK, width), lambda i: (i, dy_col_block))],
        out_specs=[pl.BlockSpec((CHUNK, 2 * width), lambda i: (i, 0)), acc8, acc8, wspec, bspec],
        out_shape=[_sds((T, 2 * width), bf16), _sds((8, width), f32), _sds((8, width), f32),
                   _sds((nh, CHUNK, CHUNK), f32), _sds((nh, CHUNK), f32)],
        compiler_params=_cparams(("arbitrary",)),
    )(proj, ln_g.reshape(1, width), ln_b.reshape(1, width), w_s, b_s, dy)


def _sel_col(x, h):
    lane = lax.broadcasted_iota(jnp.int32, x.shape, 1)
    return jnp.sum(jnp.where(lane == h, x, 0.0), axis=1, keepdims=True)


def _sel_row(x, h):
    sub = lax.broadcasted_iota(jnp.int32, x.shape, 0)
    return jnp.sum(jnp.where(sub == h, x, 0.0), axis=0, keepdims=True)


PAIR = 2 * HEAD


def _ssd_chunk(nh, ngrp, xs_l, z_l, b_l, c_l, dtraw, dtb, alog, dskip, ng_l, prev_l):
    hg = nh // ngrp
    tril = _tril_mask(CHUNK)
    tl = tril.astype(f32)
    lo = lax.broadcasted_iota(jnp.int32, (CHUNK, PAIR), 1) < HEAD
    lo_row = lo[0:1, :]
    dt = jax.nn.softplus(dtraw + dtb)
    a = dt * (-jnp.exp(alog))
    cs = jnp.dot(tl, a, precision=HI, preferred_element_type=f32)
    cst = lax.dot_general(a, tl, (((0,), (1,)), ((), ())), precision=HI, preferred_element_type=f32)
    cb_l = [lax.dot_general(c_l[g].astype(bf16), b_l[g].astype(bf16), (((1,), (1,)), ((), ())),
                            preferred_element_type=f32) for g in range(ngrp)]
    yz_l, new_prev = [], []
    for q in range(nh // 2):
        g = (2 * q) // hg
        cols = []
        for h in (2 * q, 2 * q + 1):
            cs_h = _sel_col(cs, h)
            tot = _sel_row(cs_h, CHUNK - 1)
            seg = jnp.where(tril, cs_h - _sel_row(cst, h), 0.0)
            lmat = jnp.where(tril, jnp.exp(seg), 0.0)
            cols.append((_sel_col(dt, h), cs_h, tot, lmat, _sel_col(dskip, h)))
        (dt_a, cs_a, tot_a, l_a, dsk_a), (dt_b, cs_b, tot_b, l_b, dsk_b) = cols
        xs = xs_l[q]
        x = xs * jnp.where(lo, dt_a, dt_b)
        xb = x.astype(bf16)
        ydiag = jnp.where(lo, jnp.dot((cb_l[g] * l_a).astype(bf16), xb, preferred_element_type=f32),
                          jnp.dot((cb_l[g] * l_b).astype(bf16), xb, preferred_element_type=f32))
        yoff = (jnp.dot(c_l[g].astype(bf16), prev_l[q].astype(bf16), preferred_element_type=f32)
                * jnp.where(lo, jnp.exp(cs_a), jnp.exp(cs_b)))
        xdec = x * jnp.where(lo, jnp.exp(tot_a - cs_a), jnp.exp(tot_b - cs_b))
        st = lax.dot_general(b_l[g].astype(bf16), xdec.astype(bf16), (((0,), (0,)), ((), ())),
                             preferred_element_type=f32)
        new_prev.append(prev_l[q] * jnp.where(lo_row, jnp.exp(tot_a), jnp.exp(tot_b)) + st)
        y = ydiag + yoff + jnp.where(lo_row, dsk_a, dsk_b) * xs
        yz_l.append(y * jax.nn.silu(z_l[q]))
    out = [None] * (nh // 2)
    qg = hg // 2
    for g in range(ngrp):
        ssq = sum(jnp.sum(yz_l[q] * yz_l[q], axis=-1, keepdims=True) for q in range(g * qg, (g + 1) * qg))
        r = lax.rsqrt(ssq * (1.0 / (hg * HEAD)) + EPS)
        for q in range(g * qg, (g + 1) * qg):
            out[q] = yz_l[q] * r * ng_l[q]
    return out, new_prev


def _ssd_read(nh, ngrp, nst, xbc_ref, z_ref, ng_ref, st_ref):
    cw = nh * HEAD
    xs_l = [xbc_ref[:, pl.ds(q * PAIR, PAIR)] for q in range(nh // 2)]
    b_l = [xbc_ref[:, pl.ds(cw + g * nst, nst)] for g in range(ngrp)]
    c_l = [xbc_ref[:, pl.ds(cw + ngrp * nst + g * nst, nst)] for g in range(ngrp)]
    z_l = [z_ref[:, pl.ds(q * PAIR, PAIR)] for q in range(nh // 2)]
    ng_l = [ng_ref[:, pl.ds(q * PAIR, PAIR)] for q in range(nh // 2)]
    prev_l = [st_ref[:, pl.ds(q * PAIR, PAIR)] for q in range(nh // 2)]
    return xs_l, z_l, b_l, c_l, ng_l, prev_l


def _ssd_fwd(xbc, proj, z_col_block, pdt, dtb, alog, dskip, ng, nh, ngrp, nst, n_seq):
    T = xbc.shape[0]
    cw = nh * HEAD
    nchunk = T // n_seq // CHUNK
    assert nst == CHUNK

    def body(xbc_ref, z_ref, dt_ref, dtb_ref, alog_ref, dskip_ref, ng_ref, y_ref, sin_ref, st_ref):
        @pl.when(pl.program_id(1) == 0)
        def _():
            st_ref[...] = jnp.zeros_like(st_ref)

        sin_ref[...] = st_ref[...]
        xs_l, z_l, b_l, c_l, ng_l, prev_l = _ssd_read(nh, ngrp, nst, xbc_ref, z_ref, ng_ref, st_ref)
        y_l, new_prev = _ssd_chunk(nh, ngrp, xs_l, z_l, b_l, c_l, dt_ref[...], dtb_ref[...], alog_ref[...],
                                   dskip_ref[...], ng_l, prev_l)
        for q in range(nh // 2):
            ls = pl.ds(q * PAIR, PAIR)
            y_ref[:, ls] = y_l[q].astype(y_ref.dtype)
            st_ref[:, ls] = new_prev[q]

    def blk(s, c):
        return s * nchunk + c

    prow = pl.BlockSpec((1, 128), lambda s, c: (0, 0))
    return pl.pallas_call(
        body, name="ssd_fwd", grid=(n_seq, nchunk),
        in_specs=[pl.BlockSpec((CHUNK, xbc.shape[1]), lambda s, c: (blk(s, c), 0)),
                  pl.BlockSpec((CHUNK, cw), lambda s, c: (blk(s, c), z_col_block)),
                  pl.BlockSpec((CHUNK, 128), lambda s, c: (blk(s, c), 0)),
                  prow, prow, prow, pl.BlockSpec((1, cw), lambda s, c: (0, 0))],
        out_specs=[pl.BlockSpec((CHUNK, cw), lambda s, c: (blk(s, c), 0)),
                   pl.BlockSpec((nst, cw), lambda s, c: (blk(s, c), 0))],
        out_shape=[_sds((T, cw), bf16), _sds((T, cw), f32)],
        scratch_shapes=[pltpu.VMEM((nst, cw), f32)],
        compiler_params=_cparams(("arbitrary", "arbitrary")),
    )(xbc, proj, pdt, dtb, alog, dskip, ng.reshape(1, cw))


def _ssd_bwd(xbc, proj, z_col_block, pdt, dtb, alog, dskip, ng, sin, dy, dy_col_block, nh, ngrp, nst, n_seq):
    T, xw = xbc.shape
    cw = nh * HEAD
    nchunk = T // n_seq // CHUNK

    def body(xbc_ref, z_ref, dt_ref, dtb_ref, alog_ref, dskip_ref, ng_ref, sin_ref, dy_ref,
             dxbc_ref, dz_ref, ddt_ref, ddtb_ref, dalog_ref, ddskip_ref, dng_ref, dst_ref):
        s, cc = pl.program_id(0), pl.program_id(1)

        @pl.when((s == 0) & (cc == 0))
        def _():
            ddtb_ref[...] = jnp.zeros_like(ddtb_ref)
            dalog_ref[...] = jnp.zeros_like(dalog_ref)
            ddskip_ref[...] = jnp.zeros_like(ddskip_ref)
            dng_ref[...] = jnp.zeros_like(dng_ref)

        @pl.when(cc == 0)
        def _():
            dst_ref[...] = jnp.zeros_like(dst_ref)

        xs_l, z_l, b_l, c_l, ng_l, prev_l = _ssd_read(nh, ngrp, nst, xbc_ref, z_ref, ng_ref, sin_ref)
        _, vjp = jax.vjp(functools.partial(_ssd_chunk, nh, ngrp), xs_l, z_l, b_l, c_l, dt_ref[...], dtb_ref[...],
                         alog_ref[...], dskip_ref[...], ng_l, prev_l)
        dy_l = [dy_ref[:, pl.ds(q * PAIR, PAIR)].astype(f32) for q in range(nh // 2)]
        dst_l = [dst_ref[:, pl.ds(q * PAIR, PAIR)] for q in range(nh // 2)]
        dxs_l, dz_l, db_l, dc_l, ddt, ddtb, dalog, ddskip, dng_l, dprev_l = vjp((dy_l, dst_l))
        for q in range(nh // 2):
            ls = pl.ds(q * PAIR, PAIR)
            dxbc_ref[:, ls] = dxs_l[q]
            dz_ref[:, ls] = dz_l[q].astype(dz_ref.dtype)
            dng_ref[0:1, ls] += dng_l[q]
            dst_ref[:, ls] = dprev_l[q]
        for g in range(ngrp):
            dxbc_ref[:, pl.ds(cw + g * nst, nst)] = db_l[g]
            dxbc_ref[:, pl.ds(cw + ngrp * nst + g * nst, nst)] = dc_l[g]
        ddt_ref[...] = ddt.astype(ddt_ref.dtype)
        ddtb_ref[0:1, :] += ddtb
        dalog_ref[0:1, :] += dalog
        ddskip_ref[0:1, :] += ddskip

    def blk(s, cc):
        return s * nchunk + (nchunk - 1 - cc)

    prow = pl.BlockSpec((1, 128), lambda s, c: (0, 0))
    pacc = pl.BlockSpec((8, 128), lambda s, c: (0, 0))
    return pl.pallas_call(
        body, name="ssd_bwd", grid=(n_seq, nchunk),
        in_specs=[pl.BlockSpec((CHUNK, xw), lambda s, c: (blk(s, c), 0)),
                  pl.BlockSpec((CHUNK, cw), lambda s, c: (blk(s, c), z_col_block)),
                  pl.BlockSpec((CHUNK, 128), lambda s, c: (blk(s, c), 0)),
                  prow, prow, prow, pl.BlockSpec((1, cw), lambda s, c: (0, 0)),
                  pl.BlockSpec((nst, cw), lambda s, c: (blk(s, c), 0)),
                  pl.BlockSpec((CHUNK, cw), lambda s, c: (blk(s, c), dy_col_block))],
        out_specs=[pl.BlockSpec((CHUNK, xw), lambda s, c: (blk(s, c), 0)),
                   pl.BlockSpec((CHUNK, cw), lambda s, c: (blk(s, c), 0)),
                   pl.BlockSpec((CHUNK, 128), lambda s, c: (blk(s, c), 0)),
                   pacc, pacc, pacc, pl.BlockSpec((8, cw), lambda s, c: (0, 0))],
        out_shape=[_sds((T, xw), f32), _sds((T, cw), bf16), _sds((T, 128), bf16),
                   _sds((8, 128), f32), _sds((8, 128), f32), _sds((8, 128), f32), _sds((8, cw), f32)],
        scratch_shapes=[pltpu.VMEM((nst, cw), f32)],
        compiler_params=_cparams(("arbitrary", "arbitrary")),
    )(xbc, proj, pdt, dtb, alog, dskip, ng.reshape(1, cw), sin, dy)


_HBM = pl.BlockSpec(memory_space=pltpu.HBM)
_SEM = pl.BlockSpec(memory_space=pltpu.SEMAPHORE)
_EFFECT = pltpu.SideEffectType.DATAFLOW_SIDE_EFFECTING


def _split_copies(n, scatter, src_refs, land_refs, send_sems, recv_sems):
    npeer = N_DEV - 1
    x, y, c = lax.axis_index("x"), lax.axis_index("y"), lax.axis_index("c")
    me = 4 * x + 2 * y + c
    copies = []
    for i in range(n):
        for k in range(1, N_DEV):
            px = 1 - x if k & 4 else x
            py = 1 - y if k & 2 else y
            pc = 1 - c if k & 1 else c
            src = src_refs[i].at[4 * px + 2 * py + pc] if scatter else src_refs[i]
            copies.append(pltpu.make_async_remote_copy(
                src_ref=src, dst_ref=land_refs[i].at[me],
                send_sem=send_sems.at[i * npeer + k - 1], recv_sem=recv_sems.at[i * npeer + k - 1],
                device_id=(px, py, pc), device_id_type=pl.DeviceIdType.MESH))
    return copies


def _exchange_start(name, arrs, scatter):
    n = len(arrs)
    nsem = n * (N_DEV - 1)
    me = 4 * lax.axis_index("x") + 2 * lax.axis_index("y") + lax.axis_index("c")
    lands = []
    for a in arrs:
        own = lax.dynamic_index_in_dim(a, me, 0, keepdims=True) if scatter else a[None]
        full = lax.empty(a.shape if scatter else (N_DEV,) + a.shape, a.dtype)
        lands.append(lax.dynamic_update_slice(full, own, (me,) + (0,) * (full.ndim - 1)))

    def body(*refs):
        src_refs, land_refs = refs[:n], refs[n:2 * n]
        send_sems, recv_sems = refs[2 * n], refs[2 * n + 1]
        token = refs[-1]
        for cp in _split_copies(n, scatter, src_refs, land_refs, send_sems, recv_sems):
            cp.start()
        token[...] = jnp.zeros_like(token)

    res = pl.pallas_call(
        body, name=name,
        out_shape=(pltpu.SemaphoreType.DMA((nsem,)), pltpu.SemaphoreType.DMA((nsem,)),
                   *[pltpu.HBM(a.shape, a.dtype) for a in arrs], *[pltpu.HBM(l.shape, l.dtype) for l in lands],
                   _sds((8, 128), f32)),
        in_specs=[_HBM] * (2 * n),
        out_specs=(_SEM, _SEM, *[_HBM] * (2 * n), pl.BlockSpec(memory_space=pltpu.VMEM)),
        input_output_aliases={j: 2 + j for j in range(2 * n)},
        compiler_params=pltpu.CompilerParams(has_side_effects=_EFFECT),
    )(*[pltpu.with_memory_space_constraint(a, pltpu.HBM) for a in arrs],
      *[pltpu.with_memory_space_constraint(l, pltpu.HBM) for l in lands])
    return (n, scatter, res[0], res[1], res[2:2 + n], res[2 + n:2 + 2 * n]), res[-1]


def _exchange_wait(name, handle, after):
    n, scatter, send_sems, recv_sems, srcs, lands = handle

    def body(*refs):
        src_refs, land_refs = refs[:n], refs[n:2 * n]
        for cp in _split_copies(n, scatter, src_refs, land_refs, refs[2 * n], refs[2 * n + 1]):
            cp.wait_send()
            cp.wait_recv()

    res = pl.pallas_call(
        body, name=name,
        out_shape=[pltpu.HBM(a.shape, a.dtype) for a in (*srcs, *lands)],
        in_specs=[_HBM] * (2 * n) + [_SEM, _SEM, pl.BlockSpec(memory_space=pl.ANY)],
        out_specs=[_HBM] * (2 * n),
        input_output_aliases={j: j for j in range(2 * n)},
        compiler_params=pltpu.CompilerParams(has_side_effects=_EFFECT),
    )(*srcs, *lands, send_sems, recv_sems, after)
    return res[n:]


def _adam_tiles(R, C):
    if R % 256 == 0:
        return (256, C), (R // 256, 1)
    assert C % 128 == 0
    return (R, 128), (1, C // 128)


def _adam(name, parts, w, m, v):
    P, R, C = parts.shape
    (tr, tc), (gr, gc) = _adam_tiles(R, C)
    c1 = 1.0 / (1.0 - ADAM_B1 ** ADAM_STEP)
    c2 = 1.0 / (1.0 - ADAM_B2 ** ADAM_STEP)

    def body(p_ref, w_ref, m_ref, v_ref, g_ref, d_ref, nm_ref, nv_ref):
        g = p_ref[0].astype(f32)
        for s in range(1, P):
            g = g + p_ref[s].astype(f32)
        nm = ADAM_B1 * m_ref[...] + (1.0 - ADAM_B1) * g
        nv = ADAM_B2 * v_ref[...] + (1.0 - ADAM_B2) * (g * g)
        g_ref[...] = g
        nm_ref[...] = nm
        nv_ref[...] = nv
        d_ref[...] = -ADAM_LR * ((nm * c1) / (jnp.sqrt(nv * c2) + ADAM_EPS) + ADAM_WD * w_ref[...])

    tile = pl.BlockSpec((tr, tc), lambda i, j: (i, j))
    return pl.pallas_call(
        body, name=name, grid=(gr, gc),
        in_specs=[pl.BlockSpec((P, tr, tc), lambda i, j: (0, i, j)), tile, tile, tile],
        out_specs=[tile] * 4, out_shape=[_sds((R, C), f32)] * 4,
        compiler_params=_cparams(("arbitrary", "arbitrary")),
    )(parts, w, m, v)


def _sum_parts(name, parts):
    P, R, C = parts.shape
    tr = 256 if R % 256 == 0 else R

    def body(p_ref, o_ref):
        g = p_ref[0]
        for s in range(1, P):
            g = g + p_ref[s]
        o_ref[...] = g

    return pl.pallas_call(
        body, name=name, grid=(R // tr,),
        in_specs=[pl.BlockSpec((P, tr, C), lambda i: (0, i, 0))], out_specs=pl.BlockSpec((tr, C), lambda i: (i, 0)),
        out_shape=_sds((R, C), f32), compiler_params=_cparams(("arbitrary",)),
    )(parts)


def _pad_to(a, n, axis):
    if a.shape[axis] == n:
        return a
    cfg = [(0, 0)] * a.ndim
    cfg[axis] = (0, n - a.shape[axis])
    return jnp.pad(a, cfg)


def _pack(arrs):
    flat = [_pad_to(a.reshape(-1), -(-a.size // 128) * 128, 0) for a in arrs]
    rows = jnp.concatenate(flat).reshape(-1, 128)
    return _pad_to(rows, -(-rows.shape[0] // 256) * 256, 0)


def _unpack(slab, shapes):
    flat = slab.reshape(-1)
    out, o = [], 0
    for s in shapes:
        n = math.prod(s)
        out.append(flat[o:o + n].reshape(s))
        o += -(-n // 128) * 128
    return out


def _tie(xs, dep):
    xs, _ = lax.optimization_barrier((xs, dep))
    return xs


_NAMES = ['norm1_g', 'w_in', 'conv_a_w', 'conv_a_b', 'ln_a_g', 'ln_a_b', 'ln_b_g', 'ln_b_b', 'w_spatial', 'b_spatial',
          'conv_c_w', 'conv_c_b', 'dt_bias', 'a_log', 'd_skip', 'norm_c_g', 'w_out', 'norm2_g', 'w_ff1', 'w_ff2', 'final_g']
_REPL = ['norm1_g', 'conv_a_b', 'ln_a_g', 'ln_a_b', 'ln_b_g', 'ln_b_b', 'w_spatial', 'b_spatial', 'conv_c_b',
         'dt_bias', 'a_log', 'd_skip', 'norm_c_g', 'norm2_g']
_CONVW = ['conv_a_w', 'conv_c_w']
_BIG = ['w_in', 'w_out', 'w_ff1', 'w_ff2']
_BIG_T = {'w_in': True, 'w_out': False, 'w_ff1': True, 'w_ff2': False}


def _row128(v):
    return _pad_to(v.reshape(1, -1), 128, 1)


def _step(p, m, v, x, loss_target):
    nb, S, D = x.shape
    T = nb * S
    depth = p['norm1_g'].shape[0]
    a_w = p['conv_a_b'].shape[1]
    b_w = p['ln_b_g'].shape[1]
    nh = p['dt_bias'].shape[1]
    c_w = p['norm_c_g'].shape[1]
    xw = p['conv_c_b'].shape[1]
    ngrp = 2
    nst = (xw - c_w) // (2 * ngrp)
    d_in = p['w_in'].shape[2] * N_DEV
    main = d_in - nh
    assert main == 2 * a_w + 2 * b_w + c_w + xw and 2 * a_w == 2 * b_w == c_w and xw % c_w == c_w // 2
    me = 4 * lax.axis_index("x") + 2 * lax.axis_index("y") + lax.axis_index("c")

    x2 = x.reshape(T, D)
    tgt = loss_target.reshape(T, D)

    def shards(i):
        return [p['w_in'][i].T.astype(bf16), p['w_out'][i].astype(bf16), p['w_ff1'][i].T.astype(bf16),
                p['w_ff2'][i].astype(bf16), p['conv_a_w'][i], p['conv_c_w'][i]]

    def gathered_in(wt, ca, cc):
        wt = wt.reshape(d_in, D)
        ca = jnp.transpose(ca, (1, 0, 2)).reshape(KA, a_w)
        cc = jnp.transpose(cc, (1, 0, 2)).reshape(KC, xw)
        return dict(wt_main=wt[:main], wt_dt=_pad_to(wt[main:], 128, 0), ca=_pad_to(ca, 32, 0), cc=_pad_to(cc, 8, 0))

    def gathered(got):
        return dict(gathered_in(got[0], got[4], got[5]), wout=got[1].reshape(-1, D), w1t=got[2].reshape(-1, D),
                    w2=got[3].reshape(-1, D))

    sh0 = shards(0)
    h0a, tok = _exchange_start("gather_w0a_start", [sh0[0], sh0[4], sh0[5]], False)
    h0b, tok = _exchange_start("gather_w0b_start", _tie([sh0[1]], tok), False)
    h0c, tok = _exchange_start("gather_w0c_start", _tie([sh0[2]], tok), False)
    h0d, tok = _exchange_start("gather_w0d_start", _tie([sh0[3]], tok), False)
    W = [gathered_in(*_exchange_wait("gather_w0a_wait", h0a, tok))]

    saved = []
    xc = x2
    for i in range(depth):
        w = W[i]
        if i + 1 < depth:
            handle, tok = _exchange_start("gather_w%d_start" % (i + 1), _tie(shards(i + 1), (xc, tok, w['ca'])), False)
        h1 = _rms_fwd(xc, p['norm1_g'][i])
        (proj,) = _mm("mm_proj", h1, w['wt_main'], "nt", [f32])
        (pdt,) = _mm("mm_pdt", h1, w['wt_dt'], "nt", [f32])
        ya = _conv_fwd("confa_fwd", proj, 0, w['ca'], p['conv_a_b'][i], KA, True, nb, p['ln_a_g'][i], p['ln_a_b'][i])
        yb = _gmlp_fwd(proj, 1, p['ln_b_g'][i], p['ln_b_b'][i], p['w_spatial'][i], p['b_spatial'][i])
        xbc = _conv_fwd("convc_fwd", proj, 2, w['cc'], p['conv_c_b'][i], KC, False, nb)
        dtb, alog, dsk = _row128(p['dt_bias'][i]), _row128(p['a_log'][i]), _row128(p['d_skip'][i])
        yc, sin = _ssd_fwd(xbc, proj, 2, pdt, dtb, alog, dsk, p['norm_c_g'][i], nh, ngrp, nst, nb)
        ycat = jnp.concatenate([ya, yb, yc], axis=1)
        if i == 0:
            w['wout'] = _exchange_wait("gather_w0b_wait", h0b, ycat)[0].reshape(-1, D)
        (xm,) = _mm("mm_out", ycat, w['wout'], "nn", [f32], _ep_add, (xc,))
        h2 = _rms_fwd(xm, p['norm2_g'][i])
        if i == 0:
            w['w1t'] = _exchange_wait("gather_w0c_wait", h0c, h2)[0].reshape(-1, D)
        f, a = _mm("mm_ff1", h2, w['w1t'], "nt", [f32, bf16], _ep_relu2)
        if i == 0:
            w['w2'] = _exchange_wait("gather_w0d_wait", h0d, a)[0].reshape(-1, D)
        (xo,) = _mm("mm_ff2", a, w['w2'], "nn", [f32], _ep_add, (xm,))
        saved.append(dict(x_in=xc, h1=h1, proj=proj, pdt=pdt, xbc=xbc, sin=sin, ycat=ycat, xm=xm, h2=h2, f=f, a=a,
                          dtb=dtb, alog=alog, dsk=dsk))
        xc = xo
        if i + 1 < depth:
            W.append(gathered(_exchange_wait("gather_w%d_wait" % (i + 1), handle, xo)))

    lp, dx, dfinal = _loss_head(xc, p['final_g'], tgt)
    loss = lax.psum(lp[0, 0], ("x", "y", "c"))

    out = {}
    kinds = ("grad", "delta", "new_m", "new_v")
    names1 = _REPL + _CONVW

    started, small = [], [None] * depth

    def send(n, i, g):
        started.append((n, i, _exchange_start("scatter_%s_%d_start" % (n, i), [g.reshape(N_DEV, -1, D)], True)[0]))

    for i in reversed(range(depth)):
        w, sv = W[i], saved[i]
        (df,) = _mm("mm_df", dx, w['w2'], "nt", [bf16], _ep_drelu2, (sv['f'],))
        (gw2,) = _mm("mm_gw2", sv['a'], dx, "tn", [bf16])
        send('w_ff2', i, gw2)
        (dh2,) = _mm("mm_dh2", df, w['w1t'], "nn", [f32])
        (gw1t,) = _mm("mm_gw1", df, sv['h2'], "tn", [bf16])
        send('w_ff1', i, gw1t)
        dxm, dg2 = _rms_bwd(sv['xm'], p['norm2_g'][i], dh2, dx)
        (dycat,) = _mm("mm_dycat", dxm, w['wout'], "nt", [f32])
        (gwout,) = _mm("mm_gwout", sv['ycat'], dxm, "tn", [bf16])
        send('w_out', i, gwout)
        da, dwa, dba, dlag, dlab = _conv_bwd("confa_bwd", sv['proj'], 0, w['ca'], p['conv_a_b'][i], dycat, 0, KA, True, nb,
                                             p['ln_a_g'][i], p['ln_a_b'][i])
        dbb, dlbg, dlbb, dws, dbs = _gmlp_bwd(sv['proj'], 1, p['ln_b_g'][i], p['ln_b_b'][i], p['w_spatial'][i],
                                              p['b_spatial'][i], dycat, 1)
        dxbc, dz, ddt, ddtb, dalog, ddsk, dng = _ssd_bwd(sv['xbc'], sv['proj'], 2, sv['pdt'], sv['dtb'], sv['alog'], sv['dsk'],
                                                         p['norm_c_g'][i], sv['sin'], dycat, 1, nh, ngrp, nst, nb)
        dxbcp, dwc, dbc = _conv_bwd("convc_bwd", sv['proj'], 2, w['cc'], p['conv_c_b'][i], dxbc, 0, KC, False, nb)
        dproj = jnp.concatenate([da, dbb, dz, dxbcp], axis=1)
        (dh_main,) = _mm("mm_dh1", dproj, w['wt_main'], "nn", [f32])
        (dh,) = _mm("mm_dh1dt", ddt, w['wt_dt'], "nn", [f32], _ep_add, (dh_main,))
        (gwt_main,) = _mm("mm_gwin", dproj, sv['h1'], "tn", [bf16])
        (gwt_dt,) = _mm("mm_gwdt", ddt, sv['h1'], "tn", [bf16])
        send('w_in', i, jnp.concatenate([gwt_main, gwt_dt[:nh]], axis=0))
        dx, dg1 = _rms_bwd(sv['x_in'], p['norm1_g'][i], dh, dxm)

        gi = dict(norm1_g=dg1[0], norm2_g=dg2[0], conv_a_w=dwa[:KA], conv_a_b=dba[0], ln_a_g=dlag[0], ln_a_b=dlab[0],
                  ln_b_g=dlbg[0], ln_b_b=dlbb[0], w_spatial=dws, b_spatial=dbs, conv_c_w=dwc[:KC], conv_c_b=dbc[0],
                  dt_bias=ddtb[0, :nh], a_log=dalog[0, :nh], d_skip=ddsk[0, :nh], norm_c_g=dng[0])
        parts_i = [gi[n] for n in names1] + ([dfinal[0]] if i == depth - 1 else [])
        small[i] = ([a.shape for a in parts_i], _exchange_start("gather_g%d_start" % i, [_pack(parts_i)], False)[0])
    grad_x = dx.reshape(nb, S, D)

    dep = dx
    for n, i, handle in started:
        (parts,) = _exchange_wait("scatter_%s_%d_wait" % (n, i), handle, dep)
        tr = (lambda t: t.T) if _BIG_T[n] else (lambda t: t)
        res = _adam("adam_" + n, parts, tr(p[n][i]), tr(m[n][i]), tr(v[n][i]))
        for kind, r in zip(kinds, res):
            out.setdefault((kind, n), [None] * depth)[i] = tr(r)
        dep = res[3]

    for i in reversed(range(depth)):
        last = i == depth - 1
        shapes1, handle = small[i]
        (parts,) = _exchange_wait("gather_g%d_wait" % i, handle, dep)
        gl = _unpack(_sum_parts("sum_small", parts), shapes1)
        gd = dict(zip(names1, gl))
        for n in _CONVW:
            cw_shard = p[n].shape[2]
            gd[n] = lax.dynamic_slice_in_dim(gd[n], me * cw_shard, cw_shard, axis=1)
        slab = lambda q: _pack([q[n][i] for n in names1] + ([q['final_g']] if last else []))
        g2 = _pack([gd[n] for n in names1] + ([gl[-1]] if last else []))
        res = _adam("adam_small", g2[None], slab(p), slab(m), slab(v))
        shapes2 = [p[n].shape[1:] for n in names1] + ([p['final_g'].shape] if last else [])
        for kind, r in zip(kinds, res):
            lst = _unpack(r, shapes2)
            for n, arr in zip(names1, lst):
                out.setdefault((kind, n), [None] * depth)[i] = arr
            if last:
                out[(kind, 'final_g')] = lst[-1]
        dep = res[3]
    for n in _BIG + names1:
        for kind in kinds:
            out[(kind, n)] = jnp.stack(out[(kind, n)])

    flat = [loss, grad_x]
    for kind in ("grad", "delta", "new_m", "new_v"):
        flat += [out[(kind, n)] for n in _NAMES]
    return tuple(flat)


def kernel(x, norm1_g, w_in, conv_a_w, conv_a_b, ln_a_g, ln_a_b, ln_b_g, ln_b_b, w_spatial, b_spatial, conv_c_w, conv_c_b, dt_bias, a_log, d_skip, norm_c_g, w_out, norm2_g, w_ff1, w_ff2, final_g, loss_target, m_norm1_g, m_w_in, m_conv_a_w, m_conv_a_b, m_ln_a_g, m_ln_a_b, m_ln_b_g, m_ln_b_b, m_w_spatial, m_b_spatial, m_conv_c_w, m_conv_c_b, m_dt_bias, m_a_log, m_d_skip, m_norm_c_g, m_w_out, m_norm2_g, m_w_ff1, m_w_ff2, m_final_g, v_norm1_g, v_w_in, v_conv_a_w, v_conv_a_b, v_ln_a_g, v_ln_a_b, v_ln_b_g, v_ln_b_b, v_w_spatial, v_b_spatial, v_conv_c_w, v_conv_c_b, v_dt_bias, v_a_log, v_d_skip, v_norm_c_g, v_w_out, v_norm2_g, v_w_ff1, v_w_ff2, v_final_g):
    p = dict(zip(_NAMES, (norm1_g, w_in, conv_a_w, conv_a_b, ln_a_g, ln_a_b, ln_b_g, ln_b_b, w_spatial, b_spatial, conv_c_w,
                          conv_c_b, dt_bias, a_log, d_skip, norm_c_g, w_out, norm2_g, w_ff1, w_ff2, final_g)))
    m = dict(zip(_NAMES, (m_norm1_g, m_w_in, m_conv_a_w, m_conv_a_b, m_ln_a_g, m_ln_a_b, m_ln_b_g, m_ln_b_b, m_w_spatial,
                          m_b_spatial, m_conv_c_w, m_conv_c_b, m_dt_bias, m_a_log, m_d_skip, m_norm_c_g, m_w_out, m_norm2_g,
                          m_w_ff1, m_w_ff2, m_final_g)))
    v = dict(zip(_NAMES, (v_norm1_g, v_w_in, v_conv_a_w, v_conv_a_b, v_ln_a_g, v_ln_a_b, v_ln_b_g, v_ln_b_b, v_w_spatial,
                          v_b_spatial, v_conv_c_w, v_conv_c_b, v_dt_bias, v_a_log, v_d_skip, v_norm_c_g, v_w_out, v_norm2_g,
                          v_w_ff1, v_w_ff2, v_final_g)))
    return _step(p, m, v, x, loss_target)
```

```python
import functools
import math

import jax
import jax.numpy as jnp
from jax import lax
from jax.experimental import pallas as pl
from jax.experimental.pallas import tpu as pltpu

f32 = jnp.float32
bf16 = jnp.bfloat16
HI = lax.Precision.HIGHEST
EPS = 1e-5
HEAD = 64
CHUNK = 128
KA = 31
KC = 4
N_DEV = 8
VMEM_LIMIT = 56 * 1024 * 1024

ADAM_LR = 0.001
ADAM_B1 = 0.9
ADAM_B2 = 0.999
ADAM_EPS = 1e-08
ADAM_WD = 0.01
ADAM_STEP = 10


def _cparams(sem=None):
    return pltpu.CompilerParams(dimension_semantics=sem, vmem_limit_bytes=VMEM_LIMIT)


def _sds(shape, dtype):
    return jax.ShapeDtypeStruct(shape, dtype)


_DIMS = {"nn": ((1,), (0,)), "nt": ((1,), (1,)), "tn": ((0,), (0,))}


def _tile(n, cap):
    if n <= cap:
        return n
    for d in range(cap - cap % 128, 0, -128):
        if n % d == 0:
            return d
    raise ValueError((n, cap))


def _mm(name, a, b, form, out_dtypes, epilogue=None, extras=(), tm=1024, tn=512, tk=2048, dep=None):
    if form == "tn":
        K, M = a.shape
    else:
        M, K = a.shape
    N = b.shape[0] if form == "nt" else b.shape[1]
    tm, tn, tk = _tile(M, tm), _tile(N, tn), _tile(K, tk)
    nk = K // tk
    ne, no = len(extras), len(out_dtypes)
    deps = () if dep is None else (dep,)
    if epilogue is None:
        epilogue = lambda acc: (acc,)

    def body(a_ref, b_ref, *rest):
        extra_refs = rest[:ne]
        rest = rest[ne + len(deps):]
        out_refs = rest[:no]
        part = lax.dot_general(a_ref[...].astype(bf16), b_ref[...].astype(bf16),
                               (_DIMS[form], ((), ())), preferred_element_type=f32)

        def finish(acc):
            outs = epilogue(acc, *[e[...] for e in extra_refs])
            for o_ref, v in zip(out_refs, outs):
                o_ref[...] = v.astype(o_ref.dtype)

        if nk == 1:
            finish(part)
            return
        acc_ref = rest[no]
        k = pl.program_id(2)

        @pl.when(k == 0)
        def _():
            acc_ref[...] = part

        @pl.when((k > 0) & (k < nk - 1))
        def _():
            acc_ref[...] += part

        @pl.when(k == nk - 1)
        def _():
            finish(acc_ref[...] + part)

    a_spec = pl.BlockSpec((tk, tm), lambda i, j, k: (k, i)) if form == "tn" else pl.BlockSpec((tm, tk), lambda i, j, k: (i, k))
    b_spec = pl.BlockSpec((tn, tk), lambda i, j, k: (j, k)) if form == "nt" else pl.BlockSpec((tk, tn), lambda i, j, k: (k, j))
    mn_spec = pl.BlockSpec((tm, tn), lambda i, j, k: (i, j))
    return pl.pallas_call(
        body, name=name, grid=(M // tm, N // tn, nk),
        in_specs=[a_spec, b_spec] + [mn_spec] * ne + [pl.BlockSpec((8, 128), lambda i, j, k: (0, 0))] * len(deps),
        out_specs=[mn_spec] * no,
        out_shape=[_sds((M, N), d) for d in out_dtypes],
        scratch_shapes=[pltpu.VMEM((tm, tn), f32)] if nk > 1 else [],
        compiler_params=_cparams(("parallel", "parallel", "arbitrary")),
    )(a, b, *extras, *deps)


def _ep_add(acc, r):
    return (acc + r,)


def _ep_relu2(acc):
    r = jnp.maximum(acc, 0.0)
    return acc, r * r


def _ep_drelu2(acc, f):
    return (acc * 2.0 * jnp.maximum(f, 0.0),)


def _rms(x, g):
    return x * lax.rsqrt(jnp.mean(x * x, axis=-1, keepdims=True) + EPS) * g


TT = 512


def _rms_fwd(x, g):
    T, D = x.shape

    def body(x_ref, g_ref, h_ref):
        h_ref[...] = _rms(x_ref[...], g_ref[...]).astype(bf16)

    return pl.pallas_call(
        body, name="rms_fwd", grid=(T // TT,),
        in_specs=[pl.BlockSpec((TT, D), lambda i: (i, 0)), pl.BlockSpec((1, D), lambda i: (0, 0))],
        out_specs=pl.BlockSpec((TT, D), lambda i: (i, 0)),
        out_shape=_sds((T, D), bf16), compiler_params=_cparams(("arbitrary",)),
    )(x, g.reshape(1, D))


def _rms_bwd(x, g, dh, dres):
    T, D = x.shape

    def body(x_ref, g_ref, dh_ref, dres_ref, dx_ref, dg_ref):
        _, vjp = jax.vjp(_rms, x_ref[...], g_ref[...])
        dx, dg = vjp(dh_ref[...])
        dx_ref[...] = dres_ref[...] + dx

        @pl.when(pl.program_id(0) == 0)
        def _():
            dg_ref[...] = jnp.zeros_like(dg_ref)

        dg_ref[0:1, :] += dg

    tile = pl.BlockSpec((TT, D), lambda i: (i, 0))
    return pl.pallas_call(
        body, name="rms_bwd", grid=(T // TT,),
        in_specs=[tile, pl.BlockSpec((1, D), lambda i: (0, 0)), tile, tile],
        out_specs=[tile, pl.BlockSpec((8, D), lambda i: (0, 0))],
        out_shape=[_sds((T, D), f32), _sds((8, D), f32)], compiler_params=_cparams(("arbitrary",)),
    )(x, g.reshape(1, D), dh, dres)


def _loss_head(x, g, tgt):
    T, D = x.shape

    def f(xv, gv, tv):
        e = _rms(xv, gv) - tv
        return 0.5 * jnp.sum(jnp.sum(e * e, axis=-1, keepdims=True) * (1.0 / D), axis=0, keepdims=True)

    def body(x_ref, g_ref, t_ref, loss_ref, dx_ref, dg_ref):
        tv = t_ref[...]
        l, vjp = jax.vjp(lambda xv, gv: f(xv, gv, tv), x_ref[...], g_ref[...])
        dx, dg = vjp(jnp.ones((1, 1), f32))
        dx_ref[...] = dx

        @pl.when(pl.program_id(0) == 0)
        def _():
            dg_ref[...] = jnp.zeros_like(dg_ref)
            loss_ref[...] = jnp.zeros_like(loss_ref)

        dg_ref[0:1, :] += dg
        loss_ref[...] += jnp.broadcast_to(l, loss_ref.shape)

    tile = pl.BlockSpec((TT, D), lambda i: (i, 0))
    return pl.pallas_call(
        body, name="loss_head", grid=(T // TT,),
        in_specs=[tile, pl.BlockSpec((1, D), lambda i: (0, 0)), tile],
        out_specs=[pl.BlockSpec((8, 128), lambda i: (0, 0)), tile, pl.BlockSpec((8, D), lambda i: (0, 0))],
        out_shape=[_sds((8, 128), f32), _sds((T, D), f32), _sds((8, D), f32)],
        compiler_params=_cparams(("arbitrary",)),
    )(x, g.reshape(1, D), tgt)


TB = 256


def _glu(a_val, a_gate):
    return a_val * jax.nn.sigmoid(a_gate)


def _ln_silu(v, g, b):
    mu = jnp.mean(v, axis=-1, keepdims=True)
    vc = v - mu
    var = jnp.mean(vc * vc, axis=-1, keepdims=True)
    return jax.nn.silu(vc * lax.rsqrt(var + EPS) * g + b)


def _conv_geom(kw):
    halo = 32 if kw > 9 else 8
    return halo, halo - (kw - 1)


def _conv_taps(hp_ref, w_ref, b_ref, acc_ref, kw, off, width):
    for c in range(width // 128):
        ls = pl.ds(c * 128, 128)
        acc = jnp.broadcast_to(b_ref[:, ls], (TB, 128))
        for k in range(kw):
            acc = acc + w_ref[k:k + 1, ls] * hp_ref[pl.ds(off + k, TB), ls]
        acc_ref[:, ls] = acc


def _conv_fwd(name, src, col_block, w, b, kw, conformer, n_seq, ln_g=None, ln_b=None):
    T = src.shape[0]
    cout = w.shape[1]
    cin = 2 * cout if conformer else cout
    halo, off = _conv_geom(kw)
    nblk = T // n_seq // TB
    hb = TB // halo

    def body(cur_ref, halo_ref, w_ref, b_ref, *rest):
        if conformer:
            g_ref, lb_ref, out_ref, hp_ref, acc_ref = rest
        else:
            out_ref, hp_ref, acc_ref = rest
        i = pl.program_id(1)
        first = (i == 0)
        if conformer:
            hp_ref[pl.ds(halo, TB), :] = _glu(cur_ref[:, 0:cout], cur_ref[:, cout:cin])
            hh = _glu(halo_ref[:, 0:cout], halo_ref[:, cout:cin])
        else:
            hp_ref[pl.ds(halo, TB), :] = cur_ref[...]
            hh = halo_ref[...]
        hp_ref[pl.ds(0, halo), :] = jnp.where(first, 0.0, hh)
        _conv_taps(hp_ref, w_ref, b_ref, acc_ref, kw, off, cout)
        if conformer:
            for h in range(cout // HEAD):
                ls = pl.ds(h * HEAD, HEAD)
                out_ref[:, ls] = _ln_silu(acc_ref[:, ls], g_ref[:, ls], lb_ref[:, ls]).astype(out_ref.dtype)
        else:
            out_ref[...] = jax.nn.silu(acc_ref[...]).astype(out_ref.dtype)

    row = pl.BlockSpec((1, cout), lambda s, i: (0, 0))
    in_specs = [pl.BlockSpec((TB, cin), lambda s, i: (s * nblk + i, col_block)),
                pl.BlockSpec((halo, cin), lambda s, i: (jnp.maximum((s * nblk + i) * hb - 1, 0), col_block)),
                pl.BlockSpec((w.shape[0], cout), lambda s, i: (0, 0)), row]
    args = [src, src, w, b.reshape(1, cout)]
    if conformer:
        in_specs += [row, row]
        args += [ln_g.reshape(1, cout), ln_b.reshape(1, cout)]
    out_dtype = bf16 if conformer else f32
    return pl.pallas_call(
        body, name=name, grid=(n_seq, nblk), in_specs=in_specs,
        out_specs=pl.BlockSpec((TB, cout), lambda s, i: (s * nblk + i, 0)),
        out_shape=_sds((T, cout), out_dtype),
        scratch_shapes=[pltpu.VMEM((halo + TB, cout), f32), pltpu.VMEM((TB, cout), f32)],
        compiler_params=_cparams(("arbitrary", "arbitrary")),
    )(*args)


def _conv_bwd(name, src, col_block, w, b, dy, dy_col_block, kw, conformer, n_seq, ln_g=None, ln_b=None):
    T = src.shape[0]
    cout = w.shape[1]
    wrows = w.shape[0]
    cin = 2 * cout if conformer else cout
    halo, off = _conv_geom(kw)
    nblk = T // n_seq // TB
    hb = TB // halo

    def body(cur_ref, halo_ref, w_ref, b_ref, dy_ref, *rest):
        if conformer:
            g_ref, lb_ref, dsrc_ref, dw_ref, db_ref, dg_ref, dlb_ref, hp_ref, acc_ref, dz_ref, dhp_ref, carry_ref = rest
        else:
            dsrc_ref, dw_ref, db_ref, hp_ref, acc_ref, dz_ref, dhp_ref, carry_ref = rest
        s, ii = pl.program_id(0), pl.program_id(1)
        i = nblk - 1 - ii
        first = (i == 0)

        @pl.when((s == 0) & (ii == 0))
        def _():
            dw_ref[...] = jnp.zeros_like(dw_ref)
            db_ref[...] = jnp.zeros_like(db_ref)
            if conformer:
                dg_ref[...] = jnp.zeros_like(dg_ref)
                dlb_ref[...] = jnp.zeros_like(dlb_ref)

        @pl.when(ii == 0)
        def _():
            carry_ref[...] = jnp.zeros_like(carry_ref)
            dz_ref[pl.ds(0, halo), :] = jnp.zeros((halo, cout), f32)
            dz_ref[pl.ds(halo + TB, halo), :] = jnp.zeros((halo, cout), f32)

        if conformer:
            hp_ref[pl.ds(halo, TB), :] = _glu(cur_ref[:, 0:cout], cur_ref[:, cout:cin])
            hh = _glu(halo_ref[:, 0:cout], halo_ref[:, cout:cin])
        else:
            hp_ref[pl.ds(halo, TB), :] = cur_ref[...]
            hh = halo_ref[...]
        hp_ref[pl.ds(0, halo), :] = jnp.where(first, 0.0, hh)
        _conv_taps(hp_ref, w_ref, b_ref, acc_ref, kw, off, cout)

        if conformer:
            for h in range(cout // HEAD):
                ls = pl.ds(h * HEAD, HEAD)
                _, vjp = jax.vjp(_ln_silu, acc_ref[:, ls], g_ref[:, ls], lb_ref[:, ls])
                da, dg, dlb = vjp(dy_ref[:, ls].astype(f32))
                dz_ref[pl.ds(halo, TB), ls] = da
                dg_ref[0:1, ls] += dg
                dlb_ref[0:1, ls] += dlb
        else:
            _, vjp = jax.vjp(jax.nn.silu, acc_ref[...])
            dz_ref[pl.ds(halo, TB), :] = vjp(dy_ref[...].astype(f32))[0]

        for c in range(cout // 128):
            ls = pl.ds(c * 128, 128)
            dacc = dz_ref[pl.ds(halo, TB), ls]
            db_ref[0:1, ls] += jnp.sum(dacc, axis=0, keepdims=True)
            dhp = jnp.zeros((halo + TB, 128), f32)
            for k in range(kw):
                dw_ref[k:k + 1, ls] += jnp.sum(dacc * hp_ref[pl.ds(off + k, TB), ls], axis=0, keepdims=True)
                dhp = dhp + w_ref[k:k + 1, ls] * dz_ref[pl.ds(kw - 1 - k, halo + TB), ls]
            dhp_ref[:, ls] = dhp
        dhp_ref[pl.ds(TB, halo), :] += carry_ref[...]
        carry_ref[...] = dhp_ref[pl.ds(0, halo), :]
        dcur = dhp_ref[pl.ds(halo, TB), :]
        if conformer:
            _, vjp = jax.vjp(_glu, cur_ref[:, 0:cout], cur_ref[:, cout:cin])
            dval, dgate = vjp(dcur)
            dsrc_ref[:, 0:cout] = dval.astype(dsrc_ref.dtype)
            dsrc_ref[:, cout:cin] = dgate.astype(dsrc_ref.dtype)
        else:
            dsrc_ref[...] = dcur.astype(dsrc_ref.dtype)

    def blk(s, ii):
        return s * nblk + (nblk - 1 - ii)

    row = pl.BlockSpec((1, cout), lambda s, ii: (0, 0))
    acc8 = pl.BlockSpec((8, cout), lambda s, ii: (0, 0))
    in_specs = [pl.BlockSpec((TB, cin), lambda s, ii: (blk(s, ii), col_block)),
                pl.BlockSpec((halo, cin), lambda s, ii: (jnp.maximum(blk(s, ii) * hb - 1, 0), col_block)),
                pl.BlockSpec((wrows, cout), lambda s, ii: (0, 0)), row,
                pl.BlockSpec((TB, cout), lambda s, ii: (blk(s, ii), dy_col_block))]
    args = [src, src, w, b.reshape(1, cout), dy]
    out_specs = [pl.BlockSpec((TB, cin), lambda s, ii: (blk(s, ii), 0)),
                 pl.BlockSpec((wrows, cout), lambda s, ii: (0, 0)), acc8]
    out_shape = [_sds((T, cin), bf16), _sds((wrows, cout), f32), _sds((8, cout), f32)]
    if conformer:
        in_specs += [row, row]
        args += [ln_g.reshape(1, cout), ln_b.reshape(1, cout)]
        out_specs += [acc8, acc8]
        out_shape += [_sds((8, cout), f32), _sds((8, cout), f32)]
    return pl.pallas_call(
        body, name=name, grid=(n_seq, nblk), in_specs=in_specs, out_specs=out_specs, out_shape=out_shape,
        scratch_shapes=[pltpu.VMEM((halo + TB, cout), f32), pltpu.VMEM((TB, cout), f32),
                        pltpu.VMEM((halo + TB + halo, cout), f32), pltpu.VMEM((halo + TB, cout), f32),
                        pltpu.VMEM((halo, cout), f32)],
        compiler_params=_cparams(("arbitrary", "arbitrary")),
    )(*args)


def _gelu(x):
    return 0.5 * x * (1.0 + lax.erf(x * (1.0 / math.sqrt(2.0))))


def _tril_mask(n):
    r = lax.broadcasted_iota(jnp.int32, (n, n), 0)
    c = lax.broadcasted_iota(jnp.int32, (n, n), 1)
    return r >= c


def _row_select(rows, h, width):
    r = lax.broadcasted_iota(jnp.int32, (rows, width), 0)
    return (r == h).astype(f32)


def _gmlp_head(h, bu, bv, g, b, w_h, bs):
    u = _gelu(bu)
    v = _gelu(bv)
    mu = jnp.mean(v, axis=-1, keepdims=True)
    vc = v - mu
    var = jnp.mean(vc * vc, axis=-1, keepdims=True)
    vn = vc * lax.rsqrt(var + EPS) * g + b
    wm = jnp.where(_tril_mask(CHUNK), w_h, 0.0)
    mix = jnp.dot(wm.astype(bf16), vn.astype(bf16), preferred_element_type=f32)
    bias = lax.dot_general(bs, _row_select(bs.shape[0], h, HEAD), (((0,), (0,)), ((), ())),
                           precision=HI, preferred_element_type=f32)
    return u * (mix + bias)


def _gmlp_fwd(proj, col_block, ln_g, ln_b, w_s, b_s):
    T = proj.shape[0]
    nh = w_s.shape[0]
    width = nh * HEAD

    def body(p_ref, g_ref, b_ref, w_ref, bs_ref, out_ref):
        bs = bs_ref[...]
        for h in range(nh):
            ls = pl.ds(h * HEAD, HEAD)
            lv = pl.ds(width + h * HEAD, HEAD)
            out_ref[:, ls] = _gmlp_head(h, p_ref[:, ls], p_ref[:, lv], g_ref[:, ls], b_ref[:, ls], w_ref[h], bs).astype(out_ref.dtype)

    row = pl.BlockSpec((1, width), lambda i: (0, 0))
    return pl.pallas_call(
        body, name="gmlp_fwd", grid=(T // CHUNK,),
        in_specs=[pl.BlockSpec((CHUNK, 2 * width), lambda i: (i, col_block)), row, row,
                  pl.BlockSpec((nh, CHUNK, CHUNK), lambda i: (0, 0, 0)), pl.BlockSpec((nh, CHUNK), lambda i: (0, 0))],
        out_specs=pl.BlockSpec((CHUNK, width), lambda i: (i, 0)),
        out_shape=_sds((T, width), bf16), compiler_params=_cparams(("arbitrary",)),
    )(proj, ln_g.reshape(1, width), ln_b.reshape(1, width), w_s, b_s)


def _gmlp_bwd(proj, col_block, ln_g, ln_b, w_s, b_s, dy, dy_col_block):
    T = proj.shape[0]
    nh = w_s.shape[0]
    width = nh * HEAD

    def body(p_ref, g_ref, b_ref, w_ref, bs_ref, dy_ref, dp_ref, dg_ref, db_ref, dw_ref, dbs_ref):
        @pl.when(pl.program_id(0) == 0)
        def _():
            dg_ref[...] = jnp.zeros_like(dg_ref)
            db_ref[...] = jnp.zeros_like(db_ref)
            dw_ref[...] = jnp.zeros_like(dw_ref)
            dbs_ref[...] = jnp.zeros_like(dbs_ref)

        bs = bs_ref[...]
        for h in range(nh):
            ls = pl.ds(h * HEAD, HEAD)
            lv = pl.ds(width + h * HEAD, HEAD)
            _, vjp = jax.vjp(functools.partial(_gmlp_head, h), p_ref[:, ls], p_ref[:, lv], g_ref[:, ls], b_ref[:, ls], w_ref[h], bs)
            dbu, dbv, dg, db, dw, dbs = vjp(dy_ref[:, ls].astype(f32))
            dp_ref[:, ls] = dbu.astype(dp_ref.dtype)
            dp_ref[:, lv] = dbv.astype(dp_ref.dtype)
            dg_ref[0:1, ls] += dg
            db_ref[0:1, ls] += db
            dw_ref[h] += dw
            dbs_ref[...] += dbs

    row = pl.BlockSpec((1, width), lambda i: (0, 0))
    acc8 = pl.BlockSpec((8, width), lambda i: (0, 0))
    wspec = pl.BlockSpec((nh, CHUNK, CHUNK), lambda i: (0, 0, 0))
    bspec = pl.BlockSpec((nh, CHUNK), lambda i: (0, 0))
    return pl.pallas_call(
        body, name="gmlp_bwd", grid=(T // CHUNK,),
        in_specs=[pl.BlockSpec((CHUNK, 2 * width), lambda i: (i, col_block)), row, row, wspec, bspec,
                  pl.BlockSpec((CHUNK, width), lambda i: (i, dy_col_block))],
        out_specs=[pl.BlockSpec((CHUNK, 2 * width), lambda i: (i, 0)), acc8, acc8, wspec, bspec],
        out_shape=[_sds((T, 2 * width), bf16), _sds((8, width), f32), _sds((8, width), f32),
                   _sds((nh, CHUNK, CHUNK), f32), _sds((nh, CHUNK), f32)],
        compiler_params=_cparams(("arbitrary",)),
    )(proj, ln_g.reshape(1, width), ln_b.reshape(1, width), w_s, b_s, dy)


def _sel_col(x, h):
    lane = lax.broadcasted_iota(jnp.int32, x.shape, 1)
    return jnp.sum(jnp.where(lane == h, x, 0.0), axis=1, keepdims=True)


def _sel_row(x, h):
    sub = lax.broadcasted_iota(jnp.int32, x.shape, 0)
    return jnp.sum(jnp.where(sub == h, x, 0.0), axis=0, keepdims=True)


PAIR = 2 * HEAD


def _ssd_chunk(nh, ngrp, xs_l, z_l, b_l, c_l, dtraw, dtb, alog, dskip, ng_l, prev_l):
    hg = nh // ngrp
    tril = _tril_mask(CHUNK)
    tl = tril.astype(f32)
    lo = lax.broadcasted_iota(jnp.int32, (CHUNK, PAIR), 1) < HEAD
    lo_row = lo[0:1, :]
    dt = jax.nn.softplus(dtraw + dtb)
    a = dt * (-jnp.exp(alog))
    cs = jnp.dot(tl, a, precision=HI, preferred_element_type=f32)
    cst = lax.dot_general(a, tl, (((0,), (1,)), ((), ())), precision=HI, preferred_element_type=f32)
    cb_l = [lax.dot_general(c_l[g].astype(bf16), b_l[g].astype(bf16), (((1,), (1,)), ((), ())),
                            preferred_element_type=f32) for g in range(ngrp)]
    yz_l, new_prev = [], []
    for q in range(nh // 2):
        g = (2 * q) // hg
        cols = []
        for h in (2 * q, 2 * q + 1):
            cs_h = _sel_col(cs, h)
            tot = _sel_row(cs_h, CHUNK - 1)
            seg = jnp.where(tril, cs_h - _sel_row(cst, h), 0.0)
            lmat = jnp.where(tril, jnp.exp(seg), 0.0)
            cols.append((_sel_col(dt, h), cs_h, tot, lmat, _sel_col(dskip, h)))
        (dt_a, cs_a, tot_a, l_a, dsk_a), (dt_b, cs_b, tot_b, l_b, dsk_b) = cols
        xs = xs_l[q]
        x = xs * jnp.where(lo, dt_a, dt_b)
        xb = x.astype(bf16)
        ydiag = jnp.where(lo, jnp.dot((cb_l[g] * l_a).astype(bf16), xb, preferred_element_type=f32),
                          jnp.dot((cb_l[g] * l_b).astype(bf16), xb, preferred_element_type=f32))
        yoff = (jnp.dot(c_l[g].astype(bf16), prev_l[q].astype(bf16), preferred_element_type=f32)
                * jnp.where(lo, jnp.exp(cs_a), jnp.exp(cs_b)))
        xdec = x * jnp.where(lo, jnp.exp(tot_a - cs_a), jnp.exp(tot_b - cs_b))
        st = lax.dot_general(b_l[g].astype(bf16), xdec.astype(bf16), (((0,), (0,)), ((), ())),
                             preferred_element_type=f32)
        new_prev.append(prev_l[q] * jnp.where(lo_row, jnp.exp(tot_a), jnp.exp(tot_b)) + st)
        y = ydiag + yoff + jnp.where(lo_row, dsk_a, dsk_b) * xs
        yz_l.append(y * jax.nn.silu(z_l[q]))
    out = [None] * (nh // 2)
    qg = hg // 2
    for g in range(ngrp):
        ssq = sum(jnp.sum(yz_l[q] * yz_l[q], axis=-1, keepdims=True) for q in range(g * qg, (g + 1) * qg))
        r = lax.rsqrt(ssq * (1.0 / (hg * HEAD)) + EPS)
        for q in range(g * qg, (g + 1) * qg):
            out[q] = yz_l[q] * r * ng_l[q]
    return out, new_prev


def _ssd_read(nh, ngrp, nst, xbc_ref, z_ref, ng_ref, st_ref):
    cw = nh * HEAD
    xs_l = [xbc_ref[:, pl.ds(q * PAIR, PAIR)] for q in range(nh // 2)]
    b_l = [xbc_ref[:, pl.ds(cw + g * nst, nst)] for g in range(ngrp)]
    c_l = [xbc_ref[:, pl.ds(cw + ngrp * nst + g * nst, nst)] for g in range(ngrp)]
    z_l = [z_ref[:, pl.ds(q * PAIR, PAIR)] for q in range(nh // 2)]
    ng_l = [ng_ref[:, pl.ds(q * PAIR, PAIR)] for q in range(nh // 2)]
    prev_l = [st_ref[:, pl.ds(q * PAIR, PAIR)] for q in range(nh // 2)]
    return xs_l, z_l, b_l, c_l, ng_l, prev_l


def _ssd_fwd(xbc, proj, z_col_block, pdt, dtb, alog, dskip, ng, nh, ngrp, nst, n_seq):
    T = xbc.shape[0]
    cw = nh * HEAD
    nchunk = T // n_seq // CHUNK
    assert nst == CHUNK

    def body(xbc_ref, z_ref, dt_ref, dtb_ref, alog_ref, dskip_ref, ng_ref, y_ref, sin_ref, st_ref):
        @pl.when(pl.program_id(1) == 0)
        def _():
            st_ref[...] = jnp.zeros_like(st_ref)

        sin_ref[...] = st_ref[...]
        xs_l, z_l, b_l, c_l, ng_l, prev_l = _ssd_read(nh, ngrp, nst, xbc_ref, z_ref, ng_ref, st_ref)
        y_l, new_prev = _ssd_chunk(nh, ngrp, xs_l, z_l, b_l, c_l, dt_ref[...], dtb_ref[...], alog_ref[...],
                                   dskip_ref[...], ng_l, prev_l)
        for q in range(nh // 2):
            ls = pl.ds(q * PAIR, PAIR)
            y_ref[:, ls] = y_l[q].astype(y_ref.dtype)
            st_ref[:, ls] = new_prev[q]

    def blk(s, c):
        return s * nchunk + c

    prow = pl.BlockSpec((1, 128), lambda s, c: (0, 0))
    return pl.pallas_call(
        body, name="ssd_fwd", grid=(n_seq, nchunk),
        in_specs=[pl.BlockSpec((CHUNK, xbc.shape[1]), lambda s, c: (blk(s, c), 0)),
                  pl.BlockSpec((CHUNK, cw), lambda s, c: (blk(s, c), z_col_block)),
                  pl.BlockSpec((CHUNK, 128), lambda s, c: (blk(s, c), 0)),
                  prow, prow, prow, pl.BlockSpec((1, cw), lambda s, c: (0, 0))],
        out_specs=[pl.BlockSpec((CHUNK, cw), lambda s, c: (blk(s, c), 0)),
                   pl.BlockSpec((nst, cw), lambda s, c: (blk(s, c), 0))],
        out_shape=[_sds((T, cw), bf16), _sds((T, cw), f32)],
        scratch_shapes=[pltpu.VMEM((nst, cw), f32)],
        compiler_params=_cparams(("arbitrary", "arbitrary")),
    )(xbc, proj, pdt, dtb, alog, dskip, ng.reshape(1, cw))


def _ssd_bwd(xbc, proj, z_col_block, pdt, dtb, alog, dskip, ng, sin, dy, dy_col_block, nh, ngrp, nst, n_seq):
    T, xw = xbc.shape
    cw = nh * HEAD
    nchunk = T // n_seq // CHUNK

    def body(xbc_ref, z_ref, dt_ref, dtb_ref, alog_ref, dskip_ref, ng_ref, sin_ref, dy_ref,
             dxbc_ref, dz_ref, ddt_ref, ddtb_ref, dalog_ref, ddskip_ref, dng_ref, dst_ref):
        s, cc = pl.program_id(0), pl.program_id(1)

        @pl.when((s == 0) & (cc == 0))
        def _():
            ddtb_ref[...] = jnp.zeros_like(ddtb_ref)
            dalog_ref[...] = jnp.zeros_like(dalog_ref)
            ddskip_ref[...] = jnp.zeros_like(ddskip_ref)
            dng_ref[...] = jnp.zeros_like(dng_ref)

        @pl.when(cc == 0)
        def _():
            dst_ref[...] = jnp.zeros_like(dst_ref)

        xs_l, z_l, b_l, c_l, ng_l, prev_l = _ssd_read(nh, ngrp, nst, xbc_ref, z_ref, ng_ref, sin_ref)
        _, vjp = jax.vjp(functools.partial(_ssd_chunk, nh, ngrp), xs_l, z_l, b_l, c_l, dt_ref[...], dtb_ref[...],
                         alog_ref[...], dskip_ref[...], ng_l, prev_l)
        dy_l = [dy_ref[:, pl.ds(q * PAIR, PAIR)].astype(f32) for q in range(nh // 2)]
        dst_l = [dst_ref[:, pl.ds(q * PAIR, PAIR)] for q in range(nh // 2)]
        dxs_l, dz_l, db_l, dc_l, ddt, ddtb, dalog, ddskip, dng_l, dprev_l = vjp((dy_l, dst_l))
        for q in range(nh // 2):
            ls = pl.ds(q * PAIR, PAIR)
            dxbc_ref[:, ls] = dxs_l[q]
            dz_ref[:, ls] = dz_l[q].astype(dz_ref.dtype)
            dng_ref[0:1, ls] += dng_l[q]
            dst_ref[:, ls] = dprev_l[q]
        for g in range(ngrp):
            dxbc_ref[:, pl.ds(cw + g * nst, nst)] = db_l[g]
            dxbc_ref[:, pl.ds(cw + ngrp * nst + g * nst, nst)] = dc_l[g]
        ddt_ref[...] = ddt.astype(ddt_ref.dtype)
        ddtb_ref[0:1, :] += ddtb
        dalog_ref[0:1, :] += dalog
        ddskip_ref[0:1, :] += ddskip

    def blk(s, cc):
        return s * nchunk + (nchunk - 1 - cc)

    prow = pl.BlockSpec((1, 128), lambda s, c: (0, 0))
    pacc = pl.BlockSpec((8, 128), lambda s, c: (0, 0))
    return pl.pallas_call(
        body, name="ssd_bwd", grid=(n_seq, nchunk),
        in_specs=[pl.BlockSpec((CHUNK, xw), lambda s, c: (blk(s, c), 0)),
                  pl.BlockSpec((CHUNK, cw), lambda s, c: (blk(s, c), z_col_block)),
                  pl.BlockSpec((CHUNK, 128), lambda s, c: (blk(s, c), 0)),
                  prow, prow, prow, pl.BlockSpec((1, cw), lambda s, c: (0, 0)),
                  pl.BlockSpec((nst, cw), lambda s, c: (blk(s, c), 0)),
                  pl.BlockSpec((CHUNK, cw), lambda s, c: (blk(s, c), dy_col_block))],
        out_specs=[pl.BlockSpec((CHUNK, xw), lambda s, c: (blk(s, c), 0)),
                   pl.BlockSpec((CHUNK, cw), lambda s, c: (blk(s, c), 0)),
                   pl.BlockSpec((CHUNK, 128), lambda s, c: (blk(s, c), 0)),
                   pacc, pacc, pacc, pl.BlockSpec((8, cw), lambda s, c: (0, 0))],
        out_shape=[_sds((T, xw), f32), _sds((T, cw), bf16), _sds((T, 128), bf16),
                   _sds((8, 128), f32), _sds((8, 128), f32), _sds((8, 128), f32), _sds((8, cw), f32)],
        scratch_shapes=[pltpu.VMEM((nst, cw), f32)],
        compiler_params=_cparams(("arbitrary", "arbitrary")),
    )(xbc, proj, pdt, dtb, alog, dskip, ng.reshape(1, cw), sin, dy)


_HBM = pl.BlockSpec(memory_space=pltpu.HBM)
_SEM = pl.BlockSpec(memory_space=pltpu.SEMAPHORE)
_EFFECT = pltpu.SideEffectType.DATAFLOW_SIDE_EFFECTING


def _split_copies(n, scatter, src_refs, land_refs, send_sems, recv_sems):
    npeer = N_DEV - 1
    x, y, c = lax.axis_index("x"), lax.axis_index("y"), lax.axis_index("c")
    me = 4 * x + 2 * y + c
    copies = []
    for i in range(n):
        for k in range(1, N_DEV):
            px = 1 - x if k & 4 else x
            py = 1 - y if k & 2 else y
            pc = 1 - c if k & 1 else c
            src = src_refs[i].at[4 * px + 2 * py + pc] if scatter else src_refs[i]
            copies.append(pltpu.make_async_remote_copy(
                src_ref=src, dst_ref=land_refs[i].at[me],
                send_sem=send_sems.at[i * npeer + k - 1], recv_sem=recv_sems.at[i * npeer + k - 1],
                device_id=(px, py, pc), device_id_type=pl.DeviceIdType.MESH))
    return copies


def _exchange_start(name, arrs, scatter):
    n = len(arrs)
    nsem = n * (N_DEV - 1)
    me = 4 * lax.axis_index("x") + 2 * lax.axis_index("y") + lax.axis_index("c")
    lands = []
    for a in arrs:
        own = lax.dynamic_index_in_dim(a, me, 0, keepdims=True) if scatter else a[None]
        full = lax.empty(a.shape if scatter else (N_DEV,) + a.shape, a.dtype)
        lands.append(lax.dynamic_update_slice(full, own, (me,) + (0,) * (full.ndim - 1)))

    def body(*refs):
        src_refs, land_refs = refs[:n], refs[n:2 * n]
        send_sems, recv_sems = refs[2 * n], refs[2 * n + 1]
        token = refs[-1]
        for cp in _split_copies(n, scatter, src_refs, land_refs, send_sems, recv_sems):
            cp.start()
        token[...] = jnp.zeros_like(token)

    res = pl.pallas_call(
        body, name=name,
        out_shape=(pltpu.SemaphoreType.DMA((nsem,)), pltpu.SemaphoreType.DMA((nsem,)),
                   *[pltpu.HBM(a.shape, a.dtype) for a in arrs], *[pltpu.HBM(l.shape, l.dtype) for l in lands],
                   _sds((8, 128), f32)),
        in_specs=[_HBM] * (2 * n),
        out_specs=(_SEM, _SEM, *[_HBM] * (2 * n), pl.BlockSpec(memory_space=pltpu.VMEM)),
        input_output_aliases={j: 2 + j for j in range(2 * n)},
        compiler_params=pltpu.CompilerParams(has_side_effects=_EFFECT),
    )(*[pltpu.with_memory_space_constraint(a, pltpu.HBM) for a in arrs],
      *[pltpu.with_memory_space_constraint(l, pltpu.HBM) for l in lands])
    return (n, scatter, res[0], res[1], res[2:2 + n], res[2 + n:2 + 2 * n]), res[-1]


def _exchange_wait(name, handle, after):
    n, scatter, send_sems, recv_sems, srcs, lands = handle
    after = list(after) if isinstance(after, (list, tuple)) else [after]

    def body(*refs):
        src_refs, land_refs = refs[:n], refs[n:2 * n]
        for cp in _split_copies(n, scatter, src_refs, land_refs, refs[2 * n], refs[2 * n + 1]):
            cp.wait_send()
            cp.wait_recv()

    res = pl.pallas_call(
        body, name=name,
        out_shape=[pltpu.HBM(a.shape, a.dtype) for a in (*srcs, *lands)],
        in_specs=[_HBM] * (2 * n) + [_SEM, _SEM] + [pl.BlockSpec(memory_space=pl.ANY)] * len(after),
        out_specs=[_HBM] * (2 * n),
        input_output_aliases={j: j for j in range(2 * n)},
        compiler_params=pltpu.CompilerParams(has_side_effects=_EFFECT),
    )(*srcs, *lands, send_sems, recv_sems, *after)
    return res[n:]


def _adam_tiles(R, C):
    if R % 256 == 0:
        return (256, C), (R // 256, 1)
    assert C % 128 == 0
    return (R, 128), (1, C // 128)


def _adam(name, parts, w, m, v):
    P, R, C = parts.shape
    (tr, tc), (gr, gc) = _adam_tiles(R, C)
    c1 = 1.0 / (1.0 - ADAM_B1 ** ADAM_STEP)
    c2 = 1.0 / (1.0 - ADAM_B2 ** ADAM_STEP)

    def body(p_ref, w_ref, m_ref, v_ref, g_ref, d_ref, nm_ref, nv_ref):
        g = p_ref[0].astype(f32)
        for s in range(1, P):
            g = g + p_ref[s].astype(f32)
        nm = ADAM_B1 * m_ref[...] + (1.0 - ADAM_B1) * g
        nv = ADAM_B2 * v_ref[...] + (1.0 - ADAM_B2) * (g * g)
        g_ref[...] = g
        nm_ref[...] = nm
        nv_ref[...] = nv
        d_ref[...] = -ADAM_LR * ((nm * c1) / (jnp.sqrt(nv * c2) + ADAM_EPS) + ADAM_WD * w_ref[...])

    tile = pl.BlockSpec((tr, tc), lambda i, j: (i, j))
    return pl.pallas_call(
        body, name=name, grid=(gr, gc),
        in_specs=[pl.BlockSpec((P, tr, tc), lambda i, j: (0, i, j)), tile, tile, tile],
        out_specs=[tile] * 4, out_shape=[_sds((R, C), f32)] * 4,
        compiler_params=_cparams(("arbitrary", "arbitrary")),
    )(parts, w, m, v)


def _sum_parts(name, parts):
    P, R, C = parts.shape
    tr = 256 if R % 256 == 0 else R

    def body(p_ref, o_ref):
        g = p_ref[0]
        for s in range(1, P):
            g = g + p_ref[s]
        o_ref[...] = g

    return pl.pallas_call(
        body, name=name, grid=(R // tr,),
        in_specs=[pl.BlockSpec((P, tr, C), lambda i: (0, i, 0))], out_specs=pl.BlockSpec((tr, C), lambda i: (i, 0)),
        out_shape=_sds((R, C), f32), compiler_params=_cparams(("arbitrary",)),
    )(parts)


def _pad_to(a, n, axis):
    if a.shape[axis] == n:
        return a
    cfg = [(0, 0)] * a.ndim
    cfg[axis] = (0, n - a.shape[axis])
    return jnp.pad(a, cfg)


def _pack(arrs):
    flat = [_pad_to(a.reshape(-1), -(-a.size // 128) * 128, 0) for a in arrs]
    rows = jnp.concatenate(flat).reshape(-1, 128)
    return _pad_to(rows, -(-rows.shape[0] // 256) * 256, 0)


def _unpack(slab, shapes):
    flat = slab.reshape(-1)
    out, o = [], 0
    for s in shapes:
        n = math.prod(s)
        out.append(flat[o:o + n].reshape(s))
        o += -(-n // 128) * 128
    return out


_NAMES = ['norm1_g', 'w_in', 'conv_a_w', 'conv_a_b', 'ln_a_g', 'ln_a_b', 'ln_b_g', 'ln_b_b', 'w_spatial', 'b_spatial',
          'conv_c_w', 'conv_c_b', 'dt_bias', 'a_log', 'd_skip', 'norm_c_g', 'w_out', 'norm2_g', 'w_ff1', 'w_ff2', 'final_g']
_REPL = ['norm1_g', 'conv_a_b', 'ln_a_g', 'ln_a_b', 'ln_b_g', 'ln_b_b', 'w_spatial', 'b_spatial', 'conv_c_b',
         'dt_bias', 'a_log', 'd_skip', 'norm_c_g', 'norm2_g']
_CONVW = ['conv_a_w', 'conv_c_w']
_BIG = ['w_in', 'w_out', 'w_ff1', 'w_ff2']
_BIG_T = {'w_in': True, 'w_out': False, 'w_ff1': True, 'w_ff2': False}


def _row128(v):
    return _pad_to(v.reshape(1, -1), 128, 1)


def _step(p, m, v, x, loss_target):
    nb, S, D = x.shape
    T = nb * S
    depth = p['norm1_g'].shape[0]
    a_w = p['conv_a_b'].shape[1]
    b_w = p['ln_b_g'].shape[1]
    nh = p['dt_bias'].shape[1]
    c_w = p['norm_c_g'].shape[1]
    xw = p['conv_c_b'].shape[1]
    ngrp = 2
    nst = (xw - c_w) // (2 * ngrp)
    d_in = p['w_in'].shape[2] * N_DEV
    main = d_in - nh
    assert main == 2 * a_w + 2 * b_w + c_w + xw and 2 * a_w == 2 * b_w == c_w and xw % c_w == c_w // 2
    me = 4 * lax.axis_index("x") + 2 * lax.axis_index("y") + lax.axis_index("c")

    x2 = x.reshape(T, D)
    tgt = loss_target.reshape(T, D)

    def shards(i, z=None):
        z = 0.0 if z is None else z
        return [(p['w_in'][i].T + z).astype(bf16), (p['w_out'][i] + z).astype(bf16), (p['w_ff1'][i].T + z).astype(bf16),
                (p['w_ff2'][i] + z).astype(bf16), p['conv_a_w'][i], p['conv_c_w'][i]]

    def gathered_in(wt, ca, cc):
        wt = wt.reshape(d_in, D)
        ca = jnp.transpose(ca, (1, 0, 2)).reshape(KA, a_w)
        cc = jnp.transpose(cc, (1, 0, 2)).reshape(KC, xw)
        return dict(wt_main=wt[:main], wt_dt=_pad_to(wt[main:], 128, 0), ca=_pad_to(ca, 32, 0), cc=_pad_to(cc, 8, 0))

    def gathered(got):
        return dict(gathered_in(got[0], got[4], got[5]), wout=got[1].reshape(-1, D), w1t=got[2].reshape(-1, D),
                    w2=got[3].reshape(-1, D))

    sh0 = shards(0)
    h0a, tok = _exchange_start("gather_w0a_start", [sh0[0], sh0[4], sh0[5]], False)
    sh0 = shards(0, tok[0, 0])
    h0b, tokb = _exchange_start("gather_w0b_start", [sh0[1]], False)
    h0c, tokc = _exchange_start("gather_w0c_start", [sh0[2]], False)
    h0d, tokd = _exchange_start("gather_w0d_start", [sh0[3]], False)
    W = [gathered_in(*_exchange_wait("gather_w0a_wait", h0a, [tokb, tokc, tokd]))]

    saved = []
    xc = x2
    for i in range(depth):
        w = W[i]
        h1 = _rms_fwd(xc, p['norm1_g'][i])
        tok = None
        if i + 1 < depth:
            handle, tok = _exchange_start("gather_w%d_start" % (i + 1), shards(i + 1), False)
        (proj,) = _mm("mm_proj", h1, w['wt_main'], "nt", [f32], dep=tok)
        (pdt,) = _mm("mm_pdt", h1, w['wt_dt'], "nt", [f32])
        ya = _conv_fwd("confa_fwd", proj, 0, w['ca'], p['conv_a_b'][i], KA, True, nb, p['ln_a_g'][i], p['ln_a_b'][i])
        yb = _gmlp_fwd(proj, 1, p['ln_b_g'][i], p['ln_b_b'][i], p['w_spatial'][i], p['b_spatial'][i])
        xbc = _conv_fwd("convc_fwd", proj, 2, w['cc'], p['conv_c_b'][i], KC, False, nb)
        dtb, alog, dsk = _row128(p['dt_bias'][i]), _row128(p['a_log'][i]), _row128(p['d_skip'][i])
        yc, sin = _ssd_fwd(xbc, proj, 2, pdt, dtb, alog, dsk, p['norm_c_g'][i], nh, ngrp, nst, nb)
        ycat = jnp.concatenate([ya, yb, yc], axis=1)
        if i == 0:
            w['wout'] = _exchange_wait("gather_w0b_wait", h0b, ycat)[0].reshape(-1, D)
        (xm,) = _mm("mm_out", ycat, w['wout'], "nn", [f32], _ep_add, (xc,))
        h2 = _rms_fwd(xm, p['norm2_g'][i])
        if i == 0:
            w['w1t'] = _exchange_wait("gather_w0c_wait", h0c, h2)[0].reshape(-1, D)
        f, a = _mm("mm_ff1", h2, w['w1t'], "nt", [f32, bf16], _ep_relu2)
        if i == 0:
            w['w2'] = _exchange_wait("gather_w0d_wait", h0d, a)[0].reshape(-1, D)
        (xo,) = _mm("mm_ff2", a, w['w2'], "nn", [f32], _ep_add, (xm,))
        saved.append(dict(x_in=xc, h1=h1, proj=proj, pdt=pdt, xbc=xbc, sin=sin, ycat=ycat, xm=xm, h2=h2, f=f, a=a,
                          dtb=dtb, alog=alog, dsk=dsk))
        xc = xo
        if i + 1 < depth:
            W.append(gathered(_exchange_wait("gather_w%d_wait" % (i + 1), handle, xo)))

    lp, dx, dfinal = _loss_head(xc, p['final_g'], tgt)
    loss = lax.psum(lp[0, 0], ("x", "y", "c"))

    out = {}
    kinds = ("grad", "delta", "new_m", "new_v")
    names1 = _REPL + _CONVW

    started, small = [], [None] * depth

    def send(n, i, g):
        handle, token = _exchange_start("scatter_%s_%d_start" % (n, i), [g.reshape(N_DEV, -1, D)], True)
        started.append((n, i, handle))
        return token

    tok = None
    for i in reversed(range(depth)):
        w, sv = W[i], saved[i]
        (df,) = _mm("mm_df", dx, w['w2'], "nt", [bf16], _ep_drelu2, (sv['f'],), dep=tok)
        (gw2,) = _mm("mm_gw2", sv['a'], dx, "tn", [bf16])
        tok = send('w_ff2', i, gw2)
        (dh2,) = _mm("mm_dh2", df, w['w1t'], "nn", [f32], dep=tok)
        (gw1t,) = _mm("mm_gw1", df, sv['h2'], "tn", [bf16])
        tok = send('w_ff1', i, gw1t)
        dxm, dg2 = _rms_bwd(sv['xm'], p['norm2_g'][i], dh2, dx)
        (dycat,) = _mm("mm_dycat", dxm, w['wout'], "nt", [f32], dep=tok)
        (gwout,) = _mm("mm_gwout", sv['ycat'], dxm, "tn", [bf16])
        tok = send('w_out', i, gwout)
        da, dwa, dba, dlag, dlab = _conv_bwd("confa_bwd", sv['proj'], 0, w['ca'], p['conv_a_b'][i] + tok[0, 0], dycat, 0, KA,
                                             True, nb, p['ln_a_g'][i], p['ln_a_b'][i])
        dbb, dlbg, dlbb, dws, dbs = _gmlp_bwd(sv['proj'], 1, p['ln_b_g'][i], p['ln_b_b'][i], p['w_spatial'][i],
                                              p['b_spatial'][i], dycat, 1)
        dxbc, dz, ddt, ddtb, dalog, ddsk, dng = _ssd_bwd(sv['xbc'], sv['proj'], 2, sv['pdt'], sv['dtb'], sv['alog'], sv['dsk'],
                                                         p['norm_c_g'][i], sv['sin'], dycat, 1, nh, ngrp, nst, nb)
        dxbcp, dwc, dbc = _conv_bwd("convc_bwd", sv['proj'], 2, w['cc'], p['conv_c_b'][i], dxbc, 0, KC, False, nb)
        dproj = jnp.concatenate([da, dbb, dz, dxbcp], axis=1)
        (dh_main,) = _mm("mm_dh1", dproj, w['wt_main'], "nn", [f32])
        (dh,) = _mm("mm_dh1dt", ddt, w['wt_dt'], "nn", [f32], _ep_add, (dh_main,))
        (gwt_main,) = _mm("mm_gwin", dproj, sv['h1'], "tn", [bf16])
        (gwt_dt,) = _mm("mm_gwdt", ddt, sv['h1'], "tn", [bf16])
        tok = send('w_in', i, jnp.concatenate([gwt_main, gwt_dt[:nh]], axis=0))
        dx, dg1 = _rms_bwd(sv['x_in'], p['norm1_g'][i] + tok[0, 0], dh, dxm)

        gi = dict(norm1_g=dg1[0], norm2_g=dg2[0], conv_a_w=dwa[:KA], conv_a_b=dba[0], ln_a_g=dlag[0], ln_a_b=dlab[0],
                  ln_b_g=dlbg[0], ln_b_b=dlbb[0], w_spatial=dws, b_spatial=dbs, conv_c_w=dwc[:KC], conv_c_b=dbc[0],
                  dt_bias=ddtb[0, :nh], a_log=dalog[0, :nh], d_skip=ddsk[0, :nh], norm_c_g=dng[0])
        parts_i = [gi[n] for n in names1] + ([dfinal[0]] if i == depth - 1 else [])
        handle, tok = _exchange_start("gather_g%d_start" % i, [_pack(parts_i)], False)
        small[i] = ([a.shape for a in parts_i], handle)
    grad_x = dx.reshape(nb, S, D)

    dep = [dx, tok]
    for n, i, handle in started:
        (parts,) = _exchange_wait("scatter_%s_%d_wait" % (n, i), handle, dep)
        tr = (lambda t: t.T) if _BIG_T[n] else (lambda t: t)
        res = _adam("adam_" + n, parts, tr(p[n][i]), tr(m[n][i]), tr(v[n][i]))
        for kind, r in zip(kinds, res):
            out.setdefault((kind, n), [None] * depth)[i] = tr(r)
        dep = res[3]

    for i in reversed(range(depth)):
        last = i == depth - 1
        shapes1, handle = small[i]
        (parts,) = _exchange_wait("gather_g%d_wait" % i, handle, dep)
        gl = _unpack(_sum_parts("sum_small", parts), shapes1)
        gd = dict(zip(names1, gl))
        for n in _CONVW:
            cw_shard = p[n].shape[2]
            gd[n] = lax.dynamic_slice_in_dim(gd[n], me * cw_shard, cw_shard, axis=1)
        slab = lambda q: _pack([q[n][i] for n in names1] + ([q['final_g']] if last else []))
        g2 = _pack([gd[n] for n in names1] + ([gl[-1]] if last else []))
        res = _adam("adam_small", g2[None], slab(p), slab(m), slab(v))
        shapes2 = [p[n].shape[1:] for n in names1] + ([p['final_g'].shape] if last else [])
        for kind, r in zip(kinds, res):
            lst = _unpack(r, shapes2)
            for n, arr in zip(names1, lst):
                out.setdefault((kind, n), [None] * depth)[i] = arr
            if last:
                out[(kind, 'final_g')] = lst[-1]
        dep = res[3]
    for n in _BIG + names1:
        for kind in kinds:
            out[(kind, n)] = jnp.stack(out[(kind, n)])

    flat = [loss, grad_x]
    for kind in ("grad", "delta", "new_m", "new_v"):
        flat += [out[(kind, n)] for n in _NAMES]
    return tuple(flat)


def kernel(x, norm1_g, w_in, conv_a_w, conv_a_b, ln_a_g, ln_a_b, ln_b_g, ln_b_b, w_spatial, b_spatial, conv_c_w, conv_c_b, dt_bias, a_log, d_skip, norm_c_g, w_out, norm2_g, w_ff1, w_ff2, final_g, loss_target, m_norm1_g, m_w_in, m_conv_a_w, m_conv_a_b, m_ln_a_g, m_ln_a_b, m_ln_b_g, m_ln_b_b, m_w_spatial, m_b_spatial, m_conv_c_w, m_conv_c_b, m_dt_bias, m_a_log, m_d_skip, m_norm_c_g, m_w_out, m_norm2_g, m_w_ff1, m_w_ff2, m_final_g, v_norm1_g, v_w_in, v_conv_a_w, v_conv_a_b, v_ln_a_g, v_ln_a_b, v_ln_b_g, v_ln_b_b, v_w_spatial, v_b_spatial, v_conv_c_w, v_conv_c_b, v_dt_bias, v_a_log, v_d_skip, v_norm_c_g, v_w_out, v_norm2_g, v_w_ff1, v_w_ff2, v_final_g):
    p = dict(zip(_NAMES, (norm1_g, w_in, conv_a_w, conv_a_b, ln_a_g, ln_a_b, ln_b_g, ln_b_b, w_spatial, b_spatial, conv_c_w,
                          conv_c_b, dt_bias, a_log, d_skip, norm_c_g, w_out, norm2_g, w_ff1, w_ff2, final_g)))
    m = dict(zip(_NAMES, (m_norm1_g, m_w_in, m_conv_a_w, m_conv_a_b, m_ln_a_g, m_ln_a_b, m_ln_b_g, m_ln_b_b, m_w_spatial,
                          m_b_spatial, m_conv_c_w, m_conv_c_b, m_dt_bias, m_a_log, m_d_skip, m_norm_c_g, m_w_out, m_norm2_g,
                          m_w_ff1, m_w_ff2, m_final_g)))
    v = dict(zip(_NAMES, (v_norm1_g, v_w_in, v_conv_a_w, v_conv_a_b, v_ln_a_g, v_ln_a_b, v_ln_b_g, v_ln_b_b, v_w_spatial,
                          v_b_spatial, v_conv_c_w, v_conv_c_b, v_dt_bias, v_a_log, v_d_skip, v_norm_c_g, v_w_out, v_norm2_g,
                          v_w_ff1, v_w_ff2, v_final_g)))
    return _step(p, m, v, x, loss_target)
```

```python
import functools
import math

import jax
import jax.numpy as jnp
from jax import lax
from jax.experimental import pallas as pl
from jax.experimental.pallas import tpu as pltpu

f32 = jnp.float32
bf16 = jnp.bfloat16
HI = lax.Precision.HIGHEST
EPS = 1e-5
HEAD = 64
CHUNK = 128
KA = 31
KC = 4
N_DEV = 8
VMEM_LIMIT = 56 * 1024 * 1024

ADAM_LR = 0.001
ADAM_B1 = 0.9
ADAM_B2 = 0.999
ADAM_EPS = 1e-08
ADAM_WD = 0.01
ADAM_STEP = 10


def _cparams(sem=None):
    return pltpu.CompilerParams(dimension_semantics=sem, vmem_limit_bytes=VMEM_LIMIT)


def _sds(shape, dtype):
    return jax.ShapeDtypeStruct(shape, dtype)


_DIMS = {"nn": ((1,), (0,)), "nt": ((1,), (1,)), "tn": ((0,), (0,))}


def _tile(n, cap):
    if n <= cap:
        return n
    for d in range(cap - cap % 128, 0, -128):
        if n % d == 0:
            return d
    raise ValueError((n, cap))


def _mm(name, a, b, form, out_dtypes, epilogue=None, extras=(), tm=1024, tn=512, tk=2048, dep=None):
    if form == "tn":
        K, M = a.shape
    else:
        M, K = a.shape
    N = b.shape[0] if form == "nt" else b.shape[1]
    tm, tn, tk = _tile(M, tm), _tile(N, tn), _tile(K, tk)
    nk = K // tk
    ne, no = len(extras), len(out_dtypes)
    deps = () if dep is None else (dep,)
    if epilogue is None:
        epilogue = lambda acc: (acc,)

    def body(a_ref, b_ref, *rest):
        extra_refs = rest[:ne]
        rest = rest[ne + len(deps):]
        out_refs = rest[:no]
        part = lax.dot_general(a_ref[...].astype(bf16), b_ref[...].astype(bf16),
                               (_DIMS[form], ((), ())), preferred_element_type=f32)

        def finish(acc):
            outs = epilogue(acc, *[e[...] for e in extra_refs])
            for o_ref, v in zip(out_refs, outs):
                o_ref[...] = v.astype(o_ref.dtype)

        if nk == 1:
            finish(part)
            return
        acc_ref = rest[no]
        k = pl.program_id(2)

        @pl.when(k == 0)
        def _():
            acc_ref[...] = part

        @pl.when((k > 0) & (k < nk - 1))
        def _():
            acc_ref[...] += part

        @pl.when(k == nk - 1)
        def _():
            finish(acc_ref[...] + part)

    a_spec = pl.BlockSpec((tk, tm), lambda i, j, k: (k, i)) if form == "tn" else pl.BlockSpec((tm, tk), lambda i, j, k: (i, k))
    b_spec = pl.BlockSpec((tn, tk), lambda i, j, k: (j, k)) if form == "nt" else pl.BlockSpec((tk, tn), lambda i, j, k: (k, j))
    mn_spec = pl.BlockSpec((tm, tn), lambda i, j, k: (i, j))
    return pl.pallas_call(
        body, name=name, grid=(M // tm, N // tn, nk),
        in_specs=[a_spec, b_spec] + [mn_spec] * ne + [pl.BlockSpec((8, 128), lambda i, j, k: (0, 0))] * len(deps),
        out_specs=[mn_spec] * no,
        out_shape=[_sds((M, N), d) for d in out_dtypes],
        scratch_shapes=[pltpu.VMEM((tm, tn), f32)] if nk > 1 else [],
        compiler_params=_cparams(("parallel", "parallel", "arbitrary")),
    )(a, b, *extras, *deps)


def _ep_add(acc, r):
    return (acc + r,)


def _ep_relu2(acc):
    r = jnp.maximum(acc, 0.0)
    return acc, r * r


def _ep_drelu2(acc, f):
    return (acc * 2.0 * jnp.maximum(f, 0.0),)


def _rms(x, g):
    return x * lax.rsqrt(jnp.mean(x * x, axis=-1, keepdims=True) + EPS) * g


TT = 512


def _rms_fwd(x, g):
    T, D = x.shape

    def body(x_ref, g_ref, h_ref):
        h_ref[...] = _rms(x_ref[...], g_ref[...]).astype(bf16)

    return pl.pallas_call(
        body, name="rms_fwd", grid=(T // TT,),
        in_specs=[pl.BlockSpec((TT, D), lambda i: (i, 0)), pl.BlockSpec((1, D), lambda i: (0, 0))],
        out_specs=pl.BlockSpec((TT, D), lambda i: (i, 0)),
        out_shape=_sds((T, D), bf16), compiler_params=_cparams(("arbitrary",)),
    )(x, g.reshape(1, D))


def _rms_bwd(x, g, dh, dres):
    T, D = x.shape

    def body(x_ref, g_ref, dh_ref, dres_ref, dx_ref, dg_ref):
        _, vjp = jax.vjp(_rms, x_ref[...], g_ref[...])
        dx, dg = vjp(dh_ref[...])
        dx_ref[...] = dres_ref[...] + dx

        @pl.when(pl.program_id(0) == 0)
        def _():
            dg_ref[...] = jnp.zeros_like(dg_ref)

        dg_ref[0:1, :] += dg

    tile = pl.BlockSpec((TT, D), lambda i: (i, 0))
    return pl.pallas_call(
        body, name="rms_bwd", grid=(T // TT,),
        in_specs=[tile, pl.BlockSpec((1, D), lambda i: (0, 0)), tile, tile],
        out_specs=[tile, pl.BlockSpec((8, D), lambda i: (0, 0))],
        out_shape=[_sds((T, D), f32), _sds((8, D), f32)], compiler_params=_cparams(("arbitrary",)),
    )(x, g.reshape(1, D), dh, dres)


def _loss_head(x, g, tgt):
    T, D = x.shape

    def f(xv, gv, tv):
        e = _rms(xv, gv) - tv
        return 0.5 * jnp.sum(jnp.sum(e * e, axis=-1, keepdims=True) * (1.0 / D), axis=0, keepdims=True)

    def body(x_ref, g_ref, t_ref, loss_ref, dx_ref, dg_ref):
        tv = t_ref[...]
        l, vjp = jax.vjp(lambda xv, gv: f(xv, gv, tv), x_ref[...], g_ref[...])
        dx, dg = vjp(jnp.ones((1, 1), f32))
        dx_ref[...] = dx

        @pl.when(pl.program_id(0) == 0)
        def _():
            dg_ref[...] = jnp.zeros_like(dg_ref)
            loss_ref[...] = jnp.zeros_like(loss_ref)

        dg_ref[0:1, :] += dg
        loss_ref[...] += jnp.broadcast_to(l, loss_ref.shape)

    tile = pl.BlockSpec((TT, D), lambda i: (i, 0))
    return pl.pallas_call(
        body, name="loss_head", grid=(T // TT,),
        in_specs=[tile, pl.BlockSpec((1, D), lambda i: (0, 0)), tile],
        out_specs=[pl.BlockSpec((8, 128), lambda i: (0, 0)), tile, pl.BlockSpec((8, D), lambda i: (0, 0))],
        out_shape=[_sds((8, 128), f32), _sds((T, D), f32), _sds((8, D), f32)],
        compiler_params=_cparams(("arbitrary",)),
    )(x, g.reshape(1, D), tgt)


TB = 256


def _glu(a_val, a_gate):
    return a_val * jax.nn.sigmoid(a_gate)


PAIR = 2 * HEAD


def _pair_mean(x, lo):
    s_lo = jnp.sum(jnp.where(lo, x, 0.0), axis=-1, keepdims=True)
    s_hi = jnp.sum(jnp.where(lo, 0.0, x), axis=-1, keepdims=True)
    return jnp.where(lo, s_lo, s_hi) * (1.0 / HEAD)


def _pair_ln(v, g, b):
    lo = lax.broadcasted_iota(jnp.int32, v.shape, 1) < HEAD
    vc = v - _pair_mean(v, lo)
    var = _pair_mean(vc * vc, lo)
    return vc * lax.rsqrt(var + EPS) * g + b


def _ln_silu(v, g, b):
    return jax.nn.silu(_pair_ln(v, g, b))


def _conv_geom(kw):
    halo = 32 if kw > 9 else 8
    return halo, halo - (kw - 1)


def _residues(shifts):
    return sorted({s % 8 for s in shifts} - {0})


def _shift_copies(src_ref, cp_ref, res, rows, ls):
    for j, r in enumerate(res):
        cp_ref[j, :, ls] = src_ref[pl.ds(r, rows), ls]


def _shifted(src_ref, cp_ref, res, shift, size, ls):
    r = shift % 8
    if r == 0:
        return src_ref[pl.ds(shift, size), ls]
    return cp_ref[res.index(r), pl.ds(shift - r, size), ls]


def _conv_taps(hp_ref, hs_ref, w_ref, b_ref, acc_ref, kw, off, halo, width):
    res = _residues(range(off, off + kw))
    for c in range(width // 128):
        ls = pl.ds(c * 128, 128)
        _shift_copies(hp_ref, hs_ref, res, halo + TB, ls)
        acc = jnp.broadcast_to(b_ref[:, ls], (TB, 128))
        for k in range(kw):
            acc = acc + w_ref[k:k + 1, ls] * _shifted(hp_ref, hs_ref, res, off + k, TB, ls)
        acc_ref[:, ls] = acc


def _conv_fwd(name, src, col_block, w, b, kw, conformer, n_seq, ln_g=None, ln_b=None):
    T = src.shape[0]
    cout = w.shape[1]
    cin = 2 * cout if conformer else cout
    halo, off = _conv_geom(kw)
    nblk = T // n_seq // TB
    hb = TB // halo

    def body(cur_ref, halo_ref, w_ref, b_ref, *rest):
        if conformer:
            g_ref, lb_ref, out_ref, hp_ref, acc_ref, hs_ref = rest
        else:
            out_ref, hp_ref, acc_ref, hs_ref = rest
        i = pl.program_id(1)
        first = (i == 0)

        @pl.when((pl.program_id(0) == 0) & first)
        def _():
            hp_ref[pl.ds(halo + TB, 8), :] = jnp.zeros((8, cout), f32)

        if conformer:
            hp_ref[pl.ds(halo, TB), :] = _glu(cur_ref[:, 0:cout], cur_ref[:, cout:cin])
            hh = _glu(halo_ref[:, 0:cout], halo_ref[:, cout:cin])
        else:
            hp_ref[pl.ds(halo, TB), :] = cur_ref[...]
            hh = halo_ref[...]
        hp_ref[pl.ds(0, halo), :] = jnp.where(first, 0.0, hh)
        _conv_taps(hp_ref, hs_ref, w_ref, b_ref, acc_ref, kw, off, halo, cout)
        if conformer:
            for q in range(cout // PAIR):
                ls = pl.ds(q * PAIR, PAIR)
                out_ref[:, ls] = _ln_silu(acc_ref[:, ls], g_ref[:, ls], lb_ref[:, ls]).astype(out_ref.dtype)
        else:
            out_ref[...] = jax.nn.silu(acc_ref[...]).astype(out_ref.dtype)

    nres = len(_residues(range(off, off + kw)))

    row = pl.BlockSpec((1, cout), lambda s, i: (0, 0))
    in_specs = [pl.BlockSpec((TB, cin), lambda s, i: (s * nblk + i, col_block)),
                pl.BlockSpec((halo, cin), lambda s, i: (jnp.maximum((s * nblk + i) * hb - 1, 0), col_block)),
                pl.BlockSpec((w.shape[0], cout), lambda s, i: (0, 0)), row]
    args = [src, src, w, b.reshape(1, cout)]
    if conformer:
        in_specs += [row, row]
        args += [ln_g.reshape(1, cout), ln_b.reshape(1, cout)]
    out_dtype = bf16 if conformer else f32
    return pl.pallas_call(
        body, name=name, grid=(n_seq, nblk), in_specs=in_specs,
        out_specs=pl.BlockSpec((TB, cout), lambda s, i: (s * nblk + i, 0)),
        out_shape=_sds((T, cout), out_dtype),
        scratch_shapes=[pltpu.VMEM((halo + TB + 8, cout), f32), pltpu.VMEM((TB, cout), f32),
                        pltpu.VMEM((nres, halo + TB, cout), f32)],
        compiler_params=_cparams(("arbitrary", "arbitrary")),
    )(*args)


def _conv_bwd(name, src, col_block, w, b, dy, dy_col_block, kw, conformer, n_seq, ln_g=None, ln_b=None):
    T = src.shape[0]
    cout = w.shape[1]
    wrows = w.shape[0]
    cin = 2 * cout if conformer else cout
    halo, off = _conv_geom(kw)
    nblk = T // n_seq // TB
    hb = TB // halo

    def body(cur_ref, halo_ref, w_ref, b_ref, dy_ref, *rest):
        if conformer:
            (g_ref, lb_ref, dsrc_ref, dw_ref, db_ref, dg_ref, dlb_ref,
             hp_ref, acc_ref, dz_ref, dhp_ref, carry_ref, hs_ref, dzs_ref) = rest
        else:
            dsrc_ref, dw_ref, db_ref, hp_ref, acc_ref, dz_ref, dhp_ref, carry_ref, hs_ref, dzs_ref = rest
        s, ii = pl.program_id(0), pl.program_id(1)
        i = nblk - 1 - ii
        first = (i == 0)

        @pl.when((s == 0) & (ii == 0))
        def _():
            dw_ref[...] = jnp.zeros_like(dw_ref)
            db_ref[...] = jnp.zeros_like(db_ref)
            hp_ref[pl.ds(halo + TB, 8), :] = jnp.zeros((8, cout), f32)
            if conformer:
                dg_ref[...] = jnp.zeros_like(dg_ref)
                dlb_ref[...] = jnp.zeros_like(dlb_ref)

        @pl.when(ii == 0)
        def _():
            carry_ref[...] = jnp.zeros_like(carry_ref)
            dz_ref[pl.ds(0, halo), :] = jnp.zeros((halo, cout), f32)
            dz_ref[pl.ds(halo + TB, halo), :] = jnp.zeros((halo, cout), f32)

        if conformer:
            hp_ref[pl.ds(halo, TB), :] = _glu(cur_ref[:, 0:cout], cur_ref[:, cout:cin])
            hh = _glu(halo_ref[:, 0:cout], halo_ref[:, cout:cin])
        else:
            hp_ref[pl.ds(halo, TB), :] = cur_ref[...]
            hh = halo_ref[...]
        hp_ref[pl.ds(0, halo), :] = jnp.where(first, 0.0, hh)
        _conv_taps(hp_ref, hs_ref, w_ref, b_ref, acc_ref, kw, off, halo, cout)

        if conformer:
            for q in range(cout // PAIR):
                ls = pl.ds(q * PAIR, PAIR)
                _, vjp = jax.vjp(_ln_silu, acc_ref[:, ls], g_ref[:, ls], lb_ref[:, ls])
                da, dg, dlb = vjp(dy_ref[:, ls].astype(f32))
                dz_ref[pl.ds(halo, TB), ls] = da
                dg_ref[0:1, ls] += dg
                dlb_ref[0:1, ls] += dlb
        else:
            _, vjp = jax.vjp(jax.nn.silu, acc_ref[...])
            dz_ref[pl.ds(halo, TB), :] = vjp(dy_ref[...].astype(f32))[0]

        res_h = _residues(range(off, off + kw))
        res_z = _residues(range(kw))
        for c in range(cout // 128):
            ls = pl.ds(c * 128, 128)
            _shift_copies(dz_ref, dzs_ref, res_z, halo + TB + halo - 8, ls)
            dacc = dz_ref[pl.ds(halo, TB), ls]
            db_ref[0:1, ls] += jnp.sum(dacc, axis=0, keepdims=True)
            dhp = jnp.zeros((halo + TB, 128), f32)
            for k in range(kw):
                dw_ref[k:k + 1, ls] += jnp.sum(dacc * _shifted(hp_ref, hs_ref, res_h, off + k, TB, ls), axis=0, keepdims=True)
                dhp = dhp + w_ref[k:k + 1, ls] * _shifted(dz_ref, dzs_ref, res_z, kw - 1 - k, halo + TB, ls)
            dhp_ref[:, ls] = dhp
        dhp_ref[pl.ds(TB, halo), :] += carry_ref[...]
        carry_ref[...] = dhp_ref[pl.ds(0, halo), :]
        dcur = dhp_ref[pl.ds(halo, TB), :]
        if conformer:
            _, vjp = jax.vjp(_glu, cur_ref[:, 0:cout], cur_ref[:, cout:cin])
            dval, dgate = vjp(dcur)
            dsrc_ref[:, 0:cout] = dval.astype(dsrc_ref.dtype)
            dsrc_ref[:, cout:cin] = dgate.astype(dsrc_ref.dtype)
        else:
            dsrc_ref[...] = dcur.astype(dsrc_ref.dtype)

    def blk(s, ii):
        return s * nblk + (nblk - 1 - ii)

    row = pl.BlockSpec((1, cout), lambda s, ii: (0, 0))
    acc8 = pl.BlockSpec((8, cout), lambda s, ii: (0, 0))
    in_specs = [pl.BlockSpec((TB, cin), lambda s, ii: (blk(s, ii), col_block)),
                pl.BlockSpec((halo, cin), lambda s, ii: (jnp.maximum(blk(s, ii) * hb - 1, 0), col_block)),
                pl.BlockSpec((wrows, cout), lambda s, ii: (0, 0)), row,
                pl.BlockSpec((TB, cout), lambda s, ii: (blk(s, ii), dy_col_block))]
    args = [src, src, w, b.reshape(1, cout), dy]
    out_specs = [pl.BlockSpec((TB, cin), lambda s, ii: (blk(s, ii), 0)),
                 pl.BlockSpec((wrows, cout), lambda s, ii: (0, 0)), acc8]
    out_shape = [_sds((T, cin), bf16), _sds((wrows, cout), f32), _sds((8, cout), f32)]
    if conformer:
        in_specs += [row, row]
        args += [ln_g.reshape(1, cout), ln_b.reshape(1, cout)]
        out_specs += [acc8, acc8]
        out_shape += [_sds((8, cout), f32), _sds((8, cout), f32)]
    return pl.pallas_call(
        body, name=name, grid=(n_seq, nblk), in_specs=in_specs, out_specs=out_specs, out_shape=out_shape,
        scratch_shapes=[pltpu.VMEM((halo + TB + 8, cout), f32), pltpu.VMEM((TB, cout), f32),
                        pltpu.VMEM((halo + TB + halo, cout), f32), pltpu.VMEM((halo + TB, cout), f32),
                        pltpu.VMEM((halo, cout), f32),
                        pltpu.VMEM((len(_residues(range(off, off + kw))), halo + TB, cout), f32),
                        pltpu.VMEM((len(_residues(range(kw))), halo + TB + halo - 8, cout), f32)],
        compiler_params=_cparams(("arbitrary", "arbitrary")),
    )(*args)


def _gelu(x):
    return 0.5 * x * (1.0 + lax.erf(x * (1.0 / math.sqrt(2.0))))


def _tril_mask(n):
    r = lax.broadcasted_iota(jnp.int32, (n, n), 0)
    c = lax.broadcasted_iota(jnp.int32, (n, n), 1)
    return r >= c


def _head_spread(nh):
    r = lax.broadcasted_iota(jnp.int32, (nh, nh * HEAD), 0)
    c = lax.broadcasted_iota(jnp.int32, (nh, nh * HEAD), 1)
    return (c // HEAD == r).astype(f32)


def _gmlp_bias(bs):
    return lax.dot_general(bs, _head_spread(bs.shape[0]), (((0,), (0,)), ((), ())), precision=HI, preferred_element_type=f32)


def _gmlp_pair(bu, bv, g, b, w_a, w_b, bias):
    lo = lax.broadcasted_iota(jnp.int32, bu.shape, 1) < HEAD
    tril = _tril_mask(CHUNK)
    u = _gelu(bu)
    vb = _pair_ln(_gelu(bv), g, b).astype(bf16)
    mix = jnp.where(lo, jnp.dot(jnp.where(tril, w_a, 0.0).astype(bf16), vb, preferred_element_type=f32),
                    jnp.dot(jnp.where(tril, w_b, 0.0).astype(bf16), vb, preferred_element_type=f32))
    return u * (mix + bias)


def _gmlp_fwd(proj, col_block, ln_g, ln_b, w_s, b_s):
    T = proj.shape[0]
    nh = w_s.shape[0]
    width = nh * HEAD

    def body(p_ref, g_ref, b_ref, w_ref, bs_ref, out_ref, bias_ref):
        @pl.when(pl.program_id(0) == 0)
        def _():
            bias_ref[...] = _gmlp_bias(bs_ref[...])

        for q in range(nh // 2):
            ls = pl.ds(q * PAIR, PAIR)
            lv = pl.ds(width + q * PAIR, PAIR)
            out_ref[:, ls] = _gmlp_pair(p_ref[:, ls], p_ref[:, lv], g_ref[:, ls], b_ref[:, ls], w_ref[2 * q], w_ref[2 * q + 1],
                                        bias_ref[:, ls]).astype(out_ref.dtype)

    row = pl.BlockSpec((1, width), lambda i: (0, 0))
    return pl.pallas_call(
        body, name="gmlp_fwd", grid=(T // CHUNK,),
        in_specs=[pl.BlockSpec((CHUNK, 2 * width), lambda i: (i, col_block)), row, row,
                  pl.BlockSpec((nh, CHUNK, CHUNK), lambda i: (0, 0, 0)), pl.BlockSpec((nh, CHUNK), lambda i: (0, 0))],
        out_specs=pl.BlockSpec((CHUNK, width), lambda i: (i, 0)),
        out_shape=_sds((T, width), bf16), scratch_shapes=[pltpu.VMEM((CHUNK, width), f32)],
        compiler_params=_cparams(("arbitrary",)),
    )(proj, ln_g.reshape(1, width), ln_b.reshape(1, width), w_s, b_s)


def _gmlp_bwd(proj, col_block, ln_g, ln_b, w_s, b_s, dy, dy_col_block):
    T = proj.shape[0]
    nh = w_s.shape[0]
    width = nh * HEAD
    nstep = T // CHUNK

    def body(p_ref, g_ref, b_ref, w_ref, bs_ref, dy_ref, dp_ref, dg_ref, db_ref, dw_ref, dbst_ref, bias_ref, dbias_ref):
        @pl.when(pl.program_id(0) == 0)
        def _():
            dg_ref[...] = jnp.zeros_like(dg_ref)
            db_ref[...] = jnp.zeros_like(db_ref)
            dw_ref[...] = jnp.zeros_like(dw_ref)
            dbias_ref[...] = jnp.zeros_like(dbias_ref)
            bias_ref[...] = _gmlp_bias(bs_ref[...])

        for q in range(nh // 2):
            ls = pl.ds(q * PAIR, PAIR)
            lv = pl.ds(width + q * PAIR, PAIR)
            _, vjp = jax.vjp(_gmlp_pair, p_ref[:, ls], p_ref[:, lv], g_ref[:, ls], b_ref[:, ls], w_ref[2 * q], w_ref[2 * q + 1],
                             bias_ref[:, ls])
            dbu, dbv, dg, db, dw_a, dw_b, dbias = vjp(dy_ref[:, ls].astype(f32))
            dp_ref[:, ls] = dbu.astype(dp_ref.dtype)
            dp_ref[:, lv] = dbv.astype(dp_ref.dtype)
            dg_ref[0:1, ls] += dg
            db_ref[0:1, ls] += db
            dw_ref[2 * q] += dw_a
            dw_ref[2 * q + 1] += dw_b
            dbias_ref[:, ls] += dbias

        @pl.when(pl.program_id(0) == nstep - 1)
        def _():
            dbst_ref[...] = lax.dot_general(dbias_ref[...], _head_spread(nh), (((1,), (1,)), ((), ())),
                                            precision=HI, preferred_element_type=f32)

    row = pl.BlockSpec((1, width), lambda i: (0, 0))
    acc8 = pl.BlockSpec((8, width), lambda i: (0, 0))
    wspec = pl.BlockSpec((nh, CHUNK, CHUNK), lambda i: (0, 0, 0))
    res = pl.pallas_call(
        body, name="gmlp_bwd", grid=(nstep,),
        in_specs=[pl.BlockSpec((CHUNK, 2 * width), lambda i: (i, col_block)), row, row, wspec,
                  pl.BlockSpec((nh, CHUNK), lambda i: (0, 0)), pl.BlockSpec((CHUNK, width), lambda i: (i, dy_col_block))],
        out_specs=[pl.BlockSpec((CHUNK, 2 * width), lambda i: (i, 0)), acc8, acc8, wspec,
                   pl.BlockSpec((CHUNK, nh), lambda i: (0, 0))],
        out_shape=[_sds((T, 2 * width), bf16), _sds((8, width), f32), _sds((8, width), f32),
                   _sds((nh, CHUNK, CHUNK), f32), _sds((CHUNK, nh), f32)],
        scratch_shapes=[pltpu.VMEM((CHUNK, width), f32), pltpu.VMEM((CHUNK, width), f32)],
        compiler_params=_cparams(("arbitrary",)),
    )(proj, ln_g.reshape(1, width), ln_b.reshape(1, width), w_s, b_s, dy)
    return res[0], res[1], res[2], res[3], res[4].T


def _sel_col(x, h):
    lane = lax.broadcasted_iota(jnp.int32, x.shape, 1)
    return jnp.sum(jnp.where(lane == h, x, 0.0), axis=1, keepdims=True)


def _sel_row(x, h):
    sub = lax.broadcasted_iota(jnp.int32, x.shape, 0)
    return jnp.sum(jnp.where(sub == h, x, 0.0), axis=0, keepdims=True)


def _ssd_chunk(nh, ngrp, xs_l, z_l, b_l, c_l, dtraw, dtb, alog, dskip, ng_l, prev_l):
    hg = nh // ngrp
    tril = _tril_mask(CHUNK)
    tl = tril.astype(f32)
    lo = lax.broadcasted_iota(jnp.int32, (CHUNK, PAIR), 1) < HEAD
    lo_row = lo[0:1, :]
    dt = jax.nn.softplus(dtraw + dtb)
    a = dt * (-jnp.exp(alog))
    cs = jnp.dot(tl, a, precision=HI, preferred_element_type=f32)
    cst = lax.dot_general(a, tl, (((0,), (1,)), ((), ())), precision=HI, preferred_element_type=f32)
    cb_l = [lax.dot_general(c_l[g].astype(bf16), b_l[g].astype(bf16), (((1,), (1,)), ((), ())),
                            preferred_element_type=f32) for g in range(ngrp)]
    yz_l, new_prev = [], []
    for q in range(nh // 2):
        g = (2 * q) // hg
        cols = []
        for h in (2 * q, 2 * q + 1):
            cs_h = _sel_col(cs, h)
            tot = _sel_row(cs_h, CHUNK - 1)
            seg = jnp.where(tril, cs_h - _sel_row(cst, h), 0.0)
            lmat = jnp.where(tril, jnp.exp(seg), 0.0)
            cols.append((_sel_col(dt, h), cs_h, tot, lmat, _sel_col(dskip, h)))
        (dt_a, cs_a, tot_a, l_a, dsk_a), (dt_b, cs_b, tot_b, l_b, dsk_b) = cols
        xs = xs_l[q]
        x = xs * jnp.where(lo, dt_a, dt_b)
        xb = x.astype(bf16)
        ydiag = jnp.where(lo, jnp.dot((cb_l[g] * l_a).astype(bf16), xb, preferred_element_type=f32),
                          jnp.dot((cb_l[g] * l_b).astype(bf16), xb, preferred_element_type=f32))
        yoff = (jnp.dot(c_l[g].astype(bf16), prev_l[q].astype(bf16), preferred_element_type=f32)
                * jnp.where(lo, jnp.exp(cs_a), jnp.exp(cs_b)))
        xdec = x * jnp.where(lo, jnp.exp(tot_a - cs_a), jnp.exp(tot_b - cs_b))
        st = lax.dot_general(b_l[g].astype(bf16), xdec.astype(bf16), (((0,), (0,)), ((), ())),
                             preferred_element_type=f32)
        new_prev.append(prev_l[q] * jnp.where(lo_row, jnp.exp(tot_a), jnp.exp(tot_b)) + st)
        y = ydiag + yoff + jnp.where(lo_row, dsk_a, dsk_b) * xs
        yz_l.append(y * jax.nn.silu(z_l[q]))
    out = [None] * (nh // 2)
    qg = hg // 2
    for g in range(ngrp):
        ssq = sum(jnp.sum(yz_l[q] * yz_l[q], axis=-1, keepdims=True) for q in range(g * qg, (g + 1) * qg))
        r = lax.rsqrt(ssq * (1.0 / (hg * HEAD)) + EPS)
        for q in range(g * qg, (g + 1) * qg):
            out[q] = yz_l[q] * r * ng_l[q]
    return out, new_prev


def _ssd_read(nh, ngrp, nst, xbc_ref, z_ref, ng_ref, st_ref):
    cw = nh * HEAD
    xs_l = [xbc_ref[:, pl.ds(q * PAIR, PAIR)] for q in range(nh // 2)]
    b_l = [xbc_ref[:, pl.ds(cw + g * nst, nst)] for g in range(ngrp)]
    c_l = [xbc_ref[:, pl.ds(cw + ngrp * nst + g * nst, nst)] for g in range(ngrp)]
    z_l = [z_ref[:, pl.ds(q * PAIR, PAIR)] for q in range(nh // 2)]
    ng_l = [ng_ref[:, pl.ds(q * PAIR, PAIR)] for q in range(nh // 2)]
    prev_l = [st_ref[:, pl.ds(q * PAIR, PAIR)] for q in range(nh // 2)]
    return xs_l, z_l, b_l, c_l, ng_l, prev_l


def _ssd_fwd(xbc, proj, z_col_block, pdt, dtb, alog, dskip, ng, nh, ngrp, nst, n_seq):
    T = xbc.shape[0]
    cw = nh * HEAD
    nchunk = T // n_seq // CHUNK
    assert nst == CHUNK

    def body(xbc_ref, z_ref, dt_ref, dtb_ref, alog_ref, dskip_ref, ng_ref, y_ref, sin_ref, st_ref):
        @pl.when(pl.program_id(1) == 0)
        def _():
            st_ref[...] = jnp.zeros_like(st_ref)

        sin_ref[...] = st_ref[...]
        xs_l, z_l, b_l, c_l, ng_l, prev_l = _ssd_read(nh, ngrp, nst, xbc_ref, z_ref, ng_ref, st_ref)
        y_l, new_prev = _ssd_chunk(nh, ngrp, xs_l, z_l, b_l, c_l, dt_ref[...], dtb_ref[...], alog_ref[...],
                                   dskip_ref[...], ng_l, prev_l)
        for q in range(nh // 2):
            ls = pl.ds(q * PAIR, PAIR)
            y_ref[:, ls] = y_l[q].astype(y_ref.dtype)
            st_ref[:, ls] = new_prev[q]

    def blk(s, c):
        return s * nchunk + c

    prow = pl.BlockSpec((1, 128), lambda s, c: (0, 0))
    return pl.pallas_call(
        body, name="ssd_fwd", grid=(n_seq, nchunk),
        in_specs=[pl.BlockSpec((CHUNK, xbc.shape[1]), lambda s, c: (blk(s, c), 0)),
                  pl.BlockSpec((CHUNK, cw), lambda s, c: (blk(s, c), z_col_block)),
                  pl.BlockSpec((CHUNK, 128), lambda s, c: (blk(s, c), 0)),
                  prow, prow, prow, pl.BlockSpec((1, cw), lambda s, c: (0, 0))],
        out_specs=[pl.BlockSpec((CHUNK, cw), lambda s, c: (blk(s, c), 0)),
                   pl.BlockSpec((nst, cw), lambda s, c: (blk(s, c), 0))],
        out_shape=[_sds((T, cw), bf16), _sds((T, cw), f32)],
        scratch_shapes=[pltpu.VMEM((nst, cw), f32)],
        compiler_params=_cparams(("arbitrary", "arbitrary")),
    )(xbc, proj, pdt, dtb, alog, dskip, ng.reshape(1, cw))


def _ssd_bwd(xbc, proj, z_col_block, pdt, dtb, alog, dskip, ng, sin, dy, dy_col_block, nh, ngrp, nst, n_seq):
    T, xw = xbc.shape
    cw = nh * HEAD
    nchunk = T // n_seq // CHUNK

    def body(xbc_ref, z_ref, dt_ref, dtb_ref, alog_ref, dskip_ref, ng_ref, sin_ref, dy_ref,
             dxbc_ref, dz_ref, ddt_ref, ddtb_ref, dalog_ref, ddskip_ref, dng_ref, dst_ref):
        s, cc = pl.program_id(0), pl.program_id(1)

        @pl.when((s == 0) & (cc == 0))
        def _():
            ddtb_ref[...] = jnp.zeros_like(ddtb_ref)
            dalog_ref[...] = jnp.zeros_like(dalog_ref)
            ddskip_ref[...] = jnp.zeros_like(ddskip_ref)
            dng_ref[...] = jnp.zeros_like(dng_ref)

        @pl.when(cc == 0)
        def _():
            dst_ref[...] = jnp.zeros_like(dst_ref)

        xs_l, z_l, b_l, c_l, ng_l, prev_l = _ssd_read(nh, ngrp, nst, xbc_ref, z_ref, ng_ref, sin_ref)
        _, vjp = jax.vjp(functools.partial(_ssd_chunk, nh, ngrp), xs_l, z_l, b_l, c_l, dt_ref[...], dtb_ref[...],
                         alog_ref[...], dskip_ref[...], ng_l, prev_l)
        dy_l = [dy_ref[:, pl.ds(q * PAIR, PAIR)].astype(f32) for q in range(nh // 2)]
        dst_l = [dst_ref[:, pl.ds(q * PAIR, PAIR)] for q in range(nh // 2)]
        dxs_l, dz_l, db_l, dc_l, ddt, ddtb, dalog, ddskip, dng_l, dprev_l = vjp((dy_l, dst_l))
        for q in range(nh // 2):
            ls = pl.ds(q * PAIR, PAIR)
            dxbc_ref[:, ls] = dxs_l[q]
            dz_ref[:, ls] = dz_l[q].astype(dz_ref.dtype)
            dng_ref[0:1, ls] += dng_l[q]
            dst_ref[:, ls] = dprev_l[q]
        for g in range(ngrp):
            dxbc_ref[:, pl.ds(cw + g * nst, nst)] = db_l[g]
            dxbc_ref[:, pl.ds(cw + ngrp * nst + g * nst, nst)] = dc_l[g]
        ddt_ref[...] = ddt.astype(ddt_ref.dtype)
        ddtb_ref[0:1, :] += ddtb
        dalog_ref[0:1, :] += dalog
        ddskip_ref[0:1, :] += ddskip

    def blk(s, cc):
        return s * nchunk + (nchunk - 1 - cc)

    prow = pl.BlockSpec((1, 128), lambda s, c: (0, 0))
    pacc = pl.BlockSpec((8, 128), lambda s, c: (0, 0))
    return pl.pallas_call(
        body, name="ssd_bwd", grid=(n_seq, nchunk),
        in_specs=[pl.BlockSpec((CHUNK, xw), lambda s, c: (blk(s, c), 0)),
                  pl.BlockSpec((CHUNK, cw), lambda s, c: (blk(s, c), z_col_block)),
                  pl.BlockSpec((CHUNK, 128), lambda s, c: (blk(s, c), 0)),
                  prow, prow, prow, pl.BlockSpec((1, cw), lambda s, c: (0, 0)),
                  pl.BlockSpec((nst, cw), lambda s, c: (blk(s, c), 0)),
                  pl.BlockSpec((CHUNK, cw), lambda s, c: (blk(s, c), dy_col_block))],
        out_specs=[pl.BlockSpec((CHUNK, xw), lambda s, c: (blk(s, c), 0)),
                   pl.BlockSpec((CHUNK, cw), lambda s, c: (blk(s, c), 0)),
                   pl.BlockSpec((CHUNK, 128), lambda s, c: (blk(s, c), 0)),
                   pacc, pacc, pacc, pl.BlockSpec((8, cw), lambda s, c: (0, 0))],
        out_shape=[_sds((T, xw), f32), _sds((T, cw), bf16), _sds((T, 128), bf16),
                   _sds((8, 128), f32), _sds((8, 128), f32), _sds((8, 128), f32), _sds((8, cw), f32)],
        scratch_shapes=[pltpu.VMEM((nst, cw), f32)],
        compiler_params=_cparams(("arbitrary", "arbitrary")),
    )(xbc, proj, pdt, dtb, alog, dskip, ng.reshape(1, cw), sin, dy)


_HBM = pl.BlockSpec(memory_space=pltpu.HBM)
_SEM = pl.BlockSpec(memory_space=pltpu.SEMAPHORE)
_EFFECT = pltpu.SideEffectType.DATAFLOW_SIDE_EFFECTING


def _split_copies(n, scatter, src_refs, land_refs, send_sems, recv_sems):
    npeer = N_DEV - 1
    x, y, c = lax.axis_index("x"), lax.axis_index("y"), lax.axis_index("c")
    me = 4 * x + 2 * y + c
    copies = []
    for i in range(n):
        for k in range(1, N_DEV):
            px = 1 - x if k & 4 else x
            py = 1 - y if k & 2 else y
            pc = 1 - c if k & 1 else c
            src = src_refs[i].at[4 * px + 2 * py + pc] if scatter else src_refs[i]
            copies.append(pltpu.make_async_remote_copy(
                src_ref=src, dst_ref=land_refs[i].at[me],
                send_sem=send_sems.at[i * npeer + k - 1], recv_sem=recv_sems.at[i * npeer + k - 1],
                device_id=(px, py, pc), device_id_type=pl.DeviceIdType.MESH))
    return copies


def _exchange_start(name, arrs, scatter):
    n = len(arrs)
    nsem = n * (N_DEV - 1)
    me = 4 * lax.axis_index("x") + 2 * lax.axis_index("y") + lax.axis_index("c")
    lands = []
    for a in arrs:
        own = lax.dynamic_index_in_dim(a, me, 0, keepdims=True) if scatter else a[None]
        full = lax.empty(a.shape if scatter else (N_DEV,) + a.shape, a.dtype)
        lands.append(lax.dynamic_update_slice(full, own, (me,) + (0,) * (full.ndim - 1)))

    def body(*refs):
        src_refs, land_refs = refs[:n], refs[n:2 * n]
        send_sems, recv_sems = refs[2 * n], refs[2 * n + 1]
        token = refs[-1]
        for cp in _split_copies(n, scatter, src_refs, land_refs, send_sems, recv_sems):
            cp.start()
        token[...] = jnp.zeros_like(token)

    res = pl.pallas_call(
        body, name=name,
        out_shape=(pltpu.SemaphoreType.DMA((nsem,)), pltpu.SemaphoreType.DMA((nsem,)),
                   *[pltpu.HBM(a.shape, a.dtype) for a in arrs], *[pltpu.HBM(l.shape, l.dtype) for l in lands],
                   _sds((8, 128), f32)),
        in_specs=[_HBM] * (2 * n),
        out_specs=(_SEM, _SEM, *[_HBM] * (2 * n), pl.BlockSpec(memory_space=pltpu.VMEM)),
        input_output_aliases={j: 2 + j for j in range(2 * n)},
        compiler_params=pltpu.CompilerParams(has_side_effects=_EFFECT),
    )(*[pltpu.with_memory_space_constraint(a, pltpu.HBM) for a in arrs],
      *[pltpu.with_memory_space_constraint(l, pltpu.HBM) for l in lands])
    return (n, scatter, res[0], res[1], res[2:2 + n], res[2 + n:2 + 2 * n]), res[-1]


def _exchange_wait(name, handle, after):
    n, scatter, send_sems, recv_sems, srcs, lands = handle
    after = list(after) if isinstance(after, (list, tuple)) else [after]

    def body(*refs):
        src_refs, land_refs = refs[:n], refs[n:2 * n]
        for cp in _split_copies(n, scatter, src_refs, land_refs, refs[2 * n], refs[2 * n + 1]):
            cp.wait_send()
            cp.wait_recv()

    res = pl.pallas_call(
        body, name=name,
        out_shape=[pltpu.HBM(a.shape, a.dtype) for a in (*srcs, *lands)],
        in_specs=[_HBM] * (2 * n) + [_SEM, _SEM] + [pl.BlockSpec(memory_space=pl.ANY)] * len(after),
        out_specs=[_HBM] * (2 * n),
        input_output_aliases={j: j for j in range(2 * n)},
        compiler_params=pltpu.CompilerParams(has_side_effects=_EFFECT),
    )(*srcs, *lands, send_sems, recv_sems, *after)
    return res[n:]


def _adam_tiles(R, C):
    if R % 256 == 0:
        return (256, C), (R // 256, 1)
    assert C % 128 == 0
    return (R, 128), (1, C // 128)


def _adam(name, parts, w, m, v):
    P, R, C = parts.shape
    (tr, tc), (gr, gc) = _adam_tiles(R, C)
    c1 = 1.0 / (1.0 - ADAM_B1 ** ADAM_STEP)
    c2 = 1.0 / (1.0 - ADAM_B2 ** ADAM_STEP)

    def body(p_ref, w_ref, m_ref, v_ref, g_ref, d_ref, nm_ref, nv_ref):
        g = p_ref[0].astype(f32)
        for s in range(1, P):
            g = g + p_ref[s].astype(f32)
        nm = ADAM_B1 * m_ref[...] + (1.0 - ADAM_B1) * g
        nv = ADAM_B2 * v_ref[...] + (1.0 - ADAM_B2) * (g * g)
        g_ref[...] = g
        nm_ref[...] = nm
        nv_ref[...] = nv
        d_ref[...] = -ADAM_LR * ((nm * c1) / (jnp.sqrt(nv * c2) + ADAM_EPS) + ADAM_WD * w_ref[...])

    tile = pl.BlockSpec((tr, tc), lambda i, j: (i, j))
    return pl.pallas_call(
        body, name=name, grid=(gr, gc),
        in_specs=[pl.BlockSpec((P, tr, tc), lambda i, j: (0, i, j)), tile, tile, tile],
        out_specs=[tile] * 4, out_shape=[_sds((R, C), f32)] * 4,
        compiler_params=_cparams(("arbitrary", "arbitrary")),
    )(parts, w, m, v)


def _sum_parts(name, parts):
    P, R, C = parts.shape
    tr = 256 if R % 256 == 0 else R

    def body(p_ref, o_ref):
        g = p_ref[0]
        for s in range(1, P):
            g = g + p_ref[s]
        o_ref[...] = g

    return pl.pallas_call(
        body, name=name, grid=(R // tr,),
        in_specs=[pl.BlockSpec((P, tr, C), lambda i: (0, i, 0))], out_specs=pl.BlockSpec((tr, C), lambda i: (i, 0)),
        out_shape=_sds((R, C), f32), compiler_params=_cparams(("arbitrary",)),
    )(parts)


def _pad_to(a, n, axis):
    if a.shape[axis] == n:
        return a
    cfg = [(0, 0)] * a.ndim
    cfg[axis] = (0, n - a.shape[axis])
    return jnp.pad(a, cfg)


def _pack(arrs):
    flat = [_pad_to(a.reshape(-1), -(-a.size // 128) * 128, 0) for a in arrs]
    rows = jnp.concatenate(flat).reshape(-1, 128)
    return _pad_to(rows, -(-rows.shape[0] // 256) * 256, 0)


def _unpack(slab, shapes):
    flat = slab.reshape(-1)
    out, o = [], 0
    for s in shapes:
        n = math.prod(s)
        out.append(flat[o:o + n].reshape(s))
        o += -(-n // 128) * 128
    return out


_NAMES = ['norm1_g', 'w_in', 'conv_a_w', 'conv_a_b', 'ln_a_g', 'ln_a_b', 'ln_b_g', 'ln_b_b', 'w_spatial', 'b_spatial',
          'conv_c_w', 'conv_c_b', 'dt_bias', 'a_log', 'd_skip', 'norm_c_g', 'w_out', 'norm2_g', 'w_ff1', 'w_ff2', 'final_g']
_REPL = ['norm1_g', 'conv_a_b', 'ln_a_g', 'ln_a_b', 'ln_b_g', 'ln_b_b', 'w_spatial', 'b_spatial', 'conv_c_b',
         'dt_bias', 'a_log', 'd_skip', 'norm_c_g', 'norm2_g']
_CONVW = ['conv_a_w', 'conv_c_w']
_BIG = ['w_in', 'w_out', 'w_ff1', 'w_ff2']
_BIG_T = {'w_in': True, 'w_out': False, 'w_ff1': True, 'w_ff2': False}


def _row128(v):
    return _pad_to(v.reshape(1, -1), 128, 1)


def _step(p, m, v, x, loss_target):
    nb, S, D = x.shape
    T = nb * S
    depth = p['norm1_g'].shape[0]
    a_w = p['conv_a_b'].shape[1]
    b_w = p['ln_b_g'].shape[1]
    nh = p['dt_bias'].shape[1]
    c_w = p['norm_c_g'].shape[1]
    xw = p['conv_c_b'].shape[1]
    ngrp = 2
    nst = (xw - c_w) // (2 * ngrp)
    d_in = p['w_in'].shape[2] * N_DEV
    main = d_in - nh
    assert main == 2 * a_w + 2 * b_w + c_w + xw and 2 * a_w == 2 * b_w == c_w and xw % c_w == c_w // 2
    me = 4 * lax.axis_index("x") + 2 * lax.axis_index("y") + lax.axis_index("c")

    x2 = x.reshape(T, D)
    tgt = loss_target.reshape(T, D)

    def shards(i, z=None):
        z = 0.0 if z is None else z
        return [(p['w_in'][i].T + z).astype(bf16), (p['w_out'][i] + z).astype(bf16), (p['w_ff1'][i].T + z).astype(bf16),
                (p['w_ff2'][i] + z).astype(bf16), p['conv_a_w'][i], p['conv_c_w'][i]]

    def gathered_in(wt, ca, cc):
        wt = wt.reshape(d_in, D)
        ca = jnp.transpose(ca, (1, 0, 2)).reshape(KA, a_w)
        cc = jnp.transpose(cc, (1, 0, 2)).reshape(KC, xw)
        return dict(wt_main=wt[:main], wt_dt=_pad_to(wt[main:], 128, 0), ca=_pad_to(ca, 32, 0), cc=_pad_to(cc, 8, 0))

    def gathered(got):
        return dict(gathered_in(got[0], got[4], got[5]), wout=got[1].reshape(-1, D), w1t=got[2].reshape(-1, D),
                    w2=got[3].reshape(-1, D))

    sh0 = shards(0)
    h0a, tok = _exchange_start("gather_w0a_start", [sh0[0], sh0[4], sh0[5]], False)
    sh0 = shards(0, tok[0, 0])
    h0b, tokb = _exchange_start("gather_w0b_start", [sh0[1]], False)
    h0c, tokc = _exchange_start("gather_w0c_start", [sh0[2]], False)
    h0d, tokd = _exchange_start("gather_w0d_start", [sh0[3]], False)
    W = [gathered_in(*_exchange_wait("gather_w0a_wait", h0a, [tokb, tokc, tokd]))]

    saved = []
    xc = x2
    for i in range(depth):
        w = W[i]
        h1 = _rms_fwd(xc, p['norm1_g'][i])
        tok = None
        if i + 1 < depth:
            handle, tok = _exchange_start("gather_w%d_start" % (i + 1), shards(i + 1), False)
        (proj,) = _mm("mm_proj", h1, w['wt_main'], "nt", [f32], dep=tok)
        (pdt,) = _mm("mm_pdt", h1, w['wt_dt'], "nt", [f32])
        ya = _conv_fwd("confa_fwd", proj, 0, w['ca'], p['conv_a_b'][i], KA, True, nb, p['ln_a_g'][i], p['ln_a_b'][i])
        yb = _gmlp_fwd(proj, 1, p['ln_b_g'][i], p['ln_b_b'][i], p['w_spatial'][i], p['b_spatial'][i])
        xbc = _conv_fwd("convc_fwd", proj, 2, w['cc'], p['conv_c_b'][i], KC, False, nb)
        dtb, alog, dsk = _row128(p['dt_bias'][i]), _row128(p['a_log'][i]), _row128(p['d_skip'][i])
        yc, sin = _ssd_fwd(xbc, proj, 2, pdt, dtb, alog, dsk, p['norm_c_g'][i], nh, ngrp, nst, nb)
        ycat = jnp.concatenate([ya, yb, yc], axis=1)
        if i == 0:
            w['wout'] = _exchange_wait("gather_w0b_wait", h0b, ycat)[0].reshape(-1, D)
        (xm,) = _mm("mm_out", ycat, w['wout'], "nn", [f32], _ep_add, (xc,))
        h2 = _rms_fwd(xm, p['norm2_g'][i])
        if i == 0:
            w['w1t'] = _exchange_wait("gather_w0c_wait", h0c, h2)[0].reshape(-1, D)
        f, a = _mm("mm_ff1", h2, w['w1t'], "nt", [f32, bf16], _ep_relu2)
        if i == 0:
            w['w2'] = _exchange_wait("gather_w0d_wait", h0d, a)[0].reshape(-1, D)
        (xo,) = _mm("mm_ff2", a, w['w2'], "nn", [f32], _ep_add, (xm,))
        saved.append(dict(x_in=xc, h1=h1, proj=proj, pdt=pdt, xbc=xbc, sin=sin, ycat=ycat, xm=xm, h2=h2, f=f, a=a,
                          dtb=dtb, alog=alog, dsk=dsk))
        xc = xo
        if i + 1 < depth:
            W.append(gathered(_exchange_wait("gather_w%d_wait" % (i + 1), handle, xo)))

    lp, dx, dfinal = _loss_head(xc, p['final_g'], tgt)
    loss = lax.psum(lp[0, 0], ("x", "y", "c"))

    out = {}
    kinds = ("grad", "delta", "new_m", "new_v")
    names1 = _REPL + _CONVW

    started, small = [], [None] * depth

    def send(n, i, g):
        handle, token = _exchange_start("scatter_%s_%d_start" % (n, i), [g.reshape(N_DEV, -1, D)], True)
        started.append((n, i, handle))
        return token

    tok = None
    for i in reversed(range(depth)):
        w, sv = W[i], saved[i]
        (df,) = _mm("mm_df", dx, w['w2'], "nt", [bf16], _ep_drelu2, (sv['f'],), dep=tok)
        (gw2,) = _mm("mm_gw2", sv['a'], dx, "tn", [bf16])
        tok = send('w_ff2', i, gw2)
        (dh2,) = _mm("mm_dh2", df, w['w1t'], "nn", [f32], dep=tok)
        (gw1t,) = _mm("mm_gw1", df, sv['h2'], "tn", [bf16])
        tok = send('w_ff1', i, gw1t)
        dxm, dg2 = _rms_bwd(sv['xm'], p['norm2_g'][i], dh2, dx)
        (dycat,) = _mm("mm_dycat", dxm, w['wout'], "nt", [f32], dep=tok)
        (gwout,) = _mm("mm_gwout", sv['ycat'], dxm, "tn", [bf16])
        tok = send('w_out', i, gwout)
        da, dwa, dba, dlag, dlab = _conv_bwd("confa_bwd", sv['proj'], 0, w['ca'], p['conv_a_b'][i] + tok[0, 0], dycat, 0, KA,
                                             True, nb, p['ln_a_g'][i], p['ln_a_b'][i])
        dbb, dlbg, dlbb, dws, dbs = _gmlp_bwd(sv['proj'], 1, p['ln_b_g'][i], p['ln_b_b'][i], p['w_spatial'][i],
                                              p['b_spatial'][i], dycat, 1)
        dxbc, dz, ddt, ddtb, dalog, ddsk, dng = _ssd_bwd(sv['xbc'], sv['proj'], 2, sv['pdt'], sv['dtb'], sv['alog'], sv['dsk'],
                                                         p['norm_c_g'][i], sv['sin'], dycat, 1, nh, ngrp, nst, nb)
        dxbcp, dwc, dbc = _conv_bwd("convc_bwd", sv['proj'], 2, w['cc'], p['conv_c_b'][i], dxbc, 0, KC, False, nb)
        dproj = jnp.concatenate([da, dbb, dz, dxbcp], axis=1)
        (dh_main,) = _mm("mm_dh1", dproj, w['wt_main'], "nn", [f32])
        (dh,) = _mm("mm_dh1dt", ddt, w['wt_dt'], "nn", [f32], _ep_add, (dh_main,))
        (gwt_main,) = _mm("mm_gwin", dproj, sv['h1'], "tn", [bf16])
        (gwt_dt,) = _mm("mm_gwdt", ddt, sv['h1'], "tn", [bf16])
        tok = send('w_in', i, jnp.concatenate([gwt_main, gwt_dt[:nh]], axis=0))
        dx, dg1 = _rms_bwd(sv['x_in'], p['norm1_g'][i] + tok[0, 0], dh, dxm)

        gi = dict(norm1_g=dg1[0], norm2_g=dg2[0], conv_a_w=dwa[:KA], conv_a_b=dba[0], ln_a_g=dlag[0], ln_a_b=dlab[0],
                  ln_b_g=dlbg[0], ln_b_b=dlbb[0], w_spatial=dws, b_spatial=dbs, conv_c_w=dwc[:KC], conv_c_b=dbc[0],
                  dt_bias=ddtb[0, :nh], a_log=dalog[0, :nh], d_skip=ddsk[0, :nh], norm_c_g=dng[0])
        parts_i = [gi[n] for n in names1] + ([dfinal[0]] if i == depth - 1 else [])
        handle, tok = _exchange_start("gather_g%d_start" % i, [_pack(parts_i)], False)
        small[i] = ([a.shape for a in parts_i], handle)
    grad_x = dx.reshape(nb, S, D)

    dep = [dx, tok]
    for n, i, handle in started:
        (parts,) = _exchange_wait("scatter_%s_%d_wait" % (n, i), handle, dep)
        tr = (lambda t: t.T) if _BIG_T[n] else (lambda t: t)
        res = _adam("adam_" + n, parts, tr(p[n][i]), tr(m[n][i]), tr(v[n][i]))
        for kind, r in zip(kinds, res):
            out.setdefault((kind, n), [None] * depth)[i] = tr(r)
        dep = res[3]

    for i in reversed(range(depth)):
        last = i == depth - 1
        shapes1, handle = small[i]
        (parts,) = _exchange_wait("gather_g%d_wait" % i, handle, dep)
        gl = _unpack(_sum_parts("sum_small", parts), shapes1)
        gd = dict(zip(names1, gl))
        for n in _CONVW:
            cw_shard = p[n].shape[2]
            gd[n] = lax.dynamic_slice_in_dim(gd[n], me * cw_shard, cw_shard, axis=1)
        slab = lambda q: _pack([q[n][i] for n in names1] + ([q['final_g']] if last else []))
        g2 = _pack([gd[n] for n in names1] + ([gl[-1]] if last else []))
        res = _adam("adam_small", g2[None], slab(p), slab(m), slab(v))
        shapes2 = [p[n].shape[1:] for n in names1] + ([p['final_g'].shape] if last else [])
        for kind, r in zip(kinds, res):
            lst = _unpack(r, shapes2)
            for n, arr in zip(names1, lst):
                out.setdefault((kind, n), [None] * depth)[i] = arr
            if last:
                out[(kind, 'final_g')] = lst[-1]
        dep = res[3]
    for n in _BIG + names1:
        for kind in kinds:
            out[(kind, n)] = jnp.stack(out[(kind, n)])

    flat = [loss, grad_x]
    for kind in ("grad", "delta", "new_m", "new_v"):
        flat += [out[(kind, n)] for n in _NAMES]
    return tuple(flat)


def kernel(x, norm1_g, w_in, conv_a_w, conv_a_b, ln_a_g, ln_a_b, ln_b_g, ln_b_b, w_spatial, b_spatial, conv_c_w, conv_c_b, dt_bias, a_log, d_skip, norm_c_g, w_out, norm2_g, w_ff1, w_ff2, final_g, loss_target, m_norm1_g, m_w_in, m_conv_a_w, m_conv_a_b, m_ln_a_g, m_ln_a_b, m_ln_b_g, m_ln_b_b, m_w_spatial, m_b_spatial, m_conv_c_w, m_conv_c_b, m_dt_bias, m_a_log, m_d_skip, m_norm_c_g, m_w_out, m_norm2_g, m_w_ff1, m_w_ff2, m_final_g, v_norm1_g, v_w_in, v_conv_a_w, v_conv_a_b, v_ln_a_g, v_ln_a_b, v_ln_b_g, v_ln_b_b, v_w_spatial, v_b_spatial, v_conv_c_w, v_conv_c_b, v_dt_bias, v_a_log, v_d_skip, v_norm_c_g, v_w_out, v_norm2_g, v_w_ff1, v_w_ff2, v_final_g):
    p = dict(zip(_NAMES, (norm1_g, w_in, conv_a_w, conv_a_b, ln_a_g, ln_a_b, ln_b_g, ln_b_b, w_spatial, b_spatial, conv_c_w,
                          conv_c_b, dt_bias, a_log, d_skip, norm_c_g, w_out, norm2_g, w_ff1, w_ff2, final_g)))
    m = dict(zip(_NAMES, (m_norm1_g, m_w_in, m_conv_a_w, m_conv_a_b, m_ln_a_g, m_ln_a_b, m_ln_b_g, m_ln_b_b, m_w_spatial,
                          m_b_spatial, m_conv_c_w, m_conv_c_b, m_dt_bias, m_a_log, m_d_skip, m_norm_c_g, m_w_out, m_norm2_g,
                          m_w_ff1, m_w_ff2, m_final_g)))
    v = dict(zip(_NAMES, (v_norm1_g, v_w_in, v_conv_a_w, v_conv_a_b, v_ln_a_g, v_ln_a_b, v_ln_b_g, v_ln_b_b, v_w_spatial,
                          v_b_spatial, v_conv_c_w, v_conv_c_b, v_dt_bias, v_a_log, v_d_skip, v_norm_c_g, v_w_out, v_norm2_g,
                          v_w_ff1, v_w_ff2, v_final_g)))
    return _step(p, m, v, x, loss_target)
```

```python
import functools
import math

import jax
import jax.numpy as jnp
from jax import lax
from jax.experimental import pallas as pl
from jax.experimental.pallas import tpu as pltpu

f32 = jnp.float32
bf16 = jnp.bfloat16
HI = lax.Precision.HIGHEST
EPS = 1e-5
HEAD = 64
CHUNK = 128
KA = 31
KC = 4
N_DEV = 8
VMEM_LIMIT = 56 * 1024 * 1024

ADAM_LR = 0.001
ADAM_B1 = 0.9
ADAM_B2 = 0.999
ADAM_EPS = 1e-08
ADAM_WD = 0.01
ADAM_STEP = 10


def _cparams(sem=None):
    return pltpu.CompilerParams(dimension_semantics=sem, vmem_limit_bytes=VMEM_LIMIT)


def _sds(shape, dtype):
    return jax.ShapeDtypeStruct(shape, dtype)


_DIMS = {"nn": ((1,), (0,)), "nt": ((1,), (1,)), "tn": ((0,), (0,))}


def _tile(n, cap):
    if n <= cap:
        return n
    for d in range(cap - cap % 128, 0, -128):
        if n % d == 0:
            return d
    raise ValueError((n, cap))


def _mm(name, a, b, form, out_dtypes, epilogue=None, extras=(), tm=1024, tn=512, tk=2048, dep=None):
    if form == "tn":
        K, M = a.shape
    else:
        M, K = a.shape
    N = b.shape[0] if form == "nt" else b.shape[1]
    tm, tn, tk = _tile(M, tm), _tile(N, tn), _tile(K, tk)
    nk = K // tk
    ne, no = len(extras), len(out_dtypes)
    deps = () if dep is None else (dep,)
    if epilogue is None:
        epilogue = lambda acc: (acc,)

    def body(a_ref, b_ref, *rest):
        extra_refs = rest[:ne]
        rest = rest[ne + len(deps):]
        out_refs = rest[:no]
        part = lax.dot_general(a_ref[...].astype(bf16), b_ref[...].astype(bf16),
                               (_DIMS[form], ((), ())), preferred_element_type=f32)

        def finish(acc):
            outs = epilogue(acc, *[e[...] for e in extra_refs])
            for o_ref, v in zip(out_refs, outs):
                o_ref[...] = v.astype(o_ref.dtype)

        if nk == 1:
            finish(part)
            return
        acc_ref = rest[no]
        k = pl.program_id(2)

        @pl.when(k == 0)
        def _():
            acc_ref[...] = part

        @pl.when((k > 0) & (k < nk - 1))
        def _():
            acc_ref[...] += part

        @pl.when(k == nk - 1)
        def _():
            finish(acc_ref[...] + part)

    a_spec = pl.BlockSpec((tk, tm), lambda i, j, k: (k, i)) if form == "tn" else pl.BlockSpec((tm, tk), lambda i, j, k: (i, k))
    b_spec = pl.BlockSpec((tn, tk), lambda i, j, k: (j, k)) if form == "nt" else pl.BlockSpec((tk, tn), lambda i, j, k: (k, j))
    mn_spec = pl.BlockSpec((tm, tn), lambda i, j, k: (i, j))
    return pl.pallas_call(
        body, name=name, grid=(M // tm, N // tn, nk),
        in_specs=[a_spec, b_spec] + [mn_spec] * ne + [pl.BlockSpec((8, 128), lambda i, j, k: (0, 0))] * len(deps),
        out_specs=[mn_spec] * no,
        out_shape=[_sds((M, N), d) for d in out_dtypes],
        scratch_shapes=[pltpu.VMEM((tm, tn), f32)] if nk > 1 else [],
        compiler_params=_cparams(("parallel", "parallel", "arbitrary")),
    )(a, b, *extras, *deps)


def _ep_add(acc, r):
    return (acc + r,)


def _ep_relu2(acc):
    r = jnp.maximum(acc, 0.0)
    return acc, r * r


def _ep_drelu2(acc, f):
    return (acc * 2.0 * jnp.maximum(f, 0.0),)


def _rms(x, g):
    return x * lax.rsqrt(jnp.mean(x * x, axis=-1, keepdims=True) + EPS) * g


TT = 512


def _rms_fwd(x, g):
    T, D = x.shape

    def body(x_ref, g_ref, h_ref):
        h_ref[...] = _rms(x_ref[...], g_ref[...]).astype(bf16)

    return pl.pallas_call(
        body, name="rms_fwd", grid=(T // TT,),
        in_specs=[pl.BlockSpec((TT, D), lambda i: (i, 0)), pl.BlockSpec((1, D), lambda i: (0, 0))],
        out_specs=pl.BlockSpec((TT, D), lambda i: (i, 0)),
        out_shape=_sds((T, D), bf16), compiler_params=_cparams(("arbitrary",)),
    )(x, g.reshape(1, D))


def _rms_bwd(x, g, dh, dres):
    T, D = x.shape

    def body(x_ref, g_ref, dh_ref, dres_ref, dx_ref, dg_ref):
        _, vjp = jax.vjp(_rms, x_ref[...], g_ref[...])
        dx, dg = vjp(dh_ref[...])
        dx_ref[...] = dres_ref[...] + dx

        @pl.when(pl.program_id(0) == 0)
        def _():
            dg_ref[...] = jnp.zeros_like(dg_ref)

        dg_ref[0:1, :] += dg

    tile = pl.BlockSpec((TT, D), lambda i: (i, 0))
    return pl.pallas_call(
        body, name="rms_bwd", grid=(T // TT,),
        in_specs=[tile, pl.BlockSpec((1, D), lambda i: (0, 0)), tile, tile],
        out_specs=[tile, pl.BlockSpec((8, D), lambda i: (0, 0))],
        out_shape=[_sds((T, D), f32), _sds((8, D), f32)], compiler_params=_cparams(("arbitrary",)),
    )(x, g.reshape(1, D), dh, dres)


def _loss_head(x, g, tgt):
    T, D = x.shape

    def f(xv, gv, tv):
        e = _rms(xv, gv) - tv
        return 0.5 * jnp.sum(jnp.sum(e * e, axis=-1, keepdims=True) * (1.0 / D), axis=0, keepdims=True)

    def body(x_ref, g_ref, t_ref, loss_ref, dx_ref, dg_ref):
        tv = t_ref[...]
        l, vjp = jax.vjp(lambda xv, gv: f(xv, gv, tv), x_ref[...], g_ref[...])
        dx, dg = vjp(jnp.ones((1, 1), f32))
        dx_ref[...] = dx

        @pl.when(pl.program_id(0) == 0)
        def _():
            dg_ref[...] = jnp.zeros_like(dg_ref)
            loss_ref[...] = jnp.zeros_like(loss_ref)

        dg_ref[0:1, :] += dg
        loss_ref[...] += jnp.broadcast_to(l, loss_ref.shape)

    tile = pl.BlockSpec((TT, D), lambda i: (i, 0))
    return pl.pallas_call(
        body, name="loss_head", grid=(T // TT,),
        in_specs=[tile, pl.BlockSpec((1, D), lambda i: (0, 0)), tile],
        out_specs=[pl.BlockSpec((8, 128), lambda i: (0, 0)), tile, pl.BlockSpec((8, D), lambda i: (0, 0))],
        out_shape=[_sds((8, 128), f32), _sds((T, D), f32), _sds((8, D), f32)],
        compiler_params=_cparams(("arbitrary",)),
    )(x, g.reshape(1, D), tgt)


TB = 256


def _glu(a_val, a_gate):
    return a_val * jax.nn.sigmoid(a_gate)


PAIR = 2 * HEAD


def _pair_mean(x, lo):
    s_lo = jnp.sum(jnp.where(lo, x, 0.0), axis=-1, keepdims=True)
    s_hi = jnp.sum(jnp.where(lo, 0.0, x), axis=-1, keepdims=True)
    return jnp.where(lo, s_lo, s_hi) * (1.0 / HEAD)


def _pair_ln(v, g, b):
    lo = lax.broadcasted_iota(jnp.int32, v.shape, 1) < HEAD
    vc = v - _pair_mean(v, lo)
    var = _pair_mean(vc * vc, lo)
    return vc * lax.rsqrt(var + EPS) * g + b


def _ln_silu(v, g, b):
    return jax.nn.silu(_pair_ln(v, g, b))


def _conv_geom(kw):
    halo = 32 if kw > 9 else 8
    return halo, halo - (kw - 1)


def _residues(shifts):
    return sorted({s % 8 for s in shifts} - {0})


def _shift_copies(src_ref, cp_ref, res, rows, ls):
    for j, r in enumerate(res):
        cp_ref[j, :, ls] = src_ref[pl.ds(r, rows), ls]


def _shifted(src_ref, cp_ref, res, shift, size, ls):
    r = shift % 8
    if r == 0:
        return src_ref[pl.ds(shift, size), ls]
    return cp_ref[res.index(r), pl.ds(shift - r, size), ls]


def _conv_taps(hp_ref, hs_ref, w_ref, b_ref, acc_ref, kw, off, halo, width):
    res = _residues(range(off, off + kw))
    for c in range(width // 128):
        ls = pl.ds(c * 128, 128)
        _shift_copies(hp_ref, hs_ref, res, halo + TB, ls)
        acc = jnp.broadcast_to(b_ref[:, ls], (TB, 128))
        for k in range(kw):
            acc = acc + w_ref[k:k + 1, ls] * _shifted(hp_ref, hs_ref, res, off + k, TB, ls)
        acc_ref[:, ls] = acc


def _conv_fwd(name, src, col_block, w, b, kw, conformer, n_seq, ln_g=None, ln_b=None):
    T = src.shape[0]
    cout = w.shape[1]
    cin = 2 * cout if conformer else cout
    halo, off = _conv_geom(kw)
    nblk = T // n_seq // TB
    hb = TB // halo

    def body(cur_ref, halo_ref, w_ref, b_ref, *rest):
        if conformer:
            g_ref, lb_ref, out_ref, hp_ref, acc_ref, hs_ref = rest
        else:
            out_ref, hp_ref, acc_ref, hs_ref = rest
        i = pl.program_id(1)
        first = (i == 0)

        @pl.when((pl.program_id(0) == 0) & first)
        def _():
            hp_ref[pl.ds(halo + TB, 8), :] = jnp.zeros((8, cout), f32)

        if conformer:
            hp_ref[pl.ds(halo, TB), :] = _glu(cur_ref[:, 0:cout], cur_ref[:, cout:cin])
            hh = _glu(halo_ref[:, 0:cout], halo_ref[:, cout:cin])
        else:
            hp_ref[pl.ds(halo, TB), :] = cur_ref[...]
            hh = halo_ref[...]
        hp_ref[pl.ds(0, halo), :] = jnp.where(first, 0.0, hh)
        _conv_taps(hp_ref, hs_ref, w_ref, b_ref, acc_ref, kw, off, halo, cout)
        if conformer:
            for q in range(cout // PAIR):
                ls = pl.ds(q * PAIR, PAIR)
                out_ref[:, ls] = _ln_silu(acc_ref[:, ls], g_ref[:, ls], lb_ref[:, ls]).astype(out_ref.dtype)
        else:
            out_ref[...] = jax.nn.silu(acc_ref[...]).astype(out_ref.dtype)

    nres = len(_residues(range(off, off + kw)))

    row = pl.BlockSpec((1, cout), lambda s, i: (0, 0))
    in_specs = [pl.BlockSpec((TB, cin), lambda s, i: (s * nblk + i, col_block)),
                pl.BlockSpec((halo, cin), lambda s, i: (jnp.maximum((s * nblk + i) * hb - 1, 0), col_block)),
                pl.BlockSpec((w.shape[0], cout), lambda s, i: (0, 0)), row]
    args = [src, src, w, b.reshape(1, cout)]
    if conformer:
        in_specs += [row, row]
        args += [ln_g.reshape(1, cout), ln_b.reshape(1, cout)]
    out_dtype = bf16 if conformer else f32
    return pl.pallas_call(
        body, name=name, grid=(n_seq, nblk), in_specs=in_specs,
        out_specs=pl.BlockSpec((TB, cout), lambda s, i: (s * nblk + i, 0)),
        out_shape=_sds((T, cout), out_dtype),
        scratch_shapes=[pltpu.VMEM((halo + TB + 8, cout), f32), pltpu.VMEM((TB, cout), f32),
                        pltpu.VMEM((nres, halo + TB, cout), f32)],
        compiler_params=_cparams(("arbitrary", "arbitrary")),
    )(*args)


def _conv_bwd(name, src, col_block, w, b, dy, dy_col_block, kw, conformer, n_seq, ln_g=None, ln_b=None):
    T = src.shape[0]
    cout = w.shape[1]
    wrows = w.shape[0]
    cin = 2 * cout if conformer else cout
    halo, off = _conv_geom(kw)
    nblk = T // n_seq // TB
    hb = TB // halo

    def body(cur_ref, halo_ref, w_ref, b_ref, dy_ref, *rest):
        if conformer:
            (g_ref, lb_ref, dsrc_ref, dw_ref, db_ref, dg_ref, dlb_ref,
             hp_ref, acc_ref, dz_ref, dhp_ref, carry_ref, hs_ref, dzs_ref) = rest
        else:
            dsrc_ref, dw_ref, db_ref, hp_ref, acc_ref, dz_ref, dhp_ref, carry_ref, hs_ref, dzs_ref = rest
        s, ii = pl.program_id(0), pl.program_id(1)
        i = nblk - 1 - ii
        first = (i == 0)

        @pl.when((s == 0) & (ii == 0))
        def _():
            dw_ref[...] = jnp.zeros_like(dw_ref)
            db_ref[...] = jnp.zeros_like(db_ref)
            hp_ref[pl.ds(halo + TB, 8), :] = jnp.zeros((8, cout), f32)
            if conformer:
                dg_ref[...] = jnp.zeros_like(dg_ref)
                dlb_ref[...] = jnp.zeros_like(dlb_ref)

        @pl.when(ii == 0)
        def _():
            carry_ref[...] = jnp.zeros_like(carry_ref)
            dz_ref[pl.ds(0, halo), :] = jnp.zeros((halo, cout), f32)
            dz_ref[pl.ds(halo + TB, halo), :] = jnp.zeros((halo, cout), f32)

        if conformer:
            hp_ref[pl.ds(halo, TB), :] = _glu(cur_ref[:, 0:cout], cur_ref[:, cout:cin])
            hh = _glu(halo_ref[:, 0:cout], halo_ref[:, cout:cin])
        else:
            hp_ref[pl.ds(halo, TB), :] = cur_ref[...]
            hh = halo_ref[...]
        hp_ref[pl.ds(0, halo), :] = jnp.where(first, 0.0, hh)
        _conv_taps(hp_ref, hs_ref, w_ref, b_ref, acc_ref, kw, off, halo, cout)

        if conformer:
            for q in range(cout // PAIR):
                ls = pl.ds(q * PAIR, PAIR)
                _, vjp = jax.vjp(_ln_silu, acc_ref[:, ls], g_ref[:, ls], lb_ref[:, ls])
                da, dg, dlb = vjp(dy_ref[:, ls].astype(f32))
                dz_ref[pl.ds(halo, TB), ls] = da
                dg_ref[0:1, ls] += dg
                dlb_ref[0:1, ls] += dlb
        else:
            _, vjp = jax.vjp(jax.nn.silu, acc_ref[...])
            dz_ref[pl.ds(halo, TB), :] = vjp(dy_ref[...].astype(f32))[0]

        res_h = _residues(range(off, off + kw))
        res_z = _residues(range(kw))
        for c in range(cout // 128):
            ls = pl.ds(c * 128, 128)
            _shift_copies(dz_ref, dzs_ref, res_z, halo + TB + halo - 8, ls)
            dacc = dz_ref[pl.ds(halo, TB), ls]
            db_ref[0:1, ls] += jnp.sum(dacc, axis=0, keepdims=True)
            dhp = jnp.zeros((halo + TB, 128), f32)
            for k in range(kw):
                dw_ref[k:k + 1, ls] += jnp.sum(dacc * _shifted(hp_ref, hs_ref, res_h, off + k, TB, ls), axis=0, keepdims=True)
                dhp = dhp + w_ref[k:k + 1, ls] * _shifted(dz_ref, dzs_ref, res_z, kw - 1 - k, halo + TB, ls)
            dhp_ref[:, ls] = dhp
        dhp_ref[pl.ds(TB, halo), :] += carry_ref[...]
        carry_ref[...] = dhp_ref[pl.ds(0, halo), :]
        dcur = dhp_ref[pl.ds(halo, TB), :]
        if conformer:
            _, vjp = jax.vjp(_glu, cur_ref[:, 0:cout], cur_ref[:, cout:cin])
            dval, dgate = vjp(dcur)
            dsrc_ref[:, 0:cout] = dval.astype(dsrc_ref.dtype)
            dsrc_ref[:, cout:cin] = dgate.astype(dsrc_ref.dtype)
        else:
            dsrc_ref[...] = dcur.astype(dsrc_ref.dtype)

    def blk(s, ii):
        return s * nblk + (nblk - 1 - ii)

    row = pl.BlockSpec((1, cout), lambda s, ii: (0, 0))
    acc8 = pl.BlockSpec((8, cout), lambda s, ii: (0, 0))
    in_specs = [pl.BlockSpec((TB, cin), lambda s, ii: (blk(s, ii), col_block)),
                pl.BlockSpec((halo, cin), lambda s, ii: (jnp.maximum(blk(s, ii) * hb - 1, 0), col_block)),
                pl.BlockSpec((wrows, cout), lambda s, ii: (0, 0)), row,
                pl.BlockSpec((TB, cout), lambda s, ii: (blk(s, ii), dy_col_block))]
    args = [src, src, w, b.reshape(1, cout), dy]
    out_specs = [pl.BlockSpec((TB, cin), lambda s, ii: (blk(s, ii), 0)),
                 pl.BlockSpec((wrows, cout), lambda s, ii: (0, 0)), acc8]
    out_shape = [_sds((T, cin), bf16), _sds((wrows, cout), f32), _sds((8, cout), f32)]
    if conformer:
        in_specs += [row, row]
        args += [ln_g.reshape(1, cout), ln_b.reshape(1, cout)]
        out_specs += [acc8, acc8]
        out_shape += [_sds((8, cout), f32), _sds((8, cout), f32)]
    return pl.pallas_call(
        body, name=name, grid=(n_seq, nblk), in_specs=in_specs, out_specs=out_specs, out_shape=out_shape,
        scratch_shapes=[pltpu.VMEM((halo + TB + 8, cout), f32), pltpu.VMEM((TB, cout), f32),
                        pltpu.VMEM((halo + TB + halo, cout), f32), pltpu.VMEM((halo + TB, cout), f32),
                        pltpu.VMEM((halo, cout), f32),
                        pltpu.VMEM((len(_residues(range(off, off + kw))), halo + TB, cout), f32),
                        pltpu.VMEM((len(_residues(range(kw))), halo + TB + halo - 8, cout), f32)],
        compiler_params=_cparams(("arbitrary", "arbitrary")),
    )(*args)


def _gelu(x):
    return 0.5 * x * (1.0 + lax.erf(x * (1.0 / math.sqrt(2.0))))


def _tril_mask(n):
    r = lax.broadcasted_iota(jnp.int32, (n, n), 0)
    c = lax.broadcasted_iota(jnp.int32, (n, n), 1)
    return r >= c


def _head_spread(nh):
    r = lax.broadcasted_iota(jnp.int32, (nh, nh * HEAD), 0)
    c = lax.broadcasted_iota(jnp.int32, (nh, nh * HEAD), 1)
    return (c // HEAD == r).astype(f32)


def _gmlp_bias(bs):
    return lax.dot_general(bs, _head_spread(bs.shape[0]), (((0,), (0,)), ((), ())), precision=HI, preferred_element_type=f32)


def _gmlp_pair(bu, bv, g, b, w_a, w_b, bias):
    lo = lax.broadcasted_iota(jnp.int32, bu.shape, 1) < HEAD
    tril = _tril_mask(CHUNK)
    u = _gelu(bu)
    vb = _pair_ln(_gelu(bv), g, b).astype(bf16)
    mix = jnp.where(lo, jnp.dot(jnp.where(tril, w_a, 0.0).astype(bf16), vb, preferred_element_type=f32),
                    jnp.dot(jnp.where(tril, w_b, 0.0).astype(bf16), vb, preferred_element_type=f32))
    return u * (mix + bias)


def _gmlp_fwd(proj, col_block, ln_g, ln_b, w_s, b_s):
    T = proj.shape[0]
    nh = w_s.shape[0]
    width = nh * HEAD

    def body(p_ref, g_ref, b_ref, w_ref, bs_ref, out_ref, bias_ref):
        @pl.when(pl.program_id(0) == 0)
        def _():
            bias_ref[...] = _gmlp_bias(bs_ref[...])

        for q in range(nh // 2):
            ls = pl.ds(q * PAIR, PAIR)
            lv = pl.ds(width + q * PAIR, PAIR)
            out_ref[:, ls] = _gmlp_pair(p_ref[:, ls], p_ref[:, lv], g_ref[:, ls], b_ref[:, ls], w_ref[2 * q], w_ref[2 * q + 1],
                                        bias_ref[:, ls]).astype(out_ref.dtype)

    row = pl.BlockSpec((1, width), lambda i: (0, 0))
    return pl.pallas_call(
        body, name="gmlp_fwd", grid=(T // CHUNK,),
        in_specs=[pl.BlockSpec((CHUNK, 2 * width), lambda i: (i, col_block)), row, row,
                  pl.BlockSpec((nh, CHUNK, CHUNK), lambda i: (0, 0, 0)), pl.BlockSpec((nh, CHUNK), lambda i: (0, 0))],
        out_specs=pl.BlockSpec((CHUNK, width), lambda i: (i, 0)),
        out_shape=_sds((T, width), bf16), scratch_shapes=[pltpu.VMEM((CHUNK, width), f32)],
        compiler_params=_cparams(("arbitrary",)),
    )(proj, ln_g.reshape(1, width), ln_b.reshape(1, width), w_s, b_s)


def _gmlp_bwd(proj, col_block, ln_g, ln_b, w_s, b_s, dy, dy_col_block):
    T = proj.shape[0]
    nh = w_s.shape[0]
    width = nh * HEAD
    nstep = T // CHUNK

    def body(p_ref, g_ref, b_ref, w_ref, bs_ref, dy_ref, dp_ref, dg_ref, db_ref, dw_ref, dbst_ref, bias_ref, dbias_ref):
        @pl.when(pl.program_id(0) == 0)
        def _():
            dg_ref[...] = jnp.zeros_like(dg_ref)
            db_ref[...] = jnp.zeros_like(db_ref)
            dw_ref[...] = jnp.zeros_like(dw_ref)
            dbias_ref[...] = jnp.zeros_like(dbias_ref)
            bias_ref[...] = _gmlp_bias(bs_ref[...])

        for q in range(nh // 2):
            ls = pl.ds(q * PAIR, PAIR)
            lv = pl.ds(width + q * PAIR, PAIR)
            _, vjp = jax.vjp(_gmlp_pair, p_ref[:, ls], p_ref[:, lv], g_ref[:, ls], b_ref[:, ls], w_ref[2 * q], w_ref[2 * q + 1],
                             bias_ref[:, ls])
            dbu, dbv, dg, db, dw_a, dw_b, dbias = vjp(dy_ref[:, ls].astype(f32))
            dp_ref[:, ls] = dbu.astype(dp_ref.dtype)
            dp_ref[:, lv] = dbv.astype(dp_ref.dtype)
            dg_ref[0:1, ls] += dg
            db_ref[0:1, ls] += db
            dw_ref[2 * q] += dw_a
            dw_ref[2 * q + 1] += dw_b
            dbias_ref[:, ls] += dbias

        @pl.when(pl.program_id(0) == nstep - 1)
        def _():
            dbst_ref[...] = lax.dot_general(dbias_ref[...], _head_spread(nh), (((1,), (1,)), ((), ())),
                                            precision=HI, preferred_element_type=f32)

    row = pl.BlockSpec((1, width), lambda i: (0, 0))
    acc8 = pl.BlockSpec((8, width), lambda i: (0, 0))
    wspec = pl.BlockSpec((nh, CHUNK, CHUNK), lambda i: (0, 0, 0))
    res = pl.pallas_call(
        body, name="gmlp_bwd", grid=(nstep,),
        in_specs=[pl.BlockSpec((CHUNK, 2 * width), lambda i: (i, col_block)), row, row, wspec,
                  pl.BlockSpec((nh, CHUNK), lambda i: (0, 0)), pl.BlockSpec((CHUNK, width), lambda i: (i, dy_col_block))],
        out_specs=[pl.BlockSpec((CHUNK, 2 * width), lambda i: (i, 0)), acc8, acc8, wspec,
                   pl.BlockSpec((CHUNK, nh), lambda i: (0, 0))],
        out_shape=[_sds((T, 2 * width), bf16), _sds((8, width), f32), _sds((8, width), f32),
                   _sds((nh, CHUNK, CHUNK), f32), _sds((CHUNK, nh), f32)],
        scratch_shapes=[pltpu.VMEM((CHUNK, width), f32), pltpu.VMEM((CHUNK, width), f32)],
        compiler_params=_cparams(("arbitrary",)),
    )(proj, ln_g.reshape(1, width), ln_b.reshape(1, width), w_s, b_s, dy)
    return res[0], res[1], res[2], res[3], res[4].T


def _sel_col(x, h):
    lane = lax.broadcasted_iota(jnp.int32, x.shape, 1)
    return jnp.sum(jnp.where(lane == h, x, 0.0), axis=1, keepdims=True)


def _sel_row(x, h):
    sub = lax.broadcasted_iota(jnp.int32, x.shape, 0)
    return jnp.sum(jnp.where(sub == h, x, 0.0), axis=0, keepdims=True)


def _ssd_chunk(nh, ngrp, xs_l, z_l, b_l, c_l, dtraw, dtb, alog, dskip, ng_l, prev_l):
    hg = nh // ngrp
    tril = _tril_mask(CHUNK)
    tl = tril.astype(f32)
    lo = lax.broadcasted_iota(jnp.int32, (CHUNK, PAIR), 1) < HEAD
    lo_row = lo[0:1, :]
    dt = jax.nn.softplus(dtraw + dtb)
    a = dt * (-jnp.exp(alog))
    cs = jnp.dot(tl, a, precision=HI, preferred_element_type=f32)
    cst = lax.dot_general(a, tl, (((0,), (1,)), ((), ())), precision=HI, preferred_element_type=f32)
    cb_l = [lax.dot_general(c_l[g].astype(bf16), b_l[g].astype(bf16), (((1,), (1,)), ((), ())),
                            preferred_element_type=f32) for g in range(ngrp)]
    yz_l, new_prev = [], []
    for q in range(nh // 2):
        g = (2 * q) // hg
        cols = []
        for h in (2 * q, 2 * q + 1):
            cs_h = _sel_col(cs, h)
            tot = _sel_row(cs_h, CHUNK - 1)
            seg = jnp.where(tril, cs_h - _sel_row(cst, h), 0.0)
            lmat = jnp.where(tril, jnp.exp(seg), 0.0)
            cols.append((_sel_col(dt, h), cs_h, tot, lmat, _sel_col(dskip, h)))
        (dt_a, cs_a, tot_a, l_a, dsk_a), (dt_b, cs_b, tot_b, l_b, dsk_b) = cols
        xs = xs_l[q]
        x = xs * jnp.where(lo, dt_a, dt_b)
        xb = x.astype(bf16)
        ydiag = jnp.where(lo, jnp.dot((cb_l[g] * l_a).astype(bf16), xb, preferred_element_type=f32),
                          jnp.dot((cb_l[g] * l_b).astype(bf16), xb, preferred_element_type=f32))
        yoff = (jnp.dot(c_l[g].astype(bf16), prev_l[q].astype(bf16), preferred_element_type=f32)
                * jnp.where(lo, jnp.exp(cs_a), jnp.exp(cs_b)))
        xdec = x * jnp.where(lo, jnp.exp(tot_a - cs_a), jnp.exp(tot_b - cs_b))
        st = lax.dot_general(b_l[g].astype(bf16), xdec.astype(bf16), (((0,), (0,)), ((), ())),
                             preferred_element_type=f32)
        new_prev.append(prev_l[q] * jnp.where(lo_row, jnp.exp(tot_a), jnp.exp(tot_b)) + st)
        y = ydiag + yoff + jnp.where(lo_row, dsk_a, dsk_b) * xs
        yz_l.append(y * jax.nn.silu(z_l[q]))
    out = [None] * (nh // 2)
    qg = hg // 2
    for g in range(ngrp):
        ssq = sum(jnp.sum(yz_l[q] * yz_l[q], axis=-1, keepdims=True) for q in range(g * qg, (g + 1) * qg))
        r = lax.rsqrt(ssq * (1.0 / (hg * HEAD)) + EPS)
        for q in range(g * qg, (g + 1) * qg):
            out[q] = yz_l[q] * r * ng_l[q]
    return out, new_prev


def _ssd_read(nh, ngrp, nst, xbc_ref, z_ref, ng_ref, st_ref):
    cw = nh * HEAD
    xs_l = [xbc_ref[:, pl.ds(q * PAIR, PAIR)] for q in range(nh // 2)]
    b_l = [xbc_ref[:, pl.ds(cw + g * nst, nst)] for g in range(ngrp)]
    c_l = [xbc_ref[:, pl.ds(cw + ngrp * nst + g * nst, nst)] for g in range(ngrp)]
    z_l = [z_ref[:, pl.ds(q * PAIR, PAIR)] for q in range(nh // 2)]
    ng_l = [ng_ref[:, pl.ds(q * PAIR, PAIR)] for q in range(nh // 2)]
    prev_l = [st_ref[:, pl.ds(q * PAIR, PAIR)] for q in range(nh // 2)]
    return xs_l, z_l, b_l, c_l, ng_l, prev_l


def _ssd_fwd(xbc, proj, z_col_block, pdt, dtb, alog, dskip, ng, nh, ngrp, nst, n_seq):
    T = xbc.shape[0]
    cw = nh * HEAD
    nchunk = T // n_seq // CHUNK
    assert nst == CHUNK

    def body(xbc_ref, z_ref, dt_ref, dtb_ref, alog_ref, dskip_ref, ng_ref, y_ref, sin_ref, st_ref):
        @pl.when(pl.program_id(1) == 0)
        def _():
            st_ref[...] = jnp.zeros_like(st_ref)

        sin_ref[...] = st_ref[...]
        xs_l, z_l, b_l, c_l, ng_l, prev_l = _ssd_read(nh, ngrp, nst, xbc_ref, z_ref, ng_ref, st_ref)
        y_l, new_prev = _ssd_chunk(nh, ngrp, xs_l, z_l, b_l, c_l, dt_ref[...], dtb_ref[...], alog_ref[...],
                                   dskip_ref[...], ng_l, prev_l)
        for q in range(nh // 2):
            ls = pl.ds(q * PAIR, PAIR)
            y_ref[:, ls] = y_l[q].astype(y_ref.dtype)
            st_ref[:, ls] = new_prev[q]

    def blk(s, c):
        return s * nchunk + c

    prow = pl.BlockSpec((1, 128), lambda s, c: (0, 0))
    return pl.pallas_call(
        body, name="ssd_fwd", grid=(n_seq, nchunk),
        in_specs=[pl.BlockSpec((CHUNK, xbc.shape[1]), lambda s, c: (blk(s, c), 0)),
                  pl.BlockSpec((CHUNK, cw), lambda s, c: (blk(s, c), z_col_block)),
                  pl.BlockSpec((CHUNK, 128), lambda s, c: (blk(s, c), 0)),
                  prow, prow, prow, pl.BlockSpec((1, cw), lambda s, c: (0, 0))],
        out_specs=[pl.BlockSpec((CHUNK, cw), lambda s, c: (blk(s, c), 0)),
                   pl.BlockSpec((nst, cw), lambda s, c: (blk(s, c), 0))],
        out_shape=[_sds((T, cw), bf16), _sds((T, cw), f32)],
        scratch_shapes=[pltpu.VMEM((nst, cw), f32)],
        compiler_params=_cparams(("arbitrary", "arbitrary")),
    )(xbc, proj, pdt, dtb, alog, dskip, ng.reshape(1, cw))


def _ssd_bwd(xbc, proj, z_col_block, pdt, dtb, alog, dskip, ng, sin, dy, dy_col_block, nh, ngrp, nst, n_seq):
    T, xw = xbc.shape
    cw = nh * HEAD
    nchunk = T // n_seq // CHUNK

    def body(xbc_ref, z_ref, dt_ref, dtb_ref, alog_ref, dskip_ref, ng_ref, sin_ref, dy_ref,
             dxbc_ref, dz_ref, ddt_ref, ddtb_ref, dalog_ref, ddskip_ref, dng_ref, dst_ref):
        s, cc = pl.program_id(0), pl.program_id(1)

        @pl.when((s == 0) & (cc == 0))
        def _():
            ddtb_ref[...] = jnp.zeros_like(ddtb_ref)
            dalog_ref[...] = jnp.zeros_like(dalog_ref)
            ddskip_ref[...] = jnp.zeros_like(ddskip_ref)
            dng_ref[...] = jnp.zeros_like(dng_ref)

        @pl.when(cc == 0)
        def _():
            dst_ref[...] = jnp.zeros_like(dst_ref)

        xs_l, z_l, b_l, c_l, ng_l, prev_l = _ssd_read(nh, ngrp, nst, xbc_ref, z_ref, ng_ref, sin_ref)
        _, vjp = jax.vjp(functools.partial(_ssd_chunk, nh, ngrp), xs_l, z_l, b_l, c_l, dt_ref[...], dtb_ref[...],
                         alog_ref[...], dskip_ref[...], ng_l, prev_l)
        dy_l = [dy_ref[:, pl.ds(q * PAIR, PAIR)].astype(f32) for q in range(nh // 2)]
        dst_l = [dst_ref[:, pl.ds(q * PAIR, PAIR)] for q in range(nh // 2)]
        dxs_l, dz_l, db_l, dc_l, ddt, ddtb, dalog, ddskip, dng_l, dprev_l = vjp((dy_l, dst_l))
        for q in range(nh // 2):
            ls = pl.ds(q * PAIR, PAIR)
            dxbc_ref[:, ls] = dxs_l[q]
            dz_ref[:, ls] = dz_l[q].astype(dz_ref.dtype)
            dng_ref[0:1, ls] += dng_l[q]
            dst_ref[:, ls] = dprev_l[q]
        for g in range(ngrp):
            dxbc_ref[:, pl.ds(cw + g * nst, nst)] = db_l[g]
            dxbc_ref[:, pl.ds(cw + ngrp * nst + g * nst, nst)] = dc_l[g]
        ddt_ref[...] = ddt.astype(ddt_ref.dtype)
        ddtb_ref[0:1, :] += ddtb
        dalog_ref[0:1, :] += dalog
        ddskip_ref[0:1, :] += ddskip

    def blk(s, cc):
        return s * nchunk + (nchunk - 1 - cc)

    prow = pl.BlockSpec((1, 128), lambda s, c: (0, 0))
    pacc = pl.BlockSpec((8, 128), lambda s, c: (0, 0))
    return pl.pallas_call(
        body, name="ssd_bwd", grid=(n_seq, nchunk),
        in_specs=[pl.BlockSpec((CHUNK, xw), lambda s, c: (blk(s, c), 0)),
                  pl.BlockSpec((CHUNK, cw), lambda s, c: (blk(s, c), z_col_block)),
                  pl.BlockSpec((CHUNK, 128), lambda s, c: (blk(s, c), 0)),
                  prow, prow, prow, pl.BlockSpec((1, cw), lambda s, c: (0, 0)),
                  pl.BlockSpec((nst, cw), lambda s, c: (blk(s, c), 0)),
                  pl.BlockSpec((CHUNK, cw), lambda s, c: (blk(s, c), dy_col_block))],
        out_specs=[pl.BlockSpec((CHUNK, xw), lambda s, c: (blk(s, c), 0)),
                   pl.BlockSpec((CHUNK, cw), lambda s, c: (blk(s, c), 0)),
                   pl.BlockSpec((CHUNK, 128), lambda s, c: (blk(s, c), 0)),
                   pacc, pacc, pacc, pl.BlockSpec((8, cw), lambda s, c: (0, 0))],
        out_shape=[_sds((T, xw), f32), _sds((T, cw), bf16), _sds((T, 128), bf16),
                   _sds((8, 128), f32), _sds((8, 128), f32), _sds((8, 128), f32), _sds((8, cw), f32)],
        scratch_shapes=[pltpu.VMEM((nst, cw), f32)],
        compiler_params=_cparams(("arbitrary", "arbitrary")),
    )(xbc, proj, pdt, dtb, alog, dskip, ng.reshape(1, cw), sin, dy)


_HBM = pl.BlockSpec(memory_space=pltpu.HBM)
_SEM = pl.BlockSpec(memory_space=pltpu.SEMAPHORE)
_EFFECT = pltpu.SideEffectType.DATAFLOW_SIDE_EFFECTING


def _split_copies(n, scatter, src_refs, land_refs, send_sems, recv_sems):
    npeer = N_DEV - 1
    x, y, c = lax.axis_index("x"), lax.axis_index("y"), lax.axis_index("c")
    me = 4 * x + 2 * y + c
    copies = []
    for i in range(n):
        for k in range(1, N_DEV):
            px = 1 - x if k & 4 else x
            py = 1 - y if k & 2 else y
            pc = 1 - c if k & 1 else c
            src = src_refs[i].at[4 * px + 2 * py + pc] if scatter else src_refs[i]
            copies.append(pltpu.make_async_remote_copy(
                src_ref=src, dst_ref=land_refs[i].at[me],
                send_sem=send_sems.at[i * npeer + k - 1], recv_sem=recv_sems.at[i * npeer + k - 1],
                device_id=(px, py, pc), device_id_type=pl.DeviceIdType.MESH))
    return copies


def _exchange_start(name, arrs, scatter):
    n = len(arrs)
    nsem = n * (N_DEV - 1)
    land_shapes = [a.shape if scatter else (N_DEV,) + a.shape for a in arrs]

    def body(*refs):
        src_refs = refs[:n]
        send_sems, recv_sems = refs[n], refs[n + 1]
        land_refs = refs[2 + 2 * n:2 + 3 * n]
        token, own_sems = refs[2 + 3 * n], refs[3 + 3 * n]
        me = 4 * lax.axis_index("x") + 2 * lax.axis_index("y") + lax.axis_index("c")
        for cp in _split_copies(n, scatter, src_refs, land_refs, send_sems, recv_sems):
            cp.start()
        own = [pltpu.make_async_copy(src_refs[i].at[me] if scatter else src_refs[i], land_refs[i].at[me], own_sems.at[i])
               for i in range(n)]
        for cp in own:
            cp.start()
        for cp in own:
            cp.wait()
        token[...] = jnp.zeros_like(token)

    res = pl.pallas_call(
        body, name=name,
        out_shape=(pltpu.SemaphoreType.DMA((nsem,)), pltpu.SemaphoreType.DMA((nsem,)),
                   *[pltpu.HBM(a.shape, a.dtype) for a in arrs], *[pltpu.HBM(s, a.dtype) for s, a in zip(land_shapes, arrs)],
                   _sds((8, 128), f32)),
        in_specs=[_HBM] * n,
        out_specs=(_SEM, _SEM, *[_HBM] * (2 * n), pl.BlockSpec(memory_space=pltpu.VMEM)),
        input_output_aliases={j: 2 + j for j in range(n)},
        scratch_shapes=[pltpu.SemaphoreType.DMA((n,))],
        compiler_params=pltpu.CompilerParams(has_side_effects=_EFFECT),
    )(*[pltpu.with_memory_space_constraint(a, pltpu.HBM) for a in arrs])
    return (n, scatter, res[0], res[1], res[2:2 + n], res[2 + n:2 + 2 * n]), res[-1]


def _exchange_wait(name, handle, after):
    n, scatter, send_sems, recv_sems, srcs, lands = handle
    after = list(after) if isinstance(after, (list, tuple)) else [after]

    def body(*refs):
        src_refs, land_refs = refs[:n], refs[n:2 * n]
        for cp in _split_copies(n, scatter, src_refs, land_refs, refs[2 * n], refs[2 * n + 1]):
            cp.wait_send()
            cp.wait_recv()

    res = pl.pallas_call(
        body, name=name,
        out_shape=[pltpu.HBM(a.shape, a.dtype) for a in (*srcs, *lands)],
        in_specs=[_HBM] * (2 * n) + [_SEM, _SEM] + [pl.BlockSpec(memory_space=pl.ANY)] * len(after),
        out_specs=[_HBM] * (2 * n),
        input_output_aliases={j: j for j in range(2 * n)},
        compiler_params=pltpu.CompilerParams(has_side_effects=_EFFECT),
    )(*srcs, *lands, send_sems, recv_sems, *after)
    return res[n:]


def _adam_tiles(R, C):
    if R % 256 == 0:
        return (256, C), (R // 256, 1)
    assert C % 128 == 0
    return (R, 128), (1, C // 128)


def _adam(name, parts, w, m, v):
    P, R, C = parts.shape
    (tr, tc), (gr, gc) = _adam_tiles(R, C)
    c1 = 1.0 / (1.0 - ADAM_B1 ** ADAM_STEP)
    c2 = 1.0 / (1.0 - ADAM_B2 ** ADAM_STEP)

    def body(p_ref, w_ref, m_ref, v_ref, g_ref, d_ref, nm_ref, nv_ref):
        g = p_ref[0].astype(f32)
        for s in range(1, P):
            g = g + p_ref[s].astype(f32)
        nm = ADAM_B1 * m_ref[...] + (1.0 - ADAM_B1) * g
        nv = ADAM_B2 * v_ref[...] + (1.0 - ADAM_B2) * (g * g)
        g_ref[...] = g
        nm_ref[...] = nm
        nv_ref[...] = nv
        d_ref[...] = -ADAM_LR * ((nm * c1) / (jnp.sqrt(nv * c2) + ADAM_EPS) + ADAM_WD * w_ref[...])

    tile = pl.BlockSpec((tr, tc), lambda i, j: (i, j))
    return pl.pallas_call(
        body, name=name, grid=(gr, gc),
        in_specs=[pl.BlockSpec((P, tr, tc), lambda i, j: (0, i, j)), tile, tile, tile],
        out_specs=[tile] * 4, out_shape=[_sds((R, C), f32)] * 4,
        compiler_params=_cparams(("arbitrary", "arbitrary")),
    )(parts, w, m, v)


def _sum_parts(name, parts):
    P, R, C = parts.shape
    tr = 256 if R % 256 == 0 else R

    def body(p_ref, o_ref):
        g = p_ref[0]
        for s in range(1, P):
            g = g + p_ref[s]
        o_ref[...] = g

    return pl.pallas_call(
        body, name=name, grid=(R // tr,),
        in_specs=[pl.BlockSpec((P, tr, C), lambda i: (0, i, 0))], out_specs=pl.BlockSpec((tr, C), lambda i: (i, 0)),
        out_shape=_sds((R, C), f32), compiler_params=_cparams(("arbitrary",)),
    )(parts)


def _pad_to(a, n, axis):
    if a.shape[axis] == n:
        return a
    cfg = [(0, 0)] * a.ndim
    cfg[axis] = (0, n - a.shape[axis])
    return jnp.pad(a, cfg)


def _pack(arrs):
    flat = [_pad_to(a.reshape(-1), -(-a.size // 128) * 128, 0) for a in arrs]
    rows = jnp.concatenate(flat).reshape(-1, 128)
    return _pad_to(rows, -(-rows.shape[0] // 256) * 256, 0)


def _unpack(slab, shapes):
    flat = slab.reshape(-1)
    out, o = [], 0
    for s in shapes:
        n = math.prod(s)
        out.append(flat[o:o + n].reshape(s))
        o += -(-n // 128) * 128
    return out


_NAMES = ['norm1_g', 'w_in', 'conv_a_w', 'conv_a_b', 'ln_a_g', 'ln_a_b', 'ln_b_g', 'ln_b_b', 'w_spatial', 'b_spatial',
          'conv_c_w', 'conv_c_b', 'dt_bias', 'a_log', 'd_skip', 'norm_c_g', 'w_out', 'norm2_g', 'w_ff1', 'w_ff2', 'final_g']
_REPL = ['norm1_g', 'conv_a_b', 'ln_a_g', 'ln_a_b', 'ln_b_g', 'ln_b_b', 'w_spatial', 'b_spatial', 'conv_c_b',
         'dt_bias', 'a_log', 'd_skip', 'norm_c_g', 'norm2_g']
_CONVW = ['conv_a_w', 'conv_c_w']
_BIG = ['w_in', 'w_out', 'w_ff1', 'w_ff2']
_BIG_T = {'w_in': True, 'w_out': False, 'w_ff1': True, 'w_ff2': False}


def _row128(v):
    return _pad_to(v.reshape(1, -1), 128, 1)


def _step(p, m, v, x, loss_target):
    nb, S, D = x.shape
    T = nb * S
    depth = p['norm1_g'].shape[0]
    a_w = p['conv_a_b'].shape[1]
    b_w = p['ln_b_g'].shape[1]
    nh = p['dt_bias'].shape[1]
    c_w = p['norm_c_g'].shape[1]
    xw = p['conv_c_b'].shape[1]
    ngrp = 2
    nst = (xw - c_w) // (2 * ngrp)
    d_in = p['w_in'].shape[2] * N_DEV
    main = d_in - nh
    assert main == 2 * a_w + 2 * b_w + c_w + xw and 2 * a_w == 2 * b_w == c_w and xw % c_w == c_w // 2
    me = 4 * lax.axis_index("x") + 2 * lax.axis_index("y") + lax.axis_index("c")

    x2 = x.reshape(T, D)
    tgt = loss_target.reshape(T, D)

    def shards(i, z=None):
        z = 0.0 if z is None else z
        return [(p['w_in'][i].T + z).astype(bf16), (p['w_out'][i] + z).astype(bf16), (p['w_ff1'][i].T + z).astype(bf16),
                (p['w_ff2'][i] + z).astype(bf16), p['conv_a_w'][i], p['conv_c_w'][i]]

    def gathered_in(wt, ca, cc):
        wt = wt.reshape(d_in, D)
        ca = jnp.transpose(ca, (1, 0, 2)).reshape(KA, a_w)
        cc = jnp.transpose(cc, (1, 0, 2)).reshape(KC, xw)
        return dict(wt_main=wt[:main], wt_dt=_pad_to(wt[main:], 128, 0), ca=_pad_to(ca, 32, 0), cc=_pad_to(cc, 8, 0))

    def gathered(got):
        return dict(gathered_in(got[0], got[4], got[5]), wout=got[1].reshape(-1, D), w1t=got[2].reshape(-1, D),
                    w2=got[3].reshape(-1, D))

    sh0 = shards(0)
    h0a, tok = _exchange_start("gather_w0a_start", [sh0[0], sh0[4], sh0[5]], False)
    sh0 = shards(0, tok[0, 0])
    h0b, tokb = _exchange_start("gather_w0b_start", [sh0[1]], False)
    h0c, tokc = _exchange_start("gather_w0c_start", [sh0[2]], False)
    h0d, tokd = _exchange_start("gather_w0d_start", [sh0[3]], False)
    W = [gathered_in(*_exchange_wait("gather_w0a_wait", h0a, [tokb, tokc, tokd]))]

    saved = []
    xc = x2
    for i in range(depth):
        w = W[i]
        h1 = _rms_fwd(xc, p['norm1_g'][i])
        tok = None
        if i + 1 < depth:
            handle, tok = _exchange_start("gather_w%d_start" % (i + 1), shards(i + 1), False)
        (proj,) = _mm("mm_proj", h1, w['wt_main'], "nt", [f32], dep=tok)
        (pdt,) = _mm("mm_pdt", h1, w['wt_dt'], "nt", [f32])
        ya = _conv_fwd("confa_fwd", proj, 0, w['ca'], p['conv_a_b'][i], KA, True, nb, p['ln_a_g'][i], p['ln_a_b'][i])
        yb = _gmlp_fwd(proj, 1, p['ln_b_g'][i], p['ln_b_b'][i], p['w_spatial'][i], p['b_spatial'][i])
        xbc = _conv_fwd("convc_fwd", proj, 2, w['cc'], p['conv_c_b'][i], KC, False, nb)
        dtb, alog, dsk = _row128(p['dt_bias'][i]), _row128(p['a_log'][i]), _row128(p['d_skip'][i])
        yc, sin = _ssd_fwd(xbc, proj, 2, pdt, dtb, alog, dsk, p['norm_c_g'][i], nh, ngrp, nst, nb)
        ycat = jnp.concatenate([ya, yb, yc], axis=1)
        if i == 0:
            w['wout'] = _exchange_wait("gather_w0b_wait", h0b, ycat)[0].reshape(-1, D)
        (xm,) = _mm("mm_out", ycat, w['wout'], "nn", [f32], _ep_add, (xc,))
        h2 = _rms_fwd(xm, p['norm2_g'][i])
        if i == 0:
            w['w1t'] = _exchange_wait("gather_w0c_wait", h0c, h2)[0].reshape(-1, D)
        f, a = _mm("mm_ff1", h2, w['w1t'], "nt", [bf16, bf16], _ep_relu2)
        if i == 0:
            w['w2'] = _exchange_wait("gather_w0d_wait", h0d, a)[0].reshape(-1, D)
        (xo,) = _mm("mm_ff2", a, w['w2'], "nn", [f32], _ep_add, (xm,))
        saved.append(dict(x_in=xc, h1=h1, proj=proj, pdt=pdt, xbc=xbc, sin=sin, ycat=ycat, xm=xm, h2=h2, f=f, a=a,
                          dtb=dtb, alog=alog, dsk=dsk))
        xc = xo
        if i + 1 < depth:
            W.append(gathered(_exchange_wait("gather_w%d_wait" % (i + 1), handle, xo)))

    lp, dx, dfinal = _loss_head(xc, p['final_g'], tgt)
    loss = lax.psum(lp[0, 0], ("x", "y", "c"))

    out = {}
    kinds = ("grad", "delta", "new_m", "new_v")
    names1 = _REPL + _CONVW

    started, small = [], [None] * depth

    def send(n, i, g):
        handle, token = _exchange_start("scatter_%s_%d_start" % (n, i), [g.reshape(N_DEV, -1, D)], True)
        started.append((n, i, handle))
        return token

    tok = None
    for i in reversed(range(depth)):
        w, sv = W[i], saved[i]
        (df,) = _mm("mm_df", dx, w['w2'], "nt", [bf16], _ep_drelu2, (sv['f'],), dep=tok)
        (gw2,) = _mm("mm_gw2", sv['a'], dx, "tn", [bf16])
        tok = send('w_ff2', i, gw2)
        (dh2,) = _mm("mm_dh2", df, w['w1t'], "nn", [f32], dep=tok)
        (gw1t,) = _mm("mm_gw1", df, sv['h2'], "tn", [bf16])
        tok = send('w_ff1', i, gw1t)
        dxm, dg2 = _rms_bwd(sv['xm'], p['norm2_g'][i], dh2, dx)
        (dycat,) = _mm("mm_dycat", dxm, w['wout'], "nt", [bf16], dep=tok)
        (gwout,) = _mm("mm_gwout", sv['ycat'], dxm, "tn", [bf16])
        tok = send('w_out', i, gwout)
        da, dwa, dba, dlag, dlab = _conv_bwd("confa_bwd", sv['proj'], 0, w['ca'], p['conv_a_b'][i] + tok[0, 0], dycat, 0, KA,
                                             True, nb, p['ln_a_g'][i], p['ln_a_b'][i])
        dbb, dlbg, dlbb, dws, dbs = _gmlp_bwd(sv['proj'], 1, p['ln_b_g'][i], p['ln_b_b'][i], p['w_spatial'][i],
                                              p['b_spatial'][i], dycat, 1)
        dxbc, dz, ddt, ddtb, dalog, ddsk, dng = _ssd_bwd(sv['xbc'], sv['proj'], 2, sv['pdt'], sv['dtb'], sv['alog'], sv['dsk'],
                                                         p['norm_c_g'][i], sv['sin'], dycat, 1, nh, ngrp, nst, nb)
        dxbcp, dwc, dbc = _conv_bwd("convc_bwd", sv['proj'], 2, w['cc'], p['conv_c_b'][i], dxbc, 0, KC, False, nb)
        dproj = jnp.concatenate([da, dbb, dz, dxbcp], axis=1)
        (dh_main,) = _mm("mm_dh1", dproj, w['wt_main'], "nn", [f32])
        (dh,) = _mm("mm_dh1dt", ddt, w['wt_dt'], "nn", [f32], _ep_add, (dh_main,))
        (gwt_main,) = _mm("mm_gwin", dproj, sv['h1'], "tn", [bf16])
        (gwt_dt,) = _mm("mm_gwdt", ddt, sv['h1'], "tn", [bf16])
        tok = send('w_in', i, jnp.concatenate([gwt_main, gwt_dt[:nh]], axis=0))
        dx, dg1 = _rms_bwd(sv['x_in'], p['norm1_g'][i] + tok[0, 0], dh, dxm)

        gi = dict(norm1_g=dg1[0], norm2_g=dg2[0], conv_a_w=dwa[:KA], conv_a_b=dba[0], ln_a_g=dlag[0], ln_a_b=dlab[0],
                  ln_b_g=dlbg[0], ln_b_b=dlbb[0], w_spatial=dws, b_spatial=dbs, conv_c_w=dwc[:KC], conv_c_b=dbc[0],
                  dt_bias=ddtb[0, :nh], a_log=dalog[0, :nh], d_skip=ddsk[0, :nh], norm_c_g=dng[0])
        parts_i = [gi[n] for n in names1] + ([dfinal[0]] if i == depth - 1 else [])
        handle, tok = _exchange_start("gather_g%d_start" % i, [_pack(parts_i)], False)
        small[i] = ([a.shape for a in parts_i], handle)
    grad_x = dx.reshape(nb, S, D)

    dep = [dx, tok]
    for n, i, handle in started:
        (parts,) = _exchange_wait("scatter_%s_%d_wait" % (n, i), handle, dep)
        tr = (lambda t: t.T) if _BIG_T[n] else (lambda t: t)
        res = _adam("adam_" + n, parts, tr(p[n][i]), tr(m[n][i]), tr(v[n][i]))
        for kind, r in zip(kinds, res):
            out.setdefault((kind, n), [None] * depth)[i] = tr(r)
        dep = res[3]

    for i in reversed(range(depth)):
        last = i == depth - 1
        shapes1, handle = small[i]
        (parts,) = _exchange_wait("gather_g%d_wait" % i, handle, dep)
        gl = _unpack(_sum_parts("sum_small", parts), shapes1)
        gd = dict(zip(names1, gl))
        for n in _CONVW:
            cw_shard = p[n].shape[2]
            gd[n] = lax.dynamic_slice_in_dim(gd[n], me * cw_shard, cw_shard, axis=1)
        slab = lambda q: _pack([q[n][i] for n in names1] + ([q['final_g']] if last else []))
        g2 = _pack([gd[n] for n in names1] + ([gl[-1]] if last else []))
        res = _adam("adam_small", g2[None], slab(p), slab(m), slab(v))
        shapes2 = [p[n].shape[1:] for n in names1] + ([p['final_g'].shape] if last else [])
        for kind, r in zip(kinds, res):
            lst = _unpack(r, shapes2)
            for n, arr in zip(names1, lst):
                out.setdefault((kind, n), [None] * depth)[i] = arr
            if last:
                out[(kind, 'final_g')] = lst[-1]
        dep = res[3]
    for n in _BIG + names1:
        for kind in kinds:
            out[(kind, n)] = jnp.stack(out[(kind, n)])

    flat = [loss, grad_x]
    for kind in ("grad", "delta", "new_m", "new_v"):
        flat += [out[(kind, n)] for n in _NAMES]
    return tuple(flat)


def kernel(x, norm1_g, w_in, conv_a_w, conv_a_b, ln_a_g, ln_a_b, ln_b_g, ln_b_b, w_spatial, b_spatial, conv_c_w, conv_c_b, dt_bias, a_log, d_skip, norm_c_g, w_out, norm2_g, w_ff1, w_ff2, final_g, loss_target, m_norm1_g, m_w_in, m_conv_a_w, m_conv_a_b, m_ln_a_g, m_ln_a_b, m_ln_b_g, m_ln_b_b, m_w_spatial, m_b_spatial, m_conv_c_w, m_conv_c_b, m_dt_bias, m_a_log, m_d_skip, m_norm_c_g, m_w_out, m_norm2_g, m_w_ff1, m_w_ff2, m_final_g, v_norm1_g, v_w_in, v_conv_a_w, v_conv_a_b, v_ln_a_g, v_ln_a_b, v_ln_b_g, v_ln_b_b, v_w_spatial, v_b_spatial, v_conv_c_w, v_conv_c_b, v_dt_bias, v_a_log, v_d_skip, v_norm_c_g, v_w_out, v_norm2_g, v_w_ff1, v_w_ff2, v_final_g):
    p = dict(zip(_NAMES, (norm1_g, w_in, conv_a_w, conv_a_b, ln_a_g, ln_a_b, ln_b_g, ln_b_b, w_spatial, b_spatial, conv_c_w,
                          conv_c_b, dt_bias, a_log, d_skip, norm_c_g, w_out, norm2_g, w_ff1, w_ff2, final_g)))
    m = dict(zip(_NAMES, (m_norm1_g, m_w_in, m_conv_a_w, m_conv_a_b, m_ln_a_g, m_ln_a_b, m_ln_b_g, m_ln_b_b, m_w_spatial,
                          m_b_spatial, m_conv_c_w, m_conv_c_b, m_dt_bias, m_a_log, m_d_skip, m_norm_c_g, m_w_out, m_norm2_g,
                          m_w_ff1, m_w_ff2, m_final_g)))
    v = dict(zip(_NAMES, (v_norm1_g, v_w_in, v_conv_a_w, v_conv_a_b, v_ln_a_g, v_ln_a_b, v_ln_b_g, v_ln_b_b, v_w_spatial,
                          v_b_spatial, v_conv_c_w, v_conv_c_b, v_dt_bias, v_a_log, v_d_skip, v_norm_c_g, v_w_out, v_norm2_g,
                          v_w_ff1, v_w_ff2, v_final_g)))
    return _step(p, m, v, x, loss_target)
```

```python
import functools
import math

import jax
import jax.numpy as jnp
from jax import lax
from jax.experimental import pallas as pl
from jax.experimental.pallas import tpu as pltpu

f32 = jnp.float32
bf16 = jnp.bfloat16
HI = lax.Precision.HIGHEST
EPS = 1e-5
HEAD = 64
CHUNK = 128
KA = 31
KC = 4
N_DEV = 8
VMEM_LIMIT = 56 * 1024 * 1024

ADAM_LR = 0.001
ADAM_B1 = 0.9
ADAM_B2 = 0.999
ADAM_EPS = 1e-08
ADAM_WD = 0.01
ADAM_STEP = 10


def _cparams(sem=None):
    return pltpu.CompilerParams(dimension_semantics=sem, vmem_limit_bytes=VMEM_LIMIT)


def _sds(shape, dtype):
    return jax.ShapeDtypeStruct(shape, dtype)


_DIMS = {"nn": ((1,), (0,)), "nt": ((1,), (1,)), "tn": ((0,), (0,))}


def _tile(n, cap):
    if n <= cap:
        return n
    for d in range(cap - cap % 128, 0, -128):
        if n % d == 0:
            return d
    raise ValueError((n, cap))


def _mm(name, a, b, form, out_dtypes, epilogue=None, extras=(), tm=1024, tn=512, tk=2048, dep=None):
    if form == "tn":
        K, M = a.shape
    else:
        M, K = a.shape
    N = b.shape[0] if form == "nt" else b.shape[1]
    tm, tn, tk = _tile(M, tm), _tile(N, tn), _tile(K, tk)
    nk = K // tk
    ne, no = len(extras), len(out_dtypes)
    deps = () if dep is None else (dep,)
    if epilogue is None:
        epilogue = lambda acc: (acc,)

    def body(a_ref, b_ref, *rest):
        extra_refs = rest[:ne]
        rest = rest[ne + len(deps):]
        out_refs = rest[:no]
        part = lax.dot_general(a_ref[...].astype(bf16), b_ref[...].astype(bf16),
                               (_DIMS[form], ((), ())), preferred_element_type=f32)

        def finish(acc):
            outs = epilogue(acc, *[e[...] for e in extra_refs])
            for o_ref, v in zip(out_refs, outs):
                o_ref[...] = v.astype(o_ref.dtype)

        if nk == 1:
            finish(part)
            return
        acc_ref = rest[no]
        k = pl.program_id(2)

        @pl.when(k == 0)
        def _():
            acc_ref[...] = part

        @pl.when((k > 0) & (k < nk - 1))
        def _():
            acc_ref[...] += part

        @pl.when(k == nk - 1)
        def _():
            finish(acc_ref[...] + part)

    a_spec = pl.BlockSpec((tk, tm), lambda i, j, k: (k, i)) if form == "tn" else pl.BlockSpec((tm, tk), lambda i, j, k: (i, k))
    b_spec = pl.BlockSpec((tn, tk), lambda i, j, k: (j, k)) if form == "nt" else pl.BlockSpec((tk, tn), lambda i, j, k: (k, j))
    mn_spec = pl.BlockSpec((tm, tn), lambda i, j, k: (i, j))
    return pl.pallas_call(
        body, name=name, grid=(M // tm, N // tn, nk),
        in_specs=[a_spec, b_spec] + [mn_spec] * ne + [pl.BlockSpec((8, 128), lambda i, j, k: (0, 0))] * len(deps),
        out_specs=[mn_spec] * no,
        out_shape=[_sds((M, N), d) for d in out_dtypes],
        scratch_shapes=[pltpu.VMEM((tm, tn), f32)] if nk > 1 else [],
        compiler_params=_cparams(("parallel", "parallel", "arbitrary")),
    )(a, b, *extras, *deps)


def _ep_add(acc, r):
    return (acc + r,)


def _ep_relu2(acc):
    r = jnp.maximum(acc, 0.0)
    return acc, r * r


def _ep_drelu2(acc, f):
    return (acc * 2.0 * jnp.maximum(f, 0.0),)


def _rms(x, g):
    return x * lax.rsqrt(jnp.mean(x * x, axis=-1, keepdims=True) + EPS) * g


TT = 512


def _rms_fwd(x, g):
    T, D = x.shape

    def body(x_ref, g_ref, h_ref):
        h_ref[...] = _rms(x_ref[...], g_ref[...]).astype(bf16)

    return pl.pallas_call(
        body, name="rms_fwd", grid=(T // TT,),
        in_specs=[pl.BlockSpec((TT, D), lambda i: (i, 0)), pl.BlockSpec((1, D), lambda i: (0, 0))],
        out_specs=pl.BlockSpec((TT, D), lambda i: (i, 0)),
        out_shape=_sds((T, D), bf16), compiler_params=_cparams(("arbitrary",)),
    )(x, g.reshape(1, D))


def _rms_bwd(x, g, dh, dres):
    T, D = x.shape

    def body(x_ref, g_ref, dh_ref, dres_ref, dx_ref, dg_ref):
        _, vjp = jax.vjp(_rms, x_ref[...], g_ref[...])
        dx, dg = vjp(dh_ref[...])
        dx_ref[...] = dres_ref[...] + dx

        @pl.when(pl.program_id(0) == 0)
        def _():
            dg_ref[...] = jnp.zeros_like(dg_ref)

        dg_ref[0:1, :] += dg

    tile = pl.BlockSpec((TT, D), lambda i: (i, 0))
    return pl.pallas_call(
        body, name="rms_bwd", grid=(T // TT,),
        in_specs=[tile, pl.BlockSpec((1, D), lambda i: (0, 0)), tile, tile],
        out_specs=[tile, pl.BlockSpec((8, D), lambda i: (0, 0))],
        out_shape=[_sds((T, D), f32), _sds((8, D), f32)], compiler_params=_cparams(("arbitrary",)),
    )(x, g.reshape(1, D), dh, dres)


def _loss_head(x, g, tgt):
    T, D = x.shape

    def f(xv, gv, tv):
        e = _rms(xv, gv) - tv
        return 0.5 * jnp.sum(jnp.sum(e * e, axis=-1, keepdims=True) * (1.0 / D), axis=0, keepdims=True)

    def body(x_ref, g_ref, t_ref, loss_ref, dx_ref, dg_ref):
        tv = t_ref[...]
        l, vjp = jax.vjp(lambda xv, gv: f(xv, gv, tv), x_ref[...], g_ref[...])
        dx, dg = vjp(jnp.ones((1, 1), f32))
        dx_ref[...] = dx

        @pl.when(pl.program_id(0) == 0)
        def _():
            dg_ref[...] = jnp.zeros_like(dg_ref)
            loss_ref[...] = jnp.zeros_like(loss_ref)

        dg_ref[0:1, :] += dg
        loss_ref[...] += jnp.broadcast_to(l, loss_ref.shape)

    tile = pl.BlockSpec((TT, D), lambda i: (i, 0))
    return pl.pallas_call(
        body, name="loss_head", grid=(T // TT,),
        in_specs=[tile, pl.BlockSpec((1, D), lambda i: (0, 0)), tile],
        out_specs=[pl.BlockSpec((8, 128), lambda i: (0, 0)), tile, pl.BlockSpec((8, D), lambda i: (0, 0))],
        out_shape=[_sds((8, 128), f32), _sds((T, D), f32), _sds((8, D), f32)],
        compiler_params=_cparams(("arbitrary",)),
    )(x, g.reshape(1, D), tgt)


TB = 256


def _glu(a_val, a_gate):
    return a_val * jax.nn.sigmoid(a_gate)


PAIR = 2 * HEAD


def _pair_mean(x, lo):
    s_lo = jnp.sum(jnp.where(lo, x, 0.0), axis=-1, keepdims=True)
    s_hi = jnp.sum(jnp.where(lo, 0.0, x), axis=-1, keepdims=True)
    return jnp.where(lo, s_lo, s_hi) * (1.0 / HEAD)


def _pair_ln(v, g, b):
    lo = lax.broadcasted_iota(jnp.int32, v.shape, 1) < HEAD
    vc = v - _pair_mean(v, lo)
    var = _pair_mean(vc * vc, lo)
    return vc * lax.rsqrt(var + EPS) * g + b


def _ln_silu(v, g, b):
    return jax.nn.silu(_pair_ln(v, g, b))


def _conv_geom(kw):
    halo = 32 if kw > 9 else 8
    return halo, halo - (kw - 1)


def _residues(shifts):
    return sorted({s % 8 for s in shifts} - {0})


def _shift_copies(src_ref, cp_ref, res, rows, ls):
    for j, r in enumerate(res):
        cp_ref[j, :, ls] = src_ref[pl.ds(r, rows), ls]


def _shifted(src_ref, cp_ref, res, shift, size, ls):
    r = shift % 8
    if r == 0:
        return src_ref[pl.ds(shift, size), ls]
    return cp_ref[res.index(r), pl.ds(shift - r, size), ls]


def _conv_taps(hp_ref, hs_ref, w_ref, b_ref, acc_ref, kw, off, halo, width):
    res = _residues(range(off, off + kw))
    for c in range(width // 128):
        ls = pl.ds(c * 128, 128)
        _shift_copies(hp_ref, hs_ref, res, halo + TB, ls)
        acc = jnp.broadcast_to(b_ref[:, ls], (TB, 128))
        for k in range(kw):
            acc = acc + w_ref[k:k + 1, ls] * _shifted(hp_ref, hs_ref, res, off + k, TB, ls)
        acc_ref[:, ls] = acc


def _conv_fwd(name, src, col_block, w, b, kw, conformer, n_seq, ln_g=None, ln_b=None, share=None):
    T = src.shape[0]
    cout = w.shape[1]
    cin = 2 * cout if conformer else cout
    halo, off = _conv_geom(kw)
    nblk = T // n_seq // TB
    hb = TB // halo
    out_shape, out_blk, shared_in, aliases = _shared(share, T, cout, bf16 if conformer else f32, 6 if conformer else 4)

    def body(cur_ref, halo_ref, w_ref, b_ref, *rest):
        if conformer:
            g_ref, lb_ref = rest[:2]
            rest = rest[2:]
        out_ref, hp_ref, acc_ref, hs_ref = rest[len(shared_in):]
        i = pl.program_id(1)
        first = (i == 0)

        @pl.when((pl.program_id(0) == 0) & first)
        def _():
            hp_ref[pl.ds(halo + TB, 8), :] = jnp.zeros((8, cout), f32)

        if conformer:
            hp_ref[pl.ds(halo, TB), :] = _glu(cur_ref[:, 0:cout], cur_ref[:, cout:cin])
            hh = _glu(halo_ref[:, 0:cout], halo_ref[:, cout:cin])
        else:
            hp_ref[pl.ds(halo, TB), :] = cur_ref[...]
            hh = halo_ref[...]
        hp_ref[pl.ds(0, halo), :] = jnp.where(first, 0.0, hh)
        _conv_taps(hp_ref, hs_ref, w_ref, b_ref, acc_ref, kw, off, halo, cout)
        if conformer:
            for q in range(cout // PAIR):
                ls = pl.ds(q * PAIR, PAIR)
                out_ref[:, ls] = _ln_silu(acc_ref[:, ls], g_ref[:, ls], lb_ref[:, ls]).astype(out_ref.dtype)
        else:
            out_ref[...] = jax.nn.silu(acc_ref[...]).astype(out_ref.dtype)

    nres = len(_residues(range(off, off + kw)))

    row = pl.BlockSpec((1, cout), lambda s, i: (0, 0))
    in_specs = [pl.BlockSpec((TB, cin), lambda s, i: (s * nblk + i, col_block)),
                pl.BlockSpec((halo, cin), lambda s, i: (jnp.maximum((s * nblk + i) * hb - 1, 0), col_block)),
                pl.BlockSpec((w.shape[0], cout), lambda s, i: (0, 0)), row]
    args = [src, src, w, b.reshape(1, cout)]
    if conformer:
        in_specs += [row, row]
        args += [ln_g.reshape(1, cout), ln_b.reshape(1, cout)]
    in_specs += [pl.BlockSpec(memory_space=pl.ANY)] * len(shared_in)
    args += shared_in
    return pl.pallas_call(
        body, name=name, grid=(n_seq, nblk), in_specs=in_specs,
        out_specs=pl.BlockSpec((TB, cout), lambda s, i: (s * nblk + i, out_blk)),
        out_shape=out_shape, input_output_aliases=aliases,
        scratch_shapes=[pltpu.VMEM((halo + TB + 8, cout), f32), pltpu.VMEM((TB, cout), f32),
                        pltpu.VMEM((nres, halo + TB, cout), f32)],
        compiler_params=_cparams(("arbitrary", "arbitrary")),
    )(*args)


def _shared(share, T, width, dtype, n_inputs, out_index=0):
    if share is None:
        return _sds((T, width), dtype), 0, [], {}
    total, blk, into = share
    if into is None:
        return _sds((T, total), dtype), blk, [], {}
    return _sds((T, total), dtype), blk, [into], {n_inputs: out_index}


def _conv_bwd(name, src, col_block, w, b, dy, dy_col_block, kw, conformer, n_seq, ln_g=None, ln_b=None, share=None):
    T = src.shape[0]
    cout = w.shape[1]
    wrows = w.shape[0]
    cin = 2 * cout if conformer else cout
    halo, off = _conv_geom(kw)
    nblk = T // n_seq // TB
    hb = TB // halo
    dsrc_shape, dsrc_blk, shared_in, aliases = _shared(share, T, cin, bf16, 7 if conformer else 5)

    def body(cur_ref, halo_ref, w_ref, b_ref, dy_ref, *rest):
        if conformer:
            g_ref, lb_ref = rest[:2]
            rest = rest[2:]
        rest = rest[len(shared_in):]
        if conformer:
            dsrc_ref, dw_ref, db_ref, dg_ref, dlb_ref, hp_ref, acc_ref, dz_ref, dhp_ref, carry_ref, hs_ref, dzs_ref = rest
        else:
            dsrc_ref, dw_ref, db_ref, hp_ref, acc_ref, dz_ref, dhp_ref, carry_ref, hs_ref, dzs_ref = rest
        s, ii = pl.program_id(0), pl.program_id(1)
        i = nblk - 1 - ii
        first = (i == 0)

        @pl.when((s == 0) & (ii == 0))
        def _():
            dw_ref[...] = jnp.zeros_like(dw_ref)
            db_ref[...] = jnp.zeros_like(db_ref)
            hp_ref[pl.ds(halo + TB, 8), :] = jnp.zeros((8, cout), f32)
            if conformer:
                dg_ref[...] = jnp.zeros_like(dg_ref)
                dlb_ref[...] = jnp.zeros_like(dlb_ref)

        @pl.when(ii == 0)
        def _():
            carry_ref[...] = jnp.zeros_like(carry_ref)
            dz_ref[pl.ds(0, halo), :] = jnp.zeros((halo, cout), f32)
            dz_ref[pl.ds(halo + TB, halo), :] = jnp.zeros((halo, cout), f32)

        if conformer:
            hp_ref[pl.ds(halo, TB), :] = _glu(cur_ref[:, 0:cout], cur_ref[:, cout:cin])
            hh = _glu(halo_ref[:, 0:cout], halo_ref[:, cout:cin])
        else:
            hp_ref[pl.ds(halo, TB), :] = cur_ref[...]
            hh = halo_ref[...]
        hp_ref[pl.ds(0, halo), :] = jnp.where(first, 0.0, hh)
        _conv_taps(hp_ref, hs_ref, w_ref, b_ref, acc_ref, kw, off, halo, cout)

        if conformer:
            for q in range(cout // PAIR):
                ls = pl.ds(q * PAIR, PAIR)
                _, vjp = jax.vjp(_ln_silu, acc_ref[:, ls], g_ref[:, ls], lb_ref[:, ls])
                da, dg, dlb = vjp(dy_ref[:, ls].astype(f32))
                dz_ref[pl.ds(halo, TB), ls] = da
                dg_ref[0:1, ls] += dg
                dlb_ref[0:1, ls] += dlb
        else:
            _, vjp = jax.vjp(jax.nn.silu, acc_ref[...])
            dz_ref[pl.ds(halo, TB), :] = vjp(dy_ref[...].astype(f32))[0]

        res_h = _residues(range(off, off + kw))
        res_z = _residues(range(kw))
        for c in range(cout // 128):
            ls = pl.ds(c * 128, 128)
            _shift_copies(dz_ref, dzs_ref, res_z, halo + TB + halo - 8, ls)
            dacc = dz_ref[pl.ds(halo, TB), ls]
            db_ref[0:1, ls] += jnp.sum(dacc, axis=0, keepdims=True)
            dhp = jnp.zeros((halo + TB, 128), f32)
            for k in range(kw):
                dw_ref[k:k + 1, ls] += jnp.sum(dacc * _shifted(hp_ref, hs_ref, res_h, off + k, TB, ls), axis=0, keepdims=True)
                dhp = dhp + w_ref[k:k + 1, ls] * _shifted(dz_ref, dzs_ref, res_z, kw - 1 - k, halo + TB, ls)
            dhp_ref[:, ls] = dhp
        dhp_ref[pl.ds(TB, halo), :] += carry_ref[...]
        carry_ref[...] = dhp_ref[pl.ds(0, halo), :]
        dcur = dhp_ref[pl.ds(halo, TB), :]
        if conformer:
            _, vjp = jax.vjp(_glu, cur_ref[:, 0:cout], cur_ref[:, cout:cin])
            dval, dgate = vjp(dcur)
            dsrc_ref[:, 0:cout] = dval.astype(dsrc_ref.dtype)
            dsrc_ref[:, cout:cin] = dgate.astype(dsrc_ref.dtype)
        else:
            dsrc_ref[...] = dcur.astype(dsrc_ref.dtype)

    def blk(s, ii):
        return s * nblk + (nblk - 1 - ii)

    row = pl.BlockSpec((1, cout), lambda s, ii: (0, 0))
    acc8 = pl.BlockSpec((8, cout), lambda s, ii: (0, 0))
    in_specs = [pl.BlockSpec((TB, cin), lambda s, ii: (blk(s, ii), col_block)),
                pl.BlockSpec((halo, cin), lambda s, ii: (jnp.maximum(blk(s, ii) * hb - 1, 0), col_block)),
                pl.BlockSpec((wrows, cout), lambda s, ii: (0, 0)), row,
                pl.BlockSpec((TB, cout), lambda s, ii: (blk(s, ii), dy_col_block))]
    args = [src, src, w, b.reshape(1, cout), dy]
    out_specs = [pl.BlockSpec((TB, cin), lambda s, ii: (blk(s, ii), dsrc_blk)),
                 pl.BlockSpec((wrows, cout), lambda s, ii: (0, 0)), acc8]
    out_shape = [dsrc_shape, _sds((wrows, cout), f32), _sds((8, cout), f32)]
    if conformer:
        in_specs += [row, row]
        args += [ln_g.reshape(1, cout), ln_b.reshape(1, cout)]
        out_specs += [acc8, acc8]
        out_shape += [_sds((8, cout), f32), _sds((8, cout), f32)]
    in_specs += [pl.BlockSpec(memory_space=pl.ANY)] * len(shared_in)
    args += shared_in
    return pl.pallas_call(
        body, name=name, grid=(n_seq, nblk), in_specs=in_specs, out_specs=out_specs, out_shape=out_shape,
        input_output_aliases=aliases,
        scratch_shapes=[pltpu.VMEM((halo + TB + 8, cout), f32), pltpu.VMEM((TB, cout), f32),
                        pltpu.VMEM((halo + TB + halo, cout), f32), pltpu.VMEM((halo + TB, cout), f32),
                        pltpu.VMEM((halo, cout), f32),
                        pltpu.VMEM((len(_residues(range(off, off + kw))), halo + TB, cout), f32),
                        pltpu.VMEM((len(_residues(range(kw))), halo + TB + halo - 8, cout), f32)],
        compiler_params=_cparams(("arbitrary", "arbitrary")),
    )(*args)


def _gelu(x):
    return 0.5 * x * (1.0 + lax.erf(x * (1.0 / math.sqrt(2.0))))


def _tril_mask(n):
    r = lax.broadcasted_iota(jnp.int32, (n, n), 0)
    c = lax.broadcasted_iota(jnp.int32, (n, n), 1)
    return r >= c


def _head_spread(nh):
    r = lax.broadcasted_iota(jnp.int32, (nh, nh * HEAD), 0)
    c = lax.broadcasted_iota(jnp.int32, (nh, nh * HEAD), 1)
    return (c // HEAD == r).astype(f32)


def _gmlp_bias(bs):
    return lax.dot_general(bs, _head_spread(bs.shape[0]), (((0,), (0,)), ((), ())), precision=HI, preferred_element_type=f32)


def _gmlp_pair(bu, bv, g, b, w_a, w_b, bias):
    lo = lax.broadcasted_iota(jnp.int32, bu.shape, 1) < HEAD
    tril = _tril_mask(CHUNK)
    u = _gelu(bu)
    vb = _pair_ln(_gelu(bv), g, b).astype(bf16)
    mix = jnp.where(lo, jnp.dot(jnp.where(tril, w_a, 0.0).astype(bf16), vb, preferred_element_type=f32),
                    jnp.dot(jnp.where(tril, w_b, 0.0).astype(bf16), vb, preferred_element_type=f32))
    return u * (mix + bias)


def _gmlp_fwd(proj, col_block, ln_g, ln_b, w_s, b_s, share=None):
    T = proj.shape[0]
    nh = w_s.shape[0]
    width = nh * HEAD
    out_shape, out_blk, shared_in, aliases = _shared(share, T, width, bf16, 5)

    def body(p_ref, g_ref, b_ref, w_ref, bs_ref, *rest):
        out_ref, bias_ref = rest[len(shared_in):]

        @pl.when(pl.program_id(0) == 0)
        def _():
            bias_ref[...] = _gmlp_bias(bs_ref[...])

        for q in range(nh // 2):
            ls = pl.ds(q * PAIR, PAIR)
            lv = pl.ds(width + q * PAIR, PAIR)
            out_ref[:, ls] = _gmlp_pair(p_ref[:, ls], p_ref[:, lv], g_ref[:, ls], b_ref[:, ls], w_ref[2 * q], w_ref[2 * q + 1],
                                        bias_ref[:, ls]).astype(out_ref.dtype)

    row = pl.BlockSpec((1, width), lambda i: (0, 0))
    return pl.pallas_call(
        body, name="gmlp_fwd", grid=(T // CHUNK,),
        in_specs=[pl.BlockSpec((CHUNK, 2 * width), lambda i: (i, col_block)), row, row,
                  pl.BlockSpec((nh, CHUNK, CHUNK), lambda i: (0, 0, 0)), pl.BlockSpec((nh, CHUNK), lambda i: (0, 0))]
        + [pl.BlockSpec(memory_space=pl.ANY)] * len(shared_in),
        out_specs=pl.BlockSpec((CHUNK, width), lambda i: (i, out_blk)),
        out_shape=out_shape, input_output_aliases=aliases, scratch_shapes=[pltpu.VMEM((CHUNK, width), f32)],
        compiler_params=_cparams(("arbitrary",)),
    )(proj, ln_g.reshape(1, width), ln_b.reshape(1, width), w_s, b_s, *shared_in)


def _gmlp_bwd(proj, col_block, ln_g, ln_b, w_s, b_s, dy, dy_col_block, share=None):
    T = proj.shape[0]
    nh = w_s.shape[0]
    width = nh * HEAD
    nstep = T // CHUNK
    dp_shape, dp_blk, shared_in, aliases = _shared(share, T, 2 * width, bf16, 6)

    def body(p_ref, g_ref, b_ref, w_ref, bs_ref, dy_ref, *rest):
        dp_ref, dg_ref, db_ref, dw_ref, dbst_ref, bias_ref, dbias_ref = rest[len(shared_in):]

        @pl.when(pl.program_id(0) == 0)
        def _():
            dg_ref[...] = jnp.zeros_like(dg_ref)
            db_ref[...] = jnp.zeros_like(db_ref)
            dw_ref[...] = jnp.zeros_like(dw_ref)
            dbias_ref[...] = jnp.zeros_like(dbias_ref)
            bias_ref[...] = _gmlp_bias(bs_ref[...])

        for q in range(nh // 2):
            ls = pl.ds(q * PAIR, PAIR)
            lv = pl.ds(width + q * PAIR, PAIR)
            _, vjp = jax.vjp(_gmlp_pair, p_ref[:, ls], p_ref[:, lv], g_ref[:, ls], b_ref[:, ls], w_ref[2 * q], w_ref[2 * q + 1],
                             bias_ref[:, ls])
            dbu, dbv, dg, db, dw_a, dw_b, dbias = vjp(dy_ref[:, ls].astype(f32))
            dp_ref[:, ls] = dbu.astype(dp_ref.dtype)
            dp_ref[:, lv] = dbv.astype(dp_ref.dtype)
            dg_ref[0:1, ls] += dg
            db_ref[0:1, ls] += db
            dw_ref[2 * q] += dw_a
            dw_ref[2 * q + 1] += dw_b
            dbias_ref[:, ls] += dbias

        @pl.when(pl.program_id(0) == nstep - 1)
        def _():
            dbst_ref[...] = lax.dot_general(dbias_ref[...], _head_spread(nh), (((1,), (1,)), ((), ())),
                                            precision=HI, preferred_element_type=f32)

    row = pl.BlockSpec((1, width), lambda i: (0, 0))
    acc8 = pl.BlockSpec((8, width), lambda i: (0, 0))
    wspec = pl.BlockSpec((nh, CHUNK, CHUNK), lambda i: (0, 0, 0))
    res = pl.pallas_call(
        body, name="gmlp_bwd", grid=(nstep,),
        in_specs=[pl.BlockSpec((CHUNK, 2 * width), lambda i: (i, col_block)), row, row, wspec,
                  pl.BlockSpec((nh, CHUNK), lambda i: (0, 0)), pl.BlockSpec((CHUNK, width), lambda i: (i, dy_col_block))]
        + [pl.BlockSpec(memory_space=pl.ANY)] * len(shared_in),
        out_specs=[pl.BlockSpec((CHUNK, 2 * width), lambda i: (i, dp_blk)), acc8, acc8, wspec,
                   pl.BlockSpec((CHUNK, nh), lambda i: (0, 0))],
        out_shape=[dp_shape, _sds((8, width), f32), _sds((8, width), f32),
                   _sds((nh, CHUNK, CHUNK), f32), _sds((CHUNK, nh), f32)],
        input_output_aliases=aliases,
        scratch_shapes=[pltpu.VMEM((CHUNK, width), f32), pltpu.VMEM((CHUNK, width), f32)],
        compiler_params=_cparams(("arbitrary",)),
    )(proj, ln_g.reshape(1, width), ln_b.reshape(1, width), w_s, b_s, dy, *shared_in)
    return res[0], res[1], res[2], res[3], res[4].T


def _sel_col(x, h):
    lane = lax.broadcasted_iota(jnp.int32, x.shape, 1)
    return jnp.sum(jnp.where(lane == h, x, 0.0), axis=1, keepdims=True)


def _sel_row(x, h):
    sub = lax.broadcasted_iota(jnp.int32, x.shape, 0)
    return jnp.sum(jnp.where(sub == h, x, 0.0), axis=0, keepdims=True)


def _ssd_chunk(nh, ngrp, xs_l, z_l, b_l, c_l, dtraw, dtb, alog, dskip, ng_l, prev_l):
    hg = nh // ngrp
    tril = _tril_mask(CHUNK)
    tl = tril.astype(f32)
    lo = lax.broadcasted_iota(jnp.int32, (CHUNK, PAIR), 1) < HEAD
    lo_row = lo[0:1, :]
    dt = jax.nn.softplus(dtraw + dtb)
    a = dt * (-jnp.exp(alog))
    cs = jnp.dot(tl, a, precision=HI, preferred_element_type=f32)
    cst = lax.dot_general(a, tl, (((0,), (1,)), ((), ())), precision=HI, preferred_element_type=f32)
    cb_l = [lax.dot_general(c_l[g].astype(bf16), b_l[g].astype(bf16), (((1,), (1,)), ((), ())),
                            preferred_element_type=f32) for g in range(ngrp)]
    yz_l, new_prev = [], []
    for q in range(nh // 2):
        g = (2 * q) // hg
        cols = []
        for h in (2 * q, 2 * q + 1):
            cs_h = _sel_col(cs, h)
            tot = _sel_row(cs_h, CHUNK - 1)
            seg = jnp.where(tril, cs_h - _sel_row(cst, h), 0.0)
            lmat = jnp.where(tril, jnp.exp(seg), 0.0)
            cols.append((_sel_col(dt, h), cs_h, tot, lmat, _sel_col(dskip, h)))
        (dt_a, cs_a, tot_a, l_a, dsk_a), (dt_b, cs_b, tot_b, l_b, dsk_b) = cols
        xs = xs_l[q]
        x = xs * jnp.where(lo, dt_a, dt_b)
        xb = x.astype(bf16)
        ydiag = jnp.where(lo, jnp.dot((cb_l[g] * l_a).astype(bf16), xb, preferred_element_type=f32),
                          jnp.dot((cb_l[g] * l_b).astype(bf16), xb, preferred_element_type=f32))
        yoff = (jnp.dot(c_l[g].astype(bf16), prev_l[q].astype(bf16), preferred_element_type=f32)
                * jnp.where(lo, jnp.exp(cs_a), jnp.exp(cs_b)))
        xdec = x * jnp.where(lo, jnp.exp(tot_a - cs_a), jnp.exp(tot_b - cs_b))
        st = lax.dot_general(b_l[g].astype(bf16), xdec.astype(bf16), (((0,), (0,)), ((), ())),
                             preferred_element_type=f32)
        new_prev.append(prev_l[q] * jnp.where(lo_row, jnp.exp(tot_a), jnp.exp(tot_b)) + st)
        y = ydiag + yoff + jnp.where(lo_row, dsk_a, dsk_b) * xs
        yz_l.append(y * jax.nn.silu(z_l[q]))
    out = [None] * (nh // 2)
    qg = hg // 2
    for g in range(ngrp):
        ssq = sum(jnp.sum(yz_l[q] * yz_l[q], axis=-1, keepdims=True) for q in range(g * qg, (g + 1) * qg))
        r = lax.rsqrt(ssq * (1.0 / (hg * HEAD)) + EPS)
        for q in range(g * qg, (g + 1) * qg):
            out[q] = yz_l[q] * r * ng_l[q]
    return out, new_prev


def _ssd_read(nh, ngrp, nst, xbc_ref, z_ref, ng_ref, st_ref):
    cw = nh * HEAD
    xs_l = [xbc_ref[:, pl.ds(q * PAIR, PAIR)] for q in range(nh // 2)]
    b_l = [xbc_ref[:, pl.ds(cw + g * nst, nst)] for g in range(ngrp)]
    c_l = [xbc_ref[:, pl.ds(cw + ngrp * nst + g * nst, nst)] for g in range(ngrp)]
    z_l = [z_ref[:, pl.ds(q * PAIR, PAIR)] for q in range(nh // 2)]
    ng_l = [ng_ref[:, pl.ds(q * PAIR, PAIR)] for q in range(nh // 2)]
    prev_l = [st_ref[:, pl.ds(q * PAIR, PAIR)] for q in range(nh // 2)]
    return xs_l, z_l, b_l, c_l, ng_l, prev_l


def _ssd_fwd(xbc, proj, z_col_block, pdt, dtb, alog, dskip, ng, nh, ngrp, nst, n_seq, share=None):
    T = xbc.shape[0]
    cw = nh * HEAD
    nchunk = T // n_seq // CHUNK
    assert nst == CHUNK
    y_shape, y_blk, shared_in, aliases = _shared(share, T, cw, bf16, 7)

    def body(xbc_ref, z_ref, dt_ref, dtb_ref, alog_ref, dskip_ref, ng_ref, *rest):
        y_ref, sin_ref, st_ref = rest[len(shared_in):]

        @pl.when(pl.program_id(1) == 0)
        def _():
            st_ref[...] = jnp.zeros_like(st_ref)

        sin_ref[...] = st_ref[...]
        xs_l, z_l, b_l, c_l, ng_l, prev_l = _ssd_read(nh, ngrp, nst, xbc_ref, z_ref, ng_ref, st_ref)
        y_l, new_prev = _ssd_chunk(nh, ngrp, xs_l, z_l, b_l, c_l, dt_ref[...], dtb_ref[...], alog_ref[...],
                                   dskip_ref[...], ng_l, prev_l)
        for q in range(nh // 2):
            ls = pl.ds(q * PAIR, PAIR)
            y_ref[:, ls] = y_l[q].astype(y_ref.dtype)
            st_ref[:, ls] = new_prev[q]

    def blk(s, c):
        return s * nchunk + c

    prow = pl.BlockSpec((1, 128), lambda s, c: (0, 0))
    return pl.pallas_call(
        body, name="ssd_fwd", grid=(n_seq, nchunk),
        in_specs=[pl.BlockSpec((CHUNK, xbc.shape[1]), lambda s, c: (blk(s, c), 0)),
                  pl.BlockSpec((CHUNK, cw), lambda s, c: (blk(s, c), z_col_block)),
                  pl.BlockSpec((CHUNK, 128), lambda s, c: (blk(s, c), 0)),
                  prow, prow, prow, pl.BlockSpec((1, cw), lambda s, c: (0, 0))]
        + [pl.BlockSpec(memory_space=pl.ANY)] * len(shared_in),
        out_specs=[pl.BlockSpec((CHUNK, cw), lambda s, c: (blk(s, c), y_blk)),
                   pl.BlockSpec((nst, cw), lambda s, c: (blk(s, c), 0))],
        out_shape=[y_shape, _sds((T, cw), f32)], input_output_aliases=aliases,
        scratch_shapes=[pltpu.VMEM((nst, cw), f32)],
        compiler_params=_cparams(("arbitrary", "arbitrary")),
    )(xbc, proj, pdt, dtb, alog, dskip, ng.reshape(1, cw), *shared_in)


def _ssd_bwd(xbc, proj, z_col_block, pdt, dtb, alog, dskip, ng, sin, dy, dy_col_block, nh, ngrp, nst, n_seq, share=None):
    T, xw = xbc.shape
    cw = nh * HEAD
    nchunk = T // n_seq // CHUNK
    dz_shape, dz_blk, shared_in, aliases = _shared(share, T, cw, bf16, 9, out_index=1)

    def body(xbc_ref, z_ref, dt_ref, dtb_ref, alog_ref, dskip_ref, ng_ref, sin_ref, dy_ref, *rest):
        dxbc_ref, dz_ref, ddt_ref, ddtb_ref, dalog_ref, ddskip_ref, dng_ref, dst_ref = rest[len(shared_in):]
        s, cc = pl.program_id(0), pl.program_id(1)

        @pl.when((s == 0) & (cc == 0))
        def _():
            ddtb_ref[...] = jnp.zeros_like(ddtb_ref)
            dalog_ref[...] = jnp.zeros_like(dalog_ref)
            ddskip_ref[...] = jnp.zeros_like(ddskip_ref)
            dng_ref[...] = jnp.zeros_like(dng_ref)

        @pl.when(cc == 0)
        def _():
            dst_ref[...] = jnp.zeros_like(dst_ref)

        xs_l, z_l, b_l, c_l, ng_l, prev_l = _ssd_read(nh, ngrp, nst, xbc_ref, z_ref, ng_ref, sin_ref)
        _, vjp = jax.vjp(functools.partial(_ssd_chunk, nh, ngrp), xs_l, z_l, b_l, c_l, dt_ref[...], dtb_ref[...],
                         alog_ref[...], dskip_ref[...], ng_l, prev_l)
        dy_l = [dy_ref[:, pl.ds(q * PAIR, PAIR)].astype(f32) for q in range(nh // 2)]
        dst_l = [dst_ref[:, pl.ds(q * PAIR, PAIR)] for q in range(nh // 2)]
        dxs_l, dz_l, db_l, dc_l, ddt, ddtb, dalog, ddskip, dng_l, dprev_l = vjp((dy_l, dst_l))
        for q in range(nh // 2):
            ls = pl.ds(q * PAIR, PAIR)
            dxbc_ref[:, ls] = dxs_l[q]
            dz_ref[:, ls] = dz_l[q].astype(dz_ref.dtype)
            dng_ref[0:1, ls] += dng_l[q]
            dst_ref[:, ls] = dprev_l[q]
        for g in range(ngrp):
            dxbc_ref[:, pl.ds(cw + g * nst, nst)] = db_l[g]
            dxbc_ref[:, pl.ds(cw + ngrp * nst + g * nst, nst)] = dc_l[g]
        ddt_ref[...] = ddt.astype(ddt_ref.dtype)
        ddtb_ref[0:1, :] += ddtb
        dalog_ref[0:1, :] += dalog
        ddskip_ref[0:1, :] += ddskip

    def blk(s, cc):
        return s * nchunk + (nchunk - 1 - cc)

    prow = pl.BlockSpec((1, 128), lambda s, c: (0, 0))
    pacc = pl.BlockSpec((8, 128), lambda s, c: (0, 0))
    return pl.pallas_call(
        body, name="ssd_bwd", grid=(n_seq, nchunk),
        in_specs=[pl.BlockSpec((CHUNK, xw), lambda s, c: (blk(s, c), 0)),
                  pl.BlockSpec((CHUNK, cw), lambda s, c: (blk(s, c), z_col_block)),
                  pl.BlockSpec((CHUNK, 128), lambda s, c: (blk(s, c), 0)),
                  prow, prow, prow, pl.BlockSpec((1, cw), lambda s, c: (0, 0)),
                  pl.BlockSpec((nst, cw), lambda s, c: (blk(s, c), 0)),
                  pl.BlockSpec((CHUNK, cw), lambda s, c: (blk(s, c), dy_col_block))]
        + [pl.BlockSpec(memory_space=pl.ANY)] * len(shared_in),
        out_specs=[pl.BlockSpec((CHUNK, xw), lambda s, c: (blk(s, c), 0)),
                   pl.BlockSpec((CHUNK, cw), lambda s, c: (blk(s, c), dz_blk)),
                   pl.BlockSpec((CHUNK, 128), lambda s, c: (blk(s, c), 0)),
                   pacc, pacc, pacc, pl.BlockSpec((8, cw), lambda s, c: (0, 0))],
        out_shape=[_sds((T, xw), f32), dz_shape, _sds((T, 128), bf16),
                   _sds((8, 128), f32), _sds((8, 128), f32), _sds((8, 128), f32), _sds((8, cw), f32)],
        input_output_aliases=aliases,
        scratch_shapes=[pltpu.VMEM((nst, cw), f32)],
        compiler_params=_cparams(("arbitrary", "arbitrary")),
    )(xbc, proj, pdt, dtb, alog, dskip, ng.reshape(1, cw), sin, dy, *shared_in)


_HBM = pl.BlockSpec(memory_space=pltpu.HBM)
_SEM = pl.BlockSpec(memory_space=pltpu.SEMAPHORE)
_EFFECT = pltpu.SideEffectType.DATAFLOW_SIDE_EFFECTING


def _split_copies(n, scatter, src_refs, land_refs, send_sems, recv_sems):
    npeer = N_DEV - 1
    x, y, c = lax.axis_index("x"), lax.axis_index("y"), lax.axis_index("c")
    me = 4 * x + 2 * y + c
    copies = []
    for i in range(n):
        for k in range(1, N_DEV):
            px = 1 - x if k & 4 else x
            py = 1 - y if k & 2 else y
            pc = 1 - c if k & 1 else c
            src = src_refs[i].at[4 * px + 2 * py + pc] if scatter else src_refs[i]
            copies.append(pltpu.make_async_remote_copy(
                src_ref=src, dst_ref=land_refs[i].at[me],
                send_sem=send_sems.at[i * npeer + k - 1], recv_sem=recv_sems.at[i * npeer + k - 1],
                device_id=(px, py, pc), device_id_type=pl.DeviceIdType.MESH))
    return copies


def _exchange_start(name, arrs, scatter):
    n = len(arrs)
    nsem = n * (N_DEV - 1)
    me = 4 * lax.axis_index("x") + 2 * lax.axis_index("y") + lax.axis_index("c")
    lands = []
    for a in arrs:
        own = lax.dynamic_index_in_dim(a, me, 0, keepdims=True) if scatter else a[None]
        full = lax.empty(a.shape if scatter else (N_DEV,) + a.shape, a.dtype)
        lands.append(lax.dynamic_update_slice(full, own, (me,) + (0,) * (full.ndim - 1)))

    def body(*refs):
        src_refs, land_refs = refs[:n], refs[n:2 * n]
        send_sems, recv_sems = refs[2 * n], refs[2 * n + 1]
        token = refs[-1]
        for cp in _split_copies(n, scatter, src_refs, land_refs, send_sems, recv_sems):
            cp.start()
        token[...] = jnp.zeros_like(token)

    res = pl.pallas_call(
        body, name=name,
        out_shape=(pltpu.SemaphoreType.DMA((nsem,)), pltpu.SemaphoreType.DMA((nsem,)),
                   *[pltpu.HBM(a.shape, a.dtype) for a in arrs], *[pltpu.HBM(l.shape, l.dtype) for l in lands],
                   _sds((8, 128), f32)),
        in_specs=[_HBM] * (2 * n),
        out_specs=(_SEM, _SEM, *[_HBM] * (2 * n), pl.BlockSpec(memory_space=pltpu.VMEM)),
        input_output_aliases={j: 2 + j for j in range(2 * n)},
        compiler_params=pltpu.CompilerParams(has_side_effects=_EFFECT),
    )(*[pltpu.with_memory_space_constraint(a, pltpu.HBM) for a in arrs],
      *[pltpu.with_memory_space_constraint(l, pltpu.HBM) for l in lands])
    return (n, scatter, res[0], res[1], res[2:2 + n], res[2 + n:2 + 2 * n]), res[-1]


def _exchange_wait(name, handle, after):
    n, scatter, send_sems, recv_sems, srcs, lands = handle
    after = list(after) if isinstance(after, (list, tuple)) else [after]

    def body(*refs):
        src_refs, land_refs = refs[:n], refs[n:2 * n]
        for cp in _split_copies(n, scatter, src_refs, land_refs, refs[2 * n], refs[2 * n + 1]):
            cp.wait_send()
            cp.wait_recv()

    res = pl.pallas_call(
        body, name=name,
        out_shape=[pltpu.HBM(a.shape, a.dtype) for a in (*srcs, *lands)],
        in_specs=[_HBM] * (2 * n) + [_SEM, _SEM] + [pl.BlockSpec(memory_space=pl.ANY)] * len(after),
        out_specs=[_HBM] * (2 * n),
        input_output_aliases={j: j for j in range(2 * n)},
        compiler_params=pltpu.CompilerParams(has_side_effects=_EFFECT),
    )(*srcs, *lands, send_sems, recv_sems, *after)
    return res[n:]


def _adam_tiles(R, C):
    if R % 256 == 0:
        return (256, C), (R // 256, 1)
    assert C % 128 == 0
    return (R, 128), (1, C // 128)


def _adam(name, parts, w, m, v):
    P, R, C = parts.shape
    (tr, tc), (gr, gc) = _adam_tiles(R, C)
    c1 = 1.0 / (1.0 - ADAM_B1 ** ADAM_STEP)
    c2 = 1.0 / (1.0 - ADAM_B2 ** ADAM_STEP)

    def body(p_ref, w_ref, m_ref, v_ref, g_ref, d_ref, nm_ref, nv_ref):
        g = p_ref[0].astype(f32)
        for s in range(1, P):
            g = g + p_ref[s].astype(f32)
        nm = ADAM_B1 * m_ref[...] + (1.0 - ADAM_B1) * g
        nv = ADAM_B2 * v_ref[...] + (1.0 - ADAM_B2) * (g * g)
        g_ref[...] = g
        nm_ref[...] = nm
        nv_ref[...] = nv
        d_ref[...] = -ADAM_LR * ((nm * c1) / (jnp.sqrt(nv * c2) + ADAM_EPS) + ADAM_WD * w_ref[...])

    tile = pl.BlockSpec((tr, tc), lambda i, j: (i, j))
    return pl.pallas_call(
        body, name=name, grid=(gr, gc),
        in_specs=[pl.BlockSpec((P, tr, tc), lambda i, j: (0, i, j)), tile, tile, tile],
        out_specs=[tile] * 4, out_shape=[_sds((R, C), f32)] * 4,
        compiler_params=_cparams(("arbitrary", "arbitrary")),
    )(parts, w, m, v)


def _sum_parts(name, parts):
    P, R, C = parts.shape
    tr = 256 if R % 256 == 0 else R

    def body(p_ref, o_ref):
        g = p_ref[0]
        for s in range(1, P):
            g = g + p_ref[s]
        o_ref[...] = g

    return pl.pallas_call(
        body, name=name, grid=(R // tr,),
        in_specs=[pl.BlockSpec((P, tr, C), lambda i: (0, i, 0))], out_specs=pl.BlockSpec((tr, C), lambda i: (i, 0)),
        out_shape=_sds((R, C), f32), compiler_params=_cparams(("arbitrary",)),
    )(parts)


def _pad_to(a, n, axis):
    if a.shape[axis] == n:
        return a
    cfg = [(0, 0)] * a.ndim
    cfg[axis] = (0, n - a.shape[axis])
    return jnp.pad(a, cfg)


def _pack(arrs):
    flat = [_pad_to(a.reshape(-1), -(-a.size // 128) * 128, 0) for a in arrs]
    rows = jnp.concatenate(flat).reshape(-1, 128)
    return _pad_to(rows, -(-rows.shape[0] // 256) * 256, 0)


def _unpack(slab, shapes):
    flat = slab.reshape(-1)
    out, o = [], 0
    for s in shapes:
        n = math.prod(s)
        out.append(flat[o:o + n].reshape(s))
        o += -(-n // 128) * 128
    return out


_NAMES = ['norm1_g', 'w_in', 'conv_a_w', 'conv_a_b', 'ln_a_g', 'ln_a_b', 'ln_b_g', 'ln_b_b', 'w_spatial', 'b_spatial',
          'conv_c_w', 'conv_c_b', 'dt_bias', 'a_log', 'd_skip', 'norm_c_g', 'w_out', 'norm2_g', 'w_ff1', 'w_ff2', 'final_g']
_REPL = ['norm1_g', 'conv_a_b', 'ln_a_g', 'ln_a_b', 'ln_b_g', 'ln_b_b', 'w_spatial', 'b_spatial', 'conv_c_b',
         'dt_bias', 'a_log', 'd_skip', 'norm_c_g', 'norm2_g']
_CONVW = ['conv_a_w', 'conv_c_w']
_BIG = ['w_in', 'w_out', 'w_ff1', 'w_ff2']
_BIG_T = {'w_in': True, 'w_out': False, 'w_ff1': True, 'w_ff2': False}


def _row128(v):
    return _pad_to(v.reshape(1, -1), 128, 1)


def _step(p, m, v, x, loss_target):
    nb, S, D = x.shape
    T = nb * S
    depth = p['norm1_g'].shape[0]
    a_w = p['conv_a_b'].shape[1]
    b_w = p['ln_b_g'].shape[1]
    nh = p['dt_bias'].shape[1]
    c_w = p['norm_c_g'].shape[1]
    xw = p['conv_c_b'].shape[1]
    ngrp = 2
    nst = (xw - c_w) // (2 * ngrp)
    d_in = p['w_in'].shape[2] * N_DEV
    main = d_in - nh
    assert main == 2 * a_w + 2 * b_w + c_w + xw and 2 * a_w == 2 * b_w == c_w and xw % c_w == c_w // 2
    me = 4 * lax.axis_index("x") + 2 * lax.axis_index("y") + lax.axis_index("c")

    x2 = x.reshape(T, D)
    tgt = loss_target.reshape(T, D)

    def shards(i, z=None):
        z = 0.0 if z is None else z
        return [(p['w_in'][i].T + z).astype(bf16), (p['w_out'][i] + z).astype(bf16), (p['w_ff1'][i].T + z).astype(bf16),
                (p['w_ff2'][i] + z).astype(bf16), p['conv_a_w'][i], p['conv_c_w'][i]]

    def gathered_in(wt, ca, cc):
        wt = wt.reshape(d_in, D)
        ca = jnp.transpose(ca, (1, 0, 2)).reshape(KA, a_w)
        cc = jnp.transpose(cc, (1, 0, 2)).reshape(KC, xw)
        return dict(wt_main=wt[:main], wt_dt=_pad_to(wt[main:], 128, 0), ca=_pad_to(ca, 32, 0), cc=_pad_to(cc, 8, 0))

    def gathered(got):
        return dict(gathered_in(got[0], got[4], got[5]), wout=got[1].reshape(-1, D), w1t=got[2].reshape(-1, D),
                    w2=got[3].reshape(-1, D))

    sh0 = shards(0)
    h0a, tok = _exchange_start("gather_w0a_start", [sh0[0], sh0[4], sh0[5]], False)
    h0b, tokb = _exchange_start("gather_w0b_start", [shards(0, tok[0, 0])[1]], False)
    h0c, tokc = _exchange_start("gather_w0c_start", [shards(0, tokb[0, 0])[2]], False)
    h0d, tokd = _exchange_start("gather_w0d_start", [shards(0, tokc[0, 0])[3]], False)
    W = [gathered_in(*_exchange_wait("gather_w0a_wait", h0a, [tokb, tokc, tokd]))]

    saved = []
    xc = x2
    for i in range(depth):
        w = W[i]
        h1 = _rms_fwd(xc, p['norm1_g'][i])
        tok = None
        if i + 1 < depth:
            handle, tok = _exchange_start("gather_w%d_start" % (i + 1), shards(i + 1), False)
        (proj,) = _mm("mm_proj", h1, w['wt_main'], "nt", [f32], dep=tok)
        (pdt,) = _mm("mm_pdt", h1, w['wt_dt'], "nt", [f32])
        mixw = a_w + b_w + c_w
        ycat = _conv_fwd("confa_fwd", proj, 0, w['ca'], p['conv_a_b'][i], KA, True, nb, p['ln_a_g'][i], p['ln_a_b'][i],
                         share=(mixw, 0, None))
        ycat = _gmlp_fwd(proj, 1, p['ln_b_g'][i], p['ln_b_b'][i], p['w_spatial'][i], p['b_spatial'][i], share=(mixw, 1, ycat))
        xbc = _conv_fwd("convc_fwd", proj, 2, w['cc'], p['conv_c_b'][i], KC, False, nb)
        dtb, alog, dsk = _row128(p['dt_bias'][i]), _row128(p['a_log'][i]), _row128(p['d_skip'][i])
        ycat, sin = _ssd_fwd(xbc, proj, 2, pdt, dtb, alog, dsk, p['norm_c_g'][i], nh, ngrp, nst, nb, share=(mixw, 1, ycat))
        if i == 0:
            w['wout'] = _exchange_wait("gather_w0b_wait", h0b, ycat)[0].reshape(-1, D)
        (xm,) = _mm("mm_out", ycat, w['wout'], "nn", [f32], _ep_add, (xc,))
        h2 = _rms_fwd(xm, p['norm2_g'][i])
        if i == 0:
            w['w1t'] = _exchange_wait("gather_w0c_wait", h0c, h2)[0].reshape(-1, D)
        f, a = _mm("mm_ff1", h2, w['w1t'], "nt", [bf16, bf16], _ep_relu2)
        if i == 0:
            w['w2'] = _exchange_wait("gather_w0d_wait", h0d, a)[0].reshape(-1, D)
        (xo,) = _mm("mm_ff2", a, w['w2'], "nn", [f32], _ep_add, (xm,))
        saved.append(dict(x_in=xc, h1=h1, proj=proj, pdt=pdt, xbc=xbc, sin=sin, ycat=ycat, xm=xm, h2=h2, f=f, a=a,
                          dtb=dtb, alog=alog, dsk=dsk))
        xc = xo
        if i + 1 < depth:
            W.append(gathered(_exchange_wait("gather_w%d_wait" % (i + 1), handle, xo)))

    lp, dx, dfinal = _loss_head(xc, p['final_g'], tgt)
    loss = lax.psum(lp[0, 0], ("x", "y", "c"))

    out = {}
    kinds = ("grad", "delta", "new_m", "new_v")
    names1 = _REPL + _CONVW

    started, small = [], [None] * depth

    def send(n, i, g):
        handle, token = _exchange_start("scatter_%s_%d_start" % (n, i), [g.reshape(N_DEV, -1, D)], True)
        started.append((n, i, handle))
        return token

    tok = None
    for i in reversed(range(depth)):
        w, sv = W[i], saved[i]
        (df,) = _mm("mm_df", dx, w['w2'], "nt", [bf16], _ep_drelu2, (sv['f'],), dep=tok)
        (gw2,) = _mm("mm_gw2", sv['a'], dx, "tn", [bf16])
        tok = send('w_ff2', i, gw2)
        (dh2,) = _mm("mm_dh2", df, w['w1t'], "nn", [f32], dep=tok)
        (gw1t,) = _mm("mm_gw1", df, sv['h2'], "tn", [bf16])
        tok = send('w_ff1', i, gw1t)
        dxm, dg2 = _rms_bwd(sv['xm'], p['norm2_g'][i], dh2, dx)
        (dycat,) = _mm("mm_dycat", dxm, w['wout'], "nt", [bf16], dep=tok)
        (gwout,) = _mm("mm_gwout", sv['ycat'], dxm, "tn", [bf16])
        tok = send('w_out', i, gwout)
        dproj, dwa, dba, dlag, dlab = _conv_bwd("confa_bwd", sv['proj'], 0, w['ca'], p['conv_a_b'][i] + tok[0, 0], dycat, 0, KA,
                                                True, nb, p['ln_a_g'][i], p['ln_a_b'][i], share=(main, 0, None))
        dproj, dlbg, dlbb, dws, dbs = _gmlp_bwd(sv['proj'], 1, p['ln_b_g'][i], p['ln_b_b'][i], p['w_spatial'][i],
                                                p['b_spatial'][i], dycat, 1, share=(main, 1, dproj))
        dxbc, dproj, ddt, ddtb, dalog, ddsk, dng = _ssd_bwd(sv['xbc'], sv['proj'], 2, sv['pdt'], sv['dtb'], sv['alog'],
                                                            sv['dsk'], p['norm_c_g'][i], sv['sin'], dycat, 1, nh, ngrp, nst, nb,
                                                            share=(main, 2, dproj))
        dproj, dwc, dbc = _conv_bwd("convc_bwd", sv['proj'], 2, w['cc'], p['conv_c_b'][i], dxbc, 0, KC, False, nb,
                                    share=(main, 2, dproj))
        (dh_main,) = _mm("mm_dh1", dproj, w['wt_main'], "nn", [f32])
        (dh,) = _mm("mm_dh1dt", ddt, w['wt_dt'], "nn", [f32], _ep_add, (dh_main,))
        (gwt_main,) = _mm("mm_gwin", dproj, sv['h1'], "tn", [bf16])
        (gwt_dt,) = _mm("mm_gwdt", ddt, sv['h1'], "tn", [bf16])
        tok = send('w_in', i, jnp.concatenate([gwt_main, gwt_dt[:nh]], axis=0))
        dx, dg1 = _rms_bwd(sv['x_in'], p['norm1_g'][i] + tok[0, 0], dh, dxm)

        gi = dict(norm1_g=dg1[0], norm2_g=dg2[0], conv_a_w=dwa[:KA], conv_a_b=dba[0], ln_a_g=dlag[0], ln_a_b=dlab[0],
                  ln_b_g=dlbg[0], ln_b_b=dlbb[0], w_spatial=dws, b_spatial=dbs, conv_c_w=dwc[:KC], conv_c_b=dbc[0],
                  dt_bias=ddtb[0, :nh], a_log=dalog[0, :nh], d_skip=ddsk[0, :nh], norm_c_g=dng[0])
        parts_i = [gi[n] for n in names1] + ([dfinal[0]] if i == depth - 1 else [])
        handle, tok = _exchange_start("gather_g%d_start" % i, [_pack(parts_i)], False)
        small[i] = ([a.shape for a in parts_i], handle)
    grad_x = dx.reshape(nb, S, D)

    dep = [dx, tok]
    for n, i, handle in started:
        (parts,) = _exchange_wait("scatter_%s_%d_wait" % (n, i), handle, dep)
        tr = (lambda t: t.T) if _BIG_T[n] else (lambda t: t)
        res = _adam("adam_" + n, parts, tr(p[n][i]), tr(m[n][i]), tr(v[n][i]))
        for kind, r in zip(kinds, res):
            out.setdefault((kind, n), [None] * depth)[i] = tr(r)
        dep = res[3]

    gsum = [None] * depth
    for i in reversed(range(depth)):
        (parts,) = _exchange_wait("gather_g%d_wait" % i, small[i][1], dep)
        gsum[i] = _sum_parts("sum_small", parts)
        dep = gsum[i]
    widths = [-(-math.prod(p[n].shape[1:]) // 128) * 128 for n in _REPL]
    rep_rows = sum(widths) // 128
    tot_rows = -(-depth * rep_rows // 256) * 256

    def rep_slab(q):
        cols = [_pad_to(q[n].reshape(depth, -1), wd, 1) for n, wd in zip(_REPL, widths)]
        return _pad_to(jnp.concatenate(cols, axis=1).reshape(-1, 128), tot_rows, 0)

    g_rep = _pad_to(jnp.concatenate([g[:rep_rows] for g in gsum], axis=0), tot_rows, 0)
    res = _adam("adam_small", g_rep[None], rep_slab(p), rep_slab(m), rep_slab(v))
    for kind, r in zip(kinds, res):
        view = r[:depth * rep_rows].reshape(depth, -1)
        o = 0
        for n, wd in zip(_REPL, widths):
            out[(kind, n)] = view[:, o:o + math.prod(p[n].shape[1:])].reshape(p[n].shape)
            o += wd

    extra = []
    for i in range(depth):
        tail = _unpack(gsum[i][rep_rows:], small[i][0][len(_REPL):])
        extra.append(tail)
    gconv = []
    for j, n in enumerate(_CONVW):
        cw_shard = p[n].shape[2]
        full = jnp.stack([extra[i][j] for i in range(depth)])
        gconv.append(lax.dynamic_slice_in_dim(full, me * cw_shard, cw_shard, axis=2))
    tail_names = _CONVW + ['final_g']
    res = _adam("adam_conv", _pack(gconv + [extra[depth - 1][len(_CONVW)]])[None],
                *[_pack([q[n] for n in tail_names]) for q in (p, m, v)])
    for kind, r in zip(kinds, res):
        for n, arr in zip(tail_names, _unpack(r, [p[n].shape for n in tail_names])):
            out[(kind, n)] = arr
    for n in _BIG:
        for kind in kinds:
            out[(kind, n)] = jnp.stack(out[(kind, n)])

    flat = [loss, grad_x]
    for kind in ("grad", "delta", "new_m", "new_v"):
        flat += [out[(kind, n)] for n in _NAMES]
    return tuple(flat)


def kernel(x, norm1_g, w_in, conv_a_w, conv_a_b, ln_a_g, ln_a_b, ln_b_g, ln_b_b, w_spatial, b_spatial, conv_c_w, conv_c_b, dt_bias, a_log, d_skip, norm_c_g, w_out, norm2_g, w_ff1, w_ff2, final_g, loss_target, m_norm1_g, m_w_in, m_conv_a_w, m_conv_a_b, m_ln_a_g, m_ln_a_b, m_ln_b_g, m_ln_b_b, m_w_spatial, m_b_spatial, m_conv_c_w, m_conv_c_b, m_dt_bias, m_a_log, m_d_skip, m_norm_c_g, m_w_out, m_norm2_g, m_w_ff1, m_w_ff2, m_final_g, v_norm1_g, v_w_in, v_conv_a_w, v_conv_a_b, v_ln_a_g, v_ln_a_b, v_ln_b_g, v_ln_b_b, v_w_spatial, v_b_spatial, v_conv_c_w, v_conv_c_b, v_dt_bias, v_a_log, v_d_skip, v_norm_c_g, v_w_out, v_norm2_g, v_w_ff1, v_w_ff2, v_final_g):
    p = dict(zip(_NAMES, (norm1_g, w_in, conv_a_w, conv_a_b, ln_a_g, ln_a_b, ln_b_g, ln_b_b, w_spatial, b_spatial, conv_c_w,
                          conv_c_b, dt_bias, a_log, d_skip, norm_c_g, w_out, norm2_g, w_ff1, w_ff2, final_g)))
    m = dict(zip(_NAMES, (m_norm1_g, m_w_in, m_conv_a_w, m_conv_a_b, m_ln_a_g, m_ln_a_b, m_ln_b_g, m_ln_b_b, m_w_spatial,
                          m_b_spatial, m_conv_c_w, m_conv_c_b, m_dt_bias, m_a_log, m_d_skip, m_norm_c_g, m_w_out, m_norm2_g,
                          m_w_ff1, m_w_ff2, m_final_g)))
    v = dict(zip(_NAMES, (v_norm1_g, v_w_in, v_conv_a_w, v_conv_a_b, v_ln_a_g, v_ln_a_b, v_ln_b_g, v_ln_b_b, v_w_spatial,
                          v_b_spatial, v_conv_c_w, v_conv_c_b, v_dt_bias, v_a_log, v_d_skip, v_norm_c_g, v_w_out, v_norm2_g,
                          v_w_ff1, v_w_ff2, v_final_g)))
    return _step(p, m, v, x, loss_target)
```

```python
import functools
import math

import jax
import jax.numpy as jnp
from jax import lax
from jax.experimental import pallas as pl
from jax.experimental.pallas import tpu as pltpu

f32 = jnp.float32
bf16 = jnp.bfloat16
HI = lax.Precision.HIGHEST
EPS = 1e-5
HEAD = 64
CHUNK = 128
KA = 31
KC = 4
N_DEV = 8
VMEM_LIMIT = 56 * 1024 * 1024

ADAM_LR = 0.001
ADAM_B1 = 0.9
ADAM_B2 = 0.999
ADAM_EPS = 1e-08
ADAM_WD = 0.01
ADAM_STEP = 10


def _cparams(sem=None):
    return pltpu.CompilerParams(dimension_semantics=sem, vmem_limit_bytes=VMEM_LIMIT)


def _sds(shape, dtype):
    return jax.ShapeDtypeStruct(shape, dtype)


_DIMS = {"nn": ((1,), (0,)), "nt": ((1,), (1,)), "tn": ((0,), (0,))}


def _tile(n, cap):
    if n <= cap:
        return n
    for d in range(cap - cap % 128, 0, -128):
        if n % d == 0:
            return d
    raise ValueError((n, cap))


def _mm(name, a, b, form, out_dtypes, epilogue=None, extras=(), tm=1024, tn=1024, tk=2048, dep=None):
    if form == "tn":
        K, M = a.shape
    else:
        M, K = a.shape
    N = b.shape[0] if form == "nt" else b.shape[1]
    tm, tn, tk = _tile(M, tm), _tile(N, tn), _tile(K, tk)
    nk = K // tk
    ne, no = len(extras), len(out_dtypes)
    deps = () if dep is None else (dep,)
    if epilogue is None:
        epilogue = lambda acc: (acc,)

    def body(a_ref, b_ref, *rest):
        extra_refs = rest[:ne]
        rest = rest[ne + len(deps):]
        out_refs = rest[:no]
        part = lax.dot_general(a_ref[...].astype(bf16), b_ref[...].astype(bf16),
                               (_DIMS[form], ((), ())), preferred_element_type=f32)

        def finish(acc):
            outs = epilogue(acc, *[e[...] for e in extra_refs])
            for o_ref, v in zip(out_refs, outs):
                o_ref[...] = v.astype(o_ref.dtype)

        if nk == 1:
            finish(part)
            return
        acc_ref = rest[no]
        k = pl.program_id(2)

        @pl.when(k == 0)
        def _():
            acc_ref[...] = part

        @pl.when((k > 0) & (k < nk - 1))
        def _():
            acc_ref[...] += part

        @pl.when(k == nk - 1)
        def _():
            finish(acc_ref[...] + part)

    a_spec = pl.BlockSpec((tk, tm), lambda i, j, k: (k, i)) if form == "tn" else pl.BlockSpec((tm, tk), lambda i, j, k: (i, k))
    b_spec = pl.BlockSpec((tn, tk), lambda i, j, k: (j, k)) if form == "nt" else pl.BlockSpec((tk, tn), lambda i, j, k: (k, j))
    mn_spec = pl.BlockSpec((tm, tn), lambda i, j, k: (i, j))
    return pl.pallas_call(
        body, name=name, grid=(M // tm, N // tn, nk),
        in_specs=[a_spec, b_spec] + [mn_spec] * ne + [pl.BlockSpec((8, 128), lambda i, j, k: (0, 0))] * len(deps),
        out_specs=[mn_spec] * no,
        out_shape=[_sds((M, N), d) for d in out_dtypes],
        scratch_shapes=[pltpu.VMEM((tm, tn), f32)] if nk > 1 else [],
        compiler_params=_cparams(("parallel", "parallel", "arbitrary")),
    )(a, b, *extras, *deps)


def _ep_add(acc, r):
    return (acc + r,)


def _ep_relu2(acc):
    r = jnp.maximum(acc, 0.0)
    return acc, r * r


def _ep_drelu2(acc, f):
    return (acc * 2.0 * jnp.maximum(f, 0.0),)


def _rms(x, g):
    return x * lax.rsqrt(jnp.mean(x * x, axis=-1, keepdims=True) + EPS) * g


TT = 512


def _rms_fwd(x, g):
    T, D = x.shape

    def body(x_ref, g_ref, h_ref, tok_ref):
        h_ref[...] = _rms(x_ref[...], g_ref[...]).astype(bf16)
        tok_ref[...] = jnp.zeros_like(tok_ref)

    return pl.pallas_call(
        body, name="rms_fwd", grid=(T // TT,),
        in_specs=[pl.BlockSpec((TT, D), lambda i: (i, 0)), pl.BlockSpec((1, D), lambda i: (0, 0))],
        out_specs=[pl.BlockSpec((TT, D), lambda i: (i, 0)), pl.BlockSpec((8, 128), lambda i: (0, 0))],
        out_shape=[_sds((T, D), bf16), _sds((8, 128), f32)], compiler_params=_cparams(("arbitrary",)),
    )(x, g.reshape(1, D))


def _rms_bwd(x, g, dh, dres):
    T, D = x.shape

    def body(x_ref, g_ref, dh_ref, dres_ref, dx_ref, dg_ref):
        _, vjp = jax.vjp(_rms, x_ref[...], g_ref[...])
        dx, dg = vjp(dh_ref[...])
        dx_ref[...] = dres_ref[...] + dx

        @pl.when(pl.program_id(0) == 0)
        def _():
            dg_ref[...] = jnp.zeros_like(dg_ref)

        dg_ref[0:1, :] += dg

    tile = pl.BlockSpec((TT, D), lambda i: (i, 0))
    return pl.pallas_call(
        body, name="rms_bwd", grid=(T // TT,),
        in_specs=[tile, pl.BlockSpec((1, D), lambda i: (0, 0)), tile, tile],
        out_specs=[tile, pl.BlockSpec((8, D), lambda i: (0, 0))],
        out_shape=[_sds((T, D), f32), _sds((8, D), f32)], compiler_params=_cparams(("arbitrary",)),
    )(x, g.reshape(1, D), dh, dres)


def _loss_head(x, g, tgt):
    T, D = x.shape

    def f(xv, gv, tv):
        e = _rms(xv, gv) - tv
        return 0.5 * jnp.sum(jnp.sum(e * e, axis=-1, keepdims=True) * (1.0 / D), axis=0, keepdims=True)

    def body(x_ref, g_ref, t_ref, loss_ref, dx_ref, dg_ref):
        tv = t_ref[...]
        l, vjp = jax.vjp(lambda xv, gv: f(xv, gv, tv), x_ref[...], g_ref[...])
        dx, dg = vjp(jnp.ones((1, 1), f32))
        dx_ref[...] = dx

        @pl.when(pl.program_id(0) == 0)
        def _():
            dg_ref[...] = jnp.zeros_like(dg_ref)
            loss_ref[...] = jnp.zeros_like(loss_ref)

        dg_ref[0:1, :] += dg
        loss_ref[...] += jnp.broadcast_to(l, loss_ref.shape)

    tile = pl.BlockSpec((TT, D), lambda i: (i, 0))
    return pl.pallas_call(
        body, name="loss_head", grid=(T // TT,),
        in_specs=[tile, pl.BlockSpec((1, D), lambda i: (0, 0)), tile],
        out_specs=[pl.BlockSpec((8, 128), lambda i: (0, 0)), tile, pl.BlockSpec((8, D), lambda i: (0, 0))],
        out_shape=[_sds((8, 128), f32), _sds((T, D), f32), _sds((8, D), f32)],
        compiler_params=_cparams(("arbitrary",)),
    )(x, g.reshape(1, D), tgt)


TB = 256


def _glu(a_val, a_gate):
    return a_val * jax.nn.sigmoid(a_gate)


PAIR = 2 * HEAD


def _pair_mean(x, lo):
    s_lo = jnp.sum(jnp.where(lo, x, 0.0), axis=-1, keepdims=True)
    s_hi = jnp.sum(jnp.where(lo, 0.0, x), axis=-1, keepdims=True)
    return jnp.where(lo, s_lo, s_hi) * (1.0 / HEAD)


def _pair_ln(v, g, b):
    lo = lax.broadcasted_iota(jnp.int32, v.shape, 1) < HEAD
    vc = v - _pair_mean(v, lo)
    var = _pair_mean(vc * vc, lo)
    return vc * lax.rsqrt(var + EPS) * g + b


def _ln_silu(v, g, b):
    return jax.nn.silu(_pair_ln(v, g, b))


def _conv_geom(kw):
    halo = 32 if kw > 9 else 8
    return halo, halo - (kw - 1)


def _residues(shifts):
    return sorted({s % 8 for s in shifts} - {0})


def _shift_copies(src_ref, cp_ref, res, rows, ls):
    for j, r in enumerate(res):
        cp_ref[j, :, ls] = src_ref[pl.ds(r, rows), ls]


def _shifted(src_ref, cp_ref, res, shift, size, ls):
    r = shift % 8
    if r == 0:
        return src_ref[pl.ds(shift, size), ls]
    return cp_ref[res.index(r), pl.ds(shift - r, size), ls]


def _conv_taps(hp_ref, hs_ref, w_ref, b_ref, acc_ref, kw, off, halo, width):
    res = _residues(range(off, off + kw))
    for c in range(width // 128):
        ls = pl.ds(c * 128, 128)
        _shift_copies(hp_ref, hs_ref, res, halo + TB, ls)
        acc = jnp.broadcast_to(b_ref[:, ls], (TB, 128))
        for k in range(kw):
            acc = acc + w_ref[k:k + 1, ls] * _shifted(hp_ref, hs_ref, res, off + k, TB, ls)
        acc_ref[:, ls] = acc


def _conv_fwd(name, src, col_block, w, b, kw, conformer, n_seq, ln_g=None, ln_b=None, share=None):
    T = src.shape[0]
    cout = w.shape[1]
    cin = 2 * cout if conformer else cout
    halo, off = _conv_geom(kw)
    nblk = T // n_seq // TB
    hb = TB // halo
    out_shape, out_blk, shared_in, aliases = _shared(share, T, cout, bf16 if conformer else f32, 6 if conformer else 4)

    def body(cur_ref, halo_ref, w_ref, b_ref, *rest):
        if conformer:
            g_ref, lb_ref = rest[:2]
            rest = rest[2:]
        out_ref, hp_ref, acc_ref, hs_ref = rest[len(shared_in):]
        i = pl.program_id(1)
        first = (i == 0)

        @pl.when((pl.program_id(0) == 0) & first)
        def _():
            hp_ref[pl.ds(halo + TB, 8), :] = jnp.zeros((8, cout), f32)

        if conformer:
            hp_ref[pl.ds(halo, TB), :] = _glu(cur_ref[:, 0:cout], cur_ref[:, cout:cin])
            hh = _glu(halo_ref[:, 0:cout], halo_ref[:, cout:cin])
        else:
            hp_ref[pl.ds(halo, TB), :] = cur_ref[...]
            hh = halo_ref[...]
        hp_ref[pl.ds(0, halo), :] = jnp.where(first, 0.0, hh)
        _conv_taps(hp_ref, hs_ref, w_ref, b_ref, acc_ref, kw, off, halo, cout)
        if conformer:
            for q in range(cout // PAIR):
                ls = pl.ds(q * PAIR, PAIR)
                out_ref[:, ls] = _ln_silu(acc_ref[:, ls], g_ref[:, ls], lb_ref[:, ls]).astype(out_ref.dtype)
        else:
            out_ref[...] = jax.nn.silu(acc_ref[...]).astype(out_ref.dtype)

    nres = len(_residues(range(off, off + kw)))

    row = pl.BlockSpec((1, cout), lambda s, i: (0, 0))
    in_specs = [pl.BlockSpec((TB, cin), lambda s, i: (s * nblk + i, col_block)),
                pl.BlockSpec((halo, cin), lambda s, i: (jnp.maximum((s * nblk + i) * hb - 1, 0), col_block)),
                pl.BlockSpec((w.shape[0], cout), lambda s, i: (0, 0)), row]
    args = [src, src, w, b.reshape(1, cout)]
    if conformer:
        in_specs += [row, row]
        args += [ln_g.reshape(1, cout), ln_b.reshape(1, cout)]
    in_specs += [pl.BlockSpec(memory_space=pl.ANY)] * len(shared_in)
    args += shared_in
    return pl.pallas_call(
        body, name=name, grid=(n_seq, nblk), in_specs=in_specs,
        out_specs=pl.BlockSpec((TB, cout), lambda s, i: (s * nblk + i, out_blk)),
        out_shape=out_shape, input_output_aliases=aliases,
        scratch_shapes=[pltpu.VMEM((halo + TB + 8, cout), f32), pltpu.VMEM((TB, cout), f32),
                        pltpu.VMEM((nres, halo + TB, cout), f32)],
        compiler_params=_cparams(("arbitrary", "arbitrary")),
    )(*args)


def _shared(share, T, width, dtype, n_inputs, out_index=0):
    if share is None:
        return _sds((T, width), dtype), 0, [], {}
    total, blk, into = share
    if into is None:
        return _sds((T, total), dtype), blk, [], {}
    return _sds((T, total), dtype), blk, [into], {n_inputs: out_index}


def _conv_bwd(name, src, col_block, w, b, dy, dy_col_block, kw, conformer, n_seq, ln_g=None, ln_b=None, share=None):
    T = src.shape[0]
    cout = w.shape[1]
    wrows = w.shape[0]
    cin = 2 * cout if conformer else cout
    halo, off = _conv_geom(kw)
    nblk = T // n_seq // TB
    hb = TB // halo
    dsrc_shape, dsrc_blk, shared_in, aliases = _shared(share, T, cin, bf16, 7 if conformer else 5)

    def body(cur_ref, halo_ref, w_ref, b_ref, dy_ref, *rest):
        if conformer:
            g_ref, lb_ref = rest[:2]
            rest = rest[2:]
        rest = rest[len(shared_in):]
        if conformer:
            dsrc_ref, dw_ref, db_ref, dg_ref, dlb_ref, hp_ref, acc_ref, dz_ref, dhp_ref, carry_ref, hs_ref, dzs_ref = rest
        else:
            dsrc_ref, dw_ref, db_ref, hp_ref, acc_ref, dz_ref, dhp_ref, carry_ref, hs_ref, dzs_ref = rest
        s, ii = pl.program_id(0), pl.program_id(1)
        i = nblk - 1 - ii
        first = (i == 0)

        @pl.when((s == 0) & (ii == 0))
        def _():
            dw_ref[...] = jnp.zeros_like(dw_ref)
            db_ref[...] = jnp.zeros_like(db_ref)
            hp_ref[pl.ds(halo + TB, 8), :] = jnp.zeros((8, cout), f32)
            if conformer:
                dg_ref[...] = jnp.zeros_like(dg_ref)
                dlb_ref[...] = jnp.zeros_like(dlb_ref)

        @pl.when(ii == 0)
        def _():
            carry_ref[...] = jnp.zeros_like(carry_ref)
            dz_ref[pl.ds(0, halo), :] = jnp.zeros((halo, cout), f32)
            dz_ref[pl.ds(halo + TB, halo), :] = jnp.zeros((halo, cout), f32)

        if conformer:
            hp_ref[pl.ds(halo, TB), :] = _glu(cur_ref[:, 0:cout], cur_ref[:, cout:cin])
            hh = _glu(halo_ref[:, 0:cout], halo_ref[:, cout:cin])
        else:
            hp_ref[pl.ds(halo, TB), :] = cur_ref[...]
            hh = halo_ref[...]
        hp_ref[pl.ds(0, halo), :] = jnp.where(first, 0.0, hh)
        _conv_taps(hp_ref, hs_ref, w_ref, b_ref, acc_ref, kw, off, halo, cout)

        if conformer:
            for q in range(cout // PAIR):
                ls = pl.ds(q * PAIR, PAIR)
                _, vjp = jax.vjp(_ln_silu, acc_ref[:, ls], g_ref[:, ls], lb_ref[:, ls])
                da, dg, dlb = vjp(dy_ref[:, ls].astype(f32))
                dz_ref[pl.ds(halo, TB), ls] = da
                dg_ref[0:1, ls] += dg
                dlb_ref[0:1, ls] += dlb
        else:
            _, vjp = jax.vjp(jax.nn.silu, acc_ref[...])
            dz_ref[pl.ds(halo, TB), :] = vjp(dy_ref[...].astype(f32))[0]

        res_h = _residues(range(off, off + kw))
        res_z = _residues(range(kw))
        for c in range(cout // 128):
            ls = pl.ds(c * 128, 128)
            _shift_copies(dz_ref, dzs_ref, res_z, halo + TB + halo - 8, ls)
            dacc = dz_ref[pl.ds(halo, TB), ls]
            db_ref[0:1, ls] += jnp.sum(dacc, axis=0, keepdims=True)
            dhp = jnp.zeros((halo + TB, 128), f32)
            for k in range(kw):
                dw_ref[k:k + 1, ls] += jnp.sum(dacc * _shifted(hp_ref, hs_ref, res_h, off + k, TB, ls), axis=0, keepdims=True)
                dhp = dhp + w_ref[k:k + 1, ls] * _shifted(dz_ref, dzs_ref, res_z, kw - 1 - k, halo + TB, ls)
            dhp_ref[:, ls] = dhp
        dhp_ref[pl.ds(TB, halo), :] += carry_ref[...]
        carry_ref[...] = dhp_ref[pl.ds(0, halo), :]
        dcur = dhp_ref[pl.ds(halo, TB), :]
        if conformer:
            _, vjp = jax.vjp(_glu, cur_ref[:, 0:cout], cur_ref[:, cout:cin])
            dval, dgate = vjp(dcur)
            dsrc_ref[:, 0:cout] = dval.astype(dsrc_ref.dtype)
            dsrc_ref[:, cout:cin] = dgate.astype(dsrc_ref.dtype)
        else:
            dsrc_ref[...] = dcur.astype(dsrc_ref.dtype)

    def blk(s, ii):
        return s * nblk + (nblk - 1 - ii)

    row = pl.BlockSpec((1, cout), lambda s, ii: (0, 0))
    acc8 = pl.BlockSpec((8, cout), lambda s, ii: (0, 0))
    in_specs = [pl.BlockSpec((TB, cin), lambda s, ii: (blk(s, ii), col_block)),
                pl.BlockSpec((halo, cin), lambda s, ii: (jnp.maximum(blk(s, ii) * hb - 1, 0), col_block)),
                pl.BlockSpec((wrows, cout), lambda s, ii: (0, 0)), row,
                pl.BlockSpec((TB, cout), lambda s, ii: (blk(s, ii), dy_col_block))]
    args = [src, src, w, b.reshape(1, cout), dy]
    out_specs = [pl.BlockSpec((TB, cin), lambda s, ii: (blk(s, ii), dsrc_blk)),
                 pl.BlockSpec((wrows, cout), lambda s, ii: (0, 0)), acc8]
    out_shape = [dsrc_shape, _sds((wrows, cout), f32), _sds((8, cout), f32)]
    if conformer:
        in_specs += [row, row]
        args += [ln_g.reshape(1, cout), ln_b.reshape(1, cout)]
        out_specs += [acc8, acc8]
        out_shape += [_sds((8, cout), f32), _sds((8, cout), f32)]
    in_specs += [pl.BlockSpec(memory_space=pl.ANY)] * len(shared_in)
    args += shared_in
    return pl.pallas_call(
        body, name=name, grid=(n_seq, nblk), in_specs=in_specs, out_specs=out_specs, out_shape=out_shape,
        input_output_aliases=aliases,
        scratch_shapes=[pltpu.VMEM((halo + TB + 8, cout), f32), pltpu.VMEM((TB, cout), f32),
                        pltpu.VMEM((halo + TB + halo, cout), f32), pltpu.VMEM((halo + TB, cout), f32),
                        pltpu.VMEM((halo, cout), f32),
                        pltpu.VMEM((len(_residues(range(off, off + kw))), halo + TB, cout), f32),
                        pltpu.VMEM((len(_residues(range(kw))), halo + TB + halo - 8, cout), f32)],
        compiler_params=_cparams(("arbitrary", "arbitrary")),
    )(*args)


def _gelu(x):
    return 0.5 * x * (1.0 + lax.erf(x * (1.0 / math.sqrt(2.0))))


def _tril_mask(n):
    r = lax.broadcasted_iota(jnp.int32, (n, n), 0)
    c = lax.broadcasted_iota(jnp.int32, (n, n), 1)
    return r >= c


def _head_spread(nh):
    r = lax.broadcasted_iota(jnp.int32, (nh, nh * HEAD), 0)
    c = lax.broadcasted_iota(jnp.int32, (nh, nh * HEAD), 1)
    return (c // HEAD == r).astype(f32)


def _gmlp_bias(bs):
    return lax.dot_general(bs, _head_spread(bs.shape[0]), (((0,), (0,)), ((), ())), precision=HI, preferred_element_type=f32)


def _gmlp_pair(bu, bv, g, b, w_a, w_b, bias):
    lo = lax.broadcasted_iota(jnp.int32, bu.shape, 1) < HEAD
    tril = _tril_mask(CHUNK)
    u = _gelu(bu)
    vb = _pair_ln(_gelu(bv), g, b).astype(bf16)
    mix = jnp.where(lo, jnp.dot(jnp.where(tril, w_a, 0.0).astype(bf16), vb, preferred_element_type=f32),
                    jnp.dot(jnp.where(tril, w_b, 0.0).astype(bf16), vb, preferred_element_type=f32))
    return u * (mix + bias)


def _gmlp_fwd(proj, col_block, ln_g, ln_b, w_s, b_s, share=None):
    T = proj.shape[0]
    nh = w_s.shape[0]
    width = nh * HEAD
    out_shape, out_blk, shared_in, aliases = _shared(share, T, width, bf16, 5)

    def body(p_ref, g_ref, b_ref, w_ref, bs_ref, *rest):
        out_ref, bias_ref = rest[len(shared_in):]

        @pl.when(pl.program_id(0) == 0)
        def _():
            bias_ref[...] = _gmlp_bias(bs_ref[...])

        for q in range(nh // 2):
            ls = pl.ds(q * PAIR, PAIR)
            lv = pl.ds(width + q * PAIR, PAIR)
            out_ref[:, ls] = _gmlp_pair(p_ref[:, ls], p_ref[:, lv], g_ref[:, ls], b_ref[:, ls], w_ref[2 * q], w_ref[2 * q + 1],
                                        bias_ref[:, ls]).astype(out_ref.dtype)

    row = pl.BlockSpec((1, width), lambda i: (0, 0))
    return pl.pallas_call(
        body, name="gmlp_fwd", grid=(T // CHUNK,),
        in_specs=[pl.BlockSpec((CHUNK, 2 * width), lambda i: (i, col_block)), row, row,
                  pl.BlockSpec((nh, CHUNK, CHUNK), lambda i: (0, 0, 0)), pl.BlockSpec((nh, CHUNK), lambda i: (0, 0))]
        + [pl.BlockSpec(memory_space=pl.ANY)] * len(shared_in),
        out_specs=pl.BlockSpec((CHUNK, width), lambda i: (i, out_blk)),
        out_shape=out_shape, input_output_aliases=aliases, scratch_shapes=[pltpu.VMEM((CHUNK, width), f32)],
        compiler_params=_cparams(("arbitrary",)),
    )(proj, ln_g.reshape(1, width), ln_b.reshape(1, width), w_s, b_s, *shared_in)


def _gmlp_bwd(proj, col_block, ln_g, ln_b, w_s, b_s, dy, dy_col_block, share=None):
    T = proj.shape[0]
    nh = w_s.shape[0]
    width = nh * HEAD
    nstep = T // CHUNK
    dp_shape, dp_blk, shared_in, aliases = _shared(share, T, 2 * width, bf16, 6)

    def body(p_ref, g_ref, b_ref, w_ref, bs_ref, dy_ref, *rest):
        dp_ref, dg_ref, db_ref, dw_ref, dbst_ref, bias_ref, dbias_ref = rest[len(shared_in):]

        @pl.when(pl.program_id(0) == 0)
        def _():
            dg_ref[...] = jnp.zeros_like(dg_ref)
            db_ref[...] = jnp.zeros_like(db_ref)
            dw_ref[...] = jnp.zeros_like(dw_ref)
            dbias_ref[...] = jnp.zeros_like(dbias_ref)
            bias_ref[...] = _gmlp_bias(bs_ref[...])

        for q in range(nh // 2):
            ls = pl.ds(q * PAIR, PAIR)
            lv = pl.ds(width + q * PAIR, PAIR)
            _, vjp = jax.vjp(_gmlp_pair, p_ref[:, ls], p_ref[:, lv], g_ref[:, ls], b_ref[:, ls], w_ref[2 * q], w_ref[2 * q + 1],
                             bias_ref[:, ls])
            dbu, dbv, dg, db, dw_a, dw_b, dbias = vjp(dy_ref[:, ls].astype(f32))
            dp_ref[:, ls] = dbu.astype(dp_ref.dtype)
            dp_ref[:, lv] = dbv.astype(dp_ref.dtype)
            dg_ref[0:1, ls] += dg
            db_ref[0:1, ls] += db
            dw_ref[2 * q] += dw_a
            dw_ref[2 * q + 1] += dw_b
            dbias_ref[:, ls] += dbias

        @pl.when(pl.program_id(0) == nstep - 1)
        def _():
            dbst_ref[...] = lax.dot_general(dbias_ref[...], _head_spread(nh), (((1,), (1,)), ((), ())),
                                            precision=HI, preferred_element_type=f32)

    row = pl.BlockSpec((1, width), lambda i: (0, 0))
    acc8 = pl.BlockSpec((8, width), lambda i: (0, 0))
    wspec = pl.BlockSpec((nh, CHUNK, CHUNK), lambda i: (0, 0, 0))
    res = pl.pallas_call(
        body, name="gmlp_bwd", grid=(nstep,),
        in_specs=[pl.BlockSpec((CHUNK, 2 * width), lambda i: (i, col_block)), row, row, wspec,
                  pl.BlockSpec((nh, CHUNK), lambda i: (0, 0)), pl.BlockSpec((CHUNK, width), lambda i: (i, dy_col_block))]
        + [pl.BlockSpec(memory_space=pl.ANY)] * len(shared_in),
        out_specs=[pl.BlockSpec((CHUNK, 2 * width), lambda i: (i, dp_blk)), acc8, acc8, wspec,
                   pl.BlockSpec((CHUNK, nh), lambda i: (0, 0))],
        out_shape=[dp_shape, _sds((8, width), f32), _sds((8, width), f32),
                   _sds((nh, CHUNK, CHUNK), f32), _sds((CHUNK, nh), f32)],
        input_output_aliases=aliases,
        scratch_shapes=[pltpu.VMEM((CHUNK, width), f32), pltpu.VMEM((CHUNK, width), f32)],
        compiler_params=_cparams(("arbitrary",)),
    )(proj, ln_g.reshape(1, width), ln_b.reshape(1, width), w_s, b_s, dy, *shared_in)
    return res[0], res[1], res[2], res[3], res[4].T


def _sel_col(x, h):
    lane = lax.broadcasted_iota(jnp.int32, x.shape, 1)
    return jnp.sum(jnp.where(lane == h, x, 0.0), axis=1, keepdims=True)


def _sel_row(x, h):
    sub = lax.broadcasted_iota(jnp.int32, x.shape, 0)
    return jnp.sum(jnp.where(sub == h, x, 0.0), axis=0, keepdims=True)


def _ssd_chunk(nh, ngrp, xs_l, z_l, b_l, c_l, dtraw, dtb, alog, dskip, ng_l, prev_l):
    hg = nh // ngrp
    tril = _tril_mask(CHUNK)
    tl = tril.astype(f32)
    lo = lax.broadcasted_iota(jnp.int32, (CHUNK, PAIR), 1) < HEAD
    lo_row = lo[0:1, :]
    dt = jax.nn.softplus(dtraw + dtb)
    a = dt * (-jnp.exp(alog))
    cs = jnp.dot(tl, a, precision=HI, preferred_element_type=f32)
    cst = lax.dot_general(a, tl, (((0,), (1,)), ((), ())), precision=HI, preferred_element_type=f32)
    cb_l = [lax.dot_general(c_l[g].astype(bf16), b_l[g].astype(bf16), (((1,), (1,)), ((), ())),
                            preferred_element_type=f32) for g in range(ngrp)]
    yz_l, new_prev = [], []
    for q in range(nh // 2):
        g = (2 * q) // hg
        cols = []
        for h in (2 * q, 2 * q + 1):
            cs_h = _sel_col(cs, h)
            tot = _sel_row(cs_h, CHUNK - 1)
            seg = jnp.where(tril, cs_h - _sel_row(cst, h), 0.0)
            lmat = jnp.where(tril, jnp.exp(seg), 0.0)
            cols.append((_sel_col(dt, h), cs_h, tot, lmat, _sel_col(dskip, h)))
        (dt_a, cs_a, tot_a, l_a, dsk_a), (dt_b, cs_b, tot_b, l_b, dsk_b) = cols
        xs = xs_l[q]
        x = xs * jnp.where(lo, dt_a, dt_b)
        xb = x.astype(bf16)
        ydiag = jnp.where(lo, jnp.dot((cb_l[g] * l_a).astype(bf16), xb, preferred_element_type=f32),
                          jnp.dot((cb_l[g] * l_b).astype(bf16), xb, preferred_element_type=f32))
        yoff = (jnp.dot(c_l[g].astype(bf16), prev_l[q].astype(bf16), preferred_element_type=f32)
                * jnp.where(lo, jnp.exp(cs_a), jnp.exp(cs_b)))
        xdec = x * jnp.where(lo, jnp.exp(tot_a - cs_a), jnp.exp(tot_b - cs_b))
        st = lax.dot_general(b_l[g].astype(bf16), xdec.astype(bf16), (((0,), (0,)), ((), ())),
                             preferred_element_type=f32)
        new_prev.append(prev_l[q] * jnp.where(lo_row, jnp.exp(tot_a), jnp.exp(tot_b)) + st)
        y = ydiag + yoff + jnp.where(lo_row, dsk_a, dsk_b) * xs
        yz_l.append(y * jax.nn.silu(z_l[q]))
    out = [None] * (nh // 2)
    qg = hg // 2
    for g in range(ngrp):
        ssq = sum(jnp.sum(yz_l[q] * yz_l[q], axis=-1, keepdims=True) for q in range(g * qg, (g + 1) * qg))
        r = lax.rsqrt(ssq * (1.0 / (hg * HEAD)) + EPS)
        for q in range(g * qg, (g + 1) * qg):
            out[q] = yz_l[q] * r * ng_l[q]
    return out, new_prev


def _ssd_read(nh, ngrp, nst, xbc_ref, z_ref, ng_ref, st_ref):
    cw = nh * HEAD
    xs_l = [xbc_ref[:, pl.ds(q * PAIR, PAIR)] for q in range(nh // 2)]
    b_l = [xbc_ref[:, pl.ds(cw + g * nst, nst)] for g in range(ngrp)]
    c_l = [xbc_ref[:, pl.ds(cw + ngrp * nst + g * nst, nst)] for g in range(ngrp)]
    z_l = [z_ref[:, pl.ds(q * PAIR, PAIR)] for q in range(nh // 2)]
    ng_l = [ng_ref[:, pl.ds(q * PAIR, PAIR)] for q in range(nh // 2)]
    prev_l = [st_ref[:, pl.ds(q * PAIR, PAIR)] for q in range(nh // 2)]
    return xs_l, z_l, b_l, c_l, ng_l, prev_l


def _ssd_fwd(xbc, proj, z_col_block, pdt, dtb, alog, dskip, ng, nh, ngrp, nst, n_seq, share=None):
    T = xbc.shape[0]
    cw = nh * HEAD
    nchunk = T // n_seq // CHUNK
    assert nst == CHUNK
    y_shape, y_blk, shared_in, aliases = _shared(share, T, cw, bf16, 7)

    def body(xbc_ref, z_ref, dt_ref, dtb_ref, alog_ref, dskip_ref, ng_ref, *rest):
        y_ref, sin_ref, st_ref = rest[len(shared_in):]

        @pl.when(pl.program_id(1) == 0)
        def _():
            st_ref[...] = jnp.zeros_like(st_ref)

        sin_ref[...] = st_ref[...]
        xs_l, z_l, b_l, c_l, ng_l, prev_l = _ssd_read(nh, ngrp, nst, xbc_ref, z_ref, ng_ref, st_ref)
        y_l, new_prev = _ssd_chunk(nh, ngrp, xs_l, z_l, b_l, c_l, dt_ref[...], dtb_ref[...], alog_ref[...],
                                   dskip_ref[...], ng_l, prev_l)
        for q in range(nh // 2):
            ls = pl.ds(q * PAIR, PAIR)
            y_ref[:, ls] = y_l[q].astype(y_ref.dtype)
            st_ref[:, ls] = new_prev[q]

    def blk(s, c):
        return s * nchunk + c

    prow = pl.BlockSpec((1, 128), lambda s, c: (0, 0))
    return pl.pallas_call(
        body, name="ssd_fwd", grid=(n_seq, nchunk),
        in_specs=[pl.BlockSpec((CHUNK, xbc.shape[1]), lambda s, c: (blk(s, c), 0)),
                  pl.BlockSpec((CHUNK, cw), lambda s, c: (blk(s, c), z_col_block)),
                  pl.BlockSpec((CHUNK, 128), lambda s, c: (blk(s, c), 0)),
                  prow, prow, prow, pl.BlockSpec((1, cw), lambda s, c: (0, 0))]
        + [pl.BlockSpec(memory_space=pl.ANY)] * len(shared_in),
        out_specs=[pl.BlockSpec((CHUNK, cw), lambda s, c: (blk(s, c), y_blk)),
                   pl.BlockSpec((nst, cw), lambda s, c: (blk(s, c), 0))],
        out_shape=[y_shape, _sds((T, cw), f32)], input_output_aliases=aliases,
        scratch_shapes=[pltpu.VMEM((nst, cw), f32)],
        compiler_params=_cparams(("arbitrary", "arbitrary")),
    )(xbc, proj, pdt, dtb, alog, dskip, ng.reshape(1, cw), *shared_in)


def _ssd_bwd(xbc, proj, z_col_block, pdt, dtb, alog, dskip, ng, sin, dy, dy_col_block, nh, ngrp, nst, n_seq, share=None):
    T, xw = xbc.shape
    cw = nh * HEAD
    nchunk = T // n_seq // CHUNK
    dz_shape, dz_blk, shared_in, aliases = _shared(share, T, cw, bf16, 9, out_index=1)

    def body(xbc_ref, z_ref, dt_ref, dtb_ref, alog_ref, dskip_ref, ng_ref, sin_ref, dy_ref, *rest):
        dxbc_ref, dz_ref, ddt_ref, ddtb_ref, dalog_ref, ddskip_ref, dng_ref, dst_ref = rest[len(shared_in):]
        s, cc = pl.program_id(0), pl.program_id(1)

        @pl.when((s == 0) & (cc == 0))
        def _():
            ddtb_ref[...] = jnp.zeros_like(ddtb_ref)
            dalog_ref[...] = jnp.zeros_like(dalog_ref)
            ddskip_ref[...] = jnp.zeros_like(ddskip_ref)
            dng_ref[...] = jnp.zeros_like(dng_ref)

        @pl.when(cc == 0)
        def _():
            dst_ref[...] = jnp.zeros_like(dst_ref)

        xs_l, z_l, b_l, c_l, ng_l, prev_l = _ssd_read(nh, ngrp, nst, xbc_ref, z_ref, ng_ref, sin_ref)
        _, vjp = jax.vjp(functools.partial(_ssd_chunk, nh, ngrp), xs_l, z_l, b_l, c_l, dt_ref[...], dtb_ref[...],
                         alog_ref[...], dskip_ref[...], ng_l, prev_l)
        dy_l = [dy_ref[:, pl.ds(q * PAIR, PAIR)].astype(f32) for q in range(nh // 2)]
        dst_l = [dst_ref[:, pl.ds(q * PAIR, PAIR)] for q in range(nh // 2)]
        dxs_l, dz_l, db_l, dc_l, ddt, ddtb, dalog, ddskip, dng_l, dprev_l = vjp((dy_l, dst_l))
        for q in range(nh // 2):
            ls = pl.ds(q * PAIR, PAIR)
            dxbc_ref[:, ls] = dxs_l[q]
            dz_ref[:, ls] = dz_l[q].astype(dz_ref.dtype)
            dng_ref[0:1, ls] += dng_l[q]
            dst_ref[:, ls] = dprev_l[q]
        for g in range(ngrp):
            dxbc_ref[:, pl.ds(cw + g * nst, nst)] = db_l[g]
            dxbc_ref[:, pl.ds(cw + ngrp * nst + g * nst, nst)] = dc_l[g]
        ddt_ref[...] = ddt.astype(ddt_ref.dtype)
        ddtb_ref[0:1, :] += ddtb
        dalog_ref[0:1, :] += dalog
        ddskip_ref[0:1, :] += ddskip

    def blk(s, cc):
        return s * nchunk + (nchunk - 1 - cc)

    prow = pl.BlockSpec((1, 128), lambda s, c: (0, 0))
    pacc = pl.BlockSpec((8, 128), lambda s, c: (0, 0))
    return pl.pallas_call(
        body, name="ssd_bwd", grid=(n_seq, nchunk),
        in_specs=[pl.BlockSpec((CHUNK, xw), lambda s, c: (blk(s, c), 0)),
                  pl.BlockSpec((CHUNK, cw), lambda s, c: (blk(s, c), z_col_block)),
                  pl.BlockSpec((CHUNK, 128), lambda s, c: (blk(s, c), 0)),
                  prow, prow, prow, pl.BlockSpec((1, cw), lambda s, c: (0, 0)),
                  pl.BlockSpec((nst, cw), lambda s, c: (blk(s, c), 0)),
                  pl.BlockSpec((CHUNK, cw), lambda s, c: (blk(s, c), dy_col_block))]
        + [pl.BlockSpec(memory_space=pl.ANY)] * len(shared_in),
        out_specs=[pl.BlockSpec((CHUNK, xw), lambda s, c: (blk(s, c), 0)),
                   pl.BlockSpec((CHUNK, cw), lambda s, c: (blk(s, c), dz_blk)),
                   pl.BlockSpec((CHUNK, 128), lambda s, c: (blk(s, c), 0)),
                   pacc, pacc, pacc, pl.BlockSpec((8, cw), lambda s, c: (0, 0))],
        out_shape=[_sds((T, xw), f32), dz_shape, _sds((T, 128), bf16),
                   _sds((8, 128), f32), _sds((8, 128), f32), _sds((8, 128), f32), _sds((8, cw), f32)],
        input_output_aliases=aliases,
        scratch_shapes=[pltpu.VMEM((nst, cw), f32)],
        compiler_params=_cparams(("arbitrary", "arbitrary")),
    )(xbc, proj, pdt, dtb, alog, dskip, ng.reshape(1, cw), sin, dy, *shared_in)


_HBM = pl.BlockSpec(memory_space=pltpu.HBM)
_SEM = pl.BlockSpec(memory_space=pltpu.SEMAPHORE)
_EFFECT = pltpu.SideEffectType.DATAFLOW_SIDE_EFFECTING


def _split_copies(n, scatter, src_refs, land_refs, send_sems, recv_sems):
    npeer = N_DEV - 1
    x, y, c = lax.axis_index("x"), lax.axis_index("y"), lax.axis_index("c")
    me = 4 * x + 2 * y + c
    copies = []
    for i in range(n):
        for k in range(1, N_DEV):
            px = 1 - x if k & 4 else x
            py = 1 - y if k & 2 else y
            pc = 1 - c if k & 1 else c
            src = src_refs[i].at[4 * px + 2 * py + pc] if scatter else src_refs[i]
            copies.append(pltpu.make_async_remote_copy(
                src_ref=src, dst_ref=land_refs[i].at[me],
                send_sem=send_sems.at[i * npeer + k - 1], recv_sem=recv_sems.at[i * npeer + k - 1],
                device_id=(px, py, pc), device_id_type=pl.DeviceIdType.MESH))
    return copies


def _exchange_start(name, arrs, scatter):
    n = len(arrs)
    nsem = n * (N_DEV - 1)
    me = 4 * lax.axis_index("x") + 2 * lax.axis_index("y") + lax.axis_index("c")
    lands = []
    for a in arrs:
        own = lax.dynamic_index_in_dim(a, me, 0, keepdims=True) if scatter else a[None]
        full = lax.empty(a.shape if scatter else (N_DEV,) + a.shape, a.dtype)
        lands.append(lax.dynamic_update_slice(full, own, (me,) + (0,) * (full.ndim - 1)))

    def body(*refs):
        src_refs, land_refs = refs[:n], refs[n:2 * n]
        send_sems, recv_sems = refs[2 * n], refs[2 * n + 1]
        token = refs[-1]
        for cp in _split_copies(n, scatter, src_refs, land_refs, send_sems, recv_sems):
            cp.start()
        token[...] = jnp.zeros_like(token)

    res = pl.pallas_call(
        body, name=name,
        out_shape=(pltpu.SemaphoreType.DMA((nsem,)), pltpu.SemaphoreType.DMA((nsem,)),
                   *[pltpu.HBM(a.shape, a.dtype) for a in arrs], *[pltpu.HBM(l.shape, l.dtype) for l in lands],
                   _sds((8, 128), f32)),
        in_specs=[_HBM] * (2 * n),
        out_specs=(_SEM, _SEM, *[_HBM] * (2 * n), pl.BlockSpec(memory_space=pltpu.VMEM)),
        input_output_aliases={j: 2 + j for j in range(2 * n)},
        compiler_params=pltpu.CompilerParams(has_side_effects=_EFFECT),
    )(*[pltpu.with_memory_space_constraint(a, pltpu.HBM) for a in arrs],
      *[pltpu.with_memory_space_constraint(l, pltpu.HBM) for l in lands])
    return (n, scatter, res[0], res[1], res[2:2 + n], res[2 + n:2 + 2 * n]), res[-1]


def _exchange_wait(name, handle, after):
    n, scatter, send_sems, recv_sems, srcs, lands = handle
    after = list(after) if isinstance(after, (list, tuple)) else [after]

    def body(*refs):
        src_refs, land_refs = refs[:n], refs[n:2 * n]
        for cp in _split_copies(n, scatter, src_refs, land_refs, refs[2 * n], refs[2 * n + 1]):
            cp.wait_send()
            cp.wait_recv()

    res = pl.pallas_call(
        body, name=name,
        out_shape=[pltpu.HBM(a.shape, a.dtype) for a in (*srcs, *lands)],
        in_specs=[_HBM] * (2 * n) + [_SEM, _SEM] + [pl.BlockSpec(memory_space=pl.ANY)] * len(after),
        out_specs=[_HBM] * (2 * n),
        input_output_aliases={j: j for j in range(2 * n)},
        compiler_params=pltpu.CompilerParams(has_side_effects=_EFFECT),
    )(*srcs, *lands, send_sems, recv_sems, *after)
    return res[n:]


def _adam_tiles(R, C):
    if R % 256 == 0:
        return (256, C), (R // 256, 1)
    assert C % 128 == 0
    return (R, 128), (1, C // 128)


def _adam(name, parts, w, m, v):
    P, R, C = parts.shape
    (tr, tc), (gr, gc) = _adam_tiles(R, C)
    c1 = 1.0 / (1.0 - ADAM_B1 ** ADAM_STEP)
    c2 = 1.0 / (1.0 - ADAM_B2 ** ADAM_STEP)

    def body(p_ref, w_ref, m_ref, v_ref, g_ref, d_ref, nm_ref, nv_ref):
        g = p_ref[0].astype(f32)
        for s in range(1, P):
            g = g + p_ref[s].astype(f32)
        nm = ADAM_B1 * m_ref[...] + (1.0 - ADAM_B1) * g
        nv = ADAM_B2 * v_ref[...] + (1.0 - ADAM_B2) * (g * g)
        g_ref[...] = g
        nm_ref[...] = nm
        nv_ref[...] = nv
        d_ref[...] = -ADAM_LR * ((nm * c1) / (jnp.sqrt(nv * c2) + ADAM_EPS) + ADAM_WD * w_ref[...])

    tile = pl.BlockSpec((tr, tc), lambda i, j: (i, j))
    return pl.pallas_call(
        body, name=name, grid=(gr, gc),
        in_specs=[pl.BlockSpec((P, tr, tc), lambda i, j: (0, i, j)), tile, tile, tile],
        out_specs=[tile] * 4, out_shape=[_sds((R, C), f32)] * 4,
        compiler_params=_cparams(("arbitrary", "arbitrary")),
    )(parts, w, m, v)


def _sum_parts(name, parts):
    P, R, C = parts.shape
    tr = 256 if R % 256 == 0 else R

    def body(p_ref, o_ref):
        g = p_ref[0]
        for s in range(1, P):
            g = g + p_ref[s]
        o_ref[...] = g

    return pl.pallas_call(
        body, name=name, grid=(R // tr,),
        in_specs=[pl.BlockSpec((P, tr, C), lambda i: (0, i, 0))], out_specs=pl.BlockSpec((tr, C), lambda i: (i, 0)),
        out_shape=_sds((R, C), f32), compiler_params=_cparams(("arbitrary",)),
    )(parts)


def _pad_to(a, n, axis):
    if a.shape[axis] == n:
        return a
    cfg = [(0, 0)] * a.ndim
    cfg[axis] = (0, n - a.shape[axis])
    return jnp.pad(a, cfg)


def _pack(arrs):
    flat = [_pad_to(a.reshape(-1), -(-a.size // 128) * 128, 0) for a in arrs]
    rows = jnp.concatenate(flat).reshape(-1, 128)
    return _pad_to(rows, -(-rows.shape[0] // 256) * 256, 0)


def _unpack(slab, shapes):
    flat = slab.reshape(-1)
    out, o = [], 0
    for s in shapes:
        n = math.prod(s)
        out.append(flat[o:o + n].reshape(s))
        o += -(-n // 128) * 128
    return out


_NAMES = ['norm1_g', 'w_in', 'conv_a_w', 'conv_a_b', 'ln_a_g', 'ln_a_b', 'ln_b_g', 'ln_b_b', 'w_spatial', 'b_spatial',
          'conv_c_w', 'conv_c_b', 'dt_bias', 'a_log', 'd_skip', 'norm_c_g', 'w_out', 'norm2_g', 'w_ff1', 'w_ff2', 'final_g']
_REPL = ['norm1_g', 'conv_a_b', 'ln_a_g', 'ln_a_b', 'ln_b_g', 'ln_b_b', 'w_spatial', 'b_spatial', 'conv_c_b',
         'dt_bias', 'a_log', 'd_skip', 'norm_c_g', 'norm2_g']
_CONVW = ['conv_a_w', 'conv_c_w']
_BIG = ['w_in', 'w_out', 'w_ff1', 'w_ff2']
_BIG_T = {'w_in': True, 'w_out': False, 'w_ff1': True, 'w_ff2': False}


def _row128(v):
    return _pad_to(v.reshape(1, -1), 128, 1)


def _step(p, m, v, x, loss_target):
    nb, S, D = x.shape
    T = nb * S
    depth = p['norm1_g'].shape[0]
    a_w = p['conv_a_b'].shape[1]
    b_w = p['ln_b_g'].shape[1]
    nh = p['dt_bias'].shape[1]
    c_w = p['norm_c_g'].shape[1]
    xw = p['conv_c_b'].shape[1]
    ngrp = 2
    nst = (xw - c_w) // (2 * ngrp)
    d_in = p['w_in'].shape[2] * N_DEV
    main = d_in - nh
    assert main == 2 * a_w + 2 * b_w + c_w + xw and 2 * a_w == 2 * b_w == c_w and xw % c_w == c_w // 2
    me = 4 * lax.axis_index("x") + 2 * lax.axis_index("y") + lax.axis_index("c")

    x2 = x.reshape(T, D)
    tgt = loss_target.reshape(T, D)

    def shards(i, z=None):
        z = 0.0 if z is None else z
        return [(p['w_in'][i].T + z).astype(bf16), (p['w_out'][i] + z).astype(bf16), (p['w_ff1'][i].T + z).astype(bf16),
                (p['w_ff2'][i] + z).astype(bf16), p['conv_a_w'][i], p['conv_c_w'][i]]

    def gathered_in(wt, ca, cc):
        wt = wt.reshape(d_in, D)
        ca = jnp.transpose(ca, (1, 0, 2)).reshape(KA, a_w)
        cc = jnp.transpose(cc, (1, 0, 2)).reshape(KC, xw)
        return dict(wt_main=wt[:main], wt_dt=_pad_to(wt[main:], 128, 0), ca=_pad_to(ca, 32, 0), cc=_pad_to(cc, 8, 0))

    def start_layer(i, after=None):
        sh = shards(i, None if after is None else after[0, 0])
        ha, t = _exchange_start("gather_w%da_start" % i, [sh[0], sh[4], sh[5]], False)
        hb, t = _exchange_start("gather_w%db_start" % i, [shards(i, t[0, 0])[1]], False)
        hc, t = _exchange_start("gather_w%dc_start" % i, [shards(i, t[0, 0])[2]], False)
        hd, t = _exchange_start("gather_w%dd_start" % i, [shards(i, t[0, 0])[3]], False)
        return dict(a=ha, b=hb, c=hc, d=hd), t

    W, saved = [], []
    xc = x2
    H, tok = start_layer(0)
    for i in range(depth):
        w = gathered_in(*_exchange_wait("gather_w%da_wait" % i, H['a'], [xc, tok]))
        W.append(w)
        Hi = H
        h1, rtok = _rms_fwd(xc, p['norm1_g'][i])
        if i + 1 < depth:
            H, tok = start_layer(i + 1, rtok + tok)
        else:
            tok = None
        (proj,) = _mm("mm_proj", h1, w['wt_main'], "nt", [f32], dep=tok)
        (pdt,) = _mm("mm_pdt", h1, w['wt_dt'], "nt", [f32])
        mixw = a_w + b_w + c_w
        ycat = _conv_fwd("confa_fwd", proj, 0, w['ca'], p['conv_a_b'][i], KA, True, nb, p['ln_a_g'][i], p['ln_a_b'][i],
                         share=(mixw, 0, None))
        ycat = _gmlp_fwd(proj, 1, p['ln_b_g'][i], p['ln_b_b'][i], p['w_spatial'][i], p['b_spatial'][i], share=(mixw, 1, ycat))
        xbc = _conv_fwd("convc_fwd", proj, 2, w['cc'], p['conv_c_b'][i], KC, False, nb)
        dtb, alog, dsk = _row128(p['dt_bias'][i]), _row128(p['a_log'][i]), _row128(p['d_skip'][i])
        ycat, sin = _ssd_fwd(xbc, proj, 2, pdt, dtb, alog, dsk, p['norm_c_g'][i], nh, ngrp, nst, nb, share=(mixw, 1, ycat))
        w['wout'] = _exchange_wait("gather_w%db_wait" % i, Hi['b'], ycat)[0].reshape(-1, D)
        (xm,) = _mm("mm_out", ycat, w['wout'], "nn", [f32], _ep_add, (xc,))
        h2, _ = _rms_fwd(xm, p['norm2_g'][i])
        w['w1t'] = _exchange_wait("gather_w%dc_wait" % i, Hi['c'], h2)[0].reshape(-1, D)
        f, a = _mm("mm_ff1", h2, w['w1t'], "nt", [bf16, bf16], _ep_relu2)
        w['w2'] = _exchange_wait("gather_w%dd_wait" % i, Hi['d'], a)[0].reshape(-1, D)
        (xo,) = _mm("mm_ff2", a, w['w2'], "nn", [f32], _ep_add, (xm,))
        saved.append(dict(x_in=xc, h1=h1, proj=proj, pdt=pdt, xbc=xbc, sin=sin, ycat=ycat, xm=xm, h2=h2, f=f, a=a,
                          dtb=dtb, alog=alog, dsk=dsk))
        xc = xo

    lp, dx, dfinal = _loss_head(xc, p['final_g'], tgt)
    loss = lax.psum(lp[0, 0], ("x", "y", "c"))

    out = {}
    kinds = ("grad", "delta", "new_m", "new_v")
    names1 = _REPL + _CONVW

    started, small = [], [None] * depth

    def send(n, i, g):
        handle, token = _exchange_start("scatter_%s_%d_start" % (n, i), [g.reshape(N_DEV, -1, D)], True)
        started.append((n, i, handle))
        return token

    tok = None
    for i in reversed(range(depth)):
        w, sv = W[i], saved[i]
        (df,) = _mm("mm_df", dx, w['w2'], "nt", [bf16], _ep_drelu2, (sv['f'],), dep=tok)
        (gw2,) = _mm("mm_gw2", sv['a'], dx, "tn", [bf16])
        tok = send('w_ff2', i, gw2)
        (dh2,) = _mm("mm_dh2", df, w['w1t'], "nn", [f32], dep=tok)
        (gw1t,) = _mm("mm_gw1", df, sv['h2'], "tn", [bf16])
        tok = send('w_ff1', i, gw1t)
        dxm, dg2 = _rms_bwd(sv['xm'], p['norm2_g'][i], dh2, dx)
        (dycat,) = _mm("mm_dycat", dxm, w['wout'], "nt", [bf16], dep=tok)
        (gwout,) = _mm("mm_gwout", sv['ycat'], dxm, "tn", [bf16])
        tok = send('w_out', i, gwout)
        dproj, dwa, dba, dlag, dlab = _conv_bwd("confa_bwd", sv['proj'], 0, w['ca'], p['conv_a_b'][i] + tok[0, 0], dycat, 0, KA,
                                                True, nb, p['ln_a_g'][i], p['ln_a_b'][i], share=(main, 0, None))
        dproj, dlbg, dlbb, dws, dbs = _gmlp_bwd(sv['proj'], 1, p['ln_b_g'][i], p['ln_b_b'][i], p['w_spatial'][i],
                                                p['b_spatial'][i], dycat, 1, share=(main, 1, dproj))
        dxbc, dproj, ddt, ddtb, dalog, ddsk, dng = _ssd_bwd(sv['xbc'], sv['proj'], 2, sv['pdt'], sv['dtb'], sv['alog'],
                                                            sv['dsk'], p['norm_c_g'][i], sv['sin'], dycat, 1, nh, ngrp, nst, nb,
                                                            share=(main, 2, dproj))
        dproj, dwc, dbc = _conv_bwd("convc_bwd", sv['proj'], 2, w['cc'], p['conv_c_b'][i], dxbc, 0, KC, False, nb,
                                    share=(main, 2, dproj))
        (dh_main,) = _mm("mm_dh1", dproj, w['wt_main'], "nn", [f32])
        (dh,) = _mm("mm_dh1dt", ddt, w['wt_dt'], "nn", [f32], _ep_add, (dh_main,))
        (gwt_main,) = _mm("mm_gwin", dproj, sv['h1'], "tn", [bf16])
        (gwt_dt,) = _mm("mm_gwdt", ddt, sv['h1'], "tn", [bf16])
        tok = send('w_in', i, jnp.concatenate([gwt_main, gwt_dt[:nh]], axis=0))
        dx, dg1 = _rms_bwd(sv['x_in'], p['norm1_g'][i] + tok[0, 0], dh, dxm)

        gi = dict(norm1_g=dg1[0], norm2_g=dg2[0], conv_a_w=dwa[:KA], conv_a_b=dba[0], ln_a_g=dlag[0], ln_a_b=dlab[0],
                  ln_b_g=dlbg[0], ln_b_b=dlbb[0], w_spatial=dws, b_spatial=dbs, conv_c_w=dwc[:KC], conv_c_b=dbc[0],
                  dt_bias=ddtb[0, :nh], a_log=dalog[0, :nh], d_skip=ddsk[0, :nh], norm_c_g=dng[0])
        parts_i = [gi[n] for n in names1] + ([dfinal[0]] if i == depth - 1 else [])
        handle, tok = _exchange_start("gather_g%d_start" % i, [_pack(parts_i)], False)
        small[i] = ([a.shape for a in parts_i], handle)
    grad_x = dx.reshape(nb, S, D)

    dep = [dx, tok]
    for n, i, handle in started:
        (parts,) = _exchange_wait("scatter_%s_%d_wait" % (n, i), handle, dep)
        tr = (lambda t: t.T) if _BIG_T[n] else (lambda t: t)
        res = _adam("adam_" + n, parts, tr(p[n][i]), tr(m[n][i]), tr(v[n][i]))
        for kind, r in zip(kinds, res):
            out.setdefault((kind, n), [None] * depth)[i] = tr(r)
        dep = res[3]

    gsum = [None] * depth
    for i in reversed(range(depth)):
        (parts,) = _exchange_wait("gather_g%d_wait" % i, small[i][1], dep)
        gsum[i] = _sum_parts("sum_small", parts)
        dep = gsum[i]
    widths = [-(-math.prod(p[n].shape[1:]) // 128) * 128 for n in _REPL]
    rep_rows = sum(widths) // 128
    tot_rows = -(-depth * rep_rows // 256) * 256

    def rep_slab(q):
        cols = [_pad_to(q[n].reshape(depth, -1), wd, 1) for n, wd in zip(_REPL, widths)]
        return _pad_to(jnp.concatenate(cols, axis=1).reshape(-1, 128), tot_rows, 0)

    g_rep = _pad_to(jnp.concatenate([g[:rep_rows] for g in gsum], axis=0), tot_rows, 0)
    res = _adam("adam_small", g_rep[None], rep_slab(p), rep_slab(m), rep_slab(v))
    for kind, r in zip(kinds, res):
        view = r[:depth * rep_rows].reshape(depth, -1)
        o = 0
        for n, wd in zip(_REPL, widths):
            out[(kind, n)] = view[:, o:o + math.prod(p[n].shape[1:])].reshape(p[n].shape)
            o += wd

    extra = []
    for i in range(depth):
        tail = _unpack(gsum[i][rep_rows:], small[i][0][len(_REPL):])
        extra.append(tail)
    gconv = []
    for j, n in enumerate(_CONVW):
        cw_shard = p[n].shape[2]
        full = jnp.stack([extra[i][j] for i in range(depth)])
        gconv.append(lax.dynamic_slice_in_dim(full, me * cw_shard, cw_shard, axis=2))
    tail_names = _CONVW + ['final_g']
    res = _adam("adam_conv", _pack(gconv + [extra[depth - 1][len(_CONVW)]])[None],
                *[_pack([q[n] for n in tail_names]) for q in (p, m, v)])
    for kind, r in zip(kinds, res):
        for n, arr in zip(tail_names, _unpack(r, [p[n].shape for n in tail_names])):
            out[(kind, n)] = arr
    for n in _BIG:
        for kind in kinds:
            out[(kind, n)] = jnp.stack(out[(kind, n)])

    flat = [loss, grad_x]
    for kind in ("grad", "delta", "new_m", "new_v"):
        flat += [out[(kind, n)] for n in _NAMES]
    return tuple(flat)


def kernel(x, norm1_g, w_in, conv_a_w, conv_a_b, ln_a_g, ln_a_b, ln_b_g, ln_b_b, w_spatial, b_spatial, conv_c_w, conv_c_b, dt_bias, a_log, d_skip, norm_c_g, w_out, norm2_g, w_ff1, w_ff2, final_g, loss_target, m_norm1_g, m_w_in, m_conv_a_w, m_conv_a_b, m_ln_a_g, m_ln_a_b, m_ln_b_g, m_ln_b_b, m_w_spatial, m_b_spatial, m_conv_c_w, m_conv_c_b, m_dt_bias, m_a_log, m_d_skip, m_norm_c_g, m_w_out, m_norm2_g, m_w_ff1, m_w_ff2, m_final_g, v_norm1_g, v_w_in, v_conv_a_w, v_conv_a_b, v_ln_a_g, v_ln_a_b, v_ln_b_g, v_ln_b_b, v_w_spatial, v_b_spatial, v_conv_c_w, v_conv_c_b, v_dt_bias, v_a_log, v_d_skip, v_norm_c_g, v_w_out, v_norm2_g, v_w_ff1, v_w_ff2, v_final_g):
    p = dict(zip(_NAMES, (norm1_g, w_in, conv_a_w, conv_a_b, ln_a_g, ln_a_b, ln_b_g, ln_b_b, w_spatial, b_spatial, conv_c_w,
                          conv_c_b, dt_bias, a_log, d_skip, norm_c_g, w_out, norm2_g, w_ff1, w_ff2, final_g)))
    m = dict(zip(_NAMES, (m_norm1_g, m_w_in, m_conv_a_w, m_conv_a_b, m_ln_a_g, m_ln_a_b, m_ln_b_g, m_ln_b_b, m_w_spatial,
                          m_b_spatial, m_conv_c_w, m_conv_c_b, m_dt_bias, m_a_log, m_d_skip, m_norm_c_g, m_w_out, m_norm2_g,
                          m_w_ff1, m_w_ff2, m_final_g)))
    v = dict(zip(_NAMES, (v_norm1_g, v_w_in, v_conv_a_w, v_conv_a_b, v_ln_a_g, v_ln_a_b, v_ln_b_g, v_ln_b_b, v_w_spatial,
                          v_b_spatial, v_conv_c_w, v_conv_c_b, v_dt_bias, v_a_log, v_d_skip, v_norm_c_g, v_w_out, v_norm2_g,
                          v_w_ff1, v_w_ff2, v_final_g)))
    return _step(p, m, v, x, loss_target)
```

```python
import functools
import math

import jax
import jax.numpy as jnp
from jax import lax
from jax.experimental import pallas as pl
from jax.experimental.pallas import tpu as pltpu

f32 = jnp.float32
bf16 = jnp.bfloat16
HI = lax.Precision.HIGHEST
EPS = 1e-5
HEAD = 64
CHUNK = 128
KA = 31
KC = 4
N_DEV = 8
VMEM_LIMIT = 56 * 1024 * 1024
MM_VMEM_BUDGET = 52 * 1024 * 1024

ADAM_LR = 0.001
ADAM_B1 = 0.9
ADAM_B2 = 0.999
ADAM_EPS = 1e-08
ADAM_WD = 0.01
ADAM_STEP = 10


def _cparams(sem=None):
    return pltpu.CompilerParams(dimension_semantics=sem, vmem_limit_bytes=VMEM_LIMIT)


def _sds(shape, dtype):
    return jax.ShapeDtypeStruct(shape, dtype)


_DIMS = {"nn": ((1,), (0,)), "nt": ((1,), (1,)), "tn": ((0,), (0,))}


def _tile(n, cap):
    if n <= cap:
        return n
    for d in range(cap - cap % 128, 0, -128):
        if n % d == 0:
            return d
    raise ValueError((n, cap))


def _mm(name, a, b, form, out_dtypes, epilogue=None, extras=(), tm=2048, tn=1024, tk=2048, dep=None):
    if form == "tn":
        K, M = a.shape
    else:
        M, K = a.shape
    N = b.shape[0] if form == "nt" else b.shape[1]
    tm, tn, tk = _tile(M, tm), _tile(N, tn), _tile(K, tk)
    nk = K // tk

    def vmem_bytes(tm):
        mn = sum(jnp.dtype(e.dtype).itemsize for e in extras) + sum(jnp.dtype(d).itemsize for d in out_dtypes)
        return 2 * (tm * tk * a.dtype.itemsize + tk * tn * b.dtype.itemsize + tm * tn * mn) + 2 * tm * tn * 4

    while vmem_bytes(tm) > MM_VMEM_BUDGET and tm % 256 == 0:
        tm //= 2
    ne, no = len(extras), len(out_dtypes)
    deps = () if dep is None else (dep,)
    if epilogue is None:
        epilogue = lambda acc: (acc,)

    def body(a_ref, b_ref, *rest):
        extra_refs = rest[:ne]
        rest = rest[ne + len(deps):]
        out_refs = rest[:no]
        part = lax.dot_general(a_ref[...].astype(bf16), b_ref[...].astype(bf16),
                               (_DIMS[form], ((), ())), preferred_element_type=f32)

        def finish(acc):
            outs = epilogue(acc, *[e[...] for e in extra_refs])
            for o_ref, v in zip(out_refs, outs):
                o_ref[...] = v.astype(o_ref.dtype)

        if nk == 1:
            finish(part)
            return
        acc_ref = rest[no]
        k = pl.program_id(2)

        @pl.when(k == 0)
        def _():
            acc_ref[...] = part

        @pl.when((k > 0) & (k < nk - 1))
        def _():
            acc_ref[...] += part

        @pl.when(k == nk - 1)
        def _():
            finish(acc_ref[...] + part)

    a_spec = pl.BlockSpec((tk, tm), lambda i, j, k: (k, i)) if form == "tn" else pl.BlockSpec((tm, tk), lambda i, j, k: (i, k))
    b_spec = pl.BlockSpec((tn, tk), lambda i, j, k: (j, k)) if form == "nt" else pl.BlockSpec((tk, tn), lambda i, j, k: (k, j))
    mn_spec = pl.BlockSpec((tm, tn), lambda i, j, k: (i, j))
    return pl.pallas_call(
        body, name=name, grid=(M // tm, N // tn, nk),
        in_specs=[a_spec, b_spec] + [mn_spec] * ne + [pl.BlockSpec((8, 128), lambda i, j, k: (0, 0))] * len(deps),
        out_specs=[mn_spec] * no,
        out_shape=[_sds((M, N), d) for d in out_dtypes],
        scratch_shapes=[pltpu.VMEM((tm, tn), f32)] if nk > 1 else [],
        compiler_params=_cparams(("parallel", "parallel", "arbitrary")),
    )(a, b, *extras, *deps)


def _ep_add(acc, r):
    return (acc + r,)


def _ep_relu2(acc):
    r = jnp.maximum(acc, 0.0)
    return acc, r * r


def _ep_drelu2(acc, f):
    return (acc * 2.0 * jnp.maximum(f, 0.0),)


def _rms(x, g):
    return x * lax.rsqrt(jnp.mean(x * x, axis=-1, keepdims=True) + EPS) * g


TT = 512


def _rms_fwd(x, g):
    T, D = x.shape

    def body(x_ref, g_ref, h_ref, tok_ref):
        h_ref[...] = _rms(x_ref[...], g_ref[...]).astype(bf16)
        tok_ref[...] = jnp.zeros_like(tok_ref)

    return pl.pallas_call(
        body, name="rms_fwd", grid=(T // TT,),
        in_specs=[pl.BlockSpec((TT, D), lambda i: (i, 0)), pl.BlockSpec((1, D), lambda i: (0, 0))],
        out_specs=[pl.BlockSpec((TT, D), lambda i: (i, 0)), pl.BlockSpec((8, 128), lambda i: (0, 0))],
        out_shape=[_sds((T, D), bf16), _sds((8, 128), f32)], compiler_params=_cparams(("arbitrary",)),
    )(x, g.reshape(1, D))


def _rms_bwd(x, g, dh, dres):
    T, D = x.shape

    def body(x_ref, g_ref, dh_ref, dres_ref, dx_ref, dg_ref):
        _, vjp = jax.vjp(_rms, x_ref[...], g_ref[...])
        dx, dg = vjp(dh_ref[...])
        dx_ref[...] = dres_ref[...] + dx

        @pl.when(pl.program_id(0) == 0)
        def _():
            dg_ref[...] = jnp.zeros_like(dg_ref)

        dg_ref[0:1, :] += dg

    tile = pl.BlockSpec((TT, D), lambda i: (i, 0))
    return pl.pallas_call(
        body, name="rms_bwd", grid=(T // TT,),
        in_specs=[tile, pl.BlockSpec((1, D), lambda i: (0, 0)), tile, tile],
        out_specs=[tile, pl.BlockSpec((8, D), lambda i: (0, 0))],
        out_shape=[_sds((T, D), f32), _sds((8, D), f32)], compiler_params=_cparams(("arbitrary",)),
    )(x, g.reshape(1, D), dh, dres)


def _loss_head(x, g, tgt):
    T, D = x.shape

    def f(xv, gv, tv):
        e = _rms(xv, gv) - tv
        return 0.5 * jnp.sum(jnp.sum(e * e, axis=-1, keepdims=True) * (1.0 / D), axis=0, keepdims=True)

    def body(x_ref, g_ref, t_ref, loss_ref, dx_ref, dg_ref):
        tv = t_ref[...]
        l, vjp = jax.vjp(lambda xv, gv: f(xv, gv, tv), x_ref[...], g_ref[...])
        dx, dg = vjp(jnp.ones((1, 1), f32))
        dx_ref[...] = dx

        @pl.when(pl.program_id(0) == 0)
        def _():
            dg_ref[...] = jnp.zeros_like(dg_ref)
            loss_ref[...] = jnp.zeros_like(loss_ref)

        dg_ref[0:1, :] += dg
        loss_ref[...] += jnp.broadcast_to(l, loss_ref.shape)

    tile = pl.BlockSpec((TT, D), lambda i: (i, 0))
    return pl.pallas_call(
        body, name="loss_head", grid=(T // TT,),
        in_specs=[tile, pl.BlockSpec((1, D), lambda i: (0, 0)), tile],
        out_specs=[pl.BlockSpec((8, 128), lambda i: (0, 0)), tile, pl.BlockSpec((8, D), lambda i: (0, 0))],
        out_shape=[_sds((8, 128), f32), _sds((T, D), f32), _sds((8, D), f32)],
        compiler_params=_cparams(("arbitrary",)),
    )(x, g.reshape(1, D), tgt)


TB = 256


def _glu(a_val, a_gate):
    return a_val * jax.nn.sigmoid(a_gate)


PAIR = 2 * HEAD


def _pair_mean(x, lo):
    s_lo = jnp.sum(jnp.where(lo, x, 0.0), axis=-1, keepdims=True)
    s_hi = jnp.sum(jnp.where(lo, 0.0, x), axis=-1, keepdims=True)
    return jnp.where(lo, s_lo, s_hi) * (1.0 / HEAD)


def _pair_ln(v, g, b):
    lo = lax.broadcasted_iota(jnp.int32, v.shape, 1) < HEAD
    vc = v - _pair_mean(v, lo)
    var = _pair_mean(vc * vc, lo)
    return vc * lax.rsqrt(var + EPS) * g + b


def _ln_silu(v, g, b):
    return jax.nn.silu(_pair_ln(v, g, b))


def _conv_geom(kw):
    halo = 32 if kw > 9 else 16
    return halo, halo - (kw - 1)


def _residues(shifts):
    return sorted({s % 8 for s in shifts} - {0})


def _shift_copies(src_ref, cp_ref, res, rows, ls):
    for j, r in enumerate(res):
        cp_ref[j, :, ls] = src_ref[pl.ds(r, rows), ls]


def _shifted(src_ref, cp_ref, res, shift, size, ls):
    r = shift % 8
    if r == 0:
        return src_ref[pl.ds(shift, size), ls]
    return cp_ref[res.index(r), pl.ds(shift - r, size), ls]


def _conv_taps(hp_ref, hs_ref, w_ref, b_ref, acc_ref, kw, off, halo, width):
    res = _residues(range(off, off + kw))
    for c in range(width // 128):
        ls = pl.ds(c * 128, 128)
        _shift_copies(hp_ref, hs_ref, res, halo + TB, ls)
        acc = jnp.broadcast_to(b_ref[:, ls], (TB, 128))
        for k in range(kw):
            acc = acc + w_ref[k:k + 1, ls] * _shifted(hp_ref, hs_ref, res, off + k, TB, ls)
        acc_ref[:, ls] = acc


def _conv_fwd(name, src, col_block, w, b, kw, conformer, n_seq, ln_g=None, ln_b=None, share=None):
    T = src.shape[0]
    cout = w.shape[1]
    cin = 2 * cout if conformer else cout
    halo, off = _conv_geom(kw)
    nblk = T // n_seq // TB
    hb = TB // halo
    out_shape, out_blk, shared_in, aliases = _shared(share, T, cout, bf16 if conformer else f32, 6 if conformer else 4)

    def body(cur_ref, halo_ref, w_ref, b_ref, *rest):
        if conformer:
            g_ref, lb_ref = rest[:2]
            rest = rest[2:]
        out_ref, hp_ref, acc_ref, hs_ref = rest[len(shared_in):]
        i = pl.program_id(1)
        first = (i == 0)

        @pl.when((pl.program_id(0) == 0) & first)
        def _():
            hp_ref[pl.ds(halo + TB, 8), :] = jnp.zeros((8, cout), f32)

        if conformer:
            hp_ref[pl.ds(halo, TB), :] = _glu(cur_ref[:, 0:cout].astype(f32), cur_ref[:, cout:cin].astype(f32))
            hh = _glu(halo_ref[:, 0:cout].astype(f32), halo_ref[:, cout:cin].astype(f32))
        else:
            hp_ref[pl.ds(halo, TB), :] = cur_ref[...].astype(f32)
            hh = halo_ref[...].astype(f32)
        hp_ref[pl.ds(0, halo), :] = jnp.where(first, 0.0, hh)
        _conv_taps(hp_ref, hs_ref, w_ref, b_ref, acc_ref, kw, off, halo, cout)
        if conformer:
            for q in range(cout // PAIR):
                ls = pl.ds(q * PAIR, PAIR)
                out_ref[:, ls] = _ln_silu(acc_ref[:, ls], g_ref[:, ls], lb_ref[:, ls]).astype(out_ref.dtype)
        else:
            out_ref[...] = jax.nn.silu(acc_ref[...]).astype(out_ref.dtype)

    nres = len(_residues(range(off, off + kw)))

    row = pl.BlockSpec((1, cout), lambda s, i: (0, 0))
    in_specs = [pl.BlockSpec((TB, cin), lambda s, i: (s * nblk + i, col_block)),
                pl.BlockSpec((halo, cin), lambda s, i: (jnp.maximum((s * nblk + i) * hb - 1, 0), col_block)),
                pl.BlockSpec((w.shape[0], cout), lambda s, i: (0, 0)), row]
    args = [src, src, w, b.reshape(1, cout)]
    if conformer:
        in_specs += [row, row]
        args += [ln_g.reshape(1, cout), ln_b.reshape(1, cout)]
    in_specs += [pl.BlockSpec(memory_space=pl.ANY)] * len(shared_in)
    args += shared_in
    return pl.pallas_call(
        body, name=name, grid=(n_seq, nblk), in_specs=in_specs,
        out_specs=pl.BlockSpec((TB, cout), lambda s, i: (s * nblk + i, out_blk)),
        out_shape=out_shape, input_output_aliases=aliases,
        scratch_shapes=[pltpu.VMEM((halo + TB + 8, cout), f32), pltpu.VMEM((TB, cout), f32),
                        pltpu.VMEM((nres, halo + TB, cout), f32)],
        compiler_params=_cparams(("arbitrary", "arbitrary")),
    )(*args)


def _shared(share, T, width, dtype, n_inputs, out_index=0):
    if share is None:
        return _sds((T, width), dtype), 0, [], {}
    total, blk, into = share
    if into is None:
        return _sds((T, total), dtype), blk, [], {}
    return _sds((T, total), dtype), blk, [into], {n_inputs: out_index}


def _conv_bwd(name, src, col_block, w, b, dy, dy_col_block, kw, conformer, n_seq, ln_g=None, ln_b=None, share=None):
    T = src.shape[0]
    cout = w.shape[1]
    wrows = w.shape[0]
    cin = 2 * cout if conformer else cout
    halo, off = _conv_geom(kw)
    nblk = T // n_seq // TB
    hb = TB // halo
    dsrc_shape, dsrc_blk, shared_in, aliases = _shared(share, T, cin, bf16, 7 if conformer else 5)

    def body(cur_ref, halo_ref, w_ref, b_ref, dy_ref, *rest):
        if conformer:
            g_ref, lb_ref = rest[:2]
            rest = rest[2:]
        rest = rest[len(shared_in):]
        if conformer:
            dsrc_ref, dw_ref, db_ref, dg_ref, dlb_ref, hp_ref, acc_ref, dz_ref, dhp_ref, carry_ref, hs_ref, dzs_ref = rest
        else:
            dsrc_ref, dw_ref, db_ref, hp_ref, acc_ref, dz_ref, dhp_ref, carry_ref, hs_ref, dzs_ref = rest
        s, ii = pl.program_id(0), pl.program_id(1)
        i = nblk - 1 - ii
        first = (i == 0)

        @pl.when((s == 0) & (ii == 0))
        def _():
            dw_ref[...] = jnp.zeros_like(dw_ref)
            db_ref[...] = jnp.zeros_like(db_ref)
            hp_ref[pl.ds(halo + TB, 8), :] = jnp.zeros((8, cout), f32)
            if conformer:
                dg_ref[...] = jnp.zeros_like(dg_ref)
                dlb_ref[...] = jnp.zeros_like(dlb_ref)

        @pl.when(ii == 0)
        def _():
            carry_ref[...] = jnp.zeros_like(carry_ref)
            dz_ref[pl.ds(0, halo), :] = jnp.zeros((halo, cout), f32)
            dz_ref[pl.ds(halo + TB, halo), :] = jnp.zeros((halo, cout), f32)

        if conformer:
            hp_ref[pl.ds(halo, TB), :] = _glu(cur_ref[:, 0:cout].astype(f32), cur_ref[:, cout:cin].astype(f32))
            hh = _glu(halo_ref[:, 0:cout].astype(f32), halo_ref[:, cout:cin].astype(f32))
        else:
            hp_ref[pl.ds(halo, TB), :] = cur_ref[...].astype(f32)
            hh = halo_ref[...].astype(f32)
        hp_ref[pl.ds(0, halo), :] = jnp.where(first, 0.0, hh)
        _conv_taps(hp_ref, hs_ref, w_ref, b_ref, acc_ref, kw, off, halo, cout)

        if conformer:
            for q in range(cout // PAIR):
                ls = pl.ds(q * PAIR, PAIR)
                _, vjp = jax.vjp(_ln_silu, acc_ref[:, ls], g_ref[:, ls], lb_ref[:, ls])
                da, dg, dlb = vjp(dy_ref[:, ls].astype(f32))
                dz_ref[pl.ds(halo, TB), ls] = da
                dg_ref[0:1, ls] += dg
                dlb_ref[0:1, ls] += dlb
        else:
            _, vjp = jax.vjp(jax.nn.silu, acc_ref[...])
            dz_ref[pl.ds(halo, TB), :] = vjp(dy_ref[...].astype(f32))[0]

        res_h = _residues(range(off, off + kw))
        res_z = _residues(range(kw))
        for c in range(cout // 128):
            ls = pl.ds(c * 128, 128)
            _shift_copies(dz_ref, dzs_ref, res_z, halo + TB + halo - 8, ls)
            dacc = dz_ref[pl.ds(halo, TB), ls]
            db_ref[0:1, ls] += jnp.sum(dacc, axis=0, keepdims=True)
            dhp = jnp.zeros((halo + TB, 128), f32)
            for k in range(kw):
                dw_ref[k:k + 1, ls] += jnp.sum(dacc * _shifted(hp_ref, hs_ref, res_h, off + k, TB, ls), axis=0, keepdims=True)
                dhp = dhp + w_ref[k:k + 1, ls] * _shifted(dz_ref, dzs_ref, res_z, kw - 1 - k, halo + TB, ls)
            dhp_ref[:, ls] = dhp
        dhp_ref[pl.ds(TB, halo), :] += carry_ref[...]
        carry_ref[...] = dhp_ref[pl.ds(0, halo), :]
        dcur = dhp_ref[pl.ds(halo, TB), :]
        if conformer:
            _, vjp = jax.vjp(_glu, cur_ref[:, 0:cout].astype(f32), cur_ref[:, cout:cin].astype(f32))
            dval, dgate = vjp(dcur)
            dsrc_ref[:, 0:cout] = dval.astype(dsrc_ref.dtype)
            dsrc_ref[:, cout:cin] = dgate.astype(dsrc_ref.dtype)
        else:
            dsrc_ref[...] = dcur.astype(dsrc_ref.dtype)

    def blk(s, ii):
        return s * nblk + (nblk - 1 - ii)

    row = pl.BlockSpec((1, cout), lambda s, ii: (0, 0))
    acc8 = pl.BlockSpec((8, cout), lambda s, ii: (0, 0))
    in_specs = [pl.BlockSpec((TB, cin), lambda s, ii: (blk(s, ii), col_block)),
                pl.BlockSpec((halo, cin), lambda s, ii: (jnp.maximum(blk(s, ii) * hb - 1, 0), col_block)),
                pl.BlockSpec((wrows, cout), lambda s, ii: (0, 0)), row,
                pl.BlockSpec((TB, cout), lambda s, ii: (blk(s, ii), dy_col_block))]
    args = [src, src, w, b.reshape(1, cout), dy]
    out_specs = [pl.BlockSpec((TB, cin), lambda s, ii: (blk(s, ii), dsrc_blk)),
                 pl.BlockSpec((wrows, cout), lambda s, ii: (0, 0)), acc8]
    out_shape = [dsrc_shape, _sds((wrows, cout), f32), _sds((8, cout), f32)]
    if conformer:
        in_specs += [row, row]
        args += [ln_g.reshape(1, cout), ln_b.reshape(1, cout)]
        out_specs += [acc8, acc8]
        out_shape += [_sds((8, cout), f32), _sds((8, cout), f32)]
    in_specs += [pl.BlockSpec(memory_space=pl.ANY)] * len(shared_in)
    args += shared_in
    return pl.pallas_call(
        body, name=name, grid=(n_seq, nblk), in_specs=in_specs, out_specs=out_specs, out_shape=out_shape,
        input_output_aliases=aliases,
        scratch_shapes=[pltpu.VMEM((halo + TB + 8, cout), f32), pltpu.VMEM((TB, cout), f32),
                        pltpu.VMEM((halo + TB + halo, cout), f32), pltpu.VMEM((halo + TB, cout), f32),
                        pltpu.VMEM((halo, cout), f32),
                        pltpu.VMEM((len(_residues(range(off, off + kw))), halo + TB, cout), f32),
                        pltpu.VMEM((len(_residues(range(kw))), halo + TB + halo - 8, cout), f32)],
        compiler_params=_cparams(("arbitrary", "arbitrary")),
    )(*args)


def _gelu(x):
    return 0.5 * x * (1.0 + lax.erf(x * (1.0 / math.sqrt(2.0))))


def _tril_mask(n):
    r = lax.broadcasted_iota(jnp.int32, (n, n), 0)
    c = lax.broadcasted_iota(jnp.int32, (n, n), 1)
    return r >= c


def _head_spread(nh):
    r = lax.broadcasted_iota(jnp.int32, (nh, nh * HEAD), 0)
    c = lax.broadcasted_iota(jnp.int32, (nh, nh * HEAD), 1)
    return (c // HEAD == r).astype(f32)


def _gmlp_bias(bs):
    return lax.dot_general(bs, _head_spread(bs.shape[0]), (((0,), (0,)), ((), ())), precision=HI, preferred_element_type=f32)


def _gmlp_pair(bu, bv, g, b, w_a, w_b, bias):
    lo = lax.broadcasted_iota(jnp.int32, bu.shape, 1) < HEAD
    tril = _tril_mask(CHUNK)
    u = _gelu(bu)
    vb = _pair_ln(_gelu(bv), g, b).astype(bf16)
    mix = jnp.where(lo, jnp.dot(jnp.where(tril, w_a, 0.0).astype(bf16), vb, preferred_element_type=f32),
                    jnp.dot(jnp.where(tril, w_b, 0.0).astype(bf16), vb, preferred_element_type=f32))
    return u * (mix + bias)


def _gmlp_fwd(proj, col_block, ln_g, ln_b, w_s, b_s, share=None):
    T = proj.shape[0]
    nh = w_s.shape[0]
    width = nh * HEAD
    out_shape, out_blk, shared_in, aliases = _shared(share, T, width, bf16, 5)

    def body(p_ref, g_ref, b_ref, w_ref, bs_ref, *rest):
        out_ref, bias_ref = rest[len(shared_in):]

        @pl.when(pl.program_id(0) == 0)
        def _():
            bias_ref[...] = _gmlp_bias(bs_ref[...])

        for q in range(nh // 2):
            ls = pl.ds(q * PAIR, PAIR)
            lv = pl.ds(width + q * PAIR, PAIR)
            out_ref[:, ls] = _gmlp_pair(p_ref[:, ls].astype(f32), p_ref[:, lv].astype(f32), g_ref[:, ls], b_ref[:, ls], w_ref[2 * q], w_ref[2 * q + 1],
                                        bias_ref[:, ls]).astype(out_ref.dtype)

    row = pl.BlockSpec((1, width), lambda i: (0, 0))
    return pl.pallas_call(
        body, name="gmlp_fwd", grid=(T // CHUNK,),
        in_specs=[pl.BlockSpec((CHUNK, 2 * width), lambda i: (i, col_block)), row, row,
                  pl.BlockSpec((nh, CHUNK, CHUNK), lambda i: (0, 0, 0)), pl.BlockSpec((nh, CHUNK), lambda i: (0, 0))]
        + [pl.BlockSpec(memory_space=pl.ANY)] * len(shared_in),
        out_specs=pl.BlockSpec((CHUNK, width), lambda i: (i, out_blk)),
        out_shape=out_shape, input_output_aliases=aliases, scratch_shapes=[pltpu.VMEM((CHUNK, width), f32)],
        compiler_params=_cparams(("arbitrary",)),
    )(proj, ln_g.reshape(1, width), ln_b.reshape(1, width), w_s, b_s, *shared_in)


def _gmlp_bwd(proj, col_block, ln_g, ln_b, w_s, b_s, dy, dy_col_block, share=None):
    T = proj.shape[0]
    nh = w_s.shape[0]
    width = nh * HEAD
    nstep = T // CHUNK
    dp_shape, dp_blk, shared_in, aliases = _shared(share, T, 2 * width, bf16, 6)

    def body(p_ref, g_ref, b_ref, w_ref, bs_ref, dy_ref, *rest):
        dp_ref, dg_ref, db_ref, dw_ref, dbst_ref, bias_ref, dbias_ref = rest[len(shared_in):]

        @pl.when(pl.program_id(0) == 0)
        def _():
            dg_ref[...] = jnp.zeros_like(dg_ref)
            db_ref[...] = jnp.zeros_like(db_ref)
            dw_ref[...] = jnp.zeros_like(dw_ref)
            dbias_ref[...] = jnp.zeros_like(dbias_ref)
            bias_ref[...] = _gmlp_bias(bs_ref[...])

        for q in range(nh // 2):
            ls = pl.ds(q * PAIR, PAIR)
            lv = pl.ds(width + q * PAIR, PAIR)
            _, vjp = jax.vjp(_gmlp_pair, p_ref[:, ls].astype(f32), p_ref[:, lv].astype(f32), g_ref[:, ls], b_ref[:, ls], w_ref[2 * q], w_ref[2 * q + 1],
                             bias_ref[:, ls])
            dbu, dbv, dg, db, dw_a, dw_b, dbias = vjp(dy_ref[:, ls].astype(f32))
            dp_ref[:, ls] = dbu.astype(dp_ref.dtype)
            dp_ref[:, lv] = dbv.astype(dp_ref.dtype)
            dg_ref[0:1, ls] += dg
            db_ref[0:1, ls] += db
            dw_ref[2 * q] += dw_a
            dw_ref[2 * q + 1] += dw_b
            dbias_ref[:, ls] += dbias

        @pl.when(pl.program_id(0) == nstep - 1)
        def _():
            dbst_ref[...] = lax.dot_general(dbias_ref[...], _head_spread(nh), (((1,), (1,)), ((), ())),
                                            precision=HI, preferred_element_type=f32)

    row = pl.BlockSpec((1, width), lambda i: (0, 0))
    acc8 = pl.BlockSpec((8, width), lambda i: (0, 0))
    wspec = pl.BlockSpec((nh, CHUNK, CHUNK), lambda i: (0, 0, 0))
    res = pl.pallas_call(
        body, name="gmlp_bwd", grid=(nstep,),
        in_specs=[pl.BlockSpec((CHUNK, 2 * width), lambda i: (i, col_block)), row, row, wspec,
                  pl.BlockSpec((nh, CHUNK), lambda i: (0, 0)), pl.BlockSpec((CHUNK, width), lambda i: (i, dy_col_block))]
        + [pl.BlockSpec(memory_space=pl.ANY)] * len(shared_in),
        out_specs=[pl.BlockSpec((CHUNK, 2 * width), lambda i: (i, dp_blk)), acc8, acc8, wspec,
                   pl.BlockSpec((CHUNK, nh), lambda i: (0, 0))],
        out_shape=[dp_shape, _sds((8, width), f32), _sds((8, width), f32),
                   _sds((nh, CHUNK, CHUNK), f32), _sds((CHUNK, nh), f32)],
        input_output_aliases=aliases,
        scratch_shapes=[pltpu.VMEM((CHUNK, width), f32), pltpu.VMEM((CHUNK, width), f32)],
        compiler_params=_cparams(("arbitrary",)),
    )(proj, ln_g.reshape(1, width), ln_b.reshape(1, width), w_s, b_s, dy, *shared_in)
    return res[0], res[1], res[2], res[3], res[4].T


def _sel_col(x, h):
    lane = lax.broadcasted_iota(jnp.int32, x.shape, 1)
    return jnp.sum(jnp.where(lane == h, x, 0.0), axis=1, keepdims=True)


def _sel_row(x, h):
    sub = lax.broadcasted_iota(jnp.int32, x.shape, 0)
    return jnp.sum(jnp.where(sub == h, x, 0.0), axis=0, keepdims=True)


def _ssd_chunk(nh, ngrp, xs_l, z_l, b_l, c_l, dtraw, dtb, alog, dskip, ng_l, prev_l):
    hg = nh // ngrp
    tril = _tril_mask(CHUNK)
    tl = tril.astype(f32)
    lo = lax.broadcasted_iota(jnp.int32, (CHUNK, PAIR), 1) < HEAD
    lo_row = lo[0:1, :]
    dt = jax.nn.softplus(dtraw + dtb)
    a = dt * (-jnp.exp(alog))
    cs = jnp.dot(tl, a, precision=HI, preferred_element_type=f32)
    cst = lax.dot_general(a, tl, (((0,), (1,)), ((), ())), precision=HI, preferred_element_type=f32)
    cb_l = [lax.dot_general(c_l[g].astype(bf16), b_l[g].astype(bf16), (((1,), (1,)), ((), ())),
                            preferred_element_type=f32) for g in range(ngrp)]
    yz_l, new_prev = [], []
    for q in range(nh // 2):
        g = (2 * q) // hg
        cols = []
        for h in (2 * q, 2 * q + 1):
            cs_h = _sel_col(cs, h)
            tot = _sel_row(cs_h, CHUNK - 1)
            seg = jnp.where(tril, cs_h - _sel_row(cst, h), 0.0)
            lmat = jnp.where(tril, jnp.exp(seg), 0.0)
            cols.append((_sel_col(dt, h), cs_h, tot, lmat, _sel_col(dskip, h)))
        (dt_a, cs_a, tot_a, l_a, dsk_a), (dt_b, cs_b, tot_b, l_b, dsk_b) = cols
        xs = xs_l[q]
        x = xs * jnp.where(lo, dt_a, dt_b)
        xb = x.astype(bf16)
        ydiag = jnp.where(lo, jnp.dot((cb_l[g] * l_a).astype(bf16), xb, preferred_element_type=f32),
                          jnp.dot((cb_l[g] * l_b).astype(bf16), xb, preferred_element_type=f32))
        yoff = (jnp.dot(c_l[g].astype(bf16), prev_l[q].astype(bf16), preferred_element_type=f32)
                * jnp.where(lo, jnp.exp(cs_a), jnp.exp(cs_b)))
        xdec = x * jnp.where(lo, jnp.exp(tot_a - cs_a), jnp.exp(tot_b - cs_b))
        st = lax.dot_general(b_l[g].astype(bf16), xdec.astype(bf16), (((0,), (0,)), ((), ())),
                             preferred_element_type=f32)
        new_prev.append(prev_l[q] * jnp.where(lo_row, jnp.exp(tot_a), jnp.exp(tot_b)) + st)
        y = ydiag + yoff + jnp.where(lo_row, dsk_a, dsk_b) * xs
        yz_l.append(y * jax.nn.silu(z_l[q]))
    out = [None] * (nh // 2)
    qg = hg // 2
    for g in range(ngrp):
        ssq = sum(jnp.sum(yz_l[q] * yz_l[q], axis=-1, keepdims=True) for q in range(g * qg, (g + 1) * qg))
        r = lax.rsqrt(ssq * (1.0 / (hg * HEAD)) + EPS)
        for q in range(g * qg, (g + 1) * qg):
            out[q] = yz_l[q] * r * ng_l[q]
    return out, new_prev


def _ssd_read(nh, ngrp, nst, xbc_ref, z_ref, ng_ref, st_ref):
    cw = nh * HEAD
    xs_l = [xbc_ref[:, pl.ds(q * PAIR, PAIR)] for q in range(nh // 2)]
    b_l = [xbc_ref[:, pl.ds(cw + g * nst, nst)] for g in range(ngrp)]
    c_l = [xbc_ref[:, pl.ds(cw + ngrp * nst + g * nst, nst)] for g in range(ngrp)]
    z_l = [z_ref[:, pl.ds(q * PAIR, PAIR)].astype(f32) for q in range(nh // 2)]
    ng_l = [ng_ref[:, pl.ds(q * PAIR, PAIR)] for q in range(nh // 2)]
    prev_l = [st_ref[:, pl.ds(q * PAIR, PAIR)] for q in range(nh // 2)]
    return xs_l, z_l, b_l, c_l, ng_l, prev_l


def _ssd_fwd(xbc, proj, z_col_block, pdt, dtb, alog, dskip, ng, nh, ngrp, nst, n_seq, share=None):
    T = xbc.shape[0]
    cw = nh * HEAD
    nchunk = T // n_seq // CHUNK
    assert nst == CHUNK
    y_shape, y_blk, shared_in, aliases = _shared(share, T, cw, bf16, 7)

    def body(xbc_ref, z_ref, dt_ref, dtb_ref, alog_ref, dskip_ref, ng_ref, *rest):
        y_ref, sin_ref, st_ref = rest[len(shared_in):]

        @pl.when(pl.program_id(1) == 0)
        def _():
            st_ref[...] = jnp.zeros_like(st_ref)

        sin_ref[...] = st_ref[...]
        xs_l, z_l, b_l, c_l, ng_l, prev_l = _ssd_read(nh, ngrp, nst, xbc_ref, z_ref, ng_ref, st_ref)
        y_l, new_prev = _ssd_chunk(nh, ngrp, xs_l, z_l, b_l, c_l, dt_ref[...], dtb_ref[...], alog_ref[...],
                                   dskip_ref[...], ng_l, prev_l)
        for q in range(nh // 2):
            ls = pl.ds(q * PAIR, PAIR)
            y_ref[:, ls] = y_l[q].astype(y_ref.dtype)
            st_ref[:, ls] = new_prev[q]

    def blk(s, c):
        return s * nchunk + c

    prow = pl.BlockSpec((1, 128), lambda s, c: (0, 0))
    return pl.pallas_call(
        body, name="ssd_fwd", grid=(n_seq, nchunk),
        in_specs=[pl.BlockSpec((CHUNK, xbc.shape[1]), lambda s, c: (blk(s, c), 0)),
                  pl.BlockSpec((CHUNK, cw), lambda s, c: (blk(s, c), z_col_block)),
                  pl.BlockSpec((CHUNK, 128), lambda s, c: (blk(s, c), 0)),
                  prow, prow, prow, pl.BlockSpec((1, cw), lambda s, c: (0, 0))]
        + [pl.BlockSpec(memory_space=pl.ANY)] * len(shared_in),
        out_specs=[pl.BlockSpec((CHUNK, cw), lambda s, c: (blk(s, c), y_blk)),
                   pl.BlockSpec((nst, cw), lambda s, c: (blk(s, c), 0))],
        out_shape=[y_shape, _sds((T, cw), f32)], input_output_aliases=aliases,
        scratch_shapes=[pltpu.VMEM((nst, cw), f32)],
        compiler_params=_cparams(("arbitrary", "arbitrary")),
    )(xbc, proj, pdt, dtb, alog, dskip, ng.reshape(1, cw), *shared_in)


def _ssd_bwd(xbc, proj, z_col_block, pdt, dtb, alog, dskip, ng, sin, dy, dy_col_block, nh, ngrp, nst, n_seq, share=None):
    T, xw = xbc.shape
    cw = nh * HEAD
    nchunk = T // n_seq // CHUNK
    dz_shape, dz_blk, shared_in, aliases = _shared(share, T, cw, bf16, 9, out_index=1)

    def body(xbc_ref, z_ref, dt_ref, dtb_ref, alog_ref, dskip_ref, ng_ref, sin_ref, dy_ref, *rest):
        dxbc_ref, dz_ref, ddt_ref, ddtb_ref, dalog_ref, ddskip_ref, dng_ref, dst_ref = rest[len(shared_in):]
        s, cc = pl.program_id(0), pl.program_id(1)

        @pl.when((s == 0) & (cc == 0))
        def _():
            ddtb_ref[...] = jnp.zeros_like(ddtb_ref)
            dalog_ref[...] = jnp.zeros_like(dalog_ref)
            ddskip_ref[...] = jnp.zeros_like(ddskip_ref)
            dng_ref[...] = jnp.zeros_like(dng_ref)

        @pl.when(cc == 0)
        def _():
            dst_ref[...] = jnp.zeros_like(dst_ref)

        xs_l, z_l, b_l, c_l, ng_l, prev_l = _ssd_read(nh, ngrp, nst, xbc_ref, z_ref, ng_ref, sin_ref)
        _, vjp = jax.vjp(functools.partial(_ssd_chunk, nh, ngrp), xs_l, z_l, b_l, c_l, dt_ref[...], dtb_ref[...],
                         alog_ref[...], dskip_ref[...], ng_l, prev_l)
        dy_l = [dy_ref[:, pl.ds(q * PAIR, PAIR)].astype(f32) for q in range(nh // 2)]
        dst_l = [dst_ref[:, pl.ds(q * PAIR, PAIR)] for q in range(nh // 2)]
        dxs_l, dz_l, db_l, dc_l, ddt, ddtb, dalog, ddskip, dng_l, dprev_l = vjp((dy_l, dst_l))
        for q in range(nh // 2):
            ls = pl.ds(q * PAIR, PAIR)
            dxbc_ref[:, ls] = dxs_l[q]
            dz_ref[:, ls] = dz_l[q].astype(dz_ref.dtype)
            dng_ref[0:1, ls] += dng_l[q]
            dst_ref[:, ls] = dprev_l[q]
        for g in range(ngrp):
            dxbc_ref[:, pl.ds(cw + g * nst, nst)] = db_l[g]
            dxbc_ref[:, pl.ds(cw + ngrp * nst + g * nst, nst)] = dc_l[g]
        ddt_ref[...] = ddt.astype(ddt_ref.dtype)
        ddtb_ref[0:1, :] += ddtb
        dalog_ref[0:1, :] += dalog
        ddskip_ref[0:1, :] += ddskip

    def blk(s, cc):
        return s * nchunk + (nchunk - 1 - cc)

    prow = pl.BlockSpec((1, 128), lambda s, c: (0, 0))
    pacc = pl.BlockSpec((8, 128), lambda s, c: (0, 0))
    return pl.pallas_call(
        body, name="ssd_bwd", grid=(n_seq, nchunk),
        in_specs=[pl.BlockSpec((CHUNK, xw), lambda s, c: (blk(s, c), 0)),
                  pl.BlockSpec((CHUNK, cw), lambda s, c: (blk(s, c), z_col_block)),
                  pl.BlockSpec((CHUNK, 128), lambda s, c: (blk(s, c), 0)),
                  prow, prow, prow, pl.BlockSpec((1, cw), lambda s, c: (0, 0)),
                  pl.BlockSpec((nst, cw), lambda s, c: (blk(s, c), 0)),
                  pl.BlockSpec((CHUNK, cw), lambda s, c: (blk(s, c), dy_col_block))]
        + [pl.BlockSpec(memory_space=pl.ANY)] * len(shared_in),
        out_specs=[pl.BlockSpec((CHUNK, xw), lambda s, c: (blk(s, c), 0)),
                   pl.BlockSpec((CHUNK, cw), lambda s, c: (blk(s, c), dz_blk)),
                   pl.BlockSpec((CHUNK, 128), lambda s, c: (blk(s, c), 0)),
                   pacc, pacc, pacc, pl.BlockSpec((8, cw), lambda s, c: (0, 0))],
        out_shape=[_sds((T, xw), f32), dz_shape, _sds((T, 128), bf16),
                   _sds((8, 128), f32), _sds((8, 128), f32), _sds((8, 128), f32), _sds((8, cw), f32)],
        input_output_aliases=aliases,
        scratch_shapes=[pltpu.VMEM((nst, cw), f32)],
        compiler_params=_cparams(("arbitrary", "arbitrary")),
    )(xbc, proj, pdt, dtb, alog, dskip, ng.reshape(1, cw), sin, dy, *shared_in)


_HBM = pl.BlockSpec(memory_space=pltpu.HBM)
_SEM = pl.BlockSpec(memory_space=pltpu.SEMAPHORE)
_EFFECT = pltpu.SideEffectType.DATAFLOW_SIDE_EFFECTING


def _split_copies(n, scatter, src_refs, land_refs, send_sems, recv_sems):
    npeer = N_DEV - 1
    x, y, c = lax.axis_index("x"), lax.axis_index("y"), lax.axis_index("c")
    me = 4 * x + 2 * y + c
    copies = []
    for i in range(n):
        for k in range(1, N_DEV):
            px = 1 - x if k & 4 else x
            py = 1 - y if k & 2 else y
            pc = 1 - c if k & 1 else c
            src = src_refs[i].at[4 * px + 2 * py + pc] if scatter else src_refs[i]
            copies.append(pltpu.make_async_remote_copy(
                src_ref=src, dst_ref=land_refs[i].at[me],
                send_sem=send_sems.at[i * npeer + k - 1], recv_sem=recv_sems.at[i * npeer + k - 1],
                device_id=(px, py, pc), device_id_type=pl.DeviceIdType.MESH))
    return copies


def _exchange_start(name, arrs, scatter):
    n = len(arrs)
    nsem = n * (N_DEV - 1)
    me = 4 * lax.axis_index("x") + 2 * lax.axis_index("y") + lax.axis_index("c")
    lands = []
    for a in arrs:
        own = lax.dynamic_index_in_dim(a, me, 0, keepdims=True) if scatter else a[None]
        full = lax.empty(a.shape if scatter else (N_DEV,) + a.shape, a.dtype)
        lands.append(lax.dynamic_update_slice(full, own, (me,) + (0,) * (full.ndim - 1)))

    def body(*refs):
        src_refs, land_refs = refs[:n], refs[n:2 * n]
        send_sems, recv_sems = refs[2 * n], refs[2 * n + 1]
        token = refs[-1]
        for cp in _split_copies(n, scatter, src_refs, land_refs, send_sems, recv_sems):
            cp.start()
        token[...] = jnp.zeros_like(token)

    res = pl.pallas_call(
        body, name=name,
        out_shape=(pltpu.SemaphoreType.DMA((nsem,)), pltpu.SemaphoreType.DMA((nsem,)),
                   *[pltpu.HBM(a.shape, a.dtype) for a in arrs], *[pltpu.HBM(l.shape, l.dtype) for l in lands],
                   _sds((8, 128), f32)),
        in_specs=[_HBM] * (2 * n),
        out_specs=(_SEM, _SEM, *[_HBM] * (2 * n), pl.BlockSpec(memory_space=pltpu.VMEM)),
        input_output_aliases={j: 2 + j for j in range(2 * n)},
        compiler_params=pltpu.CompilerParams(has_side_effects=_EFFECT),
    )(*[pltpu.with_memory_space_constraint(a, pltpu.HBM) for a in arrs],
      *[pltpu.with_memory_space_constraint(l, pltpu.HBM) for l in lands])
    return (n, scatter, res[0], res[1], res[2:2 + n], res[2 + n:2 + 2 * n]), res[-1]


def _exchange_wait(name, handle, after):
    n, scatter, send_sems, recv_sems, srcs, lands = handle
    after = list(after) if isinstance(after, (list, tuple)) else [after]

    def body(*refs):
        src_refs, land_refs = refs[:n], refs[n:2 * n]
        for cp in _split_copies(n, scatter, src_refs, land_refs, refs[2 * n], refs[2 * n + 1]):
            cp.wait_send()
            cp.wait_recv()

    res = pl.pallas_call(
        body, name=name,
        out_shape=[pltpu.HBM(a.shape, a.dtype) for a in (*srcs, *lands)],
        in_specs=[_HBM] * (2 * n) + [_SEM, _SEM] + [pl.BlockSpec(memory_space=pl.ANY)] * len(after),
        out_specs=[_HBM] * (2 * n),
        input_output_aliases={j: j for j in range(2 * n)},
        compiler_params=pltpu.CompilerParams(has_side_effects=_EFFECT),
    )(*srcs, *lands, send_sems, recv_sems, *after)
    return res[n:]


def _adam_tiles(R, C):
    if R % 256 == 0:
        return (256, C), (R // 256, 1)
    assert C % 128 == 0
    return (R, 128), (1, C // 128)


def _adam(name, parts, w, m, v):
    P, R, C = parts.shape
    (tr, tc), (gr, gc) = _adam_tiles(R, C)
    c1 = 1.0 / (1.0 - ADAM_B1 ** ADAM_STEP)
    c2 = 1.0 / (1.0 - ADAM_B2 ** ADAM_STEP)

    def body(p_ref, w_ref, m_ref, v_ref, g_ref, d_ref, nm_ref, nv_ref):
        g = p_ref[0].astype(f32)
        for s in range(1, P):
            g = g + p_ref[s].astype(f32)
        nm = ADAM_B1 * m_ref[...] + (1.0 - ADAM_B1) * g
        nv = ADAM_B2 * v_ref[...] + (1.0 - ADAM_B2) * (g * g)
        g_ref[...] = g
        nm_ref[...] = nm
        nv_ref[...] = nv
        d_ref[...] = -ADAM_LR * ((nm * c1) / (jnp.sqrt(nv * c2) + ADAM_EPS) + ADAM_WD * w_ref[...])

    tile = pl.BlockSpec((tr, tc), lambda i, j: (i, j))
    return pl.pallas_call(
        body, name=name, grid=(gr, gc),
        in_specs=[pl.BlockSpec((P, tr, tc), lambda i, j: (0, i, j)), tile, tile, tile],
        out_specs=[tile] * 4, out_shape=[_sds((R, C), f32)] * 4,
        compiler_params=_cparams(("arbitrary", "arbitrary")),
    )(parts, w, m, v)


def _sum_parts(name, parts):
    P, R, C = parts.shape
    tr = 256 if R % 256 == 0 else R

    def body(p_ref, o_ref):
        g = p_ref[0]
        for s in range(1, P):
            g = g + p_ref[s]
        o_ref[...] = g

    return pl.pallas_call(
        body, name=name, grid=(R // tr,),
        in_specs=[pl.BlockSpec((P, tr, C), lambda i: (0, i, 0))], out_specs=pl.BlockSpec((tr, C), lambda i: (i, 0)),
        out_shape=_sds((R, C), f32), compiler_params=_cparams(("arbitrary",)),
    )(parts)


def _pad_to(a, n, axis):
    if a.shape[axis] == n:
        return a
    cfg = [(0, 0)] * a.ndim
    cfg[axis] = (0, n - a.shape[axis])
    return jnp.pad(a, cfg)


def _pack(arrs):
    flat = [_pad_to(a.reshape(-1), -(-a.size // 128) * 128, 0) for a in arrs]
    rows = jnp.concatenate(flat).reshape(-1, 128)
    return _pad_to(rows, -(-rows.shape[0] // 256) * 256, 0)


def _unpack(slab, shapes):
    flat = slab.reshape(-1)
    out, o = [], 0
    for s in shapes:
        n = math.prod(s)
        out.append(flat[o:o + n].reshape(s))
        o += -(-n // 128) * 128
    return out


_NAMES = ['norm1_g', 'w_in', 'conv_a_w', 'conv_a_b', 'ln_a_g', 'ln_a_b', 'ln_b_g', 'ln_b_b', 'w_spatial', 'b_spatial',
          'conv_c_w', 'conv_c_b', 'dt_bias', 'a_log', 'd_skip', 'norm_c_g', 'w_out', 'norm2_g', 'w_ff1', 'w_ff2', 'final_g']
_REPL = ['norm1_g', 'conv_a_b', 'ln_a_g', 'ln_a_b', 'ln_b_g', 'ln_b_b', 'w_spatial', 'b_spatial', 'conv_c_b',
         'dt_bias', 'a_log', 'd_skip', 'norm_c_g', 'norm2_g']
_CONVW = ['conv_a_w', 'conv_c_w']
_BIG = ['w_in', 'w_out', 'w_ff1', 'w_ff2']
_BIG_T = {'w_in': True, 'w_out': False, 'w_ff1': True, 'w_ff2': False}


def _row128(v):
    return _pad_to(v.reshape(1, -1), 128, 1)


def _step(p, m, v, x, loss_target):
    nb, S, D = x.shape
    T = nb * S
    depth = p['norm1_g'].shape[0]
    a_w = p['conv_a_b'].shape[1]
    b_w = p['ln_b_g'].shape[1]
    nh = p['dt_bias'].shape[1]
    c_w = p['norm_c_g'].shape[1]
    xw = p['conv_c_b'].shape[1]
    ngrp = 2
    nst = (xw - c_w) // (2 * ngrp)
    d_in = p['w_in'].shape[2] * N_DEV
    main = d_in - nh
    assert main == 2 * a_w + 2 * b_w + c_w + xw and 2 * a_w == 2 * b_w == c_w and xw % c_w == c_w // 2
    me = 4 * lax.axis_index("x") + 2 * lax.axis_index("y") + lax.axis_index("c")

    x2 = x.reshape(T, D)
    tgt = loss_target.reshape(T, D)

    def shards(i, z=None):
        z = 0.0 if z is None else z
        return [(p['w_in'][i].T + z).astype(bf16), (p['w_out'][i] + z).astype(bf16), (p['w_ff1'][i].T + z).astype(bf16),
                (p['w_ff2'][i] + z).astype(bf16), p['conv_a_w'][i], p['conv_c_w'][i]]

    def gathered_in(wt, ca, cc):
        wt = wt.reshape(d_in, D)
        ca = jnp.transpose(ca, (1, 0, 2)).reshape(KA, a_w)
        cc = jnp.transpose(cc, (1, 0, 2)).reshape(KC, xw)
        return dict(wt_main=wt[:main], wt_dt=_pad_to(wt[main:], 128, 0), ca=_pad_to(ca, 32, 0), cc=_pad_to(cc, 8, 0))

    def start_layer(i, after=None):
        sh = shards(i, None if after is None else after[0, 0])
        ha, t = _exchange_start("gather_w%da_start" % i, [sh[0], sh[4], sh[5]], False)
        hb, t = _exchange_start("gather_w%db_start" % i, [shards(i, t[0, 0])[1]], False)
        hc, t = _exchange_start("gather_w%dc_start" % i, [shards(i, t[0, 0])[2]], False)
        hd, t = _exchange_start("gather_w%dd_start" % i, [shards(i, t[0, 0])[3]], False)
        return dict(a=ha, b=hb, c=hc, d=hd), t

    W, saved = [], []
    xc = x2
    H, tok = start_layer(0)
    for i in range(depth):
        w = gathered_in(*_exchange_wait("gather_w%da_wait" % i, H['a'], [xc, tok]))
        W.append(w)
        Hi = H
        h1, rtok = _rms_fwd(xc, p['norm1_g'][i])
        if i + 1 < depth:
            H, tok = start_layer(i + 1, rtok + tok)
        else:
            tok = None
        (proj,) = _mm("mm_proj", h1, w['wt_main'], "nt", [bf16], dep=tok)
        (pdt,) = _mm("mm_pdt", h1, w['wt_dt'], "nt", [f32])
        mixw = a_w + b_w + c_w
        ycat = _conv_fwd("confa_fwd", proj, 0, w['ca'], p['conv_a_b'][i], KA, True, nb, p['ln_a_g'][i], p['ln_a_b'][i],
                         share=(mixw, 0, None))
        ycat = _gmlp_fwd(proj, 1, p['ln_b_g'][i], p['ln_b_b'][i], p['w_spatial'][i], p['b_spatial'][i], share=(mixw, 1, ycat))
        xbc = _conv_fwd("convc_fwd", proj, 2, w['cc'], p['conv_c_b'][i], KC, False, nb)
        dtb, alog, dsk = _row128(p['dt_bias'][i]), _row128(p['a_log'][i]), _row128(p['d_skip'][i])
        ycat, sin = _ssd_fwd(xbc, proj, 2, pdt, dtb, alog, dsk, p['norm_c_g'][i], nh, ngrp, nst, nb, share=(mixw, 1, ycat))
        w['wout'] = _exchange_wait("gather_w%db_wait" % i, Hi['b'], ycat)[0].reshape(-1, D)
        (xm,) = _mm("mm_out", ycat, w['wout'], "nn", [f32], _ep_add, (xc,))
        h2, _ = _rms_fwd(xm, p['norm2_g'][i])
        w['w1t'] = _exchange_wait("gather_w%dc_wait" % i, Hi['c'], h2)[0].reshape(-1, D)
        f, a = _mm("mm_ff1", h2, w['w1t'], "nt", [bf16, bf16], _ep_relu2)
        w['w2'] = _exchange_wait("gather_w%dd_wait" % i, Hi['d'], a)[0].reshape(-1, D)
        (xo,) = _mm("mm_ff2", a, w['w2'], "nn", [f32], _ep_add, (xm,))
        saved.append(dict(x_in=xc, h1=h1, proj=proj, pdt=pdt, xbc=xbc, sin=sin, ycat=ycat, xm=xm, h2=h2, f=f, a=a,
                          dtb=dtb, alog=alog, dsk=dsk))
        xc = xo

    lp, dx, dfinal = _loss_head(xc, p['final_g'], tgt)
    loss = lax.psum(lp[0, 0], ("x", "y", "c"))

    out = {}
    kinds = ("grad", "delta", "new_m", "new_v")
    names1 = _REPL + _CONVW

    started, small = [], [None] * depth

    def send(n, i, g):
        handle, token = _exchange_start("scatter_%s_%d_start" % (n, i), [g.reshape(N_DEV, -1, D)], True)
        started.append((n, i, handle))
        return token

    tok = None
    for i in reversed(range(depth)):
        w, sv = W[i], saved[i]
        (df,) = _mm("mm_df", dx, w['w2'], "nt", [bf16], _ep_drelu2, (sv['f'],), dep=tok)
        (gw2,) = _mm("mm_gw2", sv['a'], dx, "tn", [bf16])
        tok = send('w_ff2', i, gw2)
        (dh2,) = _mm("mm_dh2", df, w['w1t'], "nn", [f32], dep=tok)
        (gw1t,) = _mm("mm_gw1", df, sv['h2'], "tn", [bf16])
        tok = send('w_ff1', i, gw1t)
        dxm, dg2 = _rms_bwd(sv['xm'], p['norm2_g'][i], dh2, dx)
        (dycat,) = _mm("mm_dycat", dxm, w['wout'], "nt", [bf16], dep=tok)
        (gwout,) = _mm("mm_gwout", sv['ycat'], dxm, "tn", [bf16])
        tok = send('w_out', i, gwout)
        dproj, dwa, dba, dlag, dlab = _conv_bwd("confa_bwd", sv['proj'], 0, w['ca'], p['conv_a_b'][i] + tok[0, 0], dycat, 0, KA,
                                                True, nb, p['ln_a_g'][i], p['ln_a_b'][i], share=(main, 0, None))
        dproj, dlbg, dlbb, dws, dbs = _gmlp_bwd(sv['proj'], 1, p['ln_b_g'][i], p['ln_b_b'][i], p['w_spatial'][i],
                                                p['b_spatial'][i], dycat, 1, share=(main, 1, dproj))
        dxbc, dproj, ddt, ddtb, dalog, ddsk, dng = _ssd_bwd(sv['xbc'], sv['proj'], 2, sv['pdt'], sv['dtb'], sv['alog'],
                                                            sv['dsk'], p['norm_c_g'][i], sv['sin'], dycat, 1, nh, ngrp, nst, nb,
                                                            share=(main, 2, dproj))
        dproj, dwc, dbc = _conv_bwd("convc_bwd", sv['proj'], 2, w['cc'], p['conv_c_b'][i], dxbc, 0, KC, False, nb,
                                    share=(main, 2, dproj))
        (dh_main,) = _mm("mm_dh1", dproj, w['wt_main'], "nn", [f32])
        (dh,) = _mm("mm_dh1dt", ddt, w['wt_dt'], "nn", [f32], _ep_add, (dh_main,))
        (gwt_main,) = _mm("mm_gwin", dproj, sv['h1'], "tn", [bf16])
        (gwt_dt,) = _mm("mm_gwdt", ddt, sv['h1'], "tn", [bf16])
        tok = send('w_in', i, jnp.concatenate([gwt_main, gwt_dt[:nh]], axis=0))
        dx, dg1 = _rms_bwd(sv['x_in'], p['norm1_g'][i] + tok[0, 0], dh, dxm)

        gi = dict(norm1_g=dg1[0], norm2_g=dg2[0], conv_a_w=dwa[:KA], conv_a_b=dba[0], ln_a_g=dlag[0], ln_a_b=dlab[0],
                  ln_b_g=dlbg[0], ln_b_b=dlbb[0], w_spatial=dws, b_spatial=dbs, conv_c_w=dwc[:KC], conv_c_b=dbc[0],
                  dt_bias=ddtb[0, :nh], a_log=dalog[0, :nh], d_skip=ddsk[0, :nh], norm_c_g=dng[0])
        parts_i = [gi[n] for n in names1] + ([dfinal[0]] if i == depth - 1 else [])
        handle, tok = _exchange_start("gather_g%d_start" % i, [_pack(parts_i)], False)
        small[i] = ([a.shape for a in parts_i], handle)
    grad_x = dx.reshape(nb, S, D)

    dep = [dx, tok]
    for n, i, handle in started:
        (parts,) = _exchange_wait("scatter_%s_%d_wait" % (n, i), handle, dep)
        tr = (lambda t: t.T) if _BIG_T[n] else (lambda t: t)
        res = _adam("adam_" + n, parts, tr(p[n][i]), tr(m[n][i]), tr(v[n][i]))
        for kind, r in zip(kinds, res):
            out.setdefault((kind, n), [None] * depth)[i] = r
        dep = res[3]

    gsum = [None] * depth
    for i in reversed(range(depth)):
        (parts,) = _exchange_wait("gather_g%d_wait" % i, small[i][1], dep)
        gsum[i] = _sum_parts("sum_small", parts)
        dep = gsum[i]
    widths = [-(-math.prod(p[n].shape[1:]) // 128) * 128 for n in _REPL]
    rep_rows = sum(widths) // 128
    tot_rows = -(-depth * rep_rows // 256) * 256

    def rep_slab(q):
        cols = [_pad_to(q[n].reshape(depth, -1), wd, 1) for n, wd in zip(_REPL, widths)]
        return _pad_to(jnp.concatenate(cols, axis=1).reshape(-1, 128), tot_rows, 0)

    g_rep = _pad_to(jnp.concatenate([g[:rep_rows] for g in gsum], axis=0), tot_rows, 0)
    res = _adam("adam_small", g_rep[None], rep_slab(p), rep_slab(m), rep_slab(v))
    for kind, r in zip(kinds, res):
        view = r[:depth * rep_rows].reshape(depth, -1)
        o = 0
        for n, wd in zip(_REPL, widths):
            out[(kind, n)] = view[:, o:o + math.prod(p[n].shape[1:])].reshape(p[n].shape)
            o += wd

    extra = []
    for i in range(depth):
        tail = _unpack(gsum[i][rep_rows:], small[i][0][len(_REPL):])
        extra.append(tail)
    gconv = []
    for j, n in enumerate(_CONVW):
        cw_shard = p[n].shape[2]
        full = jnp.stack([extra[i][j] for i in range(depth)])
        gconv.append(lax.dynamic_slice_in_dim(full, me * cw_shard, cw_shard, axis=2))
    tail_names = _CONVW + ['final_g']
    res = _adam("adam_conv", _pack(gconv + [extra[depth - 1][len(_CONVW)]])[None],
                *[_pack([q[n] for n in tail_names]) for q in (p, m, v)])
    for kind, r in zip(kinds, res):
        for n, arr in zip(tail_names, _unpack(r, [p[n].shape for n in tail_names])):
            out[(kind, n)] = arr
    for n in _BIG:
        for kind in kinds:
            stacked = jnp.stack(out[(kind, n)])
            out[(kind, n)] = jnp.swapaxes(stacked, 1, 2) if _BIG_T[n] else stacked

    flat = [loss, grad_x]
    for kind in ("grad", "delta", "new_m", "new_v"):
        flat += [out[(kind, n)] for n in _NAMES]
    return tuple(flat)


def kernel(x, norm1_g, w_in, conv_a_w, conv_a_b, ln_a_g, ln_a_b, ln_b_g, ln_b_b, w_spatial, b_spatial, conv_c_w, conv_c_b, dt_bias, a_log, d_skip, norm_c_g, w_out, norm2_g, w_ff1, w_ff2, final_g, loss_target, m_norm1_g, m_w_in, m_conv_a_w, m_conv_a_b, m_ln_a_g, m_ln_a_b, m_ln_b_g, m_ln_b_b, m_w_spatial, m_b_spatial, m_conv_c_w, m_conv_c_b, m_dt_bias, m_a_log, m_d_skip, m_norm_c_g, m_w_out, m_norm2_g, m_w_ff1, m_w_ff2, m_final_g, v_norm1_g, v_w_in, v_conv_a_w, v_conv_a_b, v_ln_a_g, v_ln_a_b, v_ln_b_g, v_ln_b_b, v_w_spatial, v_b_spatial, v_conv_c_w, v_conv_c_b, v_dt_bias, v_a_log, v_d_skip, v_norm_c_g, v_w_out, v_norm2_g, v_w_ff1, v_w_ff2, v_final_g):
    p = dict(zip(_NAMES, (norm1_g, w_in, conv_a_w, conv_a_b, ln_a_g, ln_a_b, ln_b_g, ln_b_b, w_spatial, b_spatial, conv_c_w,
                          conv_c_b, dt_bias, a_log, d_skip, norm_c_g, w_out, norm2_g, w_ff1, w_ff2, final_g)))
    m = dict(zip(_NAMES, (m_norm1_g, m_w_in, m_conv_a_w, m_conv_a_b, m_ln_a_g, m_ln_a_b, m_ln_b_g, m_ln_b_b, m_w_spatial,
                          m_b_spatial, m_conv_c_w, m_conv_c_b, m_dt_bias, m_a_log, m_d_skip, m_norm_c_g, m_w_out, m_norm2_g,
                          m_w_ff1, m_w_ff2, m_final_g)))
    v = dict(zip(_NAMES, (v_norm1_g, v_w_in, v_conv_a_w, v_conv_a_b, v_ln_a_g, v_ln_a_b, v_ln_b_g, v_ln_b_b, v_w_spatial,
                          v_b_spatial, v_conv_c_w, v_conv_c_b, v_dt_bias, v_a_log, v_d_skip, v_norm_c_g, v_w_out, v_norm2_g,
                          v_w_ff1, v_w_ff2, v_final_g)))
    return _step(p, m, v, x, loss_target)
```

```python
import functools
import math

import jax
import jax.numpy as jnp
from jax import lax
from jax.experimental import pallas as pl
from jax.experimental.pallas import tpu as pltpu

f32 = jnp.float32
bf16 = jnp.bfloat16
HI = lax.Precision.HIGHEST
EPS = 1e-5
HEAD = 64
CHUNK = 128
KA = 31
KC = 4
N_DEV = 8
VMEM_LIMIT = 56 * 1024 * 1024
MM_VMEM_BUDGET = 52 * 1024 * 1024

ADAM_LR = 0.001
ADAM_B1 = 0.9
ADAM_B2 = 0.999
ADAM_EPS = 1e-08
ADAM_WD = 0.01
ADAM_STEP = 10


def _cparams(sem=None):
    return pltpu.CompilerParams(dimension_semantics=sem, vmem_limit_bytes=VMEM_LIMIT)


def _sds(shape, dtype):
    return jax.ShapeDtypeStruct(shape, dtype)


_DIMS = {"nn": ((1,), (0,)), "nt": ((1,), (1,)), "tn": ((0,), (0,))}


def _dot16(a, b, form):
    return lax.dot_general(a.astype(bf16), b.astype(bf16), (_DIMS[form], ((), ())), preferred_element_type=f32)


@functools.partial(jax.custom_vjp, nondiff_argnums=(2,))
def _bdot(a, b, form):
    return _dot16(a, b, form)


def _bdot_fwd(a, b, form):
    return _dot16(a, b, form), (a, b)


def _bdot_bwd(form, res, ct):
    a, b = res
    if form == "nn":
        da, db = _dot16(ct, b, "nt"), _dot16(a, ct, "tn")
    elif form == "nt":
        da, db = _dot16(ct, b, "nn"), _dot16(ct, a, "tn")
    else:
        da, db = _dot16(b, ct, "nt"), _dot16(a, ct, "nn")
    return da.astype(a.dtype), db.astype(b.dtype)


_bdot.defvjp(_bdot_fwd, _bdot_bwd)


def _tile(n, cap):
    if n <= cap:
        return n
    for d in range(cap - cap % 128, 0, -128):
        if n % d == 0:
            return d
    raise ValueError((n, cap))


def _mm(name, a, b, form, out_dtypes, epilogue=None, extras=(), tm=2048, tn=1024, tk=2048, dep=None):
    if form == "tn":
        K, M = a.shape
    else:
        M, K = a.shape
    N = b.shape[0] if form == "nt" else b.shape[1]
    tm, tn, tk = _tile(M, tm), _tile(N, tn), _tile(K, tk)
    nk = K // tk

    def vmem_bytes(tm):
        mn = sum(jnp.dtype(e.dtype).itemsize for e in extras) + sum(jnp.dtype(d).itemsize for d in out_dtypes)
        return 2 * (tm * tk * a.dtype.itemsize + tk * tn * b.dtype.itemsize + tm * tn * mn) + 2 * tm * tn * 4

    while vmem_bytes(tm) > MM_VMEM_BUDGET and tm % 256 == 0:
        tm //= 2
    ne, no = len(extras), len(out_dtypes)
    deps = () if dep is None else (dep,)
    if epilogue is None:
        epilogue = lambda acc: (acc,)

    def body(a_ref, b_ref, *rest):
        extra_refs = rest[:ne]
        rest = rest[ne + len(deps):]
        out_refs = rest[:no]
        part = lax.dot_general(a_ref[...].astype(bf16), b_ref[...].astype(bf16),
                               (_DIMS[form], ((), ())), preferred_element_type=f32)

        def finish(acc):
            outs = epilogue(acc, *[e[...] for e in extra_refs])
            for o_ref, v in zip(out_refs, outs):
                o_ref[...] = v.astype(o_ref.dtype)

        if nk == 1:
            finish(part)
            return
        acc_ref = rest[no]
        k = pl.program_id(2)

        @pl.when(k == 0)
        def _():
            acc_ref[...] = part

        @pl.when((k > 0) & (k < nk - 1))
        def _():
            acc_ref[...] += part

        @pl.when(k == nk - 1)
        def _():
            finish(acc_ref[...] + part)

    a_spec = pl.BlockSpec((tk, tm), lambda i, j, k: (k, i)) if form == "tn" else pl.BlockSpec((tm, tk), lambda i, j, k: (i, k))
    b_spec = pl.BlockSpec((tn, tk), lambda i, j, k: (j, k)) if form == "nt" else pl.BlockSpec((tk, tn), lambda i, j, k: (k, j))
    mn_spec = pl.BlockSpec((tm, tn), lambda i, j, k: (i, j))
    return pl.pallas_call(
        body, name=name, grid=(M // tm, N // tn, nk),
        in_specs=[a_spec, b_spec] + [mn_spec] * ne + [pl.BlockSpec((8, 128), lambda i, j, k: (0, 0))] * len(deps),
        out_specs=[mn_spec] * no,
        out_shape=[_sds((M, N), d) for d in out_dtypes],
        scratch_shapes=[pltpu.VMEM((tm, tn), f32)] if nk > 1 else [],
        compiler_params=_cparams(("parallel", "parallel", "arbitrary")),
    )(a, b, *extras, *deps)


def _ep_add(acc, r):
    return (acc + r,)


def _ep_relu2(acc):
    r = jnp.maximum(acc, 0.0)
    return acc, r * r


def _ep_drelu2(acc, f):
    return (acc * 2.0 * jnp.maximum(f, 0.0),)


def _rms(x, g):
    return x * lax.rsqrt(jnp.mean(x * x, axis=-1, keepdims=True) + EPS) * g


TT = 512


def _rms_fwd(x, g):
    T, D = x.shape

    def body(x_ref, g_ref, h_ref, tok_ref):
        h_ref[...] = _rms(x_ref[...], g_ref[...]).astype(bf16)
        tok_ref[...] = jnp.zeros_like(tok_ref)

    return pl.pallas_call(
        body, name="rms_fwd", grid=(T // TT,),
        in_specs=[pl.BlockSpec((TT, D), lambda i: (i, 0)), pl.BlockSpec((1, D), lambda i: (0, 0))],
        out_specs=[pl.BlockSpec((TT, D), lambda i: (i, 0)), pl.BlockSpec((8, 128), lambda i: (0, 0))],
        out_shape=[_sds((T, D), bf16), _sds((8, 128), f32)], compiler_params=_cparams(("arbitrary",)),
    )(x, g.reshape(1, D))


def _rms_bwd(x, g, dh, dres):
    T, D = x.shape

    def body(x_ref, g_ref, dh_ref, dres_ref, dx_ref, dg_ref):
        _, vjp = jax.vjp(_rms, x_ref[...], g_ref[...])
        dx, dg = vjp(dh_ref[...])
        dx_ref[...] = dres_ref[...] + dx

        @pl.when(pl.program_id(0) == 0)
        def _():
            dg_ref[...] = jnp.zeros_like(dg_ref)

        dg_ref[0:1, :] += dg

    tile = pl.BlockSpec((TT, D), lambda i: (i, 0))
    return pl.pallas_call(
        body, name="rms_bwd", grid=(T // TT,),
        in_specs=[tile, pl.BlockSpec((1, D), lambda i: (0, 0)), tile, tile],
        out_specs=[tile, pl.BlockSpec((8, D), lambda i: (0, 0))],
        out_shape=[_sds((T, D), f32), _sds((8, D), f32)], compiler_params=_cparams(("arbitrary",)),
    )(x, g.reshape(1, D), dh, dres)


def _loss_head(x, g, tgt):
    T, D = x.shape

    def f(xv, gv, tv):
        e = _rms(xv, gv) - tv
        return 0.5 * jnp.sum(jnp.sum(e * e, axis=-1, keepdims=True) * (1.0 / D), axis=0, keepdims=True)

    def body(x_ref, g_ref, t_ref, loss_ref, dx_ref, dg_ref):
        tv = t_ref[...]
        l, vjp = jax.vjp(lambda xv, gv: f(xv, gv, tv), x_ref[...], g_ref[...])
        dx, dg = vjp(jnp.ones((1, 1), f32))
        dx_ref[...] = dx

        @pl.when(pl.program_id(0) == 0)
        def _():
            dg_ref[...] = jnp.zeros_like(dg_ref)
            loss_ref[...] = jnp.zeros_like(loss_ref)

        dg_ref[0:1, :] += dg
        loss_ref[...] += jnp.broadcast_to(l, loss_ref.shape)

    tile = pl.BlockSpec((TT, D), lambda i: (i, 0))
    return pl.pallas_call(
        body, name="loss_head", grid=(T // TT,),
        in_specs=[tile, pl.BlockSpec((1, D), lambda i: (0, 0)), tile],
        out_specs=[pl.BlockSpec((8, 128), lambda i: (0, 0)), tile, pl.BlockSpec((8, D), lambda i: (0, 0))],
        out_shape=[_sds((8, 128), f32), _sds((T, D), f32), _sds((8, D), f32)],
        compiler_params=_cparams(("arbitrary",)),
    )(x, g.reshape(1, D), tgt)


TB = 256


def _glu(a_val, a_gate):
    return a_val * jax.nn.sigmoid(a_gate)


PAIR = 2 * HEAD


def _pair_mean(x, lo):
    s_lo = jnp.sum(jnp.where(lo, x, 0.0), axis=-1, keepdims=True)
    s_hi = jnp.sum(jnp.where(lo, 0.0, x), axis=-1, keepdims=True)
    return jnp.where(lo, s_lo, s_hi) * (1.0 / HEAD)


def _pair_ln(v, g, b):
    lo = lax.broadcasted_iota(jnp.int32, v.shape, 1) < HEAD
    vc = v - _pair_mean(v, lo)
    var = _pair_mean(vc * vc, lo)
    return vc * lax.rsqrt(var + EPS) * g + b


def _ln_silu(v, g, b):
    return jax.nn.silu(_pair_ln(v, g, b))


def _conv_geom(kw):
    halo = 32 if kw > 9 else 16
    return halo, halo - (kw - 1)


def _residues(shifts):
    return sorted({s % 8 for s in shifts} - {0})


def _shift_copies(src_ref, cp_ref, res, rows, ls):
    for j, r in enumerate(res):
        cp_ref[j, :, ls] = src_ref[pl.ds(r, rows), ls]


def _shifted(src_ref, cp_ref, res, shift, size, ls):
    r = shift % 8
    if r == 0:
        return src_ref[pl.ds(shift, size), ls]
    return cp_ref[res.index(r), pl.ds(shift - r, size), ls]


def _conv_taps(hp_ref, hs_ref, w_ref, b_ref, acc_ref, kw, off, halo, width):
    res = _residues(range(off, off + kw))
    for c in range(width // 128):
        ls = pl.ds(c * 128, 128)
        _shift_copies(hp_ref, hs_ref, res, halo + TB, ls)
        acc = jnp.broadcast_to(b_ref[:, ls], (TB, 128))
        for k in range(kw):
            acc = acc + w_ref[k:k + 1, ls] * _shifted(hp_ref, hs_ref, res, off + k, TB, ls)
        acc_ref[:, ls] = acc


def _conv_fwd(name, src, col_block, w, b, kw, conformer, n_seq, ln_g=None, ln_b=None, share=None):
    T = src.shape[0]
    cout = w.shape[1]
    cin = 2 * cout if conformer else cout
    halo, off = _conv_geom(kw)
    nblk = T // n_seq // TB
    hb = TB // halo
    out_shape, out_blk, shared_in, aliases = _shared(share, T, cout, bf16 if conformer else f32, 6 if conformer else 4)

    def body(cur_ref, halo_ref, w_ref, b_ref, *rest):
        if conformer:
            g_ref, lb_ref = rest[:2]
            rest = rest[2:]
        out_ref, hp_ref, acc_ref, hs_ref = rest[len(shared_in):]
        i = pl.program_id(1)
        first = (i == 0)

        @pl.when((pl.program_id(0) == 0) & first)
        def _():
            hp_ref[pl.ds(halo + TB, 8), :] = jnp.zeros((8, cout), f32)

        if conformer:
            hp_ref[pl.ds(halo, TB), :] = _glu(cur_ref[:, 0:cout].astype(f32), cur_ref[:, cout:cin].astype(f32))
            hh = _glu(halo_ref[:, 0:cout].astype(f32), halo_ref[:, cout:cin].astype(f32))
        else:
            hp_ref[pl.ds(halo, TB), :] = cur_ref[...].astype(f32)
            hh = halo_ref[...].astype(f32)
        hp_ref[pl.ds(0, halo), :] = jnp.where(first, 0.0, hh)
        _conv_taps(hp_ref, hs_ref, w_ref, b_ref, acc_ref, kw, off, halo, cout)
        if conformer:
            for q in range(cout // PAIR):
                ls = pl.ds(q * PAIR, PAIR)
                out_ref[:, ls] = _ln_silu(acc_ref[:, ls], g_ref[:, ls], lb_ref[:, ls]).astype(out_ref.dtype)
        else:
            out_ref[...] = jax.nn.silu(acc_ref[...]).astype(out_ref.dtype)

    nres = len(_residues(range(off, off + kw)))

    row = pl.BlockSpec((1, cout), lambda s, i: (0, 0))
    in_specs = [pl.BlockSpec((TB, cin), lambda s, i: (s * nblk + i, col_block)),
                pl.BlockSpec((halo, cin), lambda s, i: (jnp.maximum((s * nblk + i) * hb - 1, 0), col_block)),
                pl.BlockSpec((w.shape[0], cout), lambda s, i: (0, 0)), row]
    args = [src, src, w, b.reshape(1, cout)]
    if conformer:
        in_specs += [row, row]
        args += [ln_g.reshape(1, cout), ln_b.reshape(1, cout)]
    in_specs += [pl.BlockSpec(memory_space=pl.ANY)] * len(shared_in)
    args += shared_in
    return pl.pallas_call(
        body, name=name, grid=(n_seq, nblk), in_specs=in_specs,
        out_specs=pl.BlockSpec((TB, cout), lambda s, i: (s * nblk + i, out_blk)),
        out_shape=out_shape, input_output_aliases=aliases,
        scratch_shapes=[pltpu.VMEM((halo + TB + 8, cout), f32), pltpu.VMEM((TB, cout), f32),
                        pltpu.VMEM((nres, halo + TB, cout), f32)],
        compiler_params=_cparams(("arbitrary", "arbitrary")),
    )(*args)


def _shared(share, T, width, dtype, n_inputs, out_index=0):
    if share is None:
        return _sds((T, width), dtype), 0, [], {}
    total, blk, into = share
    if into is None:
        return _sds((T, total), dtype), blk, [], {}
    return _sds((T, total), dtype), blk, [into], {n_inputs: out_index}


def _conv_bwd(name, src, col_block, w, b, dy, dy_col_block, kw, conformer, n_seq, ln_g=None, ln_b=None, share=None):
    T = src.shape[0]
    cout = w.shape[1]
    wrows = w.shape[0]
    cin = 2 * cout if conformer else cout
    halo, off = _conv_geom(kw)
    nblk = T // n_seq // TB
    hb = TB // halo
    dsrc_shape, dsrc_blk, shared_in, aliases = _shared(share, T, cin, bf16, 7 if conformer else 5)

    def body(cur_ref, halo_ref, w_ref, b_ref, dy_ref, *rest):
        if conformer:
            g_ref, lb_ref = rest[:2]
            rest = rest[2:]
        rest = rest[len(shared_in):]
        if conformer:
            dsrc_ref, dw_ref, db_ref, dg_ref, dlb_ref, hp_ref, acc_ref, dz_ref, dhp_ref, carry_ref, hs_ref, dzs_ref = rest
        else:
            dsrc_ref, dw_ref, db_ref, hp_ref, acc_ref, dz_ref, dhp_ref, carry_ref, hs_ref, dzs_ref = rest
        s, ii = pl.program_id(0), pl.program_id(1)
        i = nblk - 1 - ii
        first = (i == 0)

        @pl.when((s == 0) & (ii == 0))
        def _():
            dw_ref[...] = jnp.zeros_like(dw_ref)
            db_ref[...] = jnp.zeros_like(db_ref)
            hp_ref[pl.ds(halo + TB, 8), :] = jnp.zeros((8, cout), f32)
            if conformer:
                dg_ref[...] = jnp.zeros_like(dg_ref)
                dlb_ref[...] = jnp.zeros_like(dlb_ref)

        @pl.when(ii == 0)
        def _():
            carry_ref[...] = jnp.zeros_like(carry_ref)
            dz_ref[pl.ds(0, halo), :] = jnp.zeros((halo, cout), f32)
            dz_ref[pl.ds(halo + TB, halo), :] = jnp.zeros((halo, cout), f32)

        if conformer:
            hp_ref[pl.ds(halo, TB), :] = _glu(cur_ref[:, 0:cout].astype(f32), cur_ref[:, cout:cin].astype(f32))
            hh = _glu(halo_ref[:, 0:cout].astype(f32), halo_ref[:, cout:cin].astype(f32))
        else:
            hp_ref[pl.ds(halo, TB), :] = cur_ref[...].astype(f32)
            hh = halo_ref[...].astype(f32)
        hp_ref[pl.ds(0, halo), :] = jnp.where(first, 0.0, hh)
        _conv_taps(hp_ref, hs_ref, w_ref, b_ref, acc_ref, kw, off, halo, cout)

        if conformer:
            for q in range(cout // PAIR):
                ls = pl.ds(q * PAIR, PAIR)
                _, vjp = jax.vjp(_ln_silu, acc_ref[:, ls], g_ref[:, ls], lb_ref[:, ls])
                da, dg, dlb = vjp(dy_ref[:, ls].astype(f32))
                dz_ref[pl.ds(halo, TB), ls] = da
                dg_ref[0:1, ls] += dg
                dlb_ref[0:1, ls] += dlb
        else:
            _, vjp = jax.vjp(jax.nn.silu, acc_ref[...])
            dz_ref[pl.ds(halo, TB), :] = vjp(dy_ref[...].astype(f32))[0]

        res_h = _residues(range(off, off + kw))
        res_z = _residues(range(kw))
        for c in range(cout // 128):
            ls = pl.ds(c * 128, 128)
            _shift_copies(dz_ref, dzs_ref, res_z, halo + TB + halo - 8, ls)
            dacc = dz_ref[pl.ds(halo, TB), ls]
            db_ref[0:1, ls] += jnp.sum(dacc, axis=0, keepdims=True)
            dhp = jnp.zeros((halo + TB, 128), f32)
            for k in range(kw):
                dw_ref[k:k + 1, ls] += jnp.sum(dacc * _shifted(hp_ref, hs_ref, res_h, off + k, TB, ls), axis=0, keepdims=True)
                dhp = dhp + w_ref[k:k + 1, ls] * _shifted(dz_ref, dzs_ref, res_z, kw - 1 - k, halo + TB, ls)
            dhp_ref[:, ls] = dhp
        dhp_ref[pl.ds(TB, halo), :] += carry_ref[...]
        carry_ref[...] = dhp_ref[pl.ds(0, halo), :]
        dcur = dhp_ref[pl.ds(halo, TB), :]
        if conformer:
            _, vjp = jax.vjp(_glu, cur_ref[:, 0:cout].astype(f32), cur_ref[:, cout:cin].astype(f32))
            dval, dgate = vjp(dcur)
            dsrc_ref[:, 0:cout] = dval.astype(dsrc_ref.dtype)
            dsrc_ref[:, cout:cin] = dgate.astype(dsrc_ref.dtype)
        else:
            dsrc_ref[...] = dcur.astype(dsrc_ref.dtype)

    def blk(s, ii):
        return s * nblk + (nblk - 1 - ii)

    row = pl.BlockSpec((1, cout), lambda s, ii: (0, 0))
    acc8 = pl.BlockSpec((8, cout), lambda s, ii: (0, 0))
    in_specs = [pl.BlockSpec((TB, cin), lambda s, ii: (blk(s, ii), col_block)),
                pl.BlockSpec((halo, cin), lambda s, ii: (jnp.maximum(blk(s, ii) * hb - 1, 0), col_block)),
                pl.BlockSpec((wrows, cout), lambda s, ii: (0, 0)), row,
                pl.BlockSpec((TB, cout), lambda s, ii: (blk(s, ii), dy_col_block))]
    args = [src, src, w, b.reshape(1, cout), dy]
    out_specs = [pl.BlockSpec((TB, cin), lambda s, ii: (blk(s, ii), dsrc_blk)),
                 pl.BlockSpec((wrows, cout), lambda s, ii: (0, 0)), acc8]
    out_shape = [dsrc_shape, _sds((wrows, cout), f32), _sds((8, cout), f32)]
    if conformer:
        in_specs += [row, row]
        args += [ln_g.reshape(1, cout), ln_b.reshape(1, cout)]
        out_specs += [acc8, acc8]
        out_shape += [_sds((8, cout), f32), _sds((8, cout), f32)]
    in_specs += [pl.BlockSpec(memory_space=pl.ANY)] * len(shared_in)
    args += shared_in
    return pl.pallas_call(
        body, name=name, grid=(n_seq, nblk), in_specs=in_specs, out_specs=out_specs, out_shape=out_shape,
        input_output_aliases=aliases,
        scratch_shapes=[pltpu.VMEM((halo + TB + 8, cout), f32), pltpu.VMEM((TB, cout), f32),
                        pltpu.VMEM((halo + TB + halo, cout), f32), pltpu.VMEM((halo + TB, cout), f32),
                        pltpu.VMEM((halo, cout), f32),
                        pltpu.VMEM((len(_residues(range(off, off + kw))), halo + TB, cout), f32),
                        pltpu.VMEM((len(_residues(range(kw))), halo + TB + halo - 8, cout), f32)],
        compiler_params=_cparams(("arbitrary", "arbitrary")),
    )(*args)


def _gelu(x):
    return 0.5 * x * (1.0 + lax.erf(x * (1.0 / math.sqrt(2.0))))


def _tril_mask(n):
    r = lax.broadcasted_iota(jnp.int32, (n, n), 0)
    c = lax.broadcasted_iota(jnp.int32, (n, n), 1)
    return r >= c


def _head_spread(nh):
    r = lax.broadcasted_iota(jnp.int32, (nh, nh * HEAD), 0)
    c = lax.broadcasted_iota(jnp.int32, (nh, nh * HEAD), 1)
    return (c // HEAD == r).astype(f32)


def _gmlp_bias(bs):
    return lax.dot_general(bs, _head_spread(bs.shape[0]), (((0,), (0,)), ((), ())), precision=HI, preferred_element_type=f32)


def _gmlp_pair(bu, bv, g, b, w_a, w_b, bias):
    lo = lax.broadcasted_iota(jnp.int32, bu.shape, 1) < HEAD
    tril = _tril_mask(CHUNK)
    u = _gelu(bu)
    vn = _pair_ln(_gelu(bv), g, b)
    mix = jnp.where(lo, _bdot(jnp.where(tril, w_a, 0.0), vn, "nn"), _bdot(jnp.where(tril, w_b, 0.0), vn, "nn"))
    return u * (mix + bias)


def _gmlp_fwd(proj, col_block, ln_g, ln_b, w_s, b_s, share=None):
    T = proj.shape[0]
    nh = w_s.shape[0]
    width = nh * HEAD
    out_shape, out_blk, shared_in, aliases = _shared(share, T, width, bf16, 5)

    def body(p_ref, g_ref, b_ref, w_ref, bs_ref, *rest):
        out_ref, bias_ref = rest[len(shared_in):]

        @pl.when(pl.program_id(0) == 0)
        def _():
            bias_ref[...] = _gmlp_bias(bs_ref[...])

        for q in range(nh // 2):
            ls = pl.ds(q * PAIR, PAIR)
            lv = pl.ds(width + q * PAIR, PAIR)
            out_ref[:, ls] = _gmlp_pair(p_ref[:, ls].astype(f32), p_ref[:, lv].astype(f32), g_ref[:, ls], b_ref[:, ls], w_ref[2 * q], w_ref[2 * q + 1],
                                        bias_ref[:, ls]).astype(out_ref.dtype)

    row = pl.BlockSpec((1, width), lambda i: (0, 0))
    return pl.pallas_call(
        body, name="gmlp_fwd", grid=(T // CHUNK,),
        in_specs=[pl.BlockSpec((CHUNK, 2 * width), lambda i: (i, col_block)), row, row,
                  pl.BlockSpec((nh, CHUNK, CHUNK), lambda i: (0, 0, 0)), pl.BlockSpec((nh, CHUNK), lambda i: (0, 0))]
        + [pl.BlockSpec(memory_space=pl.ANY)] * len(shared_in),
        out_specs=pl.BlockSpec((CHUNK, width), lambda i: (i, out_blk)),
        out_shape=out_shape, input_output_aliases=aliases, scratch_shapes=[pltpu.VMEM((CHUNK, width), f32)],
        compiler_params=_cparams(("arbitrary",)),
    )(proj, ln_g.reshape(1, width), ln_b.reshape(1, width), w_s, b_s, *shared_in)


def _gmlp_bwd(proj, col_block, ln_g, ln_b, w_s, b_s, dy, dy_col_block, share=None):
    T = proj.shape[0]
    nh = w_s.shape[0]
    width = nh * HEAD
    nstep = T // CHUNK
    dp_shape, dp_blk, shared_in, aliases = _shared(share, T, 2 * width, bf16, 6)

    def body(p_ref, g_ref, b_ref, w_ref, bs_ref, dy_ref, *rest):
        dp_ref, dg_ref, db_ref, dw_ref, dbst_ref, bias_ref, dbias_ref = rest[len(shared_in):]

        @pl.when(pl.program_id(0) == 0)
        def _():
            dg_ref[...] = jnp.zeros_like(dg_ref)
            db_ref[...] = jnp.zeros_like(db_ref)
            dw_ref[...] = jnp.zeros_like(dw_ref)
            dbias_ref[...] = jnp.zeros_like(dbias_ref)
            bias_ref[...] = _gmlp_bias(bs_ref[...])

        for q in range(nh // 2):
            ls = pl.ds(q * PAIR, PAIR)
            lv = pl.ds(width + q * PAIR, PAIR)
            _, vjp = jax.vjp(_gmlp_pair, p_ref[:, ls].astype(f32), p_ref[:, lv].astype(f32), g_ref[:, ls], b_ref[:, ls], w_ref[2 * q], w_ref[2 * q + 1],
                             bias_ref[:, ls])
            dbu, dbv, dg, db, dw_a, dw_b, dbias = vjp(dy_ref[:, ls].astype(f32))
            dp_ref[:, ls] = dbu.astype(dp_ref.dtype)
            dp_ref[:, lv] = dbv.astype(dp_ref.dtype)
            dg_ref[0:1, ls] += dg
            db_ref[0:1, ls] += db
            dw_ref[2 * q] += dw_a
            dw_ref[2 * q + 1] += dw_b
            dbias_ref[:, ls] += dbias

        @pl.when(pl.program_id(0) == nstep - 1)
        def _():
            dbst_ref[...] = lax.dot_general(dbias_ref[...], _head_spread(nh), (((1,), (1,)), ((), ())),
                                            precision=HI, preferred_element_type=f32)

    row = pl.BlockSpec((1, width), lambda i: (0, 0))
    acc8 = pl.BlockSpec((8, width), lambda i: (0, 0))
    wspec = pl.BlockSpec((nh, CHUNK, CHUNK), lambda i: (0, 0, 0))
    res = pl.pallas_call(
        body, name="gmlp_bwd", grid=(nstep,),
        in_specs=[pl.BlockSpec((CHUNK, 2 * width), lambda i: (i, col_block)), row, row, wspec,
                  pl.BlockSpec((nh, CHUNK), lambda i: (0, 0)), pl.BlockSpec((CHUNK, width), lambda i: (i, dy_col_block))]
        + [pl.BlockSpec(memory_space=pl.ANY)] * len(shared_in),
        out_specs=[pl.BlockSpec((CHUNK, 2 * width), lambda i: (i, dp_blk)), acc8, acc8, wspec,
                   pl.BlockSpec((CHUNK, nh), lambda i: (0, 0))],
        out_shape=[dp_shape, _sds((8, width), f32), _sds((8, width), f32),
                   _sds((nh, CHUNK, CHUNK), f32), _sds((CHUNK, nh), f32)],
        input_output_aliases=aliases,
        scratch_shapes=[pltpu.VMEM((CHUNK, width), f32), pltpu.VMEM((CHUNK, width), f32)],
        compiler_params=_cparams(("arbitrary",)),
    )(proj, ln_g.reshape(1, width), ln_b.reshape(1, width), w_s, b_s, dy, *shared_in)
    return res[0], res[1], res[2], res[3], res[4].T


def _sel_col(x, h):
    lane = lax.broadcasted_iota(jnp.int32, x.shape, 1)
    return jnp.sum(jnp.where(lane == h, x, 0.0), axis=1, keepdims=True)


def _sel_row(x, h):
    sub = lax.broadcasted_iota(jnp.int32, x.shape, 0)
    return jnp.sum(jnp.where(sub == h, x, 0.0), axis=0, keepdims=True)


def _ssd_chunk(nh, ngrp, xs_l, z_l, b_l, c_l, dtraw, dtb, alog, dskip, ng_l, prev_l):
    hg = nh // ngrp
    tril = _tril_mask(CHUNK)
    tl = tril.astype(f32)
    lo = lax.broadcasted_iota(jnp.int32, (CHUNK, PAIR), 1) < HEAD
    lo_row = lo[0:1, :]
    dt = jax.nn.softplus(dtraw + dtb)
    a = dt * (-jnp.exp(alog))
    cs = jnp.dot(tl, a, precision=HI, preferred_element_type=f32)
    cst = lax.dot_general(a, tl, (((0,), (1,)), ((), ())), precision=HI, preferred_element_type=f32)
    cb_l = [_bdot(c_l[g], b_l[g], "nt") for g in range(ngrp)]
    yz_l, new_prev = [], []
    for q in range(nh // 2):
        g = (2 * q) // hg
        cols = []
        for h in (2 * q, 2 * q + 1):
            cs_h = _sel_col(cs, h)
            tot = _sel_row(cs_h, CHUNK - 1)
            seg = jnp.where(tril, cs_h - _sel_row(cst, h), 0.0)
            lmat = jnp.where(tril, jnp.exp(seg), 0.0)
            cols.append((_sel_col(dt, h), cs_h, tot, lmat, _sel_col(dskip, h)))
        (dt_a, cs_a, tot_a, l_a, dsk_a), (dt_b, cs_b, tot_b, l_b, dsk_b) = cols
        xs = xs_l[q]
        x = xs * jnp.where(lo, dt_a, dt_b)
        ydiag = jnp.where(lo, _bdot(cb_l[g] * l_a, x, "nn"), _bdot(cb_l[g] * l_b, x, "nn"))
        yoff = _bdot(c_l[g], prev_l[q], "nn") * jnp.where(lo, jnp.exp(cs_a), jnp.exp(cs_b))
        xdec = x * jnp.where(lo, jnp.exp(tot_a - cs_a), jnp.exp(tot_b - cs_b))
        st = _bdot(b_l[g], xdec, "tn")
        new_prev.append(prev_l[q] * jnp.where(lo_row, jnp.exp(tot_a), jnp.exp(tot_b)) + st)
        y = ydiag + yoff + jnp.where(lo_row, dsk_a, dsk_b) * xs
        yz_l.append(y * jax.nn.silu(z_l[q]))
    out = [None] * (nh // 2)
    qg = hg // 2
    for g in range(ngrp):
        ssq = sum(jnp.sum(yz_l[q] * yz_l[q], axis=-1, keepdims=True) for q in range(g * qg, (g + 1) * qg))
        r = lax.rsqrt(ssq * (1.0 / (hg * HEAD)) + EPS)
        for q in range(g * qg, (g + 1) * qg):
            out[q] = yz_l[q] * r * ng_l[q]
    return out, new_prev


def _ssd_read(nh, ngrp, nst, xbc_ref, z_ref, ng_ref, st_ref):
    cw = nh * HEAD
    xs_l = [xbc_ref[:, pl.ds(q * PAIR, PAIR)] for q in range(nh // 2)]
    b_l = [xbc_ref[:, pl.ds(cw + g * nst, nst)] for g in range(ngrp)]
    c_l = [xbc_ref[:, pl.ds(cw + ngrp * nst + g * nst, nst)] for g in range(ngrp)]
    z_l = [z_ref[:, pl.ds(q * PAIR, PAIR)].astype(f32) for q in range(nh // 2)]
    ng_l = [ng_ref[:, pl.ds(q * PAIR, PAIR)] for q in range(nh // 2)]
    prev_l = [st_ref[:, pl.ds(q * PAIR, PAIR)] for q in range(nh // 2)]
    return xs_l, z_l, b_l, c_l, ng_l, prev_l


def _ssd_fwd(xbc, proj, z_col_block, pdt, dtb, alog, dskip, ng, nh, ngrp, nst, n_seq, share=None):
    T = xbc.shape[0]
    cw = nh * HEAD
    nchunk = T // n_seq // CHUNK
    assert nst == CHUNK
    y_shape, y_blk, shared_in, aliases = _shared(share, T, cw, bf16, 7)

    def body(xbc_ref, z_ref, dt_ref, dtb_ref, alog_ref, dskip_ref, ng_ref, *rest):
        y_ref, sin_ref, st_ref = rest[len(shared_in):]

        @pl.when(pl.program_id(1) == 0)
        def _():
            st_ref[...] = jnp.zeros_like(st_ref)

        sin_ref[...] = st_ref[...]
        xs_l, z_l, b_l, c_l, ng_l, prev_l = _ssd_read(nh, ngrp, nst, xbc_ref, z_ref, ng_ref, st_ref)
        y_l, new_prev = _ssd_chunk(nh, ngrp, xs_l, z_l, b_l, c_l, dt_ref[...], dtb_ref[...], alog_ref[...],
                                   dskip_ref[...], ng_l, prev_l)
        for q in range(nh // 2):
            ls = pl.ds(q * PAIR, PAIR)
            y_ref[:, ls] = y_l[q].astype(y_ref.dtype)
            st_ref[:, ls] = new_prev[q]

    def blk(s, c):
        return s * nchunk + c

    prow = pl.BlockSpec((1, 128), lambda s, c: (0, 0))
    return pl.pallas_call(
        body, name="ssd_fwd", grid=(n_seq, nchunk),
        in_specs=[pl.BlockSpec((CHUNK, xbc.shape[1]), lambda s, c: (blk(s, c), 0)),
                  pl.BlockSpec((CHUNK, cw), lambda s, c: (blk(s, c), z_col_block)),
                  pl.BlockSpec((CHUNK, 128), lambda s, c: (blk(s, c), 0)),
                  prow, prow, prow, pl.BlockSpec((1, cw), lambda s, c: (0, 0))]
        + [pl.BlockSpec(memory_space=pl.ANY)] * len(shared_in),
        out_specs=[pl.BlockSpec((CHUNK, cw), lambda s, c: (blk(s, c), y_blk)),
                   pl.BlockSpec((nst, cw), lambda s, c: (blk(s, c), 0))],
        out_shape=[y_shape, _sds((T, cw), f32)], input_output_aliases=aliases,
        scratch_shapes=[pltpu.VMEM((nst, cw), f32)],
        compiler_params=_cparams(("arbitrary", "arbitrary")),
    )(xbc, proj, pdt, dtb, alog, dskip, ng.reshape(1, cw), *shared_in)


def _ssd_bwd(xbc, proj, z_col_block, pdt, dtb, alog, dskip, ng, sin, dy, dy_col_block, nh, ngrp, nst, n_seq, share=None):
    T, xw = xbc.shape
    cw = nh * HEAD
    nchunk = T // n_seq // CHUNK
    dz_shape, dz_blk, shared_in, aliases = _shared(share, T, cw, bf16, 9, out_index=1)

    def body(xbc_ref, z_ref, dt_ref, dtb_ref, alog_ref, dskip_ref, ng_ref, sin_ref, dy_ref, *rest):
        dxbc_ref, dz_ref, ddt_ref, ddtb_ref, dalog_ref, ddskip_ref, dng_ref, dst_ref = rest[len(shared_in):]
        s, cc = pl.program_id(0), pl.program_id(1)

        @pl.when((s == 0) & (cc == 0))
        def _():
            ddtb_ref[...] = jnp.zeros_like(ddtb_ref)
            dalog_ref[...] = jnp.zeros_like(dalog_ref)
            ddskip_ref[...] = jnp.zeros_like(ddskip_ref)
            dng_ref[...] = jnp.zeros_like(dng_ref)

        @pl.when(cc == 0)
        def _():
            dst_ref[...] = jnp.zeros_like(dst_ref)

        xs_l, z_l, b_l, c_l, ng_l, prev_l = _ssd_read(nh, ngrp, nst, xbc_ref, z_ref, ng_ref, sin_ref)
        _, vjp = jax.vjp(functools.partial(_ssd_chunk, nh, ngrp), xs_l, z_l, b_l, c_l, dt_ref[...], dtb_ref[...],
                         alog_ref[...], dskip_ref[...], ng_l, prev_l)
        dy_l = [dy_ref[:, pl.ds(q * PAIR, PAIR)].astype(f32) for q in range(nh // 2)]
        dst_l = [dst_ref[:, pl.ds(q * PAIR, PAIR)] for q in range(nh // 2)]
        dxs_l, dz_l, db_l, dc_l, ddt, ddtb, dalog, ddskip, dng_l, dprev_l = vjp((dy_l, dst_l))
        for q in range(nh // 2):
            ls = pl.ds(q * PAIR, PAIR)
            dxbc_ref[:, ls] = dxs_l[q]
            dz_ref[:, ls] = dz_l[q].astype(dz_ref.dtype)
            dng_ref[0:1, ls] += dng_l[q]
            dst_ref[:, ls] = dprev_l[q]
        for g in range(ngrp):
            dxbc_ref[:, pl.ds(cw + g * nst, nst)] = db_l[g]
            dxbc_ref[:, pl.ds(cw + ngrp * nst + g * nst, nst)] = dc_l[g]
        ddt_ref[...] = ddt.astype(ddt_ref.dtype)
        ddtb_ref[0:1, :] += ddtb
        dalog_ref[0:1, :] += dalog
        ddskip_ref[0:1, :] += ddskip

    def blk(s, cc):
        return s * nchunk + (nchunk - 1 - cc)

    prow = pl.BlockSpec((1, 128), lambda s, c: (0, 0))
    pacc = pl.BlockSpec((8, 128), lambda s, c: (0, 0))
    return pl.pallas_call(
        body, name="ssd_bwd", grid=(n_seq, nchunk),
        in_specs=[pl.BlockSpec((CHUNK, xw), lambda s, c: (blk(s, c), 0)),
                  pl.BlockSpec((CHUNK, cw), lambda s, c: (blk(s, c), z_col_block)),
                  pl.BlockSpec((CHUNK, 128), lambda s, c: (blk(s, c), 0)),
                  prow, prow, prow, pl.BlockSpec((1, cw), lambda s, c: (0, 0)),
                  pl.BlockSpec((nst, cw), lambda s, c: (blk(s, c), 0)),
                  pl.BlockSpec((CHUNK, cw), lambda s, c: (blk(s, c), dy_col_block))]
        + [pl.BlockSpec(memory_space=pl.ANY)] * len(shared_in),
        out_specs=[pl.BlockSpec((CHUNK, xw), lambda s, c: (blk(s, c), 0)),
                   pl.BlockSpec((CHUNK, cw), lambda s, c: (blk(s, c), dz_blk)),
                   pl.BlockSpec((CHUNK, 128), lambda s, c: (blk(s, c), 0)),
                   pacc, pacc, pacc, pl.BlockSpec((8, cw), lambda s, c: (0, 0))],
        out_shape=[_sds((T, xw), f32), dz_shape, _sds((T, 128), bf16),
                   _sds((8, 128), f32), _sds((8, 128), f32), _sds((8, 128), f32), _sds((8, cw), f32)],
        input_output_aliases=aliases,
        scratch_shapes=[pltpu.VMEM((nst, cw), f32)],
        compiler_params=_cparams(("arbitrary", "arbitrary")),
    )(xbc, proj, pdt, dtb, alog, dskip, ng.reshape(1, cw), sin, dy, *shared_in)


_HBM = pl.BlockSpec(memory_space=pltpu.HBM)
_SEM = pl.BlockSpec(memory_space=pltpu.SEMAPHORE)
_EFFECT = pltpu.SideEffectType.DATAFLOW_SIDE_EFFECTING


def _split_copies(n, scatter, src_refs, land_refs, send_sems, recv_sems):
    npeer = N_DEV - 1
    x, y, c = lax.axis_index("x"), lax.axis_index("y"), lax.axis_index("c")
    me = 4 * x + 2 * y + c
    copies = []
    for i in range(n):
        for k in range(1, N_DEV):
            px = 1 - x if k & 4 else x
            py = 1 - y if k & 2 else y
            pc = 1 - c if k & 1 else c
            src = src_refs[i].at[4 * px + 2 * py + pc] if scatter else src_refs[i]
            copies.append(pltpu.make_async_remote_copy(
                src_ref=src, dst_ref=land_refs[i].at[me],
                send_sem=send_sems.at[i * npeer + k - 1], recv_sem=recv_sems.at[i * npeer + k - 1],
                device_id=(px, py, pc), device_id_type=pl.DeviceIdType.MESH))
    return copies


def _exchange_start(name, arrs, scatter):
    n = len(arrs)
    nsem = n * (N_DEV - 1)
    me = 4 * lax.axis_index("x") + 2 * lax.axis_index("y") + lax.axis_index("c")
    lands = []
    for a in arrs:
        own = lax.dynamic_index_in_dim(a, me, 0, keepdims=True) if scatter else a[None]
        full = lax.empty(a.shape if scatter else (N_DEV,) + a.shape, a.dtype)
        lands.append(lax.dynamic_update_slice(full, own, (me,) + (0,) * (full.ndim - 1)))

    def body(*refs):
        src_refs, land_refs = refs[:n], refs[n:2 * n]
        send_sems, recv_sems = refs[2 * n], refs[2 * n + 1]
        token = refs[-1]
        for cp in _split_copies(n, scatter, src_refs, land_refs, send_sems, recv_sems):
            cp.start()
        token[...] = jnp.zeros_like(token)

    res = pl.pallas_call(
        body, name=name,
        out_shape=(pltpu.SemaphoreType.DMA((nsem,)), pltpu.SemaphoreType.DMA((nsem,)),
                   *[pltpu.HBM(a.shape, a.dtype) for a in arrs], *[pltpu.HBM(l.shape, l.dtype) for l in lands],
                   _sds((8, 128), f32)),
        in_specs=[_HBM] * (2 * n),
        out_specs=(_SEM, _SEM, *[_HBM] * (2 * n), pl.BlockSpec(memory_space=pltpu.VMEM)),
        input_output_aliases={j: 2 + j for j in range(2 * n)},
        compiler_params=pltpu.CompilerParams(has_side_effects=_EFFECT),
    )(*[pltpu.with_memory_space_constraint(a, pltpu.HBM) for a in arrs],
      *[pltpu.with_memory_space_constraint(l, pltpu.HBM) for l in lands])
    return (n, scatter, res[0], res[1], res[2:2 + n], res[2 + n:2 + 2 * n]), res[-1]


def _exchange_wait(name, handle, after):
    n, scatter, send_sems, recv_sems, srcs, lands = handle
    after = list(after) if isinstance(after, (list, tuple)) else [after]

    def body(*refs):
        src_refs, land_refs = refs[:n], refs[n:2 * n]
        for cp in _split_copies(n, scatter, src_refs, land_refs, refs[2 * n], refs[2 * n + 1]):
            cp.wait_send()
            cp.wait_recv()

    res = pl.pallas_call(
        body, name=name,
        out_shape=[pltpu.HBM(a.shape, a.dtype) for a in (*srcs, *lands)],
        in_specs=[_HBM] * (2 * n) + [_SEM, _SEM] + [pl.BlockSpec(memory_space=pl.ANY)] * len(after),
        out_specs=[_HBM] * (2 * n),
        input_output_aliases={j: j for j in range(2 * n)},
        compiler_params=pltpu.CompilerParams(has_side_effects=_EFFECT),
    )(*srcs, *lands, send_sems, recv_sems, *after)
    return res[n:]


def _adam_tiles(R, C):
    if R % 256 == 0:
        return (256, C), (R // 256, 1)
    assert C % 128 == 0
    return (R, 128), (1, C // 128)


def _adam(name, parts, w, m, v):
    P, R, C = parts.shape
    (tr, tc), (gr, gc) = _adam_tiles(R, C)
    c1 = 1.0 / (1.0 - ADAM_B1 ** ADAM_STEP)
    c2 = 1.0 / (1.0 - ADAM_B2 ** ADAM_STEP)

    def body(p_ref, w_ref, m_ref, v_ref, g_ref, d_ref, nm_ref, nv_ref):
        g = p_ref[0].astype(f32)
        for s in range(1, P):
            g = g + p_ref[s].astype(f32)
        nm = ADAM_B1 * m_ref[...] + (1.0 - ADAM_B1) * g
        nv = ADAM_B2 * v_ref[...] + (1.0 - ADAM_B2) * (g * g)
        g_ref[...] = g
        nm_ref[...] = nm
        nv_ref[...] = nv
        d_ref[...] = -ADAM_LR * ((nm * c1) / (jnp.sqrt(nv * c2) + ADAM_EPS) + ADAM_WD * w_ref[...])

    tile = pl.BlockSpec((tr, tc), lambda i, j: (i, j))
    return pl.pallas_call(
        body, name=name, grid=(gr, gc),
        in_specs=[pl.BlockSpec((P, tr, tc), lambda i, j: (0, i, j)), tile, tile, tile],
        out_specs=[tile] * 4, out_shape=[_sds((R, C), f32)] * 4,
        compiler_params=_cparams(("arbitrary", "arbitrary")),
    )(parts, w, m, v)


def _sum_parts(name, parts):
    P, R, C = parts.shape
    tr = 256 if R % 256 == 0 else R

    def body(p_ref, o_ref):
        g = p_ref[0]
        for s in range(1, P):
            g = g + p_ref[s]
        o_ref[...] = g

    return pl.pallas_call(
        body, name=name, grid=(R // tr,),
        in_specs=[pl.BlockSpec((P, tr, C), lambda i: (0, i, 0))], out_specs=pl.BlockSpec((tr, C), lambda i: (i, 0)),
        out_shape=_sds((R, C), f32), compiler_params=_cparams(("arbitrary",)),
    )(parts)


def _pad_to(a, n, axis):
    if a.shape[axis] == n:
        return a
    cfg = [(0, 0)] * a.ndim
    cfg[axis] = (0, n - a.shape[axis])
    return jnp.pad(a, cfg)


def _pack(arrs):
    flat = [_pad_to(a.reshape(-1), -(-a.size // 128) * 128, 0) for a in arrs]
    rows = jnp.concatenate(flat).reshape(-1, 128)
    return _pad_to(rows, -(-rows.shape[0] // 256) * 256, 0)


def _unpack(slab, shapes):
    flat = slab.reshape(-1)
    out, o = [], 0
    for s in shapes:
        n = math.prod(s)
        out.append(flat[o:o + n].reshape(s))
        o += -(-n // 128) * 128
    return out


_NAMES = ['norm1_g', 'w_in', 'conv_a_w', 'conv_a_b', 'ln_a_g', 'ln_a_b', 'ln_b_g', 'ln_b_b', 'w_spatial', 'b_spatial',
          'conv_c_w', 'conv_c_b', 'dt_bias', 'a_log', 'd_skip', 'norm_c_g', 'w_out', 'norm2_g', 'w_ff1', 'w_ff2', 'final_g']
_REPL = ['norm1_g', 'conv_a_b', 'ln_a_g', 'ln_a_b', 'ln_b_g', 'ln_b_b', 'w_spatial', 'b_spatial', 'conv_c_b',
         'dt_bias', 'a_log', 'd_skip', 'norm_c_g', 'norm2_g']
_CONVW = ['conv_a_w', 'conv_c_w']
_BIG = ['w_in', 'w_out', 'w_ff1', 'w_ff2']
_BIG_T = {'w_in': True, 'w_out': False, 'w_ff1': True, 'w_ff2': False}


def _row128(v):
    return _pad_to(v.reshape(1, -1), 128, 1)


def _step(p, m, v, x, loss_target):
    nb, S, D = x.shape
    T = nb * S
    depth = p['norm1_g'].shape[0]
    a_w = p['conv_a_b'].shape[1]
    b_w = p['ln_b_g'].shape[1]
    nh = p['dt_bias'].shape[1]
    c_w = p['norm_c_g'].shape[1]
    xw = p['conv_c_b'].shape[1]
    ngrp = 2
    nst = (xw - c_w) // (2 * ngrp)
    d_in = p['w_in'].shape[2] * N_DEV
    main = d_in - nh
    assert main == 2 * a_w + 2 * b_w + c_w + xw and 2 * a_w == 2 * b_w == c_w and xw % c_w == c_w // 2
    me = 4 * lax.axis_index("x") + 2 * lax.axis_index("y") + lax.axis_index("c")

    x2 = x.reshape(T, D)
    tgt = loss_target.reshape(T, D)

    def shards(i, z=None):
        z = 0.0 if z is None else z
        return [(p['w_in'][i].T + z).astype(bf16), (p['w_out'][i] + z).astype(bf16), (p['w_ff1'][i].T + z).astype(bf16),
                (p['w_ff2'][i] + z).astype(bf16), p['conv_a_w'][i], p['conv_c_w'][i]]

    def gathered_in(wt, ca, cc):
        wt = wt.reshape(d_in, D)
        ca = jnp.transpose(ca, (1, 0, 2)).reshape(KA, a_w)
        cc = jnp.transpose(cc, (1, 0, 2)).reshape(KC, xw)
        return dict(wt_main=wt[:main], wt_dt=_pad_to(wt[main:], 128, 0), ca=_pad_to(ca, 32, 0), cc=_pad_to(cc, 8, 0))

    def start_layer(i, after=None):
        sh = shards(i, None if after is None else after[0, 0])
        ha, t = _exchange_start("gather_w%da_start" % i, [sh[0], sh[4], sh[5]], False)
        hb, t = _exchange_start("gather_w%db_start" % i, [shards(i, t[0, 0])[1]], False)
        hc, t = _exchange_start("gather_w%dc_start" % i, [shards(i, t[0, 0])[2]], False)
        hd, t = _exchange_start("gather_w%dd_start" % i, [shards(i, t[0, 0])[3]], False)
        return dict(a=ha, b=hb, c=hc, d=hd), t

    W, saved = [], []
    xc = x2
    H, tok = start_layer(0)
    for i in range(depth):
        w = gathered_in(*_exchange_wait("gather_w%da_wait" % i, H['a'], [xc, tok]))
        W.append(w)
        Hi = H
        h1, rtok = _rms_fwd(xc, p['norm1_g'][i])
        if i + 1 < depth:
            H, tok = start_layer(i + 1, rtok + tok)
        else:
            tok = None
        (proj,) = _mm("mm_proj", h1, w['wt_main'], "nt", [bf16], dep=tok)
        (pdt,) = _mm("mm_pdt", h1, w['wt_dt'], "nt", [f32])
        mixw = a_w + b_w + c_w
        ycat = _conv_fwd("confa_fwd", proj, 0, w['ca'], p['conv_a_b'][i], KA, True, nb, p['ln_a_g'][i], p['ln_a_b'][i],
                         share=(mixw, 0, None))
        ycat = _gmlp_fwd(proj, 1, p['ln_b_g'][i], p['ln_b_b'][i], p['w_spatial'][i], p['b_spatial'][i], share=(mixw, 1, ycat))
        xbc = _conv_fwd("convc_fwd", proj, 2, w['cc'], p['conv_c_b'][i], KC, False, nb)
        dtb, alog, dsk = _row128(p['dt_bias'][i]), _row128(p['a_log'][i]), _row128(p['d_skip'][i])
        ycat, sin = _ssd_fwd(xbc, proj, 2, pdt, dtb, alog, dsk, p['norm_c_g'][i], nh, ngrp, nst, nb, share=(mixw, 1, ycat))
        w['wout'] = _exchange_wait("gather_w%db_wait" % i, Hi['b'], ycat)[0].reshape(-1, D)
        (xm,) = _mm("mm_out", ycat, w['wout'], "nn", [f32], _ep_add, (xc,))
        h2, _ = _rms_fwd(xm, p['norm2_g'][i])
        w['w1t'] = _exchange_wait("gather_w%dc_wait" % i, Hi['c'], h2)[0].reshape(-1, D)
        f, a = _mm("mm_ff1", h2, w['w1t'], "nt", [bf16, bf16], _ep_relu2)
        w['w2'] = _exchange_wait("gather_w%dd_wait" % i, Hi['d'], a)[0].reshape(-1, D)
        (xo,) = _mm("mm_ff2", a, w['w2'], "nn", [f32], _ep_add, (xm,))
        saved.append(dict(x_in=xc, h1=h1, proj=proj, pdt=pdt, xbc=xbc, sin=sin, ycat=ycat, xm=xm, h2=h2, f=f, a=a,
                          dtb=dtb, alog=alog, dsk=dsk))
        xc = xo

    lp, dx, dfinal = _loss_head(xc, p['final_g'], tgt)
    loss = lax.psum(lp[0, 0], ("x", "y", "c"))

    out = {}
    kinds = ("grad", "delta", "new_m", "new_v")
    names1 = _REPL + _CONVW

    started, small = [], [None] * depth

    def send(n, i, g):
        handle, token = _exchange_start("scatter_%s_%d_start" % (n, i), [g.reshape(N_DEV, -1, D)], True)
        started.append((n, i, handle))
        return token

    tok = None
    for i in reversed(range(depth)):
        w, sv = W[i], saved[i]
        (df,) = _mm("mm_df", dx, w['w2'], "nt", [bf16], _ep_drelu2, (sv['f'],), dep=tok)
        (gw2,) = _mm("mm_gw2", sv['a'], dx, "tn", [bf16])
        tok = send('w_ff2', i, gw2)
        (dh2,) = _mm("mm_dh2", df, w['w1t'], "nn", [f32], dep=tok)
        (gw1t,) = _mm("mm_gw1", df, sv['h2'], "tn", [bf16])
        tok = send('w_ff1', i, gw1t)
        dxm, dg2 = _rms_bwd(sv['xm'], p['norm2_g'][i], dh2, dx)
        (dycat,) = _mm("mm_dycat", dxm, w['wout'], "nt", [bf16], dep=tok)
        (gwout,) = _mm("mm_gwout", sv['ycat'], dxm, "tn", [bf16])
        tok = send('w_out', i, gwout)
        dproj, dwa, dba, dlag, dlab = _conv_bwd("confa_bwd", sv['proj'], 0, w['ca'], p['conv_a_b'][i] + tok[0, 0], dycat, 0, KA,
                                                True, nb, p['ln_a_g'][i], p['ln_a_b'][i], share=(main, 0, None))
        dproj, dlbg, dlbb, dws, dbs = _gmlp_bwd(sv['proj'], 1, p['ln_b_g'][i], p['ln_b_b'][i], p['w_spatial'][i],
                                                p['b_spatial'][i], dycat, 1, share=(main, 1, dproj))
        dxbc, dproj, ddt, ddtb, dalog, ddsk, dng = _ssd_bwd(sv['xbc'], sv['proj'], 2, sv['pdt'], sv['dtb'], sv['alog'],
                                                            sv['dsk'], p['norm_c_g'][i], sv['sin'], dycat, 1, nh, ngrp, nst, nb,
                                                            share=(main, 2, dproj))
        dproj, dwc, dbc = _conv_bwd("convc_bwd", sv['proj'], 2, w['cc'], p['conv_c_b'][i], dxbc, 0, KC, False, nb,
                                    share=(main, 2, dproj))
        (dh_main,) = _mm("mm_dh1", dproj, w['wt_main'], "nn", [f32])
        (dh,) = _mm("mm_dh1dt", ddt, w['wt_dt'], "nn", [f32], _ep_add, (dh_main,))
        (gwt_main,) = _mm("mm_gwin", dproj, sv['h1'], "tn", [bf16])
        (gwt_dt,) = _mm("mm_gwdt", ddt, sv['h1'], "tn", [bf16])
        tok = send('w_in', i, jnp.concatenate([gwt_main, gwt_dt[:nh]], axis=0))
        dx, dg1 = _rms_bwd(sv['x_in'], p['norm1_g'][i] + tok[0, 0], dh, dxm)

        gi = dict(norm1_g=dg1[0], norm2_g=dg2[0], conv_a_w=dwa[:KA], conv_a_b=dba[0], ln_a_g=dlag[0], ln_a_b=dlab[0],
                  ln_b_g=dlbg[0], ln_b_b=dlbb[0], w_spatial=dws, b_spatial=dbs, conv_c_w=dwc[:KC], conv_c_b=dbc[0],
                  dt_bias=ddtb[0, :nh], a_log=dalog[0, :nh], d_skip=ddsk[0, :nh], norm_c_g=dng[0])
        parts_i = [gi[n] for n in names1] + ([dfinal[0]] if i == depth - 1 else [])
        handle, tok = _exchange_start("gather_g%d_start" % i, [_pack(parts_i)], False)
        small[i] = ([a.shape for a in parts_i], handle)
    grad_x = dx.reshape(nb, S, D)

    dep = [dx, tok]
    for n, i, handle in started:
        (parts,) = _exchange_wait("scatter_%s_%d_wait" % (n, i), handle, dep)
        tr = (lambda t: t.T) if _BIG_T[n] else (lambda t: t)
        res = _adam("adam_" + n, parts, tr(p[n][i]), tr(m[n][i]), tr(v[n][i]))
        for kind, r in zip(kinds, res):
            out.setdefault((kind, n), [None] * depth)[i] = r
        dep = res[3]

    gsum = [None] * depth
    for i in reversed(range(depth)):
        (parts,) = _exchange_wait("gather_g%d_wait" % i, small[i][1], dep)
        gsum[i] = _sum_parts("sum_small", parts)
        dep = gsum[i]
    widths = [-(-math.prod(p[n].shape[1:]) // 128) * 128 for n in _REPL]
    rep_rows = sum(widths) // 128
    tot_rows = -(-depth * rep_rows // 256) * 256

    def rep_slab(q):
        cols = [_pad_to(q[n].reshape(depth, -1), wd, 1) for n, wd in zip(_REPL, widths)]
        return _pad_to(jnp.concatenate(cols, axis=1).reshape(-1, 128), tot_rows, 0)

    g_rep = _pad_to(jnp.concatenate([g[:rep_rows] for g in gsum], axis=0), tot_rows, 0)
    res = _adam("adam_small", g_rep[None], rep_slab(p), rep_slab(m), rep_slab(v))
    for kind, r in zip(kinds, res):
        view = r[:depth * rep_rows].reshape(depth, -1)
        o = 0
        for n, wd in zip(_REPL, widths):
            out[(kind, n)] = view[:, o:o + math.prod(p[n].shape[1:])].reshape(p[n].shape)
            o += wd

    extra = []
    for i in range(depth):
        tail = _unpack(gsum[i][rep_rows:], small[i][0][len(_REPL):])
        extra.append(tail)
    gconv = []
    for j, n in enumerate(_CONVW):
        cw_shard = p[n].shape[2]
        full = jnp.stack([extra[i][j] for i in range(depth)])
        gconv.append(lax.dynamic_slice_in_dim(full, me * cw_shard, cw_shard, axis=2))
    tail_names = _CONVW + ['final_g']
    res = _adam("adam_conv", _pack(gconv + [extra[depth - 1][len(_CONVW)]])[None],
                *[_pack([q[n] for n in tail_names]) for q in (p, m, v)])
    for kind, r in zip(kinds, res):
        for n, arr in zip(tail_names, _unpack(r, [p[n].shape for n in tail_names])):
            out[(kind, n)] = arr
    for n in _BIG:
        for kind in kinds:
            stacked = jnp.stack(out[(kind, n)])
            out[(kind, n)] = jnp.swapaxes(stacked, 1, 2) if _BIG_T[n] else stacked

    flat = [loss, grad_x]
    for kind in ("grad", "delta", "new_m", "new_v"):
        flat += [out[(kind, n)] for n in _NAMES]
    return tuple(flat)


def kernel(x, norm1_g, w_in, conv_a_w, conv_a_b, ln_a_g, ln_a_b, ln_b_g, ln_b_b, w_spatial, b_spatial, conv_c_w, conv_c_b, dt_bias, a_log, d_skip, norm_c_g, w_out, norm2_g, w_ff1, w_ff2, final_g, loss_target, m_norm1_g, m_w_in, m_conv_a_w, m_conv_a_b, m_ln_a_g, m_ln_a_b, m_ln_b_g, m_ln_b_b, m_w_spatial, m_b_spatial, m_conv_c_w, m_conv_c_b, m_dt_bias, m_a_log, m_d_skip, m_norm_c_g, m_w_out, m_norm2_g, m_w_ff1, m_w_ff2, m_final_g, v_norm1_g, v_w_in, v_conv_a_w, v_conv_a_b, v_ln_a_g, v_ln_a_b, v_ln_b_g, v_ln_b_b, v_w_spatial, v_b_spatial, v_conv_c_w, v_conv_c_b, v_dt_bias, v_a_log, v_d_skip, v_norm_c_g, v_w_out, v_norm2_g, v_w_ff1, v_w_ff2, v_final_g):
    p = dict(zip(_NAMES, (norm1_g, w_in, conv_a_w, conv_a_b, ln_a_g, ln_a_b, ln_b_g, ln_b_b, w_spatial, b_spatial, conv_c_w,
                          conv_c_b, dt_bias, a_log, d_skip, norm_c_g, w_out, norm2_g, w_ff1, w_ff2, final_g)))
    m = dict(zip(_NAMES, (m_norm1_g, m_w_in, m_conv_a_w, m_conv_a_b, m_ln_a_g, m_ln_a_b, m_ln_b_g, m_ln_b_b, m_w_spatial,
                          m_b_spatial, m_conv_c_w, m_conv_c_b, m_dt_bias, m_a_log, m_d_skip, m_norm_c_g, m_w_out, m_norm2_g,
                          m_w_ff1, m_w_ff2, m_final_g)))
    v = dict(zip(_NAMES, (v_norm1_g, v_w_in, v_conv_a_w, v_conv_a_b, v_ln_a_g, v_ln_a_b, v_ln_b_g, v_ln_b_b, v_w_spatial,
                          v_b_spatial, v_conv_c_w, v_conv_c_b, v_dt_bias, v_a_log, v_d_skip, v_norm_c_g, v_w_out, v_norm2_g,
                          v_w_ff1, v_w_ff2, v_final_g)))
    return _step(p, m, v, x, loss_target)
```

```python
import functools
import math

import jax
import jax.numpy as jnp
from jax import lax
from jax.experimental import pallas as pl
from jax.experimental.pallas import tpu as pltpu

f32 = jnp.float32
bf16 = jnp.bfloat16
HI = lax.Precision.HIGHEST
EPS = 1e-5
HEAD = 64
CHUNK = 128
KA = 31
KC = 4
N_DEV = 8
VMEM_LIMIT = 56 * 1024 * 1024
MM_VMEM_BUDGET = 52 * 1024 * 1024

ADAM_LR = 0.001
ADAM_B1 = 0.9
ADAM_B2 = 0.999
ADAM_EPS = 1e-08
ADAM_WD = 0.01
ADAM_STEP = 10


def _cparams(sem=None):
    return pltpu.CompilerParams(dimension_semantics=sem, vmem_limit_bytes=VMEM_LIMIT)


def _sds(shape, dtype):
    return jax.ShapeDtypeStruct(shape, dtype)


_DIMS = {"nn": ((1,), (0,)), "nt": ((1,), (1,)), "tn": ((0,), (0,))}


def _dot16(a, b, form):
    return lax.dot_general(a.astype(bf16), b.astype(bf16), (_DIMS[form], ((), ())), preferred_element_type=f32)


@functools.partial(jax.custom_vjp, nondiff_argnums=(2,))
def _bdot(a, b, form):
    return _dot16(a, b, form)


def _bdot_fwd(a, b, form):
    return _dot16(a, b, form), (a, b)


def _bdot_bwd(form, res, ct):
    a, b = res
    if form == "nn":
        da, db = _dot16(ct, b, "nt"), _dot16(a, ct, "tn")
    elif form == "nt":
        da, db = _dot16(ct, b, "nn"), _dot16(ct, a, "tn")
    else:
        da, db = _dot16(b, ct, "nt"), _dot16(a, ct, "nn")
    return da.astype(a.dtype), db.astype(b.dtype)


_bdot.defvjp(_bdot_fwd, _bdot_bwd)


def _tile(n, cap):
    if n <= cap:
        return n
    for d in range(cap - cap % 128, 0, -128):
        if n % d == 0:
            return d
    raise ValueError((n, cap))


def _mm(name, a, b, form, out_dtypes, epilogue=None, extras=(), tm=2048, tn=1024, tk=2048, dep=None, rows=(), n_row_out=0):
    if form == "tn":
        K, M = a.shape
    else:
        M, K = a.shape
    N = b.shape[0] if form == "nt" else b.shape[1]
    tm, tn, tk = _tile(M, tm), _tile(N, tn), _tile(K, tk)
    nk = K // tk

    def vmem_bytes(tm):
        mn = sum(jnp.dtype(e.dtype).itemsize for e in extras) + sum(jnp.dtype(d).itemsize for d in out_dtypes)
        return 2 * (tm * tk * a.dtype.itemsize + tk * tn * b.dtype.itemsize + tm * tn * mn) + 2 * tm * tn * 4

    while vmem_bytes(tm) > MM_VMEM_BUDGET and tm % 256 == 0:
        tm //= 2
    ne, no, nr = len(extras), len(out_dtypes), len(rows)
    assert n_row_out == 0 or tn == N
    deps = () if dep is None else (dep,)
    if epilogue is None:
        epilogue = lambda acc: (acc,)

    def body(a_ref, b_ref, *rest):
        extra_refs, row_refs = rest[:ne], rest[ne:ne + nr]
        rest = rest[ne + nr + len(deps):]
        out_refs, rowout_refs = rest[:no], rest[no:no + n_row_out]
        part = lax.dot_general(a_ref[...].astype(bf16), b_ref[...].astype(bf16),
                               (_DIMS[form], ((), ())), preferred_element_type=f32)

        def finish(acc):
            outs = epilogue(acc, *[e[...] for e in extra_refs], *[r[...] for r in row_refs])
            for o_ref, v in zip(out_refs, outs[:no]):
                o_ref[...] = v.astype(o_ref.dtype)
            for r_ref, v in zip(rowout_refs, outs[no:]):
                @pl.when(pl.program_id(0) == 0)
                def _():
                    r_ref[...] = jnp.zeros_like(r_ref)

                r_ref[0:1, :] += v

        if nk == 1:
            finish(part)
            return
        acc_ref = rest[no + n_row_out]
        k = pl.program_id(2)

        @pl.when(k == 0)
        def _():
            acc_ref[...] = part

        @pl.when((k > 0) & (k < nk - 1))
        def _():
            acc_ref[...] += part

        @pl.when(k == nk - 1)
        def _():
            finish(acc_ref[...] + part)

    a_spec = pl.BlockSpec((tk, tm), lambda i, j, k: (k, i)) if form == "tn" else pl.BlockSpec((tm, tk), lambda i, j, k: (i, k))
    b_spec = pl.BlockSpec((tn, tk), lambda i, j, k: (j, k)) if form == "nt" else pl.BlockSpec((tk, tn), lambda i, j, k: (k, j))
    mn_spec = pl.BlockSpec((tm, tn), lambda i, j, k: (i, j))
    return pl.pallas_call(
        body, name=name, grid=(M // tm, N // tn, nk),
        in_specs=[a_spec, b_spec] + [mn_spec] * ne + [pl.BlockSpec((1, tn), lambda i, j, k: (0, j))] * nr
        + [pl.BlockSpec((8, 128), lambda i, j, k: (0, 0))] * len(deps),
        out_specs=[mn_spec] * no + [pl.BlockSpec((8, tn), lambda i, j, k: (0, j))] * n_row_out,
        out_shape=[_sds((M, N), d) for d in out_dtypes] + [_sds((8, N), f32)] * n_row_out,
        scratch_shapes=[pltpu.VMEM((tm, tn), f32)] if nk > 1 else [],
        compiler_params=_cparams(("arbitrary", "arbitrary", "arbitrary")),
    )(a, b, *extras, *rows, *deps)


def _ep_add(acc, r):
    return (acc + r,)


def _ep_add_rms(acc, r, g):
    x = acc + r
    return x, _rms(x, g)


def _ep_rms_bwd(acc, x, dres, g):
    _, vjp = jax.vjp(_rms, x, g)
    dx, dg = vjp(acc)
    return dres + dx, dg


def _ep_add_rms_bwd(acc, more, x, dres, g):
    return _ep_rms_bwd(acc + more, x, dres, g)


def _ep_relu2(acc):
    r = jnp.maximum(acc, 0.0)
    return acc, r * r


def _ep_drelu2(acc, f):
    return (acc * 2.0 * jnp.maximum(f, 0.0),)


def _rms(x, g):
    return x * lax.rsqrt(jnp.mean(x * x, axis=-1, keepdims=True) + EPS) * g


TT = 512


def _rms_fwd(x, g):
    T, D = x.shape

    def body(x_ref, g_ref, h_ref, tok_ref):
        h_ref[...] = _rms(x_ref[...], g_ref[...]).astype(bf16)
        tok_ref[...] = jnp.zeros_like(tok_ref)

    return pl.pallas_call(
        body, name="rms_fwd", grid=(T // TT,),
        in_specs=[pl.BlockSpec((TT, D), lambda i: (i, 0)), pl.BlockSpec((1, D), lambda i: (0, 0))],
        out_specs=[pl.BlockSpec((TT, D), lambda i: (i, 0)), pl.BlockSpec((8, 128), lambda i: (0, 0))],
        out_shape=[_sds((T, D), bf16), _sds((8, 128), f32)], compiler_params=_cparams(("arbitrary",)),
    )(x, g.reshape(1, D))


def _loss_head(x, g, tgt):
    T, D = x.shape

    def f(xv, gv, tv):
        e = _rms(xv, gv) - tv
        return 0.5 * jnp.sum(jnp.sum(e * e, axis=-1, keepdims=True) * (1.0 / D), axis=0, keepdims=True)

    def body(x_ref, g_ref, t_ref, loss_ref, dx_ref, dg_ref):
        tv = t_ref[...]
        l, vjp = jax.vjp(lambda xv, gv: f(xv, gv, tv), x_ref[...], g_ref[...])
        dx, dg = vjp(jnp.ones((1, 1), f32))
        dx_ref[...] = dx

        @pl.when(pl.program_id(0) == 0)
        def _():
            dg_ref[...] = jnp.zeros_like(dg_ref)
            loss_ref[...] = jnp.zeros_like(loss_ref)

        dg_ref[0:1, :] += dg
        loss_ref[...] += jnp.broadcast_to(l, loss_ref.shape)

    tile = pl.BlockSpec((TT, D), lambda i: (i, 0))
    return pl.pallas_call(
        body, name="loss_head", grid=(T // TT,),
        in_specs=[tile, pl.BlockSpec((1, D), lambda i: (0, 0)), tile],
        out_specs=[pl.BlockSpec((8, 128), lambda i: (0, 0)), tile, pl.BlockSpec((8, D), lambda i: (0, 0))],
        out_shape=[_sds((8, 128), f32), _sds((T, D), f32), _sds((8, D), f32)],
        compiler_params=_cparams(("arbitrary",)),
    )(x, g.reshape(1, D), tgt)


TB = 256


def _glu(a_val, a_gate):
    return a_val * jax.nn.sigmoid(a_gate)


PAIR = 2 * HEAD


def _pair_mean(x, lo):
    s_lo = jnp.sum(jnp.where(lo, x, 0.0), axis=-1, keepdims=True)
    s_hi = jnp.sum(jnp.where(lo, 0.0, x), axis=-1, keepdims=True)
    return jnp.where(lo, s_lo, s_hi) * (1.0 / HEAD)


def _pair_ln(v, g, b):
    lo = lax.broadcasted_iota(jnp.int32, v.shape, 1) < HEAD
    vc = v - _pair_mean(v, lo)
    var = _pair_mean(vc * vc, lo)
    return vc * lax.rsqrt(var + EPS) * g + b


def _ln_silu(v, g, b):
    return jax.nn.silu(_pair_ln(v, g, b))


def _conv_geom(kw):
    halo = 32 if kw > 9 else 16
    return halo, halo - (kw - 1)


def _residues(shifts):
    return sorted({s % 8 for s in shifts} - {0})


def _shift_copies(src_ref, cp_ref, res, rows, ls):
    for j, r in enumerate(res):
        cp_ref[j, :, ls] = src_ref[pl.ds(r, rows), ls]


def _shifted(src_ref, cp_ref, res, shift, size, ls):
    r = shift % 8
    if r == 0:
        return src_ref[pl.ds(shift, size), ls]
    return cp_ref[res.index(r), pl.ds(shift - r, size), ls]


def _conv_taps(hp_ref, hs_ref, w_ref, b_ref, acc_ref, kw, off, halo, width):
    res = _residues(range(off, off + kw))
    for c in range(width // 128):
        ls = pl.ds(c * 128, 128)
        _shift_copies(hp_ref, hs_ref, res, halo + TB, ls)
        acc = jnp.broadcast_to(b_ref[:, ls], (TB, 128))
        for k in range(kw):
            acc = acc + w_ref[k:k + 1, ls] * _shifted(hp_ref, hs_ref, res, off + k, TB, ls)
        acc_ref[:, ls] = acc


def _conv_fwd(name, src, col_block, w, b, kw, conformer, n_seq, ln_g=None, ln_b=None, share=None):
    T = src.shape[0]
    cout = w.shape[1]
    cin = 2 * cout if conformer else cout
    halo, off = _conv_geom(kw)
    nblk = T // n_seq // TB
    hb = TB // halo
    out_shape, out_blk, shared_in, aliases = _shared(share, T, cout, bf16 if conformer else f32, 6 if conformer else 4)

    def body(cur_ref, halo_ref, w_ref, b_ref, *rest):
        if conformer:
            g_ref, lb_ref = rest[:2]
            rest = rest[2:]
        out_ref, hp_ref, acc_ref, hs_ref = rest[len(shared_in):]
        i = pl.program_id(1)
        first = (i == 0)

        @pl.when((pl.program_id(0) == 0) & first)
        def _():
            hp_ref[pl.ds(halo + TB, 8), :] = jnp.zeros((8, cout), f32)

        if conformer:
            hp_ref[pl.ds(halo, TB), :] = _glu(cur_ref[:, 0:cout].astype(f32), cur_ref[:, cout:cin].astype(f32))
            hh = _glu(halo_ref[:, 0:cout].astype(f32), halo_ref[:, cout:cin].astype(f32))
        else:
            hp_ref[pl.ds(halo, TB), :] = cur_ref[...].astype(f32)
            hh = halo_ref[...].astype(f32)
        hp_ref[pl.ds(0, halo), :] = jnp.where(first, 0.0, hh)
        _conv_taps(hp_ref, hs_ref, w_ref, b_ref, acc_ref, kw, off, halo, cout)
        if conformer:
            for q in range(cout // PAIR):
                ls = pl.ds(q * PAIR, PAIR)
                out_ref[:, ls] = _ln_silu(acc_ref[:, ls], g_ref[:, ls], lb_ref[:, ls]).astype(out_ref.dtype)
        else:
            out_ref[...] = jax.nn.silu(acc_ref[...]).astype(out_ref.dtype)

    nres = len(_residues(range(off, off + kw)))

    row = pl.BlockSpec((1, cout), lambda s, i: (0, 0))
    in_specs = [pl.BlockSpec((TB, cin), lambda s, i: (s * nblk + i, col_block)),
                pl.BlockSpec((halo, cin), lambda s, i: (jnp.maximum((s * nblk + i) * hb - 1, 0), col_block)),
                pl.BlockSpec((w.shape[0], cout), lambda s, i: (0, 0)), row]
    args = [src, src, w, b.reshape(1, cout)]
    if conformer:
        in_specs += [row, row]
        args += [ln_g.reshape(1, cout), ln_b.reshape(1, cout)]
    in_specs += [pl.BlockSpec(memory_space=pl.ANY)] * len(shared_in)
    args += shared_in
    return pl.pallas_call(
        body, name=name, grid=(n_seq, nblk), in_specs=in_specs,
        out_specs=pl.BlockSpec((TB, cout), lambda s, i: (s * nblk + i, out_blk)),
        out_shape=out_shape, input_output_aliases=aliases,
        scratch_shapes=[pltpu.VMEM((halo + TB + 8, cout), f32), pltpu.VMEM((TB, cout), f32),
                        pltpu.VMEM((nres, halo + TB, cout), f32)],
        compiler_params=_cparams(("arbitrary", "arbitrary")),
    )(*args)


def _shared(share, T, width, dtype, n_inputs, out_index=0):
    if share is None:
        return _sds((T, width), dtype), 0, [], {}
    total, blk, into = share
    if into is None:
        return _sds((T, total), dtype), blk, [], {}
    return _sds((T, total), dtype), blk, [into], {n_inputs: out_index}


def _conv_bwd(name, src, col_block, w, b, dy, dy_col_block, kw, conformer, n_seq, ln_g=None, ln_b=None, share=None):
    T = src.shape[0]
    cout = w.shape[1]
    wrows = w.shape[0]
    cin = 2 * cout if conformer else cout
    halo, off = _conv_geom(kw)
    nblk = T // n_seq // TB
    hb = TB // halo
    dsrc_shape, dsrc_blk, shared_in, aliases = _shared(share, T, cin, bf16, 7 if conformer else 5)

    def body(cur_ref, halo_ref, w_ref, b_ref, dy_ref, *rest):
        if conformer:
            g_ref, lb_ref = rest[:2]
            rest = rest[2:]
        rest = rest[len(shared_in):]
        if conformer:
            dsrc_ref, dw_ref, db_ref, dg_ref, dlb_ref, hp_ref, acc_ref, dz_ref, dhp_ref, carry_ref, hs_ref, dzs_ref = rest
        else:
            dsrc_ref, dw_ref, db_ref, hp_ref, acc_ref, dz_ref, dhp_ref, carry_ref, hs_ref, dzs_ref = rest
        s, ii = pl.program_id(0), pl.program_id(1)
        i = nblk - 1 - ii
        first = (i == 0)

        @pl.when((s == 0) & (ii == 0))
        def _():
            dw_ref[...] = jnp.zeros_like(dw_ref)
            db_ref[...] = jnp.zeros_like(db_ref)
            hp_ref[pl.ds(halo + TB, 8), :] = jnp.zeros((8, cout), f32)
            if conformer:
                dg_ref[...] = jnp.zeros_like(dg_ref)
                dlb_ref[...] = jnp.zeros_like(dlb_ref)

        @pl.when(ii == 0)
        def _():
            carry_ref[...] = jnp.zeros_like(carry_ref)
            dz_ref[pl.ds(0, halo), :] = jnp.zeros((halo, cout), f32)
            dz_ref[pl.ds(halo + TB, halo), :] = jnp.zeros((halo, cout), f32)

        if conformer:
            hp_ref[pl.ds(halo, TB), :] = _glu(cur_ref[:, 0:cout].astype(f32), cur_ref[:, cout:cin].astype(f32))
            hh = _glu(halo_ref[:, 0:cout].astype(f32), halo_ref[:, cout:cin].astype(f32))
        else:
            hp_ref[pl.ds(halo, TB), :] = cur_ref[...].astype(f32)
            hh = halo_ref[...].astype(f32)
        hp_ref[pl.ds(0, halo), :] = jnp.where(first, 0.0, hh)
        _conv_taps(hp_ref, hs_ref, w_ref, b_ref, acc_ref, kw, off, halo, cout)

        if conformer:
            for q in range(cout // PAIR):
                ls = pl.ds(q * PAIR, PAIR)
                _, vjp = jax.vjp(_ln_silu, acc_ref[:, ls], g_ref[:, ls], lb_ref[:, ls])
                da, dg, dlb = vjp(dy_ref[:, ls].astype(f32))
                dz_ref[pl.ds(halo, TB), ls] = da
                dg_ref[0:1, ls] += dg
                dlb_ref[0:1, ls] += dlb
        else:
            _, vjp = jax.vjp(jax.nn.silu, acc_ref[...])
            dz_ref[pl.ds(halo, TB), :] = vjp(dy_ref[...].astype(f32))[0]

        res_h = _residues(range(off, off + kw))
        res_z = _residues(range(kw))
        for c in range(cout // 128):
            ls = pl.ds(c * 128, 128)
            _shift_copies(dz_ref, dzs_ref, res_z, halo + TB + halo - 8, ls)
            dacc = dz_ref[pl.ds(halo, TB), ls]
            db_ref[0:1, ls] += jnp.sum(dacc, axis=0, keepdims=True)
            dhp = jnp.zeros((halo + TB, 128), f32)
            for k in range(kw):
                dw_ref[k:k + 1, ls] += jnp.sum(dacc * _shifted(hp_ref, hs_ref, res_h, off + k, TB, ls), axis=0, keepdims=True)
                dhp = dhp + w_ref[k:k + 1, ls] * _shifted(dz_ref, dzs_ref, res_z, kw - 1 - k, halo + TB, ls)
            dhp_ref[:, ls] = dhp
        dhp_ref[pl.ds(TB, halo), :] += carry_ref[...]
        carry_ref[...] = dhp_ref[pl.ds(0, halo), :]
        dcur = dhp_ref[pl.ds(halo, TB), :]
        if conformer:
            _, vjp = jax.vjp(_glu, cur_ref[:, 0:cout].astype(f32), cur_ref[:, cout:cin].astype(f32))
            dval, dgate = vjp(dcur)
            dsrc_ref[:, 0:cout] = dval.astype(dsrc_ref.dtype)
            dsrc_ref[:, cout:cin] = dgate.astype(dsrc_ref.dtype)
        else:
            dsrc_ref[...] = dcur.astype(dsrc_ref.dtype)

    def blk(s, ii):
        return s * nblk + (nblk - 1 - ii)

    row = pl.BlockSpec((1, cout), lambda s, ii: (0, 0))
    acc8 = pl.BlockSpec((8, cout), lambda s, ii: (0, 0))
    in_specs = [pl.BlockSpec((TB, cin), lambda s, ii: (blk(s, ii), col_block)),
                pl.BlockSpec((halo, cin), lambda s, ii: (jnp.maximum(blk(s, ii) * hb - 1, 0), col_block)),
                pl.BlockSpec((wrows, cout), lambda s, ii: (0, 0)), row,
                pl.BlockSpec((TB, cout), lambda s, ii: (blk(s, ii), dy_col_block))]
    args = [src, src, w, b.reshape(1, cout), dy]
    out_specs = [pl.BlockSpec((TB, cin), lambda s, ii: (blk(s, ii), dsrc_blk)),
                 pl.BlockSpec((wrows, cout), lambda s, ii: (0, 0)), acc8]
    out_shape = [dsrc_shape, _sds((wrows, cout), f32), _sds((8, cout), f32)]
    if conformer:
        in_specs += [row, row]
        args += [ln_g.reshape(1, cout), ln_b.reshape(1, cout)]
        out_specs += [acc8, acc8]
        out_shape += [_sds((8, cout), f32), _sds((8, cout), f32)]
    in_specs += [pl.BlockSpec(memory_space=pl.ANY)] * len(shared_in)
    args += shared_in
    return pl.pallas_call(
        body, name=name, grid=(n_seq, nblk), in_specs=in_specs, out_specs=out_specs, out_shape=out_shape,
        input_output_aliases=aliases,
        scratch_shapes=[pltpu.VMEM((halo + TB + 8, cout), f32), pltpu.VMEM((TB, cout), f32),
                        pltpu.VMEM((halo + TB + halo, cout), f32), pltpu.VMEM((halo + TB, cout), f32),
                        pltpu.VMEM((halo, cout), f32),
                        pltpu.VMEM((len(_residues(range(off, off + kw))), halo + TB, cout), f32),
                        pltpu.VMEM((len(_residues(range(kw))), halo + TB + halo - 8, cout), f32)],
        compiler_params=_cparams(("arbitrary", "arbitrary")),
    )(*args)


def _gelu(x):
    return 0.5 * x * (1.0 + lax.erf(x * (1.0 / math.sqrt(2.0))))


def _tril_mask(n):
    r = lax.broadcasted_iota(jnp.int32, (n, n), 0)
    c = lax.broadcasted_iota(jnp.int32, (n, n), 1)
    return r >= c


def _head_spread(nh):
    r = lax.broadcasted_iota(jnp.int32, (nh, nh * HEAD), 0)
    c = lax.broadcasted_iota(jnp.int32, (nh, nh * HEAD), 1)
    return (c // HEAD == r).astype(f32)


def _gmlp_bias(bs):
    return lax.dot_general(bs, _head_spread(bs.shape[0]), (((0,), (0,)), ((), ())), precision=HI, preferred_element_type=f32)


def _gmlp_pair(bu, bv, g, b, w_a, w_b, bias):
    lo = lax.broadcasted_iota(jnp.int32, bu.shape, 1) < HEAD
    tril = _tril_mask(CHUNK)
    u = _gelu(bu)
    vn = _pair_ln(_gelu(bv), g, b)
    mix = jnp.where(lo, _bdot(jnp.where(tril, w_a, 0.0), vn, "nn"), _bdot(jnp.where(tril, w_b, 0.0), vn, "nn"))
    return u * (mix + bias)


def _gmlp_fwd(proj, col_block, ln_g, ln_b, w_s, b_s, share=None):
    T = proj.shape[0]
    nh = w_s.shape[0]
    width = nh * HEAD
    out_shape, out_blk, shared_in, aliases = _shared(share, T, width, bf16, 5)

    def body(p_ref, g_ref, b_ref, w_ref, bs_ref, *rest):
        out_ref, bias_ref = rest[len(shared_in):]

        @pl.when(pl.program_id(0) == 0)
        def _():
            bias_ref[...] = _gmlp_bias(bs_ref[...])

        for q in range(nh // 2):
            ls = pl.ds(q * PAIR, PAIR)
            lv = pl.ds(width + q * PAIR, PAIR)
            out_ref[:, ls] = _gmlp_pair(p_ref[:, ls].astype(f32), p_ref[:, lv].astype(f32), g_ref[:, ls], b_ref[:, ls], w_ref[2 * q], w_ref[2 * q + 1],
                                        bias_ref[:, ls]).astype(out_ref.dtype)

    row = pl.BlockSpec((1, width), lambda i: (0, 0))
    return pl.pallas_call(
        body, name="gmlp_fwd", grid=(T // CHUNK,),
        in_specs=[pl.BlockSpec((CHUNK, 2 * width), lambda i: (i, col_block)), row, row,
                  pl.BlockSpec((nh, CHUNK, CHUNK), lambda i: (0, 0, 0)), pl.BlockSpec((nh, CHUNK), lambda i: (0, 0))]
        + [pl.BlockSpec(memory_space=pl.ANY)] * len(shared_in),
        out_specs=pl.BlockSpec((CHUNK, width), lambda i: (i, out_blk)),
        out_shape=out_shape, input_output_aliases=aliases, scratch_shapes=[pltpu.VMEM((CHUNK, width), f32)],
        compiler_params=_cparams(("arbitrary",)),
    )(proj, ln_g.reshape(1, width), ln_b.reshape(1, width), w_s, b_s, *shared_in)


def _gmlp_bwd(proj, col_block, ln_g, ln_b, w_s, b_s, dy, dy_col_block, share=None):
    T = proj.shape[0]
    nh = w_s.shape[0]
    width = nh * HEAD
    nstep = T // CHUNK
    dp_shape, dp_blk, shared_in, aliases = _shared(share, T, 2 * width, bf16, 6)

    def body(p_ref, g_ref, b_ref, w_ref, bs_ref, dy_ref, *rest):
        dp_ref, dg_ref, db_ref, dw_ref, dbst_ref, bias_ref, dbias_ref = rest[len(shared_in):]

        @pl.when(pl.program_id(0) == 0)
        def _():
            dg_ref[...] = jnp.zeros_like(dg_ref)
            db_ref[...] = jnp.zeros_like(db_ref)
            dw_ref[...] = jnp.zeros_like(dw_ref)
            dbias_ref[...] = jnp.zeros_like(dbias_ref)
            bias_ref[...] = _gmlp_bias(bs_ref[...])

        for q in range(nh // 2):
            ls = pl.ds(q * PAIR, PAIR)
            lv = pl.ds(width + q * PAIR, PAIR)
            _, vjp = jax.vjp(_gmlp_pair, p_ref[:, ls].astype(f32), p_ref[:, lv].astype(f32), g_ref[:, ls], b_ref[:, ls], w_ref[2 * q], w_ref[2 * q + 1],
                             bias_ref[:, ls])
            dbu, dbv, dg, db, dw_a, dw_b, dbias = vjp(dy_ref[:, ls].astype(f32))
            dp_ref[:, ls] = dbu.astype(dp_ref.dtype)
            dp_ref[:, lv] = dbv.astype(dp_ref.dtype)
            dg_ref[0:1, ls] += dg
            db_ref[0:1, ls] += db
            dw_ref[2 * q] += dw_a
            dw_ref[2 * q + 1] += dw_b
            dbias_ref[:, ls] += dbias

        @pl.when(pl.program_id(0) == nstep - 1)
        def _():
            dbst_ref[...] = lax.dot_general(dbias_ref[...], _head_spread(nh), (((1,), (1,)), ((), ())),
                                            precision=HI, preferred_element_type=f32)

    row = pl.BlockSpec((1, width), lambda i: (0, 0))
    acc8 = pl.BlockSpec((8, width), lambda i: (0, 0))
    wspec = pl.BlockSpec((nh, CHUNK, CHUNK), lambda i: (0, 0, 0))
    res = pl.pallas_call(
        body, name="gmlp_bwd", grid=(nstep,),
        in_specs=[pl.BlockSpec((CHUNK, 2 * width), lambda i: (i, col_block)), row, row, wspec,
                  pl.BlockSpec((nh, CHUNK), lambda i: (0, 0)), pl.BlockSpec((CHUNK, width), lambda i: (i, dy_col_block))]
        + [pl.BlockSpec(memory_space=pl.ANY)] * len(shared_in),
        out_specs=[pl.BlockSpec((CHUNK, 2 * width), lambda i: (i, dp_blk)), acc8, acc8, wspec,
                   pl.BlockSpec((CHUNK, nh), lambda i: (0, 0))],
        out_shape=[dp_shape, _sds((8, width), f32), _sds((8, width), f32),
                   _sds((nh, CHUNK, CHUNK), f32), _sds((CHUNK, nh), f32)],
        input_output_aliases=aliases,
        scratch_shapes=[pltpu.VMEM((CHUNK, width), f32), pltpu.VMEM((CHUNK, width), f32)],
        compiler_params=_cparams(("arbitrary",)),
    )(proj, ln_g.reshape(1, width), ln_b.reshape(1, width), w_s, b_s, dy, *shared_in)
    return res[0], res[1], res[2], res[3], res[4].T


def _sel_col(x, h):
    lane = lax.broadcasted_iota(jnp.int32, x.shape, 1)
    return jnp.sum(jnp.where(lane == h, x, 0.0), axis=1, keepdims=True)


def _sel_row(x, h):
    sub = lax.broadcasted_iota(jnp.int32, x.shape, 0)
    return jnp.sum(jnp.where(sub == h, x, 0.0), axis=0, keepdims=True)


def _ssd_chunk(nh, ngrp, xs_l, z_l, b_l, c_l, dtraw, dtb, alog, dskip, ng_l, prev_l):
    hg = nh // ngrp
    tril = _tril_mask(CHUNK)
    tl = tril.astype(f32)
    lo = lax.broadcasted_iota(jnp.int32, (CHUNK, PAIR), 1) < HEAD
    lo_row = lo[0:1, :]
    dt = jax.nn.softplus(dtraw + dtb)
    a = dt * (-jnp.exp(alog))
    cs = jnp.dot(tl, a, precision=HI, preferred_element_type=f32)
    cst = lax.dot_general(a, tl, (((0,), (1,)), ((), ())), precision=HI, preferred_element_type=f32)
    cb_l = [_bdot(c_l[g], b_l[g], "nt") for g in range(ngrp)]
    yz_l, new_prev = [], []
    for q in range(nh // 2):
        g = (2 * q) // hg
        cols = []
        for h in (2 * q, 2 * q + 1):
            cs_h = _sel_col(cs, h)
            tot = _sel_row(cs_h, CHUNK - 1)
            seg = jnp.where(tril, cs_h - _sel_row(cst, h), 0.0)
            lmat = jnp.where(tril, jnp.exp(seg), 0.0)
            cols.append((_sel_col(dt, h), cs_h, tot, lmat, _sel_col(dskip, h)))
        (dt_a, cs_a, tot_a, l_a, dsk_a), (dt_b, cs_b, tot_b, l_b, dsk_b) = cols
        xs = xs_l[q]
        x = xs * jnp.where(lo, dt_a, dt_b)
        ydiag = jnp.where(lo, _bdot(cb_l[g] * l_a, x, "nn"), _bdot(cb_l[g] * l_b, x, "nn"))
        yoff = _bdot(c_l[g], prev_l[q], "nn") * jnp.where(lo, jnp.exp(cs_a), jnp.exp(cs_b))
        xdec = x * jnp.where(lo, jnp.exp(tot_a - cs_a), jnp.exp(tot_b - cs_b))
        st = _bdot(b_l[g], xdec, "tn")
        new_prev.append(prev_l[q] * jnp.where(lo_row, jnp.exp(tot_a), jnp.exp(tot_b)) + st)
        y = ydiag + yoff + jnp.where(lo_row, dsk_a, dsk_b) * xs
        yz_l.append(y * jax.nn.silu(z_l[q]))
    out = [None] * (nh // 2)
    qg = hg // 2
    for g in range(ngrp):
        ssq = sum(jnp.sum(yz_l[q] * yz_l[q], axis=-1, keepdims=True) for q in range(g * qg, (g + 1) * qg))
        r = lax.rsqrt(ssq * (1.0 / (hg * HEAD)) + EPS)
        for q in range(g * qg, (g + 1) * qg):
            out[q] = yz_l[q] * r * ng_l[q]
    return out, new_prev


def _ssd_read(nh, ngrp, nst, xbc_ref, z_ref, ng_ref, st_ref):
    cw = nh * HEAD
    xs_l = [xbc_ref[:, pl.ds(q * PAIR, PAIR)] for q in range(nh // 2)]
    b_l = [xbc_ref[:, pl.ds(cw + g * nst, nst)] for g in range(ngrp)]
    c_l = [xbc_ref[:, pl.ds(cw + ngrp * nst + g * nst, nst)] for g in range(ngrp)]
    z_l = [z_ref[:, pl.ds(q * PAIR, PAIR)].astype(f32) for q in range(nh // 2)]
    ng_l = [ng_ref[:, pl.ds(q * PAIR, PAIR)] for q in range(nh // 2)]
    prev_l = [st_ref[:, pl.ds(q * PAIR, PAIR)] for q in range(nh // 2)]
    return xs_l, z_l, b_l, c_l, ng_l, prev_l


def _ssd_fwd(xbc, proj, z_col_block, pdt, dtb, alog, dskip, ng, nh, ngrp, nst, n_seq, share=None):
    T = xbc.shape[0]
    cw = nh * HEAD
    nchunk = T // n_seq // CHUNK
    assert nst == CHUNK
    y_shape, y_blk, shared_in, aliases = _shared(share, T, cw, bf16, 7)

    def body(xbc_ref, z_ref, dt_ref, dtb_ref, alog_ref, dskip_ref, ng_ref, *rest):
        y_ref, sin_ref, st_ref = rest[len(shared_in):]

        @pl.when(pl.program_id(1) == 0)
        def _():
            st_ref[...] = jnp.zeros_like(st_ref)

        sin_ref[...] = st_ref[...]
        xs_l, z_l, b_l, c_l, ng_l, prev_l = _ssd_read(nh, ngrp, nst, xbc_ref, z_ref, ng_ref, st_ref)
        y_l, new_prev = _ssd_chunk(nh, ngrp, xs_l, z_l, b_l, c_l, dt_ref[...], dtb_ref[...], alog_ref[...],
                                   dskip_ref[...], ng_l, prev_l)
        for q in range(nh // 2):
            ls = pl.ds(q * PAIR, PAIR)
            y_ref[:, ls] = y_l[q].astype(y_ref.dtype)
            st_ref[:, ls] = new_prev[q]

    def blk(s, c):
        return s * nchunk + c

    prow = pl.BlockSpec((1, 128), lambda s, c: (0, 0))
    return pl.pallas_call(
        body, name="ssd_fwd", grid=(n_seq, nchunk),
        in_specs=[pl.BlockSpec((CHUNK, xbc.shape[1]), lambda s, c: (blk(s, c), 0)),
                  pl.BlockSpec((CHUNK, cw), lambda s, c: (blk(s, c), z_col_block)),
                  pl.BlockSpec((CHUNK, 128), lambda s, c: (blk(s, c), 0)),
                  prow, prow, prow, pl.BlockSpec((1, cw), lambda s, c: (0, 0))]
        + [pl.BlockSpec(memory_space=pl.ANY)] * len(shared_in),
        out_specs=[pl.BlockSpec((CHUNK, cw), lambda s, c: (blk(s, c), y_blk)),
                   pl.BlockSpec((nst, cw), lambda s, c: (blk(s, c), 0))],
        out_shape=[y_shape, _sds((T, cw), f32)], input_output_aliases=aliases,
        scratch_shapes=[pltpu.VMEM((nst, cw), f32)],
        compiler_params=_cparams(("arbitrary", "arbitrary")),
    )(xbc, proj, pdt, dtb, alog, dskip, ng.reshape(1, cw), *shared_in)


def _ssd_bwd(xbc, proj, z_col_block, pdt, dtb, alog, dskip, ng, sin, dy, dy_col_block, nh, ngrp, nst, n_seq, share=None):
    T, xw = xbc.shape
    cw = nh * HEAD
    nchunk = T // n_seq // CHUNK
    dz_shape, dz_blk, shared_in, aliases = _shared(share, T, cw, bf16, 9, out_index=1)

    def body(xbc_ref, z_ref, dt_ref, dtb_ref, alog_ref, dskip_ref, ng_ref, sin_ref, dy_ref, *rest):
        dxbc_ref, dz_ref, ddt_ref, ddtb_ref, dalog_ref, ddskip_ref, dng_ref, dst_ref = rest[len(shared_in):]
        s, cc = pl.program_id(0), pl.program_id(1)

        @pl.when((s == 0) & (cc == 0))
        def _():
            ddtb_ref[...] = jnp.zeros_like(ddtb_ref)
            dalog_ref[...] = jnp.zeros_like(dalog_ref)
            ddskip_ref[...] = jnp.zeros_like(ddskip_ref)
            dng_ref[...] = jnp.zeros_like(dng_ref)

        @pl.when(cc == 0)
        def _():
            dst_ref[...] = jnp.zeros_like(dst_ref)

        xs_l, z_l, b_l, c_l, ng_l, prev_l = _ssd_read(nh, ngrp, nst, xbc_ref, z_ref, ng_ref, sin_ref)
        _, vjp = jax.vjp(functools.partial(_ssd_chunk, nh, ngrp), xs_l, z_l, b_l, c_l, dt_ref[...], dtb_ref[...],
                         alog_ref[...], dskip_ref[...], ng_l, prev_l)
        dy_l = [dy_ref[:, pl.ds(q * PAIR, PAIR)].astype(f32) for q in range(nh // 2)]
        dst_l = [dst_ref[:, pl.ds(q * PAIR, PAIR)] for q in range(nh // 2)]
        dxs_l, dz_l, db_l, dc_l, ddt, ddtb, dalog, ddskip, dng_l, dprev_l = vjp((dy_l, dst_l))
        for q in range(nh // 2):
            ls = pl.ds(q * PAIR, PAIR)
            dxbc_ref[:, ls] = dxs_l[q]
            dz_ref[:, ls] = dz_l[q].astype(dz_ref.dtype)
            dng_ref[0:1, ls] += dng_l[q]
            dst_ref[:, ls] = dprev_l[q]
        for g in range(ngrp):
            dxbc_ref[:, pl.ds(cw + g * nst, nst)] = db_l[g]
            dxbc_ref[:, pl.ds(cw + ngrp * nst + g * nst, nst)] = dc_l[g]
        ddt_ref[...] = ddt.astype(ddt_ref.dtype)
        ddtb_ref[0:1, :] += ddtb
        dalog_ref[0:1, :] += dalog
        ddskip_ref[0:1, :] += ddskip

    def blk(s, cc):
        return s * nchunk + (nchunk - 1 - cc)

    prow = pl.BlockSpec((1, 128), lambda s, c: (0, 0))
    pacc = pl.BlockSpec((8, 128), lambda s, c: (0, 0))
    return pl.pallas_call(
        body, name="ssd_bwd", grid=(n_seq, nchunk),
        in_specs=[pl.BlockSpec((CHUNK, xw), lambda s, c: (blk(s, c), 0)),
                  pl.BlockSpec((CHUNK, cw), lambda s, c: (blk(s, c), z_col_block)),
                  pl.BlockSpec((CHUNK, 128), lambda s, c: (blk(s, c), 0)),
                  prow, prow, prow, pl.BlockSpec((1, cw), lambda s, c: (0, 0)),
                  pl.BlockSpec((nst, cw), lambda s, c: (blk(s, c), 0)),
                  pl.BlockSpec((CHUNK, cw), lambda s, c: (blk(s, c), dy_col_block))]
        + [pl.BlockSpec(memory_space=pl.ANY)] * len(shared_in),
        out_specs=[pl.BlockSpec((CHUNK, xw), lambda s, c: (blk(s, c), 0)),
                   pl.BlockSpec((CHUNK, cw), lambda s, c: (blk(s, c), dz_blk)),
                   pl.BlockSpec((CHUNK, 128), lambda s, c: (blk(s, c), 0)),
                   pacc, pacc, pacc, pl.BlockSpec((8, cw), lambda s, c: (0, 0))],
        out_shape=[_sds((T, xw), f32), dz_shape, _sds((T, 128), bf16),
                   _sds((8, 128), f32), _sds((8, 128), f32), _sds((8, 128), f32), _sds((8, cw), f32)],
        input_output_aliases=aliases,
        scratch_shapes=[pltpu.VMEM((nst, cw), f32)],
        compiler_params=_cparams(("arbitrary", "arbitrary")),
    )(xbc, proj, pdt, dtb, alog, dskip, ng.reshape(1, cw), sin, dy, *shared_in)


_HBM = pl.BlockSpec(memory_space=pltpu.HBM)
_SEM = pl.BlockSpec(memory_space=pltpu.SEMAPHORE)
_EFFECT = pltpu.SideEffectType.DATAFLOW_SIDE_EFFECTING


def _split_copies(n, scatter, src_refs, land_refs, send_sems, recv_sems):
    npeer = N_DEV - 1
    x, y, c = lax.axis_index("x"), lax.axis_index("y"), lax.axis_index("c")
    me = 4 * x + 2 * y + c
    copies = []
    for i in range(n):
        for k in range(1, N_DEV):
            px = 1 - x if k & 4 else x
            py = 1 - y if k & 2 else y
            pc = 1 - c if k & 1 else c
            src = src_refs[i].at[4 * px + 2 * py + pc] if scatter else src_refs[i]
            copies.append(pltpu.make_async_remote_copy(
                src_ref=src, dst_ref=land_refs[i].at[me],
                send_sem=send_sems.at[i * npeer + k - 1], recv_sem=recv_sems.at[i * npeer + k - 1],
                device_id=(px, py, pc), device_id_type=pl.DeviceIdType.MESH))
    return copies


def _exchange_start(name, arrs, scatter):
    n = len(arrs)
    nsem = n * (N_DEV - 1)
    me = 4 * lax.axis_index("x") + 2 * lax.axis_index("y") + lax.axis_index("c")
    lands = []
    for a in arrs:
        own = lax.dynamic_index_in_dim(a, me, 0, keepdims=True) if scatter else a[None]
        full = lax.empty(a.shape if scatter else (N_DEV,) + a.shape, a.dtype)
        lands.append(lax.dynamic_update_slice(full, own, (me,) + (0,) * (full.ndim - 1)))

    def body(*refs):
        src_refs, land_refs = refs[:n], refs[n:2 * n]
        send_sems, recv_sems = refs[2 * n], refs[2 * n + 1]
        token = refs[-1]
        for cp in _split_copies(n, scatter, src_refs, land_refs, send_sems, recv_sems):
            cp.start()
        token[...] = jnp.zeros_like(token)

    res = pl.pallas_call(
        body, name=name,
        out_shape=(pltpu.SemaphoreType.DMA((nsem,)), pltpu.SemaphoreType.DMA((nsem,)),
                   *[pltpu.HBM(a.shape, a.dtype) for a in arrs], *[pltpu.HBM(l.shape, l.dtype) for l in lands],
                   _sds((8, 128), f32)),
        in_specs=[_HBM] * (2 * n),
        out_specs=(_SEM, _SEM, *[_HBM] * (2 * n), pl.BlockSpec(memory_space=pltpu.VMEM)),
        input_output_aliases={j: 2 + j for j in range(2 * n)},
        compiler_params=pltpu.CompilerParams(has_side_effects=_EFFECT),
    )(*[pltpu.with_memory_space_constraint(a, pltpu.HBM) for a in arrs],
      *[pltpu.with_memory_space_constraint(l, pltpu.HBM) for l in lands])
    return (n, scatter, res[0], res[1], res[2:2 + n], res[2 + n:2 + 2 * n]), res[-1]


def _exchange_wait(name, handle, after):
    n, scatter, send_sems, recv_sems, srcs, lands = handle
    after = list(after) if isinstance(after, (list, tuple)) else [after]

    def body(*refs):
        src_refs, land_refs = refs[:n], refs[n:2 * n]
        for cp in _split_copies(n, scatter, src_refs, land_refs, refs[2 * n], refs[2 * n + 1]):
            cp.wait_send()
            cp.wait_recv()

    res = pl.pallas_call(
        body, name=name,
        out_shape=[pltpu.HBM(a.shape, a.dtype) for a in (*srcs, *lands)],
        in_specs=[_HBM] * (2 * n) + [_SEM, _SEM] + [pl.BlockSpec(memory_space=pl.ANY)] * len(after),
        out_specs=[_HBM] * (2 * n),
        input_output_aliases={j: j for j in range(2 * n)},
        compiler_params=pltpu.CompilerParams(has_side_effects=_EFFECT),
    )(*srcs, *lands, send_sems, recv_sems, *after)
    return res[n:]


def _adam_tiles(R, C):
    if R % 256 == 0:
        return (256, C), (R // 256, 1)
    assert C % 128 == 0
    return (R, 128), (1, C // 128)


def _adam(name, parts, w, m, v, layer=None, depth=None, into=None, stacked_in=False):
    P, R, C = parts.shape
    (tr, tc), (gr, gc) = _adam_tiles(R, C)
    c1 = 1.0 / (1.0 - ADAM_B1 ** ADAM_STEP)
    c2 = 1.0 / (1.0 - ADAM_B2 ** ADAM_STEP)
    into = [] if into is None else list(into)

    def body(p_ref, w_ref, m_ref, v_ref, *rest):
        g_ref, d_ref, nm_ref, nv_ref = rest[len(into):]
        g = p_ref[0].astype(f32)
        for s in range(1, P):
            g = g + p_ref[s].astype(f32)
        nm = ADAM_B1 * m_ref[...] + (1.0 - ADAM_B1) * g
        nv = ADAM_B2 * v_ref[...] + (1.0 - ADAM_B2) * (g * g)
        g_ref[...] = g
        nm_ref[...] = nm
        nv_ref[...] = nv
        d_ref[...] = -ADAM_LR * ((nm * c1) / (jnp.sqrt(nv * c2) + ADAM_EPS) + ADAM_WD * w_ref[...])

    tile = pl.BlockSpec((tr, tc), lambda i, j: (i, j))
    layer_tile = pl.BlockSpec((None, tr, tc), lambda i, j: (layer, i, j))
    out_tile, out_sds = (tile, _sds((R, C), f32)) if layer is None else (layer_tile, _sds((depth, R, C), f32))
    return pl.pallas_call(
        body, name=name, grid=(gr, gc),
        in_specs=[pl.BlockSpec((P, tr, tc), lambda i, j: (0, i, j))] + [layer_tile if stacked_in else tile] * 3
        + [pl.BlockSpec(memory_space=pl.ANY)] * len(into),
        out_specs=[out_tile] * 4, out_shape=[out_sds] * 4,
        input_output_aliases={4 + k: k for k in range(len(into))},
        compiler_params=_cparams(("arbitrary", "arbitrary")),
    )(parts, w, m, v, *into)


def _sum_parts(name, parts):
    P, R, C = parts.shape
    tr = 256 if R % 256 == 0 else R

    def body(p_ref, o_ref):
        g = p_ref[0]
        for s in range(1, P):
            g = g + p_ref[s]
        o_ref[...] = g

    return pl.pallas_call(
        body, name=name, grid=(R // tr,),
        in_specs=[pl.BlockSpec((P, tr, C), lambda i: (0, i, 0))], out_specs=pl.BlockSpec((tr, C), lambda i: (i, 0)),
        out_shape=_sds((R, C), f32), compiler_params=_cparams(("arbitrary",)),
    )(parts)


def _pad_to(a, n, axis):
    if a.shape[axis] == n:
        return a
    cfg = [(0, 0)] * a.ndim
    cfg[axis] = (0, n - a.shape[axis])
    return jnp.pad(a, cfg)


def _pack(arrs):
    flat = [_pad_to(a.reshape(-1), -(-a.size // 128) * 128, 0) for a in arrs]
    rows = jnp.concatenate(flat).reshape(-1, 128)
    return _pad_to(rows, -(-rows.shape[0] // 256) * 256, 0)


def _unpack(slab, shapes):
    flat = slab.reshape(-1)
    out, o = [], 0
    for s in shapes:
        n = math.prod(s)
        out.append(flat[o:o + n].reshape(s))
        o += -(-n // 128) * 128
    return out


_NAMES = ['norm1_g', 'w_in', 'conv_a_w', 'conv_a_b', 'ln_a_g', 'ln_a_b', 'ln_b_g', 'ln_b_b', 'w_spatial', 'b_spatial',
          'conv_c_w', 'conv_c_b', 'dt_bias', 'a_log', 'd_skip', 'norm_c_g', 'w_out', 'norm2_g', 'w_ff1', 'w_ff2', 'final_g']
_REPL = ['norm1_g', 'conv_a_b', 'ln_a_g', 'ln_a_b', 'ln_b_g', 'ln_b_b', 'w_spatial', 'b_spatial', 'conv_c_b',
         'dt_bias', 'a_log', 'd_skip', 'norm_c_g', 'norm2_g']
_CONVW = ['conv_a_w', 'conv_c_w']
_BIG = ['w_in', 'w_out', 'w_ff1', 'w_ff2']
_BIG_T = {'w_in': True, 'w_out': False, 'w_ff1': True, 'w_ff2': False}


def _row128(v):
    return _pad_to(v.reshape(1, -1), 128, 1)


def _step(p, m, v, x, loss_target):
    nb, S, D = x.shape
    T = nb * S
    depth = p['norm1_g'].shape[0]
    a_w = p['conv_a_b'].shape[1]
    b_w = p['ln_b_g'].shape[1]
    nh = p['dt_bias'].shape[1]
    c_w = p['norm_c_g'].shape[1]
    xw = p['conv_c_b'].shape[1]
    ngrp = 2
    nst = (xw - c_w) // (2 * ngrp)
    d_in = p['w_in'].shape[2] * N_DEV
    main = d_in - nh
    assert main == 2 * a_w + 2 * b_w + c_w + xw and 2 * a_w == 2 * b_w == c_w and xw % c_w == c_w // 2
    me = 4 * lax.axis_index("x") + 2 * lax.axis_index("y") + lax.axis_index("c")

    x2 = x.reshape(T, D)
    tgt = loss_target.reshape(T, D)

    def shards(i, z=None):
        z = 0.0 if z is None else z
        return [(p['w_in'][i].T + z).astype(bf16), (p['w_out'][i] + z).astype(bf16), (p['w_ff1'][i].T + z).astype(bf16),
                (p['w_ff2'][i] + z).astype(bf16), p['conv_a_w'][i], p['conv_c_w'][i]]

    def gathered_in(wt, ca, cc):
        wt = wt.reshape(d_in, D)
        ca = jnp.transpose(ca, (1, 0, 2)).reshape(KA, a_w)
        cc = jnp.transpose(cc, (1, 0, 2)).reshape(KC, xw)
        return dict(wt_main=wt[:main], wt_dt=_pad_to(wt[main:], 128, 0), ca=_pad_to(ca, 32, 0), cc=_pad_to(cc, 8, 0))

    def start_layer(i, after=None):
        sh = shards(i, None if after is None else after[0, 0])
        ha, t = _exchange_start("gather_w%da_start" % i, [sh[0], sh[4], sh[5]], False)
        hb, t = _exchange_start("gather_w%db_start" % i, [shards(i, t[0, 0])[1]], False)
        hc, t = _exchange_start("gather_w%dc_start" % i, [shards(i, t[0, 0])[2]], False)
        hd, t = _exchange_start("gather_w%dd_start" % i, [shards(i, t[0, 0])[3]], False)
        return dict(a=ha, b=hb, c=hc, d=hd), t

    W, saved = [], []
    xc = x2
    H, tok = start_layer(0)
    for i in range(depth):
        w = gathered_in(*_exchange_wait("gather_w%da_wait" % i, H['a'], [xc, tok]))
        W.append(w)
        Hi = H
        h1, rtok = _rms_fwd(xc, p['norm1_g'][i])
        if i + 1 < depth:
            H, tok = start_layer(i + 1, rtok + tok)
        else:
            tok = None
        (proj,) = _mm("mm_proj", h1, w['wt_main'], "nt", [bf16], dep=tok)
        (pdt,) = _mm("mm_pdt", h1, w['wt_dt'], "nt", [f32])
        mixw = a_w + b_w + c_w
        ycat = _conv_fwd("confa_fwd", proj, 0, w['ca'], p['conv_a_b'][i], KA, True, nb, p['ln_a_g'][i], p['ln_a_b'][i],
                         share=(mixw, 0, None))
        ycat = _gmlp_fwd(proj, 1, p['ln_b_g'][i], p['ln_b_b'][i], p['w_spatial'][i], p['b_spatial'][i], share=(mixw, 1, ycat))
        xbc = _conv_fwd("convc_fwd", proj, 2, w['cc'], p['conv_c_b'][i], KC, False, nb)
        dtb, alog, dsk = _row128(p['dt_bias'][i]), _row128(p['a_log'][i]), _row128(p['d_skip'][i])
        ycat, sin = _ssd_fwd(xbc, proj, 2, pdt, dtb, alog, dsk, p['norm_c_g'][i], nh, ngrp, nst, nb, share=(mixw, 1, ycat))
        w['wout'] = _exchange_wait("gather_w%db_wait" % i, Hi['b'], ycat)[0].reshape(-1, D)
        xm, h2 = _mm("mm_out", ycat, w['wout'], "nn", [f32, bf16], _ep_add_rms, (xc,), rows=(p['norm2_g'][i].reshape(1, D),))
        w['w1t'] = _exchange_wait("gather_w%dc_wait" % i, Hi['c'], h2)[0].reshape(-1, D)
        f, a = _mm("mm_ff1", h2, w['w1t'], "nt", [bf16, bf16], _ep_relu2)
        w['w2'] = _exchange_wait("gather_w%dd_wait" % i, Hi['d'], a)[0].reshape(-1, D)
        (xo,) = _mm("mm_ff2", a, w['w2'], "nn", [f32], _ep_add, (xm,))
        saved.append(dict(x_in=xc, h1=h1, proj=proj, pdt=pdt, xbc=xbc, sin=sin, ycat=ycat, xm=xm, h2=h2, f=f, a=a,
                          dtb=dtb, alog=alog, dsk=dsk))
        xc = xo

    lp, dx, dfinal = _loss_head(xc, p['final_g'], tgt)
    loss = lax.psum(lp[0, 0], ("x", "y", "c"))

    out = {}
    kinds = ("grad", "delta", "new_m", "new_v")
    names1 = _REPL + _CONVW

    started, small = [], [None] * depth

    def send(n, i, g):
        handle, token = _exchange_start("scatter_%s_%d_start" % (n, i), [g.reshape(N_DEV, -1, D)], True)
        started.append((n, i, handle))
        return token

    tok = None
    for i in reversed(range(depth)):
        w, sv = W[i], saved[i]
        (df,) = _mm("mm_df", dx, w['w2'], "nt", [bf16], _ep_drelu2, (sv['f'],), dep=tok)
        (gw2,) = _mm("mm_gw2", sv['a'], dx, "tn", [bf16])
        tok = send('w_ff2', i, gw2)
        dxm, dg2 = _mm("mm_dh2", df, w['w1t'], "nn", [f32], _ep_rms_bwd, (sv['xm'], dx), dep=tok,
                       rows=(p['norm2_g'][i].reshape(1, D),), n_row_out=1)
        (gw1t,) = _mm("mm_gw1", df, sv['h2'], "tn", [bf16])
        tok = send('w_ff1', i, gw1t)
        (dycat,) = _mm("mm_dycat", dxm, w['wout'], "nt", [bf16], dep=tok)
        (gwout,) = _mm("mm_gwout", sv['ycat'], dxm, "tn", [bf16])
        tok = send('w_out', i, gwout)
        dproj, dwa, dba, dlag, dlab = _conv_bwd("confa_bwd", sv['proj'], 0, w['ca'], p['conv_a_b'][i] + tok[0, 0], dycat, 0, KA,
                                                True, nb, p['ln_a_g'][i], p['ln_a_b'][i], share=(main, 0, None))
        dproj, dlbg, dlbb, dws, dbs = _gmlp_bwd(sv['proj'], 1, p['ln_b_g'][i], p['ln_b_b'][i], p['w_spatial'][i],
                                                p['b_spatial'][i], dycat, 1, share=(main, 1, dproj))
        dxbc, dproj, ddt, ddtb, dalog, ddsk, dng = _ssd_bwd(sv['xbc'], sv['proj'], 2, sv['pdt'], sv['dtb'], sv['alog'],
                                                            sv['dsk'], p['norm_c_g'][i], sv['sin'], dycat, 1, nh, ngrp, nst, nb,
                                                            share=(main, 2, dproj))
        dproj, dwc, dbc = _conv_bwd("convc_bwd", sv['proj'], 2, w['cc'], p['conv_c_b'][i], dxbc, 0, KC, False, nb,
                                    share=(main, 2, dproj))
        (dh_main,) = _mm("mm_dh1", dproj, w['wt_main'], "nn", [f32])
        (gwt_main,) = _mm("mm_gwin", dproj, sv['h1'], "tn", [bf16])
        (gwt_dt,) = _mm("mm_gwdt", ddt, sv['h1'], "tn", [bf16])
        tok = send('w_in', i, jnp.concatenate([gwt_main, gwt_dt[:nh]], axis=0))
        dx, dg1 = _mm("mm_dh1dt", ddt, w['wt_dt'], "nn", [f32], _ep_add_rms_bwd, (dh_main, sv['x_in'], dxm), dep=tok,
                      rows=(p['norm1_g'][i].reshape(1, D),), n_row_out=1)

        gi = dict(norm1_g=dg1[0], norm2_g=dg2[0], conv_a_w=dwa[:KA], conv_a_b=dba[0], ln_a_g=dlag[0], ln_a_b=dlab[0],
                  ln_b_g=dlbg[0], ln_b_b=dlbb[0], w_spatial=dws, b_spatial=dbs, conv_c_w=dwc[:KC], conv_c_b=dbc[0],
                  dt_bias=ddtb[0, :nh], a_log=dalog[0, :nh], d_skip=ddsk[0, :nh], norm_c_g=dng[0])
        parts_i = [gi[n] for n in names1] + ([dfinal[0]] if i == depth - 1 else [])
        handle, tok = _exchange_start("gather_g%d_start" % i, [_pack(parts_i)], False)
        small[i] = ([a.shape for a in parts_i], handle)
    grad_x = dx.reshape(nb, S, D)

    dep = [dx, tok]
    for n, i, handle in started:
        (parts,) = _exchange_wait("scatter_%s_%d_wait" % (n, i), handle, dep)
        prev = [out[(kind, n)] for kind in kinds] if (kinds[0], n) in out else None
        if _BIG_T[n]:
            res = _adam("adam_%s_%d" % (n, i), parts, p[n][i].T, m[n][i].T, v[n][i].T, layer=i, depth=depth, into=prev)
        else:
            res = _adam("adam_%s_%d" % (n, i), parts, p[n], m[n], v[n], layer=i, depth=depth, into=prev, stacked_in=True)
        for kind, r in zip(kinds, res):
            out[(kind, n)] = r
        dep = res[3]

    gsum = [None] * depth
    for i in reversed(range(depth)):
        (parts,) = _exchange_wait("gather_g%d_wait" % i, small[i][1], dep)
        gsum[i] = _sum_parts("sum_small", parts)
        dep = gsum[i]
    widths = [-(-math.prod(p[n].shape[1:]) // 128) * 128 for n in _REPL]
    rep_rows = sum(widths) // 128
    tot_rows = -(-depth * rep_rows // 256) * 256

    def rep_slab(q):
        cols = [_pad_to(q[n].reshape(depth, -1), wd, 1) for n, wd in zip(_REPL, widths)]
        return _pad_to(jnp.concatenate(cols, axis=1).reshape(-1, 128), tot_rows, 0)

    g_rep = _pad_to(jnp.concatenate([g[:rep_rows] for g in gsum], axis=0), tot_rows, 0)
    res = _adam("adam_small", g_rep[None], rep_slab(p), rep_slab(m), rep_slab(v))
    for kind, r in zip(kinds, res):
        view = r[:depth * rep_rows].reshape(depth, -1)
        o = 0
        for n, wd in zip(_REPL, widths):
            out[(kind, n)] = view[:, o:o + math.prod(p[n].shape[1:])].reshape(p[n].shape)
            o += wd

    extra = []
    for i in range(depth):
        tail = _unpack(gsum[i][rep_rows:], small[i][0][len(_REPL):])
        extra.append(tail)
    gconv = []
    for j, n in enumerate(_CONVW):
        cw_shard = p[n].shape[2]
        full = jnp.stack([extra[i][j] for i in range(depth)])
        gconv.append(lax.dynamic_slice_in_dim(full, me * cw_shard, cw_shard, axis=2))
    tail_names = _CONVW + ['final_g']
    res = _adam("adam_conv", _pack(gconv + [extra[depth - 1][len(_CONVW)]])[None],
                *[_pack([q[n] for n in tail_names]) for q in (p, m, v)])
    for kind, r in zip(kinds, res):
        for n, arr in zip(tail_names, _unpack(r, [p[n].shape for n in tail_names])):
            out[(kind, n)] = arr
    for n in _BIG:
        if _BIG_T[n]:
            for kind in kinds:
                out[(kind, n)] = jnp.swapaxes(out[(kind, n)], 1, 2)

    flat = [loss, grad_x]
    for kind in ("grad", "delta", "new_m", "new_v"):
        flat += [out[(kind, n)] for n in _NAMES]
    return tuple(flat)


def kernel(x, norm1_g, w_in, conv_a_w, conv_a_b, ln_a_g, ln_a_b, ln_b_g, ln_b_b, w_spatial, b_spatial, conv_c_w, conv_c_b, dt_bias, a_log, d_skip, norm_c_g, w_out, norm2_g, w_ff1, w_ff2, final_g, loss_target, m_norm1_g, m_w_in, m_conv_a_w, m_conv_a_b, m_ln_a_g, m_ln_a_b, m_ln_b_g, m_ln_b_b, m_w_spatial, m_b_spatial, m_conv_c_w, m_conv_c_b, m_dt_bias, m_a_log, m_d_skip, m_norm_c_g, m_w_out, m_norm2_g, m_w_ff1, m_w_ff2, m_final_g, v_norm1_g, v_w_in, v_conv_a_w, v_conv_a_b, v_ln_a_g, v_ln_a_b, v_ln_b_g, v_ln_b_b, v_w_spatial, v_b_spatial, v_conv_c_w, v_conv_c_b, v_dt_bias, v_a_log, v_d_skip, v_norm_c_g, v_w_out, v_norm2_g, v_w_ff1, v_w_ff2, v_final_g):
    p = dict(zip(_NAMES, (norm1_g, w_in, conv_a_w, conv_a_b, ln_a_g, ln_a_b, ln_b_g, ln_b_b, w_spatial, b_spatial, conv_c_w,
                          conv_c_b, dt_bias, a_log, d_skip, norm_c_g, w_out, norm2_g, w_ff1, w_ff2, final_g)))
    m = dict(zip(_NAMES, (m_norm1_g, m_w_in, m_conv_a_w, m_conv_a_b, m_ln_a_g, m_ln_a_b, m_ln_b_g, m_ln_b_b, m_w_spatial,
                          m_b_spatial, m_conv_c_w, m_conv_c_b, m_dt_bias, m_a_log, m_d_skip, m_norm_c_g, m_w_out, m_norm2_g,
                          m_w_ff1, m_w_ff2, m_final_g)))
    v = dict(zip(_NAMES, (v_norm1_g, v_w_in, v_conv_a_w, v_conv_a_b, v_ln_a_g, v_ln_a_b, v_ln_b_g, v_ln_b_b, v_w_spatial,
                          v_b_spatial, v_conv_c_w, v_conv_c_b, v_dt_bias, v_a_log, v_d_skip, v_norm_c_g, v_w_out, v_norm2_g,
                          v_w_ff1, v_w_ff2, v_final_g)))
    return _step(p, m, v, x, loss_target)
```

```python
import functools
import math

import jax
import jax.numpy as jnp
from jax import lax
from jax.experimental import pallas as pl
from jax.experimental.pallas import tpu as pltpu

f32 = jnp.float32
bf16 = jnp.bfloat16
HI = lax.Precision.HIGHEST
EPS = 1e-5
HEAD = 64
CHUNK = 128
KA = 31
KC = 4
N_DEV = 8
VMEM_LIMIT = 56 * 1024 * 1024
MM_VMEM_BUDGET = 52 * 1024 * 1024

ADAM_LR = 0.001
ADAM_B1 = 0.9
ADAM_B2 = 0.999
ADAM_EPS = 1e-08
ADAM_WD = 0.01
ADAM_STEP = 10


def _cparams(sem=None):
    return pltpu.CompilerParams(dimension_semantics=sem, vmem_limit_bytes=VMEM_LIMIT)


def _sds(shape, dtype):
    return jax.ShapeDtypeStruct(shape, dtype)


_DIMS = {"nn": ((1,), (0,)), "nt": ((1,), (1,)), "tn": ((0,), (0,))}


def _dot16(a, b, form):
    return lax.dot_general(a.astype(bf16), b.astype(bf16), (_DIMS[form], ((), ())), preferred_element_type=f32)


@functools.partial(jax.custom_vjp, nondiff_argnums=(2,))
def _bdot(a, b, form):
    return _dot16(a, b, form)


def _bdot_fwd(a, b, form):
    return _dot16(a, b, form), (a, b)


def _bdot_bwd(form, res, ct):
    a, b = res
    if form == "nn":
        da, db = _dot16(ct, b, "nt"), _dot16(a, ct, "tn")
    elif form == "nt":
        da, db = _dot16(ct, b, "nn"), _dot16(ct, a, "tn")
    else:
        da, db = _dot16(b, ct, "nt"), _dot16(a, ct, "nn")
    return da.astype(a.dtype), db.astype(b.dtype)


_bdot.defvjp(_bdot_fwd, _bdot_bwd)


def _tile(n, cap):
    if n <= cap:
        return n
    for d in range(cap - cap % 128, 0, -128):
        if n % d == 0:
            return d
    raise ValueError((n, cap))


def _mm(name, a, b, form, out_dtypes, epilogue=None, extras=(), tm=2048, tn=1024, tk=2048, dep=None, rows=(), n_row_out=0):
    if form == "tn":
        K, M = a.shape
    else:
        M, K = a.shape
    N = b.shape[0] if form == "nt" else b.shape[1]
    tm, tn, tk = _tile(M, tm), _tile(N, tn), _tile(K, tk)
    nk = K // tk

    def vmem_bytes(tm):
        mn = sum(jnp.dtype(e.dtype).itemsize for e in extras) + sum(jnp.dtype(d).itemsize for d in out_dtypes)
        return 2 * (tm * tk * a.dtype.itemsize + tk * tn * b.dtype.itemsize + tm * tn * mn) + 2 * tm * tn * 4

    while vmem_bytes(tm) > MM_VMEM_BUDGET and tm % 256 == 0:
        tm //= 2
    ne, no, nr = len(extras), len(out_dtypes), len(rows)
    assert n_row_out == 0 or tn == N
    deps = () if dep is None else (dep,)
    if epilogue is None:
        epilogue = lambda acc: (acc,)

    def body(a_ref, b_ref, *rest):
        extra_refs, row_refs = rest[:ne], rest[ne:ne + nr]
        rest = rest[ne + nr + len(deps):]
        out_refs, rowout_refs = rest[:no], rest[no:no + n_row_out]
        part = lax.dot_general(a_ref[...].astype(bf16), b_ref[...].astype(bf16),
                               (_DIMS[form], ((), ())), preferred_element_type=f32)

        def finish(acc):
            outs = epilogue(acc, *[e[...] for e in extra_refs], *[r[...] for r in row_refs])
            for o_ref, v in zip(out_refs, outs[:no]):
                o_ref[...] = v.astype(o_ref.dtype)
            for r_ref, v in zip(rowout_refs, outs[no:]):
                @pl.when(pl.program_id(0) == 0)
                def _():
                    r_ref[...] = jnp.zeros_like(r_ref)

                r_ref[0:1, :] += v

        if nk == 1:
            finish(part)
            return
        acc_ref = rest[no + n_row_out]
        k = pl.program_id(2)

        @pl.when(k == 0)
        def _():
            acc_ref[...] = part

        @pl.when((k > 0) & (k < nk - 1))
        def _():
            acc_ref[...] += part

        @pl.when(k == nk - 1)
        def _():
            finish(acc_ref[...] + part)

    a_spec = pl.BlockSpec((tk, tm), lambda i, j, k: (k, i)) if form == "tn" else pl.BlockSpec((tm, tk), lambda i, j, k: (i, k))
    b_spec = pl.BlockSpec((tn, tk), lambda i, j, k: (j, k)) if form == "nt" else pl.BlockSpec((tk, tn), lambda i, j, k: (k, j))
    mn_spec = pl.BlockSpec((tm, tn), lambda i, j, k: (i, j))
    return pl.pallas_call(
        body, name=name, grid=(M // tm, N // tn, nk),
        in_specs=[a_spec, b_spec] + [mn_spec] * ne + [pl.BlockSpec((1, tn), lambda i, j, k: (0, j))] * nr
        + [pl.BlockSpec((8, 128), lambda i, j, k: (0, 0))] * len(deps),
        out_specs=[mn_spec] * no + [pl.BlockSpec((8, tn), lambda i, j, k: (0, j))] * n_row_out,
        out_shape=[_sds((M, N), d) for d in out_dtypes] + [_sds((8, N), f32)] * n_row_out,
        scratch_shapes=[pltpu.VMEM((tm, tn), f32)] if nk > 1 else [],
        compiler_params=_cparams(("arbitrary", "arbitrary", "arbitrary")),
    )(a, b, *extras, *rows, *deps)


def _ep_add(acc, r):
    return (acc + r,)


def _ep_add_rms(acc, r, g):
    x = acc + r
    return x, _rms(x, g)


def _ep_rms_bwd(acc, x, dres, g):
    _, vjp = jax.vjp(_rms, x, g)
    dx, dg = vjp(acc)
    return dres + dx, dg


def _ep_add_rms_bwd(acc, more, x, dres, g):
    return _ep_rms_bwd(acc + more, x, dres, g)


def _ep_relu2(acc):
    r = jnp.maximum(acc, 0.0)
    return acc, r * r


def _ep_drelu2(acc, f):
    return (acc * 2.0 * jnp.maximum(f, 0.0),)


def _rms(x, g):
    return x * lax.rsqrt(jnp.mean(x * x, axis=-1, keepdims=True) + EPS) * g


TT = 512


def _rms_fwd(x, g):
    T, D = x.shape

    def body(x_ref, g_ref, h_ref, tok_ref):
        h_ref[...] = _rms(x_ref[...], g_ref[...]).astype(bf16)
        tok_ref[...] = jnp.zeros_like(tok_ref)

    return pl.pallas_call(
        body, name="rms_fwd", grid=(T // TT,),
        in_specs=[pl.BlockSpec((TT, D), lambda i: (i, 0)), pl.BlockSpec((1, D), lambda i: (0, 0))],
        out_specs=[pl.BlockSpec((TT, D), lambda i: (i, 0)), pl.BlockSpec((8, 128), lambda i: (0, 0))],
        out_shape=[_sds((T, D), bf16), _sds((8, 128), f32)], compiler_params=_cparams(("arbitrary",)),
    )(x, g.reshape(1, D))


def _loss_head(x, g, tgt):
    T, D = x.shape

    def f(xv, gv, tv):
        e = _rms(xv, gv) - tv
        return 0.5 * jnp.sum(jnp.sum(e * e, axis=-1, keepdims=True) * (1.0 / D), axis=0, keepdims=True)

    def body(x_ref, g_ref, t_ref, loss_ref, dx_ref, dg_ref):
        tv = t_ref[...]
        l, vjp = jax.vjp(lambda xv, gv: f(xv, gv, tv), x_ref[...], g_ref[...])
        dx, dg = vjp(jnp.ones((1, 1), f32))
        dx_ref[...] = dx

        @pl.when(pl.program_id(0) == 0)
        def _():
            dg_ref[...] = jnp.zeros_like(dg_ref)
            loss_ref[...] = jnp.zeros_like(loss_ref)

        dg_ref[0:1, :] += dg
        loss_ref[...] += jnp.broadcast_to(l, loss_ref.shape)

    tile = pl.BlockSpec((TT, D), lambda i: (i, 0))
    return pl.pallas_call(
        body, name="loss_head", grid=(T // TT,),
        in_specs=[tile, pl.BlockSpec((1, D), lambda i: (0, 0)), tile],
        out_specs=[pl.BlockSpec((8, 128), lambda i: (0, 0)), tile, pl.BlockSpec((8, D), lambda i: (0, 0))],
        out_shape=[_sds((8, 128), f32), _sds((T, D), f32), _sds((8, D), f32)],
        compiler_params=_cparams(("arbitrary",)),
    )(x, g.reshape(1, D), tgt)


TB = 256


def _glu(a_val, a_gate):
    return a_val * jax.nn.sigmoid(a_gate)


PAIR = 2 * HEAD


def _pair_mean(x, lo):
    s_lo = jnp.sum(jnp.where(lo, x, 0.0), axis=-1, keepdims=True)
    s_hi = jnp.sum(jnp.where(lo, 0.0, x), axis=-1, keepdims=True)
    return jnp.where(lo, s_lo, s_hi) * (1.0 / HEAD)


def _pair_ln(v, g, b):
    lo = lax.broadcasted_iota(jnp.int32, v.shape, 1) < HEAD
    vc = v - _pair_mean(v, lo)
    var = _pair_mean(vc * vc, lo)
    return vc * lax.rsqrt(var + EPS) * g + b


def _ln_silu(v, g, b):
    return jax.nn.silu(_pair_ln(v, g, b))


def _conv_geom(kw):
    halo = 32 if kw > 9 else 16
    return halo, halo - (kw - 1)


def _residues(shifts):
    return sorted({s % 8 for s in shifts} - {0})


def _shift_copies(src_ref, cp_ref, res, rows, ls):
    for j, r in enumerate(res):
        cp_ref[j, :, ls] = src_ref[pl.ds(r, rows), ls]


def _shifted(src_ref, cp_ref, res, shift, size, ls):
    r = shift % 8
    if r == 0:
        return src_ref[pl.ds(shift, size), ls]
    return cp_ref[res.index(r), pl.ds(shift - r, size), ls]


def _conv_taps(hp_ref, hs_ref, w_ref, b_ref, acc_ref, kw, off, halo, width):
    res = _residues(range(off, off + kw))
    for c in range(width // 128):
        ls = pl.ds(c * 128, 128)
        _shift_copies(hp_ref, hs_ref, res, halo + TB, ls)
        acc = jnp.broadcast_to(b_ref[:, ls], (TB, 128))
        for k in range(kw):
            acc = acc + w_ref[k:k + 1, ls] * _shifted(hp_ref, hs_ref, res, off + k, TB, ls)
        acc_ref[:, ls] = acc


def _conv_fwd(name, src, col_block, w, b, kw, conformer, n_seq, ln_g=None, ln_b=None, share=None):
    T = src.shape[0]
    cout = w.shape[1]
    cin = 2 * cout if conformer else cout
    halo, off = _conv_geom(kw)
    nblk = T // n_seq // TB
    hb = TB // halo
    out_shape, out_blk, shared_in, aliases = _shared(share, T, cout, bf16 if conformer else f32, 6 if conformer else 4)

    def body(cur_ref, halo_ref, w_ref, b_ref, *rest):
        if conformer:
            g_ref, lb_ref = rest[:2]
            out_ref, acc_ref, hp_ref, hs_ref = rest[2 + len(shared_in):]
        else:
            out_ref, hp_ref, acc_ref, hs_ref = rest[len(shared_in):]
        i = pl.program_id(1)
        first = (i == 0)

        @pl.when((pl.program_id(0) == 0) & first)
        def _():
            hp_ref[pl.ds(halo + TB, 8), :] = jnp.zeros((8, cout), f32)

        if conformer:
            hp_ref[pl.ds(halo, TB), :] = _glu(cur_ref[:, 0:cout].astype(f32), cur_ref[:, cout:cin].astype(f32))
            hh = _glu(halo_ref[:, 0:cout].astype(f32), halo_ref[:, cout:cin].astype(f32))
        else:
            hp_ref[pl.ds(halo, TB), :] = cur_ref[...].astype(f32)
            hh = halo_ref[...].astype(f32)
        hp_ref[pl.ds(0, halo), :] = jnp.where(first, 0.0, hh)
        _conv_taps(hp_ref, hs_ref, w_ref, b_ref, acc_ref, kw, off, halo, cout)
        if conformer:
            for q in range(cout // PAIR):
                ls = pl.ds(q * PAIR, PAIR)
                out_ref[:, ls] = _ln_silu(acc_ref[:, ls], g_ref[:, ls], lb_ref[:, ls]).astype(out_ref.dtype)
        else:
            out_ref[...] = jax.nn.silu(acc_ref[...]).astype(out_ref.dtype)

    nres = len(_residues(range(off, off + kw)))

    row = pl.BlockSpec((1, cout), lambda s, i: (0, 0))
    in_specs = [pl.BlockSpec((TB, cin), lambda s, i: (s * nblk + i, col_block)),
                pl.BlockSpec((halo, cin), lambda s, i: (jnp.maximum((s * nblk + i) * hb - 1, 0), col_block)),
                pl.BlockSpec((w.shape[0], cout), lambda s, i: (0, 0)), row]
    args = [src, src, w, b.reshape(1, cout)]
    if conformer:
        in_specs += [row, row]
        args += [ln_g.reshape(1, cout), ln_b.reshape(1, cout)]
    in_specs += [pl.BlockSpec(memory_space=pl.ANY)] * len(shared_in)
    args += shared_in
    out_specs = [pl.BlockSpec((TB, cout), lambda s, i: (s * nblk + i, out_blk))]
    out_shapes = [out_shape]
    scratch = [pltpu.VMEM((halo + TB + 8, cout), f32), pltpu.VMEM((nres, halo + TB, cout), f32)]
    if conformer:
        out_specs.append(pl.BlockSpec((TB, cout), lambda s, i: (s * nblk + i, 0)))
        out_shapes.append(_sds((T, cout), f32))
    else:
        scratch.insert(1, pltpu.VMEM((TB, cout), f32))
    res = pl.pallas_call(
        body, name=name, grid=(n_seq, nblk), in_specs=in_specs, out_specs=out_specs, out_shape=out_shapes,
        input_output_aliases=aliases, scratch_shapes=scratch,
        compiler_params=_cparams(("arbitrary", "arbitrary")),
    )(*args)
    return res if conformer else res[0]


def _shared(share, T, width, dtype, n_inputs, out_index=0):
    if share is None:
        return _sds((T, width), dtype), 0, [], {}
    total, blk, into = share
    if into is None:
        return _sds((T, total), dtype), blk, [], {}
    return _sds((T, total), dtype), blk, [into], {n_inputs: out_index}


def _conv_bwd(name, src, col_block, w, b, dy, dy_col_block, kw, conformer, n_seq, ln_g=None, ln_b=None, share=None, acc=None):
    T = src.shape[0]
    cout = w.shape[1]
    wrows = w.shape[0]
    cin = 2 * cout if conformer else cout
    halo, off = _conv_geom(kw)
    nblk = T // n_seq // TB
    hb = TB // halo
    saved = [] if acc is None else [acc]
    dsrc_shape, dsrc_blk, shared_in, aliases = _shared(share, T, cin, bf16, (7 if conformer else 5) + len(saved))

    def body(cur_ref, halo_ref, w_ref, b_ref, dy_ref, *rest):
        if conformer:
            g_ref, lb_ref = rest[:2]
            rest = rest[2:]
        saved_ref = rest[0] if saved else None
        rest = rest[len(saved) + len(shared_in):]
        if conformer:
            dsrc_ref, dw_ref, db_ref, dg_ref, dlb_ref, hp_ref, acc_ref, dz_ref, dhp_ref, carry_ref, hs_ref, dzs_ref = rest
        else:
            dsrc_ref, dw_ref, db_ref, hp_ref, acc_ref, dz_ref, dhp_ref, carry_ref, hs_ref, dzs_ref = rest
        s, ii = pl.program_id(0), pl.program_id(1)
        i = nblk - 1 - ii
        first = (i == 0)

        @pl.when((s == 0) & (ii == 0))
        def _():
            dw_ref[...] = jnp.zeros_like(dw_ref)
            db_ref[...] = jnp.zeros_like(db_ref)
            hp_ref[pl.ds(halo + TB, 8), :] = jnp.zeros((8, cout), f32)
            if conformer:
                dg_ref[...] = jnp.zeros_like(dg_ref)
                dlb_ref[...] = jnp.zeros_like(dlb_ref)

        @pl.when(ii == 0)
        def _():
            carry_ref[...] = jnp.zeros_like(carry_ref)
            dz_ref[pl.ds(0, halo), :] = jnp.zeros((halo, cout), f32)
            dz_ref[pl.ds(halo + TB, halo), :] = jnp.zeros((halo, cout), f32)

        if conformer:
            hp_ref[pl.ds(halo, TB), :] = _glu(cur_ref[:, 0:cout].astype(f32), cur_ref[:, cout:cin].astype(f32))
            hh = _glu(halo_ref[:, 0:cout].astype(f32), halo_ref[:, cout:cin].astype(f32))
        else:
            hp_ref[pl.ds(halo, TB), :] = cur_ref[...].astype(f32)
            hh = halo_ref[...].astype(f32)
        hp_ref[pl.ds(0, halo), :] = jnp.where(first, 0.0, hh)
        if saved:
            acc_ref = saved_ref
        else:
            _conv_taps(hp_ref, hs_ref, w_ref, b_ref, acc_ref, kw, off, halo, cout)

        if conformer:
            for q in range(cout // PAIR):
                ls = pl.ds(q * PAIR, PAIR)
                _, vjp = jax.vjp(_ln_silu, acc_ref[:, ls], g_ref[:, ls], lb_ref[:, ls])
                da, dg, dlb = vjp(dy_ref[:, ls].astype(f32))
                dz_ref[pl.ds(halo, TB), ls] = da
                dg_ref[0:1, ls] += dg
                dlb_ref[0:1, ls] += dlb
        else:
            _, vjp = jax.vjp(jax.nn.silu, acc_ref[...])
            dz_ref[pl.ds(halo, TB), :] = vjp(dy_ref[...].astype(f32))[0]

        res_h = _residues(range(off, off + kw))
        res_z = _residues(range(kw))
        for c in range(cout // 128):
            ls = pl.ds(c * 128, 128)
            _shift_copies(dz_ref, dzs_ref, res_z, halo + TB + halo - 8, ls)
            if saved:
                _shift_copies(hp_ref, hs_ref, res_h, halo + TB, ls)
            dacc = dz_ref[pl.ds(halo, TB), ls]
            db_ref[0:1, ls] += jnp.sum(dacc, axis=0, keepdims=True)
            dhp = jnp.zeros((halo + TB, 128), f32)
            for k in range(kw):
                dw_ref[k:k + 1, ls] += jnp.sum(dacc * _shifted(hp_ref, hs_ref, res_h, off + k, TB, ls), axis=0, keepdims=True)
                dhp = dhp + w_ref[k:k + 1, ls] * _shifted(dz_ref, dzs_ref, res_z, kw - 1 - k, halo + TB, ls)
            dhp_ref[:, ls] = dhp
        dhp_ref[pl.ds(TB, halo), :] += carry_ref[...]
        carry_ref[...] = dhp_ref[pl.ds(0, halo), :]
        dcur = dhp_ref[pl.ds(halo, TB), :]
        if conformer:
            _, vjp = jax.vjp(_glu, cur_ref[:, 0:cout].astype(f32), cur_ref[:, cout:cin].astype(f32))
            dval, dgate = vjp(dcur)
            dsrc_ref[:, 0:cout] = dval.astype(dsrc_ref.dtype)
            dsrc_ref[:, cout:cin] = dgate.astype(dsrc_ref.dtype)
        else:
            dsrc_ref[...] = dcur.astype(dsrc_ref.dtype)

    def blk(s, ii):
        return s * nblk + (nblk - 1 - ii)

    row = pl.BlockSpec((1, cout), lambda s, ii: (0, 0))
    acc8 = pl.BlockSpec((8, cout), lambda s, ii: (0, 0))
    in_specs = [pl.BlockSpec((TB, cin), lambda s, ii: (blk(s, ii), col_block)),
                pl.BlockSpec((halo, cin), lambda s, ii: (jnp.maximum(blk(s, ii) * hb - 1, 0), col_block)),
                pl.BlockSpec((wrows, cout), lambda s, ii: (0, 0)), row,
                pl.BlockSpec((TB, cout), lambda s, ii: (blk(s, ii), dy_col_block))]
    args = [src, src, w, b.reshape(1, cout), dy]
    out_specs = [pl.BlockSpec((TB, cin), lambda s, ii: (blk(s, ii), dsrc_blk)),
                 pl.BlockSpec((wrows, cout), lambda s, ii: (0, 0)), acc8]
    out_shape = [dsrc_shape, _sds((wrows, cout), f32), _sds((8, cout), f32)]
    if conformer:
        in_specs += [row, row]
        args += [ln_g.reshape(1, cout), ln_b.reshape(1, cout)]
        out_specs += [acc8, acc8]
        out_shape += [_sds((8, cout), f32), _sds((8, cout), f32)]
    in_specs += [pl.BlockSpec((TB, cout), lambda s, ii: (blk(s, ii), 0))] * len(saved)
    in_specs += [pl.BlockSpec(memory_space=pl.ANY)] * len(shared_in)
    args += saved + shared_in
    return pl.pallas_call(
        body, name=name, grid=(n_seq, nblk), in_specs=in_specs, out_specs=out_specs, out_shape=out_shape,
        input_output_aliases=aliases,
        scratch_shapes=[pltpu.VMEM((halo + TB + 8, cout), f32), pltpu.VMEM((TB, cout), f32),
                        pltpu.VMEM((halo + TB + halo, cout), f32), pltpu.VMEM((halo + TB, cout), f32),
                        pltpu.VMEM((halo, cout), f32),
                        pltpu.VMEM((len(_residues(range(off, off + kw))), halo + TB, cout), f32),
                        pltpu.VMEM((len(_residues(range(kw))), halo + TB + halo - 8, cout), f32)],
        compiler_params=_cparams(("arbitrary", "arbitrary")),
    )(*args)


def _gelu(x):
    return 0.5 * x * (1.0 + lax.erf(x * (1.0 / math.sqrt(2.0))))


def _tril_mask(n):
    r = lax.broadcasted_iota(jnp.int32, (n, n), 0)
    c = lax.broadcasted_iota(jnp.int32, (n, n), 1)
    return r >= c


def _head_spread(nh):
    r = lax.broadcasted_iota(jnp.int32, (nh, nh * HEAD), 0)
    c = lax.broadcasted_iota(jnp.int32, (nh, nh * HEAD), 1)
    return (c // HEAD == r).astype(f32)


def _gmlp_bias(bs):
    return lax.dot_general(bs, _head_spread(bs.shape[0]), (((0,), (0,)), ((), ())), precision=HI, preferred_element_type=f32)


def _gmlp_pair(bu, bv, g, b, w_a, w_b, bias):
    lo = lax.broadcasted_iota(jnp.int32, bu.shape, 1) < HEAD
    tril = _tril_mask(CHUNK)
    u = _gelu(bu)
    vn = _pair_ln(_gelu(bv), g, b)
    mix = jnp.where(lo, _bdot(jnp.where(tril, w_a, 0.0), vn, "nn"), _bdot(jnp.where(tril, w_b, 0.0), vn, "nn"))
    return u * (mix + bias)


def _gmlp_fwd(proj, col_block, ln_g, ln_b, w_s, b_s, share=None):
    T = proj.shape[0]
    nh = w_s.shape[0]
    width = nh * HEAD
    out_shape, out_blk, shared_in, aliases = _shared(share, T, width, bf16, 5)

    def body(p_ref, g_ref, b_ref, w_ref, bs_ref, *rest):
        out_ref, bias_ref = rest[len(shared_in):]

        @pl.when(pl.program_id(0) == 0)
        def _():
            bias_ref[...] = _gmlp_bias(bs_ref[...])

        for q in range(nh // 2):
            ls = pl.ds(q * PAIR, PAIR)
            lv = pl.ds(width + q * PAIR, PAIR)
            out_ref[:, ls] = _gmlp_pair(p_ref[:, ls].astype(f32), p_ref[:, lv].astype(f32), g_ref[:, ls], b_ref[:, ls], w_ref[2 * q], w_ref[2 * q + 1],
                                        bias_ref[:, ls]).astype(out_ref.dtype)

    row = pl.BlockSpec((1, width), lambda i: (0, 0))
    return pl.pallas_call(
        body, name="gmlp_fwd", grid=(T // CHUNK,),
        in_specs=[pl.BlockSpec((CHUNK, 2 * width), lambda i: (i, col_block)), row, row,
                  pl.BlockSpec((nh, CHUNK, CHUNK), lambda i: (0, 0, 0)), pl.BlockSpec((nh, CHUNK), lambda i: (0, 0))]
        + [pl.BlockSpec(memory_space=pl.ANY)] * len(shared_in),
        out_specs=pl.BlockSpec((CHUNK, width), lambda i: (i, out_blk)),
        out_shape=out_shape, input_output_aliases=aliases, scratch_shapes=[pltpu.VMEM((CHUNK, width), f32)],
        compiler_params=_cparams(("arbitrary",)),
    )(proj, ln_g.reshape(1, width), ln_b.reshape(1, width), w_s, b_s, *shared_in)


def _gmlp_bwd(proj, col_block, ln_g, ln_b, w_s, b_s, dy, dy_col_block, share=None):
    T = proj.shape[0]
    nh = w_s.shape[0]
    width = nh * HEAD
    nstep = T // CHUNK
    dp_shape, dp_blk, shared_in, aliases = _shared(share, T, 2 * width, bf16, 6)

    def body(p_ref, g_ref, b_ref, w_ref, bs_ref, dy_ref, *rest):
        dp_ref, dg_ref, db_ref, dw_ref, dbst_ref, bias_ref, dbias_ref = rest[len(shared_in):]

        @pl.when(pl.program_id(0) == 0)
        def _():
            dg_ref[...] = jnp.zeros_like(dg_ref)
            db_ref[...] = jnp.zeros_like(db_ref)
            dw_ref[...] = jnp.zeros_like(dw_ref)
            dbias_ref[...] = jnp.zeros_like(dbias_ref)
            bias_ref[...] = _gmlp_bias(bs_ref[...])

        for q in range(nh // 2):
            ls = pl.ds(q * PAIR, PAIR)
            lv = pl.ds(width + q * PAIR, PAIR)
            _, vjp = jax.vjp(_gmlp_pair, p_ref[:, ls].astype(f32), p_ref[:, lv].astype(f32), g_ref[:, ls], b_ref[:, ls], w_ref[2 * q], w_ref[2 * q + 1],
                             bias_ref[:, ls])
            dbu, dbv, dg, db, dw_a, dw_b, dbias = vjp(dy_ref[:, ls].astype(f32))
            dp_ref[:, ls] = dbu.astype(dp_ref.dtype)
            dp_ref[:, lv] = dbv.astype(dp_ref.dtype)
            dg_ref[0:1, ls] += dg
            db_ref[0:1, ls] += db
            dw_ref[2 * q] += dw_a
            dw_ref[2 * q + 1] += dw_b
            dbias_ref[:, ls] += dbias

        @pl.when(pl.program_id(0) == nstep - 1)
        def _():
            dbst_ref[...] = lax.dot_general(dbias_ref[...], _head_spread(nh), (((1,), (1,)), ((), ())),
                                            precision=HI, preferred_element_type=f32)

    row = pl.BlockSpec((1, width), lambda i: (0, 0))
    acc8 = pl.BlockSpec((8, width), lambda i: (0, 0))
    wspec = pl.BlockSpec((nh, CHUNK, CHUNK), lambda i: (0, 0, 0))
    res = pl.pallas_call(
        body, name="gmlp_bwd", grid=(nstep,),
        in_specs=[pl.BlockSpec((CHUNK, 2 * width), lambda i: (i, col_block)), row, row, wspec,
                  pl.BlockSpec((nh, CHUNK), lambda i: (0, 0)), pl.BlockSpec((CHUNK, width), lambda i: (i, dy_col_block))]
        + [pl.BlockSpec(memory_space=pl.ANY)] * len(shared_in),
        out_specs=[pl.BlockSpec((CHUNK, 2 * width), lambda i: (i, dp_blk)), acc8, acc8, wspec,
                   pl.BlockSpec((CHUNK, nh), lambda i: (0, 0))],
        out_shape=[dp_shape, _sds((8, width), f32), _sds((8, width), f32),
                   _sds((nh, CHUNK, CHUNK), f32), _sds((CHUNK, nh), f32)],
        input_output_aliases=aliases,
        scratch_shapes=[pltpu.VMEM((CHUNK, width), f32), pltpu.VMEM((CHUNK, width), f32)],
        compiler_params=_cparams(("arbitrary",)),
    )(proj, ln_g.reshape(1, width), ln_b.reshape(1, width), w_s, b_s, dy, *shared_in)
    return res[0], res[1], res[2], res[3], res[4].T


def _sel_col(x, h):
    lane = lax.broadcasted_iota(jnp.int32, x.shape, 1)
    return jnp.sum(jnp.where(lane == h, x, 0.0), axis=1, keepdims=True)


def _sel_row(x, h):
    sub = lax.broadcasted_iota(jnp.int32, x.shape, 0)
    return jnp.sum(jnp.where(sub == h, x, 0.0), axis=0, keepdims=True)


def _ssd_chunk(nh, ngrp, xs_l, z_l, b_l, c_l, dtraw, dtb, alog, dskip, ng_l, prev_l):
    hg = nh // ngrp
    tril = _tril_mask(CHUNK)
    tl = tril.astype(f32)
    lo = lax.broadcasted_iota(jnp.int32, (CHUNK, PAIR), 1) < HEAD
    lo_row = lo[0:1, :]
    dt = jax.nn.softplus(dtraw + dtb)
    a = dt * (-jnp.exp(alog))
    cs = jnp.dot(tl, a, precision=HI, preferred_element_type=f32)
    cst = lax.dot_general(a, tl, (((0,), (1,)), ((), ())), precision=HI, preferred_element_type=f32)
    cb_l = [_bdot(c_l[g], b_l[g], "nt") for g in range(ngrp)]
    yz_l, new_prev = [], []
    for q in range(nh // 2):
        g = (2 * q) // hg
        cols = []
        for h in (2 * q, 2 * q + 1):
            cs_h = _sel_col(cs, h)
            tot = _sel_row(cs_h, CHUNK - 1)
            seg = jnp.where(tril, cs_h - _sel_row(cst, h), 0.0)
            lmat = jnp.where(tril, jnp.exp(seg), 0.0)
            cols.append((_sel_col(dt, h), cs_h, tot, lmat, _sel_col(dskip, h)))
        (dt_a, cs_a, tot_a, l_a, dsk_a), (dt_b, cs_b, tot_b, l_b, dsk_b) = cols
        xs = xs_l[q]
        x = xs * jnp.where(lo, dt_a, dt_b)
        ydiag = jnp.where(lo, _bdot(cb_l[g] * l_a, x, "nn"), _bdot(cb_l[g] * l_b, x, "nn"))
        yoff = _bdot(c_l[g], prev_l[q], "nn") * jnp.where(lo, jnp.exp(cs_a), jnp.exp(cs_b))
        xdec = x * jnp.where(lo, jnp.exp(tot_a - cs_a), jnp.exp(tot_b - cs_b))
        st = _bdot(b_l[g], xdec, "tn")
        new_prev.append(prev_l[q] * jnp.where(lo_row, jnp.exp(tot_a), jnp.exp(tot_b)) + st)
        y = ydiag + yoff + jnp.where(lo_row, dsk_a, dsk_b) * xs
        yz_l.append(y * jax.nn.silu(z_l[q]))
    out = [None] * (nh // 2)
    qg = hg // 2
    for g in range(ngrp):
        ssq = sum(jnp.sum(yz_l[q] * yz_l[q], axis=-1, keepdims=True) for q in range(g * qg, (g + 1) * qg))
        r = lax.rsqrt(ssq * (1.0 / (hg * HEAD)) + EPS)
        for q in range(g * qg, (g + 1) * qg):
            out[q] = yz_l[q] * r * ng_l[q]
    return out, new_prev


def _ssd_read(nh, ngrp, nst, xbc_ref, z_ref, ng_ref, st_ref):
    cw = nh * HEAD
    xs_l = [xbc_ref[:, pl.ds(q * PAIR, PAIR)] for q in range(nh // 2)]
    b_l = [xbc_ref[:, pl.ds(cw + g * nst, nst)] for g in range(ngrp)]
    c_l = [xbc_ref[:, pl.ds(cw + ngrp * nst + g * nst, nst)] for g in range(ngrp)]
    z_l = [z_ref[:, pl.ds(q * PAIR, PAIR)].astype(f32) for q in range(nh // 2)]
    ng_l = [ng_ref[:, pl.ds(q * PAIR, PAIR)] for q in range(nh // 2)]
    prev_l = [st_ref[:, pl.ds(q * PAIR, PAIR)] for q in range(nh // 2)]
    return xs_l, z_l, b_l, c_l, ng_l, prev_l


def _ssd_fwd(xbc, proj, z_col_block, pdt, dtb, alog, dskip, ng, nh, ngrp, nst, n_seq, share=None):
    T = xbc.shape[0]
    cw = nh * HEAD
    nchunk = T // n_seq // CHUNK
    assert nst == CHUNK
    y_shape, y_blk, shared_in, aliases = _shared(share, T, cw, bf16, 7)

    def body(xbc_ref, z_ref, dt_ref, dtb_ref, alog_ref, dskip_ref, ng_ref, *rest):
        y_ref, sin_ref, st_ref = rest[len(shared_in):]

        @pl.when(pl.program_id(1) == 0)
        def _():
            st_ref[...] = jnp.zeros_like(st_ref)

        sin_ref[...] = st_ref[...]
        xs_l, z_l, b_l, c_l, ng_l, prev_l = _ssd_read(nh, ngrp, nst, xbc_ref, z_ref, ng_ref, st_ref)
        y_l, new_prev = _ssd_chunk(nh, ngrp, xs_l, z_l, b_l, c_l, dt_ref[...], dtb_ref[...], alog_ref[...],
                                   dskip_ref[...], ng_l, prev_l)
        for q in range(nh // 2):
            ls = pl.ds(q * PAIR, PAIR)
            y_ref[:, ls] = y_l[q].astype(y_ref.dtype)
            st_ref[:, ls] = new_prev[q]

    def blk(s, c):
        return s * nchunk + c

    prow = pl.BlockSpec((1, 128), lambda s, c: (0, 0))
    return pl.pallas_call(
        body, name="ssd_fwd", grid=(n_seq, nchunk),
        in_specs=[pl.BlockSpec((CHUNK, xbc.shape[1]), lambda s, c: (blk(s, c), 0)),
                  pl.BlockSpec((CHUNK, cw), lambda s, c: (blk(s, c), z_col_block)),
                  pl.BlockSpec((CHUNK, 128), lambda s, c: (blk(s, c), 0)),
                  prow, prow, prow, pl.BlockSpec((1, cw), lambda s, c: (0, 0))]
        + [pl.BlockSpec(memory_space=pl.ANY)] * len(shared_in),
        out_specs=[pl.BlockSpec((CHUNK, cw), lambda s, c: (blk(s, c), y_blk)),
                   pl.BlockSpec((nst, cw), lambda s, c: (blk(s, c), 0))],
        out_shape=[y_shape, _sds((T, cw), f32)], input_output_aliases=aliases,
        scratch_shapes=[pltpu.VMEM((nst, cw), f32)],
        compiler_params=_cparams(("arbitrary", "arbitrary")),
    )(xbc, proj, pdt, dtb, alog, dskip, ng.reshape(1, cw), *shared_in)


def _ssd_bwd(xbc, proj, z_col_block, pdt, dtb, alog, dskip, ng, sin, dy, dy_col_block, nh, ngrp, nst, n_seq, share=None):
    T, xw = xbc.shape
    cw = nh * HEAD
    nchunk = T // n_seq // CHUNK
    dz_shape, dz_blk, shared_in, aliases = _shared(share, T, cw, bf16, 9, out_index=1)

    def body(xbc_ref, z_ref, dt_ref, dtb_ref, alog_ref, dskip_ref, ng_ref, sin_ref, dy_ref, *rest):
        dxbc_ref, dz_ref, ddt_ref, ddtb_ref, dalog_ref, ddskip_ref, dng_ref, dst_ref = rest[len(shared_in):]
        s, cc = pl.program_id(0), pl.program_id(1)

        @pl.when((s == 0) & (cc == 0))
        def _():
            ddtb_ref[...] = jnp.zeros_like(ddtb_ref)
            dalog_ref[...] = jnp.zeros_like(dalog_ref)
            ddskip_ref[...] = jnp.zeros_like(ddskip_ref)
            dng_ref[...] = jnp.zeros_like(dng_ref)

        @pl.when(cc == 0)
        def _():
            dst_ref[...] = jnp.zeros_like(dst_ref)

        xs_l, z_l, b_l, c_l, ng_l, prev_l = _ssd_read(nh, ngrp, nst, xbc_ref, z_ref, ng_ref, sin_ref)
        _, vjp = jax.vjp(functools.partial(_ssd_chunk, nh, ngrp), xs_l, z_l, b_l, c_l, dt_ref[...], dtb_ref[...],
                         alog_ref[...], dskip_ref[...], ng_l, prev_l)
        dy_l = [dy_ref[:, pl.ds(q * PAIR, PAIR)].astype(f32) for q in range(nh // 2)]
        dst_l = [dst_ref[:, pl.ds(q * PAIR, PAIR)] for q in range(nh // 2)]
        dxs_l, dz_l, db_l, dc_l, ddt, ddtb, dalog, ddskip, dng_l, dprev_l = vjp((dy_l, dst_l))
        for q in range(nh // 2):
            ls = pl.ds(q * PAIR, PAIR)
            dxbc_ref[:, ls] = dxs_l[q]
            dz_ref[:, ls] = dz_l[q].astype(dz_ref.dtype)
            dng_ref[0:1, ls] += dng_l[q]
            dst_ref[:, ls] = dprev_l[q]
        for g in range(ngrp):
            dxbc_ref[:, pl.ds(cw + g * nst, nst)] = db_l[g]
            dxbc_ref[:, pl.ds(cw + ngrp * nst + g * nst, nst)] = dc_l[g]
        ddt_ref[...] = ddt.astype(ddt_ref.dtype)
        ddtb_ref[0:1, :] += ddtb
        dalog_ref[0:1, :] += dalog
        ddskip_ref[0:1, :] += ddskip

    def blk(s, cc):
        return s * nchunk + (nchunk - 1 - cc)

    prow = pl.BlockSpec((1, 128), lambda s, c: (0, 0))
    pacc = pl.BlockSpec((8, 128), lambda s, c: (0, 0))
    return pl.pallas_call(
        body, name="ssd_bwd", grid=(n_seq, nchunk),
        in_specs=[pl.BlockSpec((CHUNK, xw), lambda s, c: (blk(s, c), 0)),
                  pl.BlockSpec((CHUNK, cw), lambda s, c: (blk(s, c), z_col_block)),
                  pl.BlockSpec((CHUNK, 128), lambda s, c: (blk(s, c), 0)),
                  prow, prow, prow, pl.BlockSpec((1, cw), lambda s, c: (0, 0)),
                  pl.BlockSpec((nst, cw), lambda s, c: (blk(s, c), 0)),
                  pl.BlockSpec((CHUNK, cw), lambda s, c: (blk(s, c), dy_col_block))]
        + [pl.BlockSpec(memory_space=pl.ANY)] * len(shared_in),
        out_specs=[pl.BlockSpec((CHUNK, xw), lambda s, c: (blk(s, c), 0)),
                   pl.BlockSpec((CHUNK, cw), lambda s, c: (blk(s, c), dz_blk)),
                   pl.BlockSpec((CHUNK, 128), lambda s, c: (blk(s, c), 0)),
                   pacc, pacc, pacc, pl.BlockSpec((8, cw), lambda s, c: (0, 0))],
        out_shape=[_sds((T, xw), f32), dz_shape, _sds((T, 128), bf16),
                   _sds((8, 128), f32), _sds((8, 128), f32), _sds((8, 128), f32), _sds((8, cw), f32)],
        input_output_aliases=aliases,
        scratch_shapes=[pltpu.VMEM((nst, cw), f32)],
        compiler_params=_cparams(("arbitrary", "arbitrary")),
    )(xbc, proj, pdt, dtb, alog, dskip, ng.reshape(1, cw), sin, dy, *shared_in)


_HBM = pl.BlockSpec(memory_space=pltpu.HBM)
_SEM = pl.BlockSpec(memory_space=pltpu.SEMAPHORE)
_EFFECT = pltpu.SideEffectType.DATAFLOW_SIDE_EFFECTING


def _split_copies(n, scatter, src_refs, land_refs, send_sems, recv_sems):
    npeer = N_DEV - 1
    x, y, c = lax.axis_index("x"), lax.axis_index("y"), lax.axis_index("c")
    me = 4 * x + 2 * y + c
    copies = []
    for i in range(n):
        for k in range(1, N_DEV):
            px = 1 - x if k & 4 else x
            py = 1 - y if k & 2 else y
            pc = 1 - c if k & 1 else c
            src = src_refs[i].at[4 * px + 2 * py + pc] if scatter else src_refs[i]
            copies.append(pltpu.make_async_remote_copy(
                src_ref=src, dst_ref=land_refs[i].at[me],
                send_sem=send_sems.at[i * npeer + k - 1], recv_sem=recv_sems.at[i * npeer + k - 1],
                device_id=(px, py, pc), device_id_type=pl.DeviceIdType.MESH))
    return copies


def _exchange_start(name, arrs, scatter):
    n = len(arrs)
    nsem = n * (N_DEV - 1)
    me = 4 * lax.axis_index("x") + 2 * lax.axis_index("y") + lax.axis_index("c")
    lands = []
    for a in arrs:
        own = lax.dynamic_index_in_dim(a, me, 0, keepdims=True) if scatter else a[None]
        full = lax.empty(a.shape if scatter else (N_DEV,) + a.shape, a.dtype)
        lands.append(lax.dynamic_update_slice(full, own, (me,) + (0,) * (full.ndim - 1)))

    def body(*refs):
        src_refs, land_refs = refs[:n], refs[n:2 * n]
        send_sems, recv_sems = refs[2 * n], refs[2 * n + 1]
        token = refs[-1]
        for cp in _split_copies(n, scatter, src_refs, land_refs, send_sems, recv_sems):
            cp.start()
        token[...] = jnp.zeros_like(token)

    res = pl.pallas_call(
        body, name=name,
        out_shape=(pltpu.SemaphoreType.DMA((nsem,)), pltpu.SemaphoreType.DMA((nsem,)),
                   *[pltpu.HBM(a.shape, a.dtype) for a in arrs], *[pltpu.HBM(l.shape, l.dtype) for l in lands],
                   _sds((8, 128), f32)),
        in_specs=[_HBM] * (2 * n),
        out_specs=(_SEM, _SEM, *[_HBM] * (2 * n), pl.BlockSpec(memory_space=pltpu.VMEM)),
        input_output_aliases={j: 2 + j for j in range(2 * n)},
        compiler_params=pltpu.CompilerParams(has_side_effects=_EFFECT),
    )(*[pltpu.with_memory_space_constraint(a, pltpu.HBM) for a in arrs],
      *[pltpu.with_memory_space_constraint(l, pltpu.HBM) for l in lands])
    return (n, scatter, res[0], res[1], res[2:2 + n], res[2 + n:2 + 2 * n]), res[-1]


def _exchange_wait(name, handle, after):
    n, scatter, send_sems, recv_sems, srcs, lands = handle
    after = list(after) if isinstance(after, (list, tuple)) else [after]

    def body(*refs):
        src_refs, land_refs = refs[:n], refs[n:2 * n]
        for cp in _split_copies(n, scatter, src_refs, land_refs, refs[2 * n], refs[2 * n + 1]):
            cp.wait_send()
            cp.wait_recv()

    res = pl.pallas_call(
        body, name=name,
        out_shape=[pltpu.HBM(a.shape, a.dtype) for a in (*srcs, *lands)],
        in_specs=[_HBM] * (2 * n) + [_SEM, _SEM] + [pl.BlockSpec(memory_space=pl.ANY)] * len(after),
        out_specs=[_HBM] * (2 * n),
        input_output_aliases={j: j for j in range(2 * n)},
        compiler_params=pltpu.CompilerParams(has_side_effects=_EFFECT),
    )(*srcs, *lands, send_sems, recv_sems, *after)
    return res[n:]


def _adam_tiles(R, C):
    if R % 256 == 0:
        return (256, C), (R // 256, 1)
    assert C % 128 == 0
    return (R, 128), (1, C // 128)


def _adam(name, parts, w, m, v, layer=None, depth=None, into=None, stacked_in=False):
    P, R, C = parts.shape
    (tr, tc), (gr, gc) = _adam_tiles(R, C)
    c1 = 1.0 / (1.0 - ADAM_B1 ** ADAM_STEP)
    c2 = 1.0 / (1.0 - ADAM_B2 ** ADAM_STEP)
    into = [] if into is None else list(into)

    def body(p_ref, w_ref, m_ref, v_ref, *rest):
        g_ref, d_ref, nm_ref, nv_ref = rest[len(into):]
        g = p_ref[0].astype(f32)
        for s in range(1, P):
            g = g + p_ref[s].astype(f32)
        nm = ADAM_B1 * m_ref[...] + (1.0 - ADAM_B1) * g
        nv = ADAM_B2 * v_ref[...] + (1.0 - ADAM_B2) * (g * g)
        g_ref[...] = g
        nm_ref[...] = nm
        nv_ref[...] = nv
        d_ref[...] = -ADAM_LR * ((nm * c1) / (jnp.sqrt(nv * c2) + ADAM_EPS) + ADAM_WD * w_ref[...])

    tile = pl.BlockSpec((tr, tc), lambda i, j: (i, j))
    layer_tile = pl.BlockSpec((None, tr, tc), lambda i, j: (layer, i, j))
    out_tile, out_sds = (tile, _sds((R, C), f32)) if layer is None else (layer_tile, _sds((depth, R, C), f32))
    return pl.pallas_call(
        body, name=name, grid=(gr, gc),
        in_specs=[pl.BlockSpec((P, tr, tc), lambda i, j: (0, i, j))] + [layer_tile if stacked_in else tile] * 3
        + [pl.BlockSpec(memory_space=pl.ANY)] * len(into),
        out_specs=[out_tile] * 4, out_shape=[out_sds] * 4,
        input_output_aliases={4 + k: k for k in range(len(into))},
        compiler_params=_cparams(("arbitrary", "arbitrary")),
    )(parts, w, m, v, *into)


def _sum_parts(name, parts):
    P, R, C = parts.shape
    tr = 256 if R % 256 == 0 else R

    def body(p_ref, o_ref):
        g = p_ref[0]
        for s in range(1, P):
            g = g + p_ref[s]
        o_ref[...] = g

    return pl.pallas_call(
        body, name=name, grid=(R // tr,),
        in_specs=[pl.BlockSpec((P, tr, C), lambda i: (0, i, 0))], out_specs=pl.BlockSpec((tr, C), lambda i: (i, 0)),
        out_shape=_sds((R, C), f32), compiler_params=_cparams(("arbitrary",)),
    )(parts)


def _pad_to(a, n, axis):
    if a.shape[axis] == n:
        return a
    cfg = [(0, 0)] * a.ndim
    cfg[axis] = (0, n - a.shape[axis])
    return jnp.pad(a, cfg)


def _pack(arrs):
    flat = [_pad_to(a.reshape(-1), -(-a.size // 128) * 128, 0) for a in arrs]
    rows = jnp.concatenate(flat).reshape(-1, 128)
    return _pad_to(rows, -(-rows.shape[0] // 256) * 256, 0)


def _unpack(slab, shapes):
    flat = slab.reshape(-1)
    out, o = [], 0
    for s in shapes:
        n = math.prod(s)
        out.append(flat[o:o + n].reshape(s))
        o += -(-n // 128) * 128
    return out


_NAMES = ['norm1_g', 'w_in', 'conv_a_w', 'conv_a_b', 'ln_a_g', 'ln_a_b', 'ln_b_g', 'ln_b_b', 'w_spatial', 'b_spatial',
          'conv_c_w', 'conv_c_b', 'dt_bias', 'a_log', 'd_skip', 'norm_c_g', 'w_out', 'norm2_g', 'w_ff1', 'w_ff2', 'final_g']
_REPL = ['norm1_g', 'conv_a_b', 'ln_a_g', 'ln_a_b', 'ln_b_g', 'ln_b_b', 'w_spatial', 'b_spatial', 'conv_c_b',
         'dt_bias', 'a_log', 'd_skip', 'norm_c_g', 'norm2_g']
_CONVW = ['conv_a_w', 'conv_c_w']
_BIG = ['w_in', 'w_out', 'w_ff1', 'w_ff2']
_BIG_T = {'w_in': True, 'w_out': False, 'w_ff1': True, 'w_ff2': False}


def _row128(v):
    return _pad_to(v.reshape(1, -1), 128, 1)


def _step(p, m, v, x, loss_target):
    nb, S, D = x.shape
    T = nb * S
    depth = p['norm1_g'].shape[0]
    a_w = p['conv_a_b'].shape[1]
    b_w = p['ln_b_g'].shape[1]
    nh = p['dt_bias'].shape[1]
    c_w = p['norm_c_g'].shape[1]
    xw = p['conv_c_b'].shape[1]
    ngrp = 2
    nst = (xw - c_w) // (2 * ngrp)
    d_in = p['w_in'].shape[2] * N_DEV
    main = d_in - nh
    assert main == 2 * a_w + 2 * b_w + c_w + xw and 2 * a_w == 2 * b_w == c_w and xw % c_w == c_w // 2
    me = 4 * lax.axis_index("x") + 2 * lax.axis_index("y") + lax.axis_index("c")

    x2 = x.reshape(T, D)
    tgt = loss_target.reshape(T, D)

    def shards(i, z=None):
        z = 0.0 if z is None else z
        return [(p['w_in'][i].T + z).astype(bf16), (p['w_out'][i] + z).astype(bf16), (p['w_ff1'][i].T + z).astype(bf16),
                (p['w_ff2'][i] + z).astype(bf16), p['conv_a_w'][i], p['conv_c_w'][i]]

    def gathered_in(wt, ca, cc):
        wt = wt.reshape(d_in, D)
        ca = jnp.transpose(ca, (1, 0, 2)).reshape(KA, a_w)
        cc = jnp.transpose(cc, (1, 0, 2)).reshape(KC, xw)
        return dict(wt_main=wt[:main], wt_dt=_pad_to(wt[main:], 128, 0), ca=_pad_to(ca, 32, 0), cc=_pad_to(cc, 8, 0))

    def start_layer(i, after=None):
        sh = shards(i, None if after is None else after[0, 0])
        ha, t = _exchange_start("gather_w%da_start" % i, [sh[0], sh[4], sh[5]], False)
        hb, t = _exchange_start("gather_w%db_start" % i, [shards(i, t[0, 0])[1]], False)
        hc, t = _exchange_start("gather_w%dc_start" % i, [shards(i, t[0, 0])[2]], False)
        hd, t = _exchange_start("gather_w%dd_start" % i, [shards(i, t[0, 0])[3]], False)
        return dict(a=ha, b=hb, c=hc, d=hd), t

    W, saved = [], []
    xc = x2
    H, tok = start_layer(0)
    for i in range(depth):
        w = gathered_in(*_exchange_wait("gather_w%da_wait" % i, H['a'], [xc, tok]))
        W.append(w)
        Hi = H
        h1, rtok = _rms_fwd(xc, p['norm1_g'][i])
        if i + 1 < depth:
            H, tok = start_layer(i + 1, rtok + tok)
        else:
            tok = None
        (proj,) = _mm("mm_proj", h1, w['wt_main'], "nt", [bf16], dep=tok)
        (pdt,) = _mm("mm_pdt", h1, w['wt_dt'], "nt", [f32])
        mixw = a_w + b_w + c_w
        ycat, acca = _conv_fwd("confa_fwd", proj, 0, w['ca'], p['conv_a_b'][i], KA, True, nb, p['ln_a_g'][i], p['ln_a_b'][i],
                               share=(mixw, 0, None))
        ycat = _gmlp_fwd(proj, 1, p['ln_b_g'][i], p['ln_b_b'][i], p['w_spatial'][i], p['b_spatial'][i], share=(mixw, 1, ycat))
        xbc = _conv_fwd("convc_fwd", proj, 2, w['cc'], p['conv_c_b'][i], KC, False, nb)
        dtb, alog, dsk = _row128(p['dt_bias'][i]), _row128(p['a_log'][i]), _row128(p['d_skip'][i])
        ycat, sin = _ssd_fwd(xbc, proj, 2, pdt, dtb, alog, dsk, p['norm_c_g'][i], nh, ngrp, nst, nb, share=(mixw, 1, ycat))
        w['wout'] = _exchange_wait("gather_w%db_wait" % i, Hi['b'], ycat)[0].reshape(-1, D)
        xm, h2 = _mm("mm_out", ycat, w['wout'], "nn", [f32, bf16], _ep_add_rms, (xc,), rows=(p['norm2_g'][i].reshape(1, D),))
        w['w1t'] = _exchange_wait("gather_w%dc_wait" % i, Hi['c'], h2)[0].reshape(-1, D)
        f, a = _mm("mm_ff1", h2, w['w1t'], "nt", [bf16, bf16], _ep_relu2)
        w['w2'] = _exchange_wait("gather_w%dd_wait" % i, Hi['d'], a)[0].reshape(-1, D)
        (xo,) = _mm("mm_ff2", a, w['w2'], "nn", [f32], _ep_add, (xm,))
        saved.append(dict(x_in=xc, h1=h1, proj=proj, pdt=pdt, xbc=xbc, sin=sin, ycat=ycat, xm=xm, h2=h2, f=f, a=a,
                          dtb=dtb, alog=alog, dsk=dsk, acca=acca))
        xc = xo

    lp, dx, dfinal = _loss_head(xc, p['final_g'], tgt)
    loss = lax.psum(lp[0, 0], ("x", "y", "c"))

    out = {}
    kinds = ("grad", "delta", "new_m", "new_v")
    names1 = _REPL + _CONVW

    started, small = [], [None] * depth

    def send(n, i, g):
        handle, token = _exchange_start("scatter_%s_%d_start" % (n, i), [g.reshape(N_DEV, -1, D)], True)
        started.append((n, i, handle))
        return token

    tok = None
    for i in reversed(range(depth)):
        w, sv = W[i], saved[i]
        (df,) = _mm("mm_df", dx, w['w2'], "nt", [bf16], _ep_drelu2, (sv['f'],), dep=tok)
        (gw2,) = _mm("mm_gw2", sv['a'], dx, "tn", [bf16])
        tok = send('w_ff2', i, gw2)
        dxm, dg2 = _mm("mm_dh2", df, w['w1t'], "nn", [f32], _ep_rms_bwd, (sv['xm'], dx), dep=tok,
                       rows=(p['norm2_g'][i].reshape(1, D),), n_row_out=1)
        (gw1t,) = _mm("mm_gw1", df, sv['h2'], "tn", [bf16])
        tok = send('w_ff1', i, gw1t)
        (dycat,) = _mm("mm_dycat", dxm, w['wout'], "nt", [bf16], dep=tok)
        (gwout,) = _mm("mm_gwout", sv['ycat'], dxm, "tn", [bf16])
        tok = send('w_out', i, gwout)
        dproj, dwa, dba, dlag, dlab = _conv_bwd("confa_bwd", sv['proj'], 0, w['ca'], p['conv_a_b'][i] + tok[0, 0], dycat, 0, KA,
                                                True, nb, p['ln_a_g'][i], p['ln_a_b'][i], share=(main, 0, None), acc=sv['acca'])
        dproj, dlbg, dlbb, dws, dbs = _gmlp_bwd(sv['proj'], 1, p['ln_b_g'][i], p['ln_b_b'][i], p['w_spatial'][i],
                                                p['b_spatial'][i], dycat, 1, share=(main, 1, dproj))
        dxbc, dproj, ddt, ddtb, dalog, ddsk, dng = _ssd_bwd(sv['xbc'], sv['proj'], 2, sv['pdt'], sv['dtb'], sv['alog'],
                                                            sv['dsk'], p['norm_c_g'][i], sv['sin'], dycat, 1, nh, ngrp, nst, nb,
                                                            share=(main, 2, dproj))
        dproj, dwc, dbc = _conv_bwd("convc_bwd", sv['proj'], 2, w['cc'], p['conv_c_b'][i], dxbc, 0, KC, False, nb,
                                    share=(main, 2, dproj))
        (dh_main,) = _mm("mm_dh1", dproj, w['wt_main'], "nn", [f32])
        (gwt_main,) = _mm("mm_gwin", dproj, sv['h1'], "tn", [bf16])
        (gwt_dt,) = _mm("mm_gwdt", ddt, sv['h1'], "tn", [bf16])
        tok = send('w_in', i, jnp.concatenate([gwt_main, gwt_dt[:nh]], axis=0))
        dx, dg1 = _mm("mm_dh1dt", ddt, w['wt_dt'], "nn", [f32], _ep_add_rms_bwd, (dh_main, sv['x_in'], dxm), dep=tok,
                      rows=(p['norm1_g'][i].reshape(1, D),), n_row_out=1)

        gi = dict(norm1_g=dg1[0], norm2_g=dg2[0], conv_a_w=dwa[:KA], conv_a_b=dba[0], ln_a_g=dlag[0], ln_a_b=dlab[0],
                  ln_b_g=dlbg[0], ln_b_b=dlbb[0], w_spatial=dws, b_spatial=dbs, conv_c_w=dwc[:KC], conv_c_b=dbc[0],
                  dt_bias=ddtb[0, :nh], a_log=dalog[0, :nh], d_skip=ddsk[0, :nh], norm_c_g=dng[0])
        parts_i = [gi[n] for n in names1] + ([dfinal[0]] if i == depth - 1 else [])
        handle, tok = _exchange_start("gather_g%d_start" % i, [_pack(parts_i)], False)
        small[i] = ([a.shape for a in parts_i], handle)
    grad_x = dx.reshape(nb, S, D)

    dep = [dx, tok]
    for n, i, handle in started:
        (parts,) = _exchange_wait("scatter_%s_%d_wait" % (n, i), handle, dep)
        prev = [out[(kind, n)] for kind in kinds] if (kinds[0], n) in out else None
        if _BIG_T[n]:
            res = _adam("adam_%s_%d" % (n, i), parts, p[n][i].T, m[n][i].T, v[n][i].T, layer=i, depth=depth, into=prev)
        else:
            res = _adam("adam_%s_%d" % (n, i), parts, p[n], m[n], v[n], layer=i, depth=depth, into=prev, stacked_in=True)
        for kind, r in zip(kinds, res):
            out[(kind, n)] = r
        dep = res[3]

    gsum = [None] * depth
    for i in reversed(range(depth)):
        (parts,) = _exchange_wait("gather_g%d_wait" % i, small[i][1], dep)
        gsum[i] = _sum_parts("sum_small", parts)
        dep = gsum[i]
    widths = [-(-math.prod(p[n].shape[1:]) // 128) * 128 for n in _REPL]
    rep_rows = sum(widths) // 128
    tot_rows = -(-depth * rep_rows // 256) * 256

    def rep_slab(q):
        cols = [_pad_to(q[n].reshape(depth, -1), wd, 1) for n, wd in zip(_REPL, widths)]
        return _pad_to(jnp.concatenate(cols, axis=1).reshape(-1, 128), tot_rows, 0)

    g_rep = _pad_to(jnp.concatenate([g[:rep_rows] for g in gsum], axis=0), tot_rows, 0)
    res = _adam("adam_small", g_rep[None], rep_slab(p), rep_slab(m), rep_slab(v))
    for kind, r in zip(kinds, res):
        view = r[:depth * rep_rows].reshape(depth, -1)
        o = 0
        for n, wd in zip(_REPL, widths):
            out[(kind, n)] = view[:, o:o + math.prod(p[n].shape[1:])].reshape(p[n].shape)
            o += wd

    extra = []
    for i in range(depth):
        tail = _unpack(gsum[i][rep_rows:], small[i][0][len(_REPL):])
        extra.append(tail)
    gconv = []
    for j, n in enumerate(_CONVW):
        cw_shard = p[n].shape[2]
        full = jnp.stack([extra[i][j] for i in range(depth)])
        gconv.append(lax.dynamic_slice_in_dim(full, me * cw_shard, cw_shard, axis=2))
    tail_names = _CONVW + ['final_g']
    res = _adam("adam_conv", _pack(gconv + [extra[depth - 1][len(_CONVW)]])[None],
                *[_pack([q[n] for n in tail_names]) for q in (p, m, v)])
    for kind, r in zip(kinds, res):
        for n, arr in zip(tail_names, _unpack(r, [p[n].shape for n in tail_names])):
            out[(kind, n)] = arr
    for n in _BIG:
        if _BIG_T[n]:
            for kind in kinds:
                out[(kind, n)] = jnp.swapaxes(out[(kind, n)], 1, 2)

    flat = [loss, grad_x]
    for kind in ("grad", "delta", "new_m", "new_v"):
        flat += [out[(kind, n)] for n in _NAMES]
    return tuple(flat)


def kernel(x, norm1_g, w_in, conv_a_w, conv_a_b, ln_a_g, ln_a_b, ln_b_g, ln_b_b, w_spatial, b_spatial, conv_c_w, conv_c_b, dt_bias, a_log, d_skip, norm_c_g, w_out, norm2_g, w_ff1, w_ff2, final_g, loss_target, m_norm1_g, m_w_in, m_conv_a_w, m_conv_a_b, m_ln_a_g, m_ln_a_b, m_ln_b_g, m_ln_b_b, m_w_spatial, m_b_spatial, m_conv_c_w, m_conv_c_b, m_dt_bias, m_a_log, m_d_skip, m_norm_c_g, m_w_out, m_norm2_g, m_w_ff1, m_w_ff2, m_final_g, v_norm1_g, v_w_in, v_conv_a_w, v_conv_a_b, v_ln_a_g, v_ln_a_b, v_ln_b_g, v_ln_b_b, v_w_spatial, v_b_spatial, v_conv_c_w, v_conv_c_b, v_dt_bias, v_a_log, v_d_skip, v_norm_c_g, v_w_out, v_norm2_g, v_w_ff1, v_w_ff2, v_final_g):
    p = dict(zip(_NAMES, (norm1_g, w_in, conv_a_w, conv_a_b, ln_a_g, ln_a_b, ln_b_g, ln_b_b, w_spatial, b_spatial, conv_c_w,
                          conv_c_b, dt_bias, a_log, d_skip, norm_c_g, w_out, norm2_g, w_ff1, w_ff2, final_g)))
    m = dict(zip(_NAMES, (m_norm1_g, m_w_in, m_conv_a_w, m_conv_a_b, m_ln_a_g, m_ln_a_b, m_ln_b_g, m_ln_b_b, m_w_spatial,
                          m_b_spatial, m_conv_c_w, m_conv_c_b, m_dt_bias, m_a_log, m_d_skip, m_norm_c_g, m_w_out, m_norm2_g,
                          m_w_ff1, m_w_ff2, m_final_g)))
    v = dict(zip(_NAMES, (v_norm1_g, v_w_in, v_conv_a_w, v_conv_a_b, v_ln_a_g, v_ln_a_b, v_ln_b_g, v_ln_b_b, v_w_spatial,
                          v_b_spatial, v_conv_c_w, v_conv_c_b, v_dt_bias, v_a_log, v_d_skip, v_norm_c_g, v_w_out, v_norm2_g,
                          v_w_ff1, v_w_ff2, v_final_g)))
    return _step(p, m, v, x, loss_target)
```

```python
import functools
import math

import jax
import jax.numpy as jnp
from jax import lax
from jax.experimental import pallas as pl
from jax.experimental.pallas import tpu as pltpu

f32 = jnp.float32
bf16 = jnp.bfloat16
HI = lax.Precision.HIGHEST
EPS = 1e-5
HEAD = 64
CHUNK = 128
KA = 31
KC = 4
N_DEV = 8
VMEM_LIMIT = 56 * 1024 * 1024
MM_VMEM_BUDGET = 52 * 1024 * 1024

ADAM_LR = 0.001
ADAM_B1 = 0.9
ADAM_B2 = 0.999
ADAM_EPS = 1e-08
ADAM_WD = 0.01
ADAM_STEP = 10


def _cparams(sem=None):
    return pltpu.CompilerParams(dimension_semantics=sem, vmem_limit_bytes=VMEM_LIMIT)


def _sds(shape, dtype):
    return jax.ShapeDtypeStruct(shape, dtype)


_DIMS = {"nn": ((1,), (0,)), "nt": ((1,), (1,)), "tn": ((0,), (0,))}


def _dot16(a, b, form):
    return lax.dot_general(a.astype(bf16), b.astype(bf16), (_DIMS[form], ((), ())), preferred_element_type=f32)


@functools.partial(jax.custom_vjp, nondiff_argnums=(2,))
def _bdot(a, b, form):
    return _dot16(a, b, form)


def _bdot_fwd(a, b, form):
    return _dot16(a, b, form), (a, b)


def _bdot_bwd(form, res, ct):
    a, b = res
    if form == "nn":
        da, db = _dot16(ct, b, "nt"), _dot16(a, ct, "tn")
    elif form == "nt":
        da, db = _dot16(ct, b, "nn"), _dot16(ct, a, "tn")
    else:
        da, db = _dot16(b, ct, "nt"), _dot16(a, ct, "nn")
    return da.astype(a.dtype), db.astype(b.dtype)


_bdot.defvjp(_bdot_fwd, _bdot_bwd)


def _tile(n, cap):
    if n <= cap:
        return n
    for d in range(cap - cap % 128, 0, -128):
        if n % d == 0:
            return d
    raise ValueError((n, cap))


def _mm(name, a, b, form, out_dtypes, epilogue=None, extras=(), tm=2048, tn=1024, tk=4096, dep=None, rows=(), n_row_out=0):
    if form == "tn":
        K, M = a.shape
    else:
        M, K = a.shape
    N = b.shape[0] if form == "nt" else b.shape[1]
    tm, tn, tk = _tile(M, tm), _tile(N, tn), _tile(K, tk)
    nk = K // tk

    def vmem_bytes(tm):
        mn = sum(jnp.dtype(e.dtype).itemsize for e in extras) + sum(jnp.dtype(d).itemsize for d in out_dtypes)
        return 2 * (tm * tk * a.dtype.itemsize + tk * tn * b.dtype.itemsize + tm * tn * mn) + 2 * tm * tn * 4

    while vmem_bytes(tm) > MM_VMEM_BUDGET and tm % 256 == 0:
        tm //= 2
    ne, no, nr = len(extras), len(out_dtypes), len(rows)
    assert n_row_out == 0 or tn == N
    deps = () if dep is None else (dep,)
    if epilogue is None:
        epilogue = lambda acc: (acc,)

    def body(a_ref, b_ref, *rest):
        extra_refs, row_refs = rest[:ne], rest[ne:ne + nr]
        rest = rest[ne + nr + len(deps):]
        out_refs, rowout_refs = rest[:no], rest[no:no + n_row_out]
        part = lax.dot_general(a_ref[...].astype(bf16), b_ref[...].astype(bf16),
                               (_DIMS[form], ((), ())), preferred_element_type=f32)

        def finish(acc):
            outs = epilogue(acc, *[e[...] for e in extra_refs], *[r[...] for r in row_refs])
            for o_ref, v in zip(out_refs, outs[:no]):
                o_ref[...] = v.astype(o_ref.dtype)
            for r_ref, v in zip(rowout_refs, outs[no:]):
                @pl.when(pl.program_id(0) == 0)
                def _():
                    r_ref[...] = jnp.zeros_like(r_ref)

                r_ref[0:1, :] += v

        if nk == 1:
            finish(part)
            return
        acc_ref = rest[no + n_row_out]
        k = pl.program_id(2)

        @pl.when(k == 0)
        def _():
            acc_ref[...] = part

        @pl.when((k > 0) & (k < nk - 1))
        def _():
            acc_ref[...] += part

        @pl.when(k == nk - 1)
        def _():
            finish(acc_ref[...] + part)

    a_spec = pl.BlockSpec((tk, tm), lambda i, j, k: (k, i)) if form == "tn" else pl.BlockSpec((tm, tk), lambda i, j, k: (i, k))
    b_spec = pl.BlockSpec((tn, tk), lambda i, j, k: (j, k)) if form == "nt" else pl.BlockSpec((tk, tn), lambda i, j, k: (k, j))
    mn_spec = pl.BlockSpec((tm, tn), lambda i, j, k: (i, j))
    return pl.pallas_call(
        body, name=name, grid=(M // tm, N // tn, nk),
        in_specs=[a_spec, b_spec] + [mn_spec] * ne + [pl.BlockSpec((1, tn), lambda i, j, k: (0, j))] * nr
        + [pl.BlockSpec((8, 128), lambda i, j, k: (0, 0))] * len(deps),
        out_specs=[mn_spec] * no + [pl.BlockSpec((8, tn), lambda i, j, k: (0, j))] * n_row_out,
        out_shape=[_sds((M, N), d) for d in out_dtypes] + [_sds((8, N), f32)] * n_row_out,
        scratch_shapes=[pltpu.VMEM((tm, tn), f32)] if nk > 1 else [],
        compiler_params=_cparams(("arbitrary", "arbitrary", "arbitrary")),
    )(a, b, *extras, *rows, *deps)


def _ep_add(acc, r):
    return (acc + r,)


def _ep_add_rms(acc, r, g):
    x = acc + r
    return x, _rms(x, g)


def _ep_rms_bwd(acc, x, dres, g):
    _, vjp = jax.vjp(_rms, x, g)
    dx, dg = vjp(acc)
    return dres + dx, dg


def _ep_add_rms_bwd(acc, more, x, dres, g):
    return _ep_rms_bwd(acc + more, x, dres, g)


def _ep_relu2(acc):
    r = jnp.maximum(acc, 0.0)
    return acc, r * r


def _ep_drelu2(acc, f):
    return (acc * 2.0 * jnp.maximum(f, 0.0),)


def _rms(x, g):
    return x * lax.rsqrt(jnp.mean(x * x, axis=-1, keepdims=True) + EPS) * g


TT = 512


def _rms_fwd(x, g):
    T, D = x.shape

    def body(x_ref, g_ref, h_ref, tok_ref):
        h_ref[...] = _rms(x_ref[...], g_ref[...]).astype(bf16)
        tok_ref[...] = jnp.zeros_like(tok_ref)

    return pl.pallas_call(
        body, name="rms_fwd", grid=(T // TT,),
        in_specs=[pl.BlockSpec((TT, D), lambda i: (i, 0)), pl.BlockSpec((1, D), lambda i: (0, 0))],
        out_specs=[pl.BlockSpec((TT, D), lambda i: (i, 0)), pl.BlockSpec((8, 128), lambda i: (0, 0))],
        out_shape=[_sds((T, D), bf16), _sds((8, 128), f32)], compiler_params=_cparams(("arbitrary",)),
    )(x, g.reshape(1, D))


def _loss_head(x, g, tgt):
    T, D = x.shape

    def f(xv, gv, tv):
        e = _rms(xv, gv) - tv
        return 0.5 * jnp.sum(jnp.sum(e * e, axis=-1, keepdims=True) * (1.0 / D), axis=0, keepdims=True)

    def body(x_ref, g_ref, t_ref, loss_ref, dx_ref, dg_ref):
        tv = t_ref[...]
        l, vjp = jax.vjp(lambda xv, gv: f(xv, gv, tv), x_ref[...], g_ref[...])
        dx, dg = vjp(jnp.ones((1, 1), f32))
        dx_ref[...] = dx

        @pl.when(pl.program_id(0) == 0)
        def _():
            dg_ref[...] = jnp.zeros_like(dg_ref)
            loss_ref[...] = jnp.zeros_like(loss_ref)

        dg_ref[0:1, :] += dg
        loss_ref[...] += jnp.broadcast_to(l, loss_ref.shape)

    tile = pl.BlockSpec((TT, D), lambda i: (i, 0))
    return pl.pallas_call(
        body, name="loss_head", grid=(T // TT,),
        in_specs=[tile, pl.BlockSpec((1, D), lambda i: (0, 0)), tile],
        out_specs=[pl.BlockSpec((8, 128), lambda i: (0, 0)), tile, pl.BlockSpec((8, D), lambda i: (0, 0))],
        out_shape=[_sds((8, 128), f32), _sds((T, D), f32), _sds((8, D), f32)],
        compiler_params=_cparams(("arbitrary",)),
    )(x, g.reshape(1, D), tgt)


TB = 256


def _glu(a_val, a_gate):
    return a_val * jax.nn.sigmoid(a_gate)


PAIR = 2 * HEAD


def _pair_mean(x, lo):
    s_lo = jnp.sum(jnp.where(lo, x, 0.0), axis=-1, keepdims=True)
    s_hi = jnp.sum(jnp.where(lo, 0.0, x), axis=-1, keepdims=True)
    return jnp.where(lo, s_lo, s_hi) * (1.0 / HEAD)


def _pair_ln(v, g, b):
    lo = lax.broadcasted_iota(jnp.int32, v.shape, 1) < HEAD
    vc = v - _pair_mean(v, lo)
    var = _pair_mean(vc * vc, lo)
    return vc * lax.rsqrt(var + EPS) * g + b


def _ln_silu(v, g, b):
    return jax.nn.silu(_pair_ln(v, g, b))


def _conv_geom(kw):
    halo = 32 if kw > 9 else 16
    return halo, halo - (kw - 1)


def _residues(shifts):
    return sorted({s % 8 for s in shifts} - {0})


def _shift_copies(src_ref, cp_ref, res, rows, ls):
    for j, r in enumerate(res):
        cp_ref[j, :, ls] = src_ref[pl.ds(r, rows), ls]


def _shifted(src_ref, cp_ref, res, shift, size, ls):
    r = shift % 8
    if r == 0:
        return src_ref[pl.ds(shift, size), ls]
    return cp_ref[res.index(r), pl.ds(shift - r, size), ls]


def _conv_taps(hp_ref, hs_ref, w_ref, b_ref, acc_ref, kw, off, halo, width):
    res = _residues(range(off, off + kw))
    for c in range(width // 128):
        ls = pl.ds(c * 128, 128)
        _shift_copies(hp_ref, hs_ref, res, halo + TB, ls)
        acc = jnp.broadcast_to(b_ref[:, ls], (TB, 128))
        for k in range(kw):
            acc = acc + w_ref[k:k + 1, ls] * _shifted(hp_ref, hs_ref, res, off + k, TB, ls)
        acc_ref[:, ls] = acc


def _conv_fwd(name, src, col_block, w, b, kw, conformer, n_seq, ln_g=None, ln_b=None, share=None):
    T = src.shape[0]
    cout = w.shape[1]
    cin = 2 * cout if conformer else cout
    halo, off = _conv_geom(kw)
    nblk = T // n_seq // TB
    hb = TB // halo
    out_shape, out_blk, shared_in, aliases = _shared(share, T, cout, bf16 if conformer else f32, 6 if conformer else 4)

    def body(cur_ref, halo_ref, w_ref, b_ref, *rest):
        if conformer:
            g_ref, lb_ref = rest[:2]
            rest = rest[2:]
        out_ref, hp_ref, acc_ref, hs_ref = rest[len(shared_in):]
        i = pl.program_id(1)
        first = (i == 0)

        @pl.when((pl.program_id(0) == 0) & first)
        def _():
            hp_ref[pl.ds(halo + TB, 8), :] = jnp.zeros((8, cout), f32)

        if conformer:
            hp_ref[pl.ds(halo, TB), :] = _glu(cur_ref[:, 0:cout].astype(f32), cur_ref[:, cout:cin].astype(f32))
            hh = _glu(halo_ref[:, 0:cout].astype(f32), halo_ref[:, cout:cin].astype(f32))
        else:
            hp_ref[pl.ds(halo, TB), :] = cur_ref[...].astype(f32)
            hh = halo_ref[...].astype(f32)
        hp_ref[pl.ds(0, halo), :] = jnp.where(first, 0.0, hh)
        _conv_taps(hp_ref, hs_ref, w_ref, b_ref, acc_ref, kw, off, halo, cout)
        if conformer:
            for q in range(cout // PAIR):
                ls = pl.ds(q * PAIR, PAIR)
                out_ref[:, ls] = _ln_silu(acc_ref[:, ls], g_ref[:, ls], lb_ref[:, ls]).astype(out_ref.dtype)
        else:
            out_ref[...] = jax.nn.silu(acc_ref[...]).astype(out_ref.dtype)

    nres = len(_residues(range(off, off + kw)))

    row = pl.BlockSpec((1, cout), lambda s, i: (0, 0))
    in_specs = [pl.BlockSpec((TB, cin), lambda s, i: (s * nblk + i, col_block)),
                pl.BlockSpec((halo, cin), lambda s, i: (jnp.maximum((s * nblk + i) * hb - 1, 0), col_block)),
                pl.BlockSpec((w.shape[0], cout), lambda s, i: (0, 0)), row]
    args = [src, src, w, b.reshape(1, cout)]
    if conformer:
        in_specs += [row, row]
        args += [ln_g.reshape(1, cout), ln_b.reshape(1, cout)]
    in_specs += [pl.BlockSpec(memory_space=pl.ANY)] * len(shared_in)
    args += shared_in
    return pl.pallas_call(
        body, name=name, grid=(n_seq, nblk), in_specs=in_specs,
        out_specs=pl.BlockSpec((TB, cout), lambda s, i: (s * nblk + i, out_blk)),
        out_shape=out_shape, input_output_aliases=aliases,
        scratch_shapes=[pltpu.VMEM((halo + TB + 8, cout), f32), pltpu.VMEM((TB, cout), f32),
                        pltpu.VMEM((nres, halo + TB, cout), f32)],
        compiler_params=_cparams(("arbitrary", "arbitrary")),
    )(*args)


def _shared(share, T, width, dtype, n_inputs, out_index=0):
    if share is None:
        return _sds((T, width), dtype), 0, [], {}
    total, blk, into = share
    if into is None:
        return _sds((T, total), dtype), blk, [], {}
    return _sds((T, total), dtype), blk, [into], {n_inputs: out_index}


def _conv_bwd(name, src, col_block, w, b, dy, dy_col_block, kw, conformer, n_seq, ln_g=None, ln_b=None, share=None):
    T = src.shape[0]
    cout = w.shape[1]
    wrows = w.shape[0]
    cin = 2 * cout if conformer else cout
    halo, off = _conv_geom(kw)
    nblk = T // n_seq // TB
    hb = TB // halo
    dsrc_shape, dsrc_blk, shared_in, aliases = _shared(share, T, cin, bf16, 7 if conformer else 5)

    def body(cur_ref, halo_ref, w_ref, b_ref, dy_ref, *rest):
        if conformer:
            g_ref, lb_ref = rest[:2]
            rest = rest[2:]
        rest = rest[len(shared_in):]
        if conformer:
            dsrc_ref, dw_ref, db_ref, dg_ref, dlb_ref, hp_ref, acc_ref, dz_ref, dhp_ref, carry_ref, hs_ref, dzs_ref = rest
        else:
            dsrc_ref, dw_ref, db_ref, hp_ref, acc_ref, dz_ref, dhp_ref, carry_ref, hs_ref, dzs_ref = rest
        s, ii = pl.program_id(0), pl.program_id(1)
        i = nblk - 1 - ii
        first = (i == 0)

        @pl.when((s == 0) & (ii == 0))
        def _():
            dw_ref[...] = jnp.zeros_like(dw_ref)
            db_ref[...] = jnp.zeros_like(db_ref)
            hp_ref[pl.ds(halo + TB, 8), :] = jnp.zeros((8, cout), f32)
            if conformer:
                dg_ref[...] = jnp.zeros_like(dg_ref)
                dlb_ref[...] = jnp.zeros_like(dlb_ref)

        @pl.when(ii == 0)
        def _():
            carry_ref[...] = jnp.zeros_like(carry_ref)
            dz_ref[pl.ds(0, halo), :] = jnp.zeros((halo, cout), f32)
            dz_ref[pl.ds(halo + TB, halo), :] = jnp.zeros((halo, cout), f32)

        if conformer:
            hp_ref[pl.ds(halo, TB), :] = _glu(cur_ref[:, 0:cout].astype(f32), cur_ref[:, cout:cin].astype(f32))
            hh = _glu(halo_ref[:, 0:cout].astype(f32), halo_ref[:, cout:cin].astype(f32))
        else:
            hp_ref[pl.ds(halo, TB), :] = cur_ref[...].astype(f32)
            hh = halo_ref[...].astype(f32)
        hp_ref[pl.ds(0, halo), :] = jnp.where(first, 0.0, hh)
        _conv_taps(hp_ref, hs_ref, w_ref, b_ref, acc_ref, kw, off, halo, cout)

        if conformer:
            for q in range(cout // PAIR):
                ls = pl.ds(q * PAIR, PAIR)
                _, vjp = jax.vjp(_ln_silu, acc_ref[:, ls], g_ref[:, ls], lb_ref[:, ls])
                da, dg, dlb = vjp(dy_ref[:, ls].astype(f32))
                dz_ref[pl.ds(halo, TB), ls] = da
                dg_ref[0:1, ls] += dg
                dlb_ref[0:1, ls] += dlb
        else:
            _, vjp = jax.vjp(jax.nn.silu, acc_ref[...])
            dz_ref[pl.ds(halo, TB), :] = vjp(dy_ref[...].astype(f32))[0]

        res_h = _residues(range(off, off + kw))
        res_z = _residues(range(kw))
        for c in range(cout // 128):
            ls = pl.ds(c * 128, 128)
            _shift_copies(dz_ref, dzs_ref, res_z, halo + TB + halo - 8, ls)
            dacc = dz_ref[pl.ds(halo, TB), ls]
            db_ref[0:1, ls] += jnp.sum(dacc, axis=0, keepdims=True)
            dhp = jnp.zeros((halo + TB, 128), f32)
            for k in range(kw):
                dw_ref[k:k + 1, ls] += jnp.sum(dacc * _shifted(hp_ref, hs_ref, res_h, off + k, TB, ls), axis=0, keepdims=True)
                dhp = dhp + w_ref[k:k + 1, ls] * _shifted(dz_ref, dzs_ref, res_z, kw - 1 - k, halo + TB, ls)
            dhp_ref[:, ls] = dhp
        dhp_ref[pl.ds(TB, halo), :] += carry_ref[...]
        carry_ref[...] = dhp_ref[pl.ds(0, halo), :]
        dcur = dhp_ref[pl.ds(halo, TB), :]
        if conformer:
            _, vjp = jax.vjp(_glu, cur_ref[:, 0:cout].astype(f32), cur_ref[:, cout:cin].astype(f32))
            dval, dgate = vjp(dcur)
            dsrc_ref[:, 0:cout] = dval.astype(dsrc_ref.dtype)
            dsrc_ref[:, cout:cin] = dgate.astype(dsrc_ref.dtype)
        else:
            dsrc_ref[...] = dcur.astype(dsrc_ref.dtype)

    def blk(s, ii):
        return s * nblk + (nblk - 1 - ii)

    row = pl.BlockSpec((1, cout), lambda s, ii: (0, 0))
    acc8 = pl.BlockSpec((8, cout), lambda s, ii: (0, 0))
    in_specs = [pl.BlockSpec((TB, cin), lambda s, ii: (blk(s, ii), col_block)),
                pl.BlockSpec((halo, cin), lambda s, ii: (jnp.maximum(blk(s, ii) * hb - 1, 0), col_block)),
                pl.BlockSpec((wrows, cout), lambda s, ii: (0, 0)), row,
                pl.BlockSpec((TB, cout), lambda s, ii: (blk(s, ii), dy_col_block))]
    args = [src, src, w, b.reshape(1, cout), dy]
    out_specs = [pl.BlockSpec((TB, cin), lambda s, ii: (blk(s, ii), dsrc_blk)),
                 pl.BlockSpec((wrows, cout), lambda s, ii: (0, 0)), acc8]
    out_shape = [dsrc_shape, _sds((wrows, cout), f32), _sds((8, cout), f32)]
    if conformer:
        in_specs += [row, row]
        args += [ln_g.reshape(1, cout), ln_b.reshape(1, cout)]
        out_specs += [acc8, acc8]
        out_shape += [_sds((8, cout), f32), _sds((8, cout), f32)]
    in_specs += [pl.BlockSpec(memory_space=pl.ANY)] * len(shared_in)
    args += shared_in
    return pl.pallas_call(
        body, name=name, grid=(n_seq, nblk), in_specs=in_specs, out_specs=out_specs, out_shape=out_shape,
        input_output_aliases=aliases,
        scratch_shapes=[pltpu.VMEM((halo + TB + 8, cout), f32), pltpu.VMEM((TB, cout), f32),
                        pltpu.VMEM((halo + TB + halo, cout), f32), pltpu.VMEM((halo + TB, cout), f32),
                        pltpu.VMEM((halo, cout), f32),
                        pltpu.VMEM((len(_residues(range(off, off + kw))), halo + TB, cout), f32),
                        pltpu.VMEM((len(_residues(range(kw))), halo + TB + halo - 8, cout), f32)],
        compiler_params=_cparams(("arbitrary", "arbitrary")),
    )(*args)


def _gelu(x):
    return 0.5 * x * (1.0 + lax.erf(x * (1.0 / math.sqrt(2.0))))


def _tril_mask(n):
    r = lax.broadcasted_iota(jnp.int32, (n, n), 0)
    c = lax.broadcasted_iota(jnp.int32, (n, n), 1)
    return r >= c


def _head_spread(nh):
    r = lax.broadcasted_iota(jnp.int32, (nh, nh * HEAD), 0)
    c = lax.broadcasted_iota(jnp.int32, (nh, nh * HEAD), 1)
    return (c // HEAD == r).astype(f32)


def _gmlp_bias(bs):
    return lax.dot_general(bs, _head_spread(bs.shape[0]), (((0,), (0,)), ((), ())), precision=HI, preferred_element_type=f32)


def _gmlp_pair(bu, bv, g, b, w_a, w_b, bias):
    lo = lax.broadcasted_iota(jnp.int32, bu.shape, 1) < HEAD
    tril = _tril_mask(CHUNK)
    u = _gelu(bu)
    vn = _pair_ln(_gelu(bv), g, b)
    mix = jnp.where(lo, _bdot(jnp.where(tril, w_a, 0.0), vn, "nn"), _bdot(jnp.where(tril, w_b, 0.0), vn, "nn"))
    return u * (mix + bias)


def _gmlp_fwd(proj, col_block, ln_g, ln_b, w_s, b_s, share=None):
    T = proj.shape[0]
    nh = w_s.shape[0]
    width = nh * HEAD
    out_shape, out_blk, shared_in, aliases = _shared(share, T, width, bf16, 5)

    def body(p_ref, g_ref, b_ref, w_ref, bs_ref, *rest):
        out_ref, bias_ref = rest[len(shared_in):]

        @pl.when(pl.program_id(0) == 0)
        def _():
            bias_ref[...] = _gmlp_bias(bs_ref[...])

        for q in range(nh // 2):
            ls = pl.ds(q * PAIR, PAIR)
            lv = pl.ds(width + q * PAIR, PAIR)
            out_ref[:, ls] = _gmlp_pair(p_ref[:, ls].astype(f32), p_ref[:, lv].astype(f32), g_ref[:, ls], b_ref[:, ls], w_ref[2 * q], w_ref[2 * q + 1],
                                        bias_ref[:, ls]).astype(out_ref.dtype)

    row = pl.BlockSpec((1, width), lambda i: (0, 0))
    return pl.pallas_call(
        body, name="gmlp_fwd", grid=(T // CHUNK,),
        in_specs=[pl.BlockSpec((CHUNK, 2 * width), lambda i: (i, col_block)), row, row,
                  pl.BlockSpec((nh, CHUNK, CHUNK), lambda i: (0, 0, 0)), pl.BlockSpec((nh, CHUNK), lambda i: (0, 0))]
        + [pl.BlockSpec(memory_space=pl.ANY)] * len(shared_in),
        out_specs=pl.BlockSpec((CHUNK, width), lambda i: (i, out_blk)),
        out_shape=out_shape, input_output_aliases=aliases, scratch_shapes=[pltpu.VMEM((CHUNK, width), f32)],
        compiler_params=_cparams(("arbitrary",)),
    )(proj, ln_g.reshape(1, width), ln_b.reshape(1, width), w_s, b_s, *shared_in)


def _gmlp_bwd(proj, col_block, ln_g, ln_b, w_s, b_s, dy, dy_col_block, share=None):
    T = proj.shape[0]
    nh = w_s.shape[0]
    width = nh * HEAD
    nstep = T // CHUNK
    dp_shape, dp_blk, shared_in, aliases = _shared(share, T, 2 * width, bf16, 6)

    def body(p_ref, g_ref, b_ref, w_ref, bs_ref, dy_ref, *rest):
        dp_ref, dg_ref, db_ref, dw_ref, dbst_ref, bias_ref, dbias_ref = rest[len(shared_in):]

        @pl.when(pl.program_id(0) == 0)
        def _():
            dg_ref[...] = jnp.zeros_like(dg_ref)
            db_ref[...] = jnp.zeros_like(db_ref)
            dw_ref[...] = jnp.zeros_like(dw_ref)
            dbias_ref[...] = jnp.zeros_like(dbias_ref)
            bias_ref[...] = _gmlp_bias(bs_ref[...])

        for q in range(nh // 2):
            ls = pl.ds(q * PAIR, PAIR)
            lv = pl.ds(width + q * PAIR, PAIR)
            _, vjp = jax.vjp(_gmlp_pair, p_ref[:, ls].astype(f32), p_ref[:, lv].astype(f32), g_ref[:, ls], b_ref[:, ls], w_ref[2 * q], w_ref[2 * q + 1],
                             bias_ref[:, ls])
            dbu, dbv, dg, db, dw_a, dw_b, dbias = vjp(dy_ref[:, ls].astype(f32))
            dp_ref[:, ls] = dbu.astype(dp_ref.dtype)
            dp_ref[:, lv] = dbv.astype(dp_ref.dtype)
            dg_ref[0:1, ls] += dg
            db_ref[0:1, ls] += db
            dw_ref[2 * q] += dw_a
            dw_ref[2 * q + 1] += dw_b
            dbias_ref[:, ls] += dbias

        @pl.when(pl.program_id(0) == nstep - 1)
        def _():
            dbst_ref[...] = lax.dot_general(dbias_ref[...], _head_spread(nh), (((1,), (1,)), ((), ())),
                                            precision=HI, preferred_element_type=f32)

    row = pl.BlockSpec((1, width), lambda i: (0, 0))
    acc8 = pl.BlockSpec((8, width), lambda i: (0, 0))
    wspec = pl.BlockSpec((nh, CHUNK, CHUNK), lambda i: (0, 0, 0))
    res = pl.pallas_call(
        body, name="gmlp_bwd", grid=(nstep,),
        in_specs=[pl.BlockSpec((CHUNK, 2 * width), lambda i: (i, col_block)), row, row, wspec,
                  pl.BlockSpec((nh, CHUNK), lambda i: (0, 0)), pl.BlockSpec((CHUNK, width), lambda i: (i, dy_col_block))]
        + [pl.BlockSpec(memory_space=pl.ANY)] * len(shared_in),
        out_specs=[pl.BlockSpec((CHUNK, 2 * width), lambda i: (i, dp_blk)), acc8, acc8, wspec,
                   pl.BlockSpec((CHUNK, nh), lambda i: (0, 0))],
        out_shape=[dp_shape, _sds((8, width), f32), _sds((8, width), f32),
                   _sds((nh, CHUNK, CHUNK), f32), _sds((CHUNK, nh), f32)],
        input_output_aliases=aliases,
        scratch_shapes=[pltpu.VMEM((CHUNK, width), f32), pltpu.VMEM((CHUNK, width), f32)],
        compiler_params=_cparams(("arbitrary",)),
    )(proj, ln_g.reshape(1, width), ln_b.reshape(1, width), w_s, b_s, dy, *shared_in)
    return res[0], res[1], res[2], res[3], res[4].T


def _sel_col(x, h):
    lane = lax.broadcasted_iota(jnp.int32, x.shape, 1)
    return jnp.sum(jnp.where(lane == h, x, 0.0), axis=1, keepdims=True)


def _sel_row(x, h):
    sub = lax.broadcasted_iota(jnp.int32, x.shape, 0)
    return jnp.sum(jnp.where(sub == h, x, 0.0), axis=0, keepdims=True)


def _ssd_chunk(nh, ngrp, xs_l, z_l, b_l, c_l, dtraw, dtb, alog, dskip, ng_l, prev_l):
    hg = nh // ngrp
    tril = _tril_mask(CHUNK)
    tl = tril.astype(f32)
    lo = lax.broadcasted_iota(jnp.int32, (CHUNK, PAIR), 1) < HEAD
    lo_row = lo[0:1, :]
    dt = jax.nn.softplus(dtraw + dtb)
    a = dt * (-jnp.exp(alog))
    cs = jnp.dot(tl, a, precision=HI, preferred_element_type=f32)
    cst = lax.dot_general(a, tl, (((0,), (1,)), ((), ())), precision=HI, preferred_element_type=f32)
    cb_l = [_bdot(c_l[g], b_l[g], "nt") for g in range(ngrp)]
    yz_l, new_prev = [], []
    for q in range(nh // 2):
        g = (2 * q) // hg
        cols = []
        for h in (2 * q, 2 * q + 1):
            cs_h = _sel_col(cs, h)
            tot = _sel_row(cs_h, CHUNK - 1)
            seg = jnp.where(tril, cs_h - _sel_row(cst, h), 0.0)
            lmat = jnp.where(tril, jnp.exp(seg), 0.0)
            cols.append((_sel_col(dt, h), cs_h, tot, lmat, _sel_col(dskip, h)))
        (dt_a, cs_a, tot_a, l_a, dsk_a), (dt_b, cs_b, tot_b, l_b, dsk_b) = cols
        xs = xs_l[q]
        x = xs * jnp.where(lo, dt_a, dt_b)
        ydiag = jnp.where(lo, _bdot(cb_l[g] * l_a, x, "nn"), _bdot(cb_l[g] * l_b, x, "nn"))
        yoff = _bdot(c_l[g], prev_l[q], "nn") * jnp.where(lo, jnp.exp(cs_a), jnp.exp(cs_b))
        xdec = x * jnp.where(lo, jnp.exp(tot_a - cs_a), jnp.exp(tot_b - cs_b))
        st = _bdot(b_l[g], xdec, "tn")
        new_prev.append(prev_l[q] * jnp.where(lo_row, jnp.exp(tot_a), jnp.exp(tot_b)) + st)
        y = ydiag + yoff + jnp.where(lo_row, dsk_a, dsk_b) * xs
        yz_l.append(y * jax.nn.silu(z_l[q]))
    out = [None] * (nh // 2)
    qg = hg // 2
    for g in range(ngrp):
        ssq = sum(jnp.sum(yz_l[q] * yz_l[q], axis=-1, keepdims=True) for q in range(g * qg, (g + 1) * qg))
        r = lax.rsqrt(ssq * (1.0 / (hg * HEAD)) + EPS)
        for q in range(g * qg, (g + 1) * qg):
            out[q] = yz_l[q] * r * ng_l[q]
    return out, new_prev


def _ssd_read(nh, ngrp, nst, xbc_ref, z_ref, ng_ref, st_ref):
    cw = nh * HEAD
    xs_l = [xbc_ref[:, pl.ds(q * PAIR, PAIR)] for q in range(nh // 2)]
    b_l = [xbc_ref[:, pl.ds(cw + g * nst, nst)] for g in range(ngrp)]
    c_l = [xbc_ref[:, pl.ds(cw + ngrp * nst + g * nst, nst)] for g in range(ngrp)]
    z_l = [z_ref[:, pl.ds(q * PAIR, PAIR)].astype(f32) for q in range(nh // 2)]
    ng_l = [ng_ref[:, pl.ds(q * PAIR, PAIR)] for q in range(nh // 2)]
    prev_l = [st_ref[:, pl.ds(q * PAIR, PAIR)] for q in range(nh // 2)]
    return xs_l, z_l, b_l, c_l, ng_l, prev_l


def _ssd_fwd(xbc, proj, z_col_block, pdt, dtb, alog, dskip, ng, nh, ngrp, nst, n_seq, share=None):
    T = xbc.shape[0]
    cw = nh * HEAD
    nchunk = T // n_seq // CHUNK
    assert nst == CHUNK
    y_shape, y_blk, shared_in, aliases = _shared(share, T, cw, bf16, 7)

    def body(xbc_ref, z_ref, dt_ref, dtb_ref, alog_ref, dskip_ref, ng_ref, *rest):
        y_ref, sin_ref, st_ref = rest[len(shared_in):]

        @pl.when(pl.program_id(1) == 0)
        def _():
            st_ref[...] = jnp.zeros_like(st_ref)

        sin_ref[...] = st_ref[...]
        xs_l, z_l, b_l, c_l, ng_l, prev_l = _ssd_read(nh, ngrp, nst, xbc_ref, z_ref, ng_ref, st_ref)
        y_l, new_prev = _ssd_chunk(nh, ngrp, xs_l, z_l, b_l, c_l, dt_ref[...], dtb_ref[...], alog_ref[...],
                                   dskip_ref[...], ng_l, prev_l)
        for q in range(nh // 2):
            ls = pl.ds(q * PAIR, PAIR)
            y_ref[:, ls] = y_l[q].astype(y_ref.dtype)
            st_ref[:, ls] = new_prev[q]

    def blk(s, c):
        return s * nchunk + c

    prow = pl.BlockSpec((1, 128), lambda s, c: (0, 0))
    return pl.pallas_call(
        body, name="ssd_fwd", grid=(n_seq, nchunk),
        in_specs=[pl.BlockSpec((CHUNK, xbc.shape[1]), lambda s, c: (blk(s, c), 0)),
                  pl.BlockSpec((CHUNK, cw), lambda s, c: (blk(s, c), z_col_block)),
                  pl.BlockSpec((CHUNK, 128), lambda s, c: (blk(s, c), 0)),
                  prow, prow, prow, pl.BlockSpec((1, cw), lambda s, c: (0, 0))]
        + [pl.BlockSpec(memory_space=pl.ANY)] * len(shared_in),
        out_specs=[pl.BlockSpec((CHUNK, cw), lambda s, c: (blk(s, c), y_blk)),
                   pl.BlockSpec((nst, cw), lambda s, c: (blk(s, c), 0))],
        out_shape=[y_shape, _sds((T, cw), f32)], input_output_aliases=aliases,
        scratch_shapes=[pltpu.VMEM((nst, cw), f32)],
        compiler_params=_cparams(("arbitrary", "arbitrary")),
    )(xbc, proj, pdt, dtb, alog, dskip, ng.reshape(1, cw), *shared_in)


def _ssd_bwd(xbc, proj, z_col_block, pdt, dtb, alog, dskip, ng, sin, dy, dy_col_block, nh, ngrp, nst, n_seq, share=None):
    T, xw = xbc.shape
    cw = nh * HEAD
    nchunk = T // n_seq // CHUNK
    dz_shape, dz_blk, shared_in, aliases = _shared(share, T, cw, bf16, 9, out_index=1)

    def body(xbc_ref, z_ref, dt_ref, dtb_ref, alog_ref, dskip_ref, ng_ref, sin_ref, dy_ref, *rest):
        dxbc_ref, dz_ref, ddt_ref, ddtb_ref, dalog_ref, ddskip_ref, dng_ref, dst_ref = rest[len(shared_in):]
        s, cc = pl.program_id(0), pl.program_id(1)

        @pl.when((s == 0) & (cc == 0))
        def _():
            ddtb_ref[...] = jnp.zeros_like(ddtb_ref)
            dalog_ref[...] = jnp.zeros_like(dalog_ref)
            ddskip_ref[...] = jnp.zeros_like(ddskip_ref)
            dng_ref[...] = jnp.zeros_like(dng_ref)

        @pl.when(cc == 0)
        def _():
            dst_ref[...] = jnp.zeros_like(dst_ref)

        xs_l, z_l, b_l, c_l, ng_l, prev_l = _ssd_read(nh, ngrp, nst, xbc_ref, z_ref, ng_ref, sin_ref)
        _, vjp = jax.vjp(functools.partial(_ssd_chunk, nh, ngrp), xs_l, z_l, b_l, c_l, dt_ref[...], dtb_ref[...],
                         alog_ref[...], dskip_ref[...], ng_l, prev_l)
        dy_l = [dy_ref[:, pl.ds(q * PAIR, PAIR)].astype(f32) for q in range(nh // 2)]
        dst_l = [dst_ref[:, pl.ds(q * PAIR, PAIR)] for q in range(nh // 2)]
        dxs_l, dz_l, db_l, dc_l, ddt, ddtb, dalog, ddskip, dng_l, dprev_l = vjp((dy_l, dst_l))
        for q in range(nh // 2):
            ls = pl.ds(q * PAIR, PAIR)
            dxbc_ref[:, ls] = dxs_l[q]
            dz_ref[:, ls] = dz_l[q].astype(dz_ref.dtype)
            dng_ref[0:1, ls] += dng_l[q]
            dst_ref[:, ls] = dprev_l[q]
        for g in range(ngrp):
            dxbc_ref[:, pl.ds(cw + g * nst, nst)] = db_l[g]
            dxbc_ref[:, pl.ds(cw + ngrp * nst + g * nst, nst)] = dc_l[g]
        ddt_ref[...] = ddt.astype(ddt_ref.dtype)
        ddtb_ref[0:1, :] += ddtb
        dalog_ref[0:1, :] += dalog
        ddskip_ref[0:1, :] += ddskip

    def blk(s, cc):
        return s * nchunk + (nchunk - 1 - cc)

    prow = pl.BlockSpec((1, 128), lambda s, c: (0, 0))
    pacc = pl.BlockSpec((8, 128), lambda s, c: (0, 0))
    return pl.pallas_call(
        body, name="ssd_bwd", grid=(n_seq, nchunk),
        in_specs=[pl.BlockSpec((CHUNK, xw), lambda s, c: (blk(s, c), 0)),
                  pl.BlockSpec((CHUNK, cw), lambda s, c: (blk(s, c), z_col_block)),
                  pl.BlockSpec((CHUNK, 128), lambda s, c: (blk(s, c), 0)),
                  prow, prow, prow, pl.BlockSpec((1, cw), lambda s, c: (0, 0)),
                  pl.BlockSpec((nst, cw), lambda s, c: (blk(s, c), 0)),
                  pl.BlockSpec((CHUNK, cw), lambda s, c: (blk(s, c), dy_col_block))]
        + [pl.BlockSpec(memory_space=pl.ANY)] * len(shared_in),
        out_specs=[pl.BlockSpec((CHUNK, xw), lambda s, c: (blk(s, c), 0)),
                   pl.BlockSpec((CHUNK, cw), lambda s, c: (blk(s, c), dz_blk)),
                   pl.BlockSpec((CHUNK, 128), lambda s, c: (blk(s, c), 0)),
                   pacc, pacc, pacc, pl.BlockSpec((8, cw), lambda s, c: (0, 0))],
        out_shape=[_sds((T, xw), f32), dz_shape, _sds((T, 128), bf16),
                   _sds((8, 128), f32), _sds((8, 128), f32), _sds((8, 128), f32), _sds((8, cw), f32)],
        input_output_aliases=aliases,
        scratch_shapes=[pltpu.VMEM((nst, cw), f32)],
        compiler_params=_cparams(("arbitrary", "arbitrary")),
    )(xbc, proj, pdt, dtb, alog, dskip, ng.reshape(1, cw), sin, dy, *shared_in)


_HBM = pl.BlockSpec(memory_space=pltpu.HBM)
_SEM = pl.BlockSpec(memory_space=pltpu.SEMAPHORE)
_EFFECT = pltpu.SideEffectType.DATAFLOW_SIDE_EFFECTING


def _split_copies(n, scatter, src_refs, land_refs, send_sems, recv_sems):
    npeer = N_DEV - 1
    x, y, c = lax.axis_index("x"), lax.axis_index("y"), lax.axis_index("c")
    me = 4 * x + 2 * y + c
    copies = []
    for i in range(n):
        for k in range(1, N_DEV):
            px = 1 - x if k & 4 else x
            py = 1 - y if k & 2 else y
            pc = 1 - c if k & 1 else c
            src = src_refs[i].at[4 * px + 2 * py + pc] if scatter else src_refs[i]
            copies.append(pltpu.make_async_remote_copy(
                src_ref=src, dst_ref=land_refs[i].at[me],
                send_sem=send_sems.at[i * npeer + k - 1], recv_sem=recv_sems.at[i * npeer + k - 1],
                device_id=(px, py, pc), device_id_type=pl.DeviceIdType.MESH))
    return copies


def _exchange_start(name, arrs, scatter):
    n = len(arrs)
    nsem = n * (N_DEV - 1)
    me = 4 * lax.axis_index("x") + 2 * lax.axis_index("y") + lax.axis_index("c")
    lands = []
    for a in arrs:
        own = lax.dynamic_index_in_dim(a, me, 0, keepdims=True) if scatter else a[None]
        full = lax.empty(a.shape if scatter else (N_DEV,) + a.shape, a.dtype)
        lands.append(lax.dynamic_update_slice(full, own, (me,) + (0,) * (full.ndim - 1)))

    def body(*refs):
        src_refs, land_refs = refs[:n], refs[n:2 * n]
        send_sems, recv_sems = refs[2 * n], refs[2 * n + 1]
        token = refs[-1]
        for cp in _split_copies(n, scatter, src_refs, land_refs, send_sems, recv_sems):
            cp.start()
        token[...] = jnp.zeros_like(token)

    res = pl.pallas_call(
        body, name=name,
        out_shape=(pltpu.SemaphoreType.DMA((nsem,)), pltpu.SemaphoreType.DMA((nsem,)),
                   *[pltpu.HBM(a.shape, a.dtype) for a in arrs], *[pltpu.HBM(l.shape, l.dtype) for l in lands],
                   _sds((8, 128), f32)),
        in_specs=[_HBM] * (2 * n),
        out_specs=(_SEM, _SEM, *[_HBM] * (2 * n), pl.BlockSpec(memory_space=pltpu.VMEM)),
        input_output_aliases={j: 2 + j for j in range(2 * n)},
        compiler_params=pltpu.CompilerParams(has_side_effects=_EFFECT),
    )(*[pltpu.with_memory_space_constraint(a, pltpu.HBM) for a in arrs],
      *[pltpu.with_memory_space_constraint(l, pltpu.HBM) for l in lands])
    return (n, scatter, res[0], res[1], res[2:2 + n], res[2 + n:2 + 2 * n]), res[-1]


def _exchange_wait(name, handle, after):
    n, scatter, send_sems, recv_sems, srcs, lands = handle
    after = list(after) if isinstance(after, (list, tuple)) else [after]

    def body(*refs):
        src_refs, land_refs = refs[:n], refs[n:2 * n]
        for cp in _split_copies(n, scatter, src_refs, land_refs, refs[2 * n], refs[2 * n + 1]):
            cp.wait_send()
            cp.wait_recv()

    res = pl.pallas_call(
        body, name=name,
        out_shape=[pltpu.HBM(a.shape, a.dtype) for a in (*srcs, *lands)],
        in_specs=[_HBM] * (2 * n) + [_SEM, _SEM] + [pl.BlockSpec(memory_space=pl.ANY)] * len(after),
        out_specs=[_HBM] * (2 * n),
        input_output_aliases={j: j for j in range(2 * n)},
        compiler_params=pltpu.CompilerParams(has_side_effects=_EFFECT),
    )(*srcs, *lands, send_sems, recv_sems, *after)
    return res[n:]


def _adam_tiles(R, C):
    if R % 256 == 0:
        return (256, C), (R // 256, 1)
    assert C % 128 == 0
    return (R, 128), (1, C // 128)


def _adam(name, parts, w, m, v, layer=None, depth=None, into=None, stacked_in=False):
    P, R, C = parts.shape
    (tr, tc), (gr, gc) = _adam_tiles(R, C)
    c1 = 1.0 / (1.0 - ADAM_B1 ** ADAM_STEP)
    c2 = 1.0 / (1.0 - ADAM_B2 ** ADAM_STEP)
    into = [] if into is None else list(into)

    def body(p_ref, w_ref, m_ref, v_ref, *rest):
        g_ref, d_ref, nm_ref, nv_ref = rest[len(into):]
        g = p_ref[0].astype(f32)
        for s in range(1, P):
            g = g + p_ref[s].astype(f32)
        nm = ADAM_B1 * m_ref[...] + (1.0 - ADAM_B1) * g
        nv = ADAM_B2 * v_ref[...] + (1.0 - ADAM_B2) * (g * g)
        g_ref[...] = g
        nm_ref[...] = nm
        nv_ref[...] = nv
        d_ref[...] = -ADAM_LR * ((nm * c1) / (jnp.sqrt(nv * c2) + ADAM_EPS) + ADAM_WD * w_ref[...])

    tile = pl.BlockSpec((tr, tc), lambda i, j: (i, j))
    layer_tile = pl.BlockSpec((None, tr, tc), lambda i, j: (layer, i, j))
    out_tile, out_sds = (tile, _sds((R, C), f32)) if layer is None else (layer_tile, _sds((depth, R, C), f32))
    return pl.pallas_call(
        body, name=name, grid=(gr, gc),
        in_specs=[pl.BlockSpec((P, tr, tc), lambda i, j: (0, i, j))] + [layer_tile if stacked_in else tile] * 3
        + [pl.BlockSpec(memory_space=pl.ANY)] * len(into),
        out_specs=[out_tile] * 4, out_shape=[out_sds] * 4,
        input_output_aliases={4 + k: k for k in range(len(into))},
        compiler_params=_cparams(("arbitrary", "arbitrary")),
    )(parts, w, m, v, *into)


def _sum_parts(name, parts):
    P, R, C = parts.shape
    tr = 256 if R % 256 == 0 else R

    def body(p_ref, o_ref):
        g = p_ref[0]
        for s in range(1, P):
            g = g + p_ref[s]
        o_ref[...] = g

    return pl.pallas_call(
        body, name=name, grid=(R // tr,),
        in_specs=[pl.BlockSpec((P, tr, C), lambda i: (0, i, 0))], out_specs=pl.BlockSpec((tr, C), lambda i: (i, 0)),
        out_shape=_sds((R, C), f32), compiler_params=_cparams(("arbitrary",)),
    )(parts)


def _pad_to(a, n, axis):
    if a.shape[axis] == n:
        return a
    cfg = [(0, 0)] * a.ndim
    cfg[axis] = (0, n - a.shape[axis])
    return jnp.pad(a, cfg)


def _pack(arrs):
    flat = [_pad_to(a.reshape(-1), -(-a.size // 128) * 128, 0) for a in arrs]
    rows = jnp.concatenate(flat).reshape(-1, 128)
    return _pad_to(rows, -(-rows.shape[0] // 256) * 256, 0)


def _unpack(slab, shapes):
    flat = slab.reshape(-1)
    out, o = [], 0
    for s in shapes:
        n = math.prod(s)
        out.append(flat[o:o + n].reshape(s))
        o += -(-n // 128) * 128
    return out


_NAMES = ['norm1_g', 'w_in', 'conv_a_w', 'conv_a_b', 'ln_a_g', 'ln_a_b', 'ln_b_g', 'ln_b_b', 'w_spatial', 'b_spatial',
          'conv_c_w', 'conv_c_b', 'dt_bias', 'a_log', 'd_skip', 'norm_c_g', 'w_out', 'norm2_g', 'w_ff1', 'w_ff2', 'final_g']
_REPL = ['norm1_g', 'conv_a_b', 'ln_a_g', 'ln_a_b', 'ln_b_g', 'ln_b_b', 'w_spatial', 'b_spatial', 'conv_c_b',
         'dt_bias', 'a_log', 'd_skip', 'norm_c_g', 'norm2_g']
_CONVW = ['conv_a_w', 'conv_c_w']
_BIG = ['w_in', 'w_out', 'w_ff1', 'w_ff2']
_BIG_T = {'w_in': True, 'w_out': False, 'w_ff1': True, 'w_ff2': False}


def _row128(v):
    return _pad_to(v.reshape(1, -1), 128, 1)


def _step(p, m, v, x, loss_target):
    nb, S, D = x.shape
    T = nb * S
    depth = p['norm1_g'].shape[0]
    a_w = p['conv_a_b'].shape[1]
    b_w = p['ln_b_g'].shape[1]
    nh = p['dt_bias'].shape[1]
    c_w = p['norm_c_g'].shape[1]
    xw = p['conv_c_b'].shape[1]
    ngrp = 2
    nst = (xw - c_w) // (2 * ngrp)
    d_in = p['w_in'].shape[2] * N_DEV
    main = d_in - nh
    assert main == 2 * a_w + 2 * b_w + c_w + xw and 2 * a_w == 2 * b_w == c_w and xw % c_w == c_w // 2
    me = 4 * lax.axis_index("x") + 2 * lax.axis_index("y") + lax.axis_index("c")

    x2 = x.reshape(T, D)
    tgt = loss_target.reshape(T, D)

    def shards(i, z=None):
        z = 0.0 if z is None else z
        return [(p['w_in'][i].T + z).astype(bf16), (p['w_out'][i] + z).astype(bf16), (p['w_ff1'][i].T + z).astype(bf16),
                (p['w_ff2'][i] + z).astype(bf16), p['conv_a_w'][i], p['conv_c_w'][i]]

    def gathered_in(wt, ca, cc):
        wt = wt.reshape(d_in, D)
        ca = jnp.transpose(ca, (1, 0, 2)).reshape(KA, a_w)
        cc = jnp.transpose(cc, (1, 0, 2)).reshape(KC, xw)
        return dict(wt_main=wt[:main], wt_dt=_pad_to(wt[main:], 128, 0), ca=_pad_to(ca, 32, 0), cc=_pad_to(cc, 8, 0))

    def start_layer(i, after=None):
        sh = shards(i, None if after is None else after[0, 0])
        ha, t = _exchange_start("gather_w%da_start" % i, [sh[0], sh[4], sh[5]], False)
        hb, t = _exchange_start("gather_w%db_start" % i, [shards(i, t[0, 0])[1]], False)
        hc, t = _exchange_start("gather_w%dc_start" % i, [shards(i, t[0, 0])[2]], False)
        hd, t = _exchange_start("gather_w%dd_start" % i, [shards(i, t[0, 0])[3]], False)
        return dict(a=ha, b=hb, c=hc, d=hd), t

    W, saved = [], []
    xc = x2
    H, tok = start_layer(0)
    for i in range(depth):
        w = gathered_in(*_exchange_wait("gather_w%da_wait" % i, H['a'], [xc, tok]))
        W.append(w)
        Hi = H
        h1, rtok = _rms_fwd(xc, p['norm1_g'][i])
        if i + 1 < depth:
            H, tok = start_layer(i + 1, rtok + tok)
        else:
            tok = None
        (proj,) = _mm("mm_proj", h1, w['wt_main'], "nt", [bf16], dep=tok)
        (pdt,) = _mm("mm_pdt", h1, w['wt_dt'], "nt", [f32])
        mixw = a_w + b_w + c_w
        ycat = _conv_fwd("confa_fwd", proj, 0, w['ca'], p['conv_a_b'][i], KA, True, nb, p['ln_a_g'][i], p['ln_a_b'][i],
                         share=(mixw, 0, None))
        ycat = _gmlp_fwd(proj, 1, p['ln_b_g'][i], p['ln_b_b'][i], p['w_spatial'][i], p['b_spatial'][i], share=(mixw, 1, ycat))
        xbc = _conv_fwd("convc_fwd", proj, 2, w['cc'], p['conv_c_b'][i], KC, False, nb)
        dtb, alog, dsk = _row128(p['dt_bias'][i]), _row128(p['a_log'][i]), _row128(p['d_skip'][i])
        ycat, sin = _ssd_fwd(xbc, proj, 2, pdt, dtb, alog, dsk, p['norm_c_g'][i], nh, ngrp, nst, nb, share=(mixw, 1, ycat))
        w['wout'] = _exchange_wait("gather_w%db_wait" % i, Hi['b'], ycat)[0].reshape(-1, D)
        xm, h2 = _mm("mm_out", ycat, w['wout'], "nn", [f32, bf16], _ep_add_rms, (xc,), rows=(p['norm2_g'][i].reshape(1, D),))
        w['w1t'] = _exchange_wait("gather_w%dc_wait" % i, Hi['c'], h2)[0].reshape(-1, D)
        f, a = _mm("mm_ff1", h2, w['w1t'], "nt", [bf16, bf16], _ep_relu2)
        w['w2'] = _exchange_wait("gather_w%dd_wait" % i, Hi['d'], a)[0].reshape(-1, D)
        (xo,) = _mm("mm_ff2", a, w['w2'], "nn", [f32], _ep_add, (xm,))
        saved.append(dict(x_in=xc, h1=h1, proj=proj, pdt=pdt, xbc=xbc, sin=sin, ycat=ycat, xm=xm, h2=h2, f=f, a=a,
                          dtb=dtb, alog=alog, dsk=dsk))
        xc = xo

    lp, dx, dfinal = _loss_head(xc, p['final_g'], tgt)
    loss = lax.psum(lp[0, 0], ("x", "y", "c"))

    out = {}
    kinds = ("grad", "delta", "new_m", "new_v")
    names1 = _REPL + _CONVW

    started, small = [], [None] * depth

    def send(n, i, g):
        handle, token = _exchange_start("scatter_%s_%d_start" % (n, i), [g.reshape(N_DEV, -1, D)], True)
        started.append((n, i, handle))
        return token

    tok = None
    for i in reversed(range(depth)):
        w, sv = W[i], saved[i]
        (df,) = _mm("mm_df", dx, w['w2'], "nt", [bf16], _ep_drelu2, (sv['f'],), dep=tok)
        (gw2,) = _mm("mm_gw2", sv['a'], dx, "tn", [bf16])
        tok = send('w_ff2', i, gw2)
        dxm, dg2 = _mm("mm_dh2", df, w['w1t'], "nn", [f32], _ep_rms_bwd, (sv['xm'], dx), dep=tok,
                       rows=(p['norm2_g'][i].reshape(1, D),), n_row_out=1)
        (gw1t,) = _mm("mm_gw1", df, sv['h2'], "tn", [bf16])
        tok = send('w_ff1', i, gw1t)
        (dycat,) = _mm("mm_dycat", dxm, w['wout'], "nt", [bf16], dep=tok)
        (gwout,) = _mm("mm_gwout", sv['ycat'], dxm, "tn", [bf16])
        tok = send('w_out', i, gwout)
        dproj, dwa, dba, dlag, dlab = _conv_bwd("confa_bwd", sv['proj'], 0, w['ca'], p['conv_a_b'][i] + tok[0, 0], dycat, 0, KA,
                                                True, nb, p['ln_a_g'][i], p['ln_a_b'][i], share=(main, 0, None))
        dproj, dlbg, dlbb, dws, dbs = _gmlp_bwd(sv['proj'], 1, p['ln_b_g'][i], p['ln_b_b'][i], p['w_spatial'][i],
                                                p['b_spatial'][i], dycat, 1, share=(main, 1, dproj))
        dxbc, dproj, ddt, ddtb, dalog, ddsk, dng = _ssd_bwd(sv['xbc'], sv['proj'], 2, sv['pdt'], sv['dtb'], sv['alog'],
                                                            sv['dsk'], p['norm_c_g'][i], sv['sin'], dycat, 1, nh, ngrp, nst, nb,
                                                            share=(main, 2, dproj))
        dproj, dwc, dbc = _conv_bwd("convc_bwd", sv['proj'], 2, w['cc'], p['conv_c_b'][i], dxbc, 0, KC, False, nb,
                                    share=(main, 2, dproj))
        (dh_main,) = _mm("mm_dh1", dproj, w['wt_main'], "nn", [f32])
        (gwt_main,) = _mm("mm_gwin", dproj, sv['h1'], "tn", [bf16])
        (gwt_dt,) = _mm("mm_gwdt", ddt, sv['h1'], "tn", [bf16])
        tok = send('w_in', i, jnp.concatenate([gwt_main, gwt_dt[:nh]], axis=0))
        dx, dg1 = _mm("mm_dh1dt", ddt, w['wt_dt'], "nn", [f32], _ep_add_rms_bwd, (dh_main, sv['x_in'], dxm), dep=tok,
                      rows=(p['norm1_g'][i].reshape(1, D),), n_row_out=1)

        gi = dict(norm1_g=dg1[0], norm2_g=dg2[0], conv_a_w=dwa[:KA], conv_a_b=dba[0], ln_a_g=dlag[0], ln_a_b=dlab[0],
                  ln_b_g=dlbg[0], ln_b_b=dlbb[0], w_spatial=dws, b_spatial=dbs, conv_c_w=dwc[:KC], conv_c_b=dbc[0],
                  dt_bias=ddtb[0, :nh], a_log=dalog[0, :nh], d_skip=ddsk[0, :nh], norm_c_g=dng[0])
        parts_i = [gi[n] for n in names1] + ([dfinal[0]] if i == depth - 1 else [])
        handle, tok = _exchange_start("gather_g%d_start" % i, [_pack(parts_i)], False)
        small[i] = ([a.shape for a in parts_i], handle)
    grad_x = dx.reshape(nb, S, D)

    dep = [dx, tok]
    for n, i, handle in started:
        (parts,) = _exchange_wait("scatter_%s_%d_wait" % (n, i), handle, dep)
        prev = [out[(kind, n)] for kind in kinds] if (kinds[0], n) in out else None
        if _BIG_T[n]:
            res = _adam("adam_%s_%d" % (n, i), parts, p[n][i].T, m[n][i].T, v[n][i].T, layer=i, depth=depth, into=prev)
        else:
            res = _adam("adam_%s_%d" % (n, i), parts, p[n], m[n], v[n], layer=i, depth=depth, into=prev, stacked_in=True)
        for kind, r in zip(kinds, res):
            out[(kind, n)] = r
        dep = res[3]

    gsum = [None] * depth
    for i in reversed(range(depth)):
        (parts,) = _exchange_wait("gather_g%d_wait" % i, small[i][1], dep)
        gsum[i] = _sum_parts("sum_small", parts)
        dep = gsum[i]
    widths = [-(-math.prod(p[n].shape[1:]) // 128) * 128 for n in _REPL]
    rep_rows = sum(widths) // 128
    tot_rows = -(-depth * rep_rows // 256) * 256

    def rep_slab(q):
        cols = [_pad_to(q[n].reshape(depth, -1), wd, 1) for n, wd in zip(_REPL, widths)]
        return _pad_to(jnp.concatenate(cols, axis=1).reshape(-1, 128), tot_rows, 0)

    g_rep = _pad_to(jnp.concatenate([g[:rep_rows] for g in gsum], axis=0), tot_rows, 0)
    res = _adam("adam_small", g_rep[None], rep_slab(p), rep_slab(m), rep_slab(v))
    for kind, r in zip(kinds, res):
        view = r[:depth * rep_rows].reshape(depth, -1)
        o = 0
        for n, wd in zip(_REPL, widths):
            out[(kind, n)] = view[:, o:o + math.prod(p[n].shape[1:])].reshape(p[n].shape)
            o += wd

    extra = []
    for i in range(depth):
        tail = _unpack(gsum[i][rep_rows:], small[i][0][len(_REPL):])
        extra.append(tail)
    gconv = []
    for j, n in enumerate(_CONVW):
        cw_shard = p[n].shape[2]
        full = jnp.stack([extra[i][j] for i in range(depth)])
        gconv.append(lax.dynamic_slice_in_dim(full, me * cw_shard, cw_shard, axis=2))
    tail_names = _CONVW + ['final_g']
    res = _adam("adam_conv", _pack(gconv + [extra[depth - 1][len(_CONVW)]])[None],
                *[_pack([q[n] for n in tail_names]) for q in (p, m, v)])
    for kind, r in zip(kinds, res):
        for n, arr in zip(tail_names, _unpack(r, [p[n].shape for n in tail_names])):
            out[(kind, n)] = arr
    for n in _BIG:
        if _BIG_T[n]:
            for kind in kinds:
                out[(kind, n)] = jnp.swapaxes(out[(kind, n)], 1, 2)

    flat = [loss, grad_x]
    for kind in ("grad", "delta", "new_m", "new_v"):
        flat += [out[(kind, n)] for n in _NAMES]
    return tuple(flat)


def kernel(x, norm1_g, w_in, conv_a_w, conv_a_b, ln_a_g, ln_a_b, ln_b_g, ln_b_b, w_spatial, b_spatial, conv_c_w, conv_c_b, dt_bias, a_log, d_skip, norm_c_g, w_out, norm2_g, w_ff1, w_ff2, final_g, loss_target, m_norm1_g, m_w_in, m_conv_a_w, m_conv_a_b, m_ln_a_g, m_ln_a_b, m_ln_b_g, m_ln_b_b, m_w_spatial, m_b_spatial, m_conv_c_w, m_conv_c_b, m_dt_bias, m_a_log, m_d_skip, m_norm_c_g, m_w_out, m_norm2_g, m_w_ff1, m_w_ff2, m_final_g, v_norm1_g, v_w_in, v_conv_a_w, v_conv_a_b, v_ln_a_g, v_ln_a_b, v_ln_b_g, v_ln_b_b, v_w_spatial, v_b_spatial, v_conv_c_w, v_conv_c_b, v_dt_bias, v_a_log, v_d_skip, v_norm_c_g, v_w_out, v_norm2_g, v_w_ff1, v_w_ff2, v_final_g):
    p = dict(zip(_NAMES, (norm1_g, w_in, conv_a_w, conv_a_b, ln_a_g, ln_a_b, ln_b_g, ln_b_b, w_spatial, b_spatial, conv_c_w,
                          conv_c_b, dt_bias, a_log, d_skip, norm_c_g, w_out, norm2_g, w_ff1, w_ff2, final_g)))
    m = dict(zip(_NAMES, (m_norm1_g, m_w_in, m_conv_a_w, m_conv_a_b, m_ln_a_g, m_ln_a_b, m_ln_b_g, m_ln_b_b, m_w_spatial,
                          m_b_spatial, m_conv_c_w, m_conv_c_b, m_dt_bias, m_a_log, m_d_skip, m_norm_c_g, m_w_out, m_norm2_g,
                          m_w_ff1, m_w_ff2, m_final_g)))
    v = dict(zip(_NAMES, (v_norm1_g, v_w_in, v_conv_a_w, v_conv_a_b, v_ln_a_g, v_ln_a_b, v_ln_b_g, v_ln_b_b, v_w_spatial,
                          v_b_spatial, v_conv_c_w, v_conv_c_b, v_dt_bias, v_a_log, v_d_skip, v_norm_c_g, v_w_out, v_norm2_g,
                          v_w_ff1, v_w_ff2, v_final_g)))
    return _step(p, m, v, x, loss_target)
```

```python
import functools
import math

import jax
import jax.numpy as jnp
from jax import lax
from jax.experimental import pallas as pl
from jax.experimental.pallas import tpu as pltpu

f32 = jnp.float32
bf16 = jnp.bfloat16
HI = lax.Precision.HIGHEST
EPS = 1e-5
HEAD = 64
CHUNK = 128
KA = 31
KC = 4
N_DEV = 8
VMEM_LIMIT = 56 * 1024 * 1024
MM_VMEM_BUDGET = 52 * 1024 * 1024

ADAM_LR = 0.001
ADAM_B1 = 0.9
ADAM_B2 = 0.999
ADAM_EPS = 1e-08
ADAM_WD = 0.01
ADAM_STEP = 10


def _cparams(sem=None):
    return pltpu.CompilerParams(dimension_semantics=sem, vmem_limit_bytes=VMEM_LIMIT)


def _sds(shape, dtype):
    return jax.ShapeDtypeStruct(shape, dtype)


_DIMS = {"nn": ((1,), (0,)), "nt": ((1,), (1,)), "tn": ((0,), (0,))}


def _dot16(a, b, form):
    return lax.dot_general(a.astype(bf16), b.astype(bf16), (_DIMS[form], ((), ())), preferred_element_type=f32)


@functools.partial(jax.custom_vjp, nondiff_argnums=(2,))
def _bdot(a, b, form):
    return _dot16(a, b, form)


def _bdot_fwd(a, b, form):
    return _dot16(a, b, form), (a, b)


def _bdot_bwd(form, res, ct):
    a, b = res
    if form == "nn":
        da, db = _dot16(ct, b, "nt"), _dot16(a, ct, "tn")
    elif form == "nt":
        da, db = _dot16(ct, b, "nn"), _dot16(ct, a, "tn")
    else:
        da, db = _dot16(b, ct, "nt"), _dot16(a, ct, "nn")
    return da.astype(a.dtype), db.astype(b.dtype)


_bdot.defvjp(_bdot_fwd, _bdot_bwd)


def _tile(n, cap):
    if n <= cap:
        return n
    for d in range(cap - cap % 128, 0, -128):
        if n % d == 0:
            return d
    raise ValueError((n, cap))


def _mm(name, a, b, form, out_dtypes, epilogue=None, extras=(), tm=2048, tn=1024, tk=8192, dep=None, rows=(), n_row_out=0,
        n=None):
    if form == "tn":
        K, M = a.shape
    else:
        M, K = a.shape
    N = n if n is not None else (b.shape[0] if form == "nt" else b.shape[1])
    tm, tn, tk = _tile(M, tm), _tile(N, tn), _tile(K, tk)
    nk = K // tk

    def vmem_bytes(tm):
        mn = sum(jnp.dtype(e.dtype).itemsize for e in extras) + sum(jnp.dtype(d).itemsize for d in out_dtypes)
        return 2 * (tm * tk * a.dtype.itemsize + tk * tn * b.dtype.itemsize + tm * tn * mn) + 2 * tm * tn * 4

    while vmem_bytes(tm) > MM_VMEM_BUDGET and tm % 256 == 0:
        tm //= 2
    ne, no, nr = len(extras), len(out_dtypes), len(rows)
    assert n_row_out == 0 or tn == N
    deps = () if dep is None else (dep,)
    if epilogue is None:
        epilogue = lambda acc: (acc,)

    def body(a_ref, b_ref, *rest):
        extra_refs, row_refs = rest[:ne], rest[ne:ne + nr]
        rest = rest[ne + nr + len(deps):]
        out_refs, rowout_refs = rest[:no], rest[no:no + n_row_out]
        part = lax.dot_general(a_ref[...].astype(bf16), b_ref[...].astype(bf16),
                               (_DIMS[form], ((), ())), preferred_element_type=f32)

        def finish(acc):
            outs = epilogue(acc, *[e[...] for e in extra_refs], *[r[...] for r in row_refs])
            for o_ref, v in zip(out_refs, outs[:no]):
                o_ref[...] = v.astype(o_ref.dtype)
            for r_ref, v in zip(rowout_refs, outs[no:]):
                @pl.when(pl.program_id(0) == 0)
                def _():
                    r_ref[...] = jnp.zeros_like(r_ref)

                r_ref[0:1, :] += v

        if nk == 1:
            finish(part)
            return
        acc_ref = rest[no + n_row_out]
        k = pl.program_id(2)

        @pl.when(k == 0)
        def _():
            acc_ref[...] = part

        @pl.when((k > 0) & (k < nk - 1))
        def _():
            acc_ref[...] += part

        @pl.when(k == nk - 1)
        def _():
            finish(acc_ref[...] + part)

    a_spec = pl.BlockSpec((tk, tm), lambda i, j, k: (k, i)) if form == "tn" else pl.BlockSpec((tm, tk), lambda i, j, k: (i, k))
    b_spec = pl.BlockSpec((tn, tk), lambda i, j, k: (j, k)) if form == "nt" else pl.BlockSpec((tk, tn), lambda i, j, k: (k, j))
    mn_spec = pl.BlockSpec((tm, tn), lambda i, j, k: (i, j))
    return pl.pallas_call(
        body, name=name, grid=(M // tm, N // tn, nk),
        in_specs=[a_spec, b_spec] + [mn_spec] * ne + [pl.BlockSpec((1, tn), lambda i, j, k: (0, j))] * nr
        + [pl.BlockSpec((8, 128), lambda i, j, k: (0, 0))] * len(deps),
        out_specs=[mn_spec] * no + [pl.BlockSpec((8, tn), lambda i, j, k: (0, j))] * n_row_out,
        out_shape=[_sds((M, N), d) for d in out_dtypes] + [_sds((8, N), f32)] * n_row_out,
        scratch_shapes=[pltpu.VMEM((tm, tn), f32)] if nk > 1 else [],
        compiler_params=_cparams(("arbitrary", "arbitrary", "arbitrary")),
    )(a, b, *extras, *rows, *deps)


def _ep_add(acc, r):
    return (acc + r,)


def _ep_add_rms(acc, r, g):
    x = acc + r
    return x, _rms(x, g)


def _ep_rms_bwd(acc, x, dres, g):
    _, vjp = jax.vjp(_rms, x, g)
    dx, dg = vjp(acc)
    return dres + dx, dg


def _ep_add_rms_bwd(acc, more, x, dres, g):
    return _ep_rms_bwd(acc + more, x, dres, g)


def _ep_relu2(acc):
    r = jnp.maximum(acc, 0.0)
    return acc, r * r


def _ep_drelu2(acc, f):
    return (acc * 2.0 * jnp.maximum(f, 0.0),)


def _rms(x, g):
    return x * lax.rsqrt(jnp.mean(x * x, axis=-1, keepdims=True) + EPS) * g


TT = 512


def _rms_fwd(x, g):
    T, D = x.shape

    def body(x_ref, g_ref, h_ref, tok_ref):
        h_ref[...] = _rms(x_ref[...], g_ref[...]).astype(bf16)
        tok_ref[...] = jnp.zeros_like(tok_ref)

    return pl.pallas_call(
        body, name="rms_fwd", grid=(T // TT,),
        in_specs=[pl.BlockSpec((TT, D), lambda i: (i, 0)), pl.BlockSpec((1, D), lambda i: (0, 0))],
        out_specs=[pl.BlockSpec((TT, D), lambda i: (i, 0)), pl.BlockSpec((8, 128), lambda i: (0, 0))],
        out_shape=[_sds((T, D), bf16), _sds((8, 128), f32)], compiler_params=_cparams(("arbitrary",)),
    )(x, g.reshape(1, D))


def _loss_head(x, g, tgt):
    T, D = x.shape

    def f(xv, gv, tv):
        e = _rms(xv, gv) - tv
        return 0.5 * jnp.sum(jnp.sum(e * e, axis=-1, keepdims=True) * (1.0 / D), axis=0, keepdims=True)

    def body(x_ref, g_ref, t_ref, loss_ref, dx_ref, dg_ref):
        tv = t_ref[...]
        l, vjp = jax.vjp(lambda xv, gv: f(xv, gv, tv), x_ref[...], g_ref[...])
        dx, dg = vjp(jnp.ones((1, 1), f32))
        dx_ref[...] = dx

        @pl.when(pl.program_id(0) == 0)
        def _():
            dg_ref[...] = jnp.zeros_like(dg_ref)
            loss_ref[...] = jnp.zeros_like(loss_ref)

        dg_ref[0:1, :] += dg
        loss_ref[...] += jnp.broadcast_to(l, loss_ref.shape)

    tile = pl.BlockSpec((TT, D), lambda i: (i, 0))
    return pl.pallas_call(
        body, name="loss_head", grid=(T // TT,),
        in_specs=[tile, pl.BlockSpec((1, D), lambda i: (0, 0)), tile],
        out_specs=[pl.BlockSpec((8, 128), lambda i: (0, 0)), tile, pl.BlockSpec((8, D), lambda i: (0, 0))],
        out_shape=[_sds((8, 128), f32), _sds((T, D), f32), _sds((8, D), f32)],
        compiler_params=_cparams(("arbitrary",)),
    )(x, g.reshape(1, D), tgt)


TB = 256


def _glu(a_val, a_gate):
    return a_val * jax.nn.sigmoid(a_gate)


PAIR = 2 * HEAD


def _pair_mean(x, lo):
    s_lo = jnp.sum(jnp.where(lo, x, 0.0), axis=-1, keepdims=True)
    s_hi = jnp.sum(jnp.where(lo, 0.0, x), axis=-1, keepdims=True)
    return jnp.where(lo, s_lo, s_hi) * (1.0 / HEAD)


def _pair_ln(v, g, b):
    lo = lax.broadcasted_iota(jnp.int32, v.shape, 1) < HEAD
    vc = v - _pair_mean(v, lo)
    var = _pair_mean(vc * vc, lo)
    return vc * lax.rsqrt(var + EPS) * g + b


def _ln_silu(v, g, b):
    return jax.nn.silu(_pair_ln(v, g, b))


def _conv_geom(kw):
    halo = 32 if kw > 9 else 16
    return halo, halo - (kw - 1)


def _residues(shifts):
    return sorted({s % 8 for s in shifts} - {0})


def _shift_copies(src_ref, cp_ref, res, rows, ls):
    for j, r in enumerate(res):
        cp_ref[j, :, ls] = src_ref[pl.ds(r, rows), ls]


def _shifted(src_ref, cp_ref, res, shift, size, ls):
    r = shift % 8
    if r == 0:
        return src_ref[pl.ds(shift, size), ls]
    return cp_ref[res.index(r), pl.ds(shift - r, size), ls]


def _conv_taps(hp_ref, hs_ref, w_ref, b_ref, acc_ref, kw, off, halo, width):
    res = _residues(range(off, off + kw))
    for c in range(width // 128):
        ls = pl.ds(c * 128, 128)
        _shift_copies(hp_ref, hs_ref, res, halo + TB, ls)
        acc = jnp.broadcast_to(b_ref[:, ls], (TB, 128))
        for k in range(kw):
            acc = acc + w_ref[k:k + 1, ls] * _shifted(hp_ref, hs_ref, res, off + k, TB, ls)
        acc_ref[:, ls] = acc


def _conv_fwd(name, src, col_block, w, b, kw, conformer, n_seq, ln_g=None, ln_b=None, share=None):
    T = src.shape[0]
    cout = w.shape[1]
    cin = 2 * cout if conformer else cout
    halo, off = _conv_geom(kw)
    nblk = T // n_seq // TB
    hb = TB // halo
    out_shape, out_blk, shared_in, aliases = _shared(share, T, cout, bf16 if conformer else f32, 6 if conformer else 4)

    def body(cur_ref, halo_ref, w_ref, b_ref, *rest):
        if conformer:
            g_ref, lb_ref = rest[:2]
            rest = rest[2:]
        out_ref, hp_ref, acc_ref, hs_ref = rest[len(shared_in):]
        i = pl.program_id(1)
        first = (i == 0)

        @pl.when((pl.program_id(0) == 0) & first)
        def _():
            hp_ref[pl.ds(halo + TB, 8), :] = jnp.zeros((8, cout), f32)

        if conformer:
            hp_ref[pl.ds(halo, TB), :] = _glu(cur_ref[:, 0:cout].astype(f32), cur_ref[:, cout:cin].astype(f32))
            hh = _glu(halo_ref[:, 0:cout].astype(f32), halo_ref[:, cout:cin].astype(f32))
        else:
            hp_ref[pl.ds(halo, TB), :] = cur_ref[...].astype(f32)
            hh = halo_ref[...].astype(f32)
        hp_ref[pl.ds(0, halo), :] = jnp.where(first, 0.0, hh)
        _conv_taps(hp_ref, hs_ref, w_ref, b_ref, acc_ref, kw, off, halo, cout)
        if conformer:
            for q in range(cout // PAIR):
                ls = pl.ds(q * PAIR, PAIR)
                out_ref[:, ls] = _ln_silu(acc_ref[:, ls], g_ref[:, ls], lb_ref[:, ls]).astype(out_ref.dtype)
        else:
            out_ref[...] = jax.nn.silu(acc_ref[...]).astype(out_ref.dtype)

    nres = len(_residues(range(off, off + kw)))

    row = pl.BlockSpec((1, cout), lambda s, i: (0, 0))
    in_specs = [pl.BlockSpec((TB, cin), lambda s, i: (s * nblk + i, col_block)),
                pl.BlockSpec((halo, cin), lambda s, i: (jnp.maximum((s * nblk + i) * hb - 1, 0), col_block)),
                pl.BlockSpec((w.shape[0], cout), lambda s, i: (0, 0)), row]
    args = [src, src, w, b.reshape(1, cout)]
    if conformer:
        in_specs += [row, row]
        args += [ln_g.reshape(1, cout), ln_b.reshape(1, cout)]
    in_specs += [pl.BlockSpec(memory_space=pl.ANY)] * len(shared_in)
    args += shared_in
    return pl.pallas_call(
        body, name=name, grid=(n_seq, nblk), in_specs=in_specs,
        out_specs=pl.BlockSpec((TB, cout), lambda s, i: (s * nblk + i, out_blk)),
        out_shape=out_shape, input_output_aliases=aliases,
        scratch_shapes=[pltpu.VMEM((halo + TB + 8, cout), f32), pltpu.VMEM((TB, cout), f32),
                        pltpu.VMEM((nres, halo + TB, cout), f32)],
        compiler_params=_cparams(("arbitrary", "arbitrary")),
    )(*args)


def _shared(share, T, width, dtype, n_inputs, out_index=0):
    if share is None:
        return _sds((T, width), dtype), 0, [], {}
    total, blk, into = share
    if into is None:
        return _sds((T, total), dtype), blk, [], {}
    return _sds((T, total), dtype), blk, [into], {n_inputs: out_index}


def _conv_bwd(name, src, col_block, w, b, dy, dy_col_block, kw, conformer, n_seq, ln_g=None, ln_b=None, share=None):
    T = src.shape[0]
    cout = w.shape[1]
    wrows = w.shape[0]
    cin = 2 * cout if conformer else cout
    halo, off = _conv_geom(kw)
    nblk = T // n_seq // TB
    hb = TB // halo
    dsrc_shape, dsrc_blk, shared_in, aliases = _shared(share, T, cin, bf16, 7 if conformer else 5)

    def body(cur_ref, halo_ref, w_ref, b_ref, dy_ref, *rest):
        if conformer:
            g_ref, lb_ref = rest[:2]
            rest = rest[2:]
        rest = rest[len(shared_in):]
        if conformer:
            dsrc_ref, dw_ref, db_ref, dg_ref, dlb_ref, hp_ref, acc_ref, dz_ref, dhp_ref, carry_ref, hs_ref, dzs_ref = rest
        else:
            dsrc_ref, dw_ref, db_ref, hp_ref, acc_ref, dz_ref, dhp_ref, carry_ref, hs_ref, dzs_ref = rest
        s, ii = pl.program_id(0), pl.program_id(1)
        i = nblk - 1 - ii
        first = (i == 0)

        @pl.when((s == 0) & (ii == 0))
        def _():
            dw_ref[...] = jnp.zeros_like(dw_ref)
            db_ref[...] = jnp.zeros_like(db_ref)
            hp_ref[pl.ds(halo + TB, 8), :] = jnp.zeros((8, cout), f32)
            if conformer:
                dg_ref[...] = jnp.zeros_like(dg_ref)
                dlb_ref[...] = jnp.zeros_like(dlb_ref)

        @pl.when(ii == 0)
        def _():
            carry_ref[...] = jnp.zeros_like(carry_ref)
            dz_ref[pl.ds(0, halo), :] = jnp.zeros((halo, cout), f32)
            dz_ref[pl.ds(halo + TB, halo), :] = jnp.zeros((halo, cout), f32)

        if conformer:
            hp_ref[pl.ds(halo, TB), :] = _glu(cur_ref[:, 0:cout].astype(f32), cur_ref[:, cout:cin].astype(f32))
            hh = _glu(halo_ref[:, 0:cout].astype(f32), halo_ref[:, cout:cin].astype(f32))
        else:
            hp_ref[pl.ds(halo, TB), :] = cur_ref[...].astype(f32)
            hh = halo_ref[...].astype(f32)
        hp_ref[pl.ds(0, halo), :] = jnp.where(first, 0.0, hh)
        _conv_taps(hp_ref, hs_ref, w_ref, b_ref, acc_ref, kw, off, halo, cout)

        if conformer:
            for q in range(cout // PAIR):
                ls = pl.ds(q * PAIR, PAIR)
                _, vjp = jax.vjp(_ln_silu, acc_ref[:, ls], g_ref[:, ls], lb_ref[:, ls])
                da, dg, dlb = vjp(dy_ref[:, ls].astype(f32))
                dz_ref[pl.ds(halo, TB), ls] = da
                dg_ref[0:1, ls] += dg
                dlb_ref[0:1, ls] += dlb
        else:
            _, vjp = jax.vjp(jax.nn.silu, acc_ref[...])
            dz_ref[pl.ds(halo, TB), :] = vjp(dy_ref[...].astype(f32))[0]

        res_h = _residues(range(off, off + kw))
        res_z = _residues(range(kw))
        for c in range(cout // 128):
            ls = pl.ds(c * 128, 128)
            _shift_copies(dz_ref, dzs_ref, res_z, halo + TB + halo - 8, ls)
            dacc = dz_ref[pl.ds(halo, TB), ls]
            db_ref[0:1, ls] += jnp.sum(dacc, axis=0, keepdims=True)
            dhp = jnp.zeros((halo + TB, 128), f32)
            for k in range(kw):
                dw_ref[k:k + 1, ls] += jnp.sum(dacc * _shifted(hp_ref, hs_ref, res_h, off + k, TB, ls), axis=0, keepdims=True)
                dhp = dhp + w_ref[k:k + 1, ls] * _shifted(dz_ref, dzs_ref, res_z, kw - 1 - k, halo + TB, ls)
            dhp_ref[:, ls] = dhp
        dhp_ref[pl.ds(TB, halo), :] += carry_ref[...]
        carry_ref[...] = dhp_ref[pl.ds(0, halo), :]
        dcur = dhp_ref[pl.ds(halo, TB), :]
        if conformer:
            _, vjp = jax.vjp(_glu, cur_ref[:, 0:cout].astype(f32), cur_ref[:, cout:cin].astype(f32))
            dval, dgate = vjp(dcur)
            dsrc_ref[:, 0:cout] = dval.astype(dsrc_ref.dtype)
            dsrc_ref[:, cout:cin] = dgate.astype(dsrc_ref.dtype)
        else:
            dsrc_ref[...] = dcur.astype(dsrc_ref.dtype)

    def blk(s, ii):
        return s * nblk + (nblk - 1 - ii)

    row = pl.BlockSpec((1, cout), lambda s, ii: (0, 0))
    acc8 = pl.BlockSpec((8, cout), lambda s, ii: (0, 0))
    in_specs = [pl.BlockSpec((TB, cin), lambda s, ii: (blk(s, ii), col_block)),
                pl.BlockSpec((halo, cin), lambda s, ii: (jnp.maximum(blk(s, ii) * hb - 1, 0), col_block)),
                pl.BlockSpec((wrows, cout), lambda s, ii: (0, 0)), row,
                pl.BlockSpec((TB, cout), lambda s, ii: (blk(s, ii), dy_col_block))]
    args = [src, src, w, b.reshape(1, cout), dy]
    out_specs = [pl.BlockSpec((TB, cin), lambda s, ii: (blk(s, ii), dsrc_blk)),
                 pl.BlockSpec((wrows, cout), lambda s, ii: (0, 0)), acc8]
    out_shape = [dsrc_shape, _sds((wrows, cout), f32), _sds((8, cout), f32)]
    if conformer:
        in_specs += [row, row]
        args += [ln_g.reshape(1, cout), ln_b.reshape(1, cout)]
        out_specs += [acc8, acc8]
        out_shape += [_sds((8, cout), f32), _sds((8, cout), f32)]
    in_specs += [pl.BlockSpec(memory_space=pl.ANY)] * len(shared_in)
    args += shared_in
    return pl.pallas_call(
        body, name=name, grid=(n_seq, nblk), in_specs=in_specs, out_specs=out_specs, out_shape=out_shape,
        input_output_aliases=aliases,
        scratch_shapes=[pltpu.VMEM((halo + TB + 8, cout), f32), pltpu.VMEM((TB, cout), f32),
                        pltpu.VMEM((halo + TB + halo, cout), f32), pltpu.VMEM((halo + TB, cout), f32),
                        pltpu.VMEM((halo, cout), f32),
                        pltpu.VMEM((len(_residues(range(off, off + kw))), halo + TB, cout), f32),
                        pltpu.VMEM((len(_residues(range(kw))), halo + TB + halo - 8, cout), f32)],
        compiler_params=_cparams(("arbitrary", "arbitrary")),
    )(*args)


def _gelu(x):
    return 0.5 * x * (1.0 + lax.erf(x * (1.0 / math.sqrt(2.0))))


def _tril_mask(n):
    r = lax.broadcasted_iota(jnp.int32, (n, n), 0)
    c = lax.broadcasted_iota(jnp.int32, (n, n), 1)
    return r >= c


def _head_spread(nh):
    r = lax.broadcasted_iota(jnp.int32, (nh, nh * HEAD), 0)
    c = lax.broadcasted_iota(jnp.int32, (nh, nh * HEAD), 1)
    return (c // HEAD == r).astype(f32)


def _gmlp_bias(bs):
    return lax.dot_general(bs, _head_spread(bs.shape[0]), (((0,), (0,)), ((), ())), precision=HI, preferred_element_type=f32)


def _gmlp_pair(bu, bv, g, b, w_a, w_b, bias):
    lo = lax.broadcasted_iota(jnp.int32, bu.shape, 1) < HEAD
    tril = _tril_mask(CHUNK)
    u = _gelu(bu)
    vn = _pair_ln(_gelu(bv), g, b)
    mix = jnp.where(lo, _bdot(jnp.where(tril, w_a, 0.0), vn, "nn"), _bdot(jnp.where(tril, w_b, 0.0), vn, "nn"))
    return u * (mix + bias)


def _gmlp_fwd(proj, col_block, ln_g, ln_b, w_s, b_s, share=None):
    T = proj.shape[0]
    nh = w_s.shape[0]
    width = nh * HEAD
    out_shape, out_blk, shared_in, aliases = _shared(share, T, width, bf16, 5)

    def body(p_ref, g_ref, b_ref, w_ref, bs_ref, *rest):
        out_ref, bias_ref = rest[len(shared_in):]

        @pl.when(pl.program_id(0) == 0)
        def _():
            bias_ref[...] = _gmlp_bias(bs_ref[...])

        for q in range(nh // 2):
            ls = pl.ds(q * PAIR, PAIR)
            lv = pl.ds(width + q * PAIR, PAIR)
            out_ref[:, ls] = _gmlp_pair(p_ref[:, ls].astype(f32), p_ref[:, lv].astype(f32), g_ref[:, ls], b_ref[:, ls], w_ref[2 * q], w_ref[2 * q + 1],
                                        bias_ref[:, ls]).astype(out_ref.dtype)

    row = pl.BlockSpec((1, width), lambda i: (0, 0))
    return pl.pallas_call(
        body, name="gmlp_fwd", grid=(T // CHUNK,),
        in_specs=[pl.BlockSpec((CHUNK, 2 * width), lambda i: (i, col_block)), row, row,
                  pl.BlockSpec((nh, CHUNK, CHUNK), lambda i: (0, 0, 0)), pl.BlockSpec((nh, CHUNK), lambda i: (0, 0))]
        + [pl.BlockSpec(memory_space=pl.ANY)] * len(shared_in),
        out_specs=pl.BlockSpec((CHUNK, width), lambda i: (i, out_blk)),
        out_shape=out_shape, input_output_aliases=aliases, scratch_shapes=[pltpu.VMEM((CHUNK, width), f32)],
        compiler_params=_cparams(("arbitrary",)),
    )(proj, ln_g.reshape(1, width), ln_b.reshape(1, width), w_s, b_s, *shared_in)


def _gmlp_bwd(proj, col_block, ln_g, ln_b, w_s, b_s, dy, dy_col_block, share=None):
    T = proj.shape[0]
    nh = w_s.shape[0]
    width = nh * HEAD
    nstep = T // CHUNK
    dp_shape, dp_blk, shared_in, aliases = _shared(share, T, 2 * width, bf16, 6)

    def body(p_ref, g_ref, b_ref, w_ref, bs_ref, dy_ref, *rest):
        dp_ref, dg_ref, db_ref, dw_ref, dbst_ref, bias_ref, dbias_ref = rest[len(shared_in):]

        @pl.when(pl.program_id(0) == 0)
        def _():
            dg_ref[...] = jnp.zeros_like(dg_ref)
            db_ref[...] = jnp.zeros_like(db_ref)
            dw_ref[...] = jnp.zeros_like(dw_ref)
            dbias_ref[...] = jnp.zeros_like(dbias_ref)
            bias_ref[...] = _gmlp_bias(bs_ref[...])

        for q in range(nh // 2):
            ls = pl.ds(q * PAIR, PAIR)
            lv = pl.ds(width + q * PAIR, PAIR)
            _, vjp = jax.vjp(_gmlp_pair, p_ref[:, ls].astype(f32), p_ref[:, lv].astype(f32), g_ref[:, ls], b_ref[:, ls], w_ref[2 * q], w_ref[2 * q + 1],
                             bias_ref[:, ls])
            dbu, dbv, dg, db, dw_a, dw_b, dbias = vjp(dy_ref[:, ls].astype(f32))
            dp_ref[:, ls] = dbu.astype(dp_ref.dtype)
            dp_ref[:, lv] = dbv.astype(dp_ref.dtype)
            dg_ref[0:1, ls] += dg
            db_ref[0:1, ls] += db
            dw_ref[2 * q] += dw_a
            dw_ref[2 * q + 1] += dw_b
            dbias_ref[:, ls] += dbias

        @pl.when(pl.program_id(0) == nstep - 1)
        def _():
            dbst_ref[...] = lax.dot_general(dbias_ref[...], _head_spread(nh), (((1,), (1,)), ((), ())),
                                            precision=HI, preferred_element_type=f32)

    row = pl.BlockSpec((1, width), lambda i: (0, 0))
    acc8 = pl.BlockSpec((8, width), lambda i: (0, 0))
    wspec = pl.BlockSpec((nh, CHUNK, CHUNK), lambda i: (0, 0, 0))
    res = pl.pallas_call(
        body, name="gmlp_bwd", grid=(nstep,),
        in_specs=[pl.BlockSpec((CHUNK, 2 * width), lambda i: (i, col_block)), row, row, wspec,
                  pl.BlockSpec((nh, CHUNK), lambda i: (0, 0)), pl.BlockSpec((CHUNK, width), lambda i: (i, dy_col_block))]
        + [pl.BlockSpec(memory_space=pl.ANY)] * len(shared_in),
        out_specs=[pl.BlockSpec((CHUNK, 2 * width), lambda i: (i, dp_blk)), acc8, acc8, wspec,
                   pl.BlockSpec((CHUNK, nh), lambda i: (0, 0))],
        out_shape=[dp_shape, _sds((8, width), f32), _sds((8, width), f32),
                   _sds((nh, CHUNK, CHUNK), f32), _sds((CHUNK, nh), f32)],
        input_output_aliases=aliases,
        scratch_shapes=[pltpu.VMEM((CHUNK, width), f32), pltpu.VMEM((CHUNK, width), f32)],
        compiler_params=_cparams(("arbitrary",)),
    )(proj, ln_g.reshape(1, width), ln_b.reshape(1, width), w_s, b_s, dy, *shared_in)
    return res[0], res[1], res[2], res[3], res[4].T


def _sel_col(x, h):
    lane = lax.broadcasted_iota(jnp.int32, x.shape, 1)
    return jnp.sum(jnp.where(lane == h, x, 0.0), axis=1, keepdims=True)


def _sel_row(x, h):
    sub = lax.broadcasted_iota(jnp.int32, x.shape, 0)
    return jnp.sum(jnp.where(sub == h, x, 0.0), axis=0, keepdims=True)


def _ssd_chunk(nh, ngrp, xs_l, z_l, b_l, c_l, dtraw, dtb, alog, dskip, ng_l, prev_l):
    hg = nh // ngrp
    tril = _tril_mask(CHUNK)
    tl = tril.astype(f32)
    lo = lax.broadcasted_iota(jnp.int32, (CHUNK, PAIR), 1) < HEAD
    lo_row = lo[0:1, :]
    dt = jax.nn.softplus(dtraw + dtb)
    a = dt * (-jnp.exp(alog))
    cs = jnp.dot(tl, a, precision=HI, preferred_element_type=f32)
    cst = lax.dot_general(a, tl, (((0,), (1,)), ((), ())), precision=HI, preferred_element_type=f32)
    cb_l = [_bdot(c_l[g], b_l[g], "nt") for g in range(ngrp)]
    yz_l, new_prev = [], []
    for q in range(nh // 2):
        g = (2 * q) // hg
        cols = []
        for h in (2 * q, 2 * q + 1):
            cs_h = _sel_col(cs, h)
            tot = _sel_row(cs_h, CHUNK - 1)
            seg = jnp.where(tril, cs_h - _sel_row(cst, h), 0.0)
            lmat = jnp.where(tril, jnp.exp(seg), 0.0)
            cols.append((_sel_col(dt, h), cs_h, tot, lmat, _sel_col(dskip, h)))
        (dt_a, cs_a, tot_a, l_a, dsk_a), (dt_b, cs_b, tot_b, l_b, dsk_b) = cols
        xs = xs_l[q]
        x = xs * jnp.where(lo, dt_a, dt_b)
        ydiag = jnp.where(lo, _bdot(cb_l[g] * l_a, x, "nn"), _bdot(cb_l[g] * l_b, x, "nn"))
        yoff = _bdot(c_l[g], prev_l[q], "nn") * jnp.where(lo, jnp.exp(cs_a), jnp.exp(cs_b))
        xdec = x * jnp.where(lo, jnp.exp(tot_a - cs_a), jnp.exp(tot_b - cs_b))
        st = _bdot(b_l[g], xdec, "tn")
        new_prev.append(prev_l[q] * jnp.where(lo_row, jnp.exp(tot_a), jnp.exp(tot_b)) + st)
        y = ydiag + yoff + jnp.where(lo_row, dsk_a, dsk_b) * xs
        yz_l.append(y * jax.nn.silu(z_l[q]))
    out = [None] * (nh // 2)
    qg = hg // 2
    for g in range(ngrp):
        ssq = sum(jnp.sum(yz_l[q] * yz_l[q], axis=-1, keepdims=True) for q in range(g * qg, (g + 1) * qg))
        r = lax.rsqrt(ssq * (1.0 / (hg * HEAD)) + EPS)
        for q in range(g * qg, (g + 1) * qg):
            out[q] = yz_l[q] * r * ng_l[q]
    return out, new_prev


def _ssd_read(nh, ngrp, nst, xbc_ref, z_ref, ng_ref, st_ref):
    cw = nh * HEAD
    xs_l = [xbc_ref[:, pl.ds(q * PAIR, PAIR)] for q in range(nh // 2)]
    b_l = [xbc_ref[:, pl.ds(cw + g * nst, nst)] for g in range(ngrp)]
    c_l = [xbc_ref[:, pl.ds(cw + ngrp * nst + g * nst, nst)] for g in range(ngrp)]
    z_l = [z_ref[:, pl.ds(q * PAIR, PAIR)].astype(f32) for q in range(nh // 2)]
    ng_l = [ng_ref[:, pl.ds(q * PAIR, PAIR)] for q in range(nh // 2)]
    prev_l = [st_ref[:, pl.ds(q * PAIR, PAIR)] for q in range(nh // 2)]
    return xs_l, z_l, b_l, c_l, ng_l, prev_l


def _ssd_fwd(xbc, proj, z_col_block, pdt, dtb, alog, dskip, ng, nh, ngrp, nst, n_seq, share=None):
    T = xbc.shape[0]
    cw = nh * HEAD
    nchunk = T // n_seq // CHUNK
    assert nst == CHUNK
    y_shape, y_blk, shared_in, aliases = _shared(share, T, cw, bf16, 7)

    def body(xbc_ref, z_ref, dt_ref, dtb_ref, alog_ref, dskip_ref, ng_ref, *rest):
        y_ref, sin_ref, st_ref = rest[len(shared_in):]

        @pl.when(pl.program_id(1) == 0)
        def _():
            st_ref[...] = jnp.zeros_like(st_ref)

        sin_ref[...] = st_ref[...]
        xs_l, z_l, b_l, c_l, ng_l, prev_l = _ssd_read(nh, ngrp, nst, xbc_ref, z_ref, ng_ref, st_ref)
        y_l, new_prev = _ssd_chunk(nh, ngrp, xs_l, z_l, b_l, c_l, dt_ref[...], dtb_ref[...], alog_ref[...],
                                   dskip_ref[...], ng_l, prev_l)
        for q in range(nh // 2):
            ls = pl.ds(q * PAIR, PAIR)
            y_ref[:, ls] = y_l[q].astype(y_ref.dtype)
            st_ref[:, ls] = new_prev[q]

    def blk(s, c):
        return s * nchunk + c

    prow = pl.BlockSpec((1, 128), lambda s, c: (0, 0))
    return pl.pallas_call(
        body, name="ssd_fwd", grid=(n_seq, nchunk),
        in_specs=[pl.BlockSpec((CHUNK, xbc.shape[1]), lambda s, c: (blk(s, c), 0)),
                  pl.BlockSpec((CHUNK, cw), lambda s, c: (blk(s, c), z_col_block)),
                  pl.BlockSpec((CHUNK, 128), lambda s, c: (blk(s, c), 0)),
                  prow, prow, prow, pl.BlockSpec((1, cw), lambda s, c: (0, 0))]
        + [pl.BlockSpec(memory_space=pl.ANY)] * len(shared_in),
        out_specs=[pl.BlockSpec((CHUNK, cw), lambda s, c: (blk(s, c), y_blk)),
                   pl.BlockSpec((nst, cw), lambda s, c: (blk(s, c), 0))],
        out_shape=[y_shape, _sds((T, cw), f32)], input_output_aliases=aliases,
        scratch_shapes=[pltpu.VMEM((nst, cw), f32)],
        compiler_params=_cparams(("arbitrary", "arbitrary")),
    )(xbc, proj, pdt, dtb, alog, dskip, ng.reshape(1, cw), *shared_in)


def _ssd_bwd(xbc, proj, z_col_block, pdt, dtb, alog, dskip, ng, sin, dy, dy_col_block, nh, ngrp, nst, n_seq, share=None):
    T, xw = xbc.shape
    cw = nh * HEAD
    nchunk = T // n_seq // CHUNK
    dz_shape, dz_blk, shared_in, aliases = _shared(share, T, cw, bf16, 9, out_index=1)

    def body(xbc_ref, z_ref, dt_ref, dtb_ref, alog_ref, dskip_ref, ng_ref, sin_ref, dy_ref, *rest):
        dxbc_ref, dz_ref, ddt_ref, ddtb_ref, dalog_ref, ddskip_ref, dng_ref, dst_ref = rest[len(shared_in):]
        s, cc = pl.program_id(0), pl.program_id(1)

        @pl.when((s == 0) & (cc == 0))
        def _():
            ddtb_ref[...] = jnp.zeros_like(ddtb_ref)
            dalog_ref[...] = jnp.zeros_like(dalog_ref)
            ddskip_ref[...] = jnp.zeros_like(ddskip_ref)
            dng_ref[...] = jnp.zeros_like(dng_ref)

        @pl.when(cc == 0)
        def _():
            dst_ref[...] = jnp.zeros_like(dst_ref)

        xs_l, z_l, b_l, c_l, ng_l, prev_l = _ssd_read(nh, ngrp, nst, xbc_ref, z_ref, ng_ref, sin_ref)
        _, vjp = jax.vjp(functools.partial(_ssd_chunk, nh, ngrp), xs_l, z_l, b_l, c_l, dt_ref[...], dtb_ref[...],
                         alog_ref[...], dskip_ref[...], ng_l, prev_l)
        dy_l = [dy_ref[:, pl.ds(q * PAIR, PAIR)].astype(f32) for q in range(nh // 2)]
        dst_l = [dst_ref[:, pl.ds(q * PAIR, PAIR)] for q in range(nh // 2)]
        dxs_l, dz_l, db_l, dc_l, ddt, ddtb, dalog, ddskip, dng_l, dprev_l = vjp((dy_l, dst_l))
        for q in range(nh // 2):
            ls = pl.ds(q * PAIR, PAIR)
            dxbc_ref[:, ls] = dxs_l[q]
            dz_ref[:, ls] = dz_l[q].astype(dz_ref.dtype)
            dng_ref[0:1, ls] += dng_l[q]
            dst_ref[:, ls] = dprev_l[q]
        for g in range(ngrp):
            dxbc_ref[:, pl.ds(cw + g * nst, nst)] = db_l[g]
            dxbc_ref[:, pl.ds(cw + ngrp * nst + g * nst, nst)] = dc_l[g]
        ddt_ref[...] = ddt.astype(ddt_ref.dtype)
        ddtb_ref[0:1, :] += ddtb
        dalog_ref[0:1, :] += dalog
        ddskip_ref[0:1, :] += ddskip

    def blk(s, cc):
        return s * nchunk + (nchunk - 1 - cc)

    prow = pl.BlockSpec((1, 128), lambda s, c: (0, 0))
    pacc = pl.BlockSpec((8, 128), lambda s, c: (0, 0))
    return pl.pallas_call(
        body, name="ssd_bwd", grid=(n_seq, nchunk),
        in_specs=[pl.BlockSpec((CHUNK, xw), lambda s, c: (blk(s, c), 0)),
                  pl.BlockSpec((CHUNK, cw), lambda s, c: (blk(s, c), z_col_block)),
                  pl.BlockSpec((CHUNK, 128), lambda s, c: (blk(s, c), 0)),
                  prow, prow, prow, pl.BlockSpec((1, cw), lambda s, c: (0, 0)),
                  pl.BlockSpec((nst, cw), lambda s, c: (blk(s, c), 0)),
                  pl.BlockSpec((CHUNK, cw), lambda s, c: (blk(s, c), dy_col_block))]
        + [pl.BlockSpec(memory_space=pl.ANY)] * len(shared_in),
        out_specs=[pl.BlockSpec((CHUNK, xw), lambda s, c: (blk(s, c), 0)),
                   pl.BlockSpec((CHUNK, cw), lambda s, c: (blk(s, c), dz_blk)),
                   pl.BlockSpec((CHUNK, 128), lambda s, c: (blk(s, c), 0)),
                   pacc, pacc, pacc, pl.BlockSpec((8, cw), lambda s, c: (0, 0))],
        out_shape=[_sds((T, xw), f32), dz_shape, _sds((T, 128), bf16),
                   _sds((8, 128), f32), _sds((8, 128), f32), _sds((8, 128), f32), _sds((8, cw), f32)],
        input_output_aliases=aliases,
        scratch_shapes=[pltpu.VMEM((nst, cw), f32)],
        compiler_params=_cparams(("arbitrary", "arbitrary")),
    )(xbc, proj, pdt, dtb, alog, dskip, ng.reshape(1, cw), sin, dy, *shared_in)


_HBM = pl.BlockSpec(memory_space=pltpu.HBM)
_SEM = pl.BlockSpec(memory_space=pltpu.SEMAPHORE)
_EFFECT = pltpu.SideEffectType.DATAFLOW_SIDE_EFFECTING


def _split_copies(n, scatter, src_refs, land_refs, send_sems, recv_sems):
    npeer = N_DEV - 1
    x, y, c = lax.axis_index("x"), lax.axis_index("y"), lax.axis_index("c")
    me = 4 * x + 2 * y + c
    copies = []
    for i in range(n):
        for k in range(1, N_DEV):
            px = 1 - x if k & 4 else x
            py = 1 - y if k & 2 else y
            pc = 1 - c if k & 1 else c
            src = src_refs[i].at[4 * px + 2 * py + pc] if scatter else src_refs[i]
            copies.append(pltpu.make_async_remote_copy(
                src_ref=src, dst_ref=land_refs[i].at[me],
                send_sem=send_sems.at[i * npeer + k - 1], recv_sem=recv_sems.at[i * npeer + k - 1],
                device_id=(px, py, pc), device_id_type=pl.DeviceIdType.MESH))
    return copies


def _exchange_start(name, arrs, scatter):
    n = len(arrs)
    nsem = n * (N_DEV - 1)
    me = 4 * lax.axis_index("x") + 2 * lax.axis_index("y") + lax.axis_index("c")
    lands = []
    for a in arrs:
        own = lax.dynamic_index_in_dim(a, me, 0, keepdims=True) if scatter else a[None]
        full = lax.empty(a.shape if scatter else (N_DEV,) + a.shape, a.dtype)
        lands.append(lax.dynamic_update_slice(full, own, (me,) + (0,) * (full.ndim - 1)))

    def body(*refs):
        src_refs, land_refs = refs[:n], refs[n:2 * n]
        send_sems, recv_sems = refs[2 * n], refs[2 * n + 1]
        token = refs[-1]
        for cp in _split_copies(n, scatter, src_refs, land_refs, send_sems, recv_sems):
            cp.start()
        token[...] = jnp.zeros_like(token)

    res = pl.pallas_call(
        body, name=name,
        out_shape=(pltpu.SemaphoreType.DMA((nsem,)), pltpu.SemaphoreType.DMA((nsem,)),
                   *[pltpu.HBM(a.shape, a.dtype) for a in arrs], *[pltpu.HBM(l.shape, l.dtype) for l in lands],
                   _sds((8, 128), f32)),
        in_specs=[_HBM] * (2 * n),
        out_specs=(_SEM, _SEM, *[_HBM] * (2 * n), pl.BlockSpec(memory_space=pltpu.VMEM)),
        input_output_aliases={j: 2 + j for j in range(2 * n)},
        compiler_params=pltpu.CompilerParams(has_side_effects=_EFFECT),
    )(*[pltpu.with_memory_space_constraint(a, pltpu.HBM) for a in arrs],
      *[pltpu.with_memory_space_constraint(l, pltpu.HBM) for l in lands])
    return (n, scatter, res[0], res[1], res[2:2 + n], res[2 + n:2 + 2 * n]), res[-1]


def _exchange_wait(name, handle, after):
    n, scatter, send_sems, recv_sems, srcs, lands = handle
    after = list(after) if isinstance(after, (list, tuple)) else [after]

    def body(*refs):
        src_refs, land_refs = refs[:n], refs[n:2 * n]
        for cp in _split_copies(n, scatter, src_refs, land_refs, refs[2 * n], refs[2 * n + 1]):
            cp.wait_send()
            cp.wait_recv()

    res = pl.pallas_call(
        body, name=name,
        out_shape=[pltpu.HBM(a.shape, a.dtype) for a in (*srcs, *lands)],
        in_specs=[_HBM] * (2 * n) + [_SEM, _SEM] + [pl.BlockSpec(memory_space=pl.ANY)] * len(after),
        out_specs=[_HBM] * (2 * n),
        input_output_aliases={j: j for j in range(2 * n)},
        compiler_params=pltpu.CompilerParams(has_side_effects=_EFFECT),
    )(*srcs, *lands, send_sems, recv_sems, *after)
    return res[n:]


def _adam_tiles(R, C):
    if R % 256 == 0:
        return (256, C), (R // 256, 1)
    assert C % 128 == 0
    return (R, 128), (1, C // 128)


def _adam(name, parts, w, m, v, layer=None, depth=None, into=None, stacked_in=False):
    P, R, C = parts.shape
    (tr, tc), (gr, gc) = _adam_tiles(R, C)
    c1 = 1.0 / (1.0 - ADAM_B1 ** ADAM_STEP)
    c2 = 1.0 / (1.0 - ADAM_B2 ** ADAM_STEP)
    into = [] if into is None else list(into)

    def body(p_ref, w_ref, m_ref, v_ref, *rest):
        g_ref, d_ref, nm_ref, nv_ref = rest[len(into):]
        g = p_ref[0].astype(f32)
        for s in range(1, P):
            g = g + p_ref[s].astype(f32)
        nm = ADAM_B1 * m_ref[...] + (1.0 - ADAM_B1) * g
        nv = ADAM_B2 * v_ref[...] + (1.0 - ADAM_B2) * (g * g)
        g_ref[...] = g
        nm_ref[...] = nm
        nv_ref[...] = nv
        d_ref[...] = -ADAM_LR * ((nm * c1) / (jnp.sqrt(nv * c2) + ADAM_EPS) + ADAM_WD * w_ref[...])

    tile = pl.BlockSpec((tr, tc), lambda i, j: (i, j))
    layer_tile = pl.BlockSpec((None, tr, tc), lambda i, j: (layer, i, j))
    out_tile, out_sds = (tile, _sds((R, C), f32)) if layer is None else (layer_tile, _sds((depth, R, C), f32))
    return pl.pallas_call(
        body, name=name, grid=(gr, gc),
        in_specs=[pl.BlockSpec((P, tr, tc), lambda i, j: (0, i, j))] + [layer_tile if stacked_in else tile] * 3
        + [pl.BlockSpec(memory_space=pl.ANY)] * len(into),
        out_specs=[out_tile] * 4, out_shape=[out_sds] * 4,
        input_output_aliases={4 + k: k for k in range(len(into))},
        compiler_params=_cparams(("arbitrary", "arbitrary")),
    )(parts, w, m, v, *into)


def _sum_parts(name, parts):
    P, R, C = parts.shape
    tr = 256 if R % 256 == 0 else R

    def body(p_ref, o_ref):
        g = p_ref[0]
        for s in range(1, P):
            g = g + p_ref[s]
        o_ref[...] = g

    return pl.pallas_call(
        body, name=name, grid=(R // tr,),
        in_specs=[pl.BlockSpec((P, tr, C), lambda i: (0, i, 0))], out_specs=pl.BlockSpec((tr, C), lambda i: (i, 0)),
        out_shape=_sds((R, C), f32), compiler_params=_cparams(("arbitrary",)),
    )(parts)


def _pad_to(a, n, axis):
    if a.shape[axis] == n:
        return a
    cfg = [(0, 0)] * a.ndim
    cfg[axis] = (0, n - a.shape[axis])
    return jnp.pad(a, cfg)


def _pack(arrs):
    flat = [_pad_to(a.reshape(-1), -(-a.size // 128) * 128, 0) for a in arrs]
    rows = jnp.concatenate(flat).reshape(-1, 128)
    return _pad_to(rows, -(-rows.shape[0] // 256) * 256, 0)


def _unpack(slab, shapes):
    flat = slab.reshape(-1)
    out, o = [], 0
    for s in shapes:
        n = math.prod(s)
        out.append(flat[o:o + n].reshape(s))
        o += -(-n // 128) * 128
    return out


_NAMES = ['norm1_g', 'w_in', 'conv_a_w', 'conv_a_b', 'ln_a_g', 'ln_a_b', 'ln_b_g', 'ln_b_b', 'w_spatial', 'b_spatial',
          'conv_c_w', 'conv_c_b', 'dt_bias', 'a_log', 'd_skip', 'norm_c_g', 'w_out', 'norm2_g', 'w_ff1', 'w_ff2', 'final_g']
_REPL = ['norm1_g', 'conv_a_b', 'ln_a_g', 'ln_a_b', 'ln_b_g', 'ln_b_b', 'w_spatial', 'b_spatial', 'conv_c_b',
         'dt_bias', 'a_log', 'd_skip', 'norm_c_g', 'norm2_g']
_CONVW = ['conv_a_w', 'conv_c_w']
_BIG = ['w_in', 'w_out', 'w_ff1', 'w_ff2']
_BIG_T = {'w_in': True, 'w_out': False, 'w_ff1': True, 'w_ff2': False}


def _row128(v):
    return _pad_to(v.reshape(1, -1), 128, 1)


def _step(p, m, v, x, loss_target):
    nb, S, D = x.shape
    T = nb * S
    depth = p['norm1_g'].shape[0]
    a_w = p['conv_a_b'].shape[1]
    b_w = p['ln_b_g'].shape[1]
    nh = p['dt_bias'].shape[1]
    c_w = p['norm_c_g'].shape[1]
    xw = p['conv_c_b'].shape[1]
    ngrp = 2
    nst = (xw - c_w) // (2 * ngrp)
    d_in = p['w_in'].shape[2] * N_DEV
    main = d_in - nh
    assert main == 2 * a_w + 2 * b_w + c_w + xw and 2 * a_w == 2 * b_w == c_w and xw % c_w == c_w // 2
    me = 4 * lax.axis_index("x") + 2 * lax.axis_index("y") + lax.axis_index("c")

    x2 = x.reshape(T, D)
    tgt = loss_target.reshape(T, D)

    def shards(i, z=None):
        z = 0.0 if z is None else z
        return [(p['w_in'][i].T + z).astype(bf16), (p['w_out'][i] + z).astype(bf16), (p['w_ff1'][i].T + z).astype(bf16),
                (p['w_ff2'][i] + z).astype(bf16), p['conv_a_w'][i], p['conv_c_w'][i]]

    def gathered_in(wt, ca, cc):
        wt = wt.reshape(d_in, D)
        ca = jnp.transpose(ca, (1, 0, 2)).reshape(KA, a_w)
        cc = jnp.transpose(cc, (1, 0, 2)).reshape(KC, xw)
        return dict(wt=wt, wt_dt=_pad_to(wt[main:], 128, 0), ca=_pad_to(ca, 32, 0), cc=_pad_to(cc, 8, 0))

    def start_layer(i, after=None):
        sh = shards(i, None if after is None else after[0, 0])
        ha, t = _exchange_start("gather_w%da_start" % i, [sh[0], sh[4], sh[5]], False)
        hb, t = _exchange_start("gather_w%db_start" % i, [shards(i, t[0, 0])[1]], False)
        hc, t = _exchange_start("gather_w%dc_start" % i, [shards(i, t[0, 0])[2]], False)
        hd, t = _exchange_start("gather_w%dd_start" % i, [shards(i, t[0, 0])[3]], False)
        return dict(a=ha, b=hb, c=hc, d=hd), t

    W, saved = [], []
    xc = x2
    H, tok = start_layer(0)
    for i in range(depth):
        w = gathered_in(*_exchange_wait("gather_w%da_wait" % i, H['a'], [xc, tok]))
        W.append(w)
        Hi = H
        h1, rtok = _rms_fwd(xc, p['norm1_g'][i])
        if i + 1 < depth:
            H, tok = start_layer(i + 1, rtok + tok)
        else:
            tok = None
        (proj,) = _mm("mm_proj", h1, w['wt'], "nt", [bf16], dep=tok, n=main)
        (pdt,) = _mm("mm_pdt", h1, w['wt_dt'], "nt", [f32])
        mixw = a_w + b_w + c_w
        ycat = _conv_fwd("confa_fwd", proj, 0, w['ca'], p['conv_a_b'][i], KA, True, nb, p['ln_a_g'][i], p['ln_a_b'][i],
                         share=(mixw, 0, None))
        ycat = _gmlp_fwd(proj, 1, p['ln_b_g'][i], p['ln_b_b'][i], p['w_spatial'][i], p['b_spatial'][i], share=(mixw, 1, ycat))
        xbc = _conv_fwd("convc_fwd", proj, 2, w['cc'], p['conv_c_b'][i], KC, False, nb)
        dtb, alog, dsk = _row128(p['dt_bias'][i]), _row128(p['a_log'][i]), _row128(p['d_skip'][i])
        ycat, sin = _ssd_fwd(xbc, proj, 2, pdt, dtb, alog, dsk, p['norm_c_g'][i], nh, ngrp, nst, nb, share=(mixw, 1, ycat))
        w['wout'] = _exchange_wait("gather_w%db_wait" % i, Hi['b'], ycat)[0].reshape(-1, D)
        xm, h2 = _mm("mm_out", ycat, w['wout'], "nn", [f32, bf16], _ep_add_rms, (xc,), rows=(p['norm2_g'][i].reshape(1, D),))
        w['w1t'] = _exchange_wait("gather_w%dc_wait" % i, Hi['c'], h2)[0].reshape(-1, D)
        f, a = _mm("mm_ff1", h2, w['w1t'], "nt", [bf16, bf16], _ep_relu2)
        w['w2'] = _exchange_wait("gather_w%dd_wait" % i, Hi['d'], a)[0].reshape(-1, D)
        (xo,) = _mm("mm_ff2", a, w['w2'], "nn", [f32], _ep_add, (xm,))
        saved.append(dict(x_in=xc, h1=h1, proj=proj, pdt=pdt, xbc=xbc, sin=sin, ycat=ycat, xm=xm, h2=h2, f=f, a=a,
                          dtb=dtb, alog=alog, dsk=dsk))
        xc = xo

    lp, dx, dfinal = _loss_head(xc, p['final_g'], tgt)
    loss = lax.psum(lp[0, 0], ("x", "y", "c"))

    out = {}
    kinds = ("grad", "delta", "new_m", "new_v")
    names1 = _REPL + _CONVW

    started, small = [], [None] * depth

    def send(n, i, g):
        handle, token = _exchange_start("scatter_%s_%d_start" % (n, i), [g.reshape(N_DEV, -1, D)], True)
        started.append((n, i, handle))
        return token

    tok = None
    for i in reversed(range(depth)):
        w, sv = W[i], saved[i]
        (df,) = _mm("mm_df", dx, w['w2'], "nt", [bf16], _ep_drelu2, (sv['f'],), dep=tok)
        (gw2,) = _mm("mm_gw2", sv['a'], dx, "tn", [bf16])
        tok = send('w_ff2', i, gw2)
        dxm, dg2 = _mm("mm_dh2", df, w['w1t'], "nn", [f32], _ep_rms_bwd, (sv['xm'], dx), dep=tok,
                       rows=(p['norm2_g'][i].reshape(1, D),), n_row_out=1)
        (gw1t,) = _mm("mm_gw1", df, sv['h2'], "tn", [bf16])
        tok = send('w_ff1', i, gw1t)
        (dycat,) = _mm("mm_dycat", dxm, w['wout'], "nt", [bf16], dep=tok)
        (gwout,) = _mm("mm_gwout", sv['ycat'], dxm, "tn", [bf16])
        tok = send('w_out', i, gwout)
        dproj, dwa, dba, dlag, dlab = _conv_bwd("confa_bwd", sv['proj'], 0, w['ca'], p['conv_a_b'][i] + tok[0, 0], dycat, 0, KA,
                                                True, nb, p['ln_a_g'][i], p['ln_a_b'][i], share=(main, 0, None))
        dproj, dlbg, dlbb, dws, dbs = _gmlp_bwd(sv['proj'], 1, p['ln_b_g'][i], p['ln_b_b'][i], p['w_spatial'][i],
                                                p['b_spatial'][i], dycat, 1, share=(main, 1, dproj))
        dxbc, dproj, ddt, ddtb, dalog, ddsk, dng = _ssd_bwd(sv['xbc'], sv['proj'], 2, sv['pdt'], sv['dtb'], sv['alog'],
                                                            sv['dsk'], p['norm_c_g'][i], sv['sin'], dycat, 1, nh, ngrp, nst, nb,
                                                            share=(main, 2, dproj))
        dproj, dwc, dbc = _conv_bwd("convc_bwd", sv['proj'], 2, w['cc'], p['conv_c_b'][i], dxbc, 0, KC, False, nb,
                                    share=(main, 2, dproj))
        (dh_main,) = _mm("mm_dh1", dproj, w['wt'], "nn", [f32])
        (gwt_main,) = _mm("mm_gwin", dproj, sv['h1'], "tn", [bf16])
        (gwt_dt,) = _mm("mm_gwdt", ddt, sv['h1'], "tn", [bf16])
        tok = send('w_in', i, jnp.concatenate([gwt_main, gwt_dt[:nh]], axis=0))
        dx, dg1 = _mm("mm_dh1dt", ddt, w['wt_dt'], "nn", [f32], _ep_add_rms_bwd, (dh_main, sv['x_in'], dxm), dep=tok,
                      rows=(p['norm1_g'][i].reshape(1, D),), n_row_out=1)

        gi = dict(norm1_g=dg1[0], norm2_g=dg2[0], conv_a_w=dwa[:KA], conv_a_b=dba[0], ln_a_g=dlag[0], ln_a_b=dlab[0],
                  ln_b_g=dlbg[0], ln_b_b=dlbb[0], w_spatial=dws, b_spatial=dbs, conv_c_w=dwc[:KC], conv_c_b=dbc[0],
                  dt_bias=ddtb[0, :nh], a_log=dalog[0, :nh], d_skip=ddsk[0, :nh], norm_c_g=dng[0])
        parts_i = [gi[n] for n in names1] + ([dfinal[0]] if i == depth - 1 else [])
        handle, tok = _exchange_start("gather_g%d_start" % i, [_pack(parts_i)], False)
        small[i] = ([a.shape for a in parts_i], handle)
    grad_x = dx.reshape(nb, S, D)

    dep = [dx, tok]
    for n, i, handle in started:
        (parts,) = _exchange_wait("scatter_%s_%d_wait" % (n, i), handle, dep)
        prev = [out[(kind, n)] for kind in kinds] if (kinds[0], n) in out else None
        if _BIG_T[n]:
            res = _adam("adam_%s_%d" % (n, i), parts, p[n][i].T, m[n][i].T, v[n][i].T, layer=i, depth=depth, into=prev)
        else:
            res = _adam("adam_%s_%d" % (n, i), parts, p[n], m[n], v[n], layer=i, depth=depth, into=prev, stacked_in=True)
        for kind, r in zip(kinds, res):
            out[(kind, n)] = r
        dep = res[3]

    gsum = [None] * depth
    for i in reversed(range(depth)):
        (parts,) = _exchange_wait("gather_g%d_wait" % i, small[i][1], dep)
        gsum[i] = _sum_parts("sum_small", parts)
        dep = gsum[i]
    widths = [-(-math.prod(p[n].shape[1:]) // 128) * 128 for n in _REPL]
    rep_rows = sum(widths) // 128
    tot_rows = -(-depth * rep_rows // 256) * 256

    def rep_slab(q):
        cols = [_pad_to(q[n].reshape(depth, -1), wd, 1) for n, wd in zip(_REPL, widths)]
        return _pad_to(jnp.concatenate(cols, axis=1).reshape(-1, 128), tot_rows, 0)

    g_rep = _pad_to(jnp.concatenate([g[:rep_rows] for g in gsum], axis=0), tot_rows, 0)
    res = _adam("adam_small", g_rep[None], rep_slab(p), rep_slab(m), rep_slab(v))
    for kind, r in zip(kinds, res):
        view = r[:depth * rep_rows].reshape(depth, -1)
        o = 0
        for n, wd in zip(_REPL, widths):
            out[(kind, n)] = view[:, o:o + math.prod(p[n].shape[1:])].reshape(p[n].shape)
            o += wd

    extra = []
    for i in range(depth):
        tail = _unpack(gsum[i][rep_rows:], small[i][0][len(_REPL):])
        extra.append(tail)
    gconv = []
    for j, n in enumerate(_CONVW):
        cw_shard = p[n].shape[2]
        full = jnp.stack([extra[i][j] for i in range(depth)])
        gconv.append(lax.dynamic_slice_in_dim(full, me * cw_shard, cw_shard, axis=2))
    tail_names = _CONVW + ['final_g']
    res = _adam("adam_conv", _pack(gconv + [extra[depth - 1][len(_CONVW)]])[None],
                *[_pack([q[n] for n in tail_names]) for q in (p, m, v)])
    for kind, r in zip(kinds, res):
        for n, arr in zip(tail_names, _unpack(r, [p[n].shape for n in tail_names])):
            out[(kind, n)] = arr
    for n in _BIG:
        if _BIG_T[n]:
            for kind in kinds:
                out[(kind, n)] = jnp.swapaxes(out[(kind, n)], 1, 2)

    flat = [loss, grad_x]
    for kind in ("grad", "delta", "new_m", "new_v"):
        flat += [out[(kind, n)] for n in _NAMES]
    return tuple(flat)


def kernel(x, norm1_g, w_in, conv_a_w, conv_a_b, ln_a_g, ln_a_b, ln_b_g, ln_b_b, w_spatial, b_spatial, conv_c_w, conv_c_b, dt_bias, a_log, d_skip, norm_c_g, w_out, norm2_g, w_ff1, w_ff2, final_g, loss_target, m_norm1_g, m_w_in, m_conv_a_w, m_conv_a_b, m_ln_a_g, m_ln_a_b, m_ln_b_g, m_ln_b_b, m_w_spatial, m_b_spatial, m_conv_c_w, m_conv_c_b, m_dt_bias, m_a_log, m_d_skip, m_norm_c_g, m_w_out, m_norm2_g, m_w_ff1, m_w_ff2, m_final_g, v_norm1_g, v_w_in, v_conv_a_w, v_conv_a_b, v_ln_a_g, v_ln_a_b, v_ln_b_g, v_ln_b_b, v_w_spatial, v_b_spatial, v_conv_c_w, v_conv_c_b, v_dt_bias, v_a_log, v_d_skip, v_norm_c_g, v_w_out, v_norm2_g, v_w_ff1, v_w_ff2, v_final_g):
    p = dict(zip(_NAMES, (norm1_g, w_in, conv_a_w, conv_a_b, ln_a_g, ln_a_b, ln_b_g, ln_b_b, w_spatial, b_spatial, conv_c_w,
                          conv_c_b, dt_bias, a_log, d_skip, norm_c_g, w_out, norm2_g, w_ff1, w_ff2, final_g)))
    m = dict(zip(_NAMES, (m_norm1_g, m_w_in, m_conv_a_w, m_conv_a_b, m_ln_a_g, m_ln_a_b, m_ln_b_g, m_ln_b_b, m_w_spatial,
                          m_b_spatial, m_conv_c_w, m_conv_c_b, m_dt_bias, m_a_log, m_d_skip, m_norm_c_g, m_w_out, m_norm2_g,
                          m_w_ff1, m_w_ff2, m_final_g)))
    v = dict(zip(_NAMES, (v_norm1_g, v_w_in, v_conv_a_w, v_conv_a_b, v_ln_a_g, v_ln_a_b, v_ln_b_g, v_ln_b_b, v_w_spatial,
                          v_b_spatial, v_conv_c_w, v_conv_c_b, v_dt_bias, v_a_log, v_d_skip, v_norm_c_g, v_w_out, v_norm2_g,
                          v_w_ff1, v_w_ff2, v_final_g)))
    return _step(p, m, v, x, loss_target)
```

```python
import functools
import math

import jax
import jax.numpy as jnp
from jax import lax
from jax.experimental import pallas as pl
from jax.experimental.pallas import tpu as pltpu

f32 = jnp.float32
bf16 = jnp.bfloat16
HI = lax.Precision.HIGHEST
EPS = 1e-5
HEAD = 64
CHUNK = 128
KA = 31
KC = 4
N_DEV = 8
VMEM_LIMIT = 56 * 1024 * 1024
MM_VMEM_BUDGET = 52 * 1024 * 1024

ADAM_LR = 0.001
ADAM_B1 = 0.9
ADAM_B2 = 0.999
ADAM_EPS = 1e-08
ADAM_WD = 0.01
ADAM_STEP = 10


def _cparams(sem=None):
    return pltpu.CompilerParams(dimension_semantics=sem, vmem_limit_bytes=VMEM_LIMIT)


def _sds(shape, dtype):
    return jax.ShapeDtypeStruct(shape, dtype)


_DIMS = {"nn": ((1,), (0,)), "nt": ((1,), (1,)), "tn": ((0,), (0,))}


def _dot16(a, b, form):
    return lax.dot_general(a.astype(bf16), b.astype(bf16), (_DIMS[form], ((), ())), preferred_element_type=f32)


@functools.partial(jax.custom_vjp, nondiff_argnums=(2,))
def _bdot(a, b, form):
    return _dot16(a, b, form)


def _bdot_fwd(a, b, form):
    return _dot16(a, b, form), (a, b)


def _bdot_bwd(form, res, ct):
    a, b = res
    if form == "nn":
        da, db = _dot16(ct, b, "nt"), _dot16(a, ct, "tn")
    elif form == "nt":
        da, db = _dot16(ct, b, "nn"), _dot16(ct, a, "tn")
    else:
        da, db = _dot16(b, ct, "nt"), _dot16(a, ct, "nn")
    return da.astype(a.dtype), db.astype(b.dtype)


_bdot.defvjp(_bdot_fwd, _bdot_bwd)


def _tile(n, cap):
    if n <= cap:
        return n
    for d in range(cap - cap % 128, 0, -128):
        if n % d == 0:
            return d
    raise ValueError((n, cap))


def _mm(name, a, b, form, out_dtypes, epilogue=None, extras=(), tm=2048, tn=1024, tk=8192, dep=None, rows=(), n_row_out=0,
        n=None, tok_out=False):
    if form == "tn":
        K, M = a.shape
    else:
        M, K = a.shape
    N = n if n is not None else (b.shape[0] if form == "nt" else b.shape[1])
    tm, tn, tk = _tile(M, tm), _tile(N, tn), _tile(K, tk)
    nk = K // tk

    def vmem_bytes(tm):
        mn = sum(jnp.dtype(e.dtype).itemsize for e in extras) + sum(jnp.dtype(d).itemsize for d in out_dtypes)
        return 2 * (tm * tk * a.dtype.itemsize + tk * tn * b.dtype.itemsize + tm * tn * mn) + 2 * tm * tn * 4

    while vmem_bytes(tm) > MM_VMEM_BUDGET and tm % 256 == 0:
        tm //= 2
    ne, no, nr = len(extras), len(out_dtypes), len(rows)
    assert n_row_out == 0 or tn == N
    deps = () if dep is None else (dep,)
    if epilogue is None:
        epilogue = lambda acc: (acc,)

    def body(a_ref, b_ref, *rest):
        extra_refs, row_refs = rest[:ne], rest[ne:ne + nr]
        rest = rest[ne + nr + len(deps):]
        out_refs, rowout_refs = rest[:no], rest[no:no + n_row_out]
        ntok = 1 if tok_out else 0
        part = lax.dot_general(a_ref[...].astype(bf16), b_ref[...].astype(bf16),
                               (_DIMS[form], ((), ())), preferred_element_type=f32)

        def finish(acc):
            if tok_out:
                rest[no + n_row_out][...] = jnp.zeros((8, 128), f32)
            outs = epilogue(acc, *[e[...] for e in extra_refs], *[r[...] for r in row_refs])
            for o_ref, v in zip(out_refs, outs[:no]):
                o_ref[...] = v.astype(o_ref.dtype)
            for r_ref, v in zip(rowout_refs, outs[no:]):
                @pl.when(pl.program_id(0) == 0)
                def _():
                    r_ref[...] = jnp.zeros_like(r_ref)

                r_ref[0:1, :] += v

        if nk == 1:
            finish(part)
            return
        acc_ref = rest[no + n_row_out + ntok]
        k = pl.program_id(2)

        @pl.when(k == 0)
        def _():
            acc_ref[...] = part

        @pl.when((k > 0) & (k < nk - 1))
        def _():
            acc_ref[...] += part

        @pl.when(k == nk - 1)
        def _():
            finish(acc_ref[...] + part)

    a_spec = pl.BlockSpec((tk, tm), lambda i, j, k: (k, i)) if form == "tn" else pl.BlockSpec((tm, tk), lambda i, j, k: (i, k))
    b_spec = pl.BlockSpec((tn, tk), lambda i, j, k: (j, k)) if form == "nt" else pl.BlockSpec((tk, tn), lambda i, j, k: (k, j))
    mn_spec = pl.BlockSpec((tm, tn), lambda i, j, k: (i, j))
    return pl.pallas_call(
        body, name=name, grid=(M // tm, N // tn, nk),
        in_specs=[a_spec, b_spec] + [mn_spec] * ne + [pl.BlockSpec((1, tn), lambda i, j, k: (0, j))] * nr
        + [pl.BlockSpec((8, 128), lambda i, j, k: (0, 0))] * len(deps),
        out_specs=[mn_spec] * no + [pl.BlockSpec((8, tn), lambda i, j, k: (0, j))] * n_row_out
        + [pl.BlockSpec((8, 128), lambda i, j, k: (0, 0))] * (1 if tok_out else 0),
        out_shape=[_sds((M, N), d) for d in out_dtypes] + [_sds((8, N), f32)] * n_row_out
        + [_sds((8, 128), f32)] * (1 if tok_out else 0),
        scratch_shapes=[pltpu.VMEM((tm, tn), f32)] if nk > 1 else [],
        compiler_params=_cparams(("arbitrary", "arbitrary", "arbitrary")),
    )(a, b, *extras, *rows, *deps)


def _ep_add(acc, r):
    return (acc + r,)


def _ep_add_rms(acc, r, g):
    x = acc + r
    return x, _rms(x, g)


def _ep_rms_bwd(acc, x, dres, g):
    _, vjp = jax.vjp(_rms, x, g)
    dx, dg = vjp(acc)
    return dres + dx, dg


def _ep_add_rms_bwd(acc, more, x, dres, g):
    return _ep_rms_bwd(acc + more, x, dres, g)


def _ep_relu2(acc):
    r = jnp.maximum(acc, 0.0)
    return acc, r * r


def _ep_drelu2(acc, f):
    return (acc * 2.0 * jnp.maximum(f, 0.0),)


def _rms(x, g):
    return x * lax.rsqrt(jnp.mean(x * x, axis=-1, keepdims=True) + EPS) * g


TT = 512


def _rms_fwd(x, g):
    T, D = x.shape

    def body(x_ref, g_ref, h_ref, tok_ref):
        h_ref[...] = _rms(x_ref[...], g_ref[...]).astype(bf16)
        tok_ref[...] = jnp.zeros_like(tok_ref)

    return pl.pallas_call(
        body, name="rms_fwd", grid=(T // TT,),
        in_specs=[pl.BlockSpec((TT, D), lambda i: (i, 0)), pl.BlockSpec((1, D), lambda i: (0, 0))],
        out_specs=[pl.BlockSpec((TT, D), lambda i: (i, 0)), pl.BlockSpec((8, 128), lambda i: (0, 0))],
        out_shape=[_sds((T, D), bf16), _sds((8, 128), f32)], compiler_params=_cparams(("arbitrary",)),
    )(x, g.reshape(1, D))


def _loss_head(x, g, tgt):
    T, D = x.shape

    def f(xv, gv, tv):
        e = _rms(xv, gv) - tv
        return 0.5 * jnp.sum(jnp.sum(e * e, axis=-1, keepdims=True) * (1.0 / D), axis=0, keepdims=True)

    def body(x_ref, g_ref, t_ref, loss_ref, dx_ref, dg_ref):
        tv = t_ref[...]
        l, vjp = jax.vjp(lambda xv, gv: f(xv, gv, tv), x_ref[...], g_ref[...])
        dx, dg = vjp(jnp.ones((1, 1), f32))
        dx_ref[...] = dx

        @pl.when(pl.program_id(0) == 0)
        def _():
            dg_ref[...] = jnp.zeros_like(dg_ref)
            loss_ref[...] = jnp.zeros_like(loss_ref)

        dg_ref[0:1, :] += dg
        loss_ref[...] += jnp.broadcast_to(l, loss_ref.shape)

    tile = pl.BlockSpec((TT, D), lambda i: (i, 0))
    return pl.pallas_call(
        body, name="loss_head", grid=(T // TT,),
        in_specs=[tile, pl.BlockSpec((1, D), lambda i: (0, 0)), tile],
        out_specs=[pl.BlockSpec((8, 128), lambda i: (0, 0)), tile, pl.BlockSpec((8, D), lambda i: (0, 0))],
        out_shape=[_sds((8, 128), f32), _sds((T, D), f32), _sds((8, D), f32)],
        compiler_params=_cparams(("arbitrary",)),
    )(x, g.reshape(1, D), tgt)


TB = 256


def _glu(a_val, a_gate):
    return a_val * jax.nn.sigmoid(a_gate)


PAIR = 2 * HEAD


def _pair_mean(x, lo):
    s_lo = jnp.sum(jnp.where(lo, x, 0.0), axis=-1, keepdims=True)
    s_hi = jnp.sum(jnp.where(lo, 0.0, x), axis=-1, keepdims=True)
    return jnp.where(lo, s_lo, s_hi) * (1.0 / HEAD)


def _pair_ln(v, g, b):
    lo = lax.broadcasted_iota(jnp.int32, v.shape, 1) < HEAD
    vc = v - _pair_mean(v, lo)
    var = _pair_mean(vc * vc, lo)
    return vc * lax.rsqrt(var + EPS) * g + b


def _ln_silu(v, g, b):
    return jax.nn.silu(_pair_ln(v, g, b))


def _conv_geom(kw):
    halo = 32 if kw > 9 else 16
    return halo, halo - (kw - 1)


def _residues(shifts):
    return sorted({s % 8 for s in shifts} - {0})


def _shift_copies(src_ref, cp_ref, res, rows, ls):
    for j, r in enumerate(res):
        cp_ref[j, :, ls] = src_ref[pl.ds(r, rows), ls]


def _shifted(src_ref, cp_ref, res, shift, size, ls):
    r = shift % 8
    if r == 0:
        return src_ref[pl.ds(shift, size), ls]
    return cp_ref[res.index(r), pl.ds(shift - r, size), ls]


def _conv_taps(hp_ref, hs_ref, w_ref, b_ref, acc_ref, kw, off, halo, width):
    res = _residues(range(off, off + kw))
    for c in range(width // 128):
        ls = pl.ds(c * 128, 128)
        _shift_copies(hp_ref, hs_ref, res, halo + TB, ls)
        acc = jnp.broadcast_to(b_ref[:, ls], (TB, 128))
        for k in range(kw):
            acc = acc + w_ref[k:k + 1, ls] * _shifted(hp_ref, hs_ref, res, off + k, TB, ls)
        acc_ref[:, ls] = acc


def _conv_fwd(name, src, col_block, w, b, kw, conformer, n_seq, ln_g=None, ln_b=None, share=None):
    T = src.shape[0]
    cout = w.shape[1]
    cin = 2 * cout if conformer else cout
    halo, off = _conv_geom(kw)
    nblk = T // n_seq // TB
    hb = TB // halo
    out_shape, out_blk, shared_in, aliases = _shared(share, T, cout, bf16 if conformer else f32, 6 if conformer else 4)

    def body(cur_ref, halo_ref, w_ref, b_ref, *rest):
        if conformer:
            g_ref, lb_ref = rest[:2]
            rest = rest[2:]
        out_ref, hp_ref, acc_ref, hs_ref = rest[len(shared_in):]
        i = pl.program_id(1)
        first = (i == 0)

        @pl.when((pl.program_id(0) == 0) & first)
        def _():
            hp_ref[pl.ds(halo + TB, 8), :] = jnp.zeros((8, cout), f32)

        if conformer:
            hp_ref[pl.ds(halo, TB), :] = _glu(cur_ref[:, 0:cout].astype(f32), cur_ref[:, cout:cin].astype(f32))
            hh = _glu(halo_ref[:, 0:cout].astype(f32), halo_ref[:, cout:cin].astype(f32))
        else:
            hp_ref[pl.ds(halo, TB), :] = cur_ref[...].astype(f32)
            hh = halo_ref[...].astype(f32)
        hp_ref[pl.ds(0, halo), :] = jnp.where(first, 0.0, hh)
        _conv_taps(hp_ref, hs_ref, w_ref, b_ref, acc_ref, kw, off, halo, cout)
        if conformer:
            for q in range(cout // PAIR):
                ls = pl.ds(q * PAIR, PAIR)
                out_ref[:, ls] = _ln_silu(acc_ref[:, ls], g_ref[:, ls], lb_ref[:, ls]).astype(out_ref.dtype)
        else:
            out_ref[...] = jax.nn.silu(acc_ref[...]).astype(out_ref.dtype)

    nres = len(_residues(range(off, off + kw)))

    row = pl.BlockSpec((1, cout), lambda s, i: (0, 0))
    in_specs = [pl.BlockSpec((TB, cin), lambda s, i: (s * nblk + i, col_block)),
                pl.BlockSpec((halo, cin), lambda s, i: (jnp.maximum((s * nblk + i) * hb - 1, 0), col_block)),
                pl.BlockSpec((w.shape[0], cout), lambda s, i: (0, 0)), row]
    args = [src, src, w, b.reshape(1, cout)]
    if conformer:
        in_specs += [row, row]
        args += [ln_g.reshape(1, cout), ln_b.reshape(1, cout)]
    in_specs += [pl.BlockSpec(memory_space=pl.ANY)] * len(shared_in)
    args += shared_in
    return pl.pallas_call(
        body, name=name, grid=(n_seq, nblk), in_specs=in_specs,
        out_specs=pl.BlockSpec((TB, cout), lambda s, i: (s * nblk + i, out_blk)),
        out_shape=out_shape, input_output_aliases=aliases,
        scratch_shapes=[pltpu.VMEM((halo + TB + 8, cout), f32), pltpu.VMEM((TB, cout), f32),
                        pltpu.VMEM((nres, halo + TB, cout), f32)],
        compiler_params=_cparams(("arbitrary", "arbitrary")),
    )(*args)


def _shared(share, T, width, dtype, n_inputs, out_index=0):
    if share is None:
        return _sds((T, width), dtype), 0, [], {}
    total, blk, into = share
    if into is None:
        return _sds((T, total), dtype), blk, [], {}
    return _sds((T, total), dtype), blk, [into], {n_inputs: out_index}


def _conv_bwd(name, src, col_block, w, b, dy, dy_col_block, kw, conformer, n_seq, ln_g=None, ln_b=None, share=None):
    T = src.shape[0]
    cout = w.shape[1]
    wrows = w.shape[0]
    cin = 2 * cout if conformer else cout
    halo, off = _conv_geom(kw)
    nblk = T // n_seq // TB
    hb = TB // halo
    dsrc_shape, dsrc_blk, shared_in, aliases = _shared(share, T, cin, bf16, 7 if conformer else 5)

    def body(cur_ref, halo_ref, w_ref, b_ref, dy_ref, *rest):
        if conformer:
            g_ref, lb_ref = rest[:2]
            rest = rest[2:]
        rest = rest[len(shared_in):]
        if conformer:
            dsrc_ref, dw_ref, db_ref, dg_ref, dlb_ref, hp_ref, acc_ref, dz_ref, dhp_ref, carry_ref, hs_ref, dzs_ref = rest
        else:
            dsrc_ref, dw_ref, db_ref, hp_ref, acc_ref, dz_ref, dhp_ref, carry_ref, hs_ref, dzs_ref = rest
        s, ii = pl.program_id(0), pl.program_id(1)
        i = nblk - 1 - ii
        first = (i == 0)

        @pl.when((s == 0) & (ii == 0))
        def _():
            dw_ref[...] = jnp.zeros_like(dw_ref)
            db_ref[...] = jnp.zeros_like(db_ref)
            hp_ref[pl.ds(halo + TB, 8), :] = jnp.zeros((8, cout), f32)
            if conformer:
                dg_ref[...] = jnp.zeros_like(dg_ref)
                dlb_ref[...] = jnp.zeros_like(dlb_ref)

        @pl.when(ii == 0)
        def _():
            carry_ref[...] = jnp.zeros_like(carry_ref)
            dz_ref[pl.ds(0, halo), :] = jnp.zeros((halo, cout), f32)
            dz_ref[pl.ds(halo + TB, halo), :] = jnp.zeros((halo, cout), f32)

        if conformer:
            hp_ref[pl.ds(halo, TB), :] = _glu(cur_ref[:, 0:cout].astype(f32), cur_ref[:, cout:cin].astype(f32))
            hh = _glu(halo_ref[:, 0:cout].astype(f32), halo_ref[:, cout:cin].astype(f32))
        else:
            hp_ref[pl.ds(halo, TB), :] = cur_ref[...].astype(f32)
            hh = halo_ref[...].astype(f32)
        hp_ref[pl.ds(0, halo), :] = jnp.where(first, 0.0, hh)
        _conv_taps(hp_ref, hs_ref, w_ref, b_ref, acc_ref, kw, off, halo, cout)

        if conformer:
            for q in range(cout // PAIR):
                ls = pl.ds(q * PAIR, PAIR)
                _, vjp = jax.vjp(_ln_silu, acc_ref[:, ls], g_ref[:, ls], lb_ref[:, ls])
                da, dg, dlb = vjp(dy_ref[:, ls].astype(f32))
                dz_ref[pl.ds(halo, TB), ls] = da
                dg_ref[0:1, ls] += dg
                dlb_ref[0:1, ls] += dlb
        else:
            _, vjp = jax.vjp(jax.nn.silu, acc_ref[...])
            dz_ref[pl.ds(halo, TB), :] = vjp(dy_ref[...].astype(f32))[0]

        res_h = _residues(range(off, off + kw))
        res_z = _residues(range(kw))
        for c in range(cout // 128):
            ls = pl.ds(c * 128, 128)
            _shift_copies(dz_ref, dzs_ref, res_z, halo + TB + halo - 8, ls)
            dacc = dz_ref[pl.ds(halo, TB), ls]
            db_ref[0:1, ls] += jnp.sum(dacc, axis=0, keepdims=True)
            dhp = jnp.zeros((halo + TB, 128), f32)
            for k in range(kw):
                dw_ref[k:k + 1, ls] += jnp.sum(dacc * _shifted(hp_ref, hs_ref, res_h, off + k, TB, ls), axis=0, keepdims=True)
                dhp = dhp + w_ref[k:k + 1, ls] * _shifted(dz_ref, dzs_ref, res_z, kw - 1 - k, halo + TB, ls)
            dhp_ref[:, ls] = dhp
        dhp_ref[pl.ds(TB, halo), :] += carry_ref[...]
        carry_ref[...] = dhp_ref[pl.ds(0, halo), :]
        dcur = dhp_ref[pl.ds(halo, TB), :]
        if conformer:
            _, vjp = jax.vjp(_glu, cur_ref[:, 0:cout].astype(f32), cur_ref[:, cout:cin].astype(f32))
            dval, dgate = vjp(dcur)
            dsrc_ref[:, 0:cout] = dval.astype(dsrc_ref.dtype)
            dsrc_ref[:, cout:cin] = dgate.astype(dsrc_ref.dtype)
        else:
            dsrc_ref[...] = dcur.astype(dsrc_ref.dtype)

    def blk(s, ii):
        return s * nblk + (nblk - 1 - ii)

    row = pl.BlockSpec((1, cout), lambda s, ii: (0, 0))
    acc8 = pl.BlockSpec((8, cout), lambda s, ii: (0, 0))
    in_specs = [pl.BlockSpec((TB, cin), lambda s, ii: (blk(s, ii), col_block)),
                pl.BlockSpec((halo, cin), lambda s, ii: (jnp.maximum(blk(s, ii) * hb - 1, 0), col_block)),
                pl.BlockSpec((wrows, cout), lambda s, ii: (0, 0)), row,
                pl.BlockSpec((TB, cout), lambda s, ii: (blk(s, ii), dy_col_block))]
    args = [src, src, w, b.reshape(1, cout), dy]
    out_specs = [pl.BlockSpec((TB, cin), lambda s, ii: (blk(s, ii), dsrc_blk)),
                 pl.BlockSpec((wrows, cout), lambda s, ii: (0, 0)), acc8]
    out_shape = [dsrc_shape, _sds((wrows, cout), f32), _sds((8, cout), f32)]
    if conformer:
        in_specs += [row, row]
        args += [ln_g.reshape(1, cout), ln_b.reshape(1, cout)]
        out_specs += [acc8, acc8]
        out_shape += [_sds((8, cout), f32), _sds((8, cout), f32)]
    in_specs += [pl.BlockSpec(memory_space=pl.ANY)] * len(shared_in)
    args += shared_in
    return pl.pallas_call(
        body, name=name, grid=(n_seq, nblk), in_specs=in_specs, out_specs=out_specs, out_shape=out_shape,
        input_output_aliases=aliases,
        scratch_shapes=[pltpu.VMEM((halo + TB + 8, cout), f32), pltpu.VMEM((TB, cout), f32),
                        pltpu.VMEM((halo + TB + halo, cout), f32), pltpu.VMEM((halo + TB, cout), f32),
                        pltpu.VMEM((halo, cout), f32),
                        pltpu.VMEM((len(_residues(range(off, off + kw))), halo + TB, cout), f32),
                        pltpu.VMEM((len(_residues(range(kw))), halo + TB + halo - 8, cout), f32)],
        compiler_params=_cparams(("arbitrary", "arbitrary")),
    )(*args)


def _gelu(x):
    return 0.5 * x * (1.0 + lax.erf(x * (1.0 / math.sqrt(2.0))))


def _tril_mask(n):
    r = lax.broadcasted_iota(jnp.int32, (n, n), 0)
    c = lax.broadcasted_iota(jnp.int32, (n, n), 1)
    return r >= c


def _head_spread(nh):
    r = lax.broadcasted_iota(jnp.int32, (nh, nh * HEAD), 0)
    c = lax.broadcasted_iota(jnp.int32, (nh, nh * HEAD), 1)
    return (c // HEAD == r).astype(f32)


def _gmlp_bias(bs):
    return lax.dot_general(bs, _head_spread(bs.shape[0]), (((0,), (0,)), ((), ())), precision=HI, preferred_element_type=f32)


def _gmlp_pair(bu, bv, g, b, w_a, w_b, bias):
    lo = lax.broadcasted_iota(jnp.int32, bu.shape, 1) < HEAD
    tril = _tril_mask(CHUNK)
    u = _gelu(bu)
    vn = _pair_ln(_gelu(bv), g, b)
    mix = jnp.where(lo, _bdot(jnp.where(tril, w_a, 0.0), vn, "nn"), _bdot(jnp.where(tril, w_b, 0.0), vn, "nn"))
    return u * (mix + bias)


def _gmlp_fwd(proj, col_block, ln_g, ln_b, w_s, b_s, share=None):
    T = proj.shape[0]
    nh = w_s.shape[0]
    width = nh * HEAD
    out_shape, out_blk, shared_in, aliases = _shared(share, T, width, bf16, 5)

    def body(p_ref, g_ref, b_ref, w_ref, bs_ref, *rest):
        out_ref, bias_ref = rest[len(shared_in):]

        @pl.when(pl.program_id(0) == 0)
        def _():
            bias_ref[...] = _gmlp_bias(bs_ref[...])

        for q in range(nh // 2):
            ls = pl.ds(q * PAIR, PAIR)
            lv = pl.ds(width + q * PAIR, PAIR)
            out_ref[:, ls] = _gmlp_pair(p_ref[:, ls].astype(f32), p_ref[:, lv].astype(f32), g_ref[:, ls], b_ref[:, ls], w_ref[2 * q], w_ref[2 * q + 1],
                                        bias_ref[:, ls]).astype(out_ref.dtype)

    row = pl.BlockSpec((1, width), lambda i: (0, 0))
    return pl.pallas_call(
        body, name="gmlp_fwd", grid=(T // CHUNK,),
        in_specs=[pl.BlockSpec((CHUNK, 2 * width), lambda i: (i, col_block)), row, row,
                  pl.BlockSpec((nh, CHUNK, CHUNK), lambda i: (0, 0, 0)), pl.BlockSpec((nh, CHUNK), lambda i: (0, 0))]
        + [pl.BlockSpec(memory_space=pl.ANY)] * len(shared_in),
        out_specs=pl.BlockSpec((CHUNK, width), lambda i: (i, out_blk)),
        out_shape=out_shape, input_output_aliases=aliases, scratch_shapes=[pltpu.VMEM((CHUNK, width), f32)],
        compiler_params=_cparams(("arbitrary",)),
    )(proj, ln_g.reshape(1, width), ln_b.reshape(1, width), w_s, b_s, *shared_in)


def _gmlp_bwd(proj, col_block, ln_g, ln_b, w_s, b_s, dy, dy_col_block, share=None):
    T = proj.shape[0]
    nh = w_s.shape[0]
    width = nh * HEAD
    nstep = T // CHUNK
    dp_shape, dp_blk, shared_in, aliases = _shared(share, T, 2 * width, bf16, 6)

    def body(p_ref, g_ref, b_ref, w_ref, bs_ref, dy_ref, *rest):
        dp_ref, dg_ref, db_ref, dw_ref, dbst_ref, bias_ref, dbias_ref = rest[len(shared_in):]

        @pl.when(pl.program_id(0) == 0)
        def _():
            dg_ref[...] = jnp.zeros_like(dg_ref)
            db_ref[...] = jnp.zeros_like(db_ref)
            dw_ref[...] = jnp.zeros_like(dw_ref)
            dbias_ref[...] = jnp.zeros_like(dbias_ref)
            bias_ref[...] = _gmlp_bias(bs_ref[...])

        for q in range(nh // 2):
            ls = pl.ds(q * PAIR, PAIR)
            lv = pl.ds(width + q * PAIR, PAIR)
            _, vjp = jax.vjp(_gmlp_pair, p_ref[:, ls].astype(f32), p_ref[:, lv].astype(f32), g_ref[:, ls], b_ref[:, ls], w_ref[2 * q], w_ref[2 * q + 1],
                             bias_ref[:, ls])
            dbu, dbv, dg, db, dw_a, dw_b, dbias = vjp(dy_ref[:, ls].astype(f32))
            dp_ref[:, ls] = dbu.astype(dp_ref.dtype)
            dp_ref[:, lv] = dbv.astype(dp_ref.dtype)
            dg_ref[0:1, ls] += dg
            db_ref[0:1, ls] += db
            dw_ref[2 * q] += dw_a
            dw_ref[2 * q + 1] += dw_b
            dbias_ref[:, ls] += dbias

        @pl.when(pl.program_id(0) == nstep - 1)
        def _():
            dbst_ref[...] = lax.dot_general(dbias_ref[...], _head_spread(nh), (((1,), (1,)), ((), ())),
                                            precision=HI, preferred_element_type=f32)

    row = pl.BlockSpec((1, width), lambda i: (0, 0))
    acc8 = pl.BlockSpec((8, width), lambda i: (0, 0))
    wspec = pl.BlockSpec((nh, CHUNK, CHUNK), lambda i: (0, 0, 0))
    res = pl.pallas_call(
        body, name="gmlp_bwd", grid=(nstep,),
        in_specs=[pl.BlockSpec((CHUNK, 2 * width), lambda i: (i, col_block)), row, row, wspec,
                  pl.BlockSpec((nh, CHUNK), lambda i: (0, 0)), pl.BlockSpec((CHUNK, width), lambda i: (i, dy_col_block))]
        + [pl.BlockSpec(memory_space=pl.ANY)] * len(shared_in),
        out_specs=[pl.BlockSpec((CHUNK, 2 * width), lambda i: (i, dp_blk)), acc8, acc8, wspec,
                   pl.BlockSpec((CHUNK, nh), lambda i: (0, 0))],
        out_shape=[dp_shape, _sds((8, width), f32), _sds((8, width), f32),
                   _sds((nh, CHUNK, CHUNK), f32), _sds((CHUNK, nh), f32)],
        input_output_aliases=aliases,
        scratch_shapes=[pltpu.VMEM((CHUNK, width), f32), pltpu.VMEM((CHUNK, width), f32)],
        compiler_params=_cparams(("arbitrary",)),
    )(proj, ln_g.reshape(1, width), ln_b.reshape(1, width), w_s, b_s, dy, *shared_in)
    return res[0], res[1], res[2], res[3], res[4].T


def _sel_col(x, h):
    lane = lax.broadcasted_iota(jnp.int32, x.shape, 1)
    return jnp.sum(jnp.where(lane == h, x, 0.0), axis=1, keepdims=True)


def _sel_row(x, h):
    sub = lax.broadcasted_iota(jnp.int32, x.shape, 0)
    return jnp.sum(jnp.where(sub == h, x, 0.0), axis=0, keepdims=True)


def _ssd_chunk(nh, ngrp, xs_l, z_l, b_l, c_l, dtraw, dtb, alog, dskip, ng_l, prev_l):
    hg = nh // ngrp
    tril = _tril_mask(CHUNK)
    tl = tril.astype(f32)
    lo = lax.broadcasted_iota(jnp.int32, (CHUNK, PAIR), 1) < HEAD
    lo_row = lo[0:1, :]
    dt = jax.nn.softplus(dtraw + dtb)
    a = dt * (-jnp.exp(alog))
    cs = jnp.dot(tl, a, precision=HI, preferred_element_type=f32)
    cst = lax.dot_general(a, tl, (((0,), (1,)), ((), ())), precision=HI, preferred_element_type=f32)
    cb_l = [_bdot(c_l[g], b_l[g], "nt") for g in range(ngrp)]
    yz_l, new_prev = [], []
    for q in range(nh // 2):
        g = (2 * q) // hg
        cols = []
        for h in (2 * q, 2 * q + 1):
            cs_h = _sel_col(cs, h)
            tot = _sel_row(cs_h, CHUNK - 1)
            seg = jnp.where(tril, cs_h - _sel_row(cst, h), 0.0)
            lmat = jnp.where(tril, jnp.exp(seg), 0.0)
            cols.append((_sel_col(dt, h), cs_h, tot, lmat, _sel_col(dskip, h)))
        (dt_a, cs_a, tot_a, l_a, dsk_a), (dt_b, cs_b, tot_b, l_b, dsk_b) = cols
        xs = xs_l[q]
        x = xs * jnp.where(lo, dt_a, dt_b)
        ydiag = jnp.where(lo, _bdot(cb_l[g] * l_a, x, "nn"), _bdot(cb_l[g] * l_b, x, "nn"))
        yoff = _bdot(c_l[g], prev_l[q], "nn") * jnp.where(lo, jnp.exp(cs_a), jnp.exp(cs_b))
        xdec = x * jnp.where(lo, jnp.exp(tot_a - cs_a), jnp.exp(tot_b - cs_b))
        st = _bdot(b_l[g], xdec, "tn")
        new_prev.append(prev_l[q] * jnp.where(lo_row, jnp.exp(tot_a), jnp.exp(tot_b)) + st)
        y = ydiag + yoff + jnp.where(lo_row, dsk_a, dsk_b) * xs
        yz_l.append(y * jax.nn.silu(z_l[q]))
    out = [None] * (nh // 2)
    qg = hg // 2
    for g in range(ngrp):
        ssq = sum(jnp.sum(yz_l[q] * yz_l[q], axis=-1, keepdims=True) for q in range(g * qg, (g + 1) * qg))
        r = lax.rsqrt(ssq * (1.0 / (hg * HEAD)) + EPS)
        for q in range(g * qg, (g + 1) * qg):
            out[q] = yz_l[q] * r * ng_l[q]
    return out, new_prev


def _ssd_read(nh, ngrp, nst, xbc_ref, z_ref, ng_ref, st_ref):
    cw = nh * HEAD
    xs_l = [xbc_ref[:, pl.ds(q * PAIR, PAIR)] for q in range(nh // 2)]
    b_l = [xbc_ref[:, pl.ds(cw + g * nst, nst)] for g in range(ngrp)]
    c_l = [xbc_ref[:, pl.ds(cw + ngrp * nst + g * nst, nst)] for g in range(ngrp)]
    z_l = [z_ref[:, pl.ds(q * PAIR, PAIR)].astype(f32) for q in range(nh // 2)]
    ng_l = [ng_ref[:, pl.ds(q * PAIR, PAIR)] for q in range(nh // 2)]
    prev_l = [st_ref[:, pl.ds(q * PAIR, PAIR)] for q in range(nh // 2)]
    return xs_l, z_l, b_l, c_l, ng_l, prev_l


def _ssd_fwd(xbc, proj, z_col_block, pdt, dtb, alog, dskip, ng, nh, ngrp, nst, n_seq, share=None):
    T = xbc.shape[0]
    cw = nh * HEAD
    nchunk = T // n_seq // CHUNK
    assert nst == CHUNK
    y_shape, y_blk, shared_in, aliases = _shared(share, T, cw, bf16, 7)

    def body(xbc_ref, z_ref, dt_ref, dtb_ref, alog_ref, dskip_ref, ng_ref, *rest):
        y_ref, sin_ref, st_ref = rest[len(shared_in):]

        @pl.when(pl.program_id(1) == 0)
        def _():
            st_ref[...] = jnp.zeros_like(st_ref)

        sin_ref[...] = st_ref[...]
        xs_l, z_l, b_l, c_l, ng_l, prev_l = _ssd_read(nh, ngrp, nst, xbc_ref, z_ref, ng_ref, st_ref)
        y_l, new_prev = _ssd_chunk(nh, ngrp, xs_l, z_l, b_l, c_l, dt_ref[...], dtb_ref[...], alog_ref[...],
                                   dskip_ref[...], ng_l, prev_l)
        for q in range(nh // 2):
            ls = pl.ds(q * PAIR, PAIR)
            y_ref[:, ls] = y_l[q].astype(y_ref.dtype)
            st_ref[:, ls] = new_prev[q]

    def blk(s, c):
        return s * nchunk + c

    prow = pl.BlockSpec((1, 128), lambda s, c: (0, 0))
    return pl.pallas_call(
        body, name="ssd_fwd", grid=(n_seq, nchunk),
        in_specs=[pl.BlockSpec((CHUNK, xbc.shape[1]), lambda s, c: (blk(s, c), 0)),
                  pl.BlockSpec((CHUNK, cw), lambda s, c: (blk(s, c), z_col_block)),
                  pl.BlockSpec((CHUNK, 128), lambda s, c: (blk(s, c), 0)),
                  prow, prow, prow, pl.BlockSpec((1, cw), lambda s, c: (0, 0))]
        + [pl.BlockSpec(memory_space=pl.ANY)] * len(shared_in),
        out_specs=[pl.BlockSpec((CHUNK, cw), lambda s, c: (blk(s, c), y_blk)),
                   pl.BlockSpec((nst, cw), lambda s, c: (blk(s, c), 0))],
        out_shape=[y_shape, _sds((T, cw), f32)], input_output_aliases=aliases,
        scratch_shapes=[pltpu.VMEM((nst, cw), f32)],
        compiler_params=_cparams(("arbitrary", "arbitrary")),
    )(xbc, proj, pdt, dtb, alog, dskip, ng.reshape(1, cw), *shared_in)


def _ssd_bwd(xbc, proj, z_col_block, pdt, dtb, alog, dskip, ng, sin, dy, dy_col_block, nh, ngrp, nst, n_seq, share=None):
    T, xw = xbc.shape
    cw = nh * HEAD
    nchunk = T // n_seq // CHUNK
    dz_shape, dz_blk, shared_in, aliases = _shared(share, T, cw, bf16, 9, out_index=1)

    def body(xbc_ref, z_ref, dt_ref, dtb_ref, alog_ref, dskip_ref, ng_ref, sin_ref, dy_ref, *rest):
        dxbc_ref, dz_ref, ddt_ref, ddtb_ref, dalog_ref, ddskip_ref, dng_ref, dst_ref = rest[len(shared_in):]
        s, cc = pl.program_id(0), pl.program_id(1)

        @pl.when((s == 0) & (cc == 0))
        def _():
            ddtb_ref[...] = jnp.zeros_like(ddtb_ref)
            dalog_ref[...] = jnp.zeros_like(dalog_ref)
            ddskip_ref[...] = jnp.zeros_like(ddskip_ref)
            dng_ref[...] = jnp.zeros_like(dng_ref)

        @pl.when(cc == 0)
        def _():
            dst_ref[...] = jnp.zeros_like(dst_ref)

        xs_l, z_l, b_l, c_l, ng_l, prev_l = _ssd_read(nh, ngrp, nst, xbc_ref, z_ref, ng_ref, sin_ref)
        _, vjp = jax.vjp(functools.partial(_ssd_chunk, nh, ngrp), xs_l, z_l, b_l, c_l, dt_ref[...], dtb_ref[...],
                         alog_ref[...], dskip_ref[...], ng_l, prev_l)
        dy_l = [dy_ref[:, pl.ds(q * PAIR, PAIR)].astype(f32) for q in range(nh // 2)]
        dst_l = [dst_ref[:, pl.ds(q * PAIR, PAIR)] for q in range(nh // 2)]
        dxs_l, dz_l, db_l, dc_l, ddt, ddtb, dalog, ddskip, dng_l, dprev_l = vjp((dy_l, dst_l))
        for q in range(nh // 2):
            ls = pl.ds(q * PAIR, PAIR)
            dxbc_ref[:, ls] = dxs_l[q]
            dz_ref[:, ls] = dz_l[q].astype(dz_ref.dtype)
            dng_ref[0:1, ls] += dng_l[q]
            dst_ref[:, ls] = dprev_l[q]
        for g in range(ngrp):
            dxbc_ref[:, pl.ds(cw + g * nst, nst)] = db_l[g]
            dxbc_ref[:, pl.ds(cw + ngrp * nst + g * nst, nst)] = dc_l[g]
        ddt_ref[...] = ddt.astype(ddt_ref.dtype)
        ddtb_ref[0:1, :] += ddtb
        dalog_ref[0:1, :] += dalog
        ddskip_ref[0:1, :] += ddskip

    def blk(s, cc):
        return s * nchunk + (nchunk - 1 - cc)

    prow = pl.BlockSpec((1, 128), lambda s, c: (0, 0))
    pacc = pl.BlockSpec((8, 128), lambda s, c: (0, 0))
    return pl.pallas_call(
        body, name="ssd_bwd", grid=(n_seq, nchunk),
        in_specs=[pl.BlockSpec((CHUNK, xw), lambda s, c: (blk(s, c), 0)),
                  pl.BlockSpec((CHUNK, cw), lambda s, c: (blk(s, c), z_col_block)),
                  pl.BlockSpec((CHUNK, 128), lambda s, c: (blk(s, c), 0)),
                  prow, prow, prow, pl.BlockSpec((1, cw), lambda s, c: (0, 0)),
                  pl.BlockSpec((nst, cw), lambda s, c: (blk(s, c), 0)),
                  pl.BlockSpec((CHUNK, cw), lambda s, c: (blk(s, c), dy_col_block))]
        + [pl.BlockSpec(memory_space=pl.ANY)] * len(shared_in),
        out_specs=[pl.BlockSpec((CHUNK, xw), lambda s, c: (blk(s, c), 0)),
                   pl.BlockSpec((CHUNK, cw), lambda s, c: (blk(s, c), dz_blk)),
                   pl.BlockSpec((CHUNK, 128), lambda s, c: (blk(s, c), 0)),
                   pacc, pacc, pacc, pl.BlockSpec((8, cw), lambda s, c: (0, 0))],
        out_shape=[_sds((T, xw), f32), dz_shape, _sds((T, 128), bf16),
                   _sds((8, 128), f32), _sds((8, 128), f32), _sds((8, 128), f32), _sds((8, cw), f32)],
        input_output_aliases=aliases,
        scratch_shapes=[pltpu.VMEM((nst, cw), f32)],
        compiler_params=_cparams(("arbitrary", "arbitrary")),
    )(xbc, proj, pdt, dtb, alog, dskip, ng.reshape(1, cw), sin, dy, *shared_in)


_HBM = pl.BlockSpec(memory_space=pltpu.HBM)
_SEM = pl.BlockSpec(memory_space=pltpu.SEMAPHORE)
_EFFECT = pltpu.SideEffectType.DATAFLOW_SIDE_EFFECTING


def _split_copies(n, scatter, src_refs, land_refs, send_sems, recv_sems):
    npeer = N_DEV - 1
    x, y, c = lax.axis_index("x"), lax.axis_index("y"), lax.axis_index("c")
    me = 4 * x + 2 * y + c
    copies = []
    for i in range(n):
        for k in range(1, N_DEV):
            px = 1 - x if k & 4 else x
            py = 1 - y if k & 2 else y
            pc = 1 - c if k & 1 else c
            src = src_refs[i].at[4 * px + 2 * py + pc] if scatter else src_refs[i]
            copies.append(pltpu.make_async_remote_copy(
                src_ref=src, dst_ref=land_refs[i].at[me],
                send_sem=send_sems.at[i * npeer + k - 1], recv_sem=recv_sems.at[i * npeer + k - 1],
                device_id=(px, py, pc), device_id_type=pl.DeviceIdType.MESH))
    return copies


def _exchange_start(name, arrs, scatter):
    n = len(arrs)
    nsem = n * (N_DEV - 1)
    me = 4 * lax.axis_index("x") + 2 * lax.axis_index("y") + lax.axis_index("c")
    lands = []
    for a in arrs:
        own = lax.dynamic_index_in_dim(a, me, 0, keepdims=True) if scatter else a[None]
        full = lax.empty(a.shape if scatter else (N_DEV,) + a.shape, a.dtype)
        lands.append(lax.dynamic_update_slice(full, own, (me,) + (0,) * (full.ndim - 1)))

    def body(*refs):
        src_refs, land_refs = refs[:n], refs[n:2 * n]
        send_sems, recv_sems = refs[2 * n], refs[2 * n + 1]
        token = refs[-1]
        for cp in _split_copies(n, scatter, src_refs, land_refs, send_sems, recv_sems):
            cp.start()
        token[...] = jnp.zeros_like(token)

    res = pl.pallas_call(
        body, name=name,
        out_shape=(pltpu.SemaphoreType.DMA((nsem,)), pltpu.SemaphoreType.DMA((nsem,)),
                   *[pltpu.HBM(a.shape, a.dtype) for a in arrs], *[pltpu.HBM(l.shape, l.dtype) for l in lands],
                   _sds((8, 128), f32)),
        in_specs=[_HBM] * (2 * n),
        out_specs=(_SEM, _SEM, *[_HBM] * (2 * n), pl.BlockSpec(memory_space=pltpu.VMEM)),
        input_output_aliases={j: 2 + j for j in range(2 * n)},
        compiler_params=pltpu.CompilerParams(has_side_effects=_EFFECT),
    )(*[pltpu.with_memory_space_constraint(a, pltpu.HBM) for a in arrs],
      *[pltpu.with_memory_space_constraint(l, pltpu.HBM) for l in lands])
    return (n, scatter, res[0], res[1], res[2:2 + n], res[2 + n:2 + 2 * n]), res[-1]


def _exchange_wait(name, handle, after):
    n, scatter, send_sems, recv_sems, srcs, lands = handle
    after = list(after) if isinstance(after, (list, tuple)) else [after]

    def body(*refs):
        src_refs, land_refs = refs[:n], refs[n:2 * n]
        for cp in _split_copies(n, scatter, src_refs, land_refs, refs[2 * n], refs[2 * n + 1]):
            cp.wait_send()
            cp.wait_recv()

    res = pl.pallas_call(
        body, name=name,
        out_shape=[pltpu.HBM(a.shape, a.dtype) for a in (*srcs, *lands)],
        in_specs=[_HBM] * (2 * n) + [_SEM, _SEM] + [pl.BlockSpec(memory_space=pl.ANY)] * len(after),
        out_specs=[_HBM] * (2 * n),
        input_output_aliases={j: j for j in range(2 * n)},
        compiler_params=pltpu.CompilerParams(has_side_effects=_EFFECT),
    )(*srcs, *lands, send_sems, recv_sems, *after)
    return res[n:]


def _adam_tiles(R, C):
    if R % 256 == 0:
        return (256, C), (R // 256, 1)
    assert C % 128 == 0
    return (R, 128), (1, C // 128)


def _adam(name, parts, w, m, v, layer=None, depth=None, into=None, stacked_in=False):
    P, R, C = parts.shape
    (tr, tc), (gr, gc) = _adam_tiles(R, C)
    c1 = 1.0 / (1.0 - ADAM_B1 ** ADAM_STEP)
    c2 = 1.0 / (1.0 - ADAM_B2 ** ADAM_STEP)
    into = [] if into is None else list(into)

    def body(p_ref, w_ref, m_ref, v_ref, *rest):
        g_ref, d_ref, nm_ref, nv_ref = rest[len(into):]
        g = p_ref[0].astype(f32)
        for s in range(1, P):
            g = g + p_ref[s].astype(f32)
        nm = ADAM_B1 * m_ref[...] + (1.0 - ADAM_B1) * g
        nv = ADAM_B2 * v_ref[...] + (1.0 - ADAM_B2) * (g * g)
        g_ref[...] = g
        nm_ref[...] = nm
        nv_ref[...] = nv
        d_ref[...] = -ADAM_LR * ((nm * c1) / (jnp.sqrt(nv * c2) + ADAM_EPS) + ADAM_WD * w_ref[...])

    tile = pl.BlockSpec((tr, tc), lambda i, j: (i, j))
    layer_tile = pl.BlockSpec((None, tr, tc), lambda i, j: (layer, i, j))
    out_tile, out_sds = (tile, _sds((R, C), f32)) if layer is None else (layer_tile, _sds((depth, R, C), f32))
    return pl.pallas_call(
        body, name=name, grid=(gr, gc),
        in_specs=[pl.BlockSpec((P, tr, tc), lambda i, j: (0, i, j))] + [layer_tile if stacked_in else tile] * 3
        + [pl.BlockSpec(memory_space=pl.ANY)] * len(into),
        out_specs=[out_tile] * 4, out_shape=[out_sds] * 4,
        input_output_aliases={4 + k: k for k in range(len(into))},
        compiler_params=_cparams(("arbitrary", "arbitrary")),
    )(parts, w, m, v, *into)


def _sum_parts(name, parts):
    P, R, C = parts.shape
    tr = 256 if R % 256 == 0 else R

    def body(p_ref, o_ref):
        g = p_ref[0]
        for s in range(1, P):
            g = g + p_ref[s]
        o_ref[...] = g

    return pl.pallas_call(
        body, name=name, grid=(R // tr,),
        in_specs=[pl.BlockSpec((P, tr, C), lambda i: (0, i, 0))], out_specs=pl.BlockSpec((tr, C), lambda i: (i, 0)),
        out_shape=_sds((R, C), f32), compiler_params=_cparams(("arbitrary",)),
    )(parts)


def _pad_to(a, n, axis):
    if a.shape[axis] == n:
        return a
    cfg = [(0, 0)] * a.ndim
    cfg[axis] = (0, n - a.shape[axis])
    return jnp.pad(a, cfg)


def _pack(arrs):
    flat = [_pad_to(a.reshape(-1), -(-a.size // 128) * 128, 0) for a in arrs]
    rows = jnp.concatenate(flat).reshape(-1, 128)
    return _pad_to(rows, -(-rows.shape[0] // 256) * 256, 0)


def _unpack(slab, shapes):
    flat = slab.reshape(-1)
    out, o = [], 0
    for s in shapes:
        n = math.prod(s)
        out.append(flat[o:o + n].reshape(s))
        o += -(-n // 128) * 128
    return out


_NAMES = ['norm1_g', 'w_in', 'conv_a_w', 'conv_a_b', 'ln_a_g', 'ln_a_b', 'ln_b_g', 'ln_b_b', 'w_spatial', 'b_spatial',
          'conv_c_w', 'conv_c_b', 'dt_bias', 'a_log', 'd_skip', 'norm_c_g', 'w_out', 'norm2_g', 'w_ff1', 'w_ff2', 'final_g']
_REPL = ['norm1_g', 'conv_a_b', 'ln_a_g', 'ln_a_b', 'ln_b_g', 'ln_b_b', 'w_spatial', 'b_spatial', 'conv_c_b',
         'dt_bias', 'a_log', 'd_skip', 'norm_c_g', 'norm2_g']
_CONVW = ['conv_a_w', 'conv_c_w']
_BIG = ['w_in', 'w_out', 'w_ff1', 'w_ff2']
_BIG_T = {'w_in': True, 'w_out': False, 'w_ff1': True, 'w_ff2': False}


def _row128(v):
    return _pad_to(v.reshape(1, -1), 128, 1)


def _step(p, m, v, x, loss_target):
    nb, S, D = x.shape
    T = nb * S
    depth = p['norm1_g'].shape[0]
    a_w = p['conv_a_b'].shape[1]
    b_w = p['ln_b_g'].shape[1]
    nh = p['dt_bias'].shape[1]
    c_w = p['norm_c_g'].shape[1]
    xw = p['conv_c_b'].shape[1]
    ngrp = 2
    nst = (xw - c_w) // (2 * ngrp)
    d_in = p['w_in'].shape[2] * N_DEV
    main = d_in - nh
    assert main == 2 * a_w + 2 * b_w + c_w + xw and 2 * a_w == 2 * b_w == c_w and xw % c_w == c_w // 2
    me = 4 * lax.axis_index("x") + 2 * lax.axis_index("y") + lax.axis_index("c")

    x2 = x.reshape(T, D)
    tgt = loss_target.reshape(T, D)

    def shards(i, z=None):
        z = 0.0 if z is None else z
        return [(p['w_in'][i].T + z).astype(bf16), (p['w_out'][i] + z).astype(bf16), (p['w_ff1'][i].T + z).astype(bf16),
                (p['w_ff2'][i] + z).astype(bf16), p['conv_a_w'][i], p['conv_c_w'][i]]

    def gathered_in(wt, ca, cc):
        wt = wt.reshape(d_in, D)
        ca = jnp.transpose(ca, (1, 0, 2)).reshape(KA, a_w)
        cc = jnp.transpose(cc, (1, 0, 2)).reshape(KC, xw)
        return dict(wt=wt, wt_dt=_pad_to(wt[main:], 128, 0), ca=_pad_to(ca, 32, 0), cc=_pad_to(cc, 8, 0))

    def start_layer(i, after=None):
        sh = shards(i, None if after is None else after[0, 0])
        ha, t = _exchange_start("gather_w%da_start" % i, [sh[0], sh[4], sh[5]], False)
        hb, t = _exchange_start("gather_w%db_start" % i, [shards(i, t[0, 0])[1]], False)
        hc, t = _exchange_start("gather_w%dc_start" % i, [shards(i, t[0, 0])[2]], False)
        hd, t = _exchange_start("gather_w%dd_start" % i, [shards(i, t[0, 0])[3]], False)
        return dict(a=ha, b=hb, c=hc, d=hd), t

    W, saved = [], []
    xc = x2
    H, tok = start_layer(0)
    h1, rtok = _rms_fwd(xc, p['norm1_g'][0])
    for i in range(depth):
        w = gathered_in(*_exchange_wait("gather_w%da_wait" % i, H['a'], [xc, tok]))
        W.append(w)
        Hi = H
        if i + 1 < depth:
            H, tok = start_layer(i + 1, rtok + tok)
        else:
            tok = None
        (proj,) = _mm("mm_proj", h1, w['wt'], "nt", [bf16], dep=tok, n=main)
        (pdt,) = _mm("mm_pdt", h1, w['wt_dt'], "nt", [f32])
        mixw = a_w + b_w + c_w
        ycat = _conv_fwd("confa_fwd", proj, 0, w['ca'], p['conv_a_b'][i], KA, True, nb, p['ln_a_g'][i], p['ln_a_b'][i],
                         share=(mixw, 0, None))
        ycat = _gmlp_fwd(proj, 1, p['ln_b_g'][i], p['ln_b_b'][i], p['w_spatial'][i], p['b_spatial'][i], share=(mixw, 1, ycat))
        xbc = _conv_fwd("convc_fwd", proj, 2, w['cc'], p['conv_c_b'][i], KC, False, nb)
        dtb, alog, dsk = _row128(p['dt_bias'][i]), _row128(p['a_log'][i]), _row128(p['d_skip'][i])
        ycat, sin = _ssd_fwd(xbc, proj, 2, pdt, dtb, alog, dsk, p['norm_c_g'][i], nh, ngrp, nst, nb, share=(mixw, 1, ycat))
        w['wout'] = _exchange_wait("gather_w%db_wait" % i, Hi['b'], ycat)[0].reshape(-1, D)
        xm, h2 = _mm("mm_out", ycat, w['wout'], "nn", [f32, bf16], _ep_add_rms, (xc,), rows=(p['norm2_g'][i].reshape(1, D),))
        w['w1t'] = _exchange_wait("gather_w%dc_wait" % i, Hi['c'], h2)[0].reshape(-1, D)
        f, a = _mm("mm_ff1", h2, w['w1t'], "nt", [bf16, bf16], _ep_relu2)
        w['w2'] = _exchange_wait("gather_w%dd_wait" % i, Hi['d'], a)[0].reshape(-1, D)
        saved.append(dict(x_in=xc, h1=h1, proj=proj, pdt=pdt, xbc=xbc, sin=sin, ycat=ycat, xm=xm, h2=h2, f=f, a=a,
                          dtb=dtb, alog=alog, dsk=dsk))
        if i + 1 < depth:
            xc, h1, rtok = _mm("mm_ff2", a, w['w2'], "nn", [f32, bf16], _ep_add_rms, (xm,), tok_out=True,
                               rows=(p['norm1_g'][i + 1].reshape(1, D),))
        else:
            (xc,) = _mm("mm_ff2", a, w['w2'], "nn", [f32], _ep_add, (xm,))

    lp, dx, dfinal = _loss_head(xc, p['final_g'], tgt)
    loss = lax.psum(lp[0, 0], ("x", "y", "c"))

    out = {}
    kinds = ("grad", "delta", "new_m", "new_v")
    names1 = _REPL + _CONVW

    started, small = [], [None] * depth

    def send(n, i, g):
        handle, token = _exchange_start("scatter_%s_%d_start" % (n, i), [g.reshape(N_DEV, -1, D)], True)
        started.append((n, i, handle))
        return token

    tok = None
    for i in reversed(range(depth)):
        w, sv = W[i], saved[i]
        (df,) = _mm("mm_df", dx, w['w2'], "nt", [bf16], _ep_drelu2, (sv['f'],), dep=tok)
        (gw2,) = _mm("mm_gw2", sv['a'], dx, "tn", [bf16])
        tok = send('w_ff2', i, gw2)
        dxm, dg2 = _mm("mm_dh2", df, w['w1t'], "nn", [f32], _ep_rms_bwd, (sv['xm'], dx), dep=tok,
                       rows=(p['norm2_g'][i].reshape(1, D),), n_row_out=1)
        (gw1t,) = _mm("mm_gw1", df, sv['h2'], "tn", [bf16])
        tok = send('w_ff1', i, gw1t)
        (dycat,) = _mm("mm_dycat", dxm, w['wout'], "nt", [bf16], dep=tok)
        (gwout,) = _mm("mm_gwout", sv['ycat'], dxm, "tn", [bf16])
        tok = send('w_out', i, gwout)
        dproj, dwa, dba, dlag, dlab = _conv_bwd("confa_bwd", sv['proj'], 0, w['ca'], p['conv_a_b'][i] + tok[0, 0], dycat, 0, KA,
                                                True, nb, p['ln_a_g'][i], p['ln_a_b'][i], share=(main, 0, None))
        dproj, dlbg, dlbb, dws, dbs = _gmlp_bwd(sv['proj'], 1, p['ln_b_g'][i], p['ln_b_b'][i], p['w_spatial'][i],
                                                p['b_spatial'][i], dycat, 1, share=(main, 1, dproj))
        dxbc, dproj, ddt, ddtb, dalog, ddsk, dng = _ssd_bwd(sv['xbc'], sv['proj'], 2, sv['pdt'], sv['dtb'], sv['alog'],
                                                            sv['dsk'], p['norm_c_g'][i], sv['sin'], dycat, 1, nh, ngrp, nst, nb,
                                                            share=(main, 2, dproj))
        dproj, dwc, dbc = _conv_bwd("convc_bwd", sv['proj'], 2, w['cc'], p['conv_c_b'][i], dxbc, 0, KC, False, nb,
                                    share=(main, 2, dproj))
        (dh_main,) = _mm("mm_dh1", dproj, w['wt'], "nn", [f32])
        (gwt_main,) = _mm("mm_gwin", dproj, sv['h1'], "tn", [bf16])
        (gwt_dt,) = _mm("mm_gwdt", ddt, sv['h1'], "tn", [bf16])
        tok = send('w_in', i, jnp.concatenate([gwt_main, gwt_dt[:nh]], axis=0))
        dx, dg1 = _mm("mm_dh1dt", ddt, w['wt_dt'], "nn", [f32], _ep_add_rms_bwd, (dh_main, sv['x_in'], dxm), dep=tok,
                      rows=(p['norm1_g'][i].reshape(1, D),), n_row_out=1)

        gi = dict(norm1_g=dg1[0], norm2_g=dg2[0], conv_a_w=dwa[:KA], conv_a_b=dba[0], ln_a_g=dlag[0], ln_a_b=dlab[0],
                  ln_b_g=dlbg[0], ln_b_b=dlbb[0], w_spatial=dws, b_spatial=dbs, conv_c_w=dwc[:KC], conv_c_b=dbc[0],
                  dt_bias=ddtb[0, :nh], a_log=dalog[0, :nh], d_skip=ddsk[0, :nh], norm_c_g=dng[0])
        parts_i = [gi[n] for n in names1] + ([dfinal[0]] if i == depth - 1 else [])
        handle, tok = _exchange_start("gather_g%d_start" % i, [_pack(parts_i)], False)
        small[i] = ([a.shape for a in parts_i], handle)
    grad_x = dx.reshape(nb, S, D)

    dep = [dx, tok]
    for n, i, handle in started:
        (parts,) = _exchange_wait("scatter_%s_%d_wait" % (n, i), handle, dep)
        prev = [out[(kind, n)] for kind in kinds] if (kinds[0], n) in out else None
        if _BIG_T[n]:
            res = _adam("adam_%s_%d" % (n, i), parts, p[n][i].T, m[n][i].T, v[n][i].T, layer=i, depth=depth, into=prev)
        else:
            res = _adam("adam_%s_%d" % (n, i), parts, p[n], m[n], v[n], layer=i, depth=depth, into=prev, stacked_in=True)
        for kind, r in zip(kinds, res):
            out[(kind, n)] = r
        dep = res[3]

    gsum = [None] * depth
    for i in reversed(range(depth)):
        (parts,) = _exchange_wait("gather_g%d_wait" % i, small[i][1], dep)
        gsum[i] = _sum_parts("sum_small", parts)
        dep = gsum[i]
    widths = [-(-math.prod(p[n].shape[1:]) // 128) * 128 for n in _REPL]
    rep_rows = sum(widths) // 128
    tot_rows = -(-depth * rep_rows // 256) * 256

    def rep_slab(q):
        cols = [_pad_to(q[n].reshape(depth, -1), wd, 1) for n, wd in zip(_REPL, widths)]
        return _pad_to(jnp.concatenate(cols, axis=1).reshape(-1, 128), tot_rows, 0)

    g_rep = _pad_to(jnp.concatenate([g[:rep_rows] for g in gsum], axis=0), tot_rows, 0)
    res = _adam("adam_small", g_rep[None], rep_slab(p), rep_slab(m), rep_slab(v))
    for kind, r in zip(kinds, res):
        view = r[:depth * rep_rows].reshape(depth, -1)
        o = 0
        for n, wd in zip(_REPL, widths):
            out[(kind, n)] = view[:, o:o + math.prod(p[n].shape[1:])].reshape(p[n].shape)
            o += wd

    extra = []
    for i in range(depth):
        tail = _unpack(gsum[i][rep_rows:], small[i][0][len(_REPL):])
        extra.append(tail)
    gconv = []
    for j, n in enumerate(_CONVW):
        cw_shard = p[n].shape[2]
        full = jnp.stack([extra[i][j] for i in range(depth)])
        gconv.append(lax.dynamic_slice_in_dim(full, me * cw_shard, cw_shard, axis=2))
    tail_names = _CONVW + ['final_g']
    res = _adam("adam_conv", _pack(gconv + [extra[depth - 1][len(_CONVW)]])[None],
                *[_pack([q[n] for n in tail_names]) for q in (p, m, v)])
    for kind, r in zip(kinds, res):
        for n, arr in zip(tail_names, _unpack(r, [p[n].shape for n in tail_names])):
            out[(kind, n)] = arr
    for n in _BIG:
        if _BIG_T[n]:
            for kind in kinds:
                out[(kind, n)] = jnp.swapaxes(out[(kind, n)], 1, 2)

    flat = [loss, grad_x]
    for kind in ("grad", "delta", "new_m", "new_v"):
        flat += [out[(kind, n)] for n in _NAMES]
    return tuple(flat)


def kernel(x, norm1_g, w_in, conv_a_w, conv_a_b, ln_a_g, ln_a_b, ln_b_g, ln_b_b, w_spatial, b_spatial, conv_c_w, conv_c_b, dt_bias, a_log, d_skip, norm_c_g, w_out, norm2_g, w_ff1, w_ff2, final_g, loss_target, m_norm1_g, m_w_in, m_conv_a_w, m_conv_a_b, m_ln_a_g, m_ln_a_b, m_ln_b_g, m_ln_b_b, m_w_spatial, m_b_spatial, m_conv_c_w, m_conv_c_b, m_dt_bias, m_a_log, m_d_skip, m_norm_c_g, m_w_out, m_norm2_g, m_w_ff1, m_w_ff2, m_final_g, v_norm1_g, v_w_in, v_conv_a_w, v_conv_a_b, v_ln_a_g, v_ln_a_b, v_ln_b_g, v_ln_b_b, v_w_spatial, v_b_spatial, v_conv_c_w, v_conv_c_b, v_dt_bias, v_a_log, v_d_skip, v_norm_c_g, v_w_out, v_norm2_g, v_w_ff1, v_w_ff2, v_final_g):
    p = dict(zip(_NAMES, (norm1_g, w_in, conv_a_w, conv_a_b, ln_a_g, ln_a_b, ln_b_g, ln_b_b, w_spatial, b_spatial, conv_c_w,
                          conv_c_b, dt_bias, a_log, d_skip, norm_c_g, w_out, norm2_g, w_ff1, w_ff2, final_g)))
    m = dict(zip(_NAMES, (m_norm1_g, m_w_in, m_conv_a_w, m_conv_a_b, m_ln_a_g, m_ln_a_b, m_ln_b_g, m_ln_b_b, m_w_spatial,
                          m_b_spatial, m_conv_c_w, m_conv_c_b, m_dt_bias, m_a_log, m_d_skip, m_norm_c_g, m_w_out, m_norm2_g,
                          m_w_ff1, m_w_ff2, m_final_g)))
    v = dict(zip(_NAMES, (v_norm1_g, v_w_in, v_conv_a_w, v_conv_a_b, v_ln_a_g, v_ln_a_b, v_ln_b_g, v_ln_b_b, v_w_spatial,
                          v_b_spatial, v_conv_c_w, v_conv_c_b, v_dt_bias, v_a_log, v_d_skip, v_norm_c_g, v_w_out, v_norm2_g,
                          v_w_ff1, v_w_ff2, v_final_g)))
    return _step(p, m, v, x, loss_target)
```

```python
import functools
import math

import jax
import jax.numpy as jnp
from jax import lax
from jax.experimental import pallas as pl
from jax.experimental.pallas import tpu as pltpu

f32 = jnp.float32
bf16 = jnp.bfloat16
HI = lax.Precision.HIGHEST
EPS = 1e-5
HEAD = 64
CHUNK = 128
KA = 31
KC = 4
N_DEV = 8
VMEM_LIMIT = 56 * 1024 * 1024
MM_VMEM_BUDGET = 52 * 1024 * 1024

ADAM_LR = 0.001
ADAM_B1 = 0.9
ADAM_B2 = 0.999
ADAM_EPS = 1e-08
ADAM_WD = 0.01
ADAM_STEP = 10


def _cparams(sem=None):
    return pltpu.CompilerParams(dimension_semantics=sem, vmem_limit_bytes=VMEM_LIMIT)


def _sds(shape, dtype):
    return jax.ShapeDtypeStruct(shape, dtype)


_DIMS = {"nn": ((1,), (0,)), "nt": ((1,), (1,)), "tn": ((0,), (0,))}


def _dot16(a, b, form):
    return lax.dot_general(a.astype(bf16), b.astype(bf16), (_DIMS[form], ((), ())), preferred_element_type=f32)


@functools.partial(jax.custom_vjp, nondiff_argnums=(2,))
def _bdot(a, b, form):
    return _dot16(a, b, form)


def _bdot_fwd(a, b, form):
    return _dot16(a, b, form), (a, b)


def _bdot_bwd(form, res, ct):
    a, b = res
    if form == "nn":
        da, db = _dot16(ct, b, "nt"), _dot16(a, ct, "tn")
    elif form == "nt":
        da, db = _dot16(ct, b, "nn"), _dot16(ct, a, "tn")
    else:
        da, db = _dot16(b, ct, "nt"), _dot16(a, ct, "nn")
    return da.astype(a.dtype), db.astype(b.dtype)


_bdot.defvjp(_bdot_fwd, _bdot_bwd)


def _tile(n, cap):
    if n <= cap:
        return n
    for d in range(cap - cap % 128, 0, -128):
        if n % d == 0:
            return d
    raise ValueError((n, cap))


def _mm(name, a, b, form, out_dtypes, epilogue=None, extras=(), tm=2048, tn=1024, tk=8192, dep=None, rows=(), n_row_out=0,
        n=None, tok_out=False):
    if form == "tn":
        K, M = a.shape
    else:
        M, K = a.shape
    N = n if n is not None else (b.shape[0] if form == "nt" else b.shape[1])
    tm, tn, tk = _tile(M, tm), _tile(N, tn), _tile(K, tk)
    nk = K // tk

    def vmem_bytes(tm):
        mn = sum(jnp.dtype(e.dtype).itemsize for e in extras) + sum(jnp.dtype(d).itemsize for d in out_dtypes)
        return 2 * (tm * tk * a.dtype.itemsize + tk * tn * b.dtype.itemsize + tm * tn * mn) + 2 * tm * tn * 4

    while vmem_bytes(tm) > MM_VMEM_BUDGET and tm % 256 == 0:
        tm //= 2
    ne, no, nr = len(extras), len(out_dtypes), len(rows)
    assert n_row_out == 0 or tn == N
    deps = () if dep is None else (dep,)
    if epilogue is None:
        epilogue = lambda acc: (acc,)

    def body(a_ref, b_ref, *rest):
        extra_refs, row_refs = rest[:ne], rest[ne:ne + nr]
        rest = rest[ne + nr + len(deps):]
        out_refs, rowout_refs = rest[:no], rest[no:no + n_row_out]
        ntok = 1 if tok_out else 0
        part = lax.dot_general(a_ref[...].astype(bf16), b_ref[...].astype(bf16),
                               (_DIMS[form], ((), ())), preferred_element_type=f32)

        def finish(acc):
            if tok_out:
                rest[no + n_row_out][...] = jnp.zeros((8, 128), f32)
            outs = epilogue(acc, *[e[...] for e in extra_refs], *[r[...] for r in row_refs])
            for o_ref, v in zip(out_refs, outs[:no]):
                o_ref[...] = v.astype(o_ref.dtype)
            for r_ref, v in zip(rowout_refs, outs[no:]):
                @pl.when(pl.program_id(0) == 0)
                def _():
                    r_ref[...] = jnp.zeros_like(r_ref)

                r_ref[0:1, :] += v

        if nk == 1:
            finish(part)
            return
        acc_ref = rest[no + n_row_out + ntok]
        k = pl.program_id(2)

        @pl.when(k == 0)
        def _():
            acc_ref[...] = part

        @pl.when((k > 0) & (k < nk - 1))
        def _():
            acc_ref[...] += part

        @pl.when(k == nk - 1)
        def _():
            finish(acc_ref[...] + part)

    a_spec = pl.BlockSpec((tk, tm), lambda i, j, k: (k, i)) if form == "tn" else pl.BlockSpec((tm, tk), lambda i, j, k: (i, k))
    b_spec = pl.BlockSpec((tn, tk), lambda i, j, k: (j, k)) if form == "nt" else pl.BlockSpec((tk, tn), lambda i, j, k: (k, j))
    mn_spec = pl.BlockSpec((tm, tn), lambda i, j, k: (i, j))
    return pl.pallas_call(
        body, name=name, grid=(M // tm, N // tn, nk),
        in_specs=[a_spec, b_spec] + [mn_spec] * ne + [pl.BlockSpec((1, tn), lambda i, j, k: (0, j))] * nr
        + [pl.BlockSpec((8, 128), lambda i, j, k: (0, 0))] * len(deps),
        out_specs=[mn_spec] * no + [pl.BlockSpec((8, tn), lambda i, j, k: (0, j))] * n_row_out
        + [pl.BlockSpec((8, 128), lambda i, j, k: (0, 0))] * (1 if tok_out else 0),
        out_shape=[_sds((M, N), d) for d in out_dtypes] + [_sds((8, N), f32)] * n_row_out
        + [_sds((8, 128), f32)] * (1 if tok_out else 0),
        scratch_shapes=[pltpu.VMEM((tm, tn), f32)] if nk > 1 else [],
        compiler_params=_cparams(("arbitrary", "arbitrary", "arbitrary")),
    )(a, b, *extras, *rows, *deps)


def _ep_add(acc, r):
    return (acc + r,)


def _ep_add_rms(acc, r, g):
    x = acc + r
    return x, _rms(x, g)


def _ep_rms_bwd(acc, x, dres, g):
    _, vjp = jax.vjp(_rms, x, g)
    dx, dg = vjp(acc)
    return dres + dx, dg


def _ep_add_rms_bwd(acc, more, x, dres, g):
    return _ep_rms_bwd(acc + more, x, dres, g)


def _ep_relu2(acc):
    r = jnp.maximum(acc, 0.0)
    return acc, r * r


def _ep_drelu2(acc, f):
    return (acc * 2.0 * jnp.maximum(f, 0.0),)


def _rms(x, g):
    return x * lax.rsqrt(jnp.mean(x * x, axis=-1, keepdims=True) + EPS) * g


TT = 512


def _rms_fwd(x, g):
    T, D = x.shape

    def body(x_ref, g_ref, h_ref, tok_ref):
        h_ref[...] = _rms(x_ref[...], g_ref[...]).astype(bf16)
        tok_ref[...] = jnp.zeros_like(tok_ref)

    return pl.pallas_call(
        body, name="rms_fwd", grid=(T // TT,),
        in_specs=[pl.BlockSpec((TT, D), lambda i: (i, 0)), pl.BlockSpec((1, D), lambda i: (0, 0))],
        out_specs=[pl.BlockSpec((TT, D), lambda i: (i, 0)), pl.BlockSpec((8, 128), lambda i: (0, 0))],
        out_shape=[_sds((T, D), bf16), _sds((8, 128), f32)], compiler_params=_cparams(("arbitrary",)),
    )(x, g.reshape(1, D))


def _loss_head(x, g, tgt):
    T, D = x.shape

    def f(xv, gv, tv):
        e = _rms(xv, gv) - tv
        return 0.5 * jnp.sum(jnp.sum(e * e, axis=-1, keepdims=True) * (1.0 / D), axis=0, keepdims=True)

    def body(x_ref, g_ref, t_ref, loss_ref, dx_ref, dg_ref):
        tv = t_ref[...]
        l, vjp = jax.vjp(lambda xv, gv: f(xv, gv, tv), x_ref[...], g_ref[...])
        dx, dg = vjp(jnp.ones((1, 1), f32))
        dx_ref[...] = dx

        @pl.when(pl.program_id(0) == 0)
        def _():
            dg_ref[...] = jnp.zeros_like(dg_ref)
            loss_ref[...] = jnp.zeros_like(loss_ref)

        dg_ref[0:1, :] += dg
        loss_ref[...] += jnp.broadcast_to(l, loss_ref.shape)

    tile = pl.BlockSpec((TT, D), lambda i: (i, 0))
    return pl.pallas_call(
        body, name="loss_head", grid=(T // TT,),
        in_specs=[tile, pl.BlockSpec((1, D), lambda i: (0, 0)), tile],
        out_specs=[pl.BlockSpec((8, 128), lambda i: (0, 0)), tile, pl.BlockSpec((8, D), lambda i: (0, 0))],
        out_shape=[_sds((8, 128), f32), _sds((T, D), f32), _sds((8, D), f32)],
        compiler_params=_cparams(("arbitrary",)),
    )(x, g.reshape(1, D), tgt)


TB = 256


def _glu(a_val, a_gate):
    return a_val * jax.nn.sigmoid(a_gate)


PAIR = 2 * HEAD


def _pair_mean(x, lo):
    s_lo = jnp.sum(jnp.where(lo, x, 0.0), axis=-1, keepdims=True)
    s_hi = jnp.sum(jnp.where(lo, 0.0, x), axis=-1, keepdims=True)
    return jnp.where(lo, s_lo, s_hi) * (1.0 / HEAD)


def _pair_ln(v, g, b):
    lo = lax.broadcasted_iota(jnp.int32, v.shape, 1) < HEAD
    vc = v - _pair_mean(v, lo)
    var = _pair_mean(vc * vc, lo)
    return vc * lax.rsqrt(var + EPS) * g + b


def _ln_silu(v, g, b):
    return jax.nn.silu(_pair_ln(v, g, b))


def _conv_geom(kw):
    halo = 32 if kw > 9 else 16
    return halo, halo - (kw - 1)


def _residues(shifts):
    shifts = list(shifts)
    return sorted({s % 8 for s in shifts} - {0}) if len(shifts) > 8 else []


def _shift_copies(src_ref, cp_ref, res, rows, ls):
    for j, r in enumerate(res):
        cp_ref[j, :, ls] = src_ref[pl.ds(r, rows), ls]


def _shifted(src_ref, cp_ref, res, shift, size, ls):
    r = shift % 8
    if r not in res:
        return src_ref[pl.ds(shift, size), ls]
    return cp_ref[res.index(r), pl.ds(shift - r, size), ls]


def _conv_taps(hp_ref, hs_ref, w_ref, b_ref, acc_ref, kw, off, halo, width):
    res = _residues(range(off, off + kw))
    for c in range(width // 128):
        ls = pl.ds(c * 128, 128)
        _shift_copies(hp_ref, hs_ref, res, halo + TB, ls)
        acc = jnp.broadcast_to(b_ref[:, ls], (TB, 128))
        for k in range(kw):
            acc = acc + w_ref[k:k + 1, ls] * _shifted(hp_ref, hs_ref, res, off + k, TB, ls)
        acc_ref[:, ls] = acc


def _conv_fwd(name, src, col_block, w, b, kw, conformer, n_seq, ln_g=None, ln_b=None, share=None):
    T = src.shape[0]
    cout = w.shape[1]
    cin = 2 * cout if conformer else cout
    halo, off = _conv_geom(kw)
    nblk = T // n_seq // TB
    hb = TB // halo
    out_shape, out_blk, shared_in, aliases = _shared(share, T, cout, bf16 if conformer else f32, 6 if conformer else 4)

    def body(cur_ref, halo_ref, w_ref, b_ref, *rest):
        if conformer:
            g_ref, lb_ref = rest[:2]
            rest = rest[2:]
        out_ref, hp_ref, acc_ref, hs_ref = rest[len(shared_in):]
        i = pl.program_id(1)
        first = (i == 0)

        @pl.when((pl.program_id(0) == 0) & first)
        def _():
            hp_ref[pl.ds(halo + TB, 8), :] = jnp.zeros((8, cout), f32)

        if conformer:
            hp_ref[pl.ds(halo, TB), :] = _glu(cur_ref[:, 0:cout].astype(f32), cur_ref[:, cout:cin].astype(f32))
            hh = _glu(halo_ref[:, 0:cout].astype(f32), halo_ref[:, cout:cin].astype(f32))
        else:
            hp_ref[pl.ds(halo, TB), :] = cur_ref[...].astype(f32)
            hh = halo_ref[...].astype(f32)
        hp_ref[pl.ds(0, halo), :] = jnp.where(first, 0.0, hh)
        _conv_taps(hp_ref, hs_ref, w_ref, b_ref, acc_ref, kw, off, halo, cout)
        if conformer:
            for q in range(cout // PAIR):
                ls = pl.ds(q * PAIR, PAIR)
                out_ref[:, ls] = _ln_silu(acc_ref[:, ls], g_ref[:, ls], lb_ref[:, ls]).astype(out_ref.dtype)
        else:
            out_ref[...] = jax.nn.silu(acc_ref[...]).astype(out_ref.dtype)

    nres = max(1, len(_residues(range(off, off + kw))))

    row = pl.BlockSpec((1, cout), lambda s, i: (0, 0))
    in_specs = [pl.BlockSpec((TB, cin), lambda s, i: (s * nblk + i, col_block)),
                pl.BlockSpec((halo, cin), lambda s, i: (jnp.maximum((s * nblk + i) * hb - 1, 0), col_block)),
                pl.BlockSpec((w.shape[0], cout), lambda s, i: (0, 0)), row]
    args = [src, src, w, b.reshape(1, cout)]
    if conformer:
        in_specs += [row, row]
        args += [ln_g.reshape(1, cout), ln_b.reshape(1, cout)]
    in_specs += [pl.BlockSpec(memory_space=pl.ANY)] * len(shared_in)
    args += shared_in
    return pl.pallas_call(
        body, name=name, grid=(n_seq, nblk), in_specs=in_specs,
        out_specs=pl.BlockSpec((TB, cout), lambda s, i: (s * nblk + i, out_blk)),
        out_shape=out_shape, input_output_aliases=aliases,
        scratch_shapes=[pltpu.VMEM((halo + TB + 8, cout), f32), pltpu.VMEM((TB, cout), f32),
                        pltpu.VMEM((nres, halo + TB, cout), f32)],
        compiler_params=_cparams(("arbitrary", "arbitrary")),
    )(*args)


def _shared(share, T, width, dtype, n_inputs, out_index=0):
    if share is None:
        return _sds((T, width), dtype), 0, [], {}
    total, blk, into = share
    if into is None:
        return _sds((T, total), dtype), blk, [], {}
    return _sds((T, total), dtype), blk, [into], {n_inputs: out_index}


def _conv_bwd(name, src, col_block, w, b, dy, dy_col_block, kw, conformer, n_seq, ln_g=None, ln_b=None, share=None):
    T = src.shape[0]
    cout = w.shape[1]
    wrows = w.shape[0]
    cin = 2 * cout if conformer else cout
    halo, off = _conv_geom(kw)
    nblk = T // n_seq // TB
    hb = TB // halo
    dsrc_shape, dsrc_blk, shared_in, aliases = _shared(share, T, cin, bf16, 7 if conformer else 5)

    def body(cur_ref, halo_ref, w_ref, b_ref, dy_ref, *rest):
        if conformer:
            g_ref, lb_ref = rest[:2]
            rest = rest[2:]
        rest = rest[len(shared_in):]
        if conformer:
            dsrc_ref, dw_ref, db_ref, dg_ref, dlb_ref, hp_ref, acc_ref, dz_ref, dhp_ref, carry_ref, hs_ref, dzs_ref = rest
        else:
            dsrc_ref, dw_ref, db_ref, hp_ref, acc_ref, dz_ref, dhp_ref, carry_ref, hs_ref, dzs_ref = rest
        s, ii = pl.program_id(0), pl.program_id(1)
        i = nblk - 1 - ii
        first = (i == 0)

        @pl.when((s == 0) & (ii == 0))
        def _():
            dw_ref[...] = jnp.zeros_like(dw_ref)
            db_ref[...] = jnp.zeros_like(db_ref)
            hp_ref[pl.ds(halo + TB, 8), :] = jnp.zeros((8, cout), f32)
            if conformer:
                dg_ref[...] = jnp.zeros_like(dg_ref)
                dlb_ref[...] = jnp.zeros_like(dlb_ref)

        @pl.when(ii == 0)
        def _():
            carry_ref[...] = jnp.zeros_like(carry_ref)
            dz_ref[pl.ds(0, halo), :] = jnp.zeros((halo, cout), f32)
            dz_ref[pl.ds(halo + TB, halo), :] = jnp.zeros((halo, cout), f32)

        if conformer:
            hp_ref[pl.ds(halo, TB), :] = _glu(cur_ref[:, 0:cout].astype(f32), cur_ref[:, cout:cin].astype(f32))
            hh = _glu(halo_ref[:, 0:cout].astype(f32), halo_ref[:, cout:cin].astype(f32))
        else:
            hp_ref[pl.ds(halo, TB), :] = cur_ref[...].astype(f32)
            hh = halo_ref[...].astype(f32)
        hp_ref[pl.ds(0, halo), :] = jnp.where(first, 0.0, hh)
        _conv_taps(hp_ref, hs_ref, w_ref, b_ref, acc_ref, kw, off, halo, cout)

        if conformer:
            for q in range(cout // PAIR):
                ls = pl.ds(q * PAIR, PAIR)
                _, vjp = jax.vjp(_ln_silu, acc_ref[:, ls], g_ref[:, ls], lb_ref[:, ls])
                da, dg, dlb = vjp(dy_ref[:, ls].astype(f32))
                dz_ref[pl.ds(halo, TB), ls] = da
                dg_ref[0:1, ls] += dg
                dlb_ref[0:1, ls] += dlb
        else:
            _, vjp = jax.vjp(jax.nn.silu, acc_ref[...])
            dz_ref[pl.ds(halo, TB), :] = vjp(dy_ref[...].astype(f32))[0]

        res_h = _residues(range(off, off + kw))
        res_z = _residues(range(kw))
        for c in range(cout // 128):
            ls = pl.ds(c * 128, 128)
            _shift_copies(dz_ref, dzs_ref, res_z, halo + TB + halo - 8, ls)
            dacc = dz_ref[pl.ds(halo, TB), ls]
            db_ref[0:1, ls] += jnp.sum(dacc, axis=0, keepdims=True)
            dhp = jnp.zeros((halo + TB, 128), f32)
            for k in range(kw):
                dw_ref[k:k + 1, ls] += jnp.sum(dacc * _shifted(hp_ref, hs_ref, res_h, off + k, TB, ls), axis=0, keepdims=True)
                dhp = dhp + w_ref[k:k + 1, ls] * _shifted(dz_ref, dzs_ref, res_z, kw - 1 - k, halo + TB, ls)
            dhp_ref[:, ls] = dhp
        dhp_ref[pl.ds(TB, halo), :] += carry_ref[...]
        carry_ref[...] = dhp_ref[pl.ds(0, halo), :]
        dcur = dhp_ref[pl.ds(halo, TB), :]
        if conformer:
            _, vjp = jax.vjp(_glu, cur_ref[:, 0:cout].astype(f32), cur_ref[:, cout:cin].astype(f32))
            dval, dgate = vjp(dcur)
            dsrc_ref[:, 0:cout] = dval.astype(dsrc_ref.dtype)
            dsrc_ref[:, cout:cin] = dgate.astype(dsrc_ref.dtype)
        else:
            dsrc_ref[...] = dcur.astype(dsrc_ref.dtype)

    def blk(s, ii):
        return s * nblk + (nblk - 1 - ii)

    row = pl.BlockSpec((1, cout), lambda s, ii: (0, 0))
    acc8 = pl.BlockSpec((8, cout), lambda s, ii: (0, 0))
    in_specs = [pl.BlockSpec((TB, cin), lambda s, ii: (blk(s, ii), col_block)),
                pl.BlockSpec((halo, cin), lambda s, ii: (jnp.maximum(blk(s, ii) * hb - 1, 0), col_block)),
                pl.BlockSpec((wrows, cout), lambda s, ii: (0, 0)), row,
                pl.BlockSpec((TB, cout), lambda s, ii: (blk(s, ii), dy_col_block))]
    args = [src, src, w, b.reshape(1, cout), dy]
    out_specs = [pl.BlockSpec((TB, cin), lambda s, ii: (blk(s, ii), dsrc_blk)),
                 pl.BlockSpec((wrows, cout), lambda s, ii: (0, 0)), acc8]
    out_shape = [dsrc_shape, _sds((wrows, cout), f32), _sds((8, cout), f32)]
    if conformer:
        in_specs += [row, row]
        args += [ln_g.reshape(1, cout), ln_b.reshape(1, cout)]
        out_specs += [acc8, acc8]
        out_shape += [_sds((8, cout), f32), _sds((8, cout), f32)]
    in_specs += [pl.BlockSpec(memory_space=pl.ANY)] * len(shared_in)
    args += shared_in
    return pl.pallas_call(
        body, name=name, grid=(n_seq, nblk), in_specs=in_specs, out_specs=out_specs, out_shape=out_shape,
        input_output_aliases=aliases,
        scratch_shapes=[pltpu.VMEM((halo + TB + 8, cout), f32), pltpu.VMEM((TB, cout), f32),
                        pltpu.VMEM((halo + TB + halo, cout), f32), pltpu.VMEM((halo + TB, cout), f32),
                        pltpu.VMEM((halo, cout), f32),
                        pltpu.VMEM((max(1, len(_residues(range(off, off + kw)))), halo + TB, cout), f32),
                        pltpu.VMEM((max(1, len(_residues(range(kw)))), halo + TB + halo - 8, cout), f32)],
        compiler_params=_cparams(("arbitrary", "arbitrary")),
    )(*args)


def _gelu(x):
    return 0.5 * x * (1.0 + lax.erf(x * (1.0 / math.sqrt(2.0))))


def _tril_mask(n):
    r = lax.broadcasted_iota(jnp.int32, (n, n), 0)
    c = lax.broadcasted_iota(jnp.int32, (n, n), 1)
    return r >= c


def _head_spread(nh):
    r = lax.broadcasted_iota(jnp.int32, (nh, nh * HEAD), 0)
    c = lax.broadcasted_iota(jnp.int32, (nh, nh * HEAD), 1)
    return (c // HEAD == r).astype(f32)


def _gmlp_bias(bs):
    return lax.dot_general(bs, _head_spread(bs.shape[0]), (((0,), (0,)), ((), ())), precision=HI, preferred_element_type=f32)


def _gmlp_pair(bu, bv, g, b, w_a, w_b, bias):
    lo = lax.broadcasted_iota(jnp.int32, bu.shape, 1) < HEAD
    tril = _tril_mask(CHUNK)
    u = _gelu(bu)
    vn = _pair_ln(_gelu(bv), g, b)
    mix = jnp.where(lo, _bdot(jnp.where(tril, w_a, 0.0), vn, "nn"), _bdot(jnp.where(tril, w_b, 0.0), vn, "nn"))
    return u * (mix + bias)


def _gmlp_fwd(proj, col_block, ln_g, ln_b, w_s, b_s, share=None):
    T = proj.shape[0]
    nh = w_s.shape[0]
    width = nh * HEAD
    out_shape, out_blk, shared_in, aliases = _shared(share, T, width, bf16, 5)

    def body(p_ref, g_ref, b_ref, w_ref, bs_ref, *rest):
        out_ref, bias_ref = rest[len(shared_in):]

        @pl.when(pl.program_id(0) == 0)
        def _():
            bias_ref[...] = _gmlp_bias(bs_ref[...])

        for q in range(nh // 2):
            ls = pl.ds(q * PAIR, PAIR)
            lv = pl.ds(width + q * PAIR, PAIR)
            out_ref[:, ls] = _gmlp_pair(p_ref[:, ls].astype(f32), p_ref[:, lv].astype(f32), g_ref[:, ls], b_ref[:, ls], w_ref[2 * q], w_ref[2 * q + 1],
                                        bias_ref[:, ls]).astype(out_ref.dtype)

    row = pl.BlockSpec((1, width), lambda i: (0, 0))
    return pl.pallas_call(
        body, name="gmlp_fwd", grid=(T // CHUNK,),
        in_specs=[pl.BlockSpec((CHUNK, 2 * width), lambda i: (i, col_block)), row, row,
                  pl.BlockSpec((nh, CHUNK, CHUNK), lambda i: (0, 0, 0)), pl.BlockSpec((nh, CHUNK), lambda i: (0, 0))]
        + [pl.BlockSpec(memory_space=pl.ANY)] * len(shared_in),
        out_specs=pl.BlockSpec((CHUNK, width), lambda i: (i, out_blk)),
        out_shape=out_shape, input_output_aliases=aliases, scratch_shapes=[pltpu.VMEM((CHUNK, width), f32)],
        compiler_params=_cparams(("arbitrary",)),
    )(proj, ln_g.reshape(1, width), ln_b.reshape(1, width), w_s, b_s, *shared_in)


def _gmlp_bwd(proj, col_block, ln_g, ln_b, w_s, b_s, dy, dy_col_block, share=None):
    T = proj.shape[0]
    nh = w_s.shape[0]
    width = nh * HEAD
    nstep = T // CHUNK
    dp_shape, dp_blk, shared_in, aliases = _shared(share, T, 2 * width, bf16, 6)

    def body(p_ref, g_ref, b_ref, w_ref, bs_ref, dy_ref, *rest):
        dp_ref, dg_ref, db_ref, dw_ref, dbst_ref, bias_ref, dbias_ref = rest[len(shared_in):]

        @pl.when(pl.program_id(0) == 0)
        def _():
            dg_ref[...] = jnp.zeros_like(dg_ref)
            db_ref[...] = jnp.zeros_like(db_ref)
            dw_ref[...] = jnp.zeros_like(dw_ref)
            dbias_ref[...] = jnp.zeros_like(dbias_ref)
            bias_ref[...] = _gmlp_bias(bs_ref[...])

        for q in range(nh // 2):
            ls = pl.ds(q * PAIR, PAIR)
            lv = pl.ds(width + q * PAIR, PAIR)
            _, vjp = jax.vjp(_gmlp_pair, p_ref[:, ls].astype(f32), p_ref[:, lv].astype(f32), g_ref[:, ls], b_ref[:, ls], w_ref[2 * q], w_ref[2 * q + 1],
                             bias_ref[:, ls])
            dbu, dbv, dg, db, dw_a, dw_b, dbias = vjp(dy_ref[:, ls].astype(f32))
            dp_ref[:, ls] = dbu.astype(dp_ref.dtype)
            dp_ref[:, lv] = dbv.astype(dp_ref.dtype)
            dg_ref[0:1, ls] += dg
            db_ref[0:1, ls] += db
            dw_ref[2 * q] += dw_a
            dw_ref[2 * q + 1] += dw_b
            dbias_ref[:, ls] += dbias

        @pl.when(pl.program_id(0) == nstep - 1)
        def _():
            dbst_ref[...] = lax.dot_general(dbias_ref[...], _head_spread(nh), (((1,), (1,)), ((), ())),
                                            precision=HI, preferred_element_type=f32)

    row = pl.BlockSpec((1, width), lambda i: (0, 0))
    acc8 = pl.BlockSpec((8, width), lambda i: (0, 0))
    wspec = pl.BlockSpec((nh, CHUNK, CHUNK), lambda i: (0, 0, 0))
    res = pl.pallas_call(
        body, name="gmlp_bwd", grid=(nstep,),
        in_specs=[pl.BlockSpec((CHUNK, 2 * width), lambda i: (i, col_block)), row, row, wspec,
                  pl.BlockSpec((nh, CHUNK), lambda i: (0, 0)), pl.BlockSpec((CHUNK, width), lambda i: (i, dy_col_block))]
        + [pl.BlockSpec(memory_space=pl.ANY)] * len(shared_in),
        out_specs=[pl.BlockSpec((CHUNK, 2 * width), lambda i: (i, dp_blk)), acc8, acc8, wspec,
                   pl.BlockSpec((CHUNK, nh), lambda i: (0, 0))],
        out_shape=[dp_shape, _sds((8, width), f32), _sds((8, width), f32),
                   _sds((nh, CHUNK, CHUNK), f32), _sds((CHUNK, nh), f32)],
        input_output_aliases=aliases,
        scratch_shapes=[pltpu.VMEM((CHUNK, width), f32), pltpu.VMEM((CHUNK, width), f32)],
        compiler_params=_cparams(("arbitrary",)),
    )(proj, ln_g.reshape(1, width), ln_b.reshape(1, width), w_s, b_s, dy, *shared_in)
    return res[0], res[1], res[2], res[3], res[4].T


def _sel_col(x, h):
    lane = lax.broadcasted_iota(jnp.int32, x.shape, 1)
    return jnp.sum(jnp.where(lane == h, x, 0.0), axis=1, keepdims=True)


def _sel_row(x, h):
    sub = lax.broadcasted_iota(jnp.int32, x.shape, 0)
    return jnp.sum(jnp.where(sub == h, x, 0.0), axis=0, keepdims=True)


def _ssd_chunk(nh, ngrp, xs_l, z_l, b_l, c_l, dtraw, dtb, alog, dskip, ng_l, prev_l):
    hg = nh // ngrp
    tril = _tril_mask(CHUNK)
    tl = tril.astype(f32)
    lo = lax.broadcasted_iota(jnp.int32, (CHUNK, PAIR), 1) < HEAD
    lo_row = lo[0:1, :]
    dt = jax.nn.softplus(dtraw + dtb)
    a = dt * (-jnp.exp(alog))
    cs = jnp.dot(tl, a, precision=HI, preferred_element_type=f32)
    cst = lax.dot_general(a, tl, (((0,), (1,)), ((), ())), precision=HI, preferred_element_type=f32)
    cb_l = [_bdot(c_l[g], b_l[g], "nt") for g in range(ngrp)]
    yz_l, new_prev = [], []
    for q in range(nh // 2):
        g = (2 * q) // hg
        cols = []
        for h in (2 * q, 2 * q + 1):
            cs_h = _sel_col(cs, h)
            tot = _sel_row(cs_h, CHUNK - 1)
            seg = jnp.where(tril, cs_h - _sel_row(cst, h), 0.0)
            lmat = jnp.where(tril, jnp.exp(seg), 0.0)
            cols.append((_sel_col(dt, h), cs_h, tot, lmat, _sel_col(dskip, h)))
        (dt_a, cs_a, tot_a, l_a, dsk_a), (dt_b, cs_b, tot_b, l_b, dsk_b) = cols
        xs = xs_l[q]
        x = xs * jnp.where(lo, dt_a, dt_b)
        ydiag = jnp.where(lo, _bdot(cb_l[g] * l_a, x, "nn"), _bdot(cb_l[g] * l_b, x, "nn"))
        yoff = _bdot(c_l[g], prev_l[q], "nn") * jnp.where(lo, jnp.exp(cs_a), jnp.exp(cs_b))
        xdec = x * jnp.where(lo, jnp.exp(tot_a - cs_a), jnp.exp(tot_b - cs_b))
        st = _bdot(b_l[g], xdec, "tn")
        new_prev.append(prev_l[q] * jnp.where(lo_row, jnp.exp(tot_a), jnp.exp(tot_b)) + st)
        y = ydiag + yoff + jnp.where(lo_row, dsk_a, dsk_b) * xs
        yz_l.append(y * jax.nn.silu(z_l[q]))
    out = [None] * (nh // 2)
    qg = hg // 2
    for g in range(ngrp):
        ssq = sum(jnp.sum(yz_l[q] * yz_l[q], axis=-1, keepdims=True) for q in range(g * qg, (g + 1) * qg))
        r = lax.rsqrt(ssq * (1.0 / (hg * HEAD)) + EPS)
        for q in range(g * qg, (g + 1) * qg):
            out[q] = yz_l[q] * r * ng_l[q]
    return out, new_prev


def _ssd_read(nh, ngrp, nst, xbc_ref, z_ref, ng_ref, st_ref):
    cw = nh * HEAD
    xs_l = [xbc_ref[:, pl.ds(q * PAIR, PAIR)] for q in range(nh // 2)]
    b_l = [xbc_ref[:, pl.ds(cw + g * nst, nst)] for g in range(ngrp)]
    c_l = [xbc_ref[:, pl.ds(cw + ngrp * nst + g * nst, nst)] for g in range(ngrp)]
    z_l = [z_ref[:, pl.ds(q * PAIR, PAIR)].astype(f32) for q in range(nh // 2)]
    ng_l = [ng_ref[:, pl.ds(q * PAIR, PAIR)] for q in range(nh // 2)]
    prev_l = [st_ref[:, pl.ds(q * PAIR, PAIR)] for q in range(nh // 2)]
    return xs_l, z_l, b_l, c_l, ng_l, prev_l


def _ssd_fwd(xbc, proj, z_col_block, pdt, dtb, alog, dskip, ng, nh, ngrp, nst, n_seq, share=None):
    T = xbc.shape[0]
    cw = nh * HEAD
    nchunk = T // n_seq // CHUNK
    assert nst == CHUNK
    y_shape, y_blk, shared_in, aliases = _shared(share, T, cw, bf16, 7)

    def body(xbc_ref, z_ref, dt_ref, dtb_ref, alog_ref, dskip_ref, ng_ref, *rest):
        y_ref, sin_ref, st_ref = rest[len(shared_in):]

        @pl.when(pl.program_id(1) == 0)
        def _():
            st_ref[...] = jnp.zeros_like(st_ref)

        sin_ref[...] = st_ref[...]
        xs_l, z_l, b_l, c_l, ng_l, prev_l = _ssd_read(nh, ngrp, nst, xbc_ref, z_ref, ng_ref, st_ref)
        y_l, new_prev = _ssd_chunk(nh, ngrp, xs_l, z_l, b_l, c_l, dt_ref[...], dtb_ref[...], alog_ref[...],
                                   dskip_ref[...], ng_l, prev_l)
        for q in range(nh // 2):
            ls = pl.ds(q * PAIR, PAIR)
            y_ref[:, ls] = y_l[q].astype(y_ref.dtype)
            st_ref[:, ls] = new_prev[q]

    def blk(s, c):
        return s * nchunk + c

    prow = pl.BlockSpec((1, 128), lambda s, c: (0, 0))
    return pl.pallas_call(
        body, name="ssd_fwd", grid=(n_seq, nchunk),
        in_specs=[pl.BlockSpec((CHUNK, xbc.shape[1]), lambda s, c: (blk(s, c), 0)),
                  pl.BlockSpec((CHUNK, cw), lambda s, c: (blk(s, c), z_col_block)),
                  pl.BlockSpec((CHUNK, 128), lambda s, c: (blk(s, c), 0)),
                  prow, prow, prow, pl.BlockSpec((1, cw), lambda s, c: (0, 0))]
        + [pl.BlockSpec(memory_space=pl.ANY)] * len(shared_in),
        out_specs=[pl.BlockSpec((CHUNK, cw), lambda s, c: (blk(s, c), y_blk)),
                   pl.BlockSpec((nst, cw), lambda s, c: (blk(s, c), 0))],
        out_shape=[y_shape, _sds((T, cw), f32)], input_output_aliases=aliases,
        scratch_shapes=[pltpu.VMEM((nst, cw), f32)],
        compiler_params=_cparams(("arbitrary", "arbitrary")),
    )(xbc, proj, pdt, dtb, alog, dskip, ng.reshape(1, cw), *shared_in)


def _ssd_bwd(xbc, proj, z_col_block, pdt, dtb, alog, dskip, ng, sin, dy, dy_col_block, nh, ngrp, nst, n_seq, share=None):
    T, xw = xbc.shape
    cw = nh * HEAD
    nchunk = T // n_seq // CHUNK
    dz_shape, dz_blk, shared_in, aliases = _shared(share, T, cw, bf16, 9, out_index=1)

    def body(xbc_ref, z_ref, dt_ref, dtb_ref, alog_ref, dskip_ref, ng_ref, sin_ref, dy_ref, *rest):
        dxbc_ref, dz_ref, ddt_ref, ddtb_ref, dalog_ref, ddskip_ref, dng_ref, dst_ref = rest[len(shared_in):]
        s, cc = pl.program_id(0), pl.program_id(1)

        @pl.when((s == 0) & (cc == 0))
        def _():
            ddtb_ref[...] = jnp.zeros_like(ddtb_ref)
            dalog_ref[...] = jnp.zeros_like(dalog_ref)
            ddskip_ref[...] = jnp.zeros_like(ddskip_ref)
            dng_ref[...] = jnp.zeros_like(dng_ref)

        @pl.when(cc == 0)
        def _():
            dst_ref[...] = jnp.zeros_like(dst_ref)

        xs_l, z_l, b_l, c_l, ng_l, prev_l = _ssd_read(nh, ngrp, nst, xbc_ref, z_ref, ng_ref, sin_ref)
        _, vjp = jax.vjp(functools.partial(_ssd_chunk, nh, ngrp), xs_l, z_l, b_l, c_l, dt_ref[...], dtb_ref[...],
                         alog_ref[...], dskip_ref[...], ng_l, prev_l)
        dy_l = [dy_ref[:, pl.ds(q * PAIR, PAIR)].astype(f32) for q in range(nh // 2)]
        dst_l = [dst_ref[:, pl.ds(q * PAIR, PAIR)] for q in range(nh // 2)]
        dxs_l, dz_l, db_l, dc_l, ddt, ddtb, dalog, ddskip, dng_l, dprev_l = vjp((dy_l, dst_l))
        for q in range(nh // 2):
            ls = pl.ds(q * PAIR, PAIR)
            dxbc_ref[:, ls] = dxs_l[q]
            dz_ref[:, ls] = dz_l[q].astype(dz_ref.dtype)
            dng_ref[0:1, ls] += dng_l[q]
            dst_ref[:, ls] = dprev_l[q]
        for g in range(ngrp):
            dxbc_ref[:, pl.ds(cw + g * nst, nst)] = db_l[g]
            dxbc_ref[:, pl.ds(cw + ngrp * nst + g * nst, nst)] = dc_l[g]
        ddt_ref[...] = ddt.astype(ddt_ref.dtype)
        ddtb_ref[0:1, :] += ddtb
        dalog_ref[0:1, :] += dalog
        ddskip_ref[0:1, :] += ddskip

    def blk(s, cc):
        return s * nchunk + (nchunk - 1 - cc)

    prow = pl.BlockSpec((1, 128), lambda s, c: (0, 0))
    pacc = pl.BlockSpec((8, 128), lambda s, c: (0, 0))
    return pl.pallas_call(
        body, name="ssd_bwd", grid=(n_seq, nchunk),
        in_specs=[pl.BlockSpec((CHUNK, xw), lambda s, c: (blk(s, c), 0)),
                  pl.BlockSpec((CHUNK, cw), lambda s, c: (blk(s, c), z_col_block)),
                  pl.BlockSpec((CHUNK, 128), lambda s, c: (blk(s, c), 0)),
                  prow, prow, prow, pl.BlockSpec((1, cw), lambda s, c: (0, 0)),
                  pl.BlockSpec((nst, cw), lambda s, c: (blk(s, c), 0)),
                  pl.BlockSpec((CHUNK, cw), lambda s, c: (blk(s, c), dy_col_block))]
        + [pl.BlockSpec(memory_space=pl.ANY)] * len(shared_in),
        out_specs=[pl.BlockSpec((CHUNK, xw), lambda s, c: (blk(s, c), 0)),
                   pl.BlockSpec((CHUNK, cw), lambda s, c: (blk(s, c), dz_blk)),
                   pl.BlockSpec((CHUNK, 128), lambda s, c: (blk(s, c), 0)),
                   pacc, pacc, pacc, pl.BlockSpec((8, cw), lambda s, c: (0, 0))],
        out_shape=[_sds((T, xw), f32), dz_shape, _sds((T, 128), bf16),
                   _sds((8, 128), f32), _sds((8, 128), f32), _sds((8, 128), f32), _sds((8, cw), f32)],
        input_output_aliases=aliases,
        scratch_shapes=[pltpu.VMEM((nst, cw), f32)],
        compiler_params=_cparams(("arbitrary", "arbitrary")),
    )(xbc, proj, pdt, dtb, alog, dskip, ng.reshape(1, cw), sin, dy, *shared_in)


_HBM = pl.BlockSpec(memory_space=pltpu.HBM)
_SEM = pl.BlockSpec(memory_space=pltpu.SEMAPHORE)
_EFFECT = pltpu.SideEffectType.DATAFLOW_SIDE_EFFECTING


def _split_copies(n, scatter, src_refs, land_refs, send_sems, recv_sems):
    npeer = N_DEV - 1
    x, y, c = lax.axis_index("x"), lax.axis_index("y"), lax.axis_index("c")
    me = 4 * x + 2 * y + c
    copies = []
    for i in range(n):
        for k in range(1, N_DEV):
            px = 1 - x if k & 4 else x
            py = 1 - y if k & 2 else y
            pc = 1 - c if k & 1 else c
            src = src_refs[i].at[4 * px + 2 * py + pc] if scatter else src_refs[i]
            copies.append(pltpu.make_async_remote_copy(
                src_ref=src, dst_ref=land_refs[i].at[me],
                send_sem=send_sems.at[i * npeer + k - 1], recv_sem=recv_sems.at[i * npeer + k - 1],
                device_id=(px, py, pc), device_id_type=pl.DeviceIdType.MESH))
    return copies


def _exchange_start(name, arrs, scatter):
    n = len(arrs)
    nsem = n * (N_DEV - 1)
    me = 4 * lax.axis_index("x") + 2 * lax.axis_index("y") + lax.axis_index("c")
    lands = []
    for a in arrs:
        own = lax.dynamic_index_in_dim(a, me, 0, keepdims=True) if scatter else a[None]
        full = lax.empty(a.shape if scatter else (N_DEV,) + a.shape, a.dtype)
        lands.append(lax.dynamic_update_slice(full, own, (me,) + (0,) * (full.ndim - 1)))

    def body(*refs):
        src_refs, land_refs = refs[:n], refs[n:2 * n]
        send_sems, recv_sems = refs[2 * n], refs[2 * n + 1]
        token = refs[-1]
        for cp in _split_copies(n, scatter, src_refs, land_refs, send_sems, recv_sems):
            cp.start()
        token[...] = jnp.zeros_like(token)

    res = pl.pallas_call(
        body, name=name,
        out_shape=(pltpu.SemaphoreType.DMA((nsem,)), pltpu.SemaphoreType.DMA((nsem,)),
                   *[pltpu.HBM(a.shape, a.dtype) for a in arrs], *[pltpu.HBM(l.shape, l.dtype) for l in lands],
                   _sds((8, 128), f32)),
        in_specs=[_HBM] * (2 * n),
        out_specs=(_SEM, _SEM, *[_HBM] * (2 * n), pl.BlockSpec(memory_space=pltpu.VMEM)),
        input_output_aliases={j: 2 + j for j in range(2 * n)},
        compiler_params=pltpu.CompilerParams(has_side_effects=_EFFECT),
    )(*[pltpu.with_memory_space_constraint(a, pltpu.HBM) for a in arrs],
      *[pltpu.with_memory_space_constraint(l, pltpu.HBM) for l in lands])
    return (n, scatter, res[0], res[1], res[2:2 + n], res[2 + n:2 + 2 * n]), res[-1]


def _exchange_wait(name, handle, after):
    n, scatter, send_sems, recv_sems, srcs, lands = handle
    after = list(after) if isinstance(after, (list, tuple)) else [after]

    def body(*refs):
        src_refs, land_refs = refs[:n], refs[n:2 * n]
        for cp in _split_copies(n, scatter, src_refs, land_refs, refs[2 * n], refs[2 * n + 1]):
            cp.wait_send()
            cp.wait_recv()

    res = pl.pallas_call(
        body, name=name,
        out_shape=[pltpu.HBM(a.shape, a.dtype) for a in (*srcs, *lands)],
        in_specs=[_HBM] * (2 * n) + [_SEM, _SEM] + [pl.BlockSpec(memory_space=pl.ANY)] * len(after),
        out_specs=[_HBM] * (2 * n),
        input_output_aliases={j: j for j in range(2 * n)},
        compiler_params=pltpu.CompilerParams(has_side_effects=_EFFECT),
    )(*srcs, *lands, send_sems, recv_sems, *after)
    return res[n:]


def _adam_tiles(R, C):
    if R % 256 == 0:
        return (256, C), (R // 256, 1)
    assert C % 128 == 0
    return (R, 128), (1, C // 128)


def _adam(name, parts, w, m, v, layer=None, depth=None, into=None, stacked_in=False):
    P, R, C = parts.shape
    (tr, tc), (gr, gc) = _adam_tiles(R, C)
    c1 = 1.0 / (1.0 - ADAM_B1 ** ADAM_STEP)
    c2 = 1.0 / (1.0 - ADAM_B2 ** ADAM_STEP)
    into = [] if into is None else list(into)

    def body(p_ref, w_ref, m_ref, v_ref, *rest):
        g_ref, d_ref, nm_ref, nv_ref = rest[len(into):]
        g = p_ref[0].astype(f32)
        for s in range(1, P):
            g = g + p_ref[s].astype(f32)
        nm = ADAM_B1 * m_ref[...] + (1.0 - ADAM_B1) * g
        nv = ADAM_B2 * v_ref[...] + (1.0 - ADAM_B2) * (g * g)
        g_ref[...] = g
        nm_ref[...] = nm
        nv_ref[...] = nv
        d_ref[...] = -ADAM_LR * ((nm * c1) / (jnp.sqrt(nv * c2) + ADAM_EPS) + ADAM_WD * w_ref[...])

    tile = pl.BlockSpec((tr, tc), lambda i, j: (i, j))
    layer_tile = pl.BlockSpec((None, tr, tc), lambda i, j: (layer, i, j))
    out_tile, out_sds = (tile, _sds((R, C), f32)) if layer is None else (layer_tile, _sds((depth, R, C), f32))
    return pl.pallas_call(
        body, name=name, grid=(gr, gc),
        in_specs=[pl.BlockSpec((P, tr, tc), lambda i, j: (0, i, j))] + [layer_tile if stacked_in else tile] * 3
        + [pl.BlockSpec(memory_space=pl.ANY)] * len(into),
        out_specs=[out_tile] * 4, out_shape=[out_sds] * 4,
        input_output_aliases={4 + k: k for k in range(len(into))},
        compiler_params=_cparams(("arbitrary", "arbitrary")),
    )(parts, w, m, v, *into)


def _sum_parts(name, parts):
    P, R, C = parts.shape
    tr = 256 if R % 256 == 0 else R

    def body(p_ref, o_ref):
        g = p_ref[0]
        for s in range(1, P):
            g = g + p_ref[s]
        o_ref[...] = g

    return pl.pallas_call(
        body, name=name, grid=(R // tr,),
        in_specs=[pl.BlockSpec((P, tr, C), lambda i: (0, i, 0))], out_specs=pl.BlockSpec((tr, C), lambda i: (i, 0)),
        out_shape=_sds((R, C), f32), compiler_params=_cparams(("arbitrary",)),
    )(parts)


def _pad_to(a, n, axis):
    if a.shape[axis] == n:
        return a
    cfg = [(0, 0)] * a.ndim
    cfg[axis] = (0, n - a.shape[axis])
    return jnp.pad(a, cfg)


def _pack(arrs):
    flat = [_pad_to(a.reshape(-1), -(-a.size // 128) * 128, 0) for a in arrs]
    rows = jnp.concatenate(flat).reshape(-1, 128)
    return _pad_to(rows, -(-rows.shape[0] // 256) * 256, 0)


def _unpack(slab, shapes):
    flat = slab.reshape(-1)
    out, o = [], 0
    for s in shapes:
        n = math.prod(s)
        out.append(flat[o:o + n].reshape(s))
        o += -(-n // 128) * 128
    return out


_NAMES = ['norm1_g', 'w_in', 'conv_a_w', 'conv_a_b', 'ln_a_g', 'ln_a_b', 'ln_b_g', 'ln_b_b', 'w_spatial', 'b_spatial',
          'conv_c_w', 'conv_c_b', 'dt_bias', 'a_log', 'd_skip', 'norm_c_g', 'w_out', 'norm2_g', 'w_ff1', 'w_ff2', 'final_g']
_REPL = ['norm1_g', 'conv_a_b', 'ln_a_g', 'ln_a_b', 'ln_b_g', 'ln_b_b', 'w_spatial', 'b_spatial', 'conv_c_b',
         'dt_bias', 'a_log', 'd_skip', 'norm_c_g', 'norm2_g']
_CONVW = ['conv_a_w', 'conv_c_w']
_BIG = ['w_in', 'w_out', 'w_ff1', 'w_ff2']
_BIG_T = {'w_in': True, 'w_out': False, 'w_ff1': True, 'w_ff2': False}


def _row128(v):
    return _pad_to(v.reshape(1, -1), 128, 1)


def _step(p, m, v, x, loss_target):
    nb, S, D = x.shape
    T = nb * S
    depth = p['norm1_g'].shape[0]
    a_w = p['conv_a_b'].shape[1]
    b_w = p['ln_b_g'].shape[1]
    nh = p['dt_bias'].shape[1]
    c_w = p['norm_c_g'].shape[1]
    xw = p['conv_c_b'].shape[1]
    ngrp = 2
    nst = (xw - c_w) // (2 * ngrp)
    d_in = p['w_in'].shape[2] * N_DEV
    main = d_in - nh
    assert main == 2 * a_w + 2 * b_w + c_w + xw and 2 * a_w == 2 * b_w == c_w and xw % c_w == c_w // 2
    me = 4 * lax.axis_index("x") + 2 * lax.axis_index("y") + lax.axis_index("c")

    x2 = x.reshape(T, D)
    tgt = loss_target.reshape(T, D)

    def shards(i, z=None):
        z = 0.0 if z is None else z
        return [(p['w_in'][i].T + z).astype(bf16), (p['w_out'][i] + z).astype(bf16), (p['w_ff1'][i].T + z).astype(bf16),
                (p['w_ff2'][i] + z).astype(bf16), p['conv_a_w'][i], p['conv_c_w'][i]]

    def gathered_in(wt, ca, cc):
        wt = wt.reshape(d_in, D)
        ca = jnp.transpose(ca, (1, 0, 2)).reshape(KA, a_w)
        cc = jnp.transpose(cc, (1, 0, 2)).reshape(KC, xw)
        return dict(wt=wt, wt_dt=_pad_to(wt[main:], 128, 0), ca=_pad_to(ca, 32, 0), cc=_pad_to(cc, 8, 0))

    def start_layer(i, after=None):
        sh = shards(i, None if after is None else after[0, 0])
        ha, t = _exchange_start("gather_w%da_start" % i, [sh[0], sh[4], sh[5]], False)
        hb, t = _exchange_start("gather_w%db_start" % i, [shards(i, t[0, 0])[1]], False)
        hc, t = _exchange_start("gather_w%dc_start" % i, [shards(i, t[0, 0])[2]], False)
        hd, t = _exchange_start("gather_w%dd_start" % i, [shards(i, t[0, 0])[3]], False)
        return dict(a=ha, b=hb, c=hc, d=hd), t

    W, saved = [], []
    xc = x2
    H, tok = start_layer(0)
    h1, rtok = _rms_fwd(xc, p['norm1_g'][0])
    for i in range(depth):
        w = gathered_in(*_exchange_wait("gather_w%da_wait" % i, H['a'], [xc, tok]))
        W.append(w)
        Hi = H
        if i + 1 < depth:
            H, tok = start_layer(i + 1, rtok + tok)
        else:
            tok = None
        (proj,) = _mm("mm_proj", h1, w['wt'], "nt", [bf16], dep=tok, n=main)
        (pdt,) = _mm("mm_pdt", h1, w['wt_dt'], "nt", [f32])
        mixw = a_w + b_w + c_w
        ycat = _conv_fwd("confa_fwd", proj, 0, w['ca'], p['conv_a_b'][i], KA, True, nb, p['ln_a_g'][i], p['ln_a_b'][i],
                         share=(mixw, 0, None))
        ycat = _gmlp_fwd(proj, 1, p['ln_b_g'][i], p['ln_b_b'][i], p['w_spatial'][i], p['b_spatial'][i], share=(mixw, 1, ycat))
        xbc = _conv_fwd("convc_fwd", proj, 2, w['cc'], p['conv_c_b'][i], KC, False, nb)
        dtb, alog, dsk = _row128(p['dt_bias'][i]), _row128(p['a_log'][i]), _row128(p['d_skip'][i])
        ycat, sin = _ssd_fwd(xbc, proj, 2, pdt, dtb, alog, dsk, p['norm_c_g'][i], nh, ngrp, nst, nb, share=(mixw, 1, ycat))
        w['wout'] = _exchange_wait("gather_w%db_wait" % i, Hi['b'], ycat)[0].reshape(-1, D)
        xm, h2 = _mm("mm_out", ycat, w['wout'], "nn", [f32, bf16], _ep_add_rms, (xc,), rows=(p['norm2_g'][i].reshape(1, D),))
        w['w1t'] = _exchange_wait("gather_w%dc_wait" % i, Hi['c'], h2)[0].reshape(-1, D)
        f, a = _mm("mm_ff1", h2, w['w1t'], "nt", [bf16, bf16], _ep_relu2)
        w['w2'] = _exchange_wait("gather_w%dd_wait" % i, Hi['d'], a)[0].reshape(-1, D)
        saved.append(dict(x_in=xc, h1=h1, proj=proj, pdt=pdt, xbc=xbc, sin=sin, ycat=ycat, xm=xm, h2=h2, f=f, a=a,
                          dtb=dtb, alog=alog, dsk=dsk))
        if i + 1 < depth:
            xc, h1, rtok = _mm("mm_ff2", a, w['w2'], "nn", [f32, bf16], _ep_add_rms, (xm,), tok_out=True,
                               rows=(p['norm1_g'][i + 1].reshape(1, D),))
        else:
            (xc,) = _mm("mm_ff2", a, w['w2'], "nn", [f32], _ep_add, (xm,))

    lp, dx, dfinal = _loss_head(xc, p['final_g'], tgt)
    loss = lax.psum(lp[0, 0], ("x", "y", "c"))

    out = {}
    kinds = ("grad", "delta", "new_m", "new_v")
    names1 = _REPL + _CONVW

    started, small = [], [None] * depth

    def send(n, i, g):
        handle, token = _exchange_start("scatter_%s_%d_start" % (n, i), [g.reshape(N_DEV, -1, D)], True)
        started.append((n, i, handle))
        return token

    tok = None
    for i in reversed(range(depth)):
        w, sv = W[i], saved[i]
        (df,) = _mm("mm_df", dx, w['w2'], "nt", [bf16], _ep_drelu2, (sv['f'],), dep=tok)
        (gw2,) = _mm("mm_gw2", sv['a'], dx, "tn", [bf16])
        tok = send('w_ff2', i, gw2)
        dxm, dg2 = _mm("mm_dh2", df, w['w1t'], "nn", [f32], _ep_rms_bwd, (sv['xm'], dx), dep=tok,
                       rows=(p['norm2_g'][i].reshape(1, D),), n_row_out=1)
        (gw1t,) = _mm("mm_gw1", df, sv['h2'], "tn", [bf16])
        tok = send('w_ff1', i, gw1t)
        (dycat,) = _mm("mm_dycat", dxm, w['wout'], "nt", [bf16], dep=tok)
        (gwout,) = _mm("mm_gwout", sv['ycat'], dxm, "tn", [bf16])
        tok = send('w_out', i, gwout)
        dproj, dwa, dba, dlag, dlab = _conv_bwd("confa_bwd", sv['proj'], 0, w['ca'], p['conv_a_b'][i] + tok[0, 0], dycat, 0, KA,
                                                True, nb, p['ln_a_g'][i], p['ln_a_b'][i], share=(main, 0, None))
        dproj, dlbg, dlbb, dws, dbs = _gmlp_bwd(sv['proj'], 1, p['ln_b_g'][i], p['ln_b_b'][i], p['w_spatial'][i],
                                                p['b_spatial'][i], dycat, 1, share=(main, 1, dproj))
        dxbc, dproj, ddt, ddtb, dalog, ddsk, dng = _ssd_bwd(sv['xbc'], sv['proj'], 2, sv['pdt'], sv['dtb'], sv['alog'],
                                                            sv['dsk'], p['norm_c_g'][i], sv['sin'], dycat, 1, nh, ngrp, nst, nb,
                                                            share=(main, 2, dproj))
        dproj, dwc, dbc = _conv_bwd("convc_bwd", sv['proj'], 2, w['cc'], p['conv_c_b'][i], dxbc, 0, KC, False, nb,
                                    share=(main, 2, dproj))
        (dh_main,) = _mm("mm_dh1", dproj, w['wt'], "nn", [f32])
        (gwt_main,) = _mm("mm_gwin", dproj, sv['h1'], "tn", [bf16])
        (gwt_dt,) = _mm("mm_gwdt", ddt, sv['h1'], "tn", [bf16])
        tok = send('w_in', i, jnp.concatenate([gwt_main, gwt_dt[:nh]], axis=0))
        dx, dg1 = _mm("mm_dh1dt", ddt, w['wt_dt'], "nn", [f32], _ep_add_rms_bwd, (dh_main, sv['x_in'], dxm), dep=tok,
                      rows=(p['norm1_g'][i].reshape(1, D),), n_row_out=1)

        gi = dict(norm1_g=dg1[0], norm2_g=dg2[0], conv_a_w=dwa[:KA], conv_a_b=dba[0], ln_a_g=dlag[0], ln_a_b=dlab[0],
                  ln_b_g=dlbg[0], ln_b_b=dlbb[0], w_spatial=dws, b_spatial=dbs, conv_c_w=dwc[:KC], conv_c_b=dbc[0],
                  dt_bias=ddtb[0, :nh], a_log=dalog[0, :nh], d_skip=ddsk[0, :nh], norm_c_g=dng[0])
        parts_i = [gi[n] for n in names1] + ([dfinal[0]] if i == depth - 1 else [])
        handle, tok = _exchange_start("gather_g%d_start" % i, [_pack(parts_i)], False)
        small[i] = ([a.shape for a in parts_i], handle)
    grad_x = dx.reshape(nb, S, D)

    dep = [dx, tok]
    for n, i, handle in started:
        (parts,) = _exchange_wait("scatter_%s_%d_wait" % (n, i), handle, dep)
        prev = [out[(kind, n)] for kind in kinds] if (kinds[0], n) in out else None
        if _BIG_T[n]:
            res = _adam("adam_%s_%d" % (n, i), parts, p[n][i].T, m[n][i].T, v[n][i].T, layer=i, depth=depth, into=prev)
        else:
            res = _adam("adam_%s_%d" % (n, i), parts, p[n], m[n], v[n], layer=i, depth=depth, into=prev, stacked_in=True)
        for kind, r in zip(kinds, res):
            out[(kind, n)] = r
        dep = res[3]

    gsum = [None] * depth
    for i in reversed(range(depth)):
        (parts,) = _exchange_wait("gather_g%d_wait" % i, small[i][1], dep)
        gsum[i] = _sum_parts("sum_small", parts)
        dep = gsum[i]
    widths = [-(-math.prod(p[n].shape[1:]) // 128) * 128 for n in _REPL]
    rep_rows = sum(widths) // 128
    tot_rows = -(-depth * rep_rows // 256) * 256

    def rep_slab(q):
        cols = [_pad_to(q[n].reshape(depth, -1), wd, 1) for n, wd in zip(_REPL, widths)]
        return _pad_to(jnp.concatenate(cols, axis=1).reshape(-1, 128), tot_rows, 0)

    g_rep = _pad_to(jnp.concatenate([g[:rep_rows] for g in gsum], axis=0), tot_rows, 0)
    res = _adam("adam_small", g_rep[None], rep_slab(p), rep_slab(m), rep_slab(v))
    for kind, r in zip(kinds, res):
        view = r[:depth * rep_rows].reshape(depth, -1)
        o = 0
        for n, wd in zip(_REPL, widths):
            out[(kind, n)] = view[:, o:o + math.prod(p[n].shape[1:])].reshape(p[n].shape)
            o += wd

    extra = []
    for i in range(depth):
        tail = _unpack(gsum[i][rep_rows:], small[i][0][len(_REPL):])
        extra.append(tail)
    gconv = []
    for j, n in enumerate(_CONVW):
        cw_shard = p[n].shape[2]
        full = jnp.stack([extra[i][j] for i in range(depth)])
        gconv.append(lax.dynamic_slice_in_dim(full, me * cw_shard, cw_shard, axis=2))
    tail_names = _CONVW + ['final_g']
    res = _adam("adam_conv", _pack(gconv + [extra[depth - 1][len(_CONVW)]])[None],
                *[_pack([q[n] for n in tail_names]) for q in (p, m, v)])
    for kind, r in zip(kinds, res):
        for n, arr in zip(tail_names, _unpack(r, [p[n].shape for n in tail_names])):
            out[(kind, n)] = arr
    for n in _BIG:
        if _BIG_T[n]:
            for kind in kinds:
                out[(kind, n)] = jnp.swapaxes(out[(kind, n)], 1, 2)

    flat = [loss, grad_x]
    for kind in ("grad", "delta", "new_m", "new_v"):
        flat += [out[(kind, n)] for n in _NAMES]
    return tuple(flat)


def kernel(x, norm1_g, w_in, conv_a_w, conv_a_b, ln_a_g, ln_a_b, ln_b_g, ln_b_b, w_spatial, b_spatial, conv_c_w, conv_c_b, dt_bias, a_log, d_skip, norm_c_g, w_out, norm2_g, w_ff1, w_ff2, final_g, loss_target, m_norm1_g, m_w_in, m_conv_a_w, m_conv_a_b, m_ln_a_g, m_ln_a_b, m_ln_b_g, m_ln_b_b, m_w_spatial, m_b_spatial, m_conv_c_w, m_conv_c_b, m_dt_bias, m_a_log, m_d_skip, m_norm_c_g, m_w_out, m_norm2_g, m_w_ff1, m_w_ff2, m_final_g, v_norm1_g, v_w_in, v_conv_a_w, v_conv_a_b, v_ln_a_g, v_ln_a_b, v_ln_b_g, v_ln_b_b, v_w_spatial, v_b_spatial, v_conv_c_w, v_conv_c_b, v_dt_bias, v_a_log, v_d_skip, v_norm_c_g, v_w_out, v_norm2_g, v_w_ff1, v_w_ff2, v_final_g):
    p = dict(zip(_NAMES, (norm1_g, w_in, conv_a_w, conv_a_b, ln_a_g, ln_a_b, ln_b_g, ln_b_b, w_spatial, b_spatial, conv_c_w,
                          conv_c_b, dt_bias, a_log, d_skip, norm_c_g, w_out, norm2_g, w_ff1, w_ff2, final_g)))
    m = dict(zip(_NAMES, (m_norm1_g, m_w_in, m_conv_a_w, m_conv_a_b, m_ln_a_g, m_ln_a_b, m_ln_b_g, m_ln_b_b, m_w_spatial,
                          m_b_spatial, m_conv_c_w, m_conv_c_b, m_dt_bias, m_a_log, m_d_skip, m_norm_c_g, m_w_out, m_norm2_g,
                          m_w_ff1, m_w_ff2, m_final_g)))
    v = dict(zip(_NAMES, (v_norm1_g, v_w_in, v_conv_a_w, v_conv_a_b, v_ln_a_g, v_ln_a_b, v_ln_b_g, v_ln_b_b, v_w_spatial,
                          v_b_spatial, v_conv_c_w, v_conv_c_b, v_dt_bias, v_a_log, v_d_skip, v_norm_c_g, v_w_out, v_norm2_g,
                          v_w_ff1, v_w_ff2, v_final_g)))
    return _step(p, m, v, x, loss_target)
```

```python
import functools
import math

import jax
import jax.numpy as jnp
from jax import lax
from jax.experimental import pallas as pl
from jax.experimental.pallas import tpu as pltpu

f32 = jnp.float32
bf16 = jnp.bfloat16
HI = lax.Precision.HIGHEST
EPS = 1e-5
HEAD = 64
CHUNK = 128
KA = 31
KC = 4
N_DEV = 8
VMEM_LIMIT = 56 * 1024 * 1024
MM_VMEM_BUDGET = 52 * 1024 * 1024

ADAM_LR = 0.001
ADAM_B1 = 0.9
ADAM_B2 = 0.999
ADAM_EPS = 1e-08
ADAM_WD = 0.01
ADAM_STEP = 10


def _cparams(sem=None):
    return pltpu.CompilerParams(dimension_semantics=sem, vmem_limit_bytes=VMEM_LIMIT)


def _sds(shape, dtype):
    return jax.ShapeDtypeStruct(shape, dtype)


_DIMS = {"nn": ((1,), (0,)), "nt": ((1,), (1,)), "tn": ((0,), (0,))}


def _dot16(a, b, form):
    return lax.dot_general(a.astype(bf16), b.astype(bf16), (_DIMS[form], ((), ())), preferred_element_type=f32)


@functools.partial(jax.custom_vjp, nondiff_argnums=(2,))
def _bdot(a, b, form):
    return _dot16(a, b, form)


def _bdot_fwd(a, b, form):
    return _dot16(a, b, form), (a, b)


def _bdot_bwd(form, res, ct):
    a, b = res
    if form == "nn":
        da, db = _dot16(ct, b, "nt"), _dot16(a, ct, "tn")
    elif form == "nt":
        da, db = _dot16(ct, b, "nn"), _dot16(ct, a, "tn")
    else:
        da, db = _dot16(b, ct, "nt"), _dot16(a, ct, "nn")
    return da.astype(a.dtype), db.astype(b.dtype)


_bdot.defvjp(_bdot_fwd, _bdot_bwd)


def _tile(n, cap):
    if n <= cap:
        return n
    for d in range(cap - cap % 128, 0, -128):
        if n % d == 0:
            return d
    raise ValueError((n, cap))


def _mm(name, a, b, form, out_dtypes, epilogue=None, extras=(), tm=2048, tn=1024, tk=8192, dep=None, rows=(), n_row_out=0,
        n=None, tok_out=False):
    if form == "tn":
        K, M = a.shape
    else:
        M, K = a.shape
    N = n if n is not None else (b.shape[0] if form == "nt" else b.shape[1])
    tm, tn, tk = _tile(M, tm), _tile(N, tn), _tile(K, tk)
    nk = K // tk

    def vmem_bytes(tm):
        mn = sum(jnp.dtype(e.dtype).itemsize for e in extras) + sum(jnp.dtype(d).itemsize for d in out_dtypes)
        return 2 * (tm * tk * a.dtype.itemsize + tk * tn * b.dtype.itemsize + tm * tn * mn) + 2 * tm * tn * 4

    while vmem_bytes(tm) > MM_VMEM_BUDGET and tm % 256 == 0:
        tm //= 2
    ne, no, nr = len(extras), len(out_dtypes), len(rows)
    assert n_row_out == 0 or tn == N
    deps = () if dep is None else (dep,)
    if epilogue is None:
        epilogue = lambda acc: (acc,)

    def body(a_ref, b_ref, *rest):
        extra_refs, row_refs = rest[:ne], rest[ne:ne + nr]
        rest = rest[ne + nr + len(deps):]
        out_refs, rowout_refs = rest[:no], rest[no:no + n_row_out]
        ntok = 1 if tok_out else 0
        part = lax.dot_general(a_ref[...].astype(bf16), b_ref[...].astype(bf16),
                               (_DIMS[form], ((), ())), preferred_element_type=f32)

        def finish(acc):
            if tok_out:
                rest[no + n_row_out][...] = jnp.zeros((8, 128), f32)
            outs = epilogue(acc, *[e[...] for e in extra_refs], *[r[...] for r in row_refs])
            for o_ref, v in zip(out_refs, outs[:no]):
                o_ref[...] = v.astype(o_ref.dtype)
            for r_ref, v in zip(rowout_refs, outs[no:]):
                @pl.when(pl.program_id(0) == 0)
                def _():
                    r_ref[...] = jnp.zeros_like(r_ref)

                r_ref[0:1, :] += v

        if nk == 1:
            finish(part)
            return
        acc_ref = rest[no + n_row_out + ntok]
        k = pl.program_id(2)

        @pl.when(k == 0)
        def _():
            acc_ref[...] = part

        @pl.when((k > 0) & (k < nk - 1))
        def _():
            acc_ref[...] += part

        @pl.when(k == nk - 1)
        def _():
            finish(acc_ref[...] + part)

    a_spec = pl.BlockSpec((tk, tm), lambda i, j, k: (k, i)) if form == "tn" else pl.BlockSpec((tm, tk), lambda i, j, k: (i, k))
    b_spec = pl.BlockSpec((tn, tk), lambda i, j, k: (j, k)) if form == "nt" else pl.BlockSpec((tk, tn), lambda i, j, k: (k, j))
    mn_spec = pl.BlockSpec((tm, tn), lambda i, j, k: (i, j))
    return pl.pallas_call(
        body, name=name, grid=(M // tm, N // tn, nk),
        in_specs=[a_spec, b_spec] + [mn_spec] * ne + [pl.BlockSpec((1, tn), lambda i, j, k: (0, j))] * nr
        + [pl.BlockSpec((8, 128), lambda i, j, k: (0, 0))] * len(deps),
        out_specs=[mn_spec] * no + [pl.BlockSpec((8, tn), lambda i, j, k: (0, j))] * n_row_out
        + [pl.BlockSpec((8, 128), lambda i, j, k: (0, 0))] * (1 if tok_out else 0),
        out_shape=[_sds((M, N), d) for d in out_dtypes] + [_sds((8, N), f32)] * n_row_out
        + [_sds((8, 128), f32)] * (1 if tok_out else 0),
        scratch_shapes=[pltpu.VMEM((tm, tn), f32)] if nk > 1 else [],
        compiler_params=_cparams(("arbitrary", "arbitrary", "arbitrary")),
    )(a, b, *extras, *rows, *deps)


def _ep_add(acc, r):
    return (acc + r,)


def _ep_add_rms(acc, r, g):
    x = acc + r
    return x, _rms(x, g)


def _ep_rms_bwd(acc, x, dres, g):
    _, vjp = jax.vjp(_rms, x, g)
    dx, dg = vjp(acc)
    return dres + dx, dg


def _ep_add_rms_bwd(acc, more, x, dres, g):
    return _ep_rms_bwd(acc + more, x, dres, g)


def _ep_relu2(acc):
    r = jnp.maximum(acc, 0.0)
    return acc, r * r


def _ep_drelu2(acc, f):
    return (acc * 2.0 * jnp.maximum(f, 0.0),)


def _rms(x, g):
    return x * lax.rsqrt(jnp.mean(x * x, axis=-1, keepdims=True) + EPS) * g


TT = 512


def _rms_fwd(x, g):
    T, D = x.shape

    def body(x_ref, g_ref, h_ref, tok_ref):
        h_ref[...] = _rms(x_ref[...], g_ref[...]).astype(bf16)
        tok_ref[...] = jnp.zeros_like(tok_ref)

    return pl.pallas_call(
        body, name="rms_fwd", grid=(T // TT,),
        in_specs=[pl.BlockSpec((TT, D), lambda i: (i, 0)), pl.BlockSpec((1, D), lambda i: (0, 0))],
        out_specs=[pl.BlockSpec((TT, D), lambda i: (i, 0)), pl.BlockSpec((8, 128), lambda i: (0, 0))],
        out_shape=[_sds((T, D), bf16), _sds((8, 128), f32)], compiler_params=_cparams(("arbitrary",)),
    )(x, g.reshape(1, D))


def _loss_head(x, g, tgt):
    T, D = x.shape

    def f(xv, gv, tv):
        e = _rms(xv, gv) - tv
        return 0.5 * jnp.sum(jnp.sum(e * e, axis=-1, keepdims=True) * (1.0 / D), axis=0, keepdims=True)

    def body(x_ref, g_ref, t_ref, loss_ref, dx_ref, dg_ref):
        tv = t_ref[...]
        l, vjp = jax.vjp(lambda xv, gv: f(xv, gv, tv), x_ref[...], g_ref[...])
        dx, dg = vjp(jnp.ones((1, 1), f32))
        dx_ref[...] = dx

        @pl.when(pl.program_id(0) == 0)
        def _():
            dg_ref[...] = jnp.zeros_like(dg_ref)
            loss_ref[...] = jnp.zeros_like(loss_ref)

        dg_ref[0:1, :] += dg
        loss_ref[...] += jnp.broadcast_to(l, loss_ref.shape)

    tile = pl.BlockSpec((TT, D), lambda i: (i, 0))
    return pl.pallas_call(
        body, name="loss_head", grid=(T // TT,),
        in_specs=[tile, pl.BlockSpec((1, D), lambda i: (0, 0)), tile],
        out_specs=[pl.BlockSpec((8, 128), lambda i: (0, 0)), tile, pl.BlockSpec((8, D), lambda i: (0, 0))],
        out_shape=[_sds((8, 128), f32), _sds((T, D), f32), _sds((8, D), f32)],
        compiler_params=_cparams(("arbitrary",)),
    )(x, g.reshape(1, D), tgt)


TB = 256


def _glu(a_val, a_gate):
    return a_val * jax.nn.sigmoid(a_gate)


PAIR = 2 * HEAD


def _pair_mean(x, lo):
    s_lo = jnp.sum(jnp.where(lo, x, 0.0), axis=-1, keepdims=True)
    s_hi = jnp.sum(jnp.where(lo, 0.0, x), axis=-1, keepdims=True)
    return jnp.where(lo, s_lo, s_hi) * (1.0 / HEAD)


def _pair_ln(v, g, b):
    lo = lax.broadcasted_iota(jnp.int32, v.shape, 1) < HEAD
    vc = v - _pair_mean(v, lo)
    var = _pair_mean(vc * vc, lo)
    return vc * lax.rsqrt(var + EPS) * g + b


def _ln_silu(v, g, b):
    return jax.nn.silu(_pair_ln(v, g, b))


def _conv_geom(kw):
    halo = 32 if kw > 9 else 16
    return halo, halo - (kw - 1)


def _residues(shifts):
    return sorted({s % 8 for s in shifts} - {0})


def _shift_copies(src_ref, cp_ref, res, rows, ls):
    for j, r in enumerate(res):
        cp_ref[j, :, ls] = src_ref[pl.ds(r, rows), ls]


def _shifted(src_ref, cp_ref, res, shift, size, ls):
    r = shift % 8
    if r == 0:
        return src_ref[pl.ds(shift, size), ls]
    return cp_ref[res.index(r), pl.ds(shift - r, size), ls]


def _conv_taps(hp_ref, hs_ref, w_ref, b_ref, acc_ref, kw, off, halo, width):
    res = _residues(range(off, off + kw))
    for c in range(width // 128):
        ls = pl.ds(c * 128, 128)
        _shift_copies(hp_ref, hs_ref, res, halo + TB, ls)
        acc = jnp.broadcast_to(b_ref[:, ls], (TB, 128))
        for k in range(kw):
            acc = acc + w_ref[k:k + 1, ls] * _shifted(hp_ref, hs_ref, res, off + k, TB, ls)
        acc_ref[:, ls] = acc


def _conv_fwd(name, src, col_block, w, b, kw, conformer, n_seq, ln_g=None, ln_b=None, share=None):
    T = src.shape[0]
    cout = w.shape[1]
    cin = 2 * cout if conformer else cout
    halo, off = _conv_geom(kw)
    nblk = T // n_seq // TB
    hb = TB // halo
    out_shape, out_blk, shared_in, aliases = _shared(share, T, cout, bf16 if conformer else f32, 6 if conformer else 4)

    def body(cur_ref, halo_ref, w_ref, b_ref, *rest):
        if conformer:
            g_ref, lb_ref = rest[:2]
            rest = rest[2:]
        out_ref, hp_ref, acc_ref, hs_ref = rest[len(shared_in):]
        i = pl.program_id(1)
        first = (i == 0)

        @pl.when((pl.program_id(0) == 0) & first)
        def _():
            hp_ref[pl.ds(halo + TB, 8), :] = jnp.zeros((8, cout), f32)

        if conformer:
            hp_ref[pl.ds(halo, TB), :] = _glu(cur_ref[:, 0:cout].astype(f32), cur_ref[:, cout:cin].astype(f32))
            hh = _glu(halo_ref[:, 0:cout].astype(f32), halo_ref[:, cout:cin].astype(f32))
        else:
            hp_ref[pl.ds(halo, TB), :] = cur_ref[...].astype(f32)
            hh = halo_ref[...].astype(f32)
        hp_ref[pl.ds(0, halo), :] = jnp.where(first, 0.0, hh)
        _conv_taps(hp_ref, hs_ref, w_ref, b_ref, acc_ref, kw, off, halo, cout)
        if conformer:
            for q in range(cout // PAIR):
                ls = pl.ds(q * PAIR, PAIR)
                out_ref[:, ls] = _ln_silu(acc_ref[:, ls], g_ref[:, ls], lb_ref[:, ls]).astype(out_ref.dtype)
        else:
            out_ref[...] = jax.nn.silu(acc_ref[...]).astype(out_ref.dtype)

    nres = len(_residues(range(off, off + kw)))

    row = pl.BlockSpec((1, cout), lambda s, i: (0, 0))
    in_specs = [pl.BlockSpec((TB, cin), lambda s, i: (s * nblk + i, col_block)),
                pl.BlockSpec((halo, cin), lambda s, i: (jnp.maximum((s * nblk + i) * hb - 1, 0), col_block)),
                pl.BlockSpec((w.shape[0], cout), lambda s, i: (0, 0)), row]
    args = [src, src, w, b.reshape(1, cout)]
    if conformer:
        in_specs += [row, row]
        args += [ln_g.reshape(1, cout), ln_b.reshape(1, cout)]
    in_specs += [pl.BlockSpec(memory_space=pl.ANY)] * len(shared_in)
    args += shared_in
    return pl.pallas_call(
        body, name=name, grid=(n_seq, nblk), in_specs=in_specs,
        out_specs=pl.BlockSpec((TB, cout), lambda s, i: (s * nblk + i, out_blk)),
        out_shape=out_shape, input_output_aliases=aliases,
        scratch_shapes=[pltpu.VMEM((halo + TB + 8, cout), f32), pltpu.VMEM((TB, cout), f32),
                        pltpu.VMEM((nres, halo + TB, cout), f32)],
        compiler_params=_cparams(("arbitrary", "arbitrary")),
    )(*args)


def _shared(share, T, width, dtype, n_inputs, out_index=0):
    if share is None:
        return _sds((T, width), dtype), 0, [], {}
    total, blk, into = share
    if into is None:
        return _sds((T, total), dtype), blk, [], {}
    return _sds((T, total), dtype), blk, [into], {n_inputs: out_index}


def _conv_bwd(name, src, col_block, w, b, dy, dy_col_block, kw, conformer, n_seq, ln_g=None, ln_b=None, share=None):
    T = src.shape[0]
    cout = w.shape[1]
    wrows = w.shape[0]
    cin = 2 * cout if conformer else cout
    halo, off = _conv_geom(kw)
    nblk = T // n_seq // TB
    hb = TB // halo
    dsrc_shape, dsrc_blk, shared_in, aliases = _shared(share, T, cin, bf16, 7 if conformer else 5)

    def body(cur_ref, halo_ref, w_ref, b_ref, dy_ref, *rest):
        if conformer:
            g_ref, lb_ref = rest[:2]
            rest = rest[2:]
        rest = rest[len(shared_in):]
        if conformer:
            dsrc_ref, dw_ref, db_ref, dg_ref, dlb_ref, hp_ref, acc_ref, dz_ref, dhp_ref, carry_ref, hs_ref, dzs_ref = rest
        else:
            dsrc_ref, dw_ref, db_ref, hp_ref, acc_ref, dz_ref, dhp_ref, carry_ref, hs_ref, dzs_ref = rest
        s, ii = pl.program_id(0), pl.program_id(1)
        i = nblk - 1 - ii
        first = (i == 0)

        @pl.when((s == 0) & (ii == 0))
        def _():
            dw_ref[...] = jnp.zeros_like(dw_ref)
            db_ref[...] = jnp.zeros_like(db_ref)
            hp_ref[pl.ds(halo + TB, 8), :] = jnp.zeros((8, cout), f32)
            if conformer:
                dg_ref[...] = jnp.zeros_like(dg_ref)
                dlb_ref[...] = jnp.zeros_like(dlb_ref)

        @pl.when(ii == 0)
        def _():
            carry_ref[...] = jnp.zeros_like(carry_ref)
            dz_ref[pl.ds(0, halo), :] = jnp.zeros((halo, cout), f32)
            dz_ref[pl.ds(halo + TB, halo), :] = jnp.zeros((halo, cout), f32)

        if conformer:
            hp_ref[pl.ds(halo, TB), :] = _glu(cur_ref[:, 0:cout].astype(f32), cur_ref[:, cout:cin].astype(f32))
            hh = _glu(halo_ref[:, 0:cout].astype(f32), halo_ref[:, cout:cin].astype(f32))
        else:
            hp_ref[pl.ds(halo, TB), :] = cur_ref[...].astype(f32)
            hh = halo_ref[...].astype(f32)
        hp_ref[pl.ds(0, halo), :] = jnp.where(first, 0.0, hh)
        _conv_taps(hp_ref, hs_ref, w_ref, b_ref, acc_ref, kw, off, halo, cout)

        if conformer:
            for q in range(cout // PAIR):
                ls = pl.ds(q * PAIR, PAIR)
                _, vjp = jax.vjp(_ln_silu, acc_ref[:, ls], g_ref[:, ls], lb_ref[:, ls])
                da, dg, dlb = vjp(dy_ref[:, ls].astype(f32))
                dz_ref[pl.ds(halo, TB), ls] = da
                dg_ref[0:1, ls] += dg
                dlb_ref[0:1, ls] += dlb
        else:
            _, vjp = jax.vjp(jax.nn.silu, acc_ref[...])
            dz_ref[pl.ds(halo, TB), :] = vjp(dy_ref[...].astype(f32))[0]

        res_h = _residues(range(off, off + kw))
        res_z = _residues(range(kw))
        for c in range(cout // 128):
            ls = pl.ds(c * 128, 128)
            _shift_copies(dz_ref, dzs_ref, res_z, halo + TB + halo - 8, ls)
            dacc = dz_ref[pl.ds(halo, TB), ls]
            db_ref[0:1, ls] += jnp.sum(dacc, axis=0, keepdims=True)
            dhp = jnp.zeros((halo + TB, 128), f32)
            for k in range(kw):
                dw_ref[k:k + 1, ls] += jnp.sum(dacc * _shifted(hp_ref, hs_ref, res_h, off + k, TB, ls), axis=0, keepdims=True)
                dhp = dhp + w_ref[k:k + 1, ls] * _shifted(dz_ref, dzs_ref, res_z, kw - 1 - k, halo + TB, ls)
            dhp_ref[:, ls] = dhp
        dhp_ref[pl.ds(TB, halo), :] += carry_ref[...]
        carry_ref[...] = dhp_ref[pl.ds(0, halo), :]
        dcur = dhp_ref[pl.ds(halo, TB), :]
        if conformer:
            _, vjp = jax.vjp(_glu, cur_ref[:, 0:cout].astype(f32), cur_ref[:, cout:cin].astype(f32))
            dval, dgate = vjp(dcur)
            dsrc_ref[:, 0:cout] = dval.astype(dsrc_ref.dtype)
            dsrc_ref[:, cout:cin] = dgate.astype(dsrc_ref.dtype)
        else:
            dsrc_ref[...] = dcur.astype(dsrc_ref.dtype)

    def blk(s, ii):
        return s * nblk + (nblk - 1 - ii)

    row = pl.BlockSpec((1, cout), lambda s, ii: (0, 0))
    acc8 = pl.BlockSpec((8, cout), lambda s, ii: (0, 0))
    in_specs = [pl.BlockSpec((TB, cin), lambda s, ii: (blk(s, ii), col_block)),
                pl.BlockSpec((halo, cin), lambda s, ii: (jnp.maximum(blk(s, ii) * hb - 1, 0), col_block)),
                pl.BlockSpec((wrows, cout), lambda s, ii: (0, 0)), row,
                pl.BlockSpec((TB, cout), lambda s, ii: (blk(s, ii), dy_col_block))]
    args = [src, src, w, b.reshape(1, cout), dy]
    out_specs = [pl.BlockSpec((TB, cin), lambda s, ii: (blk(s, ii), dsrc_blk)),
                 pl.BlockSpec((wrows, cout), lambda s, ii: (0, 0)), acc8]
    out_shape = [dsrc_shape, _sds((wrows, cout), f32), _sds((8, cout), f32)]
    if conformer:
        in_specs += [row, row]
        args += [ln_g.reshape(1, cout), ln_b.reshape(1, cout)]
        out_specs += [acc8, acc8]
        out_shape += [_sds((8, cout), f32), _sds((8, cout), f32)]
    in_specs += [pl.BlockSpec(memory_space=pl.ANY)] * len(shared_in)
    args += shared_in
    return pl.pallas_call(
        body, name=name, grid=(n_seq, nblk), in_specs=in_specs, out_specs=out_specs, out_shape=out_shape,
        input_output_aliases=aliases,
        scratch_shapes=[pltpu.VMEM((halo + TB + 8, cout), f32), pltpu.VMEM((TB, cout), f32),
                        pltpu.VMEM((halo + TB + halo, cout), f32), pltpu.VMEM((halo + TB, cout), f32),
                        pltpu.VMEM((halo, cout), f32),
                        pltpu.VMEM((len(_residues(range(off, off + kw))), halo + TB, cout), f32),
                        pltpu.VMEM((len(_residues(range(kw))), halo + TB + halo - 8, cout), f32)],
        compiler_params=_cparams(("arbitrary", "arbitrary")),
    )(*args)


def _gelu(x):
    return 0.5 * x * (1.0 + lax.erf(x * (1.0 / math.sqrt(2.0))))


def _tril_mask(n):
    r = lax.broadcasted_iota(jnp.int32, (n, n), 0)
    c = lax.broadcasted_iota(jnp.int32, (n, n), 1)
    return r >= c


def _head_spread(nh):
    r = lax.broadcasted_iota(jnp.int32, (nh, nh * HEAD), 0)
    c = lax.broadcasted_iota(jnp.int32, (nh, nh * HEAD), 1)
    return (c // HEAD == r).astype(f32)


def _gmlp_bias(bs):
    return lax.dot_general(bs, _head_spread(bs.shape[0]), (((0,), (0,)), ((), ())), precision=HI, preferred_element_type=f32)


def _gmlp_pair(bu, bv, g, b, w_a, w_b, bias):
    lo = lax.broadcasted_iota(jnp.int32, bu.shape, 1) < HEAD
    tril = _tril_mask(CHUNK)
    u = _gelu(bu)
    vn = _pair_ln(_gelu(bv), g, b)
    mix = jnp.where(lo, _bdot(jnp.where(tril, w_a, 0.0), vn, "nn"), _bdot(jnp.where(tril, w_b, 0.0), vn, "nn"))
    return u * (mix + bias)


def _gmlp_fwd(proj, col_block, ln_g, ln_b, w_s, b_s, share=None):
    T = proj.shape[0]
    nh = w_s.shape[0]
    width = nh * HEAD
    out_shape, out_blk, shared_in, aliases = _shared(share, T, width, bf16, 5)

    def body(p_ref, g_ref, b_ref, w_ref, bs_ref, *rest):
        out_ref, bias_ref = rest[len(shared_in):]

        @pl.when(pl.program_id(0) == 0)
        def _():
            bias_ref[...] = _gmlp_bias(bs_ref[...])

        for q in range(nh // 2):
            ls = pl.ds(q * PAIR, PAIR)
            lv = pl.ds(width + q * PAIR, PAIR)
            out_ref[:, ls] = _gmlp_pair(p_ref[:, ls].astype(f32), p_ref[:, lv].astype(f32), g_ref[:, ls], b_ref[:, ls], w_ref[2 * q], w_ref[2 * q + 1],
                                        bias_ref[:, ls]).astype(out_ref.dtype)

    row = pl.BlockSpec((1, width), lambda i: (0, 0))
    return pl.pallas_call(
        body, name="gmlp_fwd", grid=(T // CHUNK,),
        in_specs=[pl.BlockSpec((CHUNK, 2 * width), lambda i: (i, col_block)), row, row,
                  pl.BlockSpec((nh, CHUNK, CHUNK), lambda i: (0, 0, 0)), pl.BlockSpec((nh, CHUNK), lambda i: (0, 0))]
        + [pl.BlockSpec(memory_space=pl.ANY)] * len(shared_in),
        out_specs=pl.BlockSpec((CHUNK, width), lambda i: (i, out_blk)),
        out_shape=out_shape, input_output_aliases=aliases, scratch_shapes=[pltpu.VMEM((CHUNK, width), f32)],
        compiler_params=_cparams(("arbitrary",)),
    )(proj, ln_g.reshape(1, width), ln_b.reshape(1, width), w_s, b_s, *shared_in)


def _gmlp_bwd(proj, col_block, ln_g, ln_b, w_s, b_s, dy, dy_col_block, share=None):
    T = proj.shape[0]
    nh = w_s.shape[0]
    width = nh * HEAD
    nstep = T // CHUNK
    dp_shape, dp_blk, shared_in, aliases = _shared(share, T, 2 * width, bf16, 6)

    def body(p_ref, g_ref, b_ref, w_ref, bs_ref, dy_ref, *rest):
        dp_ref, dg_ref, db_ref, dw_ref, dbst_ref, bias_ref, dbias_ref = rest[len(shared_in):]

        @pl.when(pl.program_id(0) == 0)
        def _():
            dg_ref[...] = jnp.zeros_like(dg_ref)
            db_ref[...] = jnp.zeros_like(db_ref)
            dw_ref[...] = jnp.zeros_like(dw_ref)
            dbias_ref[...] = jnp.zeros_like(dbias_ref)
            bias_ref[...] = _gmlp_bias(bs_ref[...])

        for q in range(nh // 2):
            ls = pl.ds(q * PAIR, PAIR)
            lv = pl.ds(width + q * PAIR, PAIR)
            _, vjp = jax.vjp(_gmlp_pair, p_ref[:, ls].astype(f32), p_ref[:, lv].astype(f32), g_ref[:, ls], b_ref[:, ls], w_ref[2 * q], w_ref[2 * q + 1],
                             bias_ref[:, ls])
            dbu, dbv, dg, db, dw_a, dw_b, dbias = vjp(dy_ref[:, ls].astype(f32))
            dp_ref[:, ls] = dbu.astype(dp_ref.dtype)
            dp_ref[:, lv] = dbv.astype(dp_ref.dtype)
            dg_ref[0:1, ls] += dg
            db_ref[0:1, ls] += db
            dw_ref[2 * q] += dw_a
            dw_ref[2 * q + 1] += dw_b
            dbias_ref[:, ls] += dbias

        @pl.when(pl.program_id(0) == nstep - 1)
        def _():
            dbst_ref[...] = lax.dot_general(dbias_ref[...], _head_spread(nh), (((1,), (1,)), ((), ())),
                                            precision=HI, preferred_element_type=f32)

    row = pl.BlockSpec((1, width), lambda i: (0, 0))
    acc8 = pl.BlockSpec((8, width), lambda i: (0, 0))
    wspec = pl.BlockSpec((nh, CHUNK, CHUNK), lambda i: (0, 0, 0))
    res = pl.pallas_call(
        body, name="gmlp_bwd", grid=(nstep,),
        in_specs=[pl.BlockSpec((CHUNK, 2 * width), lambda i: (i, col_block)), row, row, wspec,
                  pl.BlockSpec((nh, CHUNK), lambda i: (0, 0)), pl.BlockSpec((CHUNK, width), lambda i: (i, dy_col_block))]
        + [pl.BlockSpec(memory_space=pl.ANY)] * len(shared_in),
        out_specs=[pl.BlockSpec((CHUNK, 2 * width), lambda i: (i, dp_blk)), acc8, acc8, wspec,
                   pl.BlockSpec((CHUNK, nh), lambda i: (0, 0))],
        out_shape=[dp_shape, _sds((8, width), f32), _sds((8, width), f32),
                   _sds((nh, CHUNK, CHUNK), f32), _sds((CHUNK, nh), f32)],
        input_output_aliases=aliases,
        scratch_shapes=[pltpu.VMEM((CHUNK, width), f32), pltpu.VMEM((CHUNK, width), f32)],
        compiler_params=_cparams(("arbitrary",)),
    )(proj, ln_g.reshape(1, width), ln_b.reshape(1, width), w_s, b_s, dy, *shared_in)
    return res[0], res[1], res[2], res[3], res[4].T


def _sel_col(x, h):
    lane = lax.broadcasted_iota(jnp.int32, x.shape, 1)
    return jnp.sum(jnp.where(lane == h, x, 0.0), axis=1, keepdims=True)


def _sel_row(x, h):
    sub = lax.broadcasted_iota(jnp.int32, x.shape, 0)
    return jnp.sum(jnp.where(sub == h, x, 0.0), axis=0, keepdims=True)


def _ssd_chunk(nh, ngrp, xs_l, z_l, b_l, c_l, dtraw, dtb, alog, dskip, ng_l, prev_l):
    hg = nh // ngrp
    tril = _tril_mask(CHUNK)
    tl = tril.astype(f32)
    lo = lax.broadcasted_iota(jnp.int32, (CHUNK, PAIR), 1) < HEAD
    lo_row = lo[0:1, :]
    dt = jax.nn.softplus(dtraw + dtb)
    a = dt * (-jnp.exp(alog))
    cs = jnp.dot(tl, a, precision=HI, preferred_element_type=f32)
    cst = lax.dot_general(a, tl, (((0,), (1,)), ((), ())), precision=HI, preferred_element_type=f32)
    cb_l = [_bdot(c_l[g], b_l[g], "nt") for g in range(ngrp)]
    yz_l, new_prev = [], []
    for q in range(nh // 2):
        g = (2 * q) // hg
        cols = []
        for h in (2 * q, 2 * q + 1):
            cs_h = _sel_col(cs, h)
            tot = _sel_row(cs_h, CHUNK - 1)
            seg = jnp.where(tril, cs_h - _sel_row(cst, h), 0.0)
            lmat = jnp.where(tril, jnp.exp(seg), 0.0)
            cols.append((_sel_col(dt, h), cs_h, tot, lmat, _sel_col(dskip, h)))
        (dt_a, cs_a, tot_a, l_a, dsk_a), (dt_b, cs_b, tot_b, l_b, dsk_b) = cols
        xs = xs_l[q]
        x = xs * jnp.where(lo, dt_a, dt_b)
        ydiag = jnp.where(lo, _bdot(cb_l[g] * l_a, x, "nn"), _bdot(cb_l[g] * l_b, x, "nn"))
        yoff = _bdot(c_l[g], prev_l[q], "nn") * jnp.where(lo, jnp.exp(cs_a), jnp.exp(cs_b))
        xdec = x * jnp.where(lo, jnp.exp(tot_a - cs_a), jnp.exp(tot_b - cs_b))
        st = _bdot(b_l[g], xdec, "tn")
        new_prev.append(prev_l[q] * jnp.where(lo_row, jnp.exp(tot_a), jnp.exp(tot_b)) + st)
        y = ydiag + yoff + jnp.where(lo_row, dsk_a, dsk_b) * xs
        yz_l.append(y * jax.nn.silu(z_l[q]))
    out = [None] * (nh // 2)
    qg = hg // 2
    for g in range(ngrp):
        ssq = sum(jnp.sum(yz_l[q] * yz_l[q], axis=-1, keepdims=True) for q in range(g * qg, (g + 1) * qg))
        r = lax.rsqrt(ssq * (1.0 / (hg * HEAD)) + EPS)
        for q in range(g * qg, (g + 1) * qg):
            out[q] = yz_l[q] * r * ng_l[q]
    return out, new_prev


def _ssd_read(nh, ngrp, nst, xbc_ref, z_ref, ng_ref, st_ref):
    cw = nh * HEAD
    xs_l = [xbc_ref[:, pl.ds(q * PAIR, PAIR)] for q in range(nh // 2)]
    b_l = [xbc_ref[:, pl.ds(cw + g * nst, nst)] for g in range(ngrp)]
    c_l = [xbc_ref[:, pl.ds(cw + ngrp * nst + g * nst, nst)] for g in range(ngrp)]
    z_l = [z_ref[:, pl.ds(q * PAIR, PAIR)].astype(f32) for q in range(nh // 2)]
    ng_l = [ng_ref[:, pl.ds(q * PAIR, PAIR)] for q in range(nh // 2)]
    prev_l = [st_ref[:, pl.ds(q * PAIR, PAIR)] for q in range(nh // 2)]
    return xs_l, z_l, b_l, c_l, ng_l, prev_l


def _ssd_fwd(xbc, proj, z_col_block, pdt, dtb, alog, dskip, ng, nh, ngrp, nst, n_seq, share=None):
    T = xbc.shape[0]
    cw = nh * HEAD
    nchunk = T // n_seq // CHUNK
    assert nst == CHUNK
    y_shape, y_blk, shared_in, aliases = _shared(share, T, cw, bf16, 7)

    def body(xbc_ref, z_ref, dt_ref, dtb_ref, alog_ref, dskip_ref, ng_ref, *rest):
        y_ref, sin_ref, st_ref = rest[len(shared_in):]

        @pl.when(pl.program_id(1) == 0)
        def _():
            st_ref[...] = jnp.zeros_like(st_ref)

        sin_ref[...] = st_ref[...]
        xs_l, z_l, b_l, c_l, ng_l, prev_l = _ssd_read(nh, ngrp, nst, xbc_ref, z_ref, ng_ref, st_ref)
        y_l, new_prev = _ssd_chunk(nh, ngrp, xs_l, z_l, b_l, c_l, dt_ref[...], dtb_ref[...], alog_ref[...],
                                   dskip_ref[...], ng_l, prev_l)
        for q in range(nh // 2):
            ls = pl.ds(q * PAIR, PAIR)
            y_ref[:, ls] = y_l[q].astype(y_ref.dtype)
            st_ref[:, ls] = new_prev[q]

    def blk(s, c):
        return s * nchunk + c

    prow = pl.BlockSpec((1, 128), lambda s, c: (0, 0))
    return pl.pallas_call(
        body, name="ssd_fwd", grid=(n_seq, nchunk),
        in_specs=[pl.BlockSpec((CHUNK, xbc.shape[1]), lambda s, c: (blk(s, c), 0)),
                  pl.BlockSpec((CHUNK, cw), lambda s, c: (blk(s, c), z_col_block)),
                  pl.BlockSpec((CHUNK, 128), lambda s, c: (blk(s, c), 0)),
                  prow, prow, prow, pl.BlockSpec((1, cw), lambda s, c: (0, 0))]
        + [pl.BlockSpec(memory_space=pl.ANY)] * len(shared_in),
        out_specs=[pl.BlockSpec((CHUNK, cw), lambda s, c: (blk(s, c), y_blk)),
                   pl.BlockSpec((nst, cw), lambda s, c: (blk(s, c), 0))],
        out_shape=[y_shape, _sds((T, cw), f32)], input_output_aliases=aliases,
        scratch_shapes=[pltpu.VMEM((nst, cw), f32)],
        compiler_params=_cparams(("arbitrary", "arbitrary")),
    )(xbc, proj, pdt, dtb, alog, dskip, ng.reshape(1, cw), *shared_in)


def _ssd_bwd(xbc, proj, z_col_block, pdt, dtb, alog, dskip, ng, sin, dy, dy_col_block, nh, ngrp, nst, n_seq, share=None):
    T, xw = xbc.shape
    cw = nh * HEAD
    nchunk = T // n_seq // CHUNK
    dz_shape, dz_blk, shared_in, aliases = _shared(share, T, cw, bf16, 9, out_index=1)

    def body(xbc_ref, z_ref, dt_ref, dtb_ref, alog_ref, dskip_ref, ng_ref, sin_ref, dy_ref, *rest):
        dxbc_ref, dz_ref, ddt_ref, ddtb_ref, dalog_ref, ddskip_ref, dng_ref, dst_ref = rest[len(shared_in):]
        s, cc = pl.program_id(0), pl.program_id(1)

        @pl.when((s == 0) & (cc == 0))
        def _():
            ddtb_ref[...] = jnp.zeros_like(ddtb_ref)
            dalog_ref[...] = jnp.zeros_like(dalog_ref)
            ddskip_ref[...] = jnp.zeros_like(ddskip_ref)
            dng_ref[...] = jnp.zeros_like(dng_ref)

        @pl.when(cc == 0)
        def _():
            dst_ref[...] = jnp.zeros_like(dst_ref)

        xs_l, z_l, b_l, c_l, ng_l, prev_l = _ssd_read(nh, ngrp, nst, xbc_ref, z_ref, ng_ref, sin_ref)
        _, vjp = jax.vjp(functools.partial(_ssd_chunk, nh, ngrp), xs_l, z_l, b_l, c_l, dt_ref[...], dtb_ref[...],
                         alog_ref[...], dskip_ref[...], ng_l, prev_l)
        dy_l = [dy_ref[:, pl.ds(q * PAIR, PAIR)].astype(f32) for q in range(nh // 2)]
        dst_l = [dst_ref[:, pl.ds(q * PAIR, PAIR)] for q in range(nh // 2)]
        dxs_l, dz_l, db_l, dc_l, ddt, ddtb, dalog, ddskip, dng_l, dprev_l = vjp((dy_l, dst_l))
        for q in range(nh // 2):
            ls = pl.ds(q * PAIR, PAIR)
            dxbc_ref[:, ls] = dxs_l[q]
            dz_ref[:, ls] = dz_l[q].astype(dz_ref.dtype)
            dng_ref[0:1, ls] += dng_l[q]
            dst_ref[:, ls] = dprev_l[q]
        for g in range(ngrp):
            dxbc_ref[:, pl.ds(cw + g * nst, nst)] = db_l[g]
            dxbc_ref[:, pl.ds(cw + ngrp * nst + g * nst, nst)] = dc_l[g]
        ddt_ref[...] = ddt.astype(ddt_ref.dtype)
        ddtb_ref[0:1, :] += ddtb
        dalog_ref[0:1, :] += dalog
        ddskip_ref[0:1, :] += ddskip

    def blk(s, cc):
        return s * nchunk + (nchunk - 1 - cc)

    prow = pl.BlockSpec((1, 128), lambda s, c: (0, 0))
    pacc = pl.BlockSpec((8, 128), lambda s, c: (0, 0))
    return pl.pallas_call(
        body, name="ssd_bwd", grid=(n_seq, nchunk),
        in_specs=[pl.BlockSpec((CHUNK, xw), lambda s, c: (blk(s, c), 0)),
                  pl.BlockSpec((CHUNK, cw), lambda s, c: (blk(s, c), z_col_block)),
                  pl.BlockSpec((CHUNK, 128), lambda s, c: (blk(s, c), 0)),
                  prow, prow, prow, pl.BlockSpec((1, cw), lambda s, c: (0, 0)),
                  pl.BlockSpec((nst, cw), lambda s, c: (blk(s, c), 0)),
                  pl.BlockSpec((CHUNK, cw), lambda s, c: (blk(s, c), dy_col_block))]
        + [pl.BlockSpec(memory_space=pl.ANY)] * len(shared_in),
        out_specs=[pl.BlockSpec((CHUNK, xw), lambda s, c: (blk(s, c), 0)),
                   pl.BlockSpec((CHUNK, cw), lambda s, c: (blk(s, c), dz_blk)),
                   pl.BlockSpec((CHUNK, 128), lambda s, c: (blk(s, c), 0)),
                   pacc, pacc, pacc, pl.BlockSpec((8, cw), lambda s, c: (0, 0))],
        out_shape=[_sds((T, xw), f32), dz_shape, _sds((T, 128), bf16),
                   _sds((8, 128), f32), _sds((8, 128), f32), _sds((8, 128), f32), _sds((8, cw), f32)],
        input_output_aliases=aliases,
        scratch_shapes=[pltpu.VMEM((nst, cw), f32)],
        compiler_params=_cparams(("arbitrary", "arbitrary")),
    )(xbc, proj, pdt, dtb, alog, dskip, ng.reshape(1, cw), sin, dy, *shared_in)


_HBM = pl.BlockSpec(memory_space=pltpu.HBM)
_SEM = pl.BlockSpec(memory_space=pltpu.SEMAPHORE)
_EFFECT = pltpu.SideEffectType.DATAFLOW_SIDE_EFFECTING


def _split_copies(n, scatter, src_refs, land_refs, send_sems, recv_sems):
    npeer = N_DEV - 1
    x, y, c = lax.axis_index("x"), lax.axis_index("y"), lax.axis_index("c")
    me = 4 * x + 2 * y + c
    copies = []
    for i in range(n):
        for k in range(1, N_DEV):
            px = 1 - x if k & 4 else x
            py = 1 - y if k & 2 else y
            pc = 1 - c if k & 1 else c
            src = src_refs[i].at[4 * px + 2 * py + pc] if scatter else src_refs[i]
            copies.append(pltpu.make_async_remote_copy(
                src_ref=src, dst_ref=land_refs[i].at[me],
                send_sem=send_sems.at[i * npeer + k - 1], recv_sem=recv_sems.at[i * npeer + k - 1],
                device_id=(px, py, pc), device_id_type=pl.DeviceIdType.MESH))
    return copies


def _exchange_start(name, arrs, scatter):
    n = len(arrs)
    nsem = n * (N_DEV - 1)
    me = 4 * lax.axis_index("x") + 2 * lax.axis_index("y") + lax.axis_index("c")
    lands = []
    for a in arrs:
        own = lax.dynamic_index_in_dim(a, me, 0, keepdims=True) if scatter else a[None]
        full = lax.empty(a.shape if scatter else (N_DEV,) + a.shape, a.dtype)
        lands.append(lax.dynamic_update_slice(full, own, (me,) + (0,) * (full.ndim - 1)))

    def body(*refs):
        src_refs, land_refs = refs[:n], refs[n:2 * n]
        send_sems, recv_sems = refs[2 * n], refs[2 * n + 1]
        token = refs[-1]
        for cp in _split_copies(n, scatter, src_refs, land_refs, send_sems, recv_sems):
            cp.start()
        token[...] = jnp.zeros_like(token)

    res = pl.pallas_call(
        body, name=name,
        out_shape=(pltpu.SemaphoreType.DMA((nsem,)), pltpu.SemaphoreType.DMA((nsem,)),
                   *[pltpu.HBM(a.shape, a.dtype) for a in arrs], *[pltpu.HBM(l.shape, l.dtype) for l in lands],
                   _sds((8, 128), f32)),
        in_specs=[_HBM] * (2 * n),
        out_specs=(_SEM, _SEM, *[_HBM] * (2 * n), pl.BlockSpec(memory_space=pltpu.VMEM)),
        input_output_aliases={j: 2 + j for j in range(2 * n)},
        compiler_params=pltpu.CompilerParams(has_side_effects=_EFFECT),
    )(*[pltpu.with_memory_space_constraint(a, pltpu.HBM) for a in arrs],
      *[pltpu.with_memory_space_constraint(l, pltpu.HBM) for l in lands])
    return (n, scatter, res[0], res[1], res[2:2 + n], res[2 + n:2 + 2 * n]), res[-1]


def _exchange_wait(name, handle, after):
    n, scatter, send_sems, recv_sems, srcs, lands = handle
    after = list(after) if isinstance(after, (list, tuple)) else [after]

    def body(*refs):
        src_refs, land_refs = refs[:n], refs[n:2 * n]
        for cp in _split_copies(n, scatter, src_refs, land_refs, refs[2 * n], refs[2 * n + 1]):
            cp.wait_send()
            cp.wait_recv()

    res = pl.pallas_call(
        body, name=name,
        out_shape=[pltpu.HBM(a.shape, a.dtype) for a in (*srcs, *lands)],
        in_specs=[_HBM] * (2 * n) + [_SEM, _SEM] + [pl.BlockSpec(memory_space=pl.ANY)] * len(after),
        out_specs=[_HBM] * (2 * n),
        input_output_aliases={j: j for j in range(2 * n)},
        compiler_params=pltpu.CompilerParams(has_side_effects=_EFFECT),
    )(*srcs, *lands, send_sems, recv_sems, *after)
    return res[n:]


def _adam_tiles(R, C):
    if R % 256 == 0:
        return (256, C), (R // 256, 1)
    assert C % 128 == 0
    return (R, 128), (1, C // 128)


def _adam(name, parts, w, m, v, layer=None, depth=None, into=None, stacked_in=False):
    P, R, C = parts.shape
    (tr, tc), (gr, gc) = _adam_tiles(R, C)
    c1 = 1.0 / (1.0 - ADAM_B1 ** ADAM_STEP)
    c2 = 1.0 / (1.0 - ADAM_B2 ** ADAM_STEP)
    into = [] if into is None else list(into)

    def body(p_ref, w_ref, m_ref, v_ref, *rest):
        g_ref, d_ref, nm_ref, nv_ref = rest[len(into):]
        g = p_ref[0].astype(f32)
        for s in range(1, P):
            g = g + p_ref[s].astype(f32)
        nm = ADAM_B1 * m_ref[...] + (1.0 - ADAM_B1) * g
        nv = ADAM_B2 * v_ref[...] + (1.0 - ADAM_B2) * (g * g)
        g_ref[...] = g
        nm_ref[...] = nm
        nv_ref[...] = nv
        d_ref[...] = -ADAM_LR * ((nm * c1) / (jnp.sqrt(nv * c2) + ADAM_EPS) + ADAM_WD * w_ref[...])

    tile = pl.BlockSpec((tr, tc), lambda i, j: (i, j))
    layer_tile = pl.BlockSpec((None, tr, tc), lambda i, j: (layer, i, j))
    out_tile, out_sds = (tile, _sds((R, C), f32)) if layer is None else (layer_tile, _sds((depth, R, C), f32))
    return pl.pallas_call(
        body, name=name, grid=(gr, gc),
        in_specs=[pl.BlockSpec((P, tr, tc), lambda i, j: (0, i, j))] + [layer_tile if stacked_in else tile] * 3
        + [pl.BlockSpec(memory_space=pl.ANY)] * len(into),
        out_specs=[out_tile] * 4, out_shape=[out_sds] * 4,
        input_output_aliases={4 + k: k for k in range(len(into))},
        compiler_params=_cparams(("arbitrary", "arbitrary")),
    )(parts, w, m, v, *into)


def _sum_parts(name, parts):
    P, R, C = parts.shape
    tr = 256 if R % 256 == 0 else R

    def body(p_ref, o_ref):
        g = p_ref[0]
        for s in range(1, P):
            g = g + p_ref[s]
        o_ref[...] = g

    return pl.pallas_call(
        body, name=name, grid=(R // tr,),
        in_specs=[pl.BlockSpec((P, tr, C), lambda i: (0, i, 0))], out_specs=pl.BlockSpec((tr, C), lambda i: (i, 0)),
        out_shape=_sds((R, C), f32), compiler_params=_cparams(("arbitrary",)),
    )(parts)


def _pad_to(a, n, axis):
    if a.shape[axis] == n:
        return a
    cfg = [(0, 0)] * a.ndim
    cfg[axis] = (0, n - a.shape[axis])
    return jnp.pad(a, cfg)


def _pack(arrs):
    flat = [_pad_to(a.reshape(-1), -(-a.size // 128) * 128, 0) for a in arrs]
    rows = jnp.concatenate(flat).reshape(-1, 128)
    return _pad_to(rows, -(-rows.shape[0] // 256) * 256, 0)


def _unpack(slab, shapes):
    flat = slab.reshape(-1)
    out, o = [], 0
    for s in shapes:
        n = math.prod(s)
        out.append(flat[o:o + n].reshape(s))
        o += -(-n // 128) * 128
    return out


_NAMES = ['norm1_g', 'w_in', 'conv_a_w', 'conv_a_b', 'ln_a_g', 'ln_a_b', 'ln_b_g', 'ln_b_b', 'w_spatial', 'b_spatial',
          'conv_c_w', 'conv_c_b', 'dt_bias', 'a_log', 'd_skip', 'norm_c_g', 'w_out', 'norm2_g', 'w_ff1', 'w_ff2', 'final_g']
_REPL = ['norm1_g', 'conv_a_b', 'ln_a_g', 'ln_a_b', 'ln_b_g', 'ln_b_b', 'w_spatial', 'b_spatial', 'conv_c_b',
         'dt_bias', 'a_log', 'd_skip', 'norm_c_g', 'norm2_g']
_CONVW = ['conv_a_w', 'conv_c_w']
_BIG = ['w_in', 'w_out', 'w_ff1', 'w_ff2']
_BIG_T = {'w_in': True, 'w_out': False, 'w_ff1': True, 'w_ff2': False}


def _row128(v):
    return _pad_to(v.reshape(1, -1), 128, 1)


def _step(p, m, v, x, loss_target):
    nb, S, D = x.shape
    T = nb * S
    depth = p['norm1_g'].shape[0]
    a_w = p['conv_a_b'].shape[1]
    b_w = p['ln_b_g'].shape[1]
    nh = p['dt_bias'].shape[1]
    c_w = p['norm_c_g'].shape[1]
    xw = p['conv_c_b'].shape[1]
    ngrp = 2
    nst = (xw - c_w) // (2 * ngrp)
    d_in = p['w_in'].shape[2] * N_DEV
    main = d_in - nh
    assert main == 2 * a_w + 2 * b_w + c_w + xw and 2 * a_w == 2 * b_w == c_w and xw % c_w == c_w // 2
    me = 4 * lax.axis_index("x") + 2 * lax.axis_index("y") + lax.axis_index("c")

    x2 = x.reshape(T, D)
    tgt = loss_target.reshape(T, D)

    def shards(i, z=None):
        z = 0.0 if z is None else z
        return [(p['w_in'][i].T + z).astype(bf16), (p['w_out'][i] + z).astype(bf16), (p['w_ff1'][i].T + z).astype(bf16),
                (p['w_ff2'][i] + z).astype(bf16), p['conv_a_w'][i], p['conv_c_w'][i]]

    def gathered_in(wt, ca, cc):
        wt = wt.reshape(d_in, D)
        ca = jnp.transpose(ca, (1, 0, 2)).reshape(KA, a_w)
        cc = jnp.transpose(cc, (1, 0, 2)).reshape(KC, xw)
        return dict(wt=wt, wt_dt=_pad_to(wt[main:], 128, 0), ca=_pad_to(ca, 32, 0), cc=_pad_to(cc, 8, 0))

    def start_layer(i, after=None):
        sh = shards(i, None if after is None else after[0, 0])
        ha, t = _exchange_start("gather_w%da_start" % i, [sh[0], sh[4], sh[5]], False)
        hb, t = _exchange_start("gather_w%db_start" % i, [shards(i, t[0, 0])[1]], False)
        hc, t = _exchange_start("gather_w%dc_start" % i, shards(i, t[0, 0])[2:4], False)
        return dict(a=ha, b=hb, c=hc), t

    W, saved = [], []
    xc = x2
    H, tok = start_layer(0)
    h1, rtok = _rms_fwd(xc, p['norm1_g'][0])
    for i in range(depth):
        w = gathered_in(*_exchange_wait("gather_w%da_wait" % i, H['a'], [xc, tok]))
        W.append(w)
        Hi = H
        if i + 1 < depth:
            H, tok = start_layer(i + 1, rtok + tok)
        else:
            tok = None
        (proj,) = _mm("mm_proj", h1, w['wt'], "nt", [bf16], dep=tok, n=main)
        (pdt,) = _mm("mm_pdt", h1, w['wt_dt'], "nt", [f32])
        mixw = a_w + b_w + c_w
        ycat = _conv_fwd("confa_fwd", proj, 0, w['ca'], p['conv_a_b'][i], KA, True, nb, p['ln_a_g'][i], p['ln_a_b'][i],
                         share=(mixw, 0, None))
        ycat = _gmlp_fwd(proj, 1, p['ln_b_g'][i], p['ln_b_b'][i], p['w_spatial'][i], p['b_spatial'][i], share=(mixw, 1, ycat))
        xbc = _conv_fwd("convc_fwd", proj, 2, w['cc'], p['conv_c_b'][i], KC, False, nb)
        dtb, alog, dsk = _row128(p['dt_bias'][i]), _row128(p['a_log'][i]), _row128(p['d_skip'][i])
        ycat, sin = _ssd_fwd(xbc, proj, 2, pdt, dtb, alog, dsk, p['norm_c_g'][i], nh, ngrp, nst, nb, share=(mixw, 1, ycat))
        w['wout'] = _exchange_wait("gather_w%db_wait" % i, Hi['b'], ycat)[0].reshape(-1, D)
        xm, h2 = _mm("mm_out", ycat, w['wout'], "nn", [f32, bf16], _ep_add_rms, (xc,), rows=(p['norm2_g'][i].reshape(1, D),))
        got = _exchange_wait("gather_w%dc_wait" % i, Hi['c'], h2)
        w['w1t'], w['w2'] = got[0].reshape(-1, D), got[1].reshape(-1, D)
        f, a = _mm("mm_ff1", h2, w['w1t'], "nt", [bf16, bf16], _ep_relu2)
        saved.append(dict(x_in=xc, h1=h1, proj=proj, pdt=pdt, xbc=xbc, sin=sin, ycat=ycat, xm=xm, h2=h2, f=f, a=a,
                          dtb=dtb, alog=alog, dsk=dsk))
        if i + 1 < depth:
            xc, h1, rtok = _mm("mm_ff2", a, w['w2'], "nn", [f32, bf16], _ep_add_rms, (xm,), tok_out=True,
                               rows=(p['norm1_g'][i + 1].reshape(1, D),))
        else:
            (xc,) = _mm("mm_ff2", a, w['w2'], "nn", [f32], _ep_add, (xm,))

    lp, dx, dfinal = _loss_head(xc, p['final_g'], tgt)
    loss = lax.psum(lp[0, 0], ("x", "y", "c"))

    out = {}
    kinds = ("grad", "delta", "new_m", "new_v")
    names1 = _REPL + _CONVW

    started, small = [], [None] * depth

    def send(n, i, g):
        handle, token = _exchange_start("scatter_%s_%d_start" % (n, i), [g.reshape(N_DEV, -1, D)], True)
        started.append((n, i, handle))
        return token

    tok = None
    for i in reversed(range(depth)):
        w, sv = W[i], saved[i]
        (df,) = _mm("mm_df", dx, w['w2'], "nt", [bf16], _ep_drelu2, (sv['f'],), dep=tok)
        (gw2,) = _mm("mm_gw2", sv['a'], dx, "tn", [bf16])
        tok = send('w_ff2', i, gw2)
        dxm, dg2 = _mm("mm_dh2", df, w['w1t'], "nn", [f32], _ep_rms_bwd, (sv['xm'], dx), dep=tok,
                       rows=(p['norm2_g'][i].reshape(1, D),), n_row_out=1)
        (gw1t,) = _mm("mm_gw1", df, sv['h2'], "tn", [bf16])
        tok = send('w_ff1', i, gw1t)
        (dycat,) = _mm("mm_dycat", dxm, w['wout'], "nt", [bf16], dep=tok)
        (gwout,) = _mm("mm_gwout", sv['ycat'], dxm, "tn", [bf16])
        tok = send('w_out', i, gwout)
        dproj, dwa, dba, dlag, dlab = _conv_bwd("confa_bwd", sv['proj'], 0, w['ca'], p['conv_a_b'][i] + tok[0, 0], dycat, 0, KA,
                                                True, nb, p['ln_a_g'][i], p['ln_a_b'][i], share=(main, 0, None))
        dproj, dlbg, dlbb, dws, dbs = _gmlp_bwd(sv['proj'], 1, p['ln_b_g'][i], p['ln_b_b'][i], p['w_spatial'][i],
                                                p['b_spatial'][i], dycat, 1, share=(main, 1, dproj))
        dxbc, dproj, ddt, ddtb, dalog, ddsk, dng = _ssd_bwd(sv['xbc'], sv['proj'], 2, sv['pdt'], sv['dtb'], sv['alog'],
                                                            sv['dsk'], p['norm_c_g'][i], sv['sin'], dycat, 1, nh, ngrp, nst, nb,
                                                            share=(main, 2, dproj))
        dproj, dwc, dbc = _conv_bwd("convc_bwd", sv['proj'], 2, w['cc'], p['conv_c_b'][i], dxbc, 0, KC, False, nb,
                                    share=(main, 2, dproj))
        (dh_main,) = _mm("mm_dh1", dproj, w['wt'], "nn", [f32])
        (gwt_main,) = _mm("mm_gwin", dproj, sv['h1'], "tn", [bf16])
        (gwt_dt,) = _mm("mm_gwdt", ddt, sv['h1'], "tn", [bf16])
        tok = send('w_in', i, jnp.concatenate([gwt_main, gwt_dt[:nh]], axis=0))
        dx, dg1 = _mm("mm_dh1dt", ddt, w['wt_dt'], "nn", [f32], _ep_add_rms_bwd, (dh_main, sv['x_in'], dxm), dep=tok,
                      rows=(p['norm1_g'][i].reshape(1, D),), n_row_out=1)

        gi = dict(norm1_g=dg1[0], norm2_g=dg2[0], conv_a_w=dwa[:KA], conv_a_b=dba[0], ln_a_g=dlag[0], ln_a_b=dlab[0],
                  ln_b_g=dlbg[0], ln_b_b=dlbb[0], w_spatial=dws, b_spatial=dbs, conv_c_w=dwc[:KC], conv_c_b=dbc[0],
                  dt_bias=ddtb[0, :nh], a_log=dalog[0, :nh], d_skip=ddsk[0, :nh], norm_c_g=dng[0])
        parts_i = [gi[n] for n in names1] + ([dfinal[0]] if i == depth - 1 else [])
        handle, tok = _exchange_start("gather_g%d_start" % i, [_pack(parts_i)], False)
        small[i] = ([a.shape for a in parts_i], handle)
    grad_x = dx.reshape(nb, S, D)

    dep = [dx, tok]
    for n, i, handle in started:
        (parts,) = _exchange_wait("scatter_%s_%d_wait" % (n, i), handle, dep)
        prev = [out[(kind, n)] for kind in kinds] if (kinds[0], n) in out else None
        if _BIG_T[n]:
            res = _adam("adam_%s_%d" % (n, i), parts, p[n][i].T, m[n][i].T, v[n][i].T, layer=i, depth=depth, into=prev)
        else:
            res = _adam("adam_%s_%d" % (n, i), parts, p[n], m[n], v[n], layer=i, depth=depth, into=prev, stacked_in=True)
        for kind, r in zip(kinds, res):
            out[(kind, n)] = r
        dep = res[3]

    gsum = [None] * depth
    for i in reversed(range(depth)):
        (parts,) = _exchange_wait("gather_g%d_wait" % i, small[i][1], dep)
        gsum[i] = _sum_parts("sum_small", parts)
        dep = gsum[i]
    widths = [-(-math.prod(p[n].shape[1:]) // 128) * 128 for n in _REPL]
    rep_rows = sum(widths) // 128
    tot_rows = -(-depth * rep_rows // 256) * 256

    def rep_slab(q):
        cols = [_pad_to(q[n].reshape(depth, -1), wd, 1) for n, wd in zip(_REPL, widths)]
        return _pad_to(jnp.concatenate(cols, axis=1).reshape(-1, 128), tot_rows, 0)

    g_rep = _pad_to(jnp.concatenate([g[:rep_rows] for g in gsum], axis=0), tot_rows, 0)
    res = _adam("adam_small", g_rep[None], rep_slab(p), rep_slab(m), rep_slab(v))
    for kind, r in zip(kinds, res):
        view = r[:depth * rep_rows].reshape(depth, -1)
        o = 0
        for n, wd in zip(_REPL, widths):
            out[(kind, n)] = view[:, o:o + math.prod(p[n].shape[1:])].reshape(p[n].shape)
            o += wd

    extra = []
    for i in range(depth):
        tail = _unpack(gsum[i][rep_rows:], small[i][0][len(_REPL):])
        extra.append(tail)
    gconv = []
    for j, n in enumerate(_CONVW):
        cw_shard = p[n].shape[2]
        full = jnp.stack([extra[i][j] for i in range(depth)])
        gconv.append(lax.dynamic_slice_in_dim(full, me * cw_shard, cw_shard, axis=2))
    tail_names = _CONVW + ['final_g']
    res = _adam("adam_conv", _pack(gconv + [extra[depth - 1][len(_CONVW)]])[None],
                *[_pack([q[n] for n in tail_names]) for q in (p, m, v)])
    for kind, r in zip(kinds, res):
        for n, arr in zip(tail_names, _unpack(r, [p[n].shape for n in tail_names])):
            out[(kind, n)] = arr
    for n in _BIG:
        if _BIG_T[n]:
            for kind in kinds:
                out[(kind, n)] = jnp.swapaxes(out[(kind, n)], 1, 2)

    flat = [loss, grad_x]
    for kind in ("grad", "delta", "new_m", "new_v"):
        flat += [out[(kind, n)] for n in _NAMES]
    return tuple(flat)


def kernel(x, norm1_g, w_in, conv_a_w, conv_a_b, ln_a_g, ln_a_b, ln_b_g, ln_b_b, w_spatial, b_spatial, conv_c_w, conv_c_b, dt_bias, a_log, d_skip, norm_c_g, w_out, norm2_g, w_ff1, w_ff2, final_g, loss_target, m_norm1_g, m_w_in, m_conv_a_w, m_conv_a_b, m_ln_a_g, m_ln_a_b, m_ln_b_g, m_ln_b_b, m_w_spatial, m_b_spatial, m_conv_c_w, m_conv_c_b, m_dt_bias, m_a_log, m_d_skip, m_norm_c_g, m_w_out, m_norm2_g, m_w_ff1, m_w_ff2, m_final_g, v_norm1_g, v_w_in, v_conv_a_w, v_conv_a_b, v_ln_a_g, v_ln_a_b, v_ln_b_g, v_ln_b_b, v_w_spatial, v_b_spatial, v_conv_c_w, v_conv_c_b, v_dt_bias, v_a_log, v_d_skip, v_norm_c_g, v_w_out, v_norm2_g, v_w_ff1, v_w_ff2, v_final_g):
    p = dict(zip(_NAMES, (norm1_g, w_in, conv_a_w, conv_a_b, ln_a_g, ln_a_b, ln_b_g, ln_b_b, w_spatial, b_spatial, conv_c_w,
                          conv_c_b, dt_bias, a_log, d_skip, norm_c_g, w_out, norm2_g, w_ff1, w_ff2, final_g)))
    m = dict(zip(_NAMES, (m_norm1_g, m_w_in, m_conv_a_w, m_conv_a_b, m_ln_a_g, m_ln_a_b, m_ln_b_g, m_ln_b_b, m_w_spatial,
                          m_b_spatial, m_conv_c_w, m_conv_c_b, m_dt_bias, m_a_log, m_d_skip, m_norm_c_g, m_w_out, m_norm2_g,
                          m_w_ff1, m_w_ff2, m_final_g)))
    v = dict(zip(_NAMES, (v_norm1_g, v_w_in, v_conv_a_w, v_conv_a_b, v_ln_a_g, v_ln_a_b, v_ln_b_g, v_ln_b_b, v_w_spatial,
                          v_b_spatial, v_conv_c_w, v_conv_c_b, v_dt_bias, v_a_log, v_d_skip, v_norm_c_g, v_w_out, v_norm2_g,
                          v_w_ff1, v_w_ff2, v_final_g)))
    return _step(p, m, v, x, loss_target)
```
